```python
import math
import jax, jax.numpy as jnp
from jax import lax
import numpy as np

D_MODEL = 1024
BATCH = 8
SEQ = 2048
DEPTH = 1

N_META = 16
N_DIFF_HEADS = 8
HEAD_DIM = 64
V_DIM = 2 * HEAD_DIM
QK_WIDTH = N_DIFF_HEADS * 2 * HEAD_DIM
ATTN_WIDTH = N_DIFF_HEADS * V_DIM
POOL_WINDOWS = (2, 4, 8, 16)
N_POOL_GROUPS = len(POOL_WINDOWS)
POOL_GROUP_DIM = 128
POOL_WIDTH = N_POOL_GROUPS * POOL_GROUP_DIM
N_BRANCHES = 2
GATE_WIDTH = N_BRANCHES * D_MODEL
IN_WIDTH = 2 * QK_WIDTH + ATTN_WIDTH + POOL_WIDTH + GATE_WIDTH
ROPE_THETA = 10000.0
Q_BLOCK = 128
N_GROUPS = 4
EXPERTS_PER_GROUP = 4
N_EXPERTS = N_GROUPS * EXPERTS_PER_GROUP
TOP_K = 2
D_EXPERT = 512
EPS = 1e-6

kernel_name = "hybrid_diffattn_pool_hmoe_block"


def rmsnorm(x, g):
    xf = x.astype(jnp.float32)
    y = xf * lax.rsqrt(jnp.mean(xf * xf, axis=-1, keepdims=True) + EPS)
    return (y * g.astype(jnp.float32)).astype(x.dtype)


def rope_tables(T, dtype):
    inv = 1.0 / (ROPE_THETA ** (jnp.arange(0, HEAD_DIM, 2, dtype=jnp.float32) / HEAD_DIM))
    ang = jnp.arange(T, dtype=jnp.float32)[:, None] * inv[None, :]
    ang = jnp.concatenate([ang, ang], axis=-1)
    return jnp.cos(ang).astype(dtype), jnp.sin(ang).astype(dtype)


def apply_rope(x, cos, sin):
    half = HEAD_DIM // 2
    rot = jnp.concatenate([-x[..., half:], x[..., :half]], axis=-1)
    c = cos[None, :, None, None, :]
    s = sin[None, :, None, None, :]
    return x * c + rot * s


def diff_attention(q, k, v, lam):
    B, T, H = q.shape[0], q.shape[1], q.shape[2]
    n_blk = -(-T // Q_BLOCK)
    Tp = n_blk * Q_BLOCK
    pad5 = ((0, 0), (0, Tp - T), (0, 0), (0, 0), (0, 0))
    q = jnp.pad(q, pad5)
    k = jnp.pad(k, pad5)
    v = jnp.pad(v, pad5[:4])
    scale = 1.0 / math.sqrt(HEAD_DIM)
    key_pos = jnp.arange(Tp)

    def block(i):
        start = i * Q_BLOCK
        qb = lax.dynamic_slice_in_dim(q, start, Q_BLOCK, axis=1)
        s = jnp.einsum('bqhcd,bkhcd->bhcqk', qb, k).astype(jnp.float32) * scale
        q_pos = start + jnp.arange(Q_BLOCK)
        mask = key_pos[None, :] <= q_pos[:, None]
        s = jnp.where(mask[None, None, None], s, -jnp.inf)
        p = jax.nn.softmax(s, axis=-1)
        a = p[:, :, 0] - lam * p[:, :, 1]
        return jnp.einsum('bhqk,bkhe->bqhe', a.astype(v.dtype), v)

    out = lax.map(block, jnp.arange(n_blk))
    out = jnp.moveaxis(out, 0, 1).reshape(B, Tp, H, V_DIM)
    return out[:, :T]


def multiscale_pool(u, pool_w, pool_scale):
    B, T, G, C = u.shape
    uf = u.astype(jnp.float32)
    cs = jnp.pad(lax.cumsum(uf, axis=1), ((0, 0), (1, 0), (0, 0), (0, 0)))
    t = jnp.arange(T)
    means = []
    for g, w in enumerate(POOL_WINDOWS):
        c = cs[:, :, g]
        hi = c[:, 1:]
        lo = jnp.pad(c[:, :T + 1 - w], ((0, 0), (w - 1, 0), (0, 0)))
        cnt = jnp.minimum(t + 1, w).astype(jnp.float32)
        means.append((hi - lo) / cnt[None, :, None])
    mean = jnp.stack(means, axis=2)
    y = (mean - uf).astype(u.dtype)
    y = jnp.einsum('btgc,gcd->btgd', y, pool_w).reshape(B, T, G * C)
    return y * pool_scale


def hier_moe(h, w_rg, b_rg, w_re, b_re, w_gate, w_up, w_down):
    B, T, D = h.shape
    hf = h.reshape(-1, D)
    glog = (hf @ w_rg + b_rg).astype(jnp.float32)
    gprob = jax.nn.softmax(glog, axis=-1)
    gsel = jnp.argmax(glog, axis=-1)
    p_group = jnp.take_along_axis(gprob, gsel[:, None], axis=1)
    elog = (hf @ w_re + b_re).astype(jnp.float32).reshape(-1, N_GROUPS, EXPERTS_PER_GROUP)
    elog_g = jnp.take_along_axis(elog, gsel[:, None, None], axis=1)[:, 0]
    top_v, top_i = lax.top_k(elog_g, TOP_K)
    w2 = jax.nn.softmax(top_v, axis=-1) * p_group
    eidx = gsel[:, None] * EXPERTS_PER_GROUP + top_i
    combine = jnp.sum(jax.nn.one_hot(eidx, N_EXPERTS, dtype=jnp.float32) * w2[..., None], axis=1)
    combine = combine.astype(h.dtype)
    y = jnp.zeros_like(hf)
    for e in range(N_EXPERTS):
        a = jax.nn.silu(hf @ w_gate[e]) * (hf @ w_up[e])
        y = y + combine[:, e:e + 1] * (a @ w_down[e])
    return y.reshape(B, T, D)


def setup_inputs(seed: int = 0) -> dict:
    key = jax.random.key(seed)
    ks = jax.random.split(key, 24)
    f32 = jnp.float32
    nrm = lambda k, shape, s: jax.random.normal(k, shape, f32) * s
    L = DEPTH
    return {
        "x": nrm(ks[0], (BATCH, SEQ, D_MODEL), 1.0),
        "meta": nrm(ks[1], (N_META, D_MODEL), 1.0),
        "norm1_g": 1.0 + nrm(ks[2], (L, D_MODEL), 0.02),
        "w_in": nrm(ks[3], (L, D_MODEL, IN_WIDTH), D_MODEL ** -0.5),
        "b_gate": nrm(ks[4], (L, GATE_WIDTH), 0.02),
        "lambda_q1": nrm(ks[5], (L, HEAD_DIM), 0.1),
        "lambda_k1": nrm(ks[6], (L, HEAD_DIM), 0.1),
        "lambda_q2": nrm(ks[7], (L, HEAD_DIM), 0.1),
        "lambda_k2": nrm(ks[8], (L, HEAD_DIM), 0.1),
        "subln_g": 1.0 + nrm(ks[9], (L, V_DIM), 0.02),
        "pool_w": nrm(ks[10], (L, N_POOL_GROUPS, POOL_GROUP_DIM, POOL_GROUP_DIM), POOL_GROUP_DIM ** -0.5),
        "pool_scale": 1.0 + nrm(ks[11], (L, POOL_WIDTH), 0.1),
        "w_attn_br": nrm(ks[12], (L, ATTN_WIDTH, D_MODEL), ATTN_WIDTH ** -0.5),
        "w_pool_br": nrm(ks[13], (L, POOL_WIDTH, D_MODEL), POOL_WIDTH ** -0.5),
        "w_out": nrm(ks[14], (L, D_MODEL, D_MODEL), D_MODEL ** -0.5),
        "norm2_g": 1.0 + nrm(ks[15], (L, D_MODEL), 0.02),
        "w_router_group": nrm(ks[16], (L, D_MODEL, N_GROUPS), D_MODEL ** -0.5),
        "b_router_group": nrm(ks[17], (L, N_GROUPS), 0.01),
        "w_router_expert": nrm(ks[18], (L, D_MODEL, N_EXPERTS), D_MODEL ** -0.5),
        "b_router_expert": nrm(ks[19], (L, N_EXPERTS), 0.01),
        "w_e_gate": nrm(ks[20], (L, N_EXPERTS, D_MODEL, D_EXPERT), D_MODEL ** -0.5),
        "w_e_up": nrm(ks[21], (L, N_EXPERTS, D_MODEL, D_EXPERT), D_MODEL ** -0.5),
        "w_e_down": nrm(ks[22], (L, N_EXPERTS, D_EXPERT, D_MODEL), D_EXPERT ** -0.5),
        "final_g": 1.0 + nrm(ks[23], (D_MODEL,), 0.02),
    }


def reference(x, meta, norm1_g, w_in, b_gate, lambda_q1, lambda_k1, lambda_q2, lambda_k2,
              subln_g, pool_w, pool_scale, w_attn_br, w_pool_br, w_out, norm2_g,
              w_router_group, b_router_group, w_router_expert, b_router_expert,
              w_e_gate, w_e_up, w_e_down, final_g):
    B = x.shape[0]
    h = jnp.concatenate([jnp.broadcast_to(meta[None].astype(x.dtype), (B, N_META, D_MODEL)), x], axis=1)
    T = h.shape[1]
    cos, sin = rope_tables(T, h.dtype)
    o_q, o_k, o_v = 0, QK_WIDTH, 2 * QK_WIDTH
    o_p = o_v + ATTN_WIDTH
    o_g = o_p + POOL_WIDTH

    for l in range(DEPTH):
        lambda_init = 0.8 - 0.6 * math.exp(-0.3 * l)
        hn = rmsnorm(h, norm1_g[l])
        z = hn @ w_in[l]
        q = z[..., o_q:o_k].reshape(B, T, N_DIFF_HEADS, 2, HEAD_DIM)
        k = z[..., o_k:o_v].reshape(B, T, N_DIFF_HEADS, 2, HEAD_DIM)
        v = z[..., o_v:o_p].reshape(B, T, N_DIFF_HEADS, V_DIM)
        u = z[..., o_p:o_g].reshape(B, T, N_POOL_GROUPS, POOL_GROUP_DIM)
        gates = jax.nn.sigmoid(z[..., o_g:] + b_gate[l]).reshape(B, T, N_BRANCHES, D_MODEL)

        q = apply_rope(q, cos, sin)
        k = apply_rope(k, cos, sin)
        lam = (jnp.exp(jnp.sum(lambda_q1[l].astype(jnp.float32) * lambda_k1[l].astype(jnp.float32)))
               - jnp.exp(jnp.sum(lambda_q2[l].astype(jnp.float32) * lambda_k2[l].astype(jnp.float32)))
               + lambda_init)
        a = diff_attention(q, k, v, lam)
        a = rmsnorm(a, subln_g[l]) * (1.0 - lambda_init)
        y_attn = a.reshape(B, T, ATTN_WIDTH) @ w_attn_br[l]

        y_pool = multiscale_pool(u, pool_w[l], pool_scale[l]) @ w_pool_br[l]

        mixed = gates[:, :, 0] * y_attn + gates[:, :, 1] * y_pool
        h = h + mixed @ w_out[l]

        hn2 = rmsnorm(h, norm2_g[l])
        h = h + hier_moe(hn2, w_router_group[l], b_router_group[l], w_router_expert[l],
                         b_router_expert[l], w_e_gate[l], w_e_up[l], w_e_down[l])

    out = rmsnorm(h, final_g)
    return out[:, N_META:]
```

```python
import functools
import math

import jax
import jax.numpy as jnp
from jax import lax
from jax.experimental import pallas as pl
from jax.experimental.pallas import tpu as pltpu

D_MODEL = 1024
N_META = 16
N_HEADS = 8
HEAD_DIM = 64
V_DIM = 2 * HEAD_DIM
POOL_WINDOWS = (2, 4, 8, 16)
N_POOL_GROUPS = len(POOL_WINDOWS)
POOL_GROUP_DIM = 128
POOL_WIDTH = N_POOL_GROUPS * POOL_GROUP_DIM
ROPE_THETA = 10000.0
N_GROUPS = 4
EXPERTS_PER_GROUP = 4
N_EXPERTS = N_GROUPS * EXPERTS_PER_GROUP
D_EXPERT = 512
EPS = 1e-6
LAMBDA_INIT = 0.8 - 0.6 * math.exp(-0.3 * 0)

LANES = 128
NEG_BIG = -1e30
VMEM_LIMIT = 48 * 1024 * 1024

F32 = jnp.float32
BF16 = jnp.bfloat16

_HEADS_PER_STEP = 2
_IN_STEPS = N_HEADS // _HEADS_PER_STEP
_QKV_BLK = _HEADS_PER_STEP * V_DIM
_GATE_BLK = 2 * D_MODEL // _IN_STEPS
_O_K = D_MODEL // _QKV_BLK
_O_V = 2 * D_MODEL // _QKV_BLK
_O_U = 3 * D_MODEL // POOL_GROUP_DIM
_O_G = (3 * D_MODEL + POOL_WIDTH) // _GATE_BLK


def _params(*sem):
    return pltpu.CompilerParams(dimension_semantics=sem, vmem_limit_bytes=VMEM_LIMIT)


def _rope(z, cos, sin_signed, first_half):
    outs = []
    for c in range(z.shape[1] // LANES):
        zc = z[:, c * LANES:(c + 1) * LANES]
        rot = jnp.where(first_half, pltpu.roll(zc, LANES - HEAD_DIM // 2, 1),
                        pltpu.roll(zc, HEAD_DIM // 2, 1))
        outs.append(zc * cos + rot * sin_signed)
    return jnp.concatenate(outs, axis=1)


def _inproj_kernel(x_ref, g_ref, cos_ref, sin_ref, wq_ref, wk_ref, wv_ref, wu_ref, wg_ref, bg_ref,
                   q_ref, k_ref, v_ref, u_ref, gate_ref, hn_ref):
    @pl.when(pl.program_id(1) == 0)
    def _():
        x = x_ref[...]
        ms = jnp.mean(x * x, axis=-1, keepdims=True)
        hn_ref[...] = (x * lax.rsqrt(ms + EPS) * g_ref[...]).astype(BF16)

    hn = hn_ref[...]
    cos = cos_ref[...]
    sin = sin_ref[...]
    lane = lax.broadcasted_iota(jnp.int32, cos.shape, 1)
    first_half = jnp.bitwise_and(lane, HEAD_DIM - 1) < (HEAD_DIM // 2)
    zq = jnp.dot(hn, wq_ref[...], preferred_element_type=F32)
    q_ref[...] = (_rope(zq, cos, sin, first_half) * (1.0 / math.sqrt(HEAD_DIM))).astype(BF16)
    zk = jnp.dot(hn, wk_ref[...], preferred_element_type=F32)
    k_ref[...] = _rope(zk, cos, sin, first_half).astype(BF16)
    v_ref[...] = jnp.dot(hn, wv_ref[...], preferred_element_type=F32).astype(BF16)
    u_ref[...] = jnp.dot(hn, wu_ref[...], preferred_element_type=F32)
    zg = jnp.dot(hn, wg_ref[...], preferred_element_type=F32) + bg_ref[...]
    gate_ref[...] = (1.0 / (1.0 + jnp.exp(-zg))).astype(BF16)


def _inproj(x2, g1, cos, sin, w_in, b_gate, tm):
    n = x2.shape[0]
    n_pos_blocks = cos.shape[0] // tm
    row = lambda i, j: (i, 0)
    pos = lambda i, j: (i % n_pos_blocks, 0)
    return pl.pallas_call(
        _inproj_kernel,
        grid=(n // tm, _IN_STEPS),
        in_specs=[
            pl.BlockSpec((tm, D_MODEL), row),
            pl.BlockSpec((1, D_MODEL), lambda i, j: (0, 0)),
            pl.BlockSpec((tm, LANES), pos),
            pl.BlockSpec((tm, LANES), pos),
            pl.BlockSpec((D_MODEL, _QKV_BLK), lambda i, j: (0, j)),
            pl.BlockSpec((D_MODEL, _QKV_BLK), lambda i, j: (0, _O_K + j)),
            pl.BlockSpec((D_MODEL, _QKV_BLK), lambda i, j: (0, _O_V + j)),
            pl.BlockSpec((D_MODEL, POOL_GROUP_DIM), lambda i, j: (0, _O_U + j)),
            pl.BlockSpec((D_MODEL, _GATE_BLK), lambda i, j: (0, _O_G + j)),
            pl.BlockSpec((1, _GATE_BLK), lambda i, j: (0, j)),
        ],
        out_specs=[
            pl.BlockSpec((tm, _QKV_BLK), lambda i, j: (i, j)),
            pl.BlockSpec((tm, _QKV_BLK), lambda i, j: (i, j)),
            pl.BlockSpec((tm, _QKV_BLK), lambda i, j: (i, j)),
            pl.BlockSpec((tm, POOL_GROUP_DIM), lambda i, j: (i, j)),
            pl.BlockSpec((tm, _GATE_BLK), lambda i, j: (i, j)),
        ],
        out_shape=[
            jax.ShapeDtypeStruct((n, D_MODEL), BF16),
            jax.ShapeDtypeStruct((n, D_MODEL), BF16),
            jax.ShapeDtypeStruct((n, D_MODEL), BF16),
            jax.ShapeDtypeStruct((n, POOL_WIDTH), F32),
            jax.ShapeDtypeStruct((n, 2 * D_MODEL), BF16),
        ],
        scratch_shapes=[pltpu.VMEM((tm, D_MODEL), BF16)],
        compiler_params=_params("arbitrary", "arbitrary"),
        name="inproj",
    )(x2, g1, cos, sin, w_in, w_in, w_in, w_in, w_in, b_gate)


def _attn_kernel(lq1_ref, lk1_ref, lq2_ref, lk2_ref, subg_ref, q_ref, k_ref, v_ref, km_ref, vm_ref,
                 o_ref, qq_sc, m_sc, l_sc, acc_sc, *, bq):
    qi = pl.program_id(2)
    q = q_ref[...]
    lane = lax.broadcasted_iota(jnp.int32, q.shape, 1)
    zero = jnp.zeros_like(q)
    qq_sc[0:bq, :] = jnp.where(lane < HEAD_DIM, q, zero)
    qq_sc[bq:2 * bq, :] = jnp.where(lane >= HEAD_DIM, q, zero)

    def scores(kb):
        return lax.dot_general(qq_sc[...], kb, (((1,), (1,)), ((), ())), preferred_element_type=F32)

    s = scores(km_ref[...])
    col = lax.broadcasted_iota(jnp.int32, s.shape, 1)
    s = jnp.where(col < N_META, s, NEG_BIG)
    m0 = jnp.max(s, axis=-1, keepdims=True)
    p = jnp.exp(s - m0)
    m_sc[...] = m0
    l_sc[...] = jnp.sum(p, axis=-1, keepdims=True)
    acc_sc[...] = jnp.dot(p.astype(BF16), vm_ref[...], preferred_element_type=F32)

    def update(j, masked):
        start = pl.multiple_of(j * bq, bq)
        s = scores(k_ref[pl.ds(start, bq), :])
        if masked:
            r = jnp.bitwise_and(lax.broadcasted_iota(jnp.int32, s.shape, 0), bq - 1)
            c = lax.broadcasted_iota(jnp.int32, s.shape, 1)
            s = jnp.where(c <= r, s, NEG_BIG)
        m_prev = m_sc[...]
        m_new = jnp.maximum(m_prev, jnp.max(s, axis=-1, keepdims=True))
        alpha = jnp.exp(m_prev - m_new)
        p = jnp.exp(s - m_new)
        l_sc[...] = alpha * l_sc[...] + jnp.sum(p, axis=-1, keepdims=True)
        acc_sc[...] = alpha * acc_sc[...] + jnp.dot(p.astype(BF16), v_ref[pl.ds(start, bq), :],
                                                    preferred_element_type=F32)
        m_sc[...] = m_new

    def body(j, carry):
        update(j, False)
        return carry

    lax.fori_loop(0, qi, body, 0)
    update(qi, True)

    lam = (jnp.exp(jnp.sum(lq1_ref[...] * lk1_ref[...], axis=-1, keepdims=True))
           - jnp.exp(jnp.sum(lq2_ref[...] * lk2_ref[...], axis=-1, keepdims=True)) + LAMBDA_INIT)
    o1 = acc_sc[0:bq, :] / l_sc[0:bq, :]
    o2 = acc_sc[bq:2 * bq, :] / l_sc[bq:2 * bq, :]
    o = o1 - lam * o2
    ms = jnp.mean(o * o, axis=-1, keepdims=True)
    o_ref[...] = (o * lax.rsqrt(ms + EPS) * subg_ref[...] * (1.0 - LAMBDA_INIT)).astype(BF16)


def _attention(q, k, v, km, vm, lq1, lk1, lq2, lk2, subg, batch, seq, bq):
    nq = seq // bq
    small = lambda b, h, i: (0, 0)
    return pl.pallas_call(
        functools.partial(_attn_kernel, bq=bq),
        grid=(batch, N_HEADS, nq),
        in_specs=[
            pl.BlockSpec((1, HEAD_DIM), small),
            pl.BlockSpec((1, HEAD_DIM), small),
            pl.BlockSpec((1, HEAD_DIM), small),
            pl.BlockSpec((1, HEAD_DIM), small),
            pl.BlockSpec((1, V_DIM), small),
            pl.BlockSpec((bq, V_DIM), lambda b, h, i: (b * nq + i, h)),
            pl.BlockSpec((seq, V_DIM), lambda b, h, i: (b, h)),
            pl.BlockSpec((seq, V_DIM), lambda b, h, i: (b, h)),
            pl.BlockSpec((LANES, V_DIM), lambda b, h, i: (0, h)),
            pl.BlockSpec((LANES, V_DIM), lambda b, h, i: (0, h)),
        ],
        out_specs=pl.BlockSpec((bq, V_DIM), lambda b, h, i: (b * nq + i, h)),
        out_shape=jax.ShapeDtypeStruct(q.shape, BF16),
        scratch_shapes=[
            pltpu.VMEM((2 * bq, V_DIM), BF16),
            pltpu.VMEM((2 * bq, 1), F32),
            pltpu.VMEM((2 * bq, 1), F32),
            pltpu.VMEM((2 * bq, V_DIM), F32),
        ],
        compiler_params=_params("arbitrary", "arbitrary", "arbitrary"),
        name="diff_attn",
    )(lq1, lk1, lq2, lk2, subg, q, k, v, km, vm)


def _mix_kernel(x_ref, a_ref, u_ref, uprev_ref, umeta_ref, gate_ref, wa_ref, pw_ref, ps_ref, wp_ref,
                wo_ref, g2_ref, wr_ref, br_ref, h2_ref, hn2_ref, logt_ref, ext_sc, *, tm, tiles_per_seq):
    i = pl.program_id(0)
    first = (i % tiles_per_seq) == 0

    @pl.when(first)
    def _():
        ext_sc[0:N_META, :] = umeta_ref[...]

    @pl.when(jnp.logical_not(first))
    def _():
        ext_sc[0:N_META, :] = uprev_ref[...]

    ext_sc[N_META:N_META + tm, :] = u_ref[...]

    pooled = []
    for g, w in enumerate(POOL_WINDOWS):
        cs = slice(g * POOL_GROUP_DIM, (g + 1) * POOL_GROUP_DIM)
        cur = ext_sc[N_META:N_META + tm, cs]
        tot = cur
        for kk in range(1, w):
            tot = tot + ext_sc[N_META - kk:N_META - kk + tm, cs]
        y = (tot * (1.0 / w) - cur).astype(BF16)
        pooled.append(jnp.dot(y, pw_ref[g], preferred_element_type=F32))
    yp = (jnp.concatenate(pooled, axis=1) * ps_ref[...]).astype(BF16)
    y_pool = jnp.dot(yp, wp_ref[...], preferred_element_type=F32)
    y_attn = jnp.dot(a_ref[...], wa_ref[...], preferred_element_type=F32)
    mixed = (gate_ref[:, 0:D_MODEL].astype(F32) * y_attn
             + gate_ref[:, D_MODEL:2 * D_MODEL].astype(F32) * y_pool)
    h2 = x_ref[...] + jnp.dot(mixed.astype(BF16), wo_ref[...], preferred_element_type=F32)
    h2_ref[...] = h2
    ms = jnp.mean(h2 * h2, axis=-1, keepdims=True)
    hn2 = h2 * lax.rsqrt(ms + EPS) * g2_ref[...]
    hn2_hi = hn2.astype(BF16)
    hn2_ref[...] = hn2_hi
    hn2_lo = (hn2 - hn2_hi.astype(F32)).astype(BF16)
    wr = wr_ref[...]
    wr_hi = wr.astype(BF16)
    wr_lo = (wr - wr_hi.astype(F32)).astype(BF16)
    nt = (((1,), (1,)), ((), ()))
    logt = (lax.dot_general(wr_hi, hn2_hi, nt, preferred_element_type=F32)
            + lax.dot_general(wr_hi, hn2_lo, nt, preferred_element_type=F32)
            + lax.dot_general(wr_lo, hn2_hi, nt, preferred_element_type=F32))
    logt_ref[...] = logt + br_ref[...]


def _mix(x2, a, u, umeta, gates, w_attn, pool_w, pool_scale, w_pool, w_out, g2, wr_t, br_t, seq, tm):
    n = x2.shape[0]
    tiles_per_seq = seq // tm
    halo_blocks = tm // N_META
    const2 = lambda i: (0, 0)
    n_rt = wr_t.shape[0]
    return pl.pallas_call(
        functools.partial(_mix_kernel, tm=tm, tiles_per_seq=tiles_per_seq),
        grid=(n // tm,),
        in_specs=[
            pl.BlockSpec((tm, D_MODEL), lambda i: (i, 0)),
            pl.BlockSpec((tm, D_MODEL), lambda i: (i, 0)),
            pl.BlockSpec((tm, POOL_WIDTH), lambda i: (i, 0)),
            pl.BlockSpec((N_META, POOL_WIDTH), lambda i: (jnp.maximum(i * halo_blocks - 1, 0), 0)),
            pl.BlockSpec((N_META, POOL_WIDTH), const2),
            pl.BlockSpec((tm, 2 * D_MODEL), lambda i: (i, 0)),
            pl.BlockSpec((D_MODEL, D_MODEL), const2),
            pl.BlockSpec((N_POOL_GROUPS, POOL_GROUP_DIM, POOL_GROUP_DIM), lambda i: (0, 0, 0)),
            pl.BlockSpec((1, POOL_WIDTH), const2),
            pl.BlockSpec((POOL_WIDTH, D_MODEL), const2),
            pl.BlockSpec((D_MODEL, D_MODEL), const2),
            pl.BlockSpec((1, D_MODEL), const2),
            pl.BlockSpec((n_rt, D_MODEL), const2),
            pl.BlockSpec((n_rt, 1), const2),
        ],
        out_specs=[
            pl.BlockSpec((tm, D_MODEL), lambda i: (i, 0)),
            pl.BlockSpec((tm, D_MODEL), lambda i: (i, 0)),
            pl.BlockSpec((n_rt, tm), lambda i: (0, i)),
        ],
        out_shape=[
            jax.ShapeDtypeStruct((n, D_MODEL), F32),
            jax.ShapeDtypeStruct((n, D_MODEL), BF16),
            jax.ShapeDtypeStruct((n_rt, n), F32),
        ],
        scratch_shapes=[pltpu.VMEM((tm + N_META, POOL_WIDTH), F32)],
        compiler_params=_params("arbitrary"),
        name="mix",
    )(x2, a, u, u, umeta, gates, w_attn, pool_w, pool_scale, w_pool, w_out, g2, wr_t, br_t)


def _route_kernel(logt_ref, comb_ref):
    lg = logt_ref[...]
    g = [lg[r:r + 1, :] for r in range(N_GROUPS)]
    gmax = functools.reduce(jnp.maximum, g)
    gsel = jnp.full_like(g[0], N_GROUPS - 1).astype(jnp.int32)
    for r in range(N_GROUPS - 2, -1, -1):
        gsel = jnp.where(g[r] == gmax, r, gsel)
    p_group = 1.0 / functools.reduce(lambda a, b: a + b, [jnp.exp(x - gmax) for x in g])
    e = []
    for jj in range(EXPERTS_PER_GROUP):
        v = lg[N_GROUPS + jj:N_GROUPS + jj + 1, :]
        for r in range(1, N_GROUPS):
            row = N_GROUPS + r * EXPERTS_PER_GROUP + jj
            v = jnp.where(gsel == r, lg[row:row + 1, :], v)
        e.append(v)
    v1 = functools.reduce(jnp.maximum, e)
    i1 = jnp.full_like(gsel, EXPERTS_PER_GROUP - 1)
    for jj in range(EXPERTS_PER_GROUP - 2, -1, -1):
        i1 = jnp.where(e[jj] == v1, jj, i1)
    rest = [jnp.where(i1 == jj, -jnp.inf, e[jj]) for jj in range(EXPERTS_PER_GROUP)]
    v2 = functools.reduce(jnp.maximum, rest)
    i2 = jnp.full_like(gsel, EXPERTS_PER_GROUP - 1)
    for jj in range(EXPERTS_PER_GROUP - 2, -1, -1):
        i2 = jnp.where(jnp.logical_and(rest[jj] == v2, i1 != jj), jj, i2)
    t = jnp.exp(v2 - v1)
    w1 = p_group / (1.0 + t)
    w2 = p_group * t / (1.0 + t)
    e1 = gsel * EXPERTS_PER_GROUP + i1
    e2 = gsel * EXPERTS_PER_GROUP + i2
    eid = lax.broadcasted_iota(jnp.int32, comb_ref.shape, 0)
    comb_ref[...] = jnp.where(eid == e1, w1, 0.0) + jnp.where(eid == e2, w2, 0.0)


def _route(logt, tl):
    n_rt, n = logt.shape
    return pl.pallas_call(
        _route_kernel,
        grid=(n // tl,),
        in_specs=[pl.BlockSpec((n_rt, tl), lambda i: (0, i))],
        out_specs=pl.BlockSpec((N_EXPERTS, tl), lambda i: (0, i)),
        out_shape=jax.ShapeDtypeStruct((N_EXPERTS, n), F32),
        compiler_params=_params("arbitrary"),
        name="route",
    )(logt)


def _moe_kernel(h2_ref, hn2_ref, comb_ref, wg_ref, wu_ref, wd_ref, gf_ref, o_ref, acc_sc):
    e = pl.program_id(1)

    @pl.when(e == 0)
    def _():
        acc_sc[...] = h2_ref[...]

    x = hn2_ref[...]
    gate = jnp.dot(x, wg_ref[0], preferred_element_type=F32)
    up = jnp.dot(x, wu_ref[0], preferred_element_type=F32)
    comb = comb_ref[...]
    lane = lax.broadcasted_iota(jnp.int32, comb.shape, 1)
    w = jnp.sum(jnp.where(lane == e, comb, 0.0), axis=-1, keepdims=True)
    act = gate / (1.0 + jnp.exp(-gate)) * up * w
    acc_sc[...] += jnp.dot(act.astype(BF16), wd_ref[0], preferred_element_type=F32)

    @pl.when(e == N_EXPERTS - 1)
    def _():
        h = acc_sc[...]
        ms = jnp.mean(h * h, axis=-1, keepdims=True)
        o_ref[...] = h * lax.rsqrt(ms + EPS) * gf_ref[...]


def _moe(h2, hn2, comb, wg, wu, wd, gf, tm):
    n = h2.shape[0]
    return pl.pallas_call(
        _moe_kernel,
        grid=(n // tm, N_EXPERTS),
        in_specs=[
            pl.BlockSpec((tm, D_MODEL), lambda i, e: (i, 0)),
            pl.BlockSpec((tm, D_MODEL), lambda i, e: (i, 0)),
            pl.BlockSpec((tm, N_EXPERTS), lambda i, e: (i, 0)),
            pl.BlockSpec((1, D_MODEL, D_EXPERT), lambda i, e: (e, 0, 0)),
            pl.BlockSpec((1, D_MODEL, D_EXPERT), lambda i, e: (e, 0, 0)),
            pl.BlockSpec((1, D_EXPERT, D_MODEL), lambda i, e: (e, 0, 0)),
            pl.BlockSpec((1, D_MODEL), lambda i, e: (0, 0)),
        ],
        out_specs=pl.BlockSpec((tm, D_MODEL), lambda i, e: (i, 0)),
        out_shape=jax.ShapeDtypeStruct((n, D_MODEL), F32),
        scratch_shapes=[pltpu.VMEM((tm, D_MODEL), F32)],
        compiler_params=_params("arbitrary", "arbitrary"),
        name="moe",
    )(h2, hn2, comb, wg, wu, wd, gf)


def _rope_tables(t):
    inv = 1.0 / (ROPE_THETA ** (jnp.arange(0, HEAD_DIM, 2, dtype=F32) / HEAD_DIM))
    ang = jnp.arange(t, dtype=F32)[:, None] * inv[None, :]
    ang = jnp.concatenate([ang, ang, ang, ang], axis=-1)
    sign = jnp.where((jnp.arange(LANES) % HEAD_DIM) < HEAD_DIM // 2, -1.0, 1.0).astype(F32)
    return jnp.cos(ang), jnp.sin(ang) * sign[None, :]


def kernel(x, meta, norm1_g, w_in, b_gate, lambda_q1, lambda_k1, lambda_q2, lambda_k2, subln_g, pool_w,
           pool_scale, w_attn_br, w_pool_br, w_out, norm2_g, w_router_group, b_router_group,
           w_router_expert, b_router_expert, w_e_gate, w_e_up, w_e_down, final_g):
    batch, seq, d = x.shape
    n = batch * seq
    x2 = x.reshape(n, d)
    cos, sin = _rope_tables(seq + N_META)
    w_in_b = w_in[0].astype(BF16)

    q, k, v, u, gates = _inproj(x2, norm1_g, cos[N_META:], sin[N_META:], w_in_b, b_gate, tm=1024)
    _, km, vm, um, _ = _inproj(meta, norm1_g, cos[:N_META], sin[:N_META], w_in_b, b_gate, tm=N_META)
    pad = ((0, LANES - N_META), (0, 0))
    a = _attention(q, k, v, jnp.pad(km, pad), jnp.pad(vm, pad), lambda_q1, lambda_k1, lambda_q2,
                   lambda_k2, subln_g, batch, seq, bq=256)

    n_rt = 32
    wr_t = jnp.concatenate([w_router_group[0], w_router_expert[0]], axis=1).T
    wr_t = jnp.pad(wr_t, ((0, n_rt - wr_t.shape[0]), (0, 0)))
    br_t = jnp.concatenate([b_router_group[0], b_router_expert[0]])
    br_t = jnp.pad(br_t, (0, n_rt - br_t.shape[0])).reshape(n_rt, 1)
    h2, hn2, logt = _mix(x2, a, u, um, gates, w_attn_br[0].astype(BF16), pool_w[0].astype(BF16),
                         pool_scale, w_pool_br[0].astype(BF16), w_out[0].astype(BF16), norm2_g,
                         wr_t, br_t, seq, tm=512)
    comb_t = _route(logt, tl=2048)
    out = _moe(h2, hn2, comb_t.T, w_e_gate[0].astype(BF16), w_e_up[0].astype(BF16),
               w_e_down[0].astype(BF16), final_g.reshape(1, d), tm=1024)
    return out.reshape(batch, seq, d)
```

```python
import functools
import math

import jax
import jax.numpy as jnp
from jax import lax
from jax.experimental import pallas as pl
from jax.experimental.pallas import tpu as pltpu

D_MODEL = 1024
N_META = 16
N_HEADS = 8
HEAD_DIM = 64
V_DIM = 2 * HEAD_DIM
POOL_WINDOWS = (2, 4, 8, 16)
N_POOL_GROUPS = len(POOL_WINDOWS)
POOL_GROUP_DIM = 128
POOL_WIDTH = N_POOL_GROUPS * POOL_GROUP_DIM
ROPE_THETA = 10000.0
N_GROUPS = 4
EXPERTS_PER_GROUP = 4
N_EXPERTS = N_GROUPS * EXPERTS_PER_GROUP
D_EXPERT = 512
EPS = 1e-6
LAMBDA_INIT = 0.8 - 0.6 * math.exp(-0.3 * 0)
LOG2_E = math.log2(math.e)

LANES = 128
NEG_BIG = -1e30
VMEM_LIMIT = 48 * 1024 * 1024

F32 = jnp.float32
BF16 = jnp.bfloat16

_HEADS_PER_STEP = 2
_IN_STEPS = N_HEADS // _HEADS_PER_STEP
_QKV_BLK = _HEADS_PER_STEP * V_DIM
_GATE_BLK = 2 * D_MODEL // _IN_STEPS
_O_K = D_MODEL // _QKV_BLK
_O_V = 2 * D_MODEL // _QKV_BLK
_O_U = 3 * D_MODEL // POOL_GROUP_DIM
_O_G = (3 * D_MODEL + POOL_WIDTH) // _GATE_BLK


def _params(*sem):
    return pltpu.CompilerParams(dimension_semantics=sem, vmem_limit_bytes=VMEM_LIMIT)


def _rope(z, cos, sin_signed, first_half):
    outs = []
    for c in range(z.shape[1] // LANES):
        zc = z[:, c * LANES:(c + 1) * LANES]
        rot = jnp.where(first_half, pltpu.roll(zc, LANES - HEAD_DIM // 2, 1),
                        pltpu.roll(zc, HEAD_DIM // 2, 1))
        outs.append(zc * cos + rot * sin_signed)
    return jnp.concatenate(outs, axis=1)


def _inproj_kernel(x_ref, g_ref, cos_ref, sin_ref, wq_ref, wk_ref, wv_ref, wu_ref, wg_ref, bg_ref,
                   q_ref, k_ref, v_ref, u_ref, gate_ref, hn_ref):
    @pl.when(pl.program_id(1) == 0)
    def _():
        x = x_ref[...]
        ms = jnp.mean(x * x, axis=-1, keepdims=True)
        hn_ref[...] = (x * lax.rsqrt(ms + EPS) * g_ref[...]).astype(BF16)

    hn = hn_ref[...]
    cos = cos_ref[...]
    sin = sin_ref[...]
    lane = lax.broadcasted_iota(jnp.int32, cos.shape, 1)
    first_half = jnp.bitwise_and(lane, HEAD_DIM - 1) < (HEAD_DIM // 2)
    zq = jnp.dot(hn, wq_ref[...], preferred_element_type=F32)
    q_ref[...] = (_rope(zq, cos, sin, first_half) * (LOG2_E / math.sqrt(HEAD_DIM))).astype(BF16)
    zk = jnp.dot(hn, wk_ref[...], preferred_element_type=F32)
    k_ref[...] = _rope(zk, cos, sin, first_half).astype(BF16)
    v_ref[...] = jnp.dot(hn, wv_ref[...], preferred_element_type=F32).astype(BF16)
    u_ref[...] = jnp.dot(hn, wu_ref[...], preferred_element_type=F32)
    zg = jnp.dot(hn, wg_ref[...], preferred_element_type=F32) + bg_ref[...]
    gate_ref[...] = (1.0 / (1.0 + jnp.exp(-zg))).astype(BF16)


def _inproj(x2, g1, cos, sin, w_in, b_gate, tm):
    n = x2.shape[0]
    n_pos_blocks = cos.shape[0] // tm
    row = lambda i, j: (i, 0)
    pos = lambda i, j: (i % n_pos_blocks, 0)
    return pl.pallas_call(
        _inproj_kernel,
        grid=(n // tm, _IN_STEPS),
        in_specs=[
            pl.BlockSpec((tm, D_MODEL), row),
            pl.BlockSpec((1, D_MODEL), lambda i, j: (0, 0)),
            pl.BlockSpec((tm, LANES), pos),
            pl.BlockSpec((tm, LANES), pos),
            pl.BlockSpec((D_MODEL, _QKV_BLK), lambda i, j: (0, j)),
            pl.BlockSpec((D_MODEL, _QKV_BLK), lambda i, j: (0, _O_K + j)),
            pl.BlockSpec((D_MODEL, _QKV_BLK), lambda i, j: (0, _O_V + j)),
            pl.BlockSpec((D_MODEL, POOL_GROUP_DIM), lambda i, j: (0, _O_U + j)),
            pl.BlockSpec((D_MODEL, _GATE_BLK), lambda i, j: (0, _O_G + j)),
            pl.BlockSpec((1, _GATE_BLK), lambda i, j: (0, j)),
        ],
        out_specs=[
            pl.BlockSpec((tm, _QKV_BLK), lambda i, j: (i, j)),
            pl.BlockSpec((tm, _QKV_BLK), lambda i, j: (i, j)),
            pl.BlockSpec((tm, _QKV_BLK), lambda i, j: (i, j)),
            pl.BlockSpec((tm, POOL_GROUP_DIM), lambda i, j: (i, j)),
            pl.BlockSpec((tm, _GATE_BLK), lambda i, j: (i, j)),
        ],
        out_shape=[
            jax.ShapeDtypeStruct((n, D_MODEL), BF16),
            jax.ShapeDtypeStruct((n, D_MODEL), BF16),
            jax.ShapeDtypeStruct((n, D_MODEL), BF16),
            jax.ShapeDtypeStruct((n, POOL_WIDTH), F32),
            jax.ShapeDtypeStruct((n, 2 * D_MODEL), BF16),
        ],
        scratch_shapes=[pltpu.VMEM((tm, D_MODEL), BF16)],
        compiler_params=_params("arbitrary", "arbitrary"),
        name="inproj",
    )(x2, g1, cos, sin, w_in, w_in, w_in, w_in, w_in, b_gate)


def _attn_kernel(lq1_ref, lk1_ref, lq2_ref, lk2_ref, subg_ref, q_ref, k_ref, v_ref, km_ref, vm_ref,
                 o_ref, qq_sc, vext_sc, m_sc, acc_sc, *, bq):
    qi = pl.program_id(2)

    @pl.when(qi == 0)
    def _():
        vext_sc[:, 0:V_DIM] = v_ref[...]
        vext_sc[:, V_DIM:2 * V_DIM] = jnp.ones(v_ref.shape, BF16)

    q = q_ref[...]
    lane = lax.broadcasted_iota(jnp.int32, q.shape, 1)
    zero = jnp.zeros_like(q)
    qq_sc[0:bq, :] = jnp.where(lane < HEAD_DIM, q, zero)
    qq_sc[bq:2 * bq, :] = jnp.where(lane >= HEAD_DIM, q, zero)

    def step(kb, vb, mask, first):
        s = lax.dot_general(qq_sc[...], kb, (((1,), (1,)), ((), ())), preferred_element_type=F32)
        if mask is not None:
            s = jnp.where(mask, s, NEG_BIG)
        n_tiles = s.shape[1] // LANES
        rm = s[:, 0:LANES]
        for t in range(1, n_tiles):
            rm = jnp.maximum(rm, s[:, t * LANES:(t + 1) * LANES])
        rmax = jnp.max(rm, axis=-1, keepdims=True)
        if first:
            m_new = jnp.broadcast_to(rmax, (2 * bq, LANES))
        else:
            m_prev = m_sc[...]
            m_new = jnp.maximum(m_prev, rmax)
        p = jnp.exp2(s - jnp.concatenate([m_new] * n_tiles, axis=1))
        pv = jnp.dot(p.astype(BF16), vb, preferred_element_type=F32)
        if first:
            acc_sc[...] = pv
        else:
            alpha = jnp.exp2(m_prev - m_new)
            acc_sc[...] = jnp.concatenate([alpha, alpha], axis=1) * acc_sc[...] + pv
        m_sc[...] = m_new

    r = jnp.bitwise_and(lax.broadcasted_iota(jnp.int32, (2 * bq, bq + LANES), 0), bq - 1)
    c = lax.broadcasted_iota(jnp.int32, (2 * bq, bq + LANES), 1)
    visible = jnp.logical_or(c <= r, jnp.logical_and(c >= bq, c < bq + N_META))
    start = pl.multiple_of(qi * bq, bq)
    vm_ext = jnp.concatenate([vm_ref[...], jnp.ones((LANES, V_DIM), BF16)], axis=1)
    step(jnp.concatenate([k_ref[pl.ds(start, bq), :], km_ref[...]], axis=0),
         jnp.concatenate([vext_sc[pl.ds(start, bq), :], vm_ext], axis=0), visible, True)

    def full_block(j):
        start = pl.multiple_of(j * bq, bq)
        step(k_ref[pl.ds(start, bq), :], vext_sc[pl.ds(start, bq), :], None, False)

    def body(jj, carry):
        full_block(2 * jj)
        full_block(2 * jj + 1)
        return carry

    lax.fori_loop(0, qi // 2, body, 0)

    @pl.when(qi % 2 == 1)
    def _():
        full_block(qi - 1)

    lam = (jnp.exp(jnp.sum(lq1_ref[...] * lk1_ref[...], axis=-1, keepdims=True))
           - jnp.exp(jnp.sum(lq2_ref[...] * lk2_ref[...], axis=-1, keepdims=True)) + LAMBDA_INIT)
    o1 = acc_sc[0:bq, 0:V_DIM] / acc_sc[0:bq, V_DIM:2 * V_DIM]
    o2 = acc_sc[bq:2 * bq, 0:V_DIM] / acc_sc[bq:2 * bq, V_DIM:2 * V_DIM]
    o = o1 - lam * o2
    ms = jnp.mean(o * o, axis=-1, keepdims=True)
    o_ref[...] = (o * lax.rsqrt(ms + EPS) * subg_ref[...] * (1.0 - LAMBDA_INIT)).astype(BF16)


def _attention(q, k, v, km, vm, lq1, lk1, lq2, lk2, subg, batch, seq, bq):
    nq = seq // bq
    small = lambda b, h, i: (0, 0)
    return pl.pallas_call(
        functools.partial(_attn_kernel, bq=bq),
        grid=(batch, N_HEADS, nq),
        in_specs=[
            pl.BlockSpec((1, HEAD_DIM), small),
            pl.BlockSpec((1, HEAD_DIM), small),
            pl.BlockSpec((1, HEAD_DIM), small),
            pl.BlockSpec((1, HEAD_DIM), small),
            pl.BlockSpec((1, V_DIM), small),
            pl.BlockSpec((bq, V_DIM), lambda b, h, i: (b * nq + i, h)),
            pl.BlockSpec((seq, V_DIM), lambda b, h, i: (b, h)),
            pl.BlockSpec((seq, V_DIM), lambda b, h, i: (b, h)),
            pl.BlockSpec((LANES, V_DIM), lambda b, h, i: (0, h)),
            pl.BlockSpec((LANES, V_DIM), lambda b, h, i: (0, h)),
        ],
        out_specs=pl.BlockSpec((bq, V_DIM), lambda b, h, i: (b * nq + i, h)),
        out_shape=jax.ShapeDtypeStruct(q.shape, BF16),
        scratch_shapes=[
            pltpu.VMEM((2 * bq, V_DIM), BF16),
            pltpu.VMEM((seq, 2 * V_DIM), BF16),
            pltpu.VMEM((2 * bq, LANES), F32),
            pltpu.VMEM((2 * bq, 2 * V_DIM), F32),
        ],
        compiler_params=_params("arbitrary", "arbitrary", "arbitrary"),
        name="diff_attn",
    )(lq1, lk1, lq2, lk2, subg, q, k, v, km, vm)


def _mix_kernel(x_ref, a_ref, u_ref, uprev_ref, umeta_ref, gate_ref, wa_ref, pw_ref, ps_ref, wp_ref,
                wo_ref, g2_ref, wr_ref, br_ref, h2_ref, hn2_ref, logt_ref, ext_sc, *, tm, tiles_per_seq):
    i = pl.program_id(0)
    first = (i % tiles_per_seq) == 0

    @pl.when(first)
    def _():
        ext_sc[0:N_META, :] = umeta_ref[...]

    @pl.when(jnp.logical_not(first))
    def _():
        ext_sc[0:N_META, :] = uprev_ref[...]

    ext_sc[N_META:N_META + tm, :] = u_ref[...]

    pooled = []
    for g, w in enumerate(POOL_WINDOWS):
        cs = slice(g * POOL_GROUP_DIM, (g + 1) * POOL_GROUP_DIM)
        cur = ext_sc[N_META:N_META + tm, cs]
        tot = cur
        for kk in range(1, w):
            tot = tot + ext_sc[N_META - kk:N_META - kk + tm, cs]
        y = (tot * (1.0 / w) - cur).astype(BF16)
        pooled.append(jnp.dot(y, pw_ref[g], preferred_element_type=F32))
    yp = (jnp.concatenate(pooled, axis=1) * ps_ref[...]).astype(BF16)
    y_pool = jnp.dot(yp, wp_ref[...], preferred_element_type=F32)
    y_attn = jnp.dot(a_ref[...], wa_ref[...], preferred_element_type=F32)
    mixed = (gate_ref[:, 0:D_MODEL].astype(F32) * y_attn
             + gate_ref[:, D_MODEL:2 * D_MODEL].astype(F32) * y_pool)
    h2 = x_ref[...] + jnp.dot(mixed.astype(BF16), wo_ref[...], preferred_element_type=F32)
    h2_ref[...] = h2
    ms = jnp.mean(h2 * h2, axis=-1, keepdims=True)
    hn2 = h2 * lax.rsqrt(ms + EPS) * g2_ref[...]
    hn2_hi = hn2.astype(BF16)
    hn2_ref[...] = hn2_hi
    hn2_lo = (hn2 - hn2_hi.astype(F32)).astype(BF16)
    wr = wr_ref[...]
    wr_hi = wr.astype(BF16)
    wr_lo = (wr - wr_hi.astype(F32)).astype(BF16)
    nt = (((1,), (1,)), ((), ()))
    logt = (lax.dot_general(wr_hi, hn2_hi, nt, preferred_element_type=F32)
            + lax.dot_general(wr_hi, hn2_lo, nt, preferred_element_type=F32)
            + lax.dot_general(wr_lo, hn2_hi, nt, preferred_element_type=F32))
    logt_ref[...] = logt + br_ref[...]


def _mix(x2, a, u, umeta, gates, w_attn, pool_w, pool_scale, w_pool, w_out, g2, wr_t, br_t, seq, tm):
    n = x2.shape[0]
    tiles_per_seq = seq // tm
    halo_blocks = tm // N_META
    const2 = lambda i: (0, 0)
    n_rt = wr_t.shape[0]
    return pl.pallas_call(
        functools.partial(_mix_kernel, tm=tm, tiles_per_seq=tiles_per_seq),
        grid=(n // tm,),
        in_specs=[
            pl.BlockSpec((tm, D_MODEL), lambda i: (i, 0)),
            pl.BlockSpec((tm, D_MODEL), lambda i: (i, 0)),
            pl.BlockSpec((tm, POOL_WIDTH), lambda i: (i, 0)),
            pl.BlockSpec((N_META, POOL_WIDTH), lambda i: (jnp.maximum(i * halo_blocks - 1, 0), 0)),
            pl.BlockSpec((N_META, POOL_WIDTH), const2),
            pl.BlockSpec((tm, 2 * D_MODEL), lambda i: (i, 0)),
            pl.BlockSpec((D_MODEL, D_MODEL), const2),
            pl.BlockSpec((N_POOL_GROUPS, POOL_GROUP_DIM, POOL_GROUP_DIM), lambda i: (0, 0, 0)),
            pl.BlockSpec((1, POOL_WIDTH), const2),
            pl.BlockSpec((POOL_WIDTH, D_MODEL), const2),
            pl.BlockSpec((D_MODEL, D_MODEL), const2),
            pl.BlockSpec((1, D_MODEL), const2),
            pl.BlockSpec((n_rt, D_MODEL), const2),
            pl.BlockSpec((n_rt, 1), const2),
        ],
        out_specs=[
            pl.BlockSpec((tm, D_MODEL), lambda i: (i, 0)),
            pl.BlockSpec((tm, D_MODEL), lambda i: (i, 0)),
            pl.BlockSpec((n_rt, tm), lambda i: (0, i)),
        ],
        out_shape=[
            jax.ShapeDtypeStruct((n, D_MODEL), F32),
            jax.ShapeDtypeStruct((n, D_MODEL), BF16),
            jax.ShapeDtypeStruct((n_rt, n), F32),
        ],
        scratch_shapes=[pltpu.VMEM((tm + N_META, POOL_WIDTH), F32)],
        compiler_params=_params("arbitrary"),
        name="mix",
    )(x2, a, u, u, umeta, gates, w_attn, pool_w, pool_scale, w_pool, w_out, g2, wr_t, br_t)


def _route_kernel(logt_ref, comb_ref):
    lg = logt_ref[...]
    g = [lg[r:r + 1, :] for r in range(N_GROUPS)]
    gmax = functools.reduce(jnp.maximum, g)
    gsel = jnp.full_like(g[0], N_GROUPS - 1).astype(jnp.int32)
    for r in range(N_GROUPS - 2, -1, -1):
        gsel = jnp.where(g[r] == gmax, r, gsel)
    p_group = 1.0 / functools.reduce(lambda a, b: a + b, [jnp.exp(x - gmax) for x in g])
    e = []
    for jj in range(EXPERTS_PER_GROUP):
        v = lg[N_GROUPS + jj:N_GROUPS + jj + 1, :]
        for r in range(1, N_GROUPS):
            row = N_GROUPS + r * EXPERTS_PER_GROUP + jj
            v = jnp.where(gsel == r, lg[row:row + 1, :], v)
        e.append(v)
    v1 = functools.reduce(jnp.maximum, e)
    i1 = jnp.full_like(gsel, EXPERTS_PER_GROUP - 1)
    for jj in range(EXPERTS_PER_GROUP - 2, -1, -1):
        i1 = jnp.where(e[jj] == v1, jj, i1)
    rest = [jnp.where(i1 == jj, -jnp.inf, e[jj]) for jj in range(EXPERTS_PER_GROUP)]
    v2 = functools.reduce(jnp.maximum, rest)
    i2 = jnp.full_like(gsel, EXPERTS_PER_GROUP - 1)
    for jj in range(EXPERTS_PER_GROUP - 2, -1, -1):
        i2 = jnp.where(jnp.logical_and(rest[jj] == v2, i1 != jj), jj, i2)
    t = jnp.exp(v2 - v1)
    w1 = p_group / (1.0 + t)
    w2 = p_group * t / (1.0 + t)
    e1 = gsel * EXPERTS_PER_GROUP + i1
    e2 = gsel * EXPERTS_PER_GROUP + i2
    eid = lax.broadcasted_iota(jnp.int32, comb_ref.shape, 0)
    comb_ref[...] = jnp.where(eid == e1, w1, 0.0) + jnp.where(eid == e2, w2, 0.0)


def _route(logt, tl):
    n_rt, n = logt.shape
    return pl.pallas_call(
        _route_kernel,
        grid=(n // tl,),
        in_specs=[pl.BlockSpec((n_rt, tl), lambda i: (0, i))],
        out_specs=pl.BlockSpec((N_EXPERTS, tl), lambda i: (0, i)),
        out_shape=jax.ShapeDtypeStruct((N_EXPERTS, n), F32),
        compiler_params=_params("arbitrary"),
        name="route",
    )(logt)


def _moe_kernel(h2_ref, hn2_ref, comb_ref, wg_ref, wu_ref, wd_ref, gf_ref, o_ref, acc_sc):
    e = pl.program_id(1)

    @pl.when(e == 0)
    def _():
        acc_sc[...] = h2_ref[...]

    x = hn2_ref[...]
    gate = jnp.dot(x, wg_ref[0], preferred_element_type=F32)
    up = jnp.dot(x, wu_ref[0], preferred_element_type=F32)
    comb = comb_ref[...]
    lane = lax.broadcasted_iota(jnp.int32, comb.shape, 1)
    w = jnp.sum(jnp.where(lane == e, comb, 0.0), axis=-1, keepdims=True)
    act = gate / (1.0 + jnp.exp(-gate)) * up * w
    acc_sc[...] += jnp.dot(act.astype(BF16), wd_ref[0], preferred_element_type=F32)

    @pl.when(e == N_EXPERTS - 1)
    def _():
        h = acc_sc[...]
        ms = jnp.mean(h * h, axis=-1, keepdims=True)
        o_ref[...] = h * lax.rsqrt(ms + EPS) * gf_ref[...]


def _moe(h2, hn2, comb, wg, wu, wd, gf, tm):
    n = h2.shape[0]
    return pl.pallas_call(
        _moe_kernel,
        grid=(n // tm, N_EXPERTS),
        in_specs=[
            pl.BlockSpec((tm, D_MODEL), lambda i, e: (i, 0)),
            pl.BlockSpec((tm, D_MODEL), lambda i, e: (i, 0)),
            pl.BlockSpec((tm, N_EXPERTS), lambda i, e: (i, 0)),
            pl.BlockSpec((1, D_MODEL, D_EXPERT), lambda i, e: (e, 0, 0)),
            pl.BlockSpec((1, D_MODEL, D_EXPERT), lambda i, e: (e, 0, 0)),
            pl.BlockSpec((1, D_EXPERT, D_MODEL), lambda i, e: (e, 0, 0)),
            pl.BlockSpec((1, D_MODEL), lambda i, e: (0, 0)),
        ],
        out_specs=pl.BlockSpec((tm, D_MODEL), lambda i, e: (i, 0)),
        out_shape=jax.ShapeDtypeStruct((n, D_MODEL), F32),
        scratch_shapes=[pltpu.VMEM((tm, D_MODEL), F32)],
        compiler_params=_params("arbitrary", "arbitrary"),
        name="moe",
    )(h2, hn2, comb, wg, wu, wd, gf)


def _rope_tables(t):
    inv = 1.0 / (ROPE_THETA ** (jnp.arange(0, HEAD_DIM, 2, dtype=F32) / HEAD_DIM))
    ang = jnp.arange(t, dtype=F32)[:, None] * inv[None, :]
    ang = jnp.concatenate([ang, ang, ang, ang], axis=-1)
    sign = jnp.where((jnp.arange(LANES) % HEAD_DIM) < HEAD_DIM // 2, -1.0, 1.0).astype(F32)
    return jnp.cos(ang), jnp.sin(ang) * sign[None, :]


def kernel(x, meta, norm1_g, w_in, b_gate, lambda_q1, lambda_k1, lambda_q2, lambda_k2, subln_g, pool_w,
           pool_scale, w_attn_br, w_pool_br, w_out, norm2_g, w_router_group, b_router_group,
           w_router_expert, b_router_expert, w_e_gate, w_e_up, w_e_down, final_g):
    batch, seq, d = x.shape
    n = batch * seq
    x2 = x.reshape(n, d)
    cos, sin = _rope_tables(seq + N_META)
    w_in_b = w_in[0].astype(BF16)

    q, k, v, u, gates = _inproj(x2, norm1_g, cos[N_META:], sin[N_META:], w_in_b, b_gate, tm=1024)
    _, km, vm, um, _ = _inproj(meta, norm1_g, cos[:N_META], sin[:N_META], w_in_b, b_gate, tm=N_META)
    pad = ((0, LANES - N_META), (0, 0))
    a = _attention(q, k, v, jnp.pad(km, pad), jnp.pad(vm, pad), lambda_q1, lambda_k1, lambda_q2,
                   lambda_k2, subln_g, batch, seq, bq=256)

    n_rt = 32
    wr_t = jnp.concatenate([w_router_group[0], w_router_expert[0]], axis=1).T
    wr_t = jnp.pad(wr_t, ((0, n_rt - wr_t.shape[0]), (0, 0)))
    br_t = jnp.concatenate([b_router_group[0], b_router_expert[0]])
    br_t = jnp.pad(br_t, (0, n_rt - br_t.shape[0])).reshape(n_rt, 1)
    h2, hn2, logt = _mix(x2, a, u, um, gates, w_attn_br[0].astype(BF16), pool_w[0].astype(BF16),
                         pool_scale, w_pool_br[0].astype(BF16), w_out[0].astype(BF16), norm2_g,
                         wr_t, br_t, seq, tm=512)
    comb_t = _route(logt, tl=2048)
    out = _moe(h2, hn2, comb_t.T, w_e_gate[0].astype(BF16), w_e_up[0].astype(BF16),
               w_e_down[0].astype(BF16), final_g.reshape(1, d), tm=1024)
    return out.reshape(batch, seq, d)
```

```python
import functools
import math

import jax
import jax.numpy as jnp
from jax import lax
from jax.experimental import pallas as pl
from jax.experimental.pallas import tpu as pltpu

D_MODEL = 1024
N_META = 16
N_HEADS = 8
HEAD_DIM = 64
V_DIM = 2 * HEAD_DIM
POOL_WINDOWS = (2, 4, 8, 16)
N_POOL_GROUPS = len(POOL_WINDOWS)
POOL_GROUP_DIM = 128
POOL_WIDTH = N_POOL_GROUPS * POOL_GROUP_DIM
ROPE_THETA = 10000.0
N_GROUPS = 4
EXPERTS_PER_GROUP = 4
N_EXPERTS = N_GROUPS * EXPERTS_PER_GROUP
D_EXPERT = 512
EPS = 1e-6
LAMBDA_INIT = 0.8 - 0.6 * math.exp(-0.3 * 0)
LOG2_E = math.log2(math.e)

_PAIR_LO = (0, 0, 0, 1, 1, 2)
_PAIR_HI = (1, 2, 3, 2, 3, 3)
N_PAIRS = len(_PAIR_LO)
N_BUCKETS = N_GROUPS * N_PAIRS
N_BUCKET_ROWS = 32
GATHER_UNROLL = 8

LANES = 128
NEG_BIG = -1e30
VMEM_LIMIT = 48 * 1024 * 1024

F32 = jnp.float32
BF16 = jnp.bfloat16

_HEADS_PER_STEP = 2
_IN_STEPS = N_HEADS // _HEADS_PER_STEP
_QKV_BLK = _HEADS_PER_STEP * V_DIM
_GATE_BLK = 2 * D_MODEL // _IN_STEPS
_O_K = D_MODEL // _QKV_BLK
_O_V = 2 * D_MODEL // _QKV_BLK
_O_U = 3 * D_MODEL // POOL_GROUP_DIM
_O_G = (3 * D_MODEL + POOL_WIDTH) // _GATE_BLK


def _params(*sem):
    return pltpu.CompilerParams(dimension_semantics=sem, vmem_limit_bytes=VMEM_LIMIT)


TOKEN_ROWS = D_MODEL // LANES


def _store_token_major(ref, val, rows):
    for s in range(TOKEN_ROWS):
        ref[pl.ds(s, rows, stride=TOKEN_ROWS), :] = val[:, s * LANES:(s + 1) * LANES]


def _load_token_major(ref, rows):
    return jnp.concatenate([ref[pl.ds(s, rows, stride=TOKEN_ROWS), :] for s in range(TOKEN_ROWS)], axis=1)


def _rope(z, cos, sin_signed, first_half):
    outs = []
    for c in range(z.shape[1] // LANES):
        zc = z[:, c * LANES:(c + 1) * LANES]
        rot = jnp.where(first_half, pltpu.roll(zc, LANES - HEAD_DIM // 2, 1),
                        pltpu.roll(zc, HEAD_DIM // 2, 1))
        outs.append(zc * cos + rot * sin_signed)
    return jnp.concatenate(outs, axis=1)


def _inproj_kernel(x_ref, g_ref, cos_ref, sin_ref, wq_ref, wk_ref, wv_ref, wu_ref, wg_ref, bg_ref,
                   q_ref, k_ref, v_ref, u_ref, gate_ref, hn_ref):
    @pl.when(pl.program_id(1) == 0)
    def _():
        x = x_ref[...]
        ms = jnp.mean(x * x, axis=-1, keepdims=True)
        hn_ref[...] = (x * lax.rsqrt(ms + EPS) * g_ref[...]).astype(BF16)

    hn = hn_ref[...]
    cos = cos_ref[...]
    sin = sin_ref[...]
    lane = lax.broadcasted_iota(jnp.int32, cos.shape, 1)
    first_half = jnp.bitwise_and(lane, HEAD_DIM - 1) < (HEAD_DIM // 2)
    zq = jnp.dot(hn, wq_ref[...], preferred_element_type=F32)
    q_ref[...] = (_rope(zq, cos, sin, first_half) * (LOG2_E / math.sqrt(HEAD_DIM))).astype(BF16)
    zk = jnp.dot(hn, wk_ref[...], preferred_element_type=F32)
    k_ref[...] = _rope(zk, cos, sin, first_half).astype(BF16)
    v_ref[...] = jnp.dot(hn, wv_ref[...], preferred_element_type=F32).astype(BF16)
    u_ref[...] = jnp.dot(hn, wu_ref[...], preferred_element_type=F32)
    zg = jnp.dot(hn, wg_ref[...], preferred_element_type=F32) + bg_ref[...]
    gate_ref[...] = (1.0 / (1.0 + jnp.exp(-zg))).astype(BF16)


def _inproj(x2, g1, cos, sin, w_in, b_gate, tm):
    n = x2.shape[0]
    n_pos_blocks = cos.shape[0] // tm
    row = lambda i, j: (i, 0)
    pos = lambda i, j: (i % n_pos_blocks, 0)
    return pl.pallas_call(
        _inproj_kernel,
        grid=(n // tm, _IN_STEPS),
        in_specs=[
            pl.BlockSpec((tm, D_MODEL), row),
            pl.BlockSpec((1, D_MODEL), lambda i, j: (0, 0)),
            pl.BlockSpec((tm, LANES), pos),
            pl.BlockSpec((tm, LANES), pos),
            pl.BlockSpec((D_MODEL, _QKV_BLK), lambda i, j: (0, j)),
            pl.BlockSpec((D_MODEL, _QKV_BLK), lambda i, j: (0, _O_K + j)),
            pl.BlockSpec((D_MODEL, _QKV_BLK), lambda i, j: (0, _O_V + j)),
            pl.BlockSpec((D_MODEL, POOL_GROUP_DIM), lambda i, j: (0, _O_U + j)),
            pl.BlockSpec((D_MODEL, _GATE_BLK), lambda i, j: (0, _O_G + j)),
            pl.BlockSpec((1, _GATE_BLK), lambda i, j: (0, j)),
        ],
        out_specs=[
            pl.BlockSpec((tm, _QKV_BLK), lambda i, j: (i, j)),
            pl.BlockSpec((tm, _QKV_BLK), lambda i, j: (i, j)),
            pl.BlockSpec((tm, _QKV_BLK), lambda i, j: (i, j)),
            pl.BlockSpec((tm, POOL_GROUP_DIM), lambda i, j: (i, j)),
            pl.BlockSpec((tm, _GATE_BLK), lambda i, j: (i, j)),
        ],
        out_shape=[
            jax.ShapeDtypeStruct((n, D_MODEL), BF16),
            jax.ShapeDtypeStruct((n, D_MODEL), BF16),
            jax.ShapeDtypeStruct((n, D_MODEL), BF16),
            jax.ShapeDtypeStruct((n, POOL_WIDTH), F32),
            jax.ShapeDtypeStruct((n, 2 * D_MODEL), BF16),
        ],
        scratch_shapes=[pltpu.VMEM((tm, D_MODEL), BF16)],
        compiler_params=_params("arbitrary", "arbitrary"),
        name="inproj",
    )(x2, g1, cos, sin, w_in, w_in, w_in, w_in, w_in, b_gate)


def _attn_kernel(lq1_ref, lk1_ref, lq2_ref, lk2_ref, subg_ref, q_ref, k_ref, v_ref, km_ref, vm_ref,
                 o_ref, qq_sc, vext_sc, m_sc, acc_sc, *, bq):
    qi = pl.program_id(2)

    @pl.when(qi == 0)
    def _():
        vext_sc[:, 0:V_DIM] = v_ref[...]
        vext_sc[:, V_DIM:2 * V_DIM] = jnp.ones(v_ref.shape, BF16)

    q = q_ref[...]
    lane = lax.broadcasted_iota(jnp.int32, q.shape, 1)
    zero = jnp.zeros_like(q)
    qq_sc[0:bq, :] = jnp.where(lane < HEAD_DIM, q, zero)
    qq_sc[bq:2 * bq, :] = jnp.where(lane >= HEAD_DIM, q, zero)

    def step(kb, vb, mask, first):
        s = lax.dot_general(qq_sc[...], kb, (((1,), (1,)), ((), ())), preferred_element_type=F32)
        if mask is not None:
            s = jnp.where(mask, s, NEG_BIG)
        n_tiles = s.shape[1] // LANES
        rm = s[:, 0:LANES]
        for t in range(1, n_tiles):
            rm = jnp.maximum(rm, s[:, t * LANES:(t + 1) * LANES])
        rmax = jnp.max(rm, axis=-1, keepdims=True)
        if first:
            m_new = jnp.broadcast_to(rmax, (2 * bq, LANES))
        else:
            m_prev = m_sc[...]
            m_new = jnp.maximum(m_prev, rmax)
        p = jnp.exp2(s - jnp.concatenate([m_new] * n_tiles, axis=1))
        pv = jnp.dot(p.astype(BF16), vb, preferred_element_type=F32)
        if first:
            acc_sc[...] = pv
        else:
            alpha = jnp.exp2(m_prev - m_new)
            acc_sc[...] = jnp.concatenate([alpha, alpha], axis=1) * acc_sc[...] + pv
        m_sc[...] = m_new

    r = jnp.bitwise_and(lax.broadcasted_iota(jnp.int32, (2 * bq, bq + LANES), 0), bq - 1)
    c = lax.broadcasted_iota(jnp.int32, (2 * bq, bq + LANES), 1)
    visible = jnp.logical_or(c <= r, jnp.logical_and(c >= bq, c < bq + N_META))
    start = pl.multiple_of(qi * bq, bq)
    vm_ext = jnp.concatenate([vm_ref[...], jnp.ones((LANES, V_DIM), BF16)], axis=1)
    step(jnp.concatenate([k_ref[pl.ds(start, bq), :], km_ref[...]], axis=0),
         jnp.concatenate([vext_sc[pl.ds(start, bq), :], vm_ext], axis=0), visible, True)

    def full_block(j):
        start = pl.multiple_of(j * bq, bq)
        step(k_ref[pl.ds(start, bq), :], vext_sc[pl.ds(start, bq), :], None, False)

    def body(jj, carry):
        full_block(2 * jj)
        full_block(2 * jj + 1)
        return carry

    lax.fori_loop(0, qi // 2, body, 0)

    @pl.when(qi % 2 == 1)
    def _():
        full_block(qi - 1)

    lam = (jnp.exp(jnp.sum(lq1_ref[...] * lk1_ref[...], axis=-1, keepdims=True))
           - jnp.exp(jnp.sum(lq2_ref[...] * lk2_ref[...], axis=-1, keepdims=True)) + LAMBDA_INIT)
    o1 = acc_sc[0:bq, 0:V_DIM] / acc_sc[0:bq, V_DIM:2 * V_DIM]
    o2 = acc_sc[bq:2 * bq, 0:V_DIM] / acc_sc[bq:2 * bq, V_DIM:2 * V_DIM]
    o = o1 - lam * o2
    ms = jnp.mean(o * o, axis=-1, keepdims=True)
    o_ref[...] = (o * lax.rsqrt(ms + EPS) * subg_ref[...] * (1.0 - LAMBDA_INIT)).astype(BF16)


def _attention(q, k, v, km, vm, lq1, lk1, lq2, lk2, subg, batch, seq, bq):
    nq = seq // bq
    small = lambda b, h, i: (0, 0)
    return pl.pallas_call(
        functools.partial(_attn_kernel, bq=bq),
        grid=(batch, N_HEADS, nq),
        in_specs=[
            pl.BlockSpec((1, HEAD_DIM), small),
            pl.BlockSpec((1, HEAD_DIM), small),
            pl.BlockSpec((1, HEAD_DIM), small),
            pl.BlockSpec((1, HEAD_DIM), small),
            pl.BlockSpec((1, V_DIM), small),
            pl.BlockSpec((bq, V_DIM), lambda b, h, i: (b * nq + i, h)),
            pl.BlockSpec((seq, V_DIM), lambda b, h, i: (b, h)),
            pl.BlockSpec((seq, V_DIM), lambda b, h, i: (b, h)),
            pl.BlockSpec((LANES, V_DIM), lambda b, h, i: (0, h)),
            pl.BlockSpec((LANES, V_DIM), lambda b, h, i: (0, h)),
        ],
        out_specs=pl.BlockSpec((bq, V_DIM), lambda b, h, i: (b * nq + i, h)),
        out_shape=jax.ShapeDtypeStruct(q.shape, BF16),
        scratch_shapes=[
            pltpu.VMEM((2 * bq, V_DIM), BF16),
            pltpu.VMEM((seq, 2 * V_DIM), BF16),
            pltpu.VMEM((2 * bq, LANES), F32),
            pltpu.VMEM((2 * bq, 2 * V_DIM), F32),
        ],
        compiler_params=_params("arbitrary", "arbitrary", "arbitrary"),
        name="diff_attn",
    )(lq1, lk1, lq2, lk2, subg, q, k, v, km, vm)


def _mix_kernel(x_ref, a_ref, u_ref, uprev_ref, umeta_ref, gate_ref, wa_ref, pw_ref, ps_ref, wp_ref,
                wo_ref, g2_ref, wr_ref, br_ref, h2c_ref, logt_ref, ext_sc, *, tm, tiles_per_seq):
    i = pl.program_id(0)
    first = (i % tiles_per_seq) == 0

    @pl.when(first)
    def _():
        ext_sc[0:N_META, :] = umeta_ref[...]

    @pl.when(jnp.logical_not(first))
    def _():
        ext_sc[0:N_META, :] = uprev_ref[...]

    ext_sc[N_META:N_META + tm, :] = u_ref[...]

    pooled = []
    for g, w in enumerate(POOL_WINDOWS):
        cs = slice(g * POOL_GROUP_DIM, (g + 1) * POOL_GROUP_DIM)
        cur = ext_sc[N_META:N_META + tm, cs]
        tot = cur
        for kk in range(1, w):
            tot = tot + ext_sc[N_META - kk:N_META - kk + tm, cs]
        y = (tot * (1.0 / w) - cur).astype(BF16)
        pooled.append(jnp.dot(y, pw_ref[g], preferred_element_type=F32))
    yp = (jnp.concatenate(pooled, axis=1) * ps_ref[...]).astype(BF16)
    y_pool = jnp.dot(yp, wp_ref[...], preferred_element_type=F32)
    y_attn = jnp.dot(a_ref[...], wa_ref[...], preferred_element_type=F32)
    mixed = (gate_ref[:, 0:D_MODEL].astype(F32) * y_attn
             + gate_ref[:, D_MODEL:2 * D_MODEL].astype(F32) * y_pool)
    h2 = x_ref[...] + jnp.dot(mixed.astype(BF16), wo_ref[...], preferred_element_type=F32)
    _store_token_major(h2c_ref, h2, tm)
    ms = jnp.mean(h2 * h2, axis=-1, keepdims=True)
    hn2 = h2 * lax.rsqrt(ms + EPS) * g2_ref[...]
    hn2_hi = hn2.astype(BF16)
    hn2_lo = (hn2 - hn2_hi.astype(F32)).astype(BF16)
    wr = wr_ref[...]
    wr_hi = wr.astype(BF16)
    wr_lo = (wr - wr_hi.astype(F32)).astype(BF16)
    nt = (((1,), (1,)), ((), ()))
    logt = (lax.dot_general(wr_hi, hn2_hi, nt, preferred_element_type=F32)
            + lax.dot_general(wr_hi, hn2_lo, nt, preferred_element_type=F32)
            + lax.dot_general(wr_lo, hn2_hi, nt, preferred_element_type=F32))
    logt_ref[...] = logt + br_ref[...]


def _mix(x2, a, u, umeta, gates, w_attn, pool_w, pool_scale, w_pool, w_out, g2, wr_t, br_t, seq, tm):
    n = x2.shape[0]
    tiles_per_seq = seq // tm
    halo_blocks = tm // N_META
    const2 = lambda i: (0, 0)
    n_rt = wr_t.shape[0]
    return pl.pallas_call(
        functools.partial(_mix_kernel, tm=tm, tiles_per_seq=tiles_per_seq),
        grid=(n // tm,),
        in_specs=[
            pl.BlockSpec((tm, D_MODEL), lambda i: (i, 0)),
            pl.BlockSpec((tm, D_MODEL), lambda i: (i, 0)),
            pl.BlockSpec((tm, POOL_WIDTH), lambda i: (i, 0)),
            pl.BlockSpec((N_META, POOL_WIDTH), lambda i: (jnp.maximum(i * halo_blocks - 1, 0), 0)),
            pl.BlockSpec((N_META, POOL_WIDTH), const2),
            pl.BlockSpec((tm, 2 * D_MODEL), lambda i: (i, 0)),
            pl.BlockSpec((D_MODEL, D_MODEL), const2),
            pl.BlockSpec((N_POOL_GROUPS, POOL_GROUP_DIM, POOL_GROUP_DIM), lambda i: (0, 0, 0)),
            pl.BlockSpec((1, POOL_WIDTH), const2),
            pl.BlockSpec((POOL_WIDTH, D_MODEL), const2),
            pl.BlockSpec((D_MODEL, D_MODEL), const2),
            pl.BlockSpec((1, D_MODEL), const2),
            pl.BlockSpec((n_rt, D_MODEL), const2),
            pl.BlockSpec((n_rt, 1), const2),
        ],
        out_specs=[
            pl.BlockSpec((tm * TOKEN_ROWS, LANES), lambda i: (i, 0)),
            pl.BlockSpec((n_rt, tm), lambda i: (0, i)),
        ],
        out_shape=[
            jax.ShapeDtypeStruct((n * TOKEN_ROWS, LANES), F32),
            jax.ShapeDtypeStruct((n_rt, n), F32),
        ],
        scratch_shapes=[pltpu.VMEM((tm + N_META, POOL_WIDTH), F32)],
        compiler_params=_params("arbitrary"),
        name="mix",
    )(x2, a, u, u, umeta, gates, w_attn, pool_w, pool_scale, w_pool, w_out, g2, wr_t, br_t)


def _route_kernel(logt_ref, tri_ref, ids_ref, wts_ref, cnt_ref, carry_sc):
    @pl.when(pl.program_id(0) == 0)
    def _():
        carry_sc[...] = jnp.zeros_like(carry_sc)

    lg = logt_ref[...]
    g = [lg[r:r + 1, :] for r in range(N_GROUPS)]
    gmax = functools.reduce(jnp.maximum, g)
    gsel = jnp.full_like(g[0], N_GROUPS - 1).astype(jnp.int32)
    for r in range(N_GROUPS - 2, -1, -1):
        gsel = jnp.where(g[r] == gmax, r, gsel)
    p_group = 1.0 / functools.reduce(lambda a, b: a + b, [jnp.exp(x - gmax) for x in g])
    e = []
    for jj in range(EXPERTS_PER_GROUP):
        v = lg[N_GROUPS + jj:N_GROUPS + jj + 1, :]
        for r in range(1, N_GROUPS):
            row = N_GROUPS + r * EXPERTS_PER_GROUP + jj
            v = jnp.where(gsel == r, lg[row:row + 1, :], v)
        e.append(v)
    v1 = functools.reduce(jnp.maximum, e)
    i1 = jnp.full_like(gsel, EXPERTS_PER_GROUP - 1)
    for jj in range(EXPERTS_PER_GROUP - 2, -1, -1):
        i1 = jnp.where(e[jj] == v1, jj, i1)
    rest = [jnp.where(i1 == jj, -jnp.inf, e[jj]) for jj in range(EXPERTS_PER_GROUP)]
    v2 = functools.reduce(jnp.maximum, rest)
    i2 = jnp.full_like(gsel, EXPERTS_PER_GROUP - 1)
    for jj in range(EXPERTS_PER_GROUP - 2, -1, -1):
        i2 = jnp.where(jnp.logical_and(rest[jj] == v2, i1 != jj), jj, i2)
    t = jnp.exp(v2 - v1)
    w1 = p_group / (1.0 + t)
    w2 = p_group * t / (1.0 + t)
    lo = jnp.minimum(i1, i2)
    hi = jnp.maximum(i1, i2)
    pair = jnp.where(lo == 0, 0, jnp.where(lo == 1, 3, 5)) + hi - lo - 1
    bucket = gsel * N_PAIRS + pair
    first_is_lo = i1 < i2
    w_lo = jnp.where(first_is_lo, w1, w2)
    w_hi = jnp.where(first_is_lo, w2, w1)

    tl = lg.shape[1]
    bid = lax.broadcasted_iota(jnp.int32, (N_BUCKET_ROWS, tl), 0)
    hot = bid == bucket
    hot_b = jnp.where(hot, 1.0, 0.0).astype(BF16)
    prefix = jnp.dot(hot_b, tri_ref[...], preferred_element_type=F32)
    carry = carry_sc[...]
    before = prefix + jnp.concatenate([carry] * (tl // LANES), axis=1)
    rank = jnp.sum(jnp.where(hot, before, 0.0), axis=0, keepdims=True)
    total = carry + jnp.dot(hot_b, jnp.ones((tl, LANES), BF16), preferred_element_type=F32)
    carry_sc[...] = total
    cnt_ref[...] = total.astype(jnp.int32)
    pad_i = jnp.zeros((6, tl), jnp.int32)
    ids_ref[...] = jnp.concatenate([bucket, rank.astype(jnp.int32), pad_i], axis=0)
    wts_ref[...] = jnp.concatenate([w_lo, w_hi, pad_i.astype(F32)], axis=0)


def _route(logt, tl):
    n_rt, n = logt.shape
    tri = jnp.triu(jnp.ones((tl, tl), BF16), k=1)
    return pl.pallas_call(
        _route_kernel,
        grid=(n // tl,),
        in_specs=[pl.BlockSpec((n_rt, tl), lambda i: (0, i)),
                  pl.BlockSpec((tl, tl), lambda i: (0, 0))],
        out_specs=[pl.BlockSpec((8, tl), lambda i: (0, i)),
                   pl.BlockSpec((8, tl), lambda i: (0, i)),
                   pl.BlockSpec((N_BUCKET_ROWS, LANES), lambda i: (0, 0))],
        out_shape=[jax.ShapeDtypeStruct((8, n), jnp.int32),
                   jax.ShapeDtypeStruct((8, n), F32),
                   jax.ShapeDtypeStruct((N_BUCKET_ROWS, LANES), jnp.int32)],
        scratch_shapes=[pltpu.VMEM((N_BUCKET_ROWS, LANES), F32)],
        compiler_params=_params("arbitrary"),
        name="route",
    )(logt, tri)


def _token_copy(src_hbm, src_token, dst_buf, dst_row, sem):
    return pltpu.make_async_copy(
        src_hbm.at[pl.ds(pl.multiple_of(src_token * TOKEN_ROWS, TOKEN_ROWS), TOKEN_ROWS), :],
        dst_buf.at[pl.ds(pl.multiple_of(dst_row * TOKEN_ROWS, TOKEN_ROWS), TOKEN_ROWS), :],
        sem)


def _start_gather(idx_ref, first, rows, src_hbm, dst_buf, sem):
    def body(rr, carry):
        for uu in range(GATHER_UNROLL):
            r = rr * GATHER_UNROLL + uu
            _token_copy(src_hbm, idx_ref[first + r], dst_buf, r, sem).start()
        return carry

    lax.fori_loop(0, rows // GATHER_UNROLL, body, 0)


def _wait_gather(rows, src_hbm, dst_buf, sem):
    pltpu.make_async_copy(src_hbm.at[pl.ds(0, rows * TOKEN_ROWS), :], dst_buf, sem).wait()


def _moe_kernel(elo_ref, ehi_ref, nused_ref, src_ref, h2c_hbm, wts_ref, g2_ref, wg_lo_ref, wu_lo_ref,
                wg_hi_ref, wu_hi_ref, wd_lo_ref, wd_hi_ref, yc_ref, xbuf, sem, *, tme):
    i = pl.program_id(0)
    n_used = nused_ref[0]
    slot = jnp.bitwise_and(i, 1)

    @pl.when(i == 0)
    def _():
        _start_gather(src_ref, 0, tme, h2c_hbm, xbuf.at[0], sem.at[0])

    @pl.when(i + 1 < n_used)
    def _():
        _start_gather(src_ref, (i + 1) * tme, tme, h2c_hbm, xbuf.at[1 - slot], sem.at[1 - slot])

    @pl.when(i < n_used)
    def _():
        _wait_gather(tme, h2c_hbm, xbuf.at[slot], sem.at[slot])
        x = _load_token_major(xbuf.at[slot], tme)
        ms = jnp.mean(x * x, axis=-1, keepdims=True)
        hn = (x * lax.rsqrt(ms + EPS) * g2_ref[...]).astype(BF16)
        wts = wts_ref[...]

        def expert_act(wg_ref, wu_ref, w):
            gate = jnp.dot(hn, wg_ref[0], preferred_element_type=F32)
            up = jnp.dot(hn, wu_ref[0], preferred_element_type=F32)
            return (gate / (1.0 + jnp.exp(-gate)) * up * w).astype(BF16)

        y = (jnp.dot(expert_act(wg_lo_ref, wu_lo_ref, wts[:, 0:1]), wd_lo_ref[0], preferred_element_type=F32)
             + jnp.dot(expert_act(wg_hi_ref, wu_hi_ref, wts[:, 1:2]), wd_hi_ref[0], preferred_element_type=F32))
        _store_token_major(yc_ref, y, tme)

    @pl.when(i >= n_used)
    def _():
        yc_ref[...] = jnp.zeros_like(yc_ref)


def _moe(elo, ehi, nused, src, h2c, wts_sorted, g2, wg, wu, wd, tme, n_tiles):
    used = lambda i, elo, ehi, nu, src: (jnp.minimum(i, nu[0] - 1), 0)
    w_lo = lambda i, elo, ehi, nu, src: (elo[i], 0, 0)
    w_hi = lambda i, elo, ehi, nu, src: (ehi[i], 0, 0)
    grid_spec = pltpu.PrefetchScalarGridSpec(
        num_scalar_prefetch=4,
        grid=(n_tiles,),
        in_specs=[
            pl.BlockSpec(memory_space=pl.ANY),
            pl.BlockSpec((tme, 2), used),
            pl.BlockSpec((1, D_MODEL), lambda i, elo, ehi, nu, src: (0, 0)),
            pl.BlockSpec((1, D_MODEL, D_EXPERT), w_lo),
            pl.BlockSpec((1, D_MODEL, D_EXPERT), w_lo),
            pl.BlockSpec((1, D_MODEL, D_EXPERT), w_hi),
            pl.BlockSpec((1, D_MODEL, D_EXPERT), w_hi),
            pl.BlockSpec((1, D_EXPERT, D_MODEL), w_lo),
            pl.BlockSpec((1, D_EXPERT, D_MODEL), w_hi),
        ],
        out_specs=pl.BlockSpec((tme * TOKEN_ROWS, LANES), lambda i, elo, ehi, nu, src: (i, 0)),
        scratch_shapes=[pltpu.VMEM((2, tme * TOKEN_ROWS, LANES), F32), pltpu.SemaphoreType.DMA((2,))],
    )
    return pl.pallas_call(
        functools.partial(_moe_kernel, tme=tme),
        grid_spec=grid_spec,
        out_shape=jax.ShapeDtypeStruct((n_tiles * tme * TOKEN_ROWS, LANES), F32),
        compiler_params=_params("arbitrary"),
        name="moe",
    )(elo, ehi, nused, src, h2c, wts_sorted, g2, wg, wu, wg, wu, wd, wd)


def _final_kernel(dest_ref, h2c_ref, yc_hbm, gf_ref, o_ref, ybuf, sem, *, tmf):
    i = pl.program_id(0)
    slot = jnp.bitwise_and(i, 1)

    @pl.when(i == 0)
    def _():
        _start_gather(dest_ref, 0, tmf, yc_hbm, ybuf.at[0], sem.at[0])

    @pl.when(i + 1 < pl.num_programs(0))
    def _():
        _start_gather(dest_ref, (i + 1) * tmf, tmf, yc_hbm, ybuf.at[1 - slot], sem.at[1 - slot])

    _wait_gather(tmf, yc_hbm, ybuf.at[slot], sem.at[slot])
    h = _load_token_major(h2c_ref, tmf) + _load_token_major(ybuf.at[slot], tmf)
    ms = jnp.mean(h * h, axis=-1, keepdims=True)
    o_ref[...] = h * lax.rsqrt(ms + EPS) * gf_ref[...]


def _final(dest, h2c, yc, gf, tmf):
    n = h2c.shape[0] // TOKEN_ROWS
    grid_spec = pltpu.PrefetchScalarGridSpec(
        num_scalar_prefetch=1,
        grid=(n // tmf,),
        in_specs=[
            pl.BlockSpec((tmf * TOKEN_ROWS, LANES), lambda i, dest: (i, 0)),
            pl.BlockSpec(memory_space=pl.ANY),
            pl.BlockSpec((1, D_MODEL), lambda i, dest: (0, 0)),
        ],
        out_specs=pl.BlockSpec((tmf, D_MODEL), lambda i, dest: (i, 0)),
        scratch_shapes=[pltpu.VMEM((2, tmf * TOKEN_ROWS, LANES), F32), pltpu.SemaphoreType.DMA((2,))],
    )
    return pl.pallas_call(
        functools.partial(_final_kernel, tmf=tmf),
        grid_spec=grid_spec,
        out_shape=jax.ShapeDtypeStruct((n, D_MODEL), F32),
        compiler_params=_params("arbitrary"),
        name="final",
    )(dest, h2c, yc, gf)


def _rope_tables(t):
    inv = 1.0 / (ROPE_THETA ** (jnp.arange(0, HEAD_DIM, 2, dtype=F32) / HEAD_DIM))
    ang = jnp.arange(t, dtype=F32)[:, None] * inv[None, :]
    ang = jnp.concatenate([ang, ang, ang, ang], axis=-1)
    sign = jnp.where((jnp.arange(LANES) % HEAD_DIM) < HEAD_DIM // 2, -1.0, 1.0).astype(F32)
    return jnp.cos(ang), jnp.sin(ang) * sign[None, :]


def kernel(x, meta, norm1_g, w_in, b_gate, lambda_q1, lambda_k1, lambda_q2, lambda_k2, subln_g, pool_w,
           pool_scale, w_attn_br, w_pool_br, w_out, norm2_g, w_router_group, b_router_group,
           w_router_expert, b_router_expert, w_e_gate, w_e_up, w_e_down, final_g):
    batch, seq, d = x.shape
    n = batch * seq
    x2 = x.reshape(n, d)
    cos, sin = _rope_tables(seq + N_META)
    w_in_b = w_in[0].astype(BF16)

    q, k, v, u, gates = _inproj(x2, norm1_g, cos[N_META:], sin[N_META:], w_in_b, b_gate, tm=1024)
    _, km, vm, um, _ = _inproj(meta, norm1_g, cos[:N_META], sin[:N_META], w_in_b, b_gate, tm=N_META)
    pad = ((0, LANES - N_META), (0, 0))
    a = _attention(q, k, v, jnp.pad(km, pad), jnp.pad(vm, pad), lambda_q1, lambda_k1, lambda_q2,
                   lambda_k2, subln_g, batch, seq, bq=256)

    n_rt = 32
    wr_t = jnp.concatenate([w_router_group[0], w_router_expert[0]], axis=1).T
    wr_t = jnp.pad(wr_t, ((0, n_rt - wr_t.shape[0]), (0, 0)))
    br_t = jnp.concatenate([b_router_group[0], b_router_expert[0]])
    br_t = jnp.pad(br_t, (0, n_rt - br_t.shape[0])).reshape(n_rt, 1)
    h2c, logt = _mix(x2, a, u, um, gates, w_attn_br[0].astype(BF16), pool_w[0].astype(BF16),
                     pool_scale, w_pool_br[0].astype(BF16), w_out[0].astype(BF16), norm2_g,
                     wr_t, br_t, seq, tm=512)
    ids, wts, cnt = _route(logt, tl=1024)

    tme = 256
    n_tiles = n // tme + N_BUCKETS
    bucket, rank = ids[0], ids[1]
    tiles_per_bucket = (cnt[:N_BUCKETS, 0] + tme - 1) // tme
    tile_end = jnp.cumsum(tiles_per_bucket)
    n_used = tile_end[-1]
    dest = (tile_end - tiles_per_bucket)[bucket] * tme + rank
    src = jnp.zeros((n_tiles * tme,), jnp.int32).at[dest].set(jnp.arange(n, dtype=jnp.int32))
    wts_sorted = jnp.zeros((n_tiles * tme, 2), F32).at[dest].set(wts[0:2].T)
    tile_bucket = jnp.searchsorted(tile_end, jnp.minimum(jnp.arange(n_tiles), n_used - 1), side="right")
    tile_bucket = jnp.minimum(tile_bucket, N_BUCKETS - 1).astype(jnp.int32)
    group, pair = tile_bucket // N_PAIRS, tile_bucket % N_PAIRS
    e_lo = group * EXPERTS_PER_GROUP + jnp.asarray(_PAIR_LO, jnp.int32)[pair]
    e_hi = group * EXPERTS_PER_GROUP + jnp.asarray(_PAIR_HI, jnp.int32)[pair]

    yc = _moe(e_lo, e_hi, n_used.reshape(1).astype(jnp.int32), src, h2c, wts_sorted, norm2_g,
              w_e_gate[0].astype(BF16), w_e_up[0].astype(BF16), w_e_down[0].astype(BF16), tme, n_tiles)
    out = _final(dest.astype(jnp.int32), h2c, yc, final_g.reshape(1, d), tmf=512)
    return out.reshape(batch, seq, d)
```

```python
import functools
import math

import jax
import jax.numpy as jnp
from jax import lax
from jax.experimental import pallas as pl
from jax.experimental.pallas import tpu as pltpu

D_MODEL = 1024
N_META = 16
N_HEADS = 8
HEAD_DIM = 64
V_DIM = 2 * HEAD_DIM
POOL_WINDOWS = (2, 4, 8, 16)
N_POOL_GROUPS = len(POOL_WINDOWS)
POOL_GROUP_DIM = 128
POOL_WIDTH = N_POOL_GROUPS * POOL_GROUP_DIM
ROPE_THETA = 10000.0
N_GROUPS = 4
EXPERTS_PER_GROUP = 4
N_EXPERTS = N_GROUPS * EXPERTS_PER_GROUP
D_EXPERT = 512
EPS = 1e-6
LAMBDA_INIT = 0.8 - 0.6 * math.exp(-0.3 * 0)
LOG2_E = math.log2(math.e)

_PAIR_LO = (0, 0, 0, 1, 1, 2)
_PAIR_HI = (1, 2, 3, 2, 3, 3)
N_PAIRS = len(_PAIR_LO)
N_BUCKETS = N_GROUPS * N_PAIRS
N_BUCKET_ROWS = 32
GATHER_UNROLL = 8

LANES = 128
NEG_BIG = -1e30
VMEM_LIMIT = 48 * 1024 * 1024

F32 = jnp.float32
BF16 = jnp.bfloat16

_HEADS_PER_STEP = 2
_IN_STEPS = N_HEADS // _HEADS_PER_STEP
_QKV_BLK = _HEADS_PER_STEP * V_DIM
_GATE_BLK = 2 * D_MODEL // _IN_STEPS
_O_K = D_MODEL // _QKV_BLK
_O_V = 2 * D_MODEL // _QKV_BLK
_O_U = 3 * D_MODEL // POOL_GROUP_DIM
_O_G = (3 * D_MODEL + POOL_WIDTH) // _GATE_BLK


def _params(*sem):
    return pltpu.CompilerParams(dimension_semantics=sem, vmem_limit_bytes=VMEM_LIMIT)


TOKEN_ROWS = D_MODEL // LANES


def _store_token_major(ref, val, rows):
    for s in range(TOKEN_ROWS):
        ref[pl.ds(s, rows, stride=TOKEN_ROWS), :] = val[:, s * LANES:(s + 1) * LANES]


def _load_token_major(ref, rows):
    return jnp.concatenate([ref[pl.ds(s, rows, stride=TOKEN_ROWS), :] for s in range(TOKEN_ROWS)], axis=1)


def _rope(z, cos, sin_signed, first_half):
    outs = []
    for c in range(z.shape[1] // LANES):
        zc = z[:, c * LANES:(c + 1) * LANES]
        rot = jnp.where(first_half, pltpu.roll(zc, LANES - HEAD_DIM // 2, 1),
                        pltpu.roll(zc, HEAD_DIM // 2, 1))
        outs.append(zc * cos + rot * sin_signed)
    return jnp.concatenate(outs, axis=1)


def _inproj_kernel(x_ref, g_ref, cos_ref, sin_ref, wq_ref, wk_ref, wv_ref, wu_ref, wg_ref, bg_ref,
                   q_ref, k_ref, v_ref, u_ref, gate_ref, hn_ref):
    @pl.when(pl.program_id(1) == 0)
    def _():
        x = x_ref[...]
        ms = jnp.mean(x * x, axis=-1, keepdims=True)
        hn_ref[...] = (x * lax.rsqrt(ms + EPS) * g_ref[...]).astype(BF16)

    hn = hn_ref[...]
    cos = cos_ref[...]
    sin = sin_ref[...]
    lane = lax.broadcasted_iota(jnp.int32, cos.shape, 1)
    first_half = jnp.bitwise_and(lane, HEAD_DIM - 1) < (HEAD_DIM // 2)
    zq = jnp.dot(hn, wq_ref[...], preferred_element_type=F32)
    q_ref[...] = (_rope(zq, cos, sin, first_half) * (LOG2_E / math.sqrt(HEAD_DIM))).astype(BF16)
    zk = jnp.dot(hn, wk_ref[...], preferred_element_type=F32)
    k_ref[...] = _rope(zk, cos, sin, first_half).astype(BF16)
    v_ref[...] = jnp.dot(hn, wv_ref[...], preferred_element_type=F32).astype(BF16)
    u_ref[...] = jnp.dot(hn, wu_ref[...], preferred_element_type=F32)
    zg = jnp.dot(hn, wg_ref[...], preferred_element_type=F32) + bg_ref[...]
    gate_ref[...] = (1.0 / (1.0 + jnp.exp(-zg))).astype(BF16)


def _inproj(x2, g1, cos, sin, w_in, b_gate, tm):
    n = x2.shape[0]
    n_pos_blocks = cos.shape[0] // tm
    row = lambda i, j: (i, 0)
    pos = lambda i, j: (i % n_pos_blocks, 0)
    return pl.pallas_call(
        _inproj_kernel,
        grid=(n // tm, _IN_STEPS),
        in_specs=[
            pl.BlockSpec((tm, D_MODEL), row),
            pl.BlockSpec((1, D_MODEL), lambda i, j: (0, 0)),
            pl.BlockSpec((tm, LANES), pos),
            pl.BlockSpec((tm, LANES), pos),
            pl.BlockSpec((D_MODEL, _QKV_BLK), lambda i, j: (0, j)),
            pl.BlockSpec((D_MODEL, _QKV_BLK), lambda i, j: (0, _O_K + j)),
            pl.BlockSpec((D_MODEL, _QKV_BLK), lambda i, j: (0, _O_V + j)),
            pl.BlockSpec((D_MODEL, POOL_GROUP_DIM), lambda i, j: (0, _O_U + j)),
            pl.BlockSpec((D_MODEL, _GATE_BLK), lambda i, j: (0, _O_G + j)),
            pl.BlockSpec((1, _GATE_BLK), lambda i, j: (0, j)),
        ],
        out_specs=[
            pl.BlockSpec((tm, _QKV_BLK), lambda i, j: (i, j)),
            pl.BlockSpec((tm, _QKV_BLK), lambda i, j: (i, j)),
            pl.BlockSpec((tm, _QKV_BLK), lambda i, j: (i, j)),
            pl.BlockSpec((tm, POOL_GROUP_DIM), lambda i, j: (i, j)),
            pl.BlockSpec((tm, _GATE_BLK), lambda i, j: (i, j)),
        ],
        out_shape=[
            jax.ShapeDtypeStruct((n, D_MODEL), BF16),
            jax.ShapeDtypeStruct((n, D_MODEL), BF16),
            jax.ShapeDtypeStruct((n, D_MODEL), BF16),
            jax.ShapeDtypeStruct((n, POOL_WIDTH), F32),
            jax.ShapeDtypeStruct((n, 2 * D_MODEL), BF16),
        ],
        scratch_shapes=[pltpu.VMEM((tm, D_MODEL), BF16)],
        compiler_params=_params("arbitrary", "arbitrary"),
        name="inproj",
    )(x2, g1, cos, sin, w_in, w_in, w_in, w_in, w_in, b_gate)


def _attn_kernel(lq1_ref, lk1_ref, lq2_ref, lk2_ref, subg_ref, q_ref, k_ref, v_ref, km_ref, vm_ref,
                 o_ref, qq_sc, vext_sc, m_sc, acc_sc, *, bq):
    qi = pl.program_id(2)

    @pl.when(qi == 0)
    def _():
        vext_sc[:, 0:V_DIM] = v_ref[...]
        vext_sc[:, V_DIM:2 * V_DIM] = jnp.ones(v_ref.shape, BF16)

    q = q_ref[...]
    lane = lax.broadcasted_iota(jnp.int32, q.shape, 1)
    zero = jnp.zeros_like(q)
    qq_sc[0:bq, :] = jnp.where(lane < HEAD_DIM, q, zero)
    qq_sc[bq:2 * bq, :] = jnp.where(lane >= HEAD_DIM, q, zero)

    def step(kb, vb, mask, first):
        s = lax.dot_general(qq_sc[...], kb, (((1,), (1,)), ((), ())), preferred_element_type=F32)
        if mask is not None:
            s = jnp.where(mask, s, NEG_BIG)
        n_tiles = s.shape[1] // LANES
        rm = s[:, 0:LANES]
        for t in range(1, n_tiles):
            rm = jnp.maximum(rm, s[:, t * LANES:(t + 1) * LANES])
        rmax = jnp.max(rm, axis=-1, keepdims=True)
        if first:
            m_new = jnp.broadcast_to(rmax, (2 * bq, LANES))
        else:
            m_prev = m_sc[...]
            m_new = jnp.maximum(m_prev, rmax)
        p = jnp.exp2(s - jnp.concatenate([m_new] * n_tiles, axis=1))
        pv = jnp.dot(p.astype(BF16), vb, preferred_element_type=F32)
        if first:
            acc_sc[...] = pv
        else:
            alpha = jnp.exp2(m_prev - m_new)
            acc_sc[...] = jnp.concatenate([alpha, alpha], axis=1) * acc_sc[...] + pv
        m_sc[...] = m_new

    r = jnp.bitwise_and(lax.broadcasted_iota(jnp.int32, (2 * bq, bq + LANES), 0), bq - 1)
    c = lax.broadcasted_iota(jnp.int32, (2 * bq, bq + LANES), 1)
    visible = jnp.logical_or(c <= r, jnp.logical_and(c >= bq, c < bq + N_META))
    vm_ext = jnp.concatenate([vm_ref[...], jnp.ones((LANES, V_DIM), BF16)], axis=1)
    lam = (jnp.exp(jnp.sum(lq1_ref[...] * lk1_ref[...], axis=-1, keepdims=True))
           - jnp.exp(jnp.sum(lq2_ref[...] * lk2_ref[...], axis=-1, keepdims=True)) + LAMBDA_INIT)

    def q_block(n_full):
        diag = slice(n_full * bq, (n_full + 1) * bq)
        step(jnp.concatenate([k_ref[diag, :], km_ref[...]], axis=0),
             jnp.concatenate([vext_sc[diag, :], vm_ext], axis=0), visible, True)
        for j in range(n_full):
            blk = slice(j * bq, (j + 1) * bq)
            step(k_ref[blk, :], vext_sc[blk, :], None, False)
        o1 = acc_sc[0:bq, 0:V_DIM] / acc_sc[0:bq, V_DIM:2 * V_DIM]
        o2 = acc_sc[bq:2 * bq, 0:V_DIM] / acc_sc[bq:2 * bq, V_DIM:2 * V_DIM]
        o = o1 - lam * o2
        ms = jnp.mean(o * o, axis=-1, keepdims=True)
        o_ref[...] = (o * lax.rsqrt(ms + EPS) * subg_ref[...] * (1.0 - LAMBDA_INIT)).astype(BF16)

    for n_full in range(k_ref.shape[0] // bq):
        pl.when(qi == n_full)(functools.partial(q_block, n_full))


def _attention(q, k, v, km, vm, lq1, lk1, lq2, lk2, subg, batch, seq, bq):
    nq = seq // bq
    small = lambda b, h, i: (0, 0)
    return pl.pallas_call(
        functools.partial(_attn_kernel, bq=bq),
        grid=(batch, N_HEADS, nq),
        in_specs=[
            pl.BlockSpec((1, HEAD_DIM), small),
            pl.BlockSpec((1, HEAD_DIM), small),
            pl.BlockSpec((1, HEAD_DIM), small),
            pl.BlockSpec((1, HEAD_DIM), small),
            pl.BlockSpec((1, V_DIM), small),
            pl.BlockSpec((bq, V_DIM), lambda b, h, i: (b * nq + i, h)),
            pl.BlockSpec((seq, V_DIM), lambda b, h, i: (b, h)),
            pl.BlockSpec((seq, V_DIM), lambda b, h, i: (b, h)),
            pl.BlockSpec((LANES, V_DIM), lambda b, h, i: (0, h)),
            pl.BlockSpec((LANES, V_DIM), lambda b, h, i: (0, h)),
        ],
        out_specs=pl.BlockSpec((bq, V_DIM), lambda b, h, i: (b * nq + i, h)),
        out_shape=jax.ShapeDtypeStruct(q.shape, BF16),
        scratch_shapes=[
            pltpu.VMEM((2 * bq, V_DIM), BF16),
            pltpu.VMEM((seq, 2 * V_DIM), BF16),
            pltpu.VMEM((2 * bq, LANES), F32),
            pltpu.VMEM((2 * bq, 2 * V_DIM), F32),
        ],
        compiler_params=_params("arbitrary", "arbitrary", "arbitrary"),
        name="diff_attn",
    )(lq1, lk1, lq2, lk2, subg, q, k, v, km, vm)


def _mix_kernel(x_ref, a_ref, u_ref, uprev_ref, umeta_ref, gate_ref, wa_ref, pw_ref, ps_ref, wp_ref,
                wo_ref, g2_ref, wr_ref, br_ref, h2c_ref, logt_ref, ext_sc, *, tm, tiles_per_seq):
    i = pl.program_id(0)
    first = (i % tiles_per_seq) == 0

    @pl.when(first)
    def _():
        ext_sc[0:N_META, :] = umeta_ref[...]

    @pl.when(jnp.logical_not(first))
    def _():
        ext_sc[0:N_META, :] = uprev_ref[...]

    ext_sc[N_META:N_META + tm, :] = u_ref[...]

    pooled = []
    for g, w in enumerate(POOL_WINDOWS):
        cs = slice(g * POOL_GROUP_DIM, (g + 1) * POOL_GROUP_DIM)
        cur = ext_sc[N_META:N_META + tm, cs]
        tot = cur
        for kk in range(1, w):
            tot = tot + ext_sc[N_META - kk:N_META - kk + tm, cs]
        y = (tot * (1.0 / w) - cur).astype(BF16)
        pooled.append(jnp.dot(y, pw_ref[g], preferred_element_type=F32))
    yp = (jnp.concatenate(pooled, axis=1) * ps_ref[...]).astype(BF16)
    y_pool = jnp.dot(yp, wp_ref[...], preferred_element_type=F32)
    y_attn = jnp.dot(a_ref[...], wa_ref[...], preferred_element_type=F32)
    mixed = (gate_ref[:, 0:D_MODEL].astype(F32) * y_attn
             + gate_ref[:, D_MODEL:2 * D_MODEL].astype(F32) * y_pool)
    h2 = x_ref[...] + jnp.dot(mixed.astype(BF16), wo_ref[...], preferred_element_type=F32)
    _store_token_major(h2c_ref, h2, tm)
    ms = jnp.mean(h2 * h2, axis=-1, keepdims=True)
    hn2 = h2 * lax.rsqrt(ms + EPS) * g2_ref[...]
    hn2_hi = hn2.astype(BF16)
    hn2_lo = (hn2 - hn2_hi.astype(F32)).astype(BF16)
    wr = wr_ref[...]
    wr_hi = wr.astype(BF16)
    wr_lo = (wr - wr_hi.astype(F32)).astype(BF16)
    nt = (((1,), (1,)), ((), ()))
    logt = (lax.dot_general(wr_hi, hn2_hi, nt, preferred_element_type=F32)
            + lax.dot_general(wr_hi, hn2_lo, nt, preferred_element_type=F32)
            + lax.dot_general(wr_lo, hn2_hi, nt, preferred_element_type=F32))
    logt_ref[...] = logt + br_ref[...]


def _mix(x2, a, u, umeta, gates, w_attn, pool_w, pool_scale, w_pool, w_out, g2, wr_t, br_t, seq, tm):
    n = x2.shape[0]
    tiles_per_seq = seq // tm
    halo_blocks = tm // N_META
    const2 = lambda i: (0, 0)
    n_rt = wr_t.shape[0]
    return pl.pallas_call(
        functools.partial(_mix_kernel, tm=tm, tiles_per_seq=tiles_per_seq),
        grid=(n // tm,),
        in_specs=[
            pl.BlockSpec((tm, D_MODEL), lambda i: (i, 0)),
            pl.BlockSpec((tm, D_MODEL), lambda i: (i, 0)),
            pl.BlockSpec((tm, POOL_WIDTH), lambda i: (i, 0)),
            pl.BlockSpec((N_META, POOL_WIDTH), lambda i: (jnp.maximum(i * halo_blocks - 1, 0), 0)),
            pl.BlockSpec((N_META, POOL_WIDTH), const2),
            pl.BlockSpec((tm, 2 * D_MODEL), lambda i: (i, 0)),
            pl.BlockSpec((D_MODEL, D_MODEL), const2),
            pl.BlockSpec((N_POOL_GROUPS, POOL_GROUP_DIM, POOL_GROUP_DIM), lambda i: (0, 0, 0)),
            pl.BlockSpec((1, POOL_WIDTH), const2),
            pl.BlockSpec((POOL_WIDTH, D_MODEL), const2),
            pl.BlockSpec((D_MODEL, D_MODEL), const2),
            pl.BlockSpec((1, D_MODEL), const2),
            pl.BlockSpec((n_rt, D_MODEL), const2),
            pl.BlockSpec((n_rt, 1), const2),
        ],
        out_specs=[
            pl.BlockSpec((tm * TOKEN_ROWS, LANES), lambda i: (i, 0)),
            pl.BlockSpec((n_rt, tm), lambda i: (0, i)),
        ],
        out_shape=[
            jax.ShapeDtypeStruct((n * TOKEN_ROWS, LANES), F32),
            jax.ShapeDtypeStruct((n_rt, n), F32),
        ],
        scratch_shapes=[pltpu.VMEM((tm + N_META, POOL_WIDTH), F32)],
        compiler_params=_params("arbitrary"),
        name="mix",
    )(x2, a, u, u, umeta, gates, w_attn, pool_w, pool_scale, w_pool, w_out, g2, wr_t, br_t)


def _route_kernel(logt_ref, tri_ref, ids_ref, wts_ref, cnt_ref, carry_sc):
    @pl.when(pl.program_id(0) == 0)
    def _():
        carry_sc[...] = jnp.zeros_like(carry_sc)

    lg = logt_ref[...]
    g = [lg[r:r + 1, :] for r in range(N_GROUPS)]
    gmax = functools.reduce(jnp.maximum, g)
    gsel = jnp.full_like(g[0], N_GROUPS - 1).astype(jnp.int32)
    for r in range(N_GROUPS - 2, -1, -1):
        gsel = jnp.where(g[r] == gmax, r, gsel)
    p_group = 1.0 / functools.reduce(lambda a, b: a + b, [jnp.exp(x - gmax) for x in g])
    e = []
    for jj in range(EXPERTS_PER_GROUP):
        v = lg[N_GROUPS + jj:N_GROUPS + jj + 1, :]
        for r in range(1, N_GROUPS):
            row = N_GROUPS + r * EXPERTS_PER_GROUP + jj
            v = jnp.where(gsel == r, lg[row:row + 1, :], v)
        e.append(v)
    v1 = functools.reduce(jnp.maximum, e)
    i1 = jnp.full_like(gsel, EXPERTS_PER_GROUP - 1)
    for jj in range(EXPERTS_PER_GROUP - 2, -1, -1):
        i1 = jnp.where(e[jj] == v1, jj, i1)
    rest = [jnp.where(i1 == jj, -jnp.inf, e[jj]) for jj in range(EXPERTS_PER_GROUP)]
    v2 = functools.reduce(jnp.maximum, rest)
    i2 = jnp.full_like(gsel, EXPERTS_PER_GROUP - 1)
    for jj in range(EXPERTS_PER_GROUP - 2, -1, -1):
        i2 = jnp.where(jnp.logical_and(rest[jj] == v2, i1 != jj), jj, i2)
    t = jnp.exp(v2 - v1)
    w1 = p_group / (1.0 + t)
    w2 = p_group * t / (1.0 + t)
    lo = jnp.minimum(i1, i2)
    hi = jnp.maximum(i1, i2)
    pair = jnp.where(lo == 0, 0, jnp.where(lo == 1, 3, 5)) + hi - lo - 1
    bucket = gsel * N_PAIRS + pair
    first_is_lo = i1 < i2
    w_lo = jnp.where(first_is_lo, w1, w2)
    w_hi = jnp.where(first_is_lo, w2, w1)

    tl = lg.shape[1]
    bid = lax.broadcasted_iota(jnp.int32, (N_BUCKET_ROWS, tl), 0)
    hot = bid == bucket
    hot_b = jnp.where(hot, 1.0, 0.0).astype(BF16)
    prefix = jnp.dot(hot_b, tri_ref[...], preferred_element_type=F32)
    carry = carry_sc[...]
    before = prefix + jnp.concatenate([carry] * (tl // LANES), axis=1)
    rank = jnp.sum(jnp.where(hot, before, 0.0), axis=0, keepdims=True)
    total = carry + jnp.dot(hot_b, jnp.ones((tl, LANES), BF16), preferred_element_type=F32)
    carry_sc[...] = total
    cnt_ref[...] = total.astype(jnp.int32)
    pad_i = jnp.zeros((6, tl), jnp.int32)
    ids_ref[...] = jnp.concatenate([bucket, rank.astype(jnp.int32), pad_i], axis=0)
    wts_ref[...] = jnp.concatenate([w_lo, w_hi, pad_i.astype(F32)], axis=0)


def _route(logt, tl):
    n_rt, n = logt.shape
    tri = jnp.triu(jnp.ones((tl, tl), BF16), k=1)
    return pl.pallas_call(
        _route_kernel,
        grid=(n // tl,),
        in_specs=[pl.BlockSpec((n_rt, tl), lambda i: (0, i)),
                  pl.BlockSpec((tl, tl), lambda i: (0, 0))],
        out_specs=[pl.BlockSpec((8, tl), lambda i: (0, i)),
                   pl.BlockSpec((8, tl), lambda i: (0, i)),
                   pl.BlockSpec((N_BUCKET_ROWS, LANES), lambda i: (0, 0))],
        out_shape=[jax.ShapeDtypeStruct((8, n), jnp.int32),
                   jax.ShapeDtypeStruct((8, n), F32),
                   jax.ShapeDtypeStruct((N_BUCKET_ROWS, LANES), jnp.int32)],
        scratch_shapes=[pltpu.VMEM((N_BUCKET_ROWS, LANES), F32)],
        compiler_params=_params("arbitrary"),
        name="route",
    )(logt, tri)


def _token_copy(src_hbm, src_token, dst_buf, dst_row, sem):
    return pltpu.make_async_copy(
        src_hbm.at[pl.ds(pl.multiple_of(src_token * TOKEN_ROWS, TOKEN_ROWS), TOKEN_ROWS), :],
        dst_buf.at[pl.ds(pl.multiple_of(dst_row * TOKEN_ROWS, TOKEN_ROWS), TOKEN_ROWS), :],
        sem)


def _start_gather(idx_ref, first, rows, src_hbm, dst_buf, sem):
    def body(rr, carry):
        for uu in range(GATHER_UNROLL):
            r = rr * GATHER_UNROLL + uu
            _token_copy(src_hbm, idx_ref[first + r], dst_buf, r, sem).start()
        return carry

    lax.fori_loop(0, rows // GATHER_UNROLL, body, 0)


def _wait_gather(rows, src_hbm, dst_buf, sem):
    pltpu.make_async_copy(src_hbm.at[pl.ds(0, rows * TOKEN_ROWS), :], dst_buf, sem).wait()


def _moe_kernel(elo_ref, ehi_ref, nused_ref, src_ref, h2c_hbm, wts_ref, g2_ref, wg_lo_ref, wu_lo_ref,
                wg_hi_ref, wu_hi_ref, wd_lo_ref, wd_hi_ref, yc_ref, xbuf, sem, *, tme):
    i = pl.program_id(0)
    n_used = nused_ref[0]
    slot = jnp.bitwise_and(i, 1)

    @pl.when(i == 0)
    def _():
        _start_gather(src_ref, 0, tme, h2c_hbm, xbuf.at[0], sem.at[0])

    @pl.when(i + 1 < n_used)
    def _():
        _start_gather(src_ref, (i + 1) * tme, tme, h2c_hbm, xbuf.at[1 - slot], sem.at[1 - slot])

    @pl.when(i < n_used)
    def _():
        _wait_gather(tme, h2c_hbm, xbuf.at[slot], sem.at[slot])
        x = _load_token_major(xbuf.at[slot], tme)
        ms = jnp.mean(x * x, axis=-1, keepdims=True)
        hn = (x * lax.rsqrt(ms + EPS) * g2_ref[...]).astype(BF16)
        wts = wts_ref[...]

        def expert_act(wg_ref, wu_ref, w):
            gate = jnp.dot(hn, wg_ref[0], preferred_element_type=F32)
            up = jnp.dot(hn, wu_ref[0], preferred_element_type=F32)
            return (gate / (1.0 + jnp.exp(-gate)) * up * w).astype(BF16)

        y = (jnp.dot(expert_act(wg_lo_ref, wu_lo_ref, wts[:, 0:1]), wd_lo_ref[0], preferred_element_type=F32)
             + jnp.dot(expert_act(wg_hi_ref, wu_hi_ref, wts[:, 1:2]), wd_hi_ref[0], preferred_element_type=F32))
        _store_token_major(yc_ref, y, tme)

    @pl.when(i >= n_used)
    def _():
        yc_ref[...] = jnp.zeros_like(yc_ref)


def _moe(elo, ehi, nused, src, h2c, wts_sorted, g2, wg, wu, wd, tme, n_tiles):
    used = lambda i, elo, ehi, nu, src: (jnp.minimum(i, nu[0] - 1), 0)
    w_lo = lambda i, elo, ehi, nu, src: (elo[i], 0, 0)
    w_hi = lambda i, elo, ehi, nu, src: (ehi[i], 0, 0)
    grid_spec = pltpu.PrefetchScalarGridSpec(
        num_scalar_prefetch=4,
        grid=(n_tiles,),
        in_specs=[
            pl.BlockSpec(memory_space=pl.ANY),
            pl.BlockSpec((tme, 2), used),
            pl.BlockSpec((1, D_MODEL), lambda i, elo, ehi, nu, src: (0, 0)),
            pl.BlockSpec((1, D_MODEL, D_EXPERT), w_lo),
            pl.BlockSpec((1, D_MODEL, D_EXPERT), w_lo),
            pl.BlockSpec((1, D_MODEL, D_EXPERT), w_hi),
            pl.BlockSpec((1, D_MODEL, D_EXPERT), w_hi),
            pl.BlockSpec((1, D_EXPERT, D_MODEL), w_lo),
            pl.BlockSpec((1, D_EXPERT, D_MODEL), w_hi),
        ],
        out_specs=pl.BlockSpec((tme * TOKEN_ROWS, LANES), lambda i, elo, ehi, nu, src: (i, 0)),
        scratch_shapes=[pltpu.VMEM((2, tme * TOKEN_ROWS, LANES), F32), pltpu.SemaphoreType.DMA((2,))],
    )
    return pl.pallas_call(
        functools.partial(_moe_kernel, tme=tme),
        grid_spec=grid_spec,
        out_shape=jax.ShapeDtypeStruct((n_tiles * tme * TOKEN_ROWS, LANES), F32),
        compiler_params=_params("arbitrary"),
        name="moe",
    )(elo, ehi, nused, src, h2c, wts_sorted, g2, wg, wu, wg, wu, wd, wd)


def _final_kernel(dest_ref, h2c_ref, yc_hbm, gf_ref, o_ref, ybuf, sem, *, tmf):
    i = pl.program_id(0)
    slot = jnp.bitwise_and(i, 1)

    @pl.when(i == 0)
    def _():
        _start_gather(dest_ref, 0, tmf, yc_hbm, ybuf.at[0], sem.at[0])

    @pl.when(i + 1 < pl.num_programs(0))
    def _():
        _start_gather(dest_ref, (i + 1) * tmf, tmf, yc_hbm, ybuf.at[1 - slot], sem.at[1 - slot])

    _wait_gather(tmf, yc_hbm, ybuf.at[slot], sem.at[slot])
    h = _load_token_major(h2c_ref, tmf) + _load_token_major(ybuf.at[slot], tmf)
    ms = jnp.mean(h * h, axis=-1, keepdims=True)
    o_ref[...] = h * lax.rsqrt(ms + EPS) * gf_ref[...]


def _final(dest, h2c, yc, gf, tmf):
    n = h2c.shape[0] // TOKEN_ROWS
    grid_spec = pltpu.PrefetchScalarGridSpec(
        num_scalar_prefetch=1,
        grid=(n // tmf,),
        in_specs=[
            pl.BlockSpec((tmf * TOKEN_ROWS, LANES), lambda i, dest: (i, 0)),
            pl.BlockSpec(memory_space=pl.ANY),
            pl.BlockSpec((1, D_MODEL), lambda i, dest: (0, 0)),
        ],
        out_specs=pl.BlockSpec((tmf, D_MODEL), lambda i, dest: (i, 0)),
        scratch_shapes=[pltpu.VMEM((2, tmf * TOKEN_ROWS, LANES), F32), pltpu.SemaphoreType.DMA((2,))],
    )
    return pl.pallas_call(
        functools.partial(_final_kernel, tmf=tmf),
        grid_spec=grid_spec,
        out_shape=jax.ShapeDtypeStruct((n, D_MODEL), F32),
        compiler_params=_params("arbitrary"),
        name="final",
    )(dest, h2c, yc, gf)


def _rope_tables(t):
    inv = 1.0 / (ROPE_THETA ** (jnp.arange(0, HEAD_DIM, 2, dtype=F32) / HEAD_DIM))
    ang = jnp.arange(t, dtype=F32)[:, None] * inv[None, :]
    ang = jnp.concatenate([ang, ang, ang, ang], axis=-1)
    sign = jnp.where((jnp.arange(LANES) % HEAD_DIM) < HEAD_DIM // 2, -1.0, 1.0).astype(F32)
    return jnp.cos(ang), jnp.sin(ang) * sign[None, :]


def kernel(x, meta, norm1_g, w_in, b_gate, lambda_q1, lambda_k1, lambda_q2, lambda_k2, subln_g, pool_w,
           pool_scale, w_attn_br, w_pool_br, w_out, norm2_g, w_router_group, b_router_group,
           w_router_expert, b_router_expert, w_e_gate, w_e_up, w_e_down, final_g):
    batch, seq, d = x.shape
    n = batch * seq
    x2 = x.reshape(n, d)
    cos, sin = _rope_tables(seq + N_META)
    w_in_b = w_in[0].astype(BF16)

    q, k, v, u, gates = _inproj(x2, norm1_g, cos[N_META:], sin[N_META:], w_in_b, b_gate, tm=1024)
    _, km, vm, um, _ = _inproj(meta, norm1_g, cos[:N_META], sin[:N_META], w_in_b, b_gate, tm=N_META)
    pad = ((0, LANES - N_META), (0, 0))
    a = _attention(q, k, v, jnp.pad(km, pad), jnp.pad(vm, pad), lambda_q1, lambda_k1, lambda_q2,
                   lambda_k2, subln_g, batch, seq, bq=256)

    n_rt = 32
    wr_t = jnp.concatenate([w_router_group[0], w_router_expert[0]], axis=1).T
    wr_t = jnp.pad(wr_t, ((0, n_rt - wr_t.shape[0]), (0, 0)))
    br_t = jnp.concatenate([b_router_group[0], b_router_expert[0]])
    br_t = jnp.pad(br_t, (0, n_rt - br_t.shape[0])).reshape(n_rt, 1)
    h2c, logt = _mix(x2, a, u, um, gates, w_attn_br[0].astype(BF16), pool_w[0].astype(BF16),
                     pool_scale, w_pool_br[0].astype(BF16), w_out[0].astype(BF16), norm2_g,
                     wr_t, br_t, seq, tm=512)
    ids, wts, cnt = _route(logt, tl=1024)

    tme = 256
    n_tiles = n // tme + N_BUCKETS
    bucket, rank = ids[0], ids[1]
    tiles_per_bucket = (cnt[:N_BUCKETS, 0] + tme - 1) // tme
    tile_end = jnp.cumsum(tiles_per_bucket)
    n_used = tile_end[-1]
    dest = (tile_end - tiles_per_bucket)[bucket] * tme + rank
    per_token = jnp.concatenate([wts[0:2], jnp.arange(n, dtype=F32)[None, :]], axis=0).T
    per_row = jnp.zeros((n_tiles * tme, 3), F32).at[dest].set(per_token)
    wts_sorted = per_row[:, 0:2]
    src = per_row[:, 2].astype(jnp.int32)
    last_tile = jnp.minimum(jnp.arange(n_tiles), n_used - 1)
    tile_bucket = jnp.sum((tile_end[None, :] <= last_tile[:, None]).astype(jnp.int32), axis=1)
    tile_bucket = jnp.minimum(tile_bucket, N_BUCKETS - 1)
    group, pair = tile_bucket // N_PAIRS, tile_bucket % N_PAIRS
    e_lo = group * EXPERTS_PER_GROUP + jnp.asarray(_PAIR_LO, jnp.int32)[pair]
    e_hi = group * EXPERTS_PER_GROUP + jnp.asarray(_PAIR_HI, jnp.int32)[pair]

    yc = _moe(e_lo, e_hi, n_used.reshape(1).astype(jnp.int32), src, h2c, wts_sorted, norm2_g,
              w_e_gate[0].astype(BF16), w_e_up[0].astype(BF16), w_e_down[0].astype(BF16), tme, n_tiles)
    out = _final(dest.astype(jnp.int32), h2c, yc, final_g.reshape(1, d), tmf=512)
    return out.reshape(batch, seq, d)
```

```python
import functools
import math

import jax
import jax.numpy as jnp
from jax import lax
from jax.experimental import pallas as pl
from jax.experimental.pallas import tpu as pltpu

D_MODEL = 1024
N_META = 16
N_HEADS = 8
HEAD_DIM = 64
V_DIM = 2 * HEAD_DIM
POOL_WINDOWS = (2, 4, 8, 16)
N_POOL_GROUPS = len(POOL_WINDOWS)
POOL_GROUP_DIM = 128
POOL_WIDTH = N_POOL_GROUPS * POOL_GROUP_DIM
ROPE_THETA = 10000.0
N_GROUPS = 4
EXPERTS_PER_GROUP = 4
N_EXPERTS = N_GROUPS * EXPERTS_PER_GROUP
D_EXPERT = 512
EPS = 1e-6
LAMBDA_INIT = 0.8 - 0.6 * math.exp(-0.3 * 0)
LOG2_E = math.log2(math.e)

_PAIR_LO = (0, 0, 0, 1, 1, 2)
_PAIR_HI = (1, 2, 3, 2, 3, 3)
N_PAIRS = len(_PAIR_LO)
N_BUCKETS = N_GROUPS * N_PAIRS
N_BUCKET_ROWS = 32
GATHER_UNROLL = 8

LANES = 128
NEG_BIG = -1e30
VMEM_LIMIT = 48 * 1024 * 1024

F32 = jnp.float32
BF16 = jnp.bfloat16

_HEADS_PER_STEP = 2
_IN_STEPS = N_HEADS // _HEADS_PER_STEP
_QKV_BLK = _HEADS_PER_STEP * V_DIM
_GATE_BLK = 2 * D_MODEL // _IN_STEPS
_O_K = D_MODEL // _QKV_BLK
_O_V = 2 * D_MODEL // _QKV_BLK
_O_U = 3 * D_MODEL // POOL_GROUP_DIM
_O_G = (3 * D_MODEL + POOL_WIDTH) // _GATE_BLK


def _params(*sem):
    return pltpu.CompilerParams(dimension_semantics=sem, vmem_limit_bytes=VMEM_LIMIT)


TOKEN_ROWS = D_MODEL // LANES


def _store_token_major(ref, val, rows):
    for s in range(TOKEN_ROWS):
        ref[pl.ds(s, rows, stride=TOKEN_ROWS), :] = val[:, s * LANES:(s + 1) * LANES]


def _load_token_major(ref, rows):
    return jnp.concatenate([ref[pl.ds(s, rows, stride=TOKEN_ROWS), :] for s in range(TOKEN_ROWS)], axis=1)


def _rope(z, cos, sin_signed, first_half):
    outs = []
    for c in range(z.shape[1] // LANES):
        zc = z[:, c * LANES:(c + 1) * LANES]
        rot = jnp.where(first_half, pltpu.roll(zc, LANES - HEAD_DIM // 2, 1),
                        pltpu.roll(zc, HEAD_DIM // 2, 1))
        outs.append(zc * cos + rot * sin_signed)
    return jnp.concatenate(outs, axis=1)


def _inproj_kernel(x_ref, g_ref, cos_ref, sin_ref, wq_ref, wk_ref, wv_ref, wu_ref, wg_ref, bg_ref,
                   q_ref, k_ref, v_ref, u_ref, gate_ref, hn_ref):
    @pl.when(pl.program_id(1) == 0)
    def _():
        x = x_ref[...]
        ms = jnp.mean(x * x, axis=-1, keepdims=True)
        hn_ref[...] = (x * lax.rsqrt(ms + EPS) * g_ref[...]).astype(BF16)

    hn = hn_ref[...]
    cos = cos_ref[...]
    sin = sin_ref[...]
    lane = lax.broadcasted_iota(jnp.int32, cos.shape, 1)
    first_half = jnp.bitwise_and(lane, HEAD_DIM - 1) < (HEAD_DIM // 2)
    zq = jnp.dot(hn, wq_ref[...], preferred_element_type=F32)
    q_ref[...] = (_rope(zq, cos, sin, first_half) * (LOG2_E / math.sqrt(HEAD_DIM))).astype(BF16)
    zk = jnp.dot(hn, wk_ref[...], preferred_element_type=F32)
    k_ref[...] = _rope(zk, cos, sin, first_half).astype(BF16)
    v_ref[...] = jnp.dot(hn, wv_ref[...], preferred_element_type=F32).astype(BF16)
    u_ref[...] = jnp.dot(hn, wu_ref[...], preferred_element_type=F32)
    zg = jnp.dot(hn, wg_ref[...], preferred_element_type=F32) + bg_ref[...]
    gate_ref[...] = (1.0 / (1.0 + jnp.exp(-zg))).astype(BF16)


def _inproj(x2, g1, cos, sin, w_in, b_gate, tm):
    n = x2.shape[0]
    n_pos_blocks = cos.shape[0] // tm
    row = lambda i, j: (i, 0)
    pos = lambda i, j: (i % n_pos_blocks, 0)
    return pl.pallas_call(
        _inproj_kernel,
        grid=(n // tm, _IN_STEPS),
        in_specs=[
            pl.BlockSpec((tm, D_MODEL), row),
            pl.BlockSpec((1, D_MODEL), lambda i, j: (0, 0)),
            pl.BlockSpec((tm, LANES), pos),
            pl.BlockSpec((tm, LANES), pos),
            pl.BlockSpec((D_MODEL, _QKV_BLK), lambda i, j: (0, j)),
            pl.BlockSpec((D_MODEL, _QKV_BLK), lambda i, j: (0, _O_K + j)),
            pl.BlockSpec((D_MODEL, _QKV_BLK), lambda i, j: (0, _O_V + j)),
            pl.BlockSpec((D_MODEL, POOL_GROUP_DIM), lambda i, j: (0, _O_U + j)),
            pl.BlockSpec((D_MODEL, _GATE_BLK), lambda i, j: (0, _O_G + j)),
            pl.BlockSpec((1, _GATE_BLK), lambda i, j: (0, j)),
        ],
        out_specs=[
            pl.BlockSpec((tm, _QKV_BLK), lambda i, j: (i, j)),
            pl.BlockSpec((tm, _QKV_BLK), lambda i, j: (i, j)),
            pl.BlockSpec((tm, _QKV_BLK), lambda i, j: (i, j)),
            pl.BlockSpec((tm, POOL_GROUP_DIM), lambda i, j: (i, j)),
            pl.BlockSpec((tm, _GATE_BLK), lambda i, j: (i, j)),
        ],
        out_shape=[
            jax.ShapeDtypeStruct((n, D_MODEL), BF16),
            jax.ShapeDtypeStruct((n, D_MODEL), BF16),
            jax.ShapeDtypeStruct((n, D_MODEL), BF16),
            jax.ShapeDtypeStruct((n, POOL_WIDTH), F32),
            jax.ShapeDtypeStruct((n, 2 * D_MODEL), BF16),
        ],
        scratch_shapes=[pltpu.VMEM((tm, D_MODEL), BF16)],
        compiler_params=_params("arbitrary", "arbitrary"),
        name="inproj",
    )(x2, g1, cos, sin, w_in, w_in, w_in, w_in, w_in, b_gate)


def _attn_kernel(lq1_ref, lk1_ref, lq2_ref, lk2_ref, subg_ref, q_ref, k_ref, v_ref, km_ref, vm_ref,
                 o_ref, vext_sc, qq_a, m_a, acc_a, qq_b, m_b, acc_b, *, bq):
    t_id = pl.program_id(2)
    nq = k_ref.shape[0] // bq

    @pl.when(t_id == 0)
    def _():
        vext_sc[:, 0:V_DIM] = v_ref[...]
        vext_sc[:, V_DIM:2 * V_DIM] = jnp.ones(v_ref.shape, BF16)

    r = lax.broadcasted_iota(jnp.int32, (bq, bq + LANES), 0)
    c = lax.broadcasted_iota(jnp.int32, (bq, bq + LANES), 1)
    visible = jnp.logical_or(c <= r, jnp.logical_and(c >= bq, c < bq + N_META))
    bias = jnp.where(visible, 0.0, NEG_BIG)
    bias = jnp.concatenate([bias, bias], axis=0)
    vm_ext = jnp.concatenate([vm_ref[...], jnp.ones((LANES, V_DIM), BF16)], axis=1)
    lam = (jnp.exp(jnp.sum(lq1_ref[...] * lk1_ref[...], axis=-1, keepdims=True))
           - jnp.exp(jnp.sum(lq2_ref[...] * lk2_ref[...], axis=-1, keepdims=True)) + LAMBDA_INIT)
    lane = lax.broadcasted_iota(jnp.int32, (bq, V_DIM), 1)

    def load_q(rows, qq_sc):
        q = q_ref[rows, :]
        zero = jnp.zeros_like(q)
        qq_sc[0:bq, :] = jnp.where(lane < HEAD_DIM, q, zero)
        qq_sc[bq:2 * bq, :] = jnp.where(lane >= HEAD_DIM, q, zero)

    def step(state, kb, vb, first):
        qq_sc, m_sc, acc_sc = state
        s = lax.dot_general(qq_sc[...], kb, (((1,), (1,)), ((), ())), preferred_element_type=F32)
        if first:
            s = s + bias
        n_tiles = s.shape[1] // LANES
        rm = s[:, 0:LANES]
        for t in range(1, n_tiles):
            rm = jnp.maximum(rm, s[:, t * LANES:(t + 1) * LANES])
        rmax = jnp.max(rm, axis=-1, keepdims=True)
        if first:
            m_new = jnp.broadcast_to(rmax, (2 * bq, LANES))
        else:
            m_prev = m_sc[...]
            m_new = jnp.maximum(m_prev, rmax)
        p = jnp.exp2(s - jnp.concatenate([m_new] * n_tiles, axis=1))
        pv = jnp.dot(p.astype(BF16), vb, preferred_element_type=F32)
        if first:
            acc_sc[...] = pv
        else:
            alpha = jnp.exp2(m_prev - m_new)
            acc_sc[...] = jnp.concatenate([alpha, alpha], axis=1) * acc_sc[...] + pv
        m_sc[...] = m_new

    def key_step(state, n_full, j):
        if j == 0:
            diag = slice(n_full * bq, (n_full + 1) * bq)
            step(state, jnp.concatenate([k_ref[diag, :], km_ref[...]], axis=0),
                 jnp.concatenate([vext_sc[diag, :], vm_ext], axis=0), True)
        else:
            blk = slice((j - 1) * bq, j * bq)
            step(state, k_ref[blk, :], vext_sc[blk, :], False)

    def finish(rows, acc_sc):
        o1 = acc_sc[0:bq, 0:V_DIM] / acc_sc[0:bq, V_DIM:2 * V_DIM]
        o2 = acc_sc[bq:2 * bq, 0:V_DIM] / acc_sc[bq:2 * bq, V_DIM:2 * V_DIM]
        o = o1 - lam * o2
        ms = jnp.mean(o * o, axis=-1, keepdims=True)
        o_ref[rows, :] = (o * lax.rsqrt(ms + EPS) * subg_ref[...] * (1.0 - LAMBDA_INIT)).astype(BF16)

    def pair(t):
        blocks = ((t, (qq_a, m_a, acc_a)), (nq - 1 - t, (qq_b, m_b, acc_b)))
        for n_full, state in blocks:
            load_q(slice(n_full * bq, (n_full + 1) * bq), state[0])
        for j in range(nq):
            for n_full, state in blocks:
                if j <= n_full:
                    key_step(state, n_full, j)
        for n_full, state in blocks:
            finish(slice(n_full * bq, (n_full + 1) * bq), state[2])

    for t in range(nq // 2):
        pl.when(t_id == t)(functools.partial(pair, t))


def _attention(q, k, v, km, vm, lq1, lk1, lq2, lk2, subg, batch, seq, bq):
    nq = seq // bq
    small = lambda b, h, i: (0, 0)
    head = lambda b, h, i: (b, h)
    state_scratch = [
        pltpu.VMEM((2 * bq, V_DIM), BF16),
        pltpu.VMEM((2 * bq, LANES), F32),
        pltpu.VMEM((2 * bq, 2 * V_DIM), F32),
    ]
    return pl.pallas_call(
        functools.partial(_attn_kernel, bq=bq),
        grid=(batch, N_HEADS, nq // 2),
        in_specs=[
            pl.BlockSpec((1, HEAD_DIM), small),
            pl.BlockSpec((1, HEAD_DIM), small),
            pl.BlockSpec((1, HEAD_DIM), small),
            pl.BlockSpec((1, HEAD_DIM), small),
            pl.BlockSpec((1, V_DIM), small),
            pl.BlockSpec((seq, V_DIM), head),
            pl.BlockSpec((seq, V_DIM), head),
            pl.BlockSpec((seq, V_DIM), head),
            pl.BlockSpec((LANES, V_DIM), lambda b, h, i: (0, h)),
            pl.BlockSpec((LANES, V_DIM), lambda b, h, i: (0, h)),
        ],
        out_specs=pl.BlockSpec((seq, V_DIM), head),
        out_shape=jax.ShapeDtypeStruct(q.shape, BF16),
        scratch_shapes=[pltpu.VMEM((seq, 2 * V_DIM), BF16)] + state_scratch + state_scratch,
        compiler_params=_params("arbitrary", "arbitrary", "arbitrary"),
        name="diff_attn",
    )(lq1, lk1, lq2, lk2, subg, q, k, v, km, vm)


def _mix_kernel(x_ref, a_ref, u_ref, uprev_ref, umeta_ref, gate_ref, wa_ref, pw_ref, ps_ref, wp_ref,
                wo_ref, g2_ref, wr_ref, br_ref, h2c_ref, logt_ref, ext_sc, *, tm, tiles_per_seq):
    i = pl.program_id(0)
    first = (i % tiles_per_seq) == 0

    @pl.when(first)
    def _():
        ext_sc[0:N_META, :] = umeta_ref[...]

    @pl.when(jnp.logical_not(first))
    def _():
        ext_sc[0:N_META, :] = uprev_ref[...]

    ext_sc[N_META:N_META + tm, :] = u_ref[...]

    pooled = []
    for g, w in enumerate(POOL_WINDOWS):
        cs = slice(g * POOL_GROUP_DIM, (g + 1) * POOL_GROUP_DIM)
        cur = ext_sc[N_META:N_META + tm, cs]
        tot = cur
        for kk in range(1, w):
            tot = tot + ext_sc[N_META - kk:N_META - kk + tm, cs]
        y = (tot * (1.0 / w) - cur).astype(BF16)
        pooled.append(jnp.dot(y, pw_ref[g], preferred_element_type=F32))
    yp = (jnp.concatenate(pooled, axis=1) * ps_ref[...]).astype(BF16)
    y_pool = jnp.dot(yp, wp_ref[...], preferred_element_type=F32)
    y_attn = jnp.dot(a_ref[...], wa_ref[...], preferred_element_type=F32)
    mixed = (gate_ref[:, 0:D_MODEL].astype(F32) * y_attn
             + gate_ref[:, D_MODEL:2 * D_MODEL].astype(F32) * y_pool)
    h2 = x_ref[...] + jnp.dot(mixed.astype(BF16), wo_ref[...], preferred_element_type=F32)
    _store_token_major(h2c_ref, h2, tm)
    ms = jnp.mean(h2 * h2, axis=-1, keepdims=True)
    hn2 = h2 * lax.rsqrt(ms + EPS) * g2_ref[...]
    hn2_hi = hn2.astype(BF16)
    hn2_lo = (hn2 - hn2_hi.astype(F32)).astype(BF16)
    wr = wr_ref[...]
    wr_hi = wr.astype(BF16)
    wr_lo = (wr - wr_hi.astype(F32)).astype(BF16)
    nt = (((1,), (1,)), ((), ()))
    logt = (lax.dot_general(wr_hi, hn2_hi, nt, preferred_element_type=F32)
            + lax.dot_general(wr_hi, hn2_lo, nt, preferred_element_type=F32)
            + lax.dot_general(wr_lo, hn2_hi, nt, preferred_element_type=F32))
    logt_ref[...] = logt + br_ref[...]


def _mix(x2, a, u, umeta, gates, w_attn, pool_w, pool_scale, w_pool, w_out, g2, wr_t, br_t, seq, tm):
    n = x2.shape[0]
    tiles_per_seq = seq // tm
    halo_blocks = tm // N_META
    const2 = lambda i: (0, 0)
    n_rt = wr_t.shape[0]
    return pl.pallas_call(
        functools.partial(_mix_kernel, tm=tm, tiles_per_seq=tiles_per_seq),
        grid=(n // tm,),
        in_specs=[
            pl.BlockSpec((tm, D_MODEL), lambda i: (i, 0)),
            pl.BlockSpec((tm, D_MODEL), lambda i: (i, 0)),
            pl.BlockSpec((tm, POOL_WIDTH), lambda i: (i, 0)),
            pl.BlockSpec((N_META, POOL_WIDTH), lambda i: (jnp.maximum(i * halo_blocks - 1, 0), 0)),
            pl.BlockSpec((N_META, POOL_WIDTH), const2),
            pl.BlockSpec((tm, 2 * D_MODEL), lambda i: (i, 0)),
            pl.BlockSpec((D_MODEL, D_MODEL), const2),
            pl.BlockSpec((N_POOL_GROUPS, POOL_GROUP_DIM, POOL_GROUP_DIM), lambda i: (0, 0, 0)),
            pl.BlockSpec((1, POOL_WIDTH), const2),
            pl.BlockSpec((POOL_WIDTH, D_MODEL), const2),
            pl.BlockSpec((D_MODEL, D_MODEL), const2),
            pl.BlockSpec((1, D_MODEL), const2),
            pl.BlockSpec((n_rt, D_MODEL), const2),
            pl.BlockSpec((n_rt, 1), const2),
        ],
        out_specs=[
            pl.BlockSpec((tm * TOKEN_ROWS, LANES), lambda i: (i, 0)),
            pl.BlockSpec((n_rt, tm), lambda i: (0, i)),
        ],
        out_shape=[
            jax.ShapeDtypeStruct((n * TOKEN_ROWS, LANES), F32),
            jax.ShapeDtypeStruct((n_rt, n), F32),
        ],
        scratch_shapes=[pltpu.VMEM((tm + N_META, POOL_WIDTH), F32)],
        compiler_params=_params("arbitrary"),
        name="mix",
    )(x2, a, u, u, umeta, gates, w_attn, pool_w, pool_scale, w_pool, w_out, g2, wr_t, br_t)


def _route_kernel(logt_ref, tri_ref, ids_ref, wts_ref, cnt_ref, carry_sc):
    @pl.when(pl.program_id(0) == 0)
    def _():
        carry_sc[...] = jnp.zeros_like(carry_sc)

    lg = logt_ref[...]
    g = [lg[r:r + 1, :] for r in range(N_GROUPS)]
    gmax = functools.reduce(jnp.maximum, g)
    gsel = jnp.full_like(g[0], N_GROUPS - 1).astype(jnp.int32)
    for r in range(N_GROUPS - 2, -1, -1):
        gsel = jnp.where(g[r] == gmax, r, gsel)
    p_group = 1.0 / functools.reduce(lambda a, b: a + b, [jnp.exp(x - gmax) for x in g])
    e = []
    for jj in range(EXPERTS_PER_GROUP):
        v = lg[N_GROUPS + jj:N_GROUPS + jj + 1, :]
        for r in range(1, N_GROUPS):
            row = N_GROUPS + r * EXPERTS_PER_GROUP + jj
            v = jnp.where(gsel == r, lg[row:row + 1, :], v)
        e.append(v)
    v1 = functools.reduce(jnp.maximum, e)
    i1 = jnp.full_like(gsel, EXPERTS_PER_GROUP - 1)
    for jj in range(EXPERTS_PER_GROUP - 2, -1, -1):
        i1 = jnp.where(e[jj] == v1, jj, i1)
    rest = [jnp.where(i1 == jj, -jnp.inf, e[jj]) for jj in range(EXPERTS_PER_GROUP)]
    v2 = functools.reduce(jnp.maximum, rest)
    i2 = jnp.full_like(gsel, EXPERTS_PER_GROUP - 1)
    for jj in range(EXPERTS_PER_GROUP - 2, -1, -1):
        i2 = jnp.where(jnp.logical_and(rest[jj] == v2, i1 != jj), jj, i2)
    t = jnp.exp(v2 - v1)
    w1 = p_group / (1.0 + t)
    w2 = p_group * t / (1.0 + t)
    lo = jnp.minimum(i1, i2)
    hi = jnp.maximum(i1, i2)
    pair = jnp.where(lo == 0, 0, jnp.where(lo == 1, 3, 5)) + hi - lo - 1
    bucket = gsel * N_PAIRS + pair
    first_is_lo = i1 < i2
    w_lo = jnp.where(first_is_lo, w1, w2)
    w_hi = jnp.where(first_is_lo, w2, w1)

    tl = lg.shape[1]
    bid = lax.broadcasted_iota(jnp.int32, (N_BUCKET_ROWS, tl), 0)
    hot = bid == bucket
    hot_b = jnp.where(hot, 1.0, 0.0).astype(BF16)
    prefix = jnp.dot(hot_b, tri_ref[...], preferred_element_type=F32)
    carry = carry_sc[...]
    before = prefix + jnp.concatenate([carry] * (tl // LANES), axis=1)
    rank = jnp.sum(jnp.where(hot, before, 0.0), axis=0, keepdims=True)
    total = carry + jnp.dot(hot_b, jnp.ones((tl, LANES), BF16), preferred_element_type=F32)
    carry_sc[...] = total
    cnt_ref[...] = total.astype(jnp.int32)
    pad_i = jnp.zeros((6, tl), jnp.int32)
    ids_ref[...] = jnp.concatenate([bucket, rank.astype(jnp.int32), pad_i], axis=0)
    wts_ref[...] = jnp.concatenate([w_lo, w_hi, pad_i.astype(F32)], axis=0)


def _route(logt, tl):
    n_rt, n = logt.shape
    tri = jnp.triu(jnp.ones((tl, tl), BF16), k=1)
    return pl.pallas_call(
        _route_kernel,
        grid=(n // tl,),
        in_specs=[pl.BlockSpec((n_rt, tl), lambda i: (0, i)),
                  pl.BlockSpec((tl, tl), lambda i: (0, 0))],
        out_specs=[pl.BlockSpec((8, tl), lambda i: (0, i)),
                   pl.BlockSpec((8, tl), lambda i: (0, i)),
                   pl.BlockSpec((N_BUCKET_ROWS, LANES), lambda i: (0, 0))],
        out_shape=[jax.ShapeDtypeStruct((8, n), jnp.int32),
                   jax.ShapeDtypeStruct((8, n), F32),
                   jax.ShapeDtypeStruct((N_BUCKET_ROWS, LANES), jnp.int32)],
        scratch_shapes=[pltpu.VMEM((N_BUCKET_ROWS, LANES), F32)],
        compiler_params=_params("arbitrary"),
        name="route",
    )(logt, tri)


def _token_copy(src_hbm, src_token, dst_buf, dst_row, sem):
    return pltpu.make_async_copy(
        src_hbm.at[pl.ds(pl.multiple_of(src_token * TOKEN_ROWS, TOKEN_ROWS), TOKEN_ROWS), :],
        dst_buf.at[pl.ds(pl.multiple_of(dst_row * TOKEN_ROWS, TOKEN_ROWS), TOKEN_ROWS), :],
        sem)


def _start_gather(idx_ref, first, rows, src_hbm, dst_buf, sem):
    def body(rr, carry):
        for uu in range(GATHER_UNROLL):
            r = rr * GATHER_UNROLL + uu
            _token_copy(src_hbm, idx_ref[first + r], dst_buf, r, sem).start(priority=uu % 2)
        return carry

    lax.fori_loop(0, rows // GATHER_UNROLL, body, 0)


def _wait_gather(rows, src_hbm, dst_buf, sem):
    pltpu.make_async_copy(src_hbm.at[pl.ds(0, rows * TOKEN_ROWS), :], dst_buf, sem).wait()


def _moe_kernel(elo_ref, ehi_ref, nused_ref, src_ref, h2c_hbm, wts_ref, g2_ref, wg_lo_ref, wu_lo_ref,
                wg_hi_ref, wu_hi_ref, wd_lo_ref, wd_hi_ref, yc_ref, xbuf, sem, *, tme):
    i = pl.program_id(0)
    n_used = nused_ref[0]
    slot = jnp.bitwise_and(i, 1)

    @pl.when(i == 0)
    def _():
        _start_gather(src_ref, 0, tme, h2c_hbm, xbuf.at[0], sem.at[0])

    def tile(gather_next):
        n_pieces = 4
        piece = tme // n_pieces

        def start_piece(k):
            if gather_next:
                for r in range(k * piece, (k + 1) * piece):
                    _token_copy(h2c_hbm, src_ref[(i + 1) * tme + r], xbuf.at[1 - slot], r,
                                sem.at[1 - slot]).start(priority=r % 2)

        _wait_gather(tme, h2c_hbm, xbuf.at[slot], sem.at[slot])
        x = _load_token_major(xbuf.at[slot], tme)
        ms = jnp.mean(x * x, axis=-1, keepdims=True)
        hn = (x * lax.rsqrt(ms + EPS) * g2_ref[...]).astype(BF16)
        wts = wts_ref[...]
        start_piece(0)
        gate_lo = jnp.dot(hn, wg_lo_ref[0], preferred_element_type=F32)
        start_piece(1)
        up_lo = jnp.dot(hn, wu_lo_ref[0], preferred_element_type=F32)
        start_piece(2)
        gate_hi = jnp.dot(hn, wg_hi_ref[0], preferred_element_type=F32)
        start_piece(3)
        up_hi = jnp.dot(hn, wu_hi_ref[0], preferred_element_type=F32)
        act_lo = (gate_lo / (1.0 + jnp.exp(-gate_lo)) * up_lo * wts[:, 0:1]).astype(BF16)
        act_hi = (gate_hi / (1.0 + jnp.exp(-gate_hi)) * up_hi * wts[:, 1:2]).astype(BF16)
        y = (jnp.dot(act_lo, wd_lo_ref[0], preferred_element_type=F32)
             + jnp.dot(act_hi, wd_hi_ref[0], preferred_element_type=F32))
        _store_token_major(yc_ref, y, tme)

    pl.when(i + 1 < n_used)(functools.partial(tile, True))
    pl.when(i + 1 == n_used)(functools.partial(tile, False))

    @pl.when(i >= n_used)
    def _():
        yc_ref[...] = jnp.zeros_like(yc_ref)


def _moe(elo, ehi, nused, src, h2c, wts_sorted, g2, wg, wu, wd, tme, n_tiles):
    used = lambda i, elo, ehi, nu, src: (jnp.minimum(i, nu[0] - 1), 0)
    w_lo = lambda i, elo, ehi, nu, src: (elo[i], 0, 0)
    w_hi = lambda i, elo, ehi, nu, src: (ehi[i], 0, 0)
    grid_spec = pltpu.PrefetchScalarGridSpec(
        num_scalar_prefetch=4,
        grid=(n_tiles,),
        in_specs=[
            pl.BlockSpec(memory_space=pl.ANY),
            pl.BlockSpec((tme, 2), used),
            pl.BlockSpec((1, D_MODEL), lambda i, elo, ehi, nu, src: (0, 0)),
            pl.BlockSpec((1, D_MODEL, D_EXPERT), w_lo),
            pl.BlockSpec((1, D_MODEL, D_EXPERT), w_lo),
            pl.BlockSpec((1, D_MODEL, D_EXPERT), w_hi),
            pl.BlockSpec((1, D_MODEL, D_EXPERT), w_hi),
            pl.BlockSpec((1, D_EXPERT, D_MODEL), w_lo),
            pl.BlockSpec((1, D_EXPERT, D_MODEL), w_hi),
        ],
        out_specs=pl.BlockSpec((tme * TOKEN_ROWS, LANES), lambda i, elo, ehi, nu, src: (i, 0)),
        scratch_shapes=[pltpu.VMEM((2, tme * TOKEN_ROWS, LANES), F32), pltpu.SemaphoreType.DMA((2,))],
    )
    return pl.pallas_call(
        functools.partial(_moe_kernel, tme=tme),
        grid_spec=grid_spec,
        out_shape=jax.ShapeDtypeStruct((n_tiles * tme * TOKEN_ROWS, LANES), F32),
        compiler_params=_params("arbitrary"),
        name="moe",
    )(elo, ehi, nused, src, h2c, wts_sorted, g2, wg, wu, wg, wu, wd, wd)


def _final_kernel(dest_ref, h2c_ref, yc_hbm, gf_ref, o_ref, ybuf, sem, *, tmf):
    i = pl.program_id(0)
    slot = jnp.bitwise_and(i, 1)

    @pl.when(i == 0)
    def _():
        _start_gather(dest_ref, 0, tmf, yc_hbm, ybuf.at[0], sem.at[0])

    @pl.when(i + 1 < pl.num_programs(0))
    def _():
        _start_gather(dest_ref, (i + 1) * tmf, tmf, yc_hbm, ybuf.at[1 - slot], sem.at[1 - slot])

    _wait_gather(tmf, yc_hbm, ybuf.at[slot], sem.at[slot])
    h = _load_token_major(h2c_ref, tmf) + _load_token_major(ybuf.at[slot], tmf)
    ms = jnp.mean(h * h, axis=-1, keepdims=True)
    o_ref[...] = h * lax.rsqrt(ms + EPS) * gf_ref[...]


def _final(dest, h2c, yc, gf, tmf):
    n = h2c.shape[0] // TOKEN_ROWS
    grid_spec = pltpu.PrefetchScalarGridSpec(
        num_scalar_prefetch=1,
        grid=(n // tmf,),
        in_specs=[
            pl.BlockSpec((tmf * TOKEN_ROWS, LANES), lambda i, dest: (i, 0)),
            pl.BlockSpec(memory_space=pl.ANY),
            pl.BlockSpec((1, D_MODEL), lambda i, dest: (0, 0)),
        ],
        out_specs=pl.BlockSpec((tmf, D_MODEL), lambda i, dest: (i, 0)),
        scratch_shapes=[pltpu.VMEM((2, tmf * TOKEN_ROWS, LANES), F32), pltpu.SemaphoreType.DMA((2,))],
    )
    return pl.pallas_call(
        functools.partial(_final_kernel, tmf=tmf),
        grid_spec=grid_spec,
        out_shape=jax.ShapeDtypeStruct((n, D_MODEL), F32),
        compiler_params=_params("arbitrary"),
        name="final",
    )(dest, h2c, yc, gf)


def _rope_tables(t):
    inv = 1.0 / (ROPE_THETA ** (jnp.arange(0, HEAD_DIM, 2, dtype=F32) / HEAD_DIM))
    ang = jnp.arange(t, dtype=F32)[:, None] * inv[None, :]
    ang = jnp.concatenate([ang, ang, ang, ang], axis=-1)
    sign = jnp.where((jnp.arange(LANES) % HEAD_DIM) < HEAD_DIM // 2, -1.0, 1.0).astype(F32)
    return jnp.cos(ang), jnp.sin(ang) * sign[None, :]


def kernel(x, meta, norm1_g, w_in, b_gate, lambda_q1, lambda_k1, lambda_q2, lambda_k2, subln_g, pool_w,
           pool_scale, w_attn_br, w_pool_br, w_out, norm2_g, w_router_group, b_router_group,
           w_router_expert, b_router_expert, w_e_gate, w_e_up, w_e_down, final_g):
    batch, seq, d = x.shape
    n = batch * seq
    x2 = x.reshape(n, d)
    cos, sin = _rope_tables(seq + N_META)
    w_in_b = w_in[0].astype(BF16)

    q, k, v, u, gates = _inproj(x2, norm1_g, cos[N_META:], sin[N_META:], w_in_b, b_gate, tm=1024)
    _, km, vm, um, _ = _inproj(meta, norm1_g, cos[:N_META], sin[:N_META], w_in_b, b_gate, tm=N_META)
    pad = ((0, LANES - N_META), (0, 0))
    a = _attention(q, k, v, jnp.pad(km, pad), jnp.pad(vm, pad), lambda_q1, lambda_k1, lambda_q2,
                   lambda_k2, subln_g, batch, seq, bq=256)

    n_rt = 32
    wr_t = jnp.concatenate([w_router_group[0], w_router_expert[0]], axis=1).T
    wr_t = jnp.pad(wr_t, ((0, n_rt - wr_t.shape[0]), (0, 0)))
    br_t = jnp.concatenate([b_router_group[0], b_router_expert[0]])
    br_t = jnp.pad(br_t, (0, n_rt - br_t.shape[0])).reshape(n_rt, 1)
    h2c, logt = _mix(x2, a, u, um, gates, w_attn_br[0].astype(BF16), pool_w[0].astype(BF16),
                     pool_scale, w_pool_br[0].astype(BF16), w_out[0].astype(BF16), norm2_g,
                     wr_t, br_t, seq, tm=512)
    ids, wts, cnt = _route(logt, tl=1024)

    tme = 256
    n_tiles = n // tme + N_BUCKETS
    bucket, rank = ids[0], ids[1]
    tiles_per_bucket = (cnt[:N_BUCKETS, 0] + tme - 1) // tme
    tile_end = jnp.cumsum(tiles_per_bucket)
    n_used = tile_end[-1]
    dest = (tile_end - tiles_per_bucket)[bucket] * tme + rank
    per_token = jnp.concatenate([wts[0:2], jnp.arange(n, dtype=F32)[None, :]], axis=0).T
    per_row = jnp.zeros((n_tiles * tme, 3), F32).at[dest].set(per_token)
    wts_sorted = per_row[:, 0:2]
    src = per_row[:, 2].astype(jnp.int32)
    last_tile = jnp.minimum(jnp.arange(n_tiles), n_used - 1)
    tile_bucket = jnp.sum((tile_end[None, :] <= last_tile[:, None]).astype(jnp.int32), axis=1)
    tile_bucket = jnp.minimum(tile_bucket, N_BUCKETS - 1)
    group, pair = tile_bucket // N_PAIRS, tile_bucket % N_PAIRS
    e_lo = group * EXPERTS_PER_GROUP + jnp.asarray(_PAIR_LO, jnp.int32)[pair]
    e_hi = group * EXPERTS_PER_GROUP + jnp.asarray(_PAIR_HI, jnp.int32)[pair]

    yc = _moe(e_lo, e_hi, n_used.reshape(1).astype(jnp.int32), src, h2c, wts_sorted, norm2_g,
              w_e_gate[0].astype(BF16), w_e_up[0].astype(BF16), w_e_down[0].astype(BF16), tme, n_tiles)
    out = _final(dest.astype(jnp.int32), h2c, yc, final_g.reshape(1, d), tmf=512)
    return out.reshape(batch, seq, d)
```

```python
import functools
import math

import jax
import jax.numpy as jnp
from jax import lax
from jax.experimental import pallas as pl
from jax.experimental.pallas import tpu as pltpu

D_MODEL = 1024
N_META = 16
N_HEADS = 8
HEAD_DIM = 64
V_DIM = 2 * HEAD_DIM
POOL_WINDOWS = (2, 4, 8, 16)
N_POOL_GROUPS = len(POOL_WINDOWS)
POOL_GROUP_DIM = 128
POOL_WIDTH = N_POOL_GROUPS * POOL_GROUP_DIM
ROPE_THETA = 10000.0
N_GROUPS = 4
EXPERTS_PER_GROUP = 4
N_EXPERTS = N_GROUPS * EXPERTS_PER_GROUP
D_EXPERT = 512
EPS = 1e-6
LAMBDA_INIT = 0.8 - 0.6 * math.exp(-0.3 * 0)
LOG2_E = math.log2(math.e)

_PAIR_LO = (0, 0, 0, 1, 1, 2)
_PAIR_HI = (1, 2, 3, 2, 3, 3)
N_PAIRS = len(_PAIR_LO)
N_BUCKETS = N_GROUPS * N_PAIRS
N_BUCKET_ROWS = 32
GATHER_UNROLL = 8

LANES = 128
NEG_BIG = -1e30
VMEM_LIMIT = 48 * 1024 * 1024

F32 = jnp.float32
BF16 = jnp.bfloat16

_HEADS_PER_STEP = 2
_IN_STEPS = N_HEADS // _HEADS_PER_STEP
_QKV_BLK = _HEADS_PER_STEP * V_DIM
_GATE_BLK = 2 * D_MODEL // _IN_STEPS
_O_K = D_MODEL // _QKV_BLK
_O_V = 2 * D_MODEL // _QKV_BLK
_O_U = 3 * D_MODEL // POOL_GROUP_DIM
_O_G = (3 * D_MODEL + POOL_WIDTH) // _GATE_BLK


def _params(*sem):
    return pltpu.CompilerParams(dimension_semantics=sem, vmem_limit_bytes=VMEM_LIMIT)


TOKEN_ROWS = D_MODEL // LANES


def _store_token_major(ref, val, rows):
    for s in range(TOKEN_ROWS):
        ref[pl.ds(s, rows, stride=TOKEN_ROWS), :] = val[:, s * LANES:(s + 1) * LANES]


def _load_token_major(ref, rows):
    return jnp.concatenate([ref[pl.ds(s, rows, stride=TOKEN_ROWS), :] for s in range(TOKEN_ROWS)], axis=1)


def _rope(z, cos, sin_signed, first_half):
    outs = []
    for c in range(z.shape[1] // LANES):
        zc = z[:, c * LANES:(c + 1) * LANES]
        rot = jnp.where(first_half, pltpu.roll(zc, LANES - HEAD_DIM // 2, 1),
                        pltpu.roll(zc, HEAD_DIM // 2, 1))
        outs.append(zc * cos + rot * sin_signed)
    return jnp.concatenate(outs, axis=1)


def _inproj_kernel(x_ref, g_ref, cos_ref, sin_ref, wq_ref, wk_ref, wv_ref, wu_ref, wg_ref, bg_ref,
                   q_ref, k_ref, v_ref, u_ref, gate_ref, hn_ref):
    @pl.when(pl.program_id(1) == 0)
    def _():
        x = x_ref[...]
        ms = jnp.mean(x * x, axis=-1, keepdims=True)
        hn_ref[...] = (x * lax.rsqrt(ms + EPS) * g_ref[...]).astype(BF16)

    hn = hn_ref[...]
    cos = cos_ref[...]
    sin = sin_ref[...]
    lane = lax.broadcasted_iota(jnp.int32, cos.shape, 1)
    first_half = jnp.bitwise_and(lane, HEAD_DIM - 1) < (HEAD_DIM // 2)
    zq = jnp.dot(hn, wq_ref[...], preferred_element_type=F32)
    q_ref[...] = (_rope(zq, cos, sin, first_half) * (LOG2_E / math.sqrt(HEAD_DIM))).astype(BF16)
    zk = jnp.dot(hn, wk_ref[...], preferred_element_type=F32)
    k_ref[...] = _rope(zk, cos, sin, first_half).astype(BF16)
    v_ref[...] = jnp.dot(hn, wv_ref[...], preferred_element_type=F32).astype(BF16)
    u_ref[...] = jnp.dot(hn, wu_ref[...], preferred_element_type=F32)
    zg = jnp.dot(hn, wg_ref[...], preferred_element_type=F32) + bg_ref[...]
    gate_ref[...] = (1.0 / (1.0 + jnp.exp(-zg))).astype(BF16)


def _inproj(x2, g1, cos, sin, w_in, b_gate, tm):
    n = x2.shape[0]
    n_pos_blocks = cos.shape[0] // tm
    row = lambda i, j: (i, 0)
    pos = lambda i, j: (i % n_pos_blocks, 0)
    return pl.pallas_call(
        _inproj_kernel,
        grid=(n // tm, _IN_STEPS),
        in_specs=[
            pl.BlockSpec((tm, D_MODEL), row),
            pl.BlockSpec((1, D_MODEL), lambda i, j: (0, 0)),
            pl.BlockSpec((tm, LANES), pos),
            pl.BlockSpec((tm, LANES), pos),
            pl.BlockSpec((D_MODEL, _QKV_BLK), lambda i, j: (0, j)),
            pl.BlockSpec((D_MODEL, _QKV_BLK), lambda i, j: (0, _O_K + j)),
            pl.BlockSpec((D_MODEL, _QKV_BLK), lambda i, j: (0, _O_V + j)),
            pl.BlockSpec((D_MODEL, POOL_GROUP_DIM), lambda i, j: (0, _O_U + j)),
            pl.BlockSpec((D_MODEL, _GATE_BLK), lambda i, j: (0, _O_G + j)),
            pl.BlockSpec((1, _GATE_BLK), lambda i, j: (0, j)),
        ],
        out_specs=[
            pl.BlockSpec((tm, _QKV_BLK), lambda i, j: (i, j)),
            pl.BlockSpec((tm, _QKV_BLK), lambda i, j: (i, j)),
            pl.BlockSpec((tm, _QKV_BLK), lambda i, j: (i, j)),
            pl.BlockSpec((tm, POOL_GROUP_DIM), lambda i, j: (i, j)),
            pl.BlockSpec((tm, _GATE_BLK), lambda i, j: (i, j)),
        ],
        out_shape=[
            jax.ShapeDtypeStruct((n, D_MODEL), BF16),
            jax.ShapeDtypeStruct((n, D_MODEL), BF16),
            jax.ShapeDtypeStruct((n, D_MODEL), BF16),
            jax.ShapeDtypeStruct((n, POOL_WIDTH), F32),
            jax.ShapeDtypeStruct((n, 2 * D_MODEL), BF16),
        ],
        scratch_shapes=[pltpu.VMEM((tm, D_MODEL), BF16)],
        compiler_params=_params("arbitrary", "arbitrary"),
        name="inproj",
    )(x2, g1, cos, sin, w_in, w_in, w_in, w_in, w_in, b_gate)


def _attn_kernel(lq1_ref, lk1_ref, lq2_ref, lk2_ref, subg_ref, q_ref, k_ref, v_ref, km_ref, vm_ref,
                 o_ref, vext_sc, *state_sc, bq):
    t_id = pl.program_id(2)
    nq = k_ref.shape[0] // bq
    states = [tuple(state_sc[3 * g:3 * g + 3]) for g in range(len(state_sc) // 3)]
    pairs_per_step = len(states) // 2

    @pl.when(t_id == 0)
    def _():
        vext_sc[:, 0:V_DIM] = v_ref[...]
        vext_sc[:, V_DIM:2 * V_DIM] = jnp.ones(v_ref.shape, BF16)

    r = lax.broadcasted_iota(jnp.int32, (bq, bq + LANES), 0)
    c = lax.broadcasted_iota(jnp.int32, (bq, bq + LANES), 1)
    visible = jnp.logical_or(c <= r, jnp.logical_and(c >= bq, c < bq + N_META))
    bias = jnp.where(visible, 0.0, NEG_BIG)
    bias = jnp.concatenate([bias, bias], axis=0)
    vm_ext = jnp.concatenate([vm_ref[...], jnp.ones((LANES, V_DIM), BF16)], axis=1)
    lam = (jnp.exp(jnp.sum(lq1_ref[...] * lk1_ref[...], axis=-1, keepdims=True))
           - jnp.exp(jnp.sum(lq2_ref[...] * lk2_ref[...], axis=-1, keepdims=True)) + LAMBDA_INIT)
    lane = lax.broadcasted_iota(jnp.int32, (bq, V_DIM), 1)

    def load_q(rows, qq_sc):
        q = q_ref[rows, :]
        zero = jnp.zeros_like(q)
        qq_sc[0:bq, :] = jnp.where(lane < HEAD_DIM, q, zero)
        qq_sc[bq:2 * bq, :] = jnp.where(lane >= HEAD_DIM, q, zero)

    def step(state, kb, vb, first):
        qq_sc, m_sc, acc_sc = state
        s = lax.dot_general(qq_sc[...], kb, (((1,), (1,)), ((), ())), preferred_element_type=F32)
        if first:
            s = s + bias
        n_tiles = s.shape[1] // LANES
        rm = s[:, 0:LANES]
        for t in range(1, n_tiles):
            rm = jnp.maximum(rm, s[:, t * LANES:(t + 1) * LANES])
        rmax = jnp.max(rm, axis=-1, keepdims=True)
        if first:
            m_new = jnp.broadcast_to(rmax, (2 * bq, LANES))
        else:
            m_prev = m_sc[...]
            m_new = jnp.maximum(m_prev, rmax)
        p = jnp.exp2(s - jnp.concatenate([m_new] * n_tiles, axis=1))
        pv = jnp.dot(p.astype(BF16), vb, preferred_element_type=F32)
        if first:
            acc_sc[...] = pv
        else:
            alpha = jnp.exp2(m_prev - m_new)
            acc_sc[...] = jnp.concatenate([alpha, alpha], axis=1) * acc_sc[...] + pv
        m_sc[...] = m_new

    def key_step(state, n_full, j):
        if j == 0:
            diag = slice(n_full * bq, (n_full + 1) * bq)
            step(state, jnp.concatenate([k_ref[diag, :], km_ref[...]], axis=0),
                 jnp.concatenate([vext_sc[diag, :], vm_ext], axis=0), True)
        else:
            blk = slice((j - 1) * bq, j * bq)
            step(state, k_ref[blk, :], vext_sc[blk, :], False)

    def finish(rows, acc_sc):
        o1 = acc_sc[0:bq, 0:V_DIM] / acc_sc[0:bq, V_DIM:2 * V_DIM]
        o2 = acc_sc[bq:2 * bq, 0:V_DIM] / acc_sc[bq:2 * bq, V_DIM:2 * V_DIM]
        o = o1 - lam * o2
        ms = jnp.mean(o * o, axis=-1, keepdims=True)
        o_ref[rows, :] = (o * lax.rsqrt(ms + EPS) * subg_ref[...] * (1.0 - LAMBDA_INIT)).astype(BF16)

    def chains(t):
        blocks = []
        for i in range(pairs_per_step):
            p = t * pairs_per_step + i
            blocks += [(p, states[2 * i]), (nq - 1 - p, states[2 * i + 1])]
        for n_full, state in blocks:
            load_q(slice(n_full * bq, (n_full + 1) * bq), state[0])
        for j in range(nq):
            for n_full, state in blocks:
                if j <= n_full:
                    key_step(state, n_full, j)
        for n_full, state in blocks:
            finish(slice(n_full * bq, (n_full + 1) * bq), state[2])

    n_steps = nq // (2 * pairs_per_step)
    if n_steps == 1:
        chains(0)
    else:
        for t in range(n_steps):
            pl.when(t_id == t)(functools.partial(chains, t))


def _attention(q, k, v, km, vm, lq1, lk1, lq2, lk2, subg, batch, seq, bq, group):
    nq = seq // bq
    small = lambda b, h, i: (0, 0)
    head = lambda b, h, i: (b, h)
    state_scratch = [
        pltpu.VMEM((2 * bq, V_DIM), BF16),
        pltpu.VMEM((2 * bq, LANES), F32),
        pltpu.VMEM((2 * bq, 2 * V_DIM), F32),
    ]
    return pl.pallas_call(
        functools.partial(_attn_kernel, bq=bq),
        grid=(batch, N_HEADS, nq // group),
        in_specs=[
            pl.BlockSpec((1, HEAD_DIM), small),
            pl.BlockSpec((1, HEAD_DIM), small),
            pl.BlockSpec((1, HEAD_DIM), small),
            pl.BlockSpec((1, HEAD_DIM), small),
            pl.BlockSpec((1, V_DIM), small),
            pl.BlockSpec((seq, V_DIM), head),
            pl.BlockSpec((seq, V_DIM), head),
            pl.BlockSpec((seq, V_DIM), head),
            pl.BlockSpec((LANES, V_DIM), lambda b, h, i: (0, h)),
            pl.BlockSpec((LANES, V_DIM), lambda b, h, i: (0, h)),
        ],
        out_specs=pl.BlockSpec((seq, V_DIM), head),
        out_shape=jax.ShapeDtypeStruct(q.shape, BF16),
        scratch_shapes=[pltpu.VMEM((seq, 2 * V_DIM), BF16)] + state_scratch * group,
        compiler_params=_params("arbitrary", "arbitrary", "arbitrary"),
        name="diff_attn",
    )(lq1, lk1, lq2, lk2, subg, q, k, v, km, vm)


def _mix_kernel(x_ref, a_ref, u_ref, uprev_ref, umeta_ref, gate_ref, wa_ref, pw_ref, ps_ref, wp_ref,
                wo_ref, g2_ref, wr_ref, br_ref, h2c_ref, logt_ref, ext_sc, *, tm, tiles_per_seq):
    i = pl.program_id(0)
    first = (i % tiles_per_seq) == 0

    @pl.when(first)
    def _():
        ext_sc[0:N_META, :] = umeta_ref[...]

    @pl.when(jnp.logical_not(first))
    def _():
        ext_sc[0:N_META, :] = uprev_ref[...]

    ext_sc[N_META:N_META + tm, :] = u_ref[...]

    pooled = []
    for g, w in enumerate(POOL_WINDOWS):
        cs = slice(g * POOL_GROUP_DIM, (g + 1) * POOL_GROUP_DIM)
        cur = ext_sc[N_META:N_META + tm, cs]
        tot = cur
        for kk in range(1, w):
            tot = tot + ext_sc[N_META - kk:N_META - kk + tm, cs]
        y = (tot * (1.0 / w) - cur).astype(BF16)
        pooled.append(jnp.dot(y, pw_ref[g], preferred_element_type=F32))
    yp = (jnp.concatenate(pooled, axis=1) * ps_ref[...]).astype(BF16)
    y_pool = jnp.dot(yp, wp_ref[...], preferred_element_type=F32)
    y_attn = jnp.dot(a_ref[...], wa_ref[...], preferred_element_type=F32)
    mixed = (gate_ref[:, 0:D_MODEL].astype(F32) * y_attn
             + gate_ref[:, D_MODEL:2 * D_MODEL].astype(F32) * y_pool)
    h2 = x_ref[...] + jnp.dot(mixed.astype(BF16), wo_ref[...], preferred_element_type=F32)
    _store_token_major(h2c_ref, h2, tm)
    ms = jnp.mean(h2 * h2, axis=-1, keepdims=True)
    hn2 = h2 * lax.rsqrt(ms + EPS) * g2_ref[...]
    hn2_hi = hn2.astype(BF16)
    hn2_lo = (hn2 - hn2_hi.astype(F32)).astype(BF16)
    wr = wr_ref[...]
    wr_hi = wr.astype(BF16)
    wr_lo = (wr - wr_hi.astype(F32)).astype(BF16)
    nt = (((1,), (1,)), ((), ()))
    logt = (lax.dot_general(wr_hi, hn2_hi, nt, preferred_element_type=F32)
            + lax.dot_general(wr_hi, hn2_lo, nt, preferred_element_type=F32)
            + lax.dot_general(wr_lo, hn2_hi, nt, preferred_element_type=F32))
    logt_ref[...] = logt + br_ref[...]


def _mix(x2, a, u, umeta, gates, w_attn, pool_w, pool_scale, w_pool, w_out, g2, wr_t, br_t, seq, tm):
    n = x2.shape[0]
    tiles_per_seq = seq // tm
    halo_blocks = tm // N_META
    const2 = lambda i: (0, 0)
    n_rt = wr_t.shape[0]
    return pl.pallas_call(
        functools.partial(_mix_kernel, tm=tm, tiles_per_seq=tiles_per_seq),
        grid=(n // tm,),
        in_specs=[
            pl.BlockSpec((tm, D_MODEL), lambda i: (i, 0)),
            pl.BlockSpec((tm, D_MODEL), lambda i: (i, 0)),
            pl.BlockSpec((tm, POOL_WIDTH), lambda i: (i, 0)),
            pl.BlockSpec((N_META, POOL_WIDTH), lambda i: (jnp.maximum(i * halo_blocks - 1, 0), 0)),
            pl.BlockSpec((N_META, POOL_WIDTH), const2),
            pl.BlockSpec((tm, 2 * D_MODEL), lambda i: (i, 0)),
            pl.BlockSpec((D_MODEL, D_MODEL), const2),
            pl.BlockSpec((N_POOL_GROUPS, POOL_GROUP_DIM, POOL_GROUP_DIM), lambda i: (0, 0, 0)),
            pl.BlockSpec((1, POOL_WIDTH), const2),
            pl.BlockSpec((POOL_WIDTH, D_MODEL), const2),
            pl.BlockSpec((D_MODEL, D_MODEL), const2),
            pl.BlockSpec((1, D_MODEL), const2),
            pl.BlockSpec((n_rt, D_MODEL), const2),
            pl.BlockSpec((n_rt, 1), const2),
        ],
        out_specs=[
            pl.BlockSpec((tm * TOKEN_ROWS, LANES), lambda i: (i, 0)),
            pl.BlockSpec((n_rt, tm), lambda i: (0, i)),
        ],
        out_shape=[
            jax.ShapeDtypeStruct((n * TOKEN_ROWS, LANES), F32),
            jax.ShapeDtypeStruct((n_rt, n), F32),
        ],
        scratch_shapes=[pltpu.VMEM((tm + N_META, POOL_WIDTH), F32)],
        compiler_params=_params("arbitrary"),
        name="mix",
    )(x2, a, u, u, umeta, gates, w_attn, pool_w, pool_scale, w_pool, w_out, g2, wr_t, br_t)


def _route_kernel(logt_ref, tri_ref, ids_ref, wts_ref, cnt_ref, carry_sc):
    @pl.when(pl.program_id(0) == 0)
    def _():
        carry_sc[...] = jnp.zeros_like(carry_sc)

    lg = logt_ref[...]
    g = [lg[r:r + 1, :] for r in range(N_GROUPS)]
    gmax = functools.reduce(jnp.maximum, g)
    gsel = jnp.full_like(g[0], N_GROUPS - 1).astype(jnp.int32)
    for r in range(N_GROUPS - 2, -1, -1):
        gsel = jnp.where(g[r] == gmax, r, gsel)
    p_group = 1.0 / functools.reduce(lambda a, b: a + b, [jnp.exp(x - gmax) for x in g])
    e = []
    for jj in range(EXPERTS_PER_GROUP):
        v = lg[N_GROUPS + jj:N_GROUPS + jj + 1, :]
        for r in range(1, N_GROUPS):
            row = N_GROUPS + r * EXPERTS_PER_GROUP + jj
            v = jnp.where(gsel == r, lg[row:row + 1, :], v)
        e.append(v)
    v1 = functools.reduce(jnp.maximum, e)
    i1 = jnp.full_like(gsel, EXPERTS_PER_GROUP - 1)
    for jj in range(EXPERTS_PER_GROUP - 2, -1, -1):
        i1 = jnp.where(e[jj] == v1, jj, i1)
    rest = [jnp.where(i1 == jj, -jnp.inf, e[jj]) for jj in range(EXPERTS_PER_GROUP)]
    v2 = functools.reduce(jnp.maximum, rest)
    i2 = jnp.full_like(gsel, EXPERTS_PER_GROUP - 1)
    for jj in range(EXPERTS_PER_GROUP - 2, -1, -1):
        i2 = jnp.where(jnp.logical_and(rest[jj] == v2, i1 != jj), jj, i2)
    t = jnp.exp(v2 - v1)
    w1 = p_group / (1.0 + t)
    w2 = p_group * t / (1.0 + t)
    lo = jnp.minimum(i1, i2)
    hi = jnp.maximum(i1, i2)
    pair = jnp.where(lo == 0, 0, jnp.where(lo == 1, 3, 5)) + hi - lo - 1
    bucket = gsel * N_PAIRS + pair
    first_is_lo = i1 < i2
    w_lo = jnp.where(first_is_lo, w1, w2)
    w_hi = jnp.where(first_is_lo, w2, w1)

    tl = lg.shape[1]
    bid = lax.broadcasted_iota(jnp.int32, (N_BUCKET_ROWS, tl), 0)
    hot = bid == bucket
    hot_b = jnp.where(hot, 1.0, 0.0).astype(BF16)
    prefix = jnp.dot(hot_b, tri_ref[...], preferred_element_type=F32)
    carry = carry_sc[...]
    before = prefix + jnp.concatenate([carry] * (tl // LANES), axis=1)
    rank = jnp.sum(jnp.where(hot, before, 0.0), axis=0, keepdims=True)
    total = carry + jnp.dot(hot_b, jnp.ones((tl, LANES), BF16), preferred_element_type=F32)
    carry_sc[...] = total
    cnt_ref[...] = total.astype(jnp.int32)
    pad_i = jnp.zeros((6, tl), jnp.int32)
    ids_ref[...] = jnp.concatenate([bucket, rank.astype(jnp.int32), pad_i], axis=0)
    wts_ref[...] = jnp.concatenate([w_lo, w_hi, pad_i.astype(F32)], axis=0)


def _route(logt, tl):
    n_rt, n = logt.shape
    tri = jnp.triu(jnp.ones((tl, tl), BF16), k=1)
    return pl.pallas_call(
        _route_kernel,
        grid=(n // tl,),
        in_specs=[pl.BlockSpec((n_rt, tl), lambda i: (0, i)),
                  pl.BlockSpec((tl, tl), lambda i: (0, 0))],
        out_specs=[pl.BlockSpec((8, tl), lambda i: (0, i)),
                   pl.BlockSpec((8, tl), lambda i: (0, i)),
                   pl.BlockSpec((N_BUCKET_ROWS, LANES), lambda i: (0, 0))],
        out_shape=[jax.ShapeDtypeStruct((8, n), jnp.int32),
                   jax.ShapeDtypeStruct((8, n), F32),
                   jax.ShapeDtypeStruct((N_BUCKET_ROWS, LANES), jnp.int32)],
        scratch_shapes=[pltpu.VMEM((N_BUCKET_ROWS, LANES), F32)],
        compiler_params=_params("arbitrary"),
        name="route",
    )(logt, tri)


def _token_copy(src_hbm, src_token, dst_buf, dst_row, sem):
    return pltpu.make_async_copy(
        src_hbm.at[pl.ds(pl.multiple_of(src_token * TOKEN_ROWS, TOKEN_ROWS), TOKEN_ROWS), :],
        dst_buf.at[pl.ds(pl.multiple_of(dst_row * TOKEN_ROWS, TOKEN_ROWS), TOKEN_ROWS), :],
        sem)


def _start_gather(idx_ref, first, rows, src_hbm, dst_buf, sem):
    def body(rr, carry):
        for uu in range(GATHER_UNROLL):
            r = rr * GATHER_UNROLL + uu
            _token_copy(src_hbm, idx_ref[first + r], dst_buf, r, sem).start(priority=uu % 2)
        return carry

    lax.fori_loop(0, rows // GATHER_UNROLL, body, 0)


def _wait_gather(rows, src_hbm, dst_buf, sem):
    pltpu.make_async_copy(src_hbm.at[pl.ds(0, rows * TOKEN_ROWS), :], dst_buf, sem).wait()


def _moe_kernel(elo_ref, ehi_ref, nused_ref, src_ref, h2c_hbm, wts_ref, g2_ref, wg_lo_ref, wu_lo_ref,
                wg_hi_ref, wu_hi_ref, wd_lo_ref, wd_hi_ref, yc_ref, xbuf, sem, *, tme):
    i = pl.program_id(0)
    n_used = nused_ref[0]
    slot = jnp.bitwise_and(i, 1)

    @pl.when(i == 0)
    def _():
        _start_gather(src_ref, 0, tme, h2c_hbm, xbuf.at[0], sem.at[0])

    @pl.when(i + 1 < n_used)
    def _():
        _start_gather(src_ref, (i + 1) * tme, tme, h2c_hbm, xbuf.at[1 - slot], sem.at[1 - slot])

    @pl.when(i < n_used)
    def _():
        _wait_gather(tme, h2c_hbm, xbuf.at[slot], sem.at[slot])
        x = _load_token_major(xbuf.at[slot], tme)
        ms = jnp.mean(x * x, axis=-1, keepdims=True)
        hn = (x * lax.rsqrt(ms + EPS) * g2_ref[...]).astype(BF16)
        wts = wts_ref[...]

        def expert_act(wg_ref, wu_ref, w):
            gate = jnp.dot(hn, wg_ref[0], preferred_element_type=F32)
            up = jnp.dot(hn, wu_ref[0], preferred_element_type=F32)
            return (gate / (1.0 + jnp.exp(-gate)) * up * w).astype(BF16)

        y = (jnp.dot(expert_act(wg_lo_ref, wu_lo_ref, wts[:, 0:1]), wd_lo_ref[0], preferred_element_type=F32)
             + jnp.dot(expert_act(wg_hi_ref, wu_hi_ref, wts[:, 1:2]), wd_hi_ref[0], preferred_element_type=F32))
        _store_token_major(yc_ref, y, tme)

    @pl.when(i >= n_used)
    def _():
        yc_ref[...] = jnp.zeros_like(yc_ref)


def _moe(elo, ehi, nused, src, h2c, wts_sorted, g2, wg, wu, wd, tme, n_tiles):
    used = lambda i, elo, ehi, nu, src: (jnp.minimum(i, nu[0] - 1), 0)
    w_lo = lambda i, elo, ehi, nu, src: (elo[i], 0, 0)
    w_hi = lambda i, elo, ehi, nu, src: (ehi[i], 0, 0)
    grid_spec = pltpu.PrefetchScalarGridSpec(
        num_scalar_prefetch=4,
        grid=(n_tiles,),
        in_specs=[
            pl.BlockSpec(memory_space=pl.ANY),
            pl.BlockSpec((tme, 2), used),
            pl.BlockSpec((1, D_MODEL), lambda i, elo, ehi, nu, src: (0, 0)),
            pl.BlockSpec((1, D_MODEL, D_EXPERT), w_lo),
            pl.BlockSpec((1, D_MODEL, D_EXPERT), w_lo),
            pl.BlockSpec((1, D_MODEL, D_EXPERT), w_hi),
            pl.BlockSpec((1, D_MODEL, D_EXPERT), w_hi),
            pl.BlockSpec((1, D_EXPERT, D_MODEL), w_lo),
            pl.BlockSpec((1, D_EXPERT, D_MODEL), w_hi),
        ],
        out_specs=pl.BlockSpec((tme * TOKEN_ROWS, LANES), lambda i, elo, ehi, nu, src: (i, 0)),
        scratch_shapes=[pltpu.VMEM((2, tme * TOKEN_ROWS, LANES), F32), pltpu.SemaphoreType.DMA((2,))],
    )
    return pl.pallas_call(
        functools.partial(_moe_kernel, tme=tme),
        grid_spec=grid_spec,
        out_shape=jax.ShapeDtypeStruct((n_tiles * tme * TOKEN_ROWS, LANES), F32),
        compiler_params=_params("arbitrary"),
        name="moe",
    )(elo, ehi, nused, src, h2c, wts_sorted, g2, wg, wu, wg, wu, wd, wd)


def _final_kernel(dest_ref, h2c_ref, yc_hbm, gf_ref, o_ref, ybuf, sem, *, tmf):
    i = pl.program_id(0)
    slot = jnp.bitwise_and(i, 1)

    @pl.when(i == 0)
    def _():
        _start_gather(dest_ref, 0, tmf, yc_hbm, ybuf.at[0], sem.at[0])

    @pl.when(i + 1 < pl.num_programs(0))
    def _():
        _start_gather(dest_ref, (i + 1) * tmf, tmf, yc_hbm, ybuf.at[1 - slot], sem.at[1 - slot])

    _wait_gather(tmf, yc_hbm, ybuf.at[slot], sem.at[slot])
    h = _load_token_major(h2c_ref, tmf) + _load_token_major(ybuf.at[slot], tmf)
    ms = jnp.mean(h * h, axis=-1, keepdims=True)
    o_ref[...] = h * lax.rsqrt(ms + EPS) * gf_ref[...]


def _final(dest, h2c, yc, gf, tmf):
    n = h2c.shape[0] // TOKEN_ROWS
    grid_spec = pltpu.PrefetchScalarGridSpec(
        num_scalar_prefetch=1,
        grid=(n // tmf,),
        in_specs=[
            pl.BlockSpec((tmf * TOKEN_ROWS, LANES), lambda i, dest: (i, 0)),
            pl.BlockSpec(memory_space=pl.ANY),
            pl.BlockSpec((1, D_MODEL), lambda i, dest: (0, 0)),
        ],
        out_specs=pl.BlockSpec((tmf, D_MODEL), lambda i, dest: (i, 0)),
        scratch_shapes=[pltpu.VMEM((2, tmf * TOKEN_ROWS, LANES), F32), pltpu.SemaphoreType.DMA((2,))],
    )
    return pl.pallas_call(
        functools.partial(_final_kernel, tmf=tmf),
        grid_spec=grid_spec,
        out_shape=jax.ShapeDtypeStruct((n, D_MODEL), F32),
        compiler_params=_params("arbitrary"),
        name="final",
    )(dest, h2c, yc, gf)


def _rope_tables(t):
    inv = 1.0 / (ROPE_THETA ** (jnp.arange(0, HEAD_DIM, 2, dtype=F32) / HEAD_DIM))
    ang = jnp.arange(t, dtype=F32)[:, None] * inv[None, :]
    ang = jnp.concatenate([ang, ang, ang, ang], axis=-1)
    sign = jnp.where((jnp.arange(LANES) % HEAD_DIM) < HEAD_DIM // 2, -1.0, 1.0).astype(F32)
    return jnp.cos(ang), jnp.sin(ang) * sign[None, :]


def kernel(x, meta, norm1_g, w_in, b_gate, lambda_q1, lambda_k1, lambda_q2, lambda_k2, subln_g, pool_w,
           pool_scale, w_attn_br, w_pool_br, w_out, norm2_g, w_router_group, b_router_group,
           w_router_expert, b_router_expert, w_e_gate, w_e_up, w_e_down, final_g):
    batch, seq, d = x.shape
    n = batch * seq
    x2 = x.reshape(n, d)
    cos, sin = _rope_tables(seq + N_META)
    w_in_b = w_in[0].astype(BF16)

    q, k, v, u, gates = _inproj(x2, norm1_g, cos[N_META:], sin[N_META:], w_in_b, b_gate, tm=1024)
    _, km, vm, um, _ = _inproj(meta, norm1_g, cos[:N_META], sin[:N_META], w_in_b, b_gate, tm=N_META)
    pad = ((0, LANES - N_META), (0, 0))
    a = _attention(q, k, v, jnp.pad(km, pad), jnp.pad(vm, pad), lambda_q1, lambda_k1, lambda_q2,
                   lambda_k2, subln_g, batch, seq, bq=256, group=8)

    n_rt = 32
    wr_t = jnp.concatenate([w_router_group[0], w_router_expert[0]], axis=1).T
    wr_t = jnp.pad(wr_t, ((0, n_rt - wr_t.shape[0]), (0, 0)))
    br_t = jnp.concatenate([b_router_group[0], b_router_expert[0]])
    br_t = jnp.pad(br_t, (0, n_rt - br_t.shape[0])).reshape(n_rt, 1)
    h2c, logt = _mix(x2, a, u, um, gates, w_attn_br[0].astype(BF16), pool_w[0].astype(BF16),
                     pool_scale, w_pool_br[0].astype(BF16), w_out[0].astype(BF16), norm2_g,
                     wr_t, br_t, seq, tm=512)
    ids, wts, cnt = _route(logt, tl=1024)

    tme = 256
    n_tiles = n // tme + N_BUCKETS
    bucket, rank = ids[0], ids[1]
    tiles_per_bucket = (cnt[:N_BUCKETS, 0] + tme - 1) // tme
    tile_end = jnp.cumsum(tiles_per_bucket)
    n_used = tile_end[-1]
    dest = (tile_end - tiles_per_bucket)[bucket] * tme + rank
    per_token = jnp.concatenate([wts[0:2], jnp.arange(n, dtype=F32)[None, :]], axis=0).T
    per_row = jnp.zeros((n_tiles * tme, 3), F32).at[dest].set(per_token)
    wts_sorted = per_row[:, 0:2]
    src = per_row[:, 2].astype(jnp.int32)
    last_tile = jnp.minimum(jnp.arange(n_tiles), n_used - 1)
    tile_bucket = jnp.sum((tile_end[None, :] <= last_tile[:, None]).astype(jnp.int32), axis=1)
    tile_bucket = jnp.minimum(tile_bucket, N_BUCKETS - 1)
    group, pair = tile_bucket // N_PAIRS, tile_bucket % N_PAIRS
    e_lo = group * EXPERTS_PER_GROUP + jnp.asarray(_PAIR_LO, jnp.int32)[pair]
    e_hi = group * EXPERTS_PER_GROUP + jnp.asarray(_PAIR_HI, jnp.int32)[pair]

    yc = _moe(e_lo, e_hi, n_used.reshape(1).astype(jnp.int32), src, h2c, wts_sorted, norm2_g,
              w_e_gate[0].astype(BF16), w_e_up[0].astype(BF16), w_e_down[0].astype(BF16), tme, n_tiles)
    out = _final(dest.astype(jnp.int32), h2c, yc, final_g.reshape(1, d), tmf=512)
    return out.reshape(batch, seq, d)
```

```python
import functools
import math

import jax
import jax.numpy as jnp
from jax import lax
from jax.experimental import pallas as pl
from jax.experimental.pallas import tpu as pltpu

D_MODEL = 1024
N_META = 16
N_HEADS = 8
HEAD_DIM = 64
V_DIM = 2 * HEAD_DIM
POOL_WINDOWS = (2, 4, 8, 16)
N_POOL_GROUPS = len(POOL_WINDOWS)
POOL_GROUP_DIM = 128
POOL_WIDTH = N_POOL_GROUPS * POOL_GROUP_DIM
ROPE_THETA = 10000.0
N_GROUPS = 4
EXPERTS_PER_GROUP = 4
N_EXPERTS = N_GROUPS * EXPERTS_PER_GROUP
D_EXPERT = 512
EPS = 1e-6
LAMBDA_INIT = 0.8 - 0.6 * math.exp(-0.3 * 0)
LOG2_E = math.log2(math.e)

_PAIR_LO = (0, 0, 0, 1, 1, 2)
_PAIR_HI = (1, 2, 3, 2, 3, 3)
N_PAIRS = len(_PAIR_LO)
N_BUCKETS = N_GROUPS * N_PAIRS
N_BUCKET_ROWS = 32
GATHER_UNROLL = 8

LANES = 128
NEG_BIG = -1e30
VMEM_LIMIT = 48 * 1024 * 1024

F32 = jnp.float32
BF16 = jnp.bfloat16

_HEADS_PER_STEP = 2
_IN_STEPS = N_HEADS // _HEADS_PER_STEP
_QKV_BLK = _HEADS_PER_STEP * V_DIM
_GATE_BLK = 2 * D_MODEL // _IN_STEPS
_O_K = D_MODEL // _QKV_BLK
_O_V = 2 * D_MODEL // _QKV_BLK
_O_U = 3 * D_MODEL // POOL_GROUP_DIM
_O_G = (3 * D_MODEL + POOL_WIDTH) // _GATE_BLK


def _params(*sem):
    return pltpu.CompilerParams(dimension_semantics=sem, vmem_limit_bytes=VMEM_LIMIT)


TOKEN_ROWS = D_MODEL // LANES


def _store_token_major(ref, val, rows):
    for s in range(TOKEN_ROWS):
        ref[pl.ds(s, rows, stride=TOKEN_ROWS), :] = val[:, s * LANES:(s + 1) * LANES]


def _load_token_major(ref, rows):
    return jnp.concatenate([ref[pl.ds(s, rows, stride=TOKEN_ROWS), :] for s in range(TOKEN_ROWS)], axis=1)


def _rope(z, cos, sin_signed, first_half):
    outs = []
    for c in range(z.shape[1] // LANES):
        zc = z[:, c * LANES:(c + 1) * LANES]
        rot = jnp.where(first_half, pltpu.roll(zc, LANES - HEAD_DIM // 2, 1),
                        pltpu.roll(zc, HEAD_DIM // 2, 1))
        outs.append(zc * cos + rot * sin_signed)
    return jnp.concatenate(outs, axis=1)


def _inproj_kernel(x_ref, g_ref, cos_ref, sin_ref, wq_ref, wk_ref, wv_ref, wu_ref, wg_ref, bg_ref,
                   q_ref, k_ref, v_ref, u_ref, gate_ref, hn_ref):
    @pl.when(pl.program_id(1) == 0)
    def _():
        x = x_ref[...]
        ms = jnp.mean(x * x, axis=-1, keepdims=True)
        hn_ref[...] = (x * lax.rsqrt(ms + EPS) * g_ref[...]).astype(BF16)

    hn = hn_ref[...]
    cos = cos_ref[...]
    sin = sin_ref[...]
    lane = lax.broadcasted_iota(jnp.int32, cos.shape, 1)
    first_half = jnp.bitwise_and(lane, HEAD_DIM - 1) < (HEAD_DIM // 2)
    zq = jnp.dot(hn, wq_ref[...], preferred_element_type=F32)
    q_ref[...] = (_rope(zq, cos, sin, first_half) * (LOG2_E / math.sqrt(HEAD_DIM))).astype(BF16)
    zk = jnp.dot(hn, wk_ref[...], preferred_element_type=F32)
    k_ref[...] = _rope(zk, cos, sin, first_half).astype(BF16)
    v_ref[...] = jnp.dot(hn, wv_ref[...], preferred_element_type=F32).astype(BF16)
    u_ref[...] = jnp.dot(hn, wu_ref[...], preferred_element_type=F32)
    zg = jnp.dot(hn, wg_ref[...], preferred_element_type=F32) + bg_ref[...]
    gate_ref[...] = (1.0 / (1.0 + jnp.exp(-zg))).astype(BF16)


def _inproj(x2, g1, cos, sin, w_in, b_gate, tm):
    n = x2.shape[0]
    n_pos_blocks = cos.shape[0] // tm
    row = lambda i, j: (i, 0)
    pos = lambda i, j: (i % n_pos_blocks, 0)
    return pl.pallas_call(
        _inproj_kernel,
        grid=(n // tm, _IN_STEPS),
        in_specs=[
            pl.BlockSpec((tm, D_MODEL), row),
            pl.BlockSpec((1, D_MODEL), lambda i, j: (0, 0)),
            pl.BlockSpec((tm, LANES), pos),
            pl.BlockSpec((tm, LANES), pos),
            pl.BlockSpec((D_MODEL, _QKV_BLK), lambda i, j: (0, j)),
            pl.BlockSpec((D_MODEL, _QKV_BLK), lambda i, j: (0, _O_K + j)),
            pl.BlockSpec((D_MODEL, _QKV_BLK), lambda i, j: (0, _O_V + j)),
            pl.BlockSpec((D_MODEL, POOL_GROUP_DIM), lambda i, j: (0, _O_U + j)),
            pl.BlockSpec((D_MODEL, _GATE_BLK), lambda i, j: (0, _O_G + j)),
            pl.BlockSpec((1, _GATE_BLK), lambda i, j: (0, j)),
        ],
        out_specs=[
            pl.BlockSpec((tm, _QKV_BLK), lambda i, j: (i, j)),
            pl.BlockSpec((tm, _QKV_BLK), lambda i, j: (i, j)),
            pl.BlockSpec((tm, _QKV_BLK), lambda i, j: (i, j)),
            pl.BlockSpec((tm, POOL_GROUP_DIM), lambda i, j: (i, j)),
            pl.BlockSpec((tm, _GATE_BLK), lambda i, j: (i, j)),
        ],
        out_shape=[
            jax.ShapeDtypeStruct((n, D_MODEL), BF16),
            jax.ShapeDtypeStruct((n, D_MODEL), BF16),
            jax.ShapeDtypeStruct((n, D_MODEL), BF16),
            jax.ShapeDtypeStruct((n, POOL_WIDTH), F32),
            jax.ShapeDtypeStruct((n, 2 * D_MODEL), BF16),
        ],
        scratch_shapes=[pltpu.VMEM((tm, D_MODEL), BF16)],
        compiler_params=_params("arbitrary", "arbitrary"),
        name="inproj",
    )(x2, g1, cos, sin, w_in, w_in, w_in, w_in, w_in, b_gate)


def _attn_kernel(lq1_ref, lk1_ref, lq2_ref, lk2_ref, subg_ref, q_ref, k_ref, v_ref, km_ref, vm_ref,
                 o_ref, vext_sc, *state_sc, bq):
    t_id = pl.program_id(2)
    nq = k_ref.shape[0] // bq
    states = [tuple(state_sc[3 * g:3 * g + 3]) for g in range(len(state_sc) // 3)]
    pairs_per_step = len(states) // 2

    @pl.when(t_id == 0)
    def _():
        vext_sc[:, 0:V_DIM] = v_ref[...]
        vext_sc[:, V_DIM:2 * V_DIM] = jnp.ones(v_ref.shape, BF16)

    r = lax.broadcasted_iota(jnp.int32, (bq, bq + LANES), 0)
    c = lax.broadcasted_iota(jnp.int32, (bq, bq + LANES), 1)
    visible = jnp.logical_or(c <= r, jnp.logical_and(c >= bq, c < bq + N_META))
    bias = jnp.where(visible, 0.0, NEG_BIG)
    bias = jnp.concatenate([bias, bias], axis=0)
    vm_ext = jnp.concatenate([vm_ref[...], jnp.ones((LANES, V_DIM), BF16)], axis=1)
    lam = (jnp.exp(jnp.sum(lq1_ref[...] * lk1_ref[...], axis=-1, keepdims=True))
           - jnp.exp(jnp.sum(lq2_ref[...] * lk2_ref[...], axis=-1, keepdims=True)) + LAMBDA_INIT)
    lane = lax.broadcasted_iota(jnp.int32, (bq, V_DIM), 1)

    def load_q(rows, qq_sc):
        q = q_ref[rows, :]
        zero = jnp.zeros_like(q)
        qq_sc[0:bq, :] = jnp.where(lane < HEAD_DIM, q, zero)
        qq_sc[bq:2 * bq, :] = jnp.where(lane >= HEAD_DIM, q, zero)

    def step(state, kb, vb, first):
        qq_sc, m_sc, acc_sc = state
        s = lax.dot_general(qq_sc[...], kb, (((1,), (1,)), ((), ())), preferred_element_type=F32)
        if first:
            s = s + bias
        n_tiles = s.shape[1] // LANES
        rm = s[:, 0:LANES]
        for t in range(1, n_tiles):
            rm = jnp.maximum(rm, s[:, t * LANES:(t + 1) * LANES])
        rmax = jnp.max(rm, axis=-1, keepdims=True)
        if first:
            m_new = jnp.broadcast_to(rmax, (2 * bq, LANES))
        else:
            m_prev = m_sc[...]
            m_new = jnp.maximum(m_prev, rmax)
        p = jnp.exp2(s - jnp.concatenate([m_new] * n_tiles, axis=1))
        pv = jnp.dot(p.astype(BF16), vb, preferred_element_type=F32)
        if first:
            acc_sc[...] = pv
        else:
            alpha = jnp.exp2(m_prev - m_new)
            acc_sc[...] = jnp.concatenate([alpha, alpha], axis=1) * acc_sc[...] + pv
        m_sc[...] = m_new

    def key_step(state, n_full, j):
        if j == 0:
            diag = slice(n_full * bq, (n_full + 1) * bq)
            step(state, jnp.concatenate([k_ref[diag, :], km_ref[...]], axis=0),
                 jnp.concatenate([vext_sc[diag, :], vm_ext], axis=0), True)
        else:
            blk = slice((j - 1) * bq, j * bq)
            step(state, k_ref[blk, :], vext_sc[blk, :], False)

    def finish(rows, acc_sc):
        o1 = acc_sc[0:bq, 0:V_DIM] / acc_sc[0:bq, V_DIM:2 * V_DIM]
        o2 = acc_sc[bq:2 * bq, 0:V_DIM] / acc_sc[bq:2 * bq, V_DIM:2 * V_DIM]
        o = o1 - lam * o2
        ms = jnp.mean(o * o, axis=-1, keepdims=True)
        o_ref[rows, :] = (o * lax.rsqrt(ms + EPS) * subg_ref[...] * (1.0 - LAMBDA_INIT)).astype(BF16)

    def chains(t):
        blocks = []
        for i in range(pairs_per_step):
            p = t * pairs_per_step + i
            blocks += [(p, states[2 * i]), (nq - 1 - p, states[2 * i + 1])]
        for n_full, state in blocks:
            load_q(slice(n_full * bq, (n_full + 1) * bq), state[0])
        for j in range(nq):
            for n_full, state in blocks:
                if j <= n_full:
                    key_step(state, n_full, j)
        for n_full, state in blocks:
            finish(slice(n_full * bq, (n_full + 1) * bq), state[2])

    n_steps = nq // (2 * pairs_per_step)
    if n_steps == 1:
        chains(0)
    else:
        for t in range(n_steps):
            pl.when(t_id == t)(functools.partial(chains, t))


def _attention(q, k, v, km, vm, lq1, lk1, lq2, lk2, subg, batch, seq, bq, group):
    nq = seq // bq
    small = lambda b, h, i: (0, 0)
    head = lambda b, h, i: (b, h)
    state_scratch = [
        pltpu.VMEM((2 * bq, V_DIM), BF16),
        pltpu.VMEM((2 * bq, LANES), F32),
        pltpu.VMEM((2 * bq, 2 * V_DIM), F32),
    ]
    return pl.pallas_call(
        functools.partial(_attn_kernel, bq=bq),
        grid=(batch, N_HEADS, nq // group),
        in_specs=[
            pl.BlockSpec((1, HEAD_DIM), small),
            pl.BlockSpec((1, HEAD_DIM), small),
            pl.BlockSpec((1, HEAD_DIM), small),
            pl.BlockSpec((1, HEAD_DIM), small),
            pl.BlockSpec((1, V_DIM), small),
            pl.BlockSpec((seq, V_DIM), head),
            pl.BlockSpec((seq, V_DIM), head),
            pl.BlockSpec((seq, V_DIM), head),
            pl.BlockSpec((LANES, V_DIM), lambda b, h, i: (0, h)),
            pl.BlockSpec((LANES, V_DIM), lambda b, h, i: (0, h)),
        ],
        out_specs=pl.BlockSpec((seq, V_DIM), head),
        out_shape=jax.ShapeDtypeStruct(q.shape, BF16),
        scratch_shapes=[pltpu.VMEM((seq, 2 * V_DIM), BF16)] + state_scratch * group,
        compiler_params=_params("arbitrary", "arbitrary", "arbitrary"),
        name="diff_attn",
    )(lq1, lk1, lq2, lk2, subg, q, k, v, km, vm)


def _mix_kernel(x_ref, a_ref, u_ref, uprev_ref, umeta_ref, gate_ref, wa_ref, pw_ref, ps_ref, wp_ref,
                wo_ref, g2_ref, wr_ref, br_ref, h2c_ref, logt_ref, ext_sc, *, tm, tiles_per_seq):
    i = pl.program_id(0)
    first = (i % tiles_per_seq) == 0

    @pl.when(first)
    def _():
        ext_sc[0:N_META, :] = umeta_ref[...]

    @pl.when(jnp.logical_not(first))
    def _():
        ext_sc[0:N_META, :] = uprev_ref[...]

    ext_sc[N_META:N_META + tm, :] = u_ref[...]

    pooled = []
    for g, w in enumerate(POOL_WINDOWS):
        cs = slice(g * POOL_GROUP_DIM, (g + 1) * POOL_GROUP_DIM)
        cur = ext_sc[N_META:N_META + tm, cs]
        tot = cur
        for kk in range(1, w):
            tot = tot + ext_sc[N_META - kk:N_META - kk + tm, cs]
        y = (tot * (1.0 / w) - cur).astype(BF16)
        pooled.append(jnp.dot(y, pw_ref[g], preferred_element_type=F32))
    yp = (jnp.concatenate(pooled, axis=1) * ps_ref[...]).astype(BF16)
    y_pool = jnp.dot(yp, wp_ref[...], preferred_element_type=F32)
    y_attn = jnp.dot(a_ref[...], wa_ref[...], preferred_element_type=F32)
    mixed = (gate_ref[:, 0:D_MODEL].astype(F32) * y_attn
             + gate_ref[:, D_MODEL:2 * D_MODEL].astype(F32) * y_pool)
    h2 = x_ref[...] + jnp.dot(mixed.astype(BF16), wo_ref[...], preferred_element_type=F32)
    _store_token_major(h2c_ref, h2, tm)
    ms = jnp.mean(h2 * h2, axis=-1, keepdims=True)
    hn2 = h2 * lax.rsqrt(ms + EPS) * g2_ref[...]
    hn2_hi = hn2.astype(BF16)
    hn2_lo = (hn2 - hn2_hi.astype(F32)).astype(BF16)
    wr = wr_ref[...]
    wr_hi = wr.astype(BF16)
    wr_lo = (wr - wr_hi.astype(F32)).astype(BF16)
    nt = (((1,), (1,)), ((), ()))
    logt = (lax.dot_general(wr_hi, hn2_hi, nt, preferred_element_type=F32)
            + lax.dot_general(wr_hi, hn2_lo, nt, preferred_element_type=F32)
            + lax.dot_general(wr_lo, hn2_hi, nt, preferred_element_type=F32))
    logt_ref[...] = logt + br_ref[...]


def _mix(x2, a, u, umeta, gates, w_attn, pool_w, pool_scale, w_pool, w_out, g2, wr_t, br_t, seq, tm):
    n = x2.shape[0]
    tiles_per_seq = seq // tm
    halo_blocks = tm // N_META
    const2 = lambda i: (0, 0)
    n_rt = wr_t.shape[0]
    return pl.pallas_call(
        functools.partial(_mix_kernel, tm=tm, tiles_per_seq=tiles_per_seq),
        grid=(n // tm,),
        in_specs=[
            pl.BlockSpec((tm, D_MODEL), lambda i: (i, 0)),
            pl.BlockSpec((tm, D_MODEL), lambda i: (i, 0)),
            pl.BlockSpec((tm, POOL_WIDTH), lambda i: (i, 0)),
            pl.BlockSpec((N_META, POOL_WIDTH), lambda i: (jnp.maximum(i * halo_blocks - 1, 0), 0)),
            pl.BlockSpec((N_META, POOL_WIDTH), const2),
            pl.BlockSpec((tm, 2 * D_MODEL), lambda i: (i, 0)),
            pl.BlockSpec((D_MODEL, D_MODEL), const2),
            pl.BlockSpec((N_POOL_GROUPS, POOL_GROUP_DIM, POOL_GROUP_DIM), lambda i: (0, 0, 0)),
            pl.BlockSpec((1, POOL_WIDTH), const2),
            pl.BlockSpec((POOL_WIDTH, D_MODEL), const2),
            pl.BlockSpec((D_MODEL, D_MODEL), const2),
            pl.BlockSpec((1, D_MODEL), const2),
            pl.BlockSpec((n_rt, D_MODEL), const2),
            pl.BlockSpec((n_rt, 1), const2),
        ],
        out_specs=[
            pl.BlockSpec((tm * TOKEN_ROWS, LANES), lambda i: (i, 0)),
            pl.BlockSpec((n_rt, tm), lambda i: (0, i)),
        ],
        out_shape=[
            jax.ShapeDtypeStruct((n * TOKEN_ROWS, LANES), F32),
            jax.ShapeDtypeStruct((n_rt, n), F32),
        ],
        scratch_shapes=[pltpu.VMEM((tm + N_META, POOL_WIDTH), F32)],
        compiler_params=_params("arbitrary"),
        name="mix",
    )(x2, a, u, u, umeta, gates, w_attn, pool_w, pool_scale, w_pool, w_out, g2, wr_t, br_t)


def _route_kernel(logt_ref, tri_ref, ids_ref, wts_ref, cnt_ref, carry_sc):
    @pl.when(pl.program_id(0) == 0)
    def _():
        carry_sc[...] = jnp.zeros_like(carry_sc)

    lg = logt_ref[...]
    g = [lg[r:r + 1, :] for r in range(N_GROUPS)]
    gmax = functools.reduce(jnp.maximum, g)
    gsel = jnp.full_like(g[0], N_GROUPS - 1).astype(jnp.int32)
    for r in range(N_GROUPS - 2, -1, -1):
        gsel = jnp.where(g[r] == gmax, r, gsel)
    p_group = 1.0 / functools.reduce(lambda a, b: a + b, [jnp.exp(x - gmax) for x in g])
    e = []
    for jj in range(EXPERTS_PER_GROUP):
        v = lg[N_GROUPS + jj:N_GROUPS + jj + 1, :]
        for r in range(1, N_GROUPS):
            row = N_GROUPS + r * EXPERTS_PER_GROUP + jj
            v = jnp.where(gsel == r, lg[row:row + 1, :], v)
        e.append(v)
    v1 = functools.reduce(jnp.maximum, e)
    i1 = jnp.full_like(gsel, EXPERTS_PER_GROUP - 1)
    for jj in range(EXPERTS_PER_GROUP - 2, -1, -1):
        i1 = jnp.where(e[jj] == v1, jj, i1)
    rest = [jnp.where(i1 == jj, -jnp.inf, e[jj]) for jj in range(EXPERTS_PER_GROUP)]
    v2 = functools.reduce(jnp.maximum, rest)
    i2 = jnp.full_like(gsel, EXPERTS_PER_GROUP - 1)
    for jj in range(EXPERTS_PER_GROUP - 2, -1, -1):
        i2 = jnp.where(jnp.logical_and(rest[jj] == v2, i1 != jj), jj, i2)
    t = jnp.exp(v2 - v1)
    w1 = p_group / (1.0 + t)
    w2 = p_group * t / (1.0 + t)
    lo = jnp.minimum(i1, i2)
    hi = jnp.maximum(i1, i2)
    pair = jnp.where(lo == 0, 0, jnp.where(lo == 1, 3, 5)) + hi - lo - 1
    bucket = gsel * N_PAIRS + pair
    first_is_lo = i1 < i2
    w_lo = jnp.where(first_is_lo, w1, w2)
    w_hi = jnp.where(first_is_lo, w2, w1)

    tl = lg.shape[1]
    bid = lax.broadcasted_iota(jnp.int32, (N_BUCKET_ROWS, tl), 0)
    hot = bid == bucket
    hot_b = jnp.where(hot, 1.0, 0.0).astype(BF16)
    prefix = jnp.dot(hot_b, tri_ref[...], preferred_element_type=F32)
    carry = carry_sc[...]
    before = prefix + jnp.concatenate([carry] * (tl // LANES), axis=1)
    rank = jnp.sum(jnp.where(hot, before, 0.0), axis=0, keepdims=True)
    total = carry + jnp.dot(hot_b, jnp.ones((tl, LANES), BF16), preferred_element_type=F32)
    carry_sc[...] = total
    cnt_ref[...] = total.astype(jnp.int32)
    pad_i = jnp.zeros((6, tl), jnp.int32)
    ids_ref[...] = jnp.concatenate([bucket, rank.astype(jnp.int32), pad_i], axis=0)
    wts_ref[...] = jnp.concatenate([w_lo, w_hi, pad_i.astype(F32)], axis=0)


def _route(logt, tl):
    n_rt, n = logt.shape
    tri = jnp.triu(jnp.ones((tl, tl), BF16), k=1)
    return pl.pallas_call(
        _route_kernel,
        grid=(n // tl,),
        in_specs=[pl.BlockSpec((n_rt, tl), lambda i: (0, i)),
                  pl.BlockSpec((tl, tl), lambda i: (0, 0))],
        out_specs=[pl.BlockSpec((8, tl), lambda i: (0, i)),
                   pl.BlockSpec((8, tl), lambda i: (0, i)),
                   pl.BlockSpec((N_BUCKET_ROWS, LANES), lambda i: (0, 0))],
        out_shape=[jax.ShapeDtypeStruct((8, n), jnp.int32),
                   jax.ShapeDtypeStruct((8, n), F32),
                   jax.ShapeDtypeStruct((N_BUCKET_ROWS, LANES), jnp.int32)],
        scratch_shapes=[pltpu.VMEM((N_BUCKET_ROWS, LANES), F32)],
        compiler_params=_params("arbitrary"),
        name="route",
    )(logt, tri)


def _token_copy(src_hbm, src_token, dst_buf, dst_row, sem):
    return pltpu.make_async_copy(
        src_hbm.at[pl.ds(pl.multiple_of(src_token * TOKEN_ROWS, TOKEN_ROWS), TOKEN_ROWS), :],
        dst_buf.at[pl.ds(pl.multiple_of(dst_row * TOKEN_ROWS, TOKEN_ROWS), TOKEN_ROWS), :],
        sem)


def _start_gather(idx_ref, first, row0, rows, src_hbm, dst_buf, sem):
    def body(rr, carry):
        for uu in range(GATHER_UNROLL):
            r = row0 + rr * GATHER_UNROLL + uu
            _token_copy(src_hbm, idx_ref[first + r], dst_buf, r, sem).start(priority=uu % 2)
        return carry

    lax.fori_loop(0, rows // GATHER_UNROLL, body, 0)


def _wait_gather(rows, src_hbm, dst_buf, sem):
    pltpu.make_async_copy(src_hbm.at[pl.ds(0, rows * TOKEN_ROWS), :], dst_buf, sem).wait()


def _moe_kernel(elo_ref, ehi_ref, nused_ref, src_ref, h2c_hbm, wts_ref, g2_ref, wg_lo_ref, wu_lo_ref,
                wg_hi_ref, wu_hi_ref, wd_lo_ref, wd_hi_ref, yc_ref, xbuf, hn_sc, act_sc, sem, *, tme):
    i = pl.program_id(0)
    n_used = nused_ref[0]
    slot = jnp.bitwise_and(i, 1)

    @pl.when(i == 0)
    def _():
        _start_gather(src_ref, 0, 0, tme, h2c_hbm, xbuf.at[0], sem.at[0])

    n_stages = 4
    share = tme // n_stages

    def stage(k, compute):
        @pl.when(i < n_used)
        def _():
            @pl.when(i + 1 < n_used)
            def _():
                _start_gather(src_ref, (i + 1) * tme, k * share, share, h2c_hbm, xbuf.at[1 - slot],
                              sem.at[1 - slot])
            compute()

    def norm():
        _wait_gather(tme, h2c_hbm, xbuf.at[slot], sem.at[slot])
        x = _load_token_major(xbuf.at[slot], tme)
        ms = jnp.mean(x * x, axis=-1, keepdims=True)
        hn_sc[...] = (x * lax.rsqrt(ms + EPS) * g2_ref[...]).astype(BF16)

    def expert_act(wg_ref, wu_ref, col):
        hn = hn_sc[...]
        gate = jnp.dot(hn, wg_ref[0], preferred_element_type=F32)
        up = jnp.dot(hn, wu_ref[0], preferred_element_type=F32)
        w = wts_ref[:, col:col + 1]
        act_sc[:, col * D_EXPERT:(col + 1) * D_EXPERT] = (
            gate / (1.0 + jnp.exp(-gate)) * up * w).astype(BF16)

    def down():
        y = (jnp.dot(act_sc[:, 0:D_EXPERT], wd_lo_ref[0], preferred_element_type=F32)
             + jnp.dot(act_sc[:, D_EXPERT:2 * D_EXPERT], wd_hi_ref[0], preferred_element_type=F32))
        _store_token_major(yc_ref, y, tme)

    stage(0, norm)
    stage(1, functools.partial(expert_act, wg_lo_ref, wu_lo_ref, 0))
    stage(2, functools.partial(expert_act, wg_hi_ref, wu_hi_ref, 1))
    stage(3, down)

    @pl.when(i >= n_used)
    def _():
        yc_ref[...] = jnp.zeros_like(yc_ref)


def _moe(elo, ehi, nused, src, h2c, wts_sorted, g2, wg, wu, wd, tme, n_tiles):
    used = lambda i, elo, ehi, nu, src: (jnp.minimum(i, nu[0] - 1), 0)
    w_lo = lambda i, elo, ehi, nu, src: (elo[i], 0, 0)
    w_hi = lambda i, elo, ehi, nu, src: (ehi[i], 0, 0)
    grid_spec = pltpu.PrefetchScalarGridSpec(
        num_scalar_prefetch=4,
        grid=(n_tiles,),
        in_specs=[
            pl.BlockSpec(memory_space=pl.ANY),
            pl.BlockSpec((tme, 2), used),
            pl.BlockSpec((1, D_MODEL), lambda i, elo, ehi, nu, src: (0, 0)),
            pl.BlockSpec((1, D_MODEL, D_EXPERT), w_lo),
            pl.BlockSpec((1, D_MODEL, D_EXPERT), w_lo),
            pl.BlockSpec((1, D_MODEL, D_EXPERT), w_hi),
            pl.BlockSpec((1, D_MODEL, D_EXPERT), w_hi),
            pl.BlockSpec((1, D_EXPERT, D_MODEL), w_lo),
            pl.BlockSpec((1, D_EXPERT, D_MODEL), w_hi),
        ],
        out_specs=pl.BlockSpec((tme * TOKEN_ROWS, LANES), lambda i, elo, ehi, nu, src: (i, 0)),
        scratch_shapes=[pltpu.VMEM((2, tme * TOKEN_ROWS, LANES), F32),
                        pltpu.VMEM((tme, D_MODEL), BF16),
                        pltpu.VMEM((tme, 2 * D_EXPERT), BF16),
                        pltpu.SemaphoreType.DMA((2,))],
    )
    return pl.pallas_call(
        functools.partial(_moe_kernel, tme=tme),
        grid_spec=grid_spec,
        out_shape=jax.ShapeDtypeStruct((n_tiles * tme * TOKEN_ROWS, LANES), F32),
        compiler_params=_params("arbitrary"),
        name="moe",
    )(elo, ehi, nused, src, h2c, wts_sorted, g2, wg, wu, wg, wu, wd, wd)


def _final_kernel(dest_ref, h2c_ref, yc_hbm, gf_ref, o_ref, ybuf, sem, *, tmf):
    i = pl.program_id(0)
    slot = jnp.bitwise_and(i, 1)

    @pl.when(i == 0)
    def _():
        _start_gather(dest_ref, 0, 0, tmf, yc_hbm, ybuf.at[0], sem.at[0])

    @pl.when(i + 1 < pl.num_programs(0))
    def _():
        _start_gather(dest_ref, (i + 1) * tmf, 0, tmf, yc_hbm, ybuf.at[1 - slot], sem.at[1 - slot])

    _wait_gather(tmf, yc_hbm, ybuf.at[slot], sem.at[slot])
    h = _load_token_major(h2c_ref, tmf) + _load_token_major(ybuf.at[slot], tmf)
    ms = jnp.mean(h * h, axis=-1, keepdims=True)
    o_ref[...] = h * lax.rsqrt(ms + EPS) * gf_ref[...]


def _final(dest, h2c, yc, gf, tmf):
    n = h2c.shape[0] // TOKEN_ROWS
    grid_spec = pltpu.PrefetchScalarGridSpec(
        num_scalar_prefetch=1,
        grid=(n // tmf,),
        in_specs=[
            pl.BlockSpec((tmf * TOKEN_ROWS, LANES), lambda i, dest: (i, 0)),
            pl.BlockSpec(memory_space=pl.ANY),
            pl.BlockSpec((1, D_MODEL), lambda i, dest: (0, 0)),
        ],
        out_specs=pl.BlockSpec((tmf, D_MODEL), lambda i, dest: (i, 0)),
        scratch_shapes=[pltpu.VMEM((2, tmf * TOKEN_ROWS, LANES), F32), pltpu.SemaphoreType.DMA((2,))],
    )
    return pl.pallas_call(
        functools.partial(_final_kernel, tmf=tmf),
        grid_spec=grid_spec,
        out_shape=jax.ShapeDtypeStruct((n, D_MODEL), F32),
        compiler_params=_params("arbitrary"),
        name="final",
    )(dest, h2c, yc, gf)


def _rope_tables(t):
    inv = 1.0 / (ROPE_THETA ** (jnp.arange(0, HEAD_DIM, 2, dtype=F32) / HEAD_DIM))
    ang = jnp.arange(t, dtype=F32)[:, None] * inv[None, :]
    ang = jnp.concatenate([ang, ang, ang, ang], axis=-1)
    sign = jnp.where((jnp.arange(LANES) % HEAD_DIM) < HEAD_DIM // 2, -1.0, 1.0).astype(F32)
    return jnp.cos(ang), jnp.sin(ang) * sign[None, :]


def kernel(x, meta, norm1_g, w_in, b_gate, lambda_q1, lambda_k1, lambda_q2, lambda_k2, subln_g, pool_w,
           pool_scale, w_attn_br, w_pool_br, w_out, norm2_g, w_router_group, b_router_group,
           w_router_expert, b_router_expert, w_e_gate, w_e_up, w_e_down, final_g):
    batch, seq, d = x.shape
    n = batch * seq
    x2 = x.reshape(n, d)
    cos, sin = _rope_tables(seq + N_META)
    w_in_b = w_in[0].astype(BF16)

    q, k, v, u, gates = _inproj(x2, norm1_g, cos[N_META:], sin[N_META:], w_in_b, b_gate, tm=1024)
    _, km, vm, um, _ = _inproj(meta, norm1_g, cos[:N_META], sin[:N_META], w_in_b, b_gate, tm=N_META)
    pad = ((0, LANES - N_META), (0, 0))
    a = _attention(q, k, v, jnp.pad(km, pad), jnp.pad(vm, pad), lambda_q1, lambda_k1, lambda_q2,
                   lambda_k2, subln_g, batch, seq, bq=256, group=8)

    n_rt = 32
    wr_t = jnp.concatenate([w_router_group[0], w_router_expert[0]], axis=1).T
    wr_t = jnp.pad(wr_t, ((0, n_rt - wr_t.shape[0]), (0, 0)))
    br_t = jnp.concatenate([b_router_group[0], b_router_expert[0]])
    br_t = jnp.pad(br_t, (0, n_rt - br_t.shape[0])).reshape(n_rt, 1)
    h2c, logt = _mix(x2, a, u, um, gates, w_attn_br[0].astype(BF16), pool_w[0].astype(BF16),
                     pool_scale, w_pool_br[0].astype(BF16), w_out[0].astype(BF16), norm2_g,
                     wr_t, br_t, seq, tm=512)
    ids, wts, cnt = _route(logt, tl=1024)

    tme = 256
    n_tiles = n // tme + N_BUCKETS
    bucket, rank = ids[0], ids[1]
    tiles_per_bucket = (cnt[:N_BUCKETS, 0] + tme - 1) // tme
    tile_end = jnp.cumsum(tiles_per_bucket)
    n_used = tile_end[-1]
    dest = (tile_end - tiles_per_bucket)[bucket] * tme + rank
    per_token = jnp.concatenate([wts[0:2], jnp.arange(n, dtype=F32)[None, :]], axis=0).T
    per_row = jnp.zeros((n_tiles * tme, 3), F32).at[dest].set(per_token)
    wts_sorted = per_row[:, 0:2]
    src = per_row[:, 2].astype(jnp.int32)
    last_tile = jnp.minimum(jnp.arange(n_tiles), n_used - 1)
    tile_bucket = jnp.sum((tile_end[None, :] <= last_tile[:, None]).astype(jnp.int32), axis=1)
    tile_bucket = jnp.minimum(tile_bucket, N_BUCKETS - 1)
    group, pair = tile_bucket // N_PAIRS, tile_bucket % N_PAIRS
    e_lo = group * EXPERTS_PER_GROUP + jnp.asarray(_PAIR_LO, jnp.int32)[pair]
    e_hi = group * EXPERTS_PER_GROUP + jnp.asarray(_PAIR_HI, jnp.int32)[pair]

    yc = _moe(e_lo, e_hi, n_used.reshape(1).astype(jnp.int32), src, h2c, wts_sorted, norm2_g,
              w_e_gate[0].astype(BF16), w_e_up[0].astype(BF16), w_e_down[0].astype(BF16), tme, n_tiles)
    out = _final(dest.astype(jnp.int32), h2c, yc, final_g.reshape(1, d), tmf=512)
    return out.reshape(batch, seq, d)
```

```python
import functools
import math

import jax
import jax.numpy as jnp
from jax import lax
from jax.experimental import pallas as pl
from jax.experimental.pallas import tpu as pltpu

D_MODEL = 1024
N_META = 16
N_HEADS = 8
HEAD_DIM = 64
V_DIM = 2 * HEAD_DIM
POOL_WINDOWS = (2, 4, 8, 16)
N_POOL_GROUPS = len(POOL_WINDOWS)
POOL_GROUP_DIM = 128
POOL_WIDTH = N_POOL_GROUPS * POOL_GROUP_DIM
ROPE_THETA = 10000.0
N_GROUPS = 4
EXPERTS_PER_GROUP = 4
N_EXPERTS = N_GROUPS * EXPERTS_PER_GROUP
D_EXPERT = 512
EPS = 1e-6
LAMBDA_INIT = 0.8 - 0.6 * math.exp(-0.3 * 0)
LOG2_E = math.log2(math.e)

_PAIR_A = (0, 0, 0, 1, 1, 3)
_PAIR_B = (1, 2, 3, 3, 2, 2)
N_PAIRS = len(_PAIR_A)
N_BUCKETS = N_GROUPS * N_PAIRS
N_BUCKET_ROWS = 32
GATHER_UNROLL = 8

LANES = 128
NEG_BIG = -1e30
VMEM_LIMIT = 48 * 1024 * 1024

F32 = jnp.float32
BF16 = jnp.bfloat16

_HEADS_PER_STEP = 2
_IN_STEPS = N_HEADS // _HEADS_PER_STEP
_QKV_BLK = _HEADS_PER_STEP * V_DIM
_GATE_BLK = 2 * D_MODEL // _IN_STEPS
_O_K = D_MODEL // _QKV_BLK
_O_V = 2 * D_MODEL // _QKV_BLK
_O_U = 3 * D_MODEL // POOL_GROUP_DIM
_O_G = (3 * D_MODEL + POOL_WIDTH) // _GATE_BLK


def _params(*sem):
    return pltpu.CompilerParams(dimension_semantics=sem, vmem_limit_bytes=VMEM_LIMIT)


TOKEN_ROWS = D_MODEL // LANES


def _store_token_major(ref, val, rows):
    for s in range(TOKEN_ROWS):
        ref[pl.ds(s, rows, stride=TOKEN_ROWS), :] = val[:, s * LANES:(s + 1) * LANES]


def _load_token_major(ref, rows):
    return jnp.concatenate([ref[pl.ds(s, rows, stride=TOKEN_ROWS), :] for s in range(TOKEN_ROWS)], axis=1)


def _rope(z, cos, sin_signed, first_half):
    outs = []
    for c in range(z.shape[1] // LANES):
        zc = z[:, c * LANES:(c + 1) * LANES]
        rot = jnp.where(first_half, pltpu.roll(zc, LANES - HEAD_DIM // 2, 1),
                        pltpu.roll(zc, HEAD_DIM // 2, 1))
        outs.append(zc * cos + rot * sin_signed)
    return jnp.concatenate(outs, axis=1)


def _inproj_kernel(x_ref, g_ref, cos_ref, sin_ref, wq_ref, wk_ref, wv_ref, wu_ref, wg_ref, bg_ref,
                   q_ref, k_ref, v_ref, u_ref, gate_ref, wq_sc, wk_sc, wv_sc, wu_sc, wg_sc):
    @pl.when(pl.program_id(1) == 0)
    def _():
        for src, dst in ((wq_ref, wq_sc), (wk_ref, wk_sc), (wv_ref, wv_sc), (wu_ref, wu_sc),
                         (wg_ref, wg_sc)):
            dst[...] = src[...].astype(BF16)

    x = x_ref[...]
    ms = jnp.mean(x * x, axis=-1, keepdims=True)
    hn = (x * lax.rsqrt(ms + EPS) * g_ref[...]).astype(BF16)
    cos = cos_ref[...]
    sin = sin_ref[...]
    lane = lax.broadcasted_iota(jnp.int32, cos.shape, 1)
    first_half = jnp.bitwise_and(lane, HEAD_DIM - 1) < (HEAD_DIM // 2)
    zq = jnp.dot(hn, wq_sc[...], preferred_element_type=F32)
    q_ref[...] = (_rope(zq, cos, sin, first_half) * (LOG2_E / math.sqrt(HEAD_DIM))).astype(BF16)
    zk = jnp.dot(hn, wk_sc[...], preferred_element_type=F32)
    k_ref[...] = _rope(zk, cos, sin, first_half).astype(BF16)
    v_ref[...] = jnp.dot(hn, wv_sc[...], preferred_element_type=F32).astype(BF16)
    u_ref[...] = jnp.dot(hn, wu_sc[...], preferred_element_type=F32)
    zg = jnp.dot(hn, wg_sc[...], preferred_element_type=F32) + bg_ref[...]
    gate_ref[...] = (1.0 / (1.0 + jnp.exp(-zg))).astype(BF16)


def _inproj(x2, g1, cos, sin, w_in, b_gate, tm):
    n = x2.shape[0]
    n_pos_blocks = cos.shape[0] // tm
    row = lambda j, i: (i, 0)
    pos = lambda j, i: (i % n_pos_blocks, 0)
    return pl.pallas_call(
        _inproj_kernel,
        grid=(_IN_STEPS, n // tm),
        in_specs=[
            pl.BlockSpec((tm, D_MODEL), row),
            pl.BlockSpec((1, D_MODEL), lambda j, i: (0, 0)),
            pl.BlockSpec((tm, LANES), pos),
            pl.BlockSpec((tm, LANES), pos),
            pl.BlockSpec((D_MODEL, _QKV_BLK), lambda j, i: (0, j)),
            pl.BlockSpec((D_MODEL, _QKV_BLK), lambda j, i: (0, _O_K + j)),
            pl.BlockSpec((D_MODEL, _QKV_BLK), lambda j, i: (0, _O_V + j)),
            pl.BlockSpec((D_MODEL, POOL_GROUP_DIM), lambda j, i: (0, _O_U + j)),
            pl.BlockSpec((D_MODEL, _GATE_BLK), lambda j, i: (0, _O_G + j)),
            pl.BlockSpec((1, _GATE_BLK), lambda j, i: (0, j)),
        ],
        out_specs=[
            pl.BlockSpec((tm, _QKV_BLK), lambda j, i: (i, j)),
            pl.BlockSpec((tm, _QKV_BLK), lambda j, i: (i, j)),
            pl.BlockSpec((tm, _QKV_BLK), lambda j, i: (i, j)),
            pl.BlockSpec((tm, POOL_GROUP_DIM), lambda j, i: (i, j)),
            pl.BlockSpec((tm, _GATE_BLK), lambda j, i: (i, j)),
        ],
        out_shape=[
            jax.ShapeDtypeStruct((n, D_MODEL), BF16),
            jax.ShapeDtypeStruct((n, D_MODEL), BF16),
            jax.ShapeDtypeStruct((n, D_MODEL), BF16),
            jax.ShapeDtypeStruct((n, POOL_WIDTH), F32),
            jax.ShapeDtypeStruct((n, 2 * D_MODEL), BF16),
        ],
        scratch_shapes=[
            pltpu.VMEM((D_MODEL, _QKV_BLK), BF16),
            pltpu.VMEM((D_MODEL, _QKV_BLK), BF16),
            pltpu.VMEM((D_MODEL, _QKV_BLK), BF16),
            pltpu.VMEM((D_MODEL, POOL_GROUP_DIM), BF16),
            pltpu.VMEM((D_MODEL, _GATE_BLK), BF16),
        ],
        compiler_params=_params("arbitrary", "arbitrary"),
        name="inproj",
    )(x2, g1, cos, sin, w_in, w_in, w_in, w_in, w_in, b_gate)


def _attn_kernel(lq1_ref, lk1_ref, lq2_ref, lk2_ref, subg_ref, q_ref, k_ref, v_ref, km_ref, vm_ref,
                 o_ref, vext_sc, *state_sc, bq):
    t_id = pl.program_id(2)
    nq = k_ref.shape[0] // bq
    states = [tuple(state_sc[3 * g:3 * g + 3]) for g in range(len(state_sc) // 3)]
    pairs_per_step = len(states) // 2

    @pl.when(t_id == 0)
    def _():
        vext_sc[:, 0:V_DIM] = v_ref[...]
        vext_sc[:, V_DIM:2 * V_DIM] = jnp.ones(v_ref.shape, BF16)

    r = lax.broadcasted_iota(jnp.int32, (bq, bq + LANES), 0)
    c = lax.broadcasted_iota(jnp.int32, (bq, bq + LANES), 1)
    visible = jnp.logical_or(c <= r, jnp.logical_and(c >= bq, c < bq + N_META))
    bias = jnp.where(visible, 0.0, NEG_BIG)
    bias = jnp.concatenate([bias, bias], axis=0)
    vm_ext = jnp.concatenate([vm_ref[...], jnp.ones((LANES, V_DIM), BF16)], axis=1)
    lam = (jnp.exp(jnp.sum(lq1_ref[...] * lk1_ref[...], axis=-1, keepdims=True))
           - jnp.exp(jnp.sum(lq2_ref[...] * lk2_ref[...], axis=-1, keepdims=True)) + LAMBDA_INIT)
    lane = lax.broadcasted_iota(jnp.int32, (bq, V_DIM), 1)

    def load_q(rows, qq_sc):
        q = q_ref[rows, :]
        zero = jnp.zeros_like(q)
        qq_sc[0:bq, :] = jnp.where(lane < HEAD_DIM, q, zero)
        qq_sc[bq:2 * bq, :] = jnp.where(lane >= HEAD_DIM, q, zero)

    def step(state, kb, vb, first):
        qq_sc, m_sc, acc_sc = state
        s = lax.dot_general(qq_sc[...], kb, (((1,), (1,)), ((), ())), preferred_element_type=F32)
        if first:
            s = s + bias
        n_tiles = s.shape[1] // LANES
        rm = s[:, 0:LANES]
        for t in range(1, n_tiles):
            rm = jnp.maximum(rm, s[:, t * LANES:(t + 1) * LANES])
        rmax = jnp.max(rm, axis=-1, keepdims=True)
        if first:
            m_new = jnp.broadcast_to(rmax, (2 * bq, LANES))
        else:
            m_prev = m_sc[...]
            m_new = jnp.maximum(m_prev, rmax)
        p = jnp.exp2(s - jnp.concatenate([m_new] * n_tiles, axis=1))
        pv = jnp.dot(p.astype(BF16), vb, preferred_element_type=F32)
        if first:
            acc_sc[...] = pv
        else:
            alpha = jnp.exp2(m_prev - m_new)
            acc_sc[...] = jnp.concatenate([alpha, alpha], axis=1) * acc_sc[...] + pv
        m_sc[...] = m_new

    def key_step(state, n_full, j):
        if j == 0:
            diag = slice(n_full * bq, (n_full + 1) * bq)
            step(state, jnp.concatenate([k_ref[diag, :], km_ref[...]], axis=0),
                 jnp.concatenate([vext_sc[diag, :], vm_ext], axis=0), True)
        else:
            blk = slice((j - 1) * bq, j * bq)
            step(state, k_ref[blk, :], vext_sc[blk, :], False)

    def finish(rows, acc_sc):
        o1 = acc_sc[0:bq, 0:V_DIM] / acc_sc[0:bq, V_DIM:2 * V_DIM]
        o2 = acc_sc[bq:2 * bq, 0:V_DIM] / acc_sc[bq:2 * bq, V_DIM:2 * V_DIM]
        o = o1 - lam * o2
        ms = jnp.mean(o * o, axis=-1, keepdims=True)
        o_ref[rows, :] = (o * lax.rsqrt(ms + EPS) * subg_ref[...] * (1.0 - LAMBDA_INIT)).astype(BF16)

    def chains(t):
        blocks = []
        for i in range(pairs_per_step):
            p = t * pairs_per_step + i
            blocks += [(p, states[2 * i]), (nq - 1 - p, states[2 * i + 1])]
        for n_full, state in blocks:
            load_q(slice(n_full * bq, (n_full + 1) * bq), state[0])
        for j in range(nq):
            for n_full, state in blocks:
                if j <= n_full:
                    key_step(state, n_full, j)
        for n_full, state in blocks:
            finish(slice(n_full * bq, (n_full + 1) * bq), state[2])

    n_steps = nq // (2 * pairs_per_step)
    if n_steps == 1:
        chains(0)
    else:
        for t in range(n_steps):
            pl.when(t_id == t)(functools.partial(chains, t))


def _attention(q, k, v, km, vm, lq1, lk1, lq2, lk2, subg, batch, seq, bq, group):
    nq = seq // bq
    small = lambda b, h, i: (0, 0)
    head = lambda b, h, i: (b, h)
    state_scratch = [
        pltpu.VMEM((2 * bq, V_DIM), BF16),
        pltpu.VMEM((2 * bq, LANES), F32),
        pltpu.VMEM((2 * bq, 2 * V_DIM), F32),
    ]
    return pl.pallas_call(
        functools.partial(_attn_kernel, bq=bq),
        grid=(batch, N_HEADS, nq // group),
        in_specs=[
            pl.BlockSpec((1, HEAD_DIM), small),
            pl.BlockSpec((1, HEAD_DIM), small),
            pl.BlockSpec((1, HEAD_DIM), small),
            pl.BlockSpec((1, HEAD_DIM), small),
            pl.BlockSpec((1, V_DIM), small),
            pl.BlockSpec((seq, V_DIM), head),
            pl.BlockSpec((seq, V_DIM), head),
            pl.BlockSpec((seq, V_DIM), head),
            pl.BlockSpec((LANES, V_DIM), lambda b, h, i: (0, h)),
            pl.BlockSpec((LANES, V_DIM), lambda b, h, i: (0, h)),
        ],
        out_specs=pl.BlockSpec((seq, V_DIM), head),
        out_shape=jax.ShapeDtypeStruct(q.shape, BF16),
        scratch_shapes=[pltpu.VMEM((seq, 2 * V_DIM), BF16)] + state_scratch * group,
        compiler_params=_params("arbitrary", "arbitrary", "arbitrary"),
        name="diff_attn",
    )(lq1, lk1, lq2, lk2, subg, q, k, v, km, vm)


def _mix_kernel(x_ref, a_ref, u_ref, uprev_ref, umeta_ref, gate_ref, wa_ref, pw_ref, ps_ref, wp_ref,
                wo_ref, g2_ref, wr_ref, br_ref, h2c_ref, logt_ref, ext_sc, *, tm, tiles_per_seq):
    i = pl.program_id(0)
    first = (i % tiles_per_seq) == 0

    @pl.when(first)
    def _():
        ext_sc[0:N_META, :] = umeta_ref[...]

    @pl.when(jnp.logical_not(first))
    def _():
        ext_sc[0:N_META, :] = uprev_ref[...]

    ext_sc[N_META:N_META + tm, :] = u_ref[...]

    pooled = []
    for g, w in enumerate(POOL_WINDOWS):
        cs = slice(g * POOL_GROUP_DIM, (g + 1) * POOL_GROUP_DIM)
        cur = ext_sc[N_META:N_META + tm, cs]
        tot = cur
        for kk in range(1, w):
            tot = tot + ext_sc[N_META - kk:N_META - kk + tm, cs]
        y = (tot * (1.0 / w) - cur).astype(BF16)
        pooled.append(jnp.dot(y, pw_ref[g], preferred_element_type=F32))
    yp = (jnp.concatenate(pooled, axis=1) * ps_ref[...]).astype(BF16)
    y_pool = jnp.dot(yp, wp_ref[...], preferred_element_type=F32)
    y_attn = jnp.dot(a_ref[...], wa_ref[...], preferred_element_type=F32)
    mixed = (gate_ref[:, 0:D_MODEL].astype(F32) * y_attn
             + gate_ref[:, D_MODEL:2 * D_MODEL].astype(F32) * y_pool)
    h2 = x_ref[...] + jnp.dot(mixed.astype(BF16), wo_ref[...], preferred_element_type=F32)
    _store_token_major(h2c_ref, h2, tm)
    ms = jnp.mean(h2 * h2, axis=-1, keepdims=True)
    hn2 = h2 * lax.rsqrt(ms + EPS) * g2_ref[...]
    hn2_hi = hn2.astype(BF16)
    hn2_lo = (hn2 - hn2_hi.astype(F32)).astype(BF16)
    wr = wr_ref[...]
    wr_hi = wr.astype(BF16)
    wr_lo = (wr - wr_hi.astype(F32)).astype(BF16)
    nt = (((1,), (1,)), ((), ()))
    logt = (lax.dot_general(wr_hi, hn2_hi, nt, preferred_element_type=F32)
            + lax.dot_general(wr_hi, hn2_lo, nt, preferred_element_type=F32)
            + lax.dot_general(wr_lo, hn2_hi, nt, preferred_element_type=F32))
    logt_ref[...] = logt + br_ref[...]


def _mix(x2, a, u, umeta, gates, w_attn, pool_w, pool_scale, w_pool, w_out, g2, wr_t, br_t, seq, tm):
    n = x2.shape[0]
    tiles_per_seq = seq // tm
    halo_blocks = tm // N_META
    const2 = lambda i: (0, 0)
    n_rt = wr_t.shape[0]
    return pl.pallas_call(
        functools.partial(_mix_kernel, tm=tm, tiles_per_seq=tiles_per_seq),
        grid=(n // tm,),
        in_specs=[
            pl.BlockSpec((tm, D_MODEL), lambda i: (i, 0)),
            pl.BlockSpec((tm, D_MODEL), lambda i: (i, 0)),
            pl.BlockSpec((tm, POOL_WIDTH), lambda i: (i, 0)),
            pl.BlockSpec((N_META, POOL_WIDTH), lambda i: (jnp.maximum(i * halo_blocks - 1, 0), 0)),
            pl.BlockSpec((N_META, POOL_WIDTH), const2),
            pl.BlockSpec((tm, 2 * D_MODEL), lambda i: (i, 0)),
            pl.BlockSpec((D_MODEL, D_MODEL), const2),
            pl.BlockSpec((N_POOL_GROUPS, POOL_GROUP_DIM, POOL_GROUP_DIM), lambda i: (0, 0, 0)),
            pl.BlockSpec((1, POOL_WIDTH), const2),
            pl.BlockSpec((POOL_WIDTH, D_MODEL), const2),
            pl.BlockSpec((D_MODEL, D_MODEL), const2),
            pl.BlockSpec((1, D_MODEL), const2),
            pl.BlockSpec((n_rt, D_MODEL), const2),
            pl.BlockSpec((n_rt, 1), const2),
        ],
        out_specs=[
            pl.BlockSpec((tm * TOKEN_ROWS, LANES), lambda i: (i, 0)),
            pl.BlockSpec((n_rt, tm), lambda i: (0, i)),
        ],
        out_shape=[
            jax.ShapeDtypeStruct((n * TOKEN_ROWS, LANES), F32),
            jax.ShapeDtypeStruct((n_rt, n), F32),
        ],
        scratch_shapes=[pltpu.VMEM((tm + N_META, POOL_WIDTH), F32)],
        compiler_params=_params("arbitrary"),
        name="mix",
    )(x2, a, u, u, umeta, gates, w_attn, pool_w, pool_scale, w_pool, w_out, g2, wr_t, br_t)


def _route_kernel(logt_ref, tri_ref, ids_ref, wts_ref, cnt_ref, carry_sc):
    @pl.when(pl.program_id(0) == 0)
    def _():
        carry_sc[...] = jnp.zeros_like(carry_sc)

    lg = logt_ref[...]
    g = [lg[r:r + 1, :] for r in range(N_GROUPS)]
    gmax = functools.reduce(jnp.maximum, g)
    gsel = jnp.full_like(g[0], N_GROUPS - 1).astype(jnp.int32)
    for r in range(N_GROUPS - 2, -1, -1):
        gsel = jnp.where(g[r] == gmax, r, gsel)
    p_group = 1.0 / functools.reduce(lambda a, b: a + b, [jnp.exp(x - gmax) for x in g])
    e = []
    for jj in range(EXPERTS_PER_GROUP):
        v = lg[N_GROUPS + jj:N_GROUPS + jj + 1, :]
        for r in range(1, N_GROUPS):
            row = N_GROUPS + r * EXPERTS_PER_GROUP + jj
            v = jnp.where(gsel == r, lg[row:row + 1, :], v)
        e.append(v)
    v1 = functools.reduce(jnp.maximum, e)
    i1 = jnp.full_like(gsel, EXPERTS_PER_GROUP - 1)
    for jj in range(EXPERTS_PER_GROUP - 2, -1, -1):
        i1 = jnp.where(e[jj] == v1, jj, i1)
    rest = [jnp.where(i1 == jj, -jnp.inf, e[jj]) for jj in range(EXPERTS_PER_GROUP)]
    v2 = functools.reduce(jnp.maximum, rest)
    i2 = jnp.full_like(gsel, EXPERTS_PER_GROUP - 1)
    for jj in range(EXPERTS_PER_GROUP - 2, -1, -1):
        i2 = jnp.where(jnp.logical_and(rest[jj] == v2, i1 != jj), jj, i2)
    t = jnp.exp(v2 - v1)
    w1 = p_group / (1.0 + t)
    w2 = p_group * t / (1.0 + t)
    lo = jnp.minimum(i1, i2)
    hi = jnp.maximum(i1, i2)
    pair = jnp.where(lo == 0, hi - 1, jnp.where(lo == 1, jnp.where(hi == 3, 3, 4), 5))
    bucket = gsel * N_PAIRS + pair
    w_of_lo = jnp.where(i1 < i2, w1, w2)
    w_of_hi = jnp.where(i1 < i2, w2, w1)
    swapped = pair == 5
    w_a = jnp.where(swapped, w_of_hi, w_of_lo)
    w_b = jnp.where(swapped, w_of_lo, w_of_hi)

    tl = lg.shape[1]
    bid = lax.broadcasted_iota(jnp.int32, (N_BUCKET_ROWS, tl), 0)
    hot = bid == bucket
    hot_b = jnp.where(hot, 1.0, 0.0).astype(BF16)
    prefix = jnp.dot(hot_b, tri_ref[...], preferred_element_type=F32)
    carry = carry_sc[...]
    before = prefix + jnp.concatenate([carry] * (tl // LANES), axis=1)
    rank = jnp.sum(jnp.where(hot, before, 0.0), axis=0, keepdims=True)
    total = carry + jnp.dot(hot_b, jnp.ones((tl, LANES), BF16), preferred_element_type=F32)
    carry_sc[...] = total
    cnt_ref[...] = total.astype(jnp.int32)
    pad_i = jnp.zeros((6, tl), jnp.int32)
    ids_ref[...] = jnp.concatenate([bucket, rank.astype(jnp.int32), pad_i], axis=0)
    wts_ref[...] = jnp.concatenate([w_a, w_b, pad_i.astype(F32)], axis=0)


def _route(logt, tl):
    n_rt, n = logt.shape
    tri = jnp.triu(jnp.ones((tl, tl), BF16), k=1)
    return pl.pallas_call(
        _route_kernel,
        grid=(n // tl,),
        in_specs=[pl.BlockSpec((n_rt, tl), lambda i: (0, i)),
                  pl.BlockSpec((tl, tl), lambda i: (0, 0))],
        out_specs=[pl.BlockSpec((8, tl), lambda i: (0, i)),
                   pl.BlockSpec((8, tl), lambda i: (0, i)),
                   pl.BlockSpec((N_BUCKET_ROWS, LANES), lambda i: (0, 0))],
        out_shape=[jax.ShapeDtypeStruct((8, n), jnp.int32),
                   jax.ShapeDtypeStruct((8, n), F32),
                   jax.ShapeDtypeStruct((N_BUCKET_ROWS, LANES), jnp.int32)],
        scratch_shapes=[pltpu.VMEM((N_BUCKET_ROWS, LANES), F32)],
        compiler_params=_params("arbitrary"),
        name="route",
    )(logt, tri)


def _token_copy(src_hbm, src_token, dst_buf, dst_row, sem):
    return pltpu.make_async_copy(
        src_hbm.at[pl.ds(pl.multiple_of(src_token * TOKEN_ROWS, TOKEN_ROWS), TOKEN_ROWS), :],
        dst_buf.at[pl.ds(pl.multiple_of(dst_row * TOKEN_ROWS, TOKEN_ROWS), TOKEN_ROWS), :],
        sem)


def _start_gather(idx_ref, first, row0, rows, src_hbm, dst_buf, sem):
    def body(rr, carry):
        for uu in range(GATHER_UNROLL):
            r = row0 + rr * GATHER_UNROLL + uu
            _token_copy(src_hbm, idx_ref[first + r], dst_buf, r, sem).start(priority=uu % 2)
        return carry

    lax.fori_loop(0, rows // GATHER_UNROLL, body, 0)


def _wait_gather(rows, src_hbm, dst_buf, sem):
    pltpu.make_async_copy(src_hbm.at[pl.ds(0, rows * TOKEN_ROWS), :], dst_buf, sem).wait()


def _moe_kernel(ea_ref, eb_ref, new_a_ref, new_b_ref, nused_ref, src_ref, h2c_hbm, wts_ref, g2_ref,
                wg_a_ref, wu_a_ref, wd_a_ref, wg_b_ref, wu_b_ref, wd_b_ref, yc_ref,
                xbuf, wg_a_sc, wu_a_sc, wd_a_sc, wg_b_sc, wu_b_sc, wd_b_sc, sem, *, tme):
    i = pl.program_id(0)
    n_used = nused_ref[0]
    slot = jnp.bitwise_and(i, 1)

    @pl.when(i == 0)
    def _():
        _start_gather(src_ref, 0, 0, tme, h2c_hbm, xbuf.at[0], sem.at[0])

    @pl.when(i + 1 < n_used)
    def _():
        _start_gather(src_ref, (i + 1) * tme, 0, tme, h2c_hbm, xbuf.at[1 - slot], sem.at[1 - slot])

    @pl.when(new_a_ref[i] == 1)
    def _():
        for src, dst in ((wg_a_ref, wg_a_sc), (wu_a_ref, wu_a_sc), (wd_a_ref, wd_a_sc)):
            dst[...] = src[0].astype(BF16)

    @pl.when(new_b_ref[i] == 1)
    def _():
        for src, dst in ((wg_b_ref, wg_b_sc), (wu_b_ref, wu_b_sc), (wd_b_ref, wd_b_sc)):
            dst[...] = src[0].astype(BF16)

    @pl.when(i < n_used)
    def _():
        _wait_gather(tme, h2c_hbm, xbuf.at[slot], sem.at[slot])
        x = _load_token_major(xbuf.at[slot], tme)
        ms = jnp.mean(x * x, axis=-1, keepdims=True)
        hn = (x * lax.rsqrt(ms + EPS) * g2_ref[...]).astype(BF16)
        wts = wts_ref[...]

        def expert_act(wg_sc, wu_sc, w):
            gate = jnp.dot(hn, wg_sc[...], preferred_element_type=F32)
            up = jnp.dot(hn, wu_sc[...], preferred_element_type=F32)
            return (gate / (1.0 + jnp.exp(-gate)) * up * w).astype(BF16)

        y = (jnp.dot(expert_act(wg_a_sc, wu_a_sc, wts[:, 0:1]), wd_a_sc[...], preferred_element_type=F32)
             + jnp.dot(expert_act(wg_b_sc, wu_b_sc, wts[:, 1:2]), wd_b_sc[...], preferred_element_type=F32))
        _store_token_major(yc_ref, y, tme)

    @pl.when(i >= n_used)
    def _():
        yc_ref[...] = jnp.zeros_like(yc_ref)


def _moe(ea, eb, new_a, new_b, nused, src, h2c, wts_sorted, g2, wg, wu, wd, tme, n_tiles):
    used = lambda i, ea, eb, na, nb, nu, src: (jnp.minimum(i, nu[0] - 1), 0)
    w_a = lambda i, ea, eb, na, nb, nu, src: (ea[i], 0, 0)
    w_b = lambda i, ea, eb, na, nb, nu, src: (eb[i], 0, 0)
    up_shape, down_shape = (D_MODEL, D_EXPERT), (D_EXPERT, D_MODEL)
    grid_spec = pltpu.PrefetchScalarGridSpec(
        num_scalar_prefetch=6,
        grid=(n_tiles,),
        in_specs=[
            pl.BlockSpec(memory_space=pl.ANY),
            pl.BlockSpec((tme, 2), used),
            pl.BlockSpec((1, D_MODEL), lambda i, ea, eb, na, nb, nu, src: (0, 0)),
            pl.BlockSpec((1,) + up_shape, w_a),
            pl.BlockSpec((1,) + up_shape, w_a),
            pl.BlockSpec((1,) + down_shape, w_a),
            pl.BlockSpec((1,) + up_shape, w_b),
            pl.BlockSpec((1,) + up_shape, w_b),
            pl.BlockSpec((1,) + down_shape, w_b),
        ],
        out_specs=pl.BlockSpec((tme * TOKEN_ROWS, LANES), lambda i, ea, eb, na, nb, nu, src: (i, 0)),
        scratch_shapes=[pltpu.VMEM((2, tme * TOKEN_ROWS, LANES), F32)]
        + [pltpu.VMEM(s, BF16) for s in (up_shape, up_shape, down_shape) * 2]
        + [pltpu.SemaphoreType.DMA((2,))],
    )
    return pl.pallas_call(
        functools.partial(_moe_kernel, tme=tme),
        grid_spec=grid_spec,
        out_shape=jax.ShapeDtypeStruct((n_tiles * tme * TOKEN_ROWS, LANES), F32),
        compiler_params=_params("arbitrary"),
        name="moe",
    )(ea, eb, new_a, new_b, nused, src, h2c, wts_sorted, g2, wg, wu, wd, wg, wu, wd)


def _final_kernel(dest_ref, h2c_ref, yc_hbm, gf_ref, o_ref, ybuf, sem, *, tmf):
    i = pl.program_id(0)
    slot = jnp.bitwise_and(i, 1)

    @pl.when(i == 0)
    def _():
        _start_gather(dest_ref, 0, 0, tmf, yc_hbm, ybuf.at[0], sem.at[0])

    @pl.when(i + 1 < pl.num_programs(0))
    def _():
        _start_gather(dest_ref, (i + 1) * tmf, 0, tmf, yc_hbm, ybuf.at[1 - slot], sem.at[1 - slot])

    _wait_gather(tmf, yc_hbm, ybuf.at[slot], sem.at[slot])
    h = _load_token_major(h2c_ref, tmf) + _load_token_major(ybuf.at[slot], tmf)
    ms = jnp.mean(h * h, axis=-1, keepdims=True)
    o_ref[...] = h * lax.rsqrt(ms + EPS) * gf_ref[...]


def _final(dest, h2c, yc, gf, tmf):
    n = h2c.shape[0] // TOKEN_ROWS
    grid_spec = pltpu.PrefetchScalarGridSpec(
        num_scalar_prefetch=1,
        grid=(n // tmf,),
        in_specs=[
            pl.BlockSpec((tmf * TOKEN_ROWS, LANES), lambda i, dest: (i, 0)),
            pl.BlockSpec(memory_space=pl.ANY),
            pl.BlockSpec((1, D_MODEL), lambda i, dest: (0, 0)),
        ],
        out_specs=pl.BlockSpec((tmf, D_MODEL), lambda i, dest: (i, 0)),
        scratch_shapes=[pltpu.VMEM((2, tmf * TOKEN_ROWS, LANES), F32), pltpu.SemaphoreType.DMA((2,))],
    )
    return pl.pallas_call(
        functools.partial(_final_kernel, tmf=tmf),
        grid_spec=grid_spec,
        out_shape=jax.ShapeDtypeStruct((n, D_MODEL), F32),
        compiler_params=_params("arbitrary"),
        name="final",
    )(dest, h2c, yc, gf)


def _rope_tables(t):
    inv = 1.0 / (ROPE_THETA ** (jnp.arange(0, HEAD_DIM, 2, dtype=F32) / HEAD_DIM))
    ang = jnp.arange(t, dtype=F32)[:, None] * inv[None, :]
    ang = jnp.concatenate([ang, ang, ang, ang], axis=-1)
    sign = jnp.where((jnp.arange(LANES) % HEAD_DIM) < HEAD_DIM // 2, -1.0, 1.0).astype(F32)
    return jnp.cos(ang), jnp.sin(ang) * sign[None, :]


def kernel(x, meta, norm1_g, w_in, b_gate, lambda_q1, lambda_k1, lambda_q2, lambda_k2, subln_g, pool_w,
           pool_scale, w_attn_br, w_pool_br, w_out, norm2_g, w_router_group, b_router_group,
           w_router_expert, b_router_expert, w_e_gate, w_e_up, w_e_down, final_g):
    batch, seq, d = x.shape
    n = batch * seq
    x2 = x.reshape(n, d)
    cos, sin = _rope_tables(seq + N_META)
    q, k, v, u, gates = _inproj(x2, norm1_g, cos[N_META:], sin[N_META:], w_in[0], b_gate, tm=1024)
    _, km, vm, um, _ = _inproj(meta, norm1_g, cos[:N_META], sin[:N_META], w_in[0], b_gate, tm=N_META)
    pad = ((0, LANES - N_META), (0, 0))
    a = _attention(q, k, v, jnp.pad(km, pad), jnp.pad(vm, pad), lambda_q1, lambda_k1, lambda_q2,
                   lambda_k2, subln_g, batch, seq, bq=256, group=8)

    n_rt = 32
    wr_t = jnp.concatenate([w_router_group[0], w_router_expert[0]], axis=1).T
    wr_t = jnp.pad(wr_t, ((0, n_rt - wr_t.shape[0]), (0, 0)))
    br_t = jnp.concatenate([b_router_group[0], b_router_expert[0]])
    br_t = jnp.pad(br_t, (0, n_rt - br_t.shape[0])).reshape(n_rt, 1)
    h2c, logt = _mix(x2, a, u, um, gates, w_attn_br[0].astype(BF16), pool_w[0].astype(BF16),
                     pool_scale, w_pool_br[0].astype(BF16), w_out[0].astype(BF16), norm2_g,
                     wr_t, br_t, seq, tm=512)
    ids, wts, cnt = _route(logt, tl=1024)

    tme = 256
    n_tiles = n // tme + N_BUCKETS
    bucket, rank = ids[0], ids[1]
    tiles_per_bucket = (cnt[:N_BUCKETS, 0] + tme - 1) // tme
    tile_end = jnp.cumsum(tiles_per_bucket)
    n_used = tile_end[-1]
    dest = (tile_end - tiles_per_bucket)[bucket] * tme + rank
    per_token = jnp.concatenate([wts[0:2], jnp.arange(n, dtype=F32)[None, :]], axis=0).T
    per_row = jnp.zeros((n_tiles * tme, 3), F32).at[dest].set(per_token)
    wts_sorted = per_row[:, 0:2]
    src = per_row[:, 2].astype(jnp.int32)
    last_tile = jnp.minimum(jnp.arange(n_tiles), n_used - 1)
    tile_bucket = jnp.sum((tile_end[None, :] <= last_tile[:, None]).astype(jnp.int32), axis=1)
    tile_bucket = jnp.minimum(tile_bucket, N_BUCKETS - 1)
    group, pair = tile_bucket // N_PAIRS, tile_bucket % N_PAIRS
    e_a = (group * EXPERTS_PER_GROUP + jnp.asarray(_PAIR_A, jnp.int32)[pair]).astype(jnp.int32)
    e_b = (group * EXPERTS_PER_GROUP + jnp.asarray(_PAIR_B, jnp.int32)[pair]).astype(jnp.int32)
    changed = lambda e: jnp.concatenate([jnp.ones((1,), jnp.int32), (e[1:] != e[:-1]).astype(jnp.int32)])

    yc = _moe(e_a, e_b, changed(e_a), changed(e_b), n_used.reshape(1).astype(jnp.int32), src, h2c,
              wts_sorted, norm2_g, w_e_gate[0], w_e_up[0], w_e_down[0], tme, n_tiles)
    out = _final(dest.astype(jnp.int32), h2c, yc, final_g.reshape(1, d), tmf=512)
    return out.reshape(batch, seq, d)
```

```python
import functools
import math

import jax
import jax.numpy as jnp
import numpy as np
from jax import lax
from jax.experimental import pallas as pl
from jax.experimental.pallas import tpu as pltpu

D_MODEL = 1024
N_META = 16
N_HEADS = 8
HEAD_DIM = 64
V_DIM = 2 * HEAD_DIM
POOL_WINDOWS = (2, 4, 8, 16)
N_POOL_GROUPS = len(POOL_WINDOWS)
POOL_GROUP_DIM = 128
POOL_WIDTH = N_POOL_GROUPS * POOL_GROUP_DIM
ROPE_THETA = 10000.0
N_GROUPS = 4
EXPERTS_PER_GROUP = 4
N_EXPERTS = N_GROUPS * EXPERTS_PER_GROUP
D_EXPERT = 512
EPS = 1e-6
LAMBDA_INIT = 0.8 - 0.6 * math.exp(-0.3 * 0)
LOG2_E = math.log2(math.e)

_PAIR_A = (0, 0, 0, 1, 1, 3)
_PAIR_B = (1, 2, 3, 3, 2, 2)
N_PAIRS = len(_PAIR_A)
N_BUCKETS = N_GROUPS * N_PAIRS
N_BUCKET_ROWS = 32
GATHER_UNROLL = 8

LANES = 128
NEG_BIG = -1e30
VMEM_LIMIT = 48 * 1024 * 1024

F32 = jnp.float32
BF16 = jnp.bfloat16

_HEADS_PER_STEP = 2
_IN_STEPS = N_HEADS // _HEADS_PER_STEP
_QKV_BLK = _HEADS_PER_STEP * V_DIM
_GATE_BLK = 2 * D_MODEL // _IN_STEPS
_O_K = D_MODEL // _QKV_BLK
_O_V = 2 * D_MODEL // _QKV_BLK
_O_U = 3 * D_MODEL // POOL_GROUP_DIM
_O_G = (3 * D_MODEL + POOL_WIDTH) // _GATE_BLK


def _params(*sem):
    return pltpu.CompilerParams(dimension_semantics=sem, vmem_limit_bytes=VMEM_LIMIT)


TOKEN_ROWS = D_MODEL // LANES


def _store_token_major(ref, val, rows):
    for s in range(TOKEN_ROWS):
        ref[pl.ds(s, rows, stride=TOKEN_ROWS), :] = val[:, s * LANES:(s + 1) * LANES]


def _load_token_major(ref, rows):
    return jnp.concatenate([ref[pl.ds(s, rows, stride=TOKEN_ROWS), :] for s in range(TOKEN_ROWS)], axis=1)


def _rope(z, cos, sin_signed, first_half):
    outs = []
    for c in range(z.shape[1] // LANES):
        zc = z[:, c * LANES:(c + 1) * LANES]
        rot = jnp.where(first_half, pltpu.roll(zc, LANES - HEAD_DIM // 2, 1),
                        pltpu.roll(zc, HEAD_DIM // 2, 1))
        outs.append(zc * cos + rot * sin_signed)
    return jnp.concatenate(outs, axis=1)


def _inproj_kernel(x_ref, g_ref, cos_ref, sin_ref, wq_ref, wk_ref, wv_ref, wu_ref, wg_ref, bg_ref,
                   q_ref, k_ref, v_ref, u_ref, gate_ref, wq_sc, wk_sc, wv_sc, wu_sc, wg_sc):
    @pl.when(pl.program_id(1) == 0)
    def _():
        for src, dst in ((wq_ref, wq_sc), (wk_ref, wk_sc), (wv_ref, wv_sc), (wu_ref, wu_sc),
                         (wg_ref, wg_sc)):
            dst[...] = src[...].astype(BF16)

    x = x_ref[...]
    ms = jnp.mean(x * x, axis=-1, keepdims=True)
    hn = (x * lax.rsqrt(ms + EPS) * g_ref[...]).astype(BF16)
    cos = cos_ref[...]
    sin = sin_ref[...]
    lane = lax.broadcasted_iota(jnp.int32, cos.shape, 1)
    first_half = jnp.bitwise_and(lane, HEAD_DIM - 1) < (HEAD_DIM // 2)
    zg = jnp.dot(hn, wg_sc[...], preferred_element_type=F32) + bg_ref[...]
    gate_ref[...] = (1.0 / (1.0 + jnp.exp(-zg))).astype(BF16)
    zq = jnp.dot(hn, wq_sc[...], preferred_element_type=F32)
    q_ref[...] = (_rope(zq, cos, sin, first_half) * (LOG2_E / math.sqrt(HEAD_DIM))).astype(BF16)
    zk = jnp.dot(hn, wk_sc[...], preferred_element_type=F32)
    k_ref[...] = _rope(zk, cos, sin, first_half).astype(BF16)
    u_ref[...] = jnp.dot(hn, wu_sc[...], preferred_element_type=F32)
    v_ref[...] = jnp.dot(hn, wv_sc[...], preferred_element_type=F32).astype(BF16)


def _inproj(x2, g1, cos, sin, w_in, b_gate, tm):
    n = x2.shape[0]
    n_pos_blocks = cos.shape[0] // tm
    row = lambda j, i: (i, 0)
    pos = lambda j, i: (i % n_pos_blocks, 0)
    return pl.pallas_call(
        _inproj_kernel,
        grid=(_IN_STEPS, n // tm),
        in_specs=[
            pl.BlockSpec((tm, D_MODEL), row),
            pl.BlockSpec((1, D_MODEL), lambda j, i: (0, 0)),
            pl.BlockSpec((tm, LANES), pos),
            pl.BlockSpec((tm, LANES), pos),
            pl.BlockSpec((D_MODEL, _QKV_BLK), lambda j, i: (0, j)),
            pl.BlockSpec((D_MODEL, _QKV_BLK), lambda j, i: (0, _O_K + j)),
            pl.BlockSpec((D_MODEL, _QKV_BLK), lambda j, i: (0, _O_V + j)),
            pl.BlockSpec((D_MODEL, POOL_GROUP_DIM), lambda j, i: (0, _O_U + j)),
            pl.BlockSpec((D_MODEL, _GATE_BLK), lambda j, i: (0, _O_G + j)),
            pl.BlockSpec((1, _GATE_BLK), lambda j, i: (0, j)),
        ],
        out_specs=[
            pl.BlockSpec((tm, _QKV_BLK), lambda j, i: (i, j)),
            pl.BlockSpec((tm, _QKV_BLK), lambda j, i: (i, j)),
            pl.BlockSpec((tm, _QKV_BLK), lambda j, i: (i, j)),
            pl.BlockSpec((tm, POOL_GROUP_DIM), lambda j, i: (i, j)),
            pl.BlockSpec((tm, _GATE_BLK), lambda j, i: (i, j)),
        ],
        out_shape=[
            jax.ShapeDtypeStruct((n, D_MODEL), BF16),
            jax.ShapeDtypeStruct((n, D_MODEL), BF16),
            jax.ShapeDtypeStruct((n, D_MODEL), BF16),
            jax.ShapeDtypeStruct((n, POOL_WIDTH), F32),
            jax.ShapeDtypeStruct((n, 2 * D_MODEL), BF16),
        ],
        scratch_shapes=[
            pltpu.VMEM((D_MODEL, _QKV_BLK), BF16),
            pltpu.VMEM((D_MODEL, _QKV_BLK), BF16),
            pltpu.VMEM((D_MODEL, _QKV_BLK), BF16),
            pltpu.VMEM((D_MODEL, POOL_GROUP_DIM), BF16),
            pltpu.VMEM((D_MODEL, _GATE_BLK), BF16),
        ],
        compiler_params=_params("arbitrary", "arbitrary"),
        name="inproj",
    )(x2, g1, cos, sin, w_in, w_in, w_in, w_in, w_in, b_gate)


def _attn_kernel(lq1_ref, lk1_ref, lq2_ref, lk2_ref, subg_ref, q_ref, k_ref, v_ref, km_ref, vm_ref,
                 o_ref, vext_sc, *state_sc, bq):
    t_id = pl.program_id(2)
    nq = k_ref.shape[0] // bq
    states = [tuple(state_sc[3 * g:3 * g + 3]) for g in range(len(state_sc) // 3)]
    pairs_per_step = len(states) // 2

    @pl.when(t_id == 0)
    def _():
        vext_sc[:, 0:V_DIM] = v_ref[...]
        vext_sc[:, V_DIM:2 * V_DIM] = jnp.ones(v_ref.shape, BF16)

    r = lax.broadcasted_iota(jnp.int32, (bq, bq + LANES), 0)
    c = lax.broadcasted_iota(jnp.int32, (bq, bq + LANES), 1)
    visible = jnp.logical_or(c <= r, jnp.logical_and(c >= bq, c < bq + N_META))
    bias = jnp.where(visible, 0.0, NEG_BIG)
    bias = jnp.concatenate([bias, bias], axis=0)
    vm_ext = jnp.concatenate([vm_ref[...], jnp.ones((LANES, V_DIM), BF16)], axis=1)
    lam = (jnp.exp(jnp.sum(lq1_ref[...] * lk1_ref[...], axis=-1, keepdims=True))
           - jnp.exp(jnp.sum(lq2_ref[...] * lk2_ref[...], axis=-1, keepdims=True)) + LAMBDA_INIT)
    lane = lax.broadcasted_iota(jnp.int32, (bq, V_DIM), 1)

    def load_q(rows, qq_sc):
        q = q_ref[rows, :]
        zero = jnp.zeros_like(q)
        qq_sc[0:bq, :] = jnp.where(lane < HEAD_DIM, q, zero)
        qq_sc[bq:2 * bq, :] = jnp.where(lane >= HEAD_DIM, q, zero)

    def step(state, kb, vb, first):
        qq_sc, m_sc, acc_sc = state
        s = lax.dot_general(qq_sc[...], kb, (((1,), (1,)), ((), ())), preferred_element_type=F32)
        if first:
            s = s + bias
        n_tiles = s.shape[1] // LANES
        rm = s[:, 0:LANES]
        for t in range(1, n_tiles):
            rm = jnp.maximum(rm, s[:, t * LANES:(t + 1) * LANES])
        rmax = jnp.max(rm, axis=-1, keepdims=True)
        if first:
            m_new = jnp.broadcast_to(rmax, (2 * bq, LANES))
        else:
            m_prev = m_sc[...]
            m_new = jnp.maximum(m_prev, rmax)
        p = jnp.exp2(s - jnp.concatenate([m_new] * n_tiles, axis=1))
        pv = jnp.dot(p.astype(BF16), vb, preferred_element_type=F32)
        if first:
            acc_sc[...] = pv
        else:
            alpha = jnp.exp2(m_prev - m_new)
            acc_sc[...] = jnp.concatenate([alpha, alpha], axis=1) * acc_sc[...] + pv
        m_sc[...] = m_new

    def key_step(state, n_full, j):
        if j == 0:
            diag = slice(n_full * bq, (n_full + 1) * bq)
            step(state, jnp.concatenate([k_ref[diag, :], km_ref[...]], axis=0),
                 jnp.concatenate([vext_sc[diag, :], vm_ext], axis=0), True)
        else:
            blk = slice((j - 1) * bq, j * bq)
            step(state, k_ref[blk, :], vext_sc[blk, :], False)

    def finish(rows, acc_sc):
        o1 = acc_sc[0:bq, 0:V_DIM] / acc_sc[0:bq, V_DIM:2 * V_DIM]
        o2 = acc_sc[bq:2 * bq, 0:V_DIM] / acc_sc[bq:2 * bq, V_DIM:2 * V_DIM]
        o = o1 - lam * o2
        ms = jnp.mean(o * o, axis=-1, keepdims=True)
        o_ref[rows, :] = (o * lax.rsqrt(ms + EPS) * subg_ref[...] * (1.0 - LAMBDA_INIT)).astype(BF16)

    def chains(t):
        blocks = []
        for i in range(pairs_per_step):
            p = t * pairs_per_step + i
            blocks += [(p, states[2 * i]), (nq - 1 - p, states[2 * i + 1])]
        for n_full, state in blocks:
            load_q(slice(n_full * bq, (n_full + 1) * bq), state[0])
        for j in range(nq):
            for n_full, state in blocks:
                if j <= n_full:
                    key_step(state, n_full, j)
        for n_full, state in blocks:
            finish(slice(n_full * bq, (n_full + 1) * bq), state[2])

    n_steps = nq // (2 * pairs_per_step)
    if n_steps == 1:
        chains(0)
    else:
        for t in range(n_steps):
            pl.when(t_id == t)(functools.partial(chains, t))


def _attention(q, k, v, km, vm, lq1, lk1, lq2, lk2, subg, batch, seq, bq, group):
    nq = seq // bq
    small = lambda b, h, i: (0, 0)
    head = lambda b, h, i: (b, h)
    state_scratch = [
        pltpu.VMEM((2 * bq, V_DIM), BF16),
        pltpu.VMEM((2 * bq, LANES), F32),
        pltpu.VMEM((2 * bq, 2 * V_DIM), F32),
    ]
    return pl.pallas_call(
        functools.partial(_attn_kernel, bq=bq),
        grid=(batch, N_HEADS, nq // group),
        in_specs=[
            pl.BlockSpec((1, HEAD_DIM), small),
            pl.BlockSpec((1, HEAD_DIM), small),
            pl.BlockSpec((1, HEAD_DIM), small),
            pl.BlockSpec((1, HEAD_DIM), small),
            pl.BlockSpec((1, V_DIM), small),
            pl.BlockSpec((seq, V_DIM), head),
            pl.BlockSpec((seq, V_DIM), head),
            pl.BlockSpec((seq, V_DIM), head),
            pl.BlockSpec((LANES, V_DIM), lambda b, h, i: (0, h)),
            pl.BlockSpec((LANES, V_DIM), lambda b, h, i: (0, h)),
        ],
        out_specs=pl.BlockSpec((seq, V_DIM), head),
        out_shape=jax.ShapeDtypeStruct(q.shape, BF16),
        scratch_shapes=[pltpu.VMEM((seq, 2 * V_DIM), BF16)] + state_scratch * group,
        compiler_params=_params("arbitrary", "arbitrary", "arbitrary"),
        name="diff_attn",
    )(lq1, lk1, lq2, lk2, subg, q, k, v, km, vm)


def _mix_kernel(x_ref, a_ref, u_ref, uprev_ref, umeta_ref, gate_ref, wa_ref, pw_ref, ps_ref, wp_ref,
                wo_ref, g2_ref, wr_ref, br_ref, h2c_ref, logt_ref, ext_sc, *, tm, tiles_per_seq):
    i = pl.program_id(0)
    first = (i % tiles_per_seq) == 0

    @pl.when(first)
    def _():
        ext_sc[0:N_META, :] = umeta_ref[...]

    @pl.when(jnp.logical_not(first))
    def _():
        ext_sc[0:N_META, :] = uprev_ref[...]

    ext_sc[N_META:N_META + tm, :] = u_ref[...]

    pooled = []
    for g, w in enumerate(POOL_WINDOWS):
        cs = slice(g * POOL_GROUP_DIM, (g + 1) * POOL_GROUP_DIM)
        cur = ext_sc[N_META:N_META + tm, cs]
        tot = cur
        for kk in range(1, w):
            tot = tot + ext_sc[N_META - kk:N_META - kk + tm, cs]
        y = (tot * (1.0 / w) - cur).astype(BF16)
        pooled.append(jnp.dot(y, pw_ref[g], preferred_element_type=F32))
    yp = (jnp.concatenate(pooled, axis=1) * ps_ref[...]).astype(BF16)
    y_pool = jnp.dot(yp, wp_ref[...], preferred_element_type=F32)
    y_attn = jnp.dot(a_ref[...], wa_ref[...], preferred_element_type=F32)
    mixed = (gate_ref[:, 0:D_MODEL].astype(F32) * y_attn
             + gate_ref[:, D_MODEL:2 * D_MODEL].astype(F32) * y_pool)
    h2 = x_ref[...] + jnp.dot(mixed.astype(BF16), wo_ref[...], preferred_element_type=F32)
    _store_token_major(h2c_ref, h2, tm)
    ms = jnp.mean(h2 * h2, axis=-1, keepdims=True)
    hn2 = h2 * lax.rsqrt(ms + EPS) * g2_ref[...]
    hn2_hi = hn2.astype(BF16)
    hn2_lo = (hn2 - hn2_hi.astype(F32)).astype(BF16)
    wr = wr_ref[...]
    wr_hi = wr.astype(BF16)
    wr_lo = (wr - wr_hi.astype(F32)).astype(BF16)
    nt = (((1,), (1,)), ((), ()))
    logt = (lax.dot_general(wr_hi, hn2_hi, nt, preferred_element_type=F32)
            + lax.dot_general(wr_hi, hn2_lo, nt, preferred_element_type=F32)
            + lax.dot_general(wr_lo, hn2_hi, nt, preferred_element_type=F32))
    logt_ref[...] = logt + br_ref[...]


def _mix(x2, a, u, umeta, gates, w_attn, pool_w, pool_scale, w_pool, w_out, g2, wr_t, br_t, seq, tm):
    n = x2.shape[0]
    tiles_per_seq = seq // tm
    halo_blocks = tm // N_META
    const2 = lambda i: (0, 0)
    n_rt = wr_t.shape[0]
    return pl.pallas_call(
        functools.partial(_mix_kernel, tm=tm, tiles_per_seq=tiles_per_seq),
        grid=(n // tm,),
        in_specs=[
            pl.BlockSpec((tm, D_MODEL), lambda i: (i, 0)),
            pl.BlockSpec((tm, D_MODEL), lambda i: (i, 0)),
            pl.BlockSpec((tm, POOL_WIDTH), lambda i: (i, 0)),
            pl.BlockSpec((N_META, POOL_WIDTH), lambda i: (jnp.maximum(i * halo_blocks - 1, 0), 0)),
            pl.BlockSpec((N_META, POOL_WIDTH), const2),
            pl.BlockSpec((tm, 2 * D_MODEL), lambda i: (i, 0)),
            pl.BlockSpec((D_MODEL, D_MODEL), const2),
            pl.BlockSpec((N_POOL_GROUPS, POOL_GROUP_DIM, POOL_GROUP_DIM), lambda i: (0, 0, 0)),
            pl.BlockSpec((1, POOL_WIDTH), const2),
            pl.BlockSpec((POOL_WIDTH, D_MODEL), const2),
            pl.BlockSpec((D_MODEL, D_MODEL), const2),
            pl.BlockSpec((1, D_MODEL), const2),
            pl.BlockSpec((n_rt, D_MODEL), const2),
            pl.BlockSpec((n_rt, 1), const2),
        ],
        out_specs=[
            pl.BlockSpec((tm * TOKEN_ROWS, LANES), lambda i: (i, 0)),
            pl.BlockSpec((n_rt, tm), lambda i: (0, i)),
        ],
        out_shape=[
            jax.ShapeDtypeStruct((n * TOKEN_ROWS, LANES), F32),
            jax.ShapeDtypeStruct((n_rt, n), F32),
        ],
        scratch_shapes=[pltpu.VMEM((tm + N_META, POOL_WIDTH), F32)],
        compiler_params=_params("arbitrary"),
        name="mix",
    )(x2, a, u, u, umeta, gates, w_attn, pool_w, pool_scale, w_pool, w_out, g2, wr_t, br_t)


def _route_kernel(logt_ref, tri_ref, ids_ref, wts_ref, cnt_ref, carry_sc):
    @pl.when(pl.program_id(0) == 0)
    def _():
        carry_sc[...] = jnp.zeros_like(carry_sc)

    lg = logt_ref[...]
    g = [lg[r:r + 1, :] for r in range(N_GROUPS)]
    gmax = functools.reduce(jnp.maximum, g)
    gsel = jnp.full_like(g[0], N_GROUPS - 1).astype(jnp.int32)
    for r in range(N_GROUPS - 2, -1, -1):
        gsel = jnp.where(g[r] == gmax, r, gsel)
    p_group = 1.0 / functools.reduce(lambda a, b: a + b, [jnp.exp(x - gmax) for x in g])
    e = []
    for jj in range(EXPERTS_PER_GROUP):
        v = lg[N_GROUPS + jj:N_GROUPS + jj + 1, :]
        for r in range(1, N_GROUPS):
            row = N_GROUPS + r * EXPERTS_PER_GROUP + jj
            v = jnp.where(gsel == r, lg[row:row + 1, :], v)
        e.append(v)
    v1 = functools.reduce(jnp.maximum, e)
    i1 = jnp.full_like(gsel, EXPERTS_PER_GROUP - 1)
    for jj in range(EXPERTS_PER_GROUP - 2, -1, -1):
        i1 = jnp.where(e[jj] == v1, jj, i1)
    rest = [jnp.where(i1 == jj, -jnp.inf, e[jj]) for jj in range(EXPERTS_PER_GROUP)]
    v2 = functools.reduce(jnp.maximum, rest)
    i2 = jnp.full_like(gsel, EXPERTS_PER_GROUP - 1)
    for jj in range(EXPERTS_PER_GROUP - 2, -1, -1):
        i2 = jnp.where(jnp.logical_and(rest[jj] == v2, i1 != jj), jj, i2)
    t = jnp.exp(v2 - v1)
    w1 = p_group / (1.0 + t)
    w2 = p_group * t / (1.0 + t)
    lo = jnp.minimum(i1, i2)
    hi = jnp.maximum(i1, i2)
    pair = jnp.where(lo == 0, hi - 1, jnp.where(lo == 1, jnp.where(hi == 3, 3, 4), 5))
    bucket = gsel * N_PAIRS + pair
    w_of_lo = jnp.where(i1 < i2, w1, w2)
    w_of_hi = jnp.where(i1 < i2, w2, w1)
    swapped = pair == 5
    w_a = jnp.where(swapped, w_of_hi, w_of_lo)
    w_b = jnp.where(swapped, w_of_lo, w_of_hi)

    tl = lg.shape[1]
    bid = lax.broadcasted_iota(jnp.int32, (N_BUCKET_ROWS, tl), 0)
    hot = bid == bucket
    hot_b = jnp.where(hot, 1.0, 0.0).astype(BF16)
    prefix = jnp.dot(hot_b, tri_ref[...], preferred_element_type=F32)
    carry = carry_sc[...]
    before = prefix + jnp.concatenate([carry] * (tl // LANES), axis=1)
    rank = jnp.sum(jnp.where(hot, before, 0.0), axis=0, keepdims=True)
    total = carry + jnp.dot(hot_b, jnp.ones((tl, LANES), BF16), preferred_element_type=F32)
    carry_sc[...] = total
    cnt_ref[...] = total.astype(jnp.int32)
    pad_i = jnp.zeros((6, tl), jnp.int32)
    ids_ref[...] = jnp.concatenate([bucket, rank.astype(jnp.int32), pad_i], axis=0)
    wts_ref[...] = jnp.concatenate([w_a, w_b, pad_i.astype(F32)], axis=0)


def _route(logt, tl):
    n_rt, n = logt.shape
    tri = jnp.asarray(np.triu(np.ones((tl, tl), np.float32), k=1), dtype=BF16)
    return pl.pallas_call(
        _route_kernel,
        grid=(n // tl,),
        in_specs=[pl.BlockSpec((n_rt, tl), lambda i: (0, i)),
                  pl.BlockSpec((tl, tl), lambda i: (0, 0))],
        out_specs=[pl.BlockSpec((8, tl), lambda i: (0, i)),
                   pl.BlockSpec((8, tl), lambda i: (0, i)),
                   pl.BlockSpec((N_BUCKET_ROWS, LANES), lambda i: (0, 0))],
        out_shape=[jax.ShapeDtypeStruct((8, n), jnp.int32),
                   jax.ShapeDtypeStruct((8, n), F32),
                   jax.ShapeDtypeStruct((N_BUCKET_ROWS, LANES), jnp.int32)],
        scratch_shapes=[pltpu.VMEM((N_BUCKET_ROWS, LANES), F32)],
        compiler_params=_params("arbitrary"),
        name="route",
    )(logt, tri)


def _token_copy(src_hbm, src_token, dst_buf, dst_row, sem):
    return pltpu.make_async_copy(
        src_hbm.at[pl.ds(pl.multiple_of(src_token * TOKEN_ROWS, TOKEN_ROWS), TOKEN_ROWS), :],
        dst_buf.at[pl.ds(pl.multiple_of(dst_row * TOKEN_ROWS, TOKEN_ROWS), TOKEN_ROWS), :],
        sem)


def _start_gather(idx_ref, first, n_groups, src_hbm, dst_buf, sem):
    def body(rr, carry):
        for uu in range(GATHER_UNROLL):
            r = rr * GATHER_UNROLL + uu
            _token_copy(src_hbm, idx_ref[first + r], dst_buf, r, sem).start(priority=uu % 2)
        return carry

    lax.fori_loop(0, n_groups, body, 0)


def _start_padded_gather(idx_ref, first, n_valid, rows, src_hbm, dst_buf, sem):
    n_groups = (n_valid + GATHER_UNROLL - 1) // GATHER_UNROLL
    _start_gather(idx_ref, first, n_groups, src_hbm, dst_buf, sem)
    pad_groups = rows // GATHER_UNROLL - n_groups
    off = n_groups * GATHER_UNROLL
    size = rows // 2
    while size >= GATHER_UNROLL:
        take = jnp.bitwise_and(pad_groups, size // GATHER_UNROLL) != 0

        @pl.when(take)
        def _(off=off, size=size):
            pltpu.make_async_copy(
                src_hbm.at[pl.ds(0, size * TOKEN_ROWS), :],
                dst_buf.at[pl.ds(pl.multiple_of(off * TOKEN_ROWS, GATHER_UNROLL * TOKEN_ROWS),
                                 size * TOKEN_ROWS), :],
                sem).start()

        off = off + jnp.where(take, size, 0)
        size //= 2


def _wait_gather(rows, src_hbm, dst_buf, sem):
    pltpu.make_async_copy(src_hbm.at[pl.ds(0, rows * TOKEN_ROWS), :], dst_buf, sem).wait()


def _moe_kernel(ea_ref, eb_ref, new_a_ref, new_b_ref, nvalid_ref, nused_ref, src_ref, h2c_hbm, wts_ref,
                g2_ref, wg_a_ref, wu_a_ref, wd_a_ref, wg_b_ref, wu_b_ref, wd_b_ref, yc_ref,
                xbuf, wg_a_sc, wu_a_sc, wd_a_sc, wg_b_sc, wu_b_sc, wd_b_sc, sem, *, tme):
    i = pl.program_id(0)
    n_used = nused_ref[0]
    slot = jnp.bitwise_and(i, 1)

    @pl.when(i == 0)
    def _():
        _start_padded_gather(src_ref, 0, nvalid_ref[0], tme, h2c_hbm, xbuf.at[0], sem.at[0])

    @pl.when(i + 1 < n_used)
    def _():
        _start_padded_gather(src_ref, (i + 1) * tme, nvalid_ref[i + 1], tme, h2c_hbm,
                             xbuf.at[1 - slot], sem.at[1 - slot])

    @pl.when(new_a_ref[i] == 1)
    def _():
        for src, dst in ((wg_a_ref, wg_a_sc), (wu_a_ref, wu_a_sc), (wd_a_ref, wd_a_sc)):
            dst[...] = src[0].astype(BF16)

    @pl.when(new_b_ref[i] == 1)
    def _():
        for src, dst in ((wg_b_ref, wg_b_sc), (wu_b_ref, wu_b_sc), (wd_b_ref, wd_b_sc)):
            dst[...] = src[0].astype(BF16)

    @pl.when(i < n_used)
    def _():
        _wait_gather(tme, h2c_hbm, xbuf.at[slot], sem.at[slot])
        x = _load_token_major(xbuf.at[slot], tme)
        ms = jnp.mean(x * x, axis=-1, keepdims=True)
        hn = (x * lax.rsqrt(ms + EPS) * g2_ref[...]).astype(BF16)
        wts = wts_ref[...]

        def expert_act(wg_sc, wu_sc, w):
            gate = jnp.dot(hn, wg_sc[...], preferred_element_type=F32)
            up = jnp.dot(hn, wu_sc[...], preferred_element_type=F32)
            return (gate / (1.0 + jnp.exp(-gate)) * up * w).astype(BF16)

        y = (jnp.dot(expert_act(wg_a_sc, wu_a_sc, wts[:, 0:1]), wd_a_sc[...], preferred_element_type=F32)
             + jnp.dot(expert_act(wg_b_sc, wu_b_sc, wts[:, 1:2]), wd_b_sc[...], preferred_element_type=F32))
        _store_token_major(yc_ref, y, tme)

    @pl.when(i >= n_used)
    def _():
        yc_ref[...] = jnp.zeros_like(yc_ref)


def _moe(ea, eb, new_a, new_b, nvalid, nused, src, h2c, wts_sorted, g2, wg, wu, wd, tme, n_tiles):
    used = lambda i, ea, eb, na, nb, nv, nu, src: (jnp.minimum(i, nu[0] - 1), 0)
    w_a = lambda i, ea, eb, na, nb, nv, nu, src: (ea[i], 0, 0)
    w_b = lambda i, ea, eb, na, nb, nv, nu, src: (eb[i], 0, 0)
    up_shape, down_shape = (D_MODEL, D_EXPERT), (D_EXPERT, D_MODEL)
    grid_spec = pltpu.PrefetchScalarGridSpec(
        num_scalar_prefetch=7,
        grid=(n_tiles,),
        in_specs=[
            pl.BlockSpec(memory_space=pl.ANY),
            pl.BlockSpec((tme, 2), used),
            pl.BlockSpec((1, D_MODEL), lambda i, ea, eb, na, nb, nv, nu, src: (0, 0)),
            pl.BlockSpec((1,) + up_shape, w_a),
            pl.BlockSpec((1,) + up_shape, w_a),
            pl.BlockSpec((1,) + down_shape, w_a),
            pl.BlockSpec((1,) + up_shape, w_b),
            pl.BlockSpec((1,) + up_shape, w_b),
            pl.BlockSpec((1,) + down_shape, w_b),
        ],
        out_specs=pl.BlockSpec((tme * TOKEN_ROWS, LANES), lambda i, ea, eb, na, nb, nv, nu, src: (i, 0)),
        scratch_shapes=[pltpu.VMEM((2, tme * TOKEN_ROWS, LANES), F32)]
        + [pltpu.VMEM(s, BF16) for s in (up_shape, up_shape, down_shape) * 2]
        + [pltpu.SemaphoreType.DMA((2,))],
    )
    return pl.pallas_call(
        functools.partial(_moe_kernel, tme=tme),
        grid_spec=grid_spec,
        out_shape=jax.ShapeDtypeStruct((n_tiles * tme * TOKEN_ROWS, LANES), F32),
        compiler_params=_params("arbitrary"),
        name="moe",
    )(ea, eb, new_a, new_b, nvalid, nused, src, h2c, wts_sorted, g2, wg, wu, wd, wg, wu, wd)


def _final_kernel(dest_ref, h2c_ref, yc_hbm, gf_ref, o_ref, ybuf, sem, *, tmf):
    i = pl.program_id(0)
    slot = jnp.bitwise_and(i, 1)

    @pl.when(i == 0)
    def _():
        _start_gather(dest_ref, 0, tmf // GATHER_UNROLL, yc_hbm, ybuf.at[0], sem.at[0])

    @pl.when(i + 1 < pl.num_programs(0))
    def _():
        _start_gather(dest_ref, (i + 1) * tmf, tmf // GATHER_UNROLL, yc_hbm, ybuf.at[1 - slot],
                      sem.at[1 - slot])

    _wait_gather(tmf, yc_hbm, ybuf.at[slot], sem.at[slot])
    h = _load_token_major(h2c_ref, tmf) + _load_token_major(ybuf.at[slot], tmf)
    ms = jnp.mean(h * h, axis=-1, keepdims=True)
    o_ref[...] = h * lax.rsqrt(ms + EPS) * gf_ref[...]


def _final(dest, h2c, yc, gf, tmf):
    n = h2c.shape[0] // TOKEN_ROWS
    grid_spec = pltpu.PrefetchScalarGridSpec(
        num_scalar_prefetch=1,
        grid=(n // tmf,),
        in_specs=[
            pl.BlockSpec((tmf * TOKEN_ROWS, LANES), lambda i, dest: (i, 0)),
            pl.BlockSpec(memory_space=pl.ANY),
            pl.BlockSpec((1, D_MODEL), lambda i, dest: (0, 0)),
        ],
        out_specs=pl.BlockSpec((tmf, D_MODEL), lambda i, dest: (i, 0)),
        scratch_shapes=[pltpu.VMEM((2, tmf * TOKEN_ROWS, LANES), F32), pltpu.SemaphoreType.DMA((2,))],
    )
    return pl.pallas_call(
        functools.partial(_final_kernel, tmf=tmf),
        grid_spec=grid_spec,
        out_shape=jax.ShapeDtypeStruct((n, D_MODEL), F32),
        compiler_params=_params("arbitrary"),
        name="final",
    )(dest, h2c, yc, gf)


def _rope_tables(t):
    inv = 1.0 / (ROPE_THETA ** (np.arange(0, HEAD_DIM, 2, dtype=np.float64) / HEAD_DIM))
    ang = np.arange(t, dtype=np.float64)[:, None] * inv[None, :]
    ang = np.concatenate([ang, ang, ang, ang], axis=-1)
    sign = np.where((np.arange(LANES) % HEAD_DIM) < HEAD_DIM // 2, -1.0, 1.0)
    return (jnp.asarray(np.cos(ang), dtype=F32), jnp.asarray(np.sin(ang) * sign[None, :], dtype=F32))


def kernel(x, meta, norm1_g, w_in, b_gate, lambda_q1, lambda_k1, lambda_q2, lambda_k2, subln_g, pool_w,
           pool_scale, w_attn_br, w_pool_br, w_out, norm2_g, w_router_group, b_router_group,
           w_router_expert, b_router_expert, w_e_gate, w_e_up, w_e_down, final_g):
    batch, seq, d = x.shape
    n = batch * seq
    x2 = x.reshape(n, d)
    cos, sin = _rope_tables(seq + N_META)
    q, k, v, u, gates = _inproj(x2, norm1_g, cos[N_META:], sin[N_META:], w_in[0], b_gate, tm=1024)
    _, km, vm, um, _ = _inproj(meta, norm1_g, cos[:N_META], sin[:N_META], w_in[0], b_gate, tm=N_META)
    pad = ((0, LANES - N_META), (0, 0))
    a = _attention(q, k, v, jnp.pad(km, pad), jnp.pad(vm, pad), lambda_q1, lambda_k1, lambda_q2,
                   lambda_k2, subln_g, batch, seq, bq=256, group=8)

    n_rt = 32
    wr_t = jnp.concatenate([w_router_group[0], w_router_expert[0]], axis=1).T
    wr_t = jnp.pad(wr_t, ((0, n_rt - wr_t.shape[0]), (0, 0)))
    br_t = jnp.concatenate([b_router_group[0], b_router_expert[0]])
    br_t = jnp.pad(br_t, (0, n_rt - br_t.shape[0])).reshape(n_rt, 1)
    h2c, logt = _mix(x2, a, u, um, gates, w_attn_br[0].astype(BF16), pool_w[0].astype(BF16),
                     pool_scale, w_pool_br[0].astype(BF16), w_out[0].astype(BF16), norm2_g,
                     wr_t, br_t, seq, tm=512)
    ids, wts, cnt = _route(logt, tl=1024)

    tme = 256
    n_tiles = n // tme + N_BUCKETS
    bucket, rank = ids[0], ids[1]
    tiles_per_bucket = (cnt[:N_BUCKETS, 0] + tme - 1) // tme
    tile_end = jnp.cumsum(tiles_per_bucket)
    n_used = tile_end[-1]
    dest = (tile_end - tiles_per_bucket)[bucket] * tme + rank
    per_token = jnp.concatenate([wts[0:2], jnp.arange(n, dtype=F32)[None, :]], axis=0).T
    per_row = jnp.zeros((n_tiles * tme, 3), F32).at[dest].set(per_token)
    wts_sorted = per_row[:, 0:2]
    src = per_row[:, 2].astype(jnp.int32)
    last_tile = jnp.minimum(jnp.arange(n_tiles), n_used - 1)
    tile_bucket = jnp.sum((tile_end[None, :] <= last_tile[:, None]).astype(jnp.int32), axis=1)
    tile_bucket = jnp.minimum(tile_bucket, N_BUCKETS - 1)
    group, pair = tile_bucket // N_PAIRS, tile_bucket % N_PAIRS
    e_a = (group * EXPERTS_PER_GROUP + jnp.asarray(_PAIR_A, jnp.int32)[pair]).astype(jnp.int32)
    e_b = (group * EXPERTS_PER_GROUP + jnp.asarray(_PAIR_B, jnp.int32)[pair]).astype(jnp.int32)
    changed = lambda e: jnp.concatenate([jnp.ones((1,), jnp.int32), (e[1:] != e[:-1]).astype(jnp.int32)])
    tile_in_bucket = jnp.arange(n_tiles) - (tile_end - tiles_per_bucket)[tile_bucket]
    n_valid = jnp.clip(cnt[:N_BUCKETS, 0][tile_bucket] - tile_in_bucket * tme, 0, tme).astype(jnp.int32)

    yc = _moe(e_a, e_b, changed(e_a), changed(e_b), n_valid, n_used.reshape(1).astype(jnp.int32), src,
              h2c, wts_sorted, norm2_g, w_e_gate[0], w_e_up[0], w_e_down[0], tme, n_tiles)
    out = _final(dest.astype(jnp.int32), h2c, yc, final_g.reshape(1, d), tmf=512)
    return out.reshape(batch, seq, d)
```

```python
import functools
import math

import jax
import jax.numpy as jnp
import numpy as np
from jax import lax
from jax.experimental import pallas as pl
from jax.experimental.pallas import tpu as pltpu

D_MODEL = 1024
N_META = 16
N_HEADS = 8
HEAD_DIM = 64
V_DIM = 2 * HEAD_DIM
POOL_WINDOWS = (2, 4, 8, 16)
N_POOL_GROUPS = len(POOL_WINDOWS)
POOL_GROUP_DIM = 128
POOL_WIDTH = N_POOL_GROUPS * POOL_GROUP_DIM
ROPE_THETA = 10000.0
N_GROUPS = 4
EXPERTS_PER_GROUP = 4
N_EXPERTS = N_GROUPS * EXPERTS_PER_GROUP
D_EXPERT = 512
EPS = 1e-6
LAMBDA_INIT = 0.8 - 0.6 * math.exp(-0.3 * 0)
LOG2_E = math.log2(math.e)

_PAIR_A = (0, 0, 0, 1, 1, 3)
_PAIR_B = (1, 2, 3, 3, 2, 2)
N_PAIRS = len(_PAIR_A)
N_BUCKETS = N_GROUPS * N_PAIRS
N_BUCKET_ROWS = 32
GATHER_UNROLL = 8

LANES = 128
NEG_BIG = -1e30
VMEM_LIMIT = 48 * 1024 * 1024

F32 = jnp.float32
BF16 = jnp.bfloat16

_HEADS_PER_STEP = 2
_IN_STEPS = N_HEADS // _HEADS_PER_STEP
_QKV_BLK = _HEADS_PER_STEP * V_DIM
_GATE_BLK = 2 * D_MODEL // _IN_STEPS
_O_K = D_MODEL // _QKV_BLK
_O_V = 2 * D_MODEL // _QKV_BLK
_O_U = 3 * D_MODEL // POOL_GROUP_DIM
_O_G = (3 * D_MODEL + POOL_WIDTH) // _GATE_BLK


def _params(*sem):
    return pltpu.CompilerParams(dimension_semantics=sem, vmem_limit_bytes=VMEM_LIMIT)


TOKEN_ROWS = D_MODEL // LANES


def _store_token_major(ref, val, rows):
    for s in range(TOKEN_ROWS):
        ref[pl.ds(s, rows, stride=TOKEN_ROWS), :] = val[:, s * LANES:(s + 1) * LANES]


def _load_token_major(ref, rows):
    return jnp.concatenate([ref[pl.ds(s, rows, stride=TOKEN_ROWS), :] for s in range(TOKEN_ROWS)], axis=1)


def _rope(z, cos, sin_signed, first_half):
    outs = []
    for c in range(z.shape[1] // LANES):
        zc = z[:, c * LANES:(c + 1) * LANES]
        rot = jnp.where(first_half, pltpu.roll(zc, LANES - HEAD_DIM // 2, 1),
                        pltpu.roll(zc, HEAD_DIM // 2, 1))
        outs.append(zc * cos + rot * sin_signed)
    return jnp.concatenate(outs, axis=1)


def _inproj_kernel(x_ref, g_ref, cos_ref, sin_ref, wq_ref, wk_ref, wv_ref, wu_ref, wg_ref, bg_ref,
                   q_ref, k_ref, v_ref, u_ref, gate_ref, wq_sc, wk_sc, wv_sc, wu_sc, wg_sc):
    @pl.when(pl.program_id(1) == 0)
    def _():
        for src, dst in ((wq_ref, wq_sc), (wk_ref, wk_sc), (wv_ref, wv_sc), (wu_ref, wu_sc),
                         (wg_ref, wg_sc)):
            dst[...] = src[...].astype(BF16)

    x = x_ref[...]
    ms = jnp.mean(x * x, axis=-1, keepdims=True)
    hn = (x * lax.rsqrt(ms + EPS) * g_ref[...]).astype(BF16)
    cos = cos_ref[...]
    sin = sin_ref[...]
    lane = lax.broadcasted_iota(jnp.int32, cos.shape, 1)
    first_half = jnp.bitwise_and(lane, HEAD_DIM - 1) < (HEAD_DIM // 2)
    zg = jnp.dot(hn, wg_sc[...], preferred_element_type=F32) + bg_ref[...]
    gate_ref[...] = (1.0 / (1.0 + jnp.exp(-zg))).astype(BF16)
    zq = jnp.dot(hn, wq_sc[...], preferred_element_type=F32)
    q_ref[...] = (_rope(zq, cos, sin, first_half) * (LOG2_E / math.sqrt(HEAD_DIM))).astype(BF16)
    zk = jnp.dot(hn, wk_sc[...], preferred_element_type=F32)
    k_ref[...] = _rope(zk, cos, sin, first_half).astype(BF16)
    u_ref[...] = jnp.dot(hn, wu_sc[...], preferred_element_type=F32)
    v_ref[...] = jnp.dot(hn, wv_sc[...], preferred_element_type=F32).astype(BF16)


def _inproj(x2, g1, cos, sin, w_in, b_gate, tm):
    n = x2.shape[0]
    n_pos_blocks = cos.shape[0] // tm
    row = lambda j, i: (i, 0)
    pos = lambda j, i: (i % n_pos_blocks, 0)
    return pl.pallas_call(
        _inproj_kernel,
        grid=(_IN_STEPS, n // tm),
        in_specs=[
            pl.BlockSpec((tm, D_MODEL), row),
            pl.BlockSpec((1, D_MODEL), lambda j, i: (0, 0)),
            pl.BlockSpec((tm, LANES), pos),
            pl.BlockSpec((tm, LANES), pos),
            pl.BlockSpec((D_MODEL, _QKV_BLK), lambda j, i: (0, j)),
            pl.BlockSpec((D_MODEL, _QKV_BLK), lambda j, i: (0, _O_K + j)),
            pl.BlockSpec((D_MODEL, _QKV_BLK), lambda j, i: (0, _O_V + j)),
            pl.BlockSpec((D_MODEL, POOL_GROUP_DIM), lambda j, i: (0, _O_U + j)),
            pl.BlockSpec((D_MODEL, _GATE_BLK), lambda j, i: (0, _O_G + j)),
            pl.BlockSpec((1, _GATE_BLK), lambda j, i: (0, j)),
        ],
        out_specs=[
            pl.BlockSpec((tm, _QKV_BLK), lambda j, i: (i, j)),
            pl.BlockSpec((tm, _QKV_BLK), lambda j, i: (i, j)),
            pl.BlockSpec((tm, _QKV_BLK), lambda j, i: (i, j)),
            pl.BlockSpec((tm, POOL_GROUP_DIM), lambda j, i: (i, j)),
            pl.BlockSpec((tm, _GATE_BLK), lambda j, i: (i, j)),
        ],
        out_shape=[
            jax.ShapeDtypeStruct((n, D_MODEL), BF16),
            jax.ShapeDtypeStruct((n, D_MODEL), BF16),
            jax.ShapeDtypeStruct((n, D_MODEL), BF16),
            jax.ShapeDtypeStruct((n, POOL_WIDTH), F32),
            jax.ShapeDtypeStruct((n, 2 * D_MODEL), BF16),
        ],
        scratch_shapes=[
            pltpu.VMEM((D_MODEL, _QKV_BLK), BF16),
            pltpu.VMEM((D_MODEL, _QKV_BLK), BF16),
            pltpu.VMEM((D_MODEL, _QKV_BLK), BF16),
            pltpu.VMEM((D_MODEL, POOL_GROUP_DIM), BF16),
            pltpu.VMEM((D_MODEL, _GATE_BLK), BF16),
        ],
        compiler_params=_params("arbitrary", "arbitrary"),
        name="inproj",
    )(x2, g1, cos, sin, w_in, w_in, w_in, w_in, w_in, b_gate)


def _attn_kernel(lq1_ref, lk1_ref, lq2_ref, lk2_ref, subg_ref, q_ref, k_ref, v_ref, qm_ref, km_ref,
                 vm_ref, o_ref, *state_sc, bq):
    seq = k_ref.shape[0]
    nq = seq // bq
    states = [tuple(state_sc[3 * g:3 * g + 3]) for g in range(nq + 1)]

    def rows_at(ref, meta_ref, p0, n):
        if p0 == 0:
            return jnp.concatenate([meta_ref[...], ref[0:n - N_META, :]], axis=0)
        real = ref[p0 - N_META:min(p0 - N_META + n, seq), :]
        if real.shape[0] < n:
            real = jnp.concatenate([real, jnp.zeros((n - real.shape[0], real.shape[1]), real.dtype)], axis=0)
        return real

    def block_rows(i):
        return (i * bq, bq, bq) if i < nq else (nq * bq, N_META, LANES)

    lam = (jnp.exp(jnp.sum(lq1_ref[...] * lk1_ref[...], axis=-1, keepdims=True))
           - jnp.exp(jnp.sum(lq2_ref[...] * lk2_ref[...], axis=-1, keepdims=True)) + LAMBDA_INIT)

    def causal_bias(n_rows, n_cols):
        r = lax.broadcasted_iota(jnp.int32, (n_rows, n_cols), 0)
        c = lax.broadcasted_iota(jnp.int32, (n_rows, n_cols), 1)
        bias = jnp.where(c <= r, 0.0, NEG_BIG)
        return jnp.concatenate([bias, bias], axis=0)

    biases = {bq: causal_bias(bq, bq), N_META: causal_bias(N_META, LANES)}

    def load_q(i, qq_sc):
        p0, n, _ = block_rows(i)
        q = rows_at(q_ref, qm_ref, p0, n)
        lane = lax.broadcasted_iota(jnp.int32, q.shape, 1)
        zero = jnp.zeros_like(q)
        qq_sc[0:n, :] = jnp.where(lane < HEAD_DIM, q, zero)
        qq_sc[n:2 * n, :] = jnp.where(lane >= HEAD_DIM, q, zero)

    def step(state, kb, vb, bias):
        qq_sc, m_sc, acc_sc = state
        first = bias is not None
        s = lax.dot_general(qq_sc[...], kb, (((1,), (1,)), ((), ())), preferred_element_type=F32)
        if first:
            s = s + bias
        n_tiles = s.shape[1] // LANES
        rm = s[:, 0:LANES]
        for t in range(1, n_tiles):
            rm = jnp.maximum(rm, s[:, t * LANES:(t + 1) * LANES])
        rmax = jnp.max(rm, axis=-1, keepdims=True)
        if first:
            m_new = jnp.broadcast_to(rmax, (s.shape[0], LANES))
        else:
            m_prev = m_sc[...]
            m_new = jnp.maximum(m_prev, rmax)
        p = jnp.exp2(s - jnp.concatenate([m_new] * n_tiles, axis=1))
        pv = jnp.dot(p.astype(BF16), vb, preferred_element_type=F32)
        if first:
            acc_sc[...] = pv
        else:
            alpha = jnp.exp2(m_prev - m_new)
            acc_sc[...] = jnp.concatenate([alpha, alpha], axis=1) * acc_sc[...] + pv
        m_sc[...] = m_new

    def key_step(state, i, j):
        p0, n, width = block_rows(i) if j == 0 else ((j - 1) * bq, None, bq)
        kb = rows_at(k_ref, km_ref, p0, width)
        vb = jnp.concatenate([rows_at(v_ref, vm_ref, p0, width), jnp.ones((width, V_DIM), BF16)], axis=1)
        step(state, kb, vb, biases[n] if j == 0 else None)

    def finish(i, acc_sc):
        p0, n, _ = block_rows(i)
        o1 = acc_sc[0:n, 0:V_DIM] / acc_sc[0:n, V_DIM:2 * V_DIM]
        o2 = acc_sc[n:2 * n, 0:V_DIM] / acc_sc[n:2 * n, V_DIM:2 * V_DIM]
        o = o1 - lam * o2
        ms = jnp.mean(o * o, axis=-1, keepdims=True)
        o = (o * lax.rsqrt(ms + EPS) * subg_ref[...] * (1.0 - LAMBDA_INIT)).astype(BF16)
        if p0 == 0:
            o_ref[0:n - N_META, :] = o[N_META:n, :]
        else:
            o_ref[p0 - N_META:p0 - N_META + n, :] = o

    order = []
    for p in range((nq + 2) // 2):
        order += [p] if p == nq - p else [p, nq - p]
    for i in order:
        load_q(i, states[i][0])
    for j in range(nq + 1):
        for i in order:
            if j <= i:
                key_step(states[i], i, j)
    for i in order:
        finish(i, states[i][2])


def _attention(q, k, v, qm, km, vm, lq1, lk1, lq2, lk2, subg, batch, seq, bq):
    nq = seq // bq
    small = lambda b, h: (0, 0)
    head = lambda b, h: (b, h)
    meta = lambda b, h: (0, h)

    def state_scratch(rows):
        return [pltpu.VMEM((2 * rows, V_DIM), BF16), pltpu.VMEM((2 * rows, LANES), F32),
                pltpu.VMEM((2 * rows, 2 * V_DIM), F32)]

    return pl.pallas_call(
        functools.partial(_attn_kernel, bq=bq),
        grid=(batch, N_HEADS),
        in_specs=[
            pl.BlockSpec((1, HEAD_DIM), small),
            pl.BlockSpec((1, HEAD_DIM), small),
            pl.BlockSpec((1, HEAD_DIM), small),
            pl.BlockSpec((1, HEAD_DIM), small),
            pl.BlockSpec((1, V_DIM), small),
            pl.BlockSpec((seq, V_DIM), head),
            pl.BlockSpec((seq, V_DIM), head),
            pl.BlockSpec((seq, V_DIM), head),
            pl.BlockSpec((N_META, V_DIM), meta),
            pl.BlockSpec((N_META, V_DIM), meta),
            pl.BlockSpec((N_META, V_DIM), meta),
        ],
        out_specs=pl.BlockSpec((seq, V_DIM), head),
        out_shape=jax.ShapeDtypeStruct(q.shape, BF16),
        scratch_shapes=state_scratch(bq) * nq + state_scratch(N_META),
        compiler_params=_params("arbitrary", "arbitrary"),
        name="diff_attn",
    )(lq1, lk1, lq2, lk2, subg, q, k, v, qm, km, vm)


def _mix_kernel(x_ref, a_ref, u_ref, uprev_ref, umeta_ref, gate_ref, wa_ref, pw_ref, ps_ref, wp_ref,
                wo_ref, g2_ref, wr_ref, br_ref, h2c_ref, logt_ref, ext_sc, *, tm, tiles_per_seq):
    i = pl.program_id(0)
    first = (i % tiles_per_seq) == 0

    @pl.when(first)
    def _():
        ext_sc[0:N_META, :] = umeta_ref[...]

    @pl.when(jnp.logical_not(first))
    def _():
        ext_sc[0:N_META, :] = uprev_ref[...]

    ext_sc[N_META:N_META + tm, :] = u_ref[...]

    pooled = []
    for g, w in enumerate(POOL_WINDOWS):
        cs = slice(g * POOL_GROUP_DIM, (g + 1) * POOL_GROUP_DIM)
        cur = ext_sc[N_META:N_META + tm, cs]
        tot = cur
        for kk in range(1, w):
            tot = tot + ext_sc[N_META - kk:N_META - kk + tm, cs]
        y = (tot * (1.0 / w) - cur).astype(BF16)
        pooled.append(jnp.dot(y, pw_ref[g], preferred_element_type=F32))
    yp = (jnp.concatenate(pooled, axis=1) * ps_ref[...]).astype(BF16)
    y_pool = jnp.dot(yp, wp_ref[...], preferred_element_type=F32)
    y_attn = jnp.dot(a_ref[...], wa_ref[...], preferred_element_type=F32)
    mixed = (gate_ref[:, 0:D_MODEL].astype(F32) * y_attn
             + gate_ref[:, D_MODEL:2 * D_MODEL].astype(F32) * y_pool)
    h2 = x_ref[...] + jnp.dot(mixed.astype(BF16), wo_ref[...], preferred_element_type=F32)
    _store_token_major(h2c_ref, h2, tm)
    ms = jnp.mean(h2 * h2, axis=-1, keepdims=True)
    hn2 = h2 * lax.rsqrt(ms + EPS) * g2_ref[...]
    hn2_hi = hn2.astype(BF16)
    hn2_lo = (hn2 - hn2_hi.astype(F32)).astype(BF16)
    wr = wr_ref[...]
    wr_hi = wr.astype(BF16)
    wr_lo = (wr - wr_hi.astype(F32)).astype(BF16)
    nt = (((1,), (1,)), ((), ()))
    logt = (lax.dot_general(wr_hi, hn2_hi, nt, preferred_element_type=F32)
            + lax.dot_general(wr_hi, hn2_lo, nt, preferred_element_type=F32)
            + lax.dot_general(wr_lo, hn2_hi, nt, preferred_element_type=F32))
    logt_ref[...] = logt + br_ref[...]


def _mix(x2, a, u, umeta, gates, w_attn, pool_w, pool_scale, w_pool, w_out, g2, wr_t, br_t, seq, tm):
    n = x2.shape[0]
    tiles_per_seq = seq // tm
    halo_blocks = tm // N_META
    const2 = lambda i: (0, 0)
    n_rt = wr_t.shape[0]
    return pl.pallas_call(
        functools.partial(_mix_kernel, tm=tm, tiles_per_seq=tiles_per_seq),
        grid=(n // tm,),
        in_specs=[
            pl.BlockSpec((tm, D_MODEL), lambda i: (i, 0)),
            pl.BlockSpec((tm, D_MODEL), lambda i: (i, 0)),
            pl.BlockSpec((tm, POOL_WIDTH), lambda i: (i, 0)),
            pl.BlockSpec((N_META, POOL_WIDTH), lambda i: (jnp.maximum(i * halo_blocks - 1, 0), 0)),
            pl.BlockSpec((N_META, POOL_WIDTH), const2),
            pl.BlockSpec((tm, 2 * D_MODEL), lambda i: (i, 0)),
            pl.BlockSpec((D_MODEL, D_MODEL), const2),
            pl.BlockSpec((N_POOL_GROUPS, POOL_GROUP_DIM, POOL_GROUP_DIM), lambda i: (0, 0, 0)),
            pl.BlockSpec((1, POOL_WIDTH), const2),
            pl.BlockSpec((POOL_WIDTH, D_MODEL), const2),
            pl.BlockSpec((D_MODEL, D_MODEL), const2),
            pl.BlockSpec((1, D_MODEL), const2),
            pl.BlockSpec((n_rt, D_MODEL), const2),
            pl.BlockSpec((n_rt, 1), const2),
        ],
        out_specs=[
            pl.BlockSpec((tm * TOKEN_ROWS, LANES), lambda i: (i, 0)),
            pl.BlockSpec((n_rt, tm), lambda i: (0, i)),
        ],
        out_shape=[
            jax.ShapeDtypeStruct((n * TOKEN_ROWS, LANES), F32),
            jax.ShapeDtypeStruct((n_rt, n), F32),
        ],
        scratch_shapes=[pltpu.VMEM((tm + N_META, POOL_WIDTH), F32)],
        compiler_params=_params("arbitrary"),
        name="mix",
    )(x2, a, u, u, umeta, gates, w_attn, pool_w, pool_scale, w_pool, w_out, g2, wr_t, br_t)


def _route_kernel(logt_ref, tri_ref, ids_ref, wts_ref, cnt_ref, carry_sc):
    @pl.when(pl.program_id(0) == 0)
    def _():
        carry_sc[...] = jnp.zeros_like(carry_sc)

    lg = logt_ref[...]
    g = [lg[r:r + 1, :] for r in range(N_GROUPS)]
    gmax = functools.reduce(jnp.maximum, g)
    gsel = jnp.full_like(g[0], N_GROUPS - 1).astype(jnp.int32)
    for r in range(N_GROUPS - 2, -1, -1):
        gsel = jnp.where(g[r] == gmax, r, gsel)
    p_group = 1.0 / functools.reduce(lambda a, b: a + b, [jnp.exp(x - gmax) for x in g])
    e = []
    for jj in range(EXPERTS_PER_GROUP):
        v = lg[N_GROUPS + jj:N_GROUPS + jj + 1, :]
        for r in range(1, N_GROUPS):
            row = N_GROUPS + r * EXPERTS_PER_GROUP + jj
            v = jnp.where(gsel == r, lg[row:row + 1, :], v)
        e.append(v)
    v1 = functools.reduce(jnp.maximum, e)
    i1 = jnp.full_like(gsel, EXPERTS_PER_GROUP - 1)
    for jj in range(EXPERTS_PER_GROUP - 2, -1, -1):
        i1 = jnp.where(e[jj] == v1, jj, i1)
    rest = [jnp.where(i1 == jj, -jnp.inf, e[jj]) for jj in range(EXPERTS_PER_GROUP)]
    v2 = functools.reduce(jnp.maximum, rest)
    i2 = jnp.full_like(gsel, EXPERTS_PER_GROUP - 1)
    for jj in range(EXPERTS_PER_GROUP - 2, -1, -1):
        i2 = jnp.where(jnp.logical_and(rest[jj] == v2, i1 != jj), jj, i2)
    t = jnp.exp(v2 - v1)
    w1 = p_group / (1.0 + t)
    w2 = p_group * t / (1.0 + t)
    lo = jnp.minimum(i1, i2)
    hi = jnp.maximum(i1, i2)
    pair = jnp.where(lo == 0, hi - 1, jnp.where(lo == 1, jnp.where(hi == 3, 3, 4), 5))
    bucket = gsel * N_PAIRS + pair
    w_of_lo = jnp.where(i1 < i2, w1, w2)
    w_of_hi = jnp.where(i1 < i2, w2, w1)
    swapped = pair == 5
    w_a = jnp.where(swapped, w_of_hi, w_of_lo)
    w_b = jnp.where(swapped, w_of_lo, w_of_hi)

    tl = lg.shape[1]
    bid = lax.broadcasted_iota(jnp.int32, (N_BUCKET_ROWS, tl), 0)
    hot = bid == bucket
    hot_b = jnp.where(hot, 1.0, 0.0).astype(BF16)
    prefix = jnp.dot(hot_b, tri_ref[...], preferred_element_type=F32)
    carry = carry_sc[...]
    before = prefix + jnp.concatenate([carry] * (tl // LANES), axis=1)
    rank = jnp.sum(jnp.where(hot, before, 0.0), axis=0, keepdims=True)
    total = carry + jnp.dot(hot_b, jnp.ones((tl, LANES), BF16), preferred_element_type=F32)
    carry_sc[...] = total
    cnt_ref[...] = total.astype(jnp.int32)
    pad_i = jnp.zeros((6, tl), jnp.int32)
    ids_ref[...] = jnp.concatenate([bucket, rank.astype(jnp.int32), pad_i], axis=0)
    wts_ref[...] = jnp.concatenate([w_a, w_b, pad_i.astype(F32)], axis=0)


def _route(logt, tl):
    n_rt, n = logt.shape
    tri = jnp.asarray(np.triu(np.ones((tl, tl), np.float32), k=1), dtype=BF16)
    return pl.pallas_call(
        _route_kernel,
        grid=(n // tl,),
        in_specs=[pl.BlockSpec((n_rt, tl), lambda i: (0, i)),
                  pl.BlockSpec((tl, tl), lambda i: (0, 0))],
        out_specs=[pl.BlockSpec((8, tl), lambda i: (0, i)),
                   pl.BlockSpec((8, tl), lambda i: (0, i)),
                   pl.BlockSpec((N_BUCKET_ROWS, LANES), lambda i: (0, 0))],
        out_shape=[jax.ShapeDtypeStruct((8, n), jnp.int32),
                   jax.ShapeDtypeStruct((8, n), F32),
                   jax.ShapeDtypeStruct((N_BUCKET_ROWS, LANES), jnp.int32)],
        scratch_shapes=[pltpu.VMEM((N_BUCKET_ROWS, LANES), F32)],
        compiler_params=_params("arbitrary"),
        name="route",
    )(logt, tri)


def _token_copy(src_hbm, src_token, dst_buf, dst_row, sem):
    return pltpu.make_async_copy(
        src_hbm.at[pl.ds(pl.multiple_of(src_token * TOKEN_ROWS, TOKEN_ROWS), TOKEN_ROWS), :],
        dst_buf.at[pl.ds(pl.multiple_of(dst_row * TOKEN_ROWS, TOKEN_ROWS), TOKEN_ROWS), :],
        sem)


def _start_gather(idx_ref, first, n_groups, src_hbm, dst_buf, sem):
    def body(rr, carry):
        for uu in range(GATHER_UNROLL):
            r = rr * GATHER_UNROLL + uu
            _token_copy(src_hbm, idx_ref[first + r], dst_buf, r, sem).start(priority=uu % 2)
        return carry

    lax.fori_loop(0, n_groups, body, 0)


def _start_padded_gather(idx_ref, first, n_valid, rows, src_hbm, dst_buf, sem):
    n_groups = (n_valid + GATHER_UNROLL - 1) // GATHER_UNROLL
    _start_gather(idx_ref, first, n_groups, src_hbm, dst_buf, sem)
    pad_groups = rows // GATHER_UNROLL - n_groups
    off = n_groups * GATHER_UNROLL
    size = rows // 2
    while size >= GATHER_UNROLL:
        take = jnp.bitwise_and(pad_groups, size // GATHER_UNROLL) != 0

        @pl.when(take)
        def _(off=off, size=size):
            pltpu.make_async_copy(
                src_hbm.at[pl.ds(0, size * TOKEN_ROWS), :],
                dst_buf.at[pl.ds(pl.multiple_of(off * TOKEN_ROWS, GATHER_UNROLL * TOKEN_ROWS),
                                 size * TOKEN_ROWS), :],
                sem).start()

        off = off + jnp.where(take, size, 0)
        size //= 2


def _wait_gather(rows, src_hbm, dst_buf, sem):
    pltpu.make_async_copy(src_hbm.at[pl.ds(0, rows * TOKEN_ROWS), :], dst_buf, sem).wait()


def _moe_kernel(ea_ref, eb_ref, new_a_ref, new_b_ref, nvalid_ref, nused_ref, src_ref, h2c_hbm, wts_ref,
                g2_ref, wg_a_ref, wu_a_ref, wd_a_ref, wg_b_ref, wu_b_ref, wd_b_ref, yc_ref,
                xbuf, wg_a_sc, wu_a_sc, wd_a_sc, wg_b_sc, wu_b_sc, wd_b_sc, sem, *, tme):
    i = pl.program_id(0)
    n_used = nused_ref[0]
    slot = jnp.bitwise_and(i, 1)

    @pl.when(i == 0)
    def _():
        _start_padded_gather(src_ref, 0, nvalid_ref[0], tme, h2c_hbm, xbuf.at[0], sem.at[0])

    @pl.when(i + 1 < n_used)
    def _():
        _start_padded_gather(src_ref, (i + 1) * tme, nvalid_ref[i + 1], tme, h2c_hbm,
                             xbuf.at[1 - slot], sem.at[1 - slot])

    @pl.when(new_a_ref[i] == 1)
    def _():
        for src, dst in ((wg_a_ref, wg_a_sc), (wu_a_ref, wu_a_sc), (wd_a_ref, wd_a_sc)):
            dst[...] = src[0].astype(BF16)

    @pl.when(new_b_ref[i] == 1)
    def _():
        for src, dst in ((wg_b_ref, wg_b_sc), (wu_b_ref, wu_b_sc), (wd_b_ref, wd_b_sc)):
            dst[...] = src[0].astype(BF16)

    @pl.when(i < n_used)
    def _():
        _wait_gather(tme, h2c_hbm, xbuf.at[slot], sem.at[slot])
        x = _load_token_major(xbuf.at[slot], tme)
        ms = jnp.mean(x * x, axis=-1, keepdims=True)
        hn = (x * lax.rsqrt(ms + EPS) * g2_ref[...]).astype(BF16)
        wts = wts_ref[...]

        def expert_act(wg_sc, wu_sc, w):
            gate = jnp.dot(hn, wg_sc[...], preferred_element_type=F32)
            up = jnp.dot(hn, wu_sc[...], preferred_element_type=F32)
            return (gate / (1.0 + jnp.exp(-gate)) * up * w).astype(BF16)

        y = (jnp.dot(expert_act(wg_a_sc, wu_a_sc, wts[:, 0:1]), wd_a_sc[...], preferred_element_type=F32)
             + jnp.dot(expert_act(wg_b_sc, wu_b_sc, wts[:, 1:2]), wd_b_sc[...], preferred_element_type=F32))
        _store_token_major(yc_ref, y, tme)

    @pl.when(i >= n_used)
    def _():
        yc_ref[...] = jnp.zeros_like(yc_ref)


def _moe(ea, eb, new_a, new_b, nvalid, nused, src, h2c, wts_sorted, g2, wg, wu, wd, tme, n_tiles):
    used = lambda i, ea, eb, na, nb, nv, nu, src: (jnp.minimum(i, nu[0] - 1), 0)
    w_a = lambda i, ea, eb, na, nb, nv, nu, src: (ea[i], 0, 0)
    w_b = lambda i, ea, eb, na, nb, nv, nu, src: (eb[i], 0, 0)
    up_shape, down_shape = (D_MODEL, D_EXPERT), (D_EXPERT, D_MODEL)
    grid_spec = pltpu.PrefetchScalarGridSpec(
        num_scalar_prefetch=7,
        grid=(n_tiles,),
        in_specs=[
            pl.BlockSpec(memory_space=pl.ANY),
            pl.BlockSpec((tme, 2), used),
            pl.BlockSpec((1, D_MODEL), lambda i, ea, eb, na, nb, nv, nu, src: (0, 0)),
            pl.BlockSpec((1,) + up_shape, w_a),
            pl.BlockSpec((1,) + up_shape, w_a),
            pl.BlockSpec((1,) + down_shape, w_a),
            pl.BlockSpec((1,) + up_shape, w_b),
            pl.BlockSpec((1,) + up_shape, w_b),
            pl.BlockSpec((1,) + down_shape, w_b),
        ],
        out_specs=pl.BlockSpec((tme * TOKEN_ROWS, LANES), lambda i, ea, eb, na, nb, nv, nu, src: (i, 0)),
        scratch_shapes=[pltpu.VMEM((2, tme * TOKEN_ROWS, LANES), F32)]
        + [pltpu.VMEM(s, BF16) for s in (up_shape, up_shape, down_shape) * 2]
        + [pltpu.SemaphoreType.DMA((2,))],
    )
    return pl.pallas_call(
        functools.partial(_moe_kernel, tme=tme),
        grid_spec=grid_spec,
        out_shape=jax.ShapeDtypeStruct((n_tiles * tme * TOKEN_ROWS, LANES), F32),
        compiler_params=_params("arbitrary"),
        name="moe",
    )(ea, eb, new_a, new_b, nvalid, nused, src, h2c, wts_sorted, g2, wg, wu, wd, wg, wu, wd)


def _final_kernel(dest_ref, h2c_ref, yc_hbm, gf_ref, o_ref, ybuf, sem, *, tmf):
    i = pl.program_id(0)
    slot = jnp.bitwise_and(i, 1)

    @pl.when(i == 0)
    def _():
        _start_gather(dest_ref, 0, tmf // GATHER_UNROLL, yc_hbm, ybuf.at[0], sem.at[0])

    @pl.when(i + 1 < pl.num_programs(0))
    def _():
        _start_gather(dest_ref, (i + 1) * tmf, tmf // GATHER_UNROLL, yc_hbm, ybuf.at[1 - slot],
                      sem.at[1 - slot])

    _wait_gather(tmf, yc_hbm, ybuf.at[slot], sem.at[slot])
    h = _load_token_major(h2c_ref, tmf) + _load_token_major(ybuf.at[slot], tmf)
    ms = jnp.mean(h * h, axis=-1, keepdims=True)
    o_ref[...] = h * lax.rsqrt(ms + EPS) * gf_ref[...]


def _final(dest, h2c, yc, gf, tmf):
    n = h2c.shape[0] // TOKEN_ROWS
    grid_spec = pltpu.PrefetchScalarGridSpec(
        num_scalar_prefetch=1,
        grid=(n // tmf,),
        in_specs=[
            pl.BlockSpec((tmf * TOKEN_ROWS, LANES), lambda i, dest: (i, 0)),
            pl.BlockSpec(memory_space=pl.ANY),
            pl.BlockSpec((1, D_MODEL), lambda i, dest: (0, 0)),
        ],
        out_specs=pl.BlockSpec((tmf, D_MODEL), lambda i, dest: (i, 0)),
        scratch_shapes=[pltpu.VMEM((2, tmf * TOKEN_ROWS, LANES), F32), pltpu.SemaphoreType.DMA((2,))],
    )
    return pl.pallas_call(
        functools.partial(_final_kernel, tmf=tmf),
        grid_spec=grid_spec,
        out_shape=jax.ShapeDtypeStruct((n, D_MODEL), F32),
        compiler_params=_params("arbitrary"),
        name="final",
    )(dest, h2c, yc, gf)


def _rope_tables(t):
    inv = 1.0 / (ROPE_THETA ** (np.arange(0, HEAD_DIM, 2, dtype=np.float64) / HEAD_DIM))
    ang = np.arange(t, dtype=np.float64)[:, None] * inv[None, :]
    ang = np.concatenate([ang, ang, ang, ang], axis=-1)
    sign = np.where((np.arange(LANES) % HEAD_DIM) < HEAD_DIM // 2, -1.0, 1.0)
    return (jnp.asarray(np.cos(ang), dtype=F32), jnp.asarray(np.sin(ang) * sign[None, :], dtype=F32))


def kernel(x, meta, norm1_g, w_in, b_gate, lambda_q1, lambda_k1, lambda_q2, lambda_k2, subln_g, pool_w,
           pool_scale, w_attn_br, w_pool_br, w_out, norm2_g, w_router_group, b_router_group,
           w_router_expert, b_router_expert, w_e_gate, w_e_up, w_e_down, final_g):
    batch, seq, d = x.shape
    n = batch * seq
    x2 = x.reshape(n, d)
    cos, sin = _rope_tables(seq + N_META)
    q, k, v, u, gates = _inproj(x2, norm1_g, cos[N_META:], sin[N_META:], w_in[0], b_gate, tm=1024)
    qm, km, vm, um, _ = _inproj(meta, norm1_g, cos[:N_META], sin[:N_META], w_in[0], b_gate, tm=N_META)
    a = _attention(q, k, v, qm, km, vm, lambda_q1, lambda_k1, lambda_q2, lambda_k2, subln_g, batch, seq,
                   bq=256)

    n_rt = 32
    wr_t = jnp.concatenate([w_router_group[0], w_router_expert[0]], axis=1).T
    wr_t = jnp.pad(wr_t, ((0, n_rt - wr_t.shape[0]), (0, 0)))
    br_t = jnp.concatenate([b_router_group[0], b_router_expert[0]])
    br_t = jnp.pad(br_t, (0, n_rt - br_t.shape[0])).reshape(n_rt, 1)
    h2c, logt = _mix(x2, a, u, um, gates, w_attn_br[0].astype(BF16), pool_w[0].astype(BF16),
                     pool_scale, w_pool_br[0].astype(BF16), w_out[0].astype(BF16), norm2_g,
                     wr_t, br_t, seq, tm=512)
    ids, wts, cnt = _route(logt, tl=1024)

    tme = 256
    n_tiles = n // tme + N_BUCKETS
    bucket, rank = ids[0], ids[1]
    tiles_per_bucket = (cnt[:N_BUCKETS, 0] + tme - 1) // tme
    tile_end = jnp.cumsum(tiles_per_bucket)
    n_used = tile_end[-1]
    dest = (tile_end - tiles_per_bucket)[bucket] * tme + rank
    per_token = jnp.concatenate([wts[0:2], jnp.arange(n, dtype=F32)[None, :]], axis=0).T
    per_row = jnp.zeros((n_tiles * tme, 3), F32).at[dest].set(per_token)
    wts_sorted = per_row[:, 0:2]
    src = per_row[:, 2].astype(jnp.int32)
    last_tile = jnp.minimum(jnp.arange(n_tiles), n_used - 1)
    tile_bucket = jnp.sum((tile_end[None, :] <= last_tile[:, None]).astype(jnp.int32), axis=1)
    tile_bucket = jnp.minimum(tile_bucket, N_BUCKETS - 1)
    group, pair = tile_bucket // N_PAIRS, tile_bucket % N_PAIRS
    e_a = (group * EXPERTS_PER_GROUP + jnp.asarray(_PAIR_A, jnp.int32)[pair]).astype(jnp.int32)
    e_b = (group * EXPERTS_PER_GROUP + jnp.asarray(_PAIR_B, jnp.int32)[pair]).astype(jnp.int32)
    changed = lambda e: jnp.concatenate([jnp.ones((1,), jnp.int32), (e[1:] != e[:-1]).astype(jnp.int32)])
    tile_in_bucket = jnp.arange(n_tiles) - (tile_end - tiles_per_bucket)[tile_bucket]
    n_valid = jnp.clip(cnt[:N_BUCKETS, 0][tile_bucket] - tile_in_bucket * tme, 0, tme).astype(jnp.int32)

    yc = _moe(e_a, e_b, changed(e_a), changed(e_b), n_valid, n_used.reshape(1).astype(jnp.int32), src,
              h2c, wts_sorted, norm2_g, w_e_gate[0], w_e_up[0], w_e_down[0], tme, n_tiles)
    out = _final(dest.astype(jnp.int32), h2c, yc, final_g.reshape(1, d), tmf=512)
    return out.reshape(batch, seq, d)
```

```python
import functools
import math

import jax
import jax.numpy as jnp
import numpy as np
from jax import lax
from jax.experimental import pallas as pl
from jax.experimental.pallas import tpu as pltpu

D_MODEL = 1024
N_META = 16
N_HEADS = 8
HEAD_DIM = 64
V_DIM = 2 * HEAD_DIM
POOL_WINDOWS = (2, 4, 8, 16)
N_POOL_GROUPS = len(POOL_WINDOWS)
POOL_GROUP_DIM = 128
POOL_WIDTH = N_POOL_GROUPS * POOL_GROUP_DIM
ROPE_THETA = 10000.0
N_GROUPS = 4
EXPERTS_PER_GROUP = 4
N_EXPERTS = N_GROUPS * EXPERTS_PER_GROUP
D_EXPERT = 512
EPS = 1e-6
LAMBDA_INIT = 0.8 - 0.6 * math.exp(-0.3 * 0)
LOG2_E = math.log2(math.e)

_PAIR_A = (0, 0, 0, 1, 1, 3)
_PAIR_B = (1, 2, 3, 3, 2, 2)
N_PAIRS = len(_PAIR_A)
N_BUCKETS = N_GROUPS * N_PAIRS
N_BUCKET_ROWS = 32
GATHER_UNROLL = 8

LANES = 128
NEG_BIG = -1e30
VMEM_LIMIT = 48 * 1024 * 1024

F32 = jnp.float32
BF16 = jnp.bfloat16

_HEADS_PER_STEP = 2
_IN_STEPS = N_HEADS // _HEADS_PER_STEP
_QKV_BLK = _HEADS_PER_STEP * V_DIM
_GATE_BLK = 2 * D_MODEL // _IN_STEPS
_O_K = D_MODEL // _QKV_BLK
_O_V = 2 * D_MODEL // _QKV_BLK
_O_U = 3 * D_MODEL // POOL_GROUP_DIM
_O_G = (3 * D_MODEL + POOL_WIDTH) // _GATE_BLK


def _params(*sem):
    return pltpu.CompilerParams(dimension_semantics=sem, vmem_limit_bytes=VMEM_LIMIT)


TOKEN_ROWS = D_MODEL // LANES


def _store_token_major(ref, val, rows):
    for s in range(TOKEN_ROWS):
        ref[pl.ds(s, rows, stride=TOKEN_ROWS), :] = val[:, s * LANES:(s + 1) * LANES]


def _load_token_major(ref, rows):
    return jnp.concatenate([ref[pl.ds(s, rows, stride=TOKEN_ROWS), :] for s in range(TOKEN_ROWS)], axis=1)


def _rope(z, cos, sin_signed, first_half):
    outs = []
    for c in range(z.shape[1] // LANES):
        zc = z[:, c * LANES:(c + 1) * LANES]
        rot = jnp.where(first_half, pltpu.roll(zc, LANES - HEAD_DIM // 2, 1),
                        pltpu.roll(zc, HEAD_DIM // 2, 1))
        outs.append(zc * cos + rot * sin_signed)
    return jnp.concatenate(outs, axis=1)


def _inproj_kernel(x_ref, g_ref, cos_ref, sin_ref, wq_ref, wk_ref, wv_ref, wu_ref, wg_ref, bg_ref,
                   q_ref, k_ref, v_ref, u_ref, gate_ref, wq_sc, wk_sc, wv_sc, wu_sc, wg_sc):
    @pl.when(pl.program_id(1) == 0)
    def _():
        for src, dst in ((wq_ref, wq_sc), (wk_ref, wk_sc), (wv_ref, wv_sc), (wu_ref, wu_sc),
                         (wg_ref, wg_sc)):
            dst[...] = src[...].astype(BF16)

    x = x_ref[...]
    ms = jnp.mean(x * x, axis=-1, keepdims=True)
    hn = (x * lax.rsqrt(ms + EPS) * g_ref[...]).astype(BF16)
    cos = cos_ref[...]
    sin = sin_ref[...]
    lane = lax.broadcasted_iota(jnp.int32, cos.shape, 1)
    first_half = jnp.bitwise_and(lane, HEAD_DIM - 1) < (HEAD_DIM // 2)
    zg = jnp.dot(hn, wg_sc[...], preferred_element_type=F32) + bg_ref[...]
    gate_ref[...] = (1.0 / (1.0 + jnp.exp(-zg))).astype(BF16)
    zq = jnp.dot(hn, wq_sc[...], preferred_element_type=F32)
    q_ref[...] = (_rope(zq, cos, sin, first_half) * (LOG2_E / math.sqrt(HEAD_DIM))).astype(BF16)
    zk = jnp.dot(hn, wk_sc[...], preferred_element_type=F32)
    k_ref[...] = _rope(zk, cos, sin, first_half).astype(BF16)
    u_ref[...] = jnp.dot(hn, wu_sc[...], preferred_element_type=F32)
    v_ref[...] = jnp.dot(hn, wv_sc[...], preferred_element_type=F32).astype(BF16)


def _inproj(x2, g1, cos, sin, w_in, b_gate, tm):
    n = x2.shape[0]
    n_pos_blocks = cos.shape[0] // tm
    row = lambda j, i: (i, 0)
    pos = lambda j, i: (i % n_pos_blocks, 0)
    return pl.pallas_call(
        _inproj_kernel,
        grid=(_IN_STEPS, n // tm),
        in_specs=[
            pl.BlockSpec((tm, D_MODEL), row),
            pl.BlockSpec((1, D_MODEL), lambda j, i: (0, 0)),
            pl.BlockSpec((tm, LANES), pos),
            pl.BlockSpec((tm, LANES), pos),
            pl.BlockSpec((D_MODEL, _QKV_BLK), lambda j, i: (0, j)),
            pl.BlockSpec((D_MODEL, _QKV_BLK), lambda j, i: (0, _O_K + j)),
            pl.BlockSpec((D_MODEL, _QKV_BLK), lambda j, i: (0, _O_V + j)),
            pl.BlockSpec((D_MODEL, POOL_GROUP_DIM), lambda j, i: (0, _O_U + j)),
            pl.BlockSpec((D_MODEL, _GATE_BLK), lambda j, i: (0, _O_G + j)),
            pl.BlockSpec((1, _GATE_BLK), lambda j, i: (0, j)),
        ],
        out_specs=[
            pl.BlockSpec((tm, _QKV_BLK), lambda j, i: (i, j)),
            pl.BlockSpec((tm, _QKV_BLK), lambda j, i: (i, j)),
            pl.BlockSpec((tm, _QKV_BLK), lambda j, i: (i, j)),
            pl.BlockSpec((tm, POOL_GROUP_DIM), lambda j, i: (i, j)),
            pl.BlockSpec((tm, _GATE_BLK), lambda j, i: (i, j)),
        ],
        out_shape=[
            jax.ShapeDtypeStruct((n, D_MODEL), BF16),
            jax.ShapeDtypeStruct((n, D_MODEL), BF16),
            jax.ShapeDtypeStruct((n, D_MODEL), BF16),
            jax.ShapeDtypeStruct((n, POOL_WIDTH), F32),
            jax.ShapeDtypeStruct((n, 2 * D_MODEL), BF16),
        ],
        scratch_shapes=[
            pltpu.VMEM((D_MODEL, _QKV_BLK), BF16),
            pltpu.VMEM((D_MODEL, _QKV_BLK), BF16),
            pltpu.VMEM((D_MODEL, _QKV_BLK), BF16),
            pltpu.VMEM((D_MODEL, POOL_GROUP_DIM), BF16),
            pltpu.VMEM((D_MODEL, _GATE_BLK), BF16),
        ],
        compiler_params=_params("arbitrary", "arbitrary"),
        name="inproj",
    )(x2, g1, cos, sin, w_in, w_in, w_in, w_in, w_in, b_gate)


def _attn_kernel(lq1_ref, lk1_ref, lq2_ref, lk2_ref, subg_ref, q_ref, k_ref, v_ref, qm_ref, km_ref,
                 vm_ref, o_ref, *state_sc, bq):
    seq = k_ref.shape[0]
    nq = seq // bq
    states = [tuple(state_sc[3 * g:3 * g + 3]) for g in range(nq + 1)]

    def rows_at(ref, meta_ref, p0, n):
        if p0 == 0:
            return jnp.concatenate([meta_ref[...], ref[0:n - N_META, :]], axis=0)
        real = ref[p0 - N_META:min(p0 - N_META + n, seq), :]
        if real.shape[0] < n:
            real = jnp.concatenate([real, jnp.zeros((n - real.shape[0], real.shape[1]), real.dtype)], axis=0)
        return real

    def block_rows(i):
        return (i * bq, bq, bq) if i < nq else (nq * bq, N_META, LANES)

    lam = (jnp.exp(jnp.sum(lq1_ref[...] * lk1_ref[...], axis=-1, keepdims=True))
           - jnp.exp(jnp.sum(lq2_ref[...] * lk2_ref[...], axis=-1, keepdims=True)) + LAMBDA_INIT)

    def causal_bias(n_rows, n_cols):
        r = lax.broadcasted_iota(jnp.int32, (n_rows, n_cols), 0)
        c = lax.broadcasted_iota(jnp.int32, (n_rows, n_cols), 1)
        bias = jnp.where(c <= r, 0.0, NEG_BIG)
        return jnp.concatenate([bias, bias], axis=0)

    biases = {bq: causal_bias(bq, bq), N_META: causal_bias(N_META, LANES)}

    def load_q(i, qq_sc):
        p0, n, _ = block_rows(i)
        q = rows_at(q_ref, qm_ref, p0, n)
        lane = lax.broadcasted_iota(jnp.int32, q.shape, 1)
        zero = jnp.zeros_like(q)
        qq_sc[0:n, :] = jnp.where(lane < HEAD_DIM, q, zero)
        qq_sc[n:2 * n, :] = jnp.where(lane >= HEAD_DIM, q, zero)

    def step(state, kb, vb, bias):
        qq_sc, m_sc, acc_sc = state
        first = bias is not None
        s = lax.dot_general(qq_sc[...], kb, (((1,), (1,)), ((), ())), preferred_element_type=F32)
        if first:
            s = s + bias
        n_tiles = s.shape[1] // LANES
        rm = s[:, 0:LANES]
        for t in range(1, n_tiles):
            rm = jnp.maximum(rm, s[:, t * LANES:(t + 1) * LANES])
        rmax = jnp.max(rm, axis=-1, keepdims=True)
        if first:
            m_new = jnp.broadcast_to(rmax, (s.shape[0], LANES))
        else:
            m_prev = m_sc[...]
            m_new = jnp.maximum(m_prev, rmax)
        p = jnp.exp2(s - jnp.concatenate([m_new] * n_tiles, axis=1))
        pv = jnp.dot(p.astype(BF16), vb, preferred_element_type=F32)
        if first:
            acc_sc[...] = pv
        else:
            alpha = jnp.exp2(m_prev - m_new)
            acc_sc[...] = jnp.concatenate([alpha, alpha], axis=1) * acc_sc[...] + pv
        m_sc[...] = m_new

    def key_step(state, i, j):
        p0, n, width = block_rows(i) if j == 0 else ((j - 1) * bq, None, bq)
        kb = rows_at(k_ref, km_ref, p0, width)
        vb = jnp.concatenate([rows_at(v_ref, vm_ref, p0, width), jnp.ones((width, V_DIM), BF16)], axis=1)
        step(state, kb, vb, biases[n] if j == 0 else None)

    def finish(i, acc_sc):
        p0, n, _ = block_rows(i)
        o1 = acc_sc[0:n, 0:V_DIM] / acc_sc[0:n, V_DIM:2 * V_DIM]
        o2 = acc_sc[n:2 * n, 0:V_DIM] / acc_sc[n:2 * n, V_DIM:2 * V_DIM]
        o = o1 - lam * o2
        ms = jnp.mean(o * o, axis=-1, keepdims=True)
        o = (o * lax.rsqrt(ms + EPS) * subg_ref[...] * (1.0 - LAMBDA_INIT)).astype(BF16)
        if p0 == 0:
            o_ref[0:n - N_META, :] = o[N_META:n, :]
        else:
            o_ref[p0 - N_META:p0 - N_META + n, :] = o

    order = []
    for p in range((nq + 2) // 2):
        order += [p] if p == nq - p else [p, nq - p]
    for i in order:
        load_q(i, states[i][0])
    for j in range(nq + 1):
        for i in order:
            if j <= i:
                key_step(states[i], i, j)
    for i in order:
        finish(i, states[i][2])


def _attention(q, k, v, qm, km, vm, lq1, lk1, lq2, lk2, subg, batch, seq, bq):
    nq = seq // bq
    small = lambda b, h: (0, 0)
    head = lambda b, h: (b, h)
    meta = lambda b, h: (0, h)

    def state_scratch(rows):
        return [pltpu.VMEM((2 * rows, V_DIM), BF16), pltpu.VMEM((2 * rows, LANES), F32),
                pltpu.VMEM((2 * rows, 2 * V_DIM), F32)]

    return pl.pallas_call(
        functools.partial(_attn_kernel, bq=bq),
        grid=(batch, N_HEADS),
        in_specs=[
            pl.BlockSpec((1, HEAD_DIM), small),
            pl.BlockSpec((1, HEAD_DIM), small),
            pl.BlockSpec((1, HEAD_DIM), small),
            pl.BlockSpec((1, HEAD_DIM), small),
            pl.BlockSpec((1, V_DIM), small),
            pl.BlockSpec((seq, V_DIM), head),
            pl.BlockSpec((seq, V_DIM), head),
            pl.BlockSpec((seq, V_DIM), head),
            pl.BlockSpec((N_META, V_DIM), meta),
            pl.BlockSpec((N_META, V_DIM), meta),
            pl.BlockSpec((N_META, V_DIM), meta),
        ],
        out_specs=pl.BlockSpec((seq, V_DIM), head),
        out_shape=jax.ShapeDtypeStruct(q.shape, BF16),
        scratch_shapes=state_scratch(bq) * nq + state_scratch(N_META),
        compiler_params=_params("arbitrary", "arbitrary"),
        name="diff_attn",
    )(lq1, lk1, lq2, lk2, subg, q, k, v, qm, km, vm)


def _mix_kernel(x_ref, a_ref, u_ref, uprev_ref, umeta_ref, gate_ref, wa_ref, pw_ref, ps_ref, wp_ref,
                wo_ref, g2_ref, wr_ref, br_ref, h2c_ref, logit_ref, ext_sc, sum_a_sc, sum_b_sc, *, tm,
                tiles_per_seq):
    i = pl.program_id(0)
    first = (i % tiles_per_seq) == 0
    y_attn = jnp.dot(a_ref[...], wa_ref[...], preferred_element_type=F32)

    top = 2 * N_META
    for ref in (ext_sc, sum_a_sc, sum_b_sc):
        ref[0:N_META, :] = jnp.zeros((N_META, POOL_WIDTH), F32)

    @pl.when(first)
    def _():
        ext_sc[N_META:top, :] = umeta_ref[...]

    @pl.when(jnp.logical_not(first))
    def _():
        ext_sc[N_META:top, :] = uprev_ref[...]

    ext_sc[top:top + tm, :] = u_ref[...]

    pooled = []
    for g, w in enumerate(POOL_WINDOWS):
        cs = slice(g * POOL_GROUP_DIM, (g + 1) * POOL_GROUP_DIM)
        src, shift = ext_sc, 1
        for dst in (sum_a_sc, sum_b_sc, sum_a_sc, sum_b_sc):
            if shift >= w:
                break
            dst[N_META:top + tm, cs] = src[N_META:top + tm, cs] + src[N_META - shift:top + tm - shift, cs]
            src, shift = dst, 2 * shift
        y = (src[top:top + tm, cs] * (1.0 / w) - ext_sc[top:top + tm, cs]).astype(BF16)
        pooled.append(jnp.dot(y, pw_ref[g], preferred_element_type=F32))
    yp = (jnp.concatenate(pooled, axis=1) * ps_ref[...]).astype(BF16)
    y_pool = jnp.dot(yp, wp_ref[...], preferred_element_type=F32)
    mixed = (gate_ref[:, 0:D_MODEL].astype(F32) * y_attn
             + gate_ref[:, D_MODEL:2 * D_MODEL].astype(F32) * y_pool)
    h2 = x_ref[...] + jnp.dot(mixed.astype(BF16), wo_ref[...], preferred_element_type=F32)
    _store_token_major(h2c_ref, h2, tm)
    ms = jnp.mean(h2 * h2, axis=-1, keepdims=True)
    hn2 = h2 * lax.rsqrt(ms + EPS) * g2_ref[...]
    hn2_hi = hn2.astype(BF16)
    hn2_lo = (hn2 - hn2_hi.astype(F32)).astype(BF16)
    wr = wr_ref[...]
    wr_hi = wr.astype(BF16)
    wr_lo = (wr - wr_hi.astype(F32)).astype(BF16)
    lhs = jnp.concatenate([hn2_hi, hn2_lo, hn2_hi], axis=1)
    rhs = jnp.concatenate([wr_hi, wr_hi, wr_lo], axis=0)
    for half in range(2):
        rows = slice(half * tm // 2, (half + 1) * tm // 2)
        logit_ref[rows, :] = jnp.dot(lhs[rows, :], rhs, preferred_element_type=F32) + br_ref[...]


def _mix(x2, a, u, umeta, gates, w_attn, pool_w, pool_scale, w_pool, w_out, g2, wr, br, seq, tm):
    n = x2.shape[0]
    tiles_per_seq = seq // tm
    halo_blocks = tm // N_META
    const2 = lambda i: (0, 0)
    return pl.pallas_call(
        functools.partial(_mix_kernel, tm=tm, tiles_per_seq=tiles_per_seq),
        grid=(n // tm,),
        in_specs=[
            pl.BlockSpec((tm, D_MODEL), lambda i: (i, 0)),
            pl.BlockSpec((tm, D_MODEL), lambda i: (i, 0)),
            pl.BlockSpec((tm, POOL_WIDTH), lambda i: (i, 0)),
            pl.BlockSpec((N_META, POOL_WIDTH), lambda i: (jnp.maximum(i * halo_blocks - 1, 0), 0)),
            pl.BlockSpec((N_META, POOL_WIDTH), const2),
            pl.BlockSpec((tm, 2 * D_MODEL), lambda i: (i, 0)),
            pl.BlockSpec((D_MODEL, D_MODEL), const2),
            pl.BlockSpec((N_POOL_GROUPS, POOL_GROUP_DIM, POOL_GROUP_DIM), lambda i: (0, 0, 0)),
            pl.BlockSpec((1, POOL_WIDTH), const2),
            pl.BlockSpec((POOL_WIDTH, D_MODEL), const2),
            pl.BlockSpec((D_MODEL, D_MODEL), const2),
            pl.BlockSpec((1, D_MODEL), const2),
            pl.BlockSpec((D_MODEL, LANES), const2),
            pl.BlockSpec((1, LANES), const2),
        ],
        out_specs=[
            pl.BlockSpec((tm * TOKEN_ROWS, LANES), lambda i: (i, 0)),
            pl.BlockSpec((tm, LANES), lambda i: (i, 0)),
        ],
        out_shape=[
            jax.ShapeDtypeStruct((n * TOKEN_ROWS, LANES), F32),
            jax.ShapeDtypeStruct((n, LANES), F32),
        ],
        scratch_shapes=[pltpu.VMEM((tm + 2 * N_META, POOL_WIDTH), F32)] * 3,
        compiler_params=_params("arbitrary"),
        name="mix",
    )(x2, a, u, u, umeta, gates, w_attn, pool_w, pool_scale, w_pool, w_out, g2, wr, br)


def _route_kernel(logit_ref, tri_ref, ids_ref, wts_ref, cnt_ref, carry_sc):
    @pl.when(pl.program_id(0) == 0)
    def _():
        carry_sc[...] = jnp.zeros_like(carry_sc)

    lg = logit_ref[...].T
    g = [lg[r:r + 1, :] for r in range(N_GROUPS)]
    gmax = functools.reduce(jnp.maximum, g)
    gsel = jnp.full_like(g[0], N_GROUPS - 1).astype(jnp.int32)
    for r in range(N_GROUPS - 2, -1, -1):
        gsel = jnp.where(g[r] == gmax, r, gsel)
    p_group = 1.0 / functools.reduce(lambda a, b: a + b, [jnp.exp(x - gmax) for x in g])
    e = []
    for jj in range(EXPERTS_PER_GROUP):
        v = lg[N_GROUPS + jj:N_GROUPS + jj + 1, :]
        for r in range(1, N_GROUPS):
            row = N_GROUPS + r * EXPERTS_PER_GROUP + jj
            v = jnp.where(gsel == r, lg[row:row + 1, :], v)
        e.append(v)
    v1 = functools.reduce(jnp.maximum, e)
    i1 = jnp.full_like(gsel, EXPERTS_PER_GROUP - 1)
    for jj in range(EXPERTS_PER_GROUP - 2, -1, -1):
        i1 = jnp.where(e[jj] == v1, jj, i1)
    rest = [jnp.where(i1 == jj, -jnp.inf, e[jj]) for jj in range(EXPERTS_PER_GROUP)]
    v2 = functools.reduce(jnp.maximum, rest)
    i2 = jnp.full_like(gsel, EXPERTS_PER_GROUP - 1)
    for jj in range(EXPERTS_PER_GROUP - 2, -1, -1):
        i2 = jnp.where(jnp.logical_and(rest[jj] == v2, i1 != jj), jj, i2)
    t = jnp.exp(v2 - v1)
    w1 = p_group / (1.0 + t)
    w2 = p_group * t / (1.0 + t)
    lo = jnp.minimum(i1, i2)
    hi = jnp.maximum(i1, i2)
    pair = jnp.where(lo == 0, hi - 1, jnp.where(lo == 1, jnp.where(hi == 3, 3, 4), 5))
    bucket = gsel * N_PAIRS + pair
    w_of_lo = jnp.where(i1 < i2, w1, w2)
    w_of_hi = jnp.where(i1 < i2, w2, w1)
    swapped = pair == 5
    w_a = jnp.where(swapped, w_of_hi, w_of_lo)
    w_b = jnp.where(swapped, w_of_lo, w_of_hi)

    tl = lg.shape[1]
    bid = lax.broadcasted_iota(jnp.int32, (N_BUCKET_ROWS, tl), 0)
    hot = bid == bucket
    hot_b = jnp.where(hot, 1.0, 0.0).astype(BF16)
    prefix = jnp.dot(hot_b, tri_ref[...], preferred_element_type=F32)
    carry = carry_sc[...]
    before = prefix + jnp.concatenate([carry] * (tl // LANES), axis=1)
    rank = jnp.sum(jnp.where(hot, before, 0.0), axis=0, keepdims=True)
    total = carry + jnp.dot(hot_b, jnp.ones((tl, LANES), BF16), preferred_element_type=F32)
    carry_sc[...] = total
    cnt_ref[...] = total.astype(jnp.int32)
    pad_i = jnp.zeros((6, tl), jnp.int32)
    ids_ref[...] = jnp.concatenate([bucket, rank.astype(jnp.int32), pad_i], axis=0)
    wts_ref[...] = jnp.concatenate([w_a, w_b, pad_i.astype(F32)], axis=0)


def _route(logits, tl):
    n = logits.shape[0]
    tri = jnp.asarray(np.triu(np.ones((tl, tl), np.float32), k=1), dtype=BF16)
    return pl.pallas_call(
        _route_kernel,
        grid=(n // tl,),
        in_specs=[pl.BlockSpec((tl, LANES), lambda i: (i, 0)),
                  pl.BlockSpec((tl, tl), lambda i: (0, 0))],
        out_specs=[pl.BlockSpec((8, tl), lambda i: (0, i)),
                   pl.BlockSpec((8, tl), lambda i: (0, i)),
                   pl.BlockSpec((N_BUCKET_ROWS, LANES), lambda i: (0, 0))],
        out_shape=[jax.ShapeDtypeStruct((8, n), jnp.int32),
                   jax.ShapeDtypeStruct((8, n), F32),
                   jax.ShapeDtypeStruct((N_BUCKET_ROWS, LANES), jnp.int32)],
        scratch_shapes=[pltpu.VMEM((N_BUCKET_ROWS, LANES), F32)],
        compiler_params=_params("arbitrary"),
        name="route",
    )(logits, tri)


def _token_copy(src_hbm, src_token, dst_buf, dst_row, sem):
    return pltpu.make_async_copy(
        src_hbm.at[pl.ds(pl.multiple_of(src_token * TOKEN_ROWS, TOKEN_ROWS), TOKEN_ROWS), :],
        dst_buf.at[pl.ds(pl.multiple_of(dst_row * TOKEN_ROWS, TOKEN_ROWS), TOKEN_ROWS), :],
        sem)


def _start_gather(idx_ref, first, n_groups, src_hbm, dst_buf, sem):
    def body(rr, carry):
        for uu in range(GATHER_UNROLL):
            r = rr * GATHER_UNROLL + uu
            _token_copy(src_hbm, idx_ref[first + r], dst_buf, r, sem).start(priority=uu % 2)
        return carry

    lax.fori_loop(0, n_groups, body, 0)


def _start_padded_gather(idx_ref, first, n_valid, rows, src_hbm, dst_buf, sem):
    n_groups = (n_valid + GATHER_UNROLL - 1) // GATHER_UNROLL
    _start_gather(idx_ref, first, n_groups, src_hbm, dst_buf, sem)
    pad_groups = rows // GATHER_UNROLL - n_groups
    off = n_groups * GATHER_UNROLL
    size = rows // 2
    while size >= GATHER_UNROLL:
        take = jnp.bitwise_and(pad_groups, size // GATHER_UNROLL) != 0

        @pl.when(take)
        def _(off=off, size=size):
            pltpu.make_async_copy(
                src_hbm.at[pl.ds(0, size * TOKEN_ROWS), :],
                dst_buf.at[pl.ds(pl.multiple_of(off * TOKEN_ROWS, GATHER_UNROLL * TOKEN_ROWS),
                                 size * TOKEN_ROWS), :],
                sem).start()

        off = off + jnp.where(take, size, 0)
        size //= 2


def _wait_gather(rows, src_hbm, dst_buf, sem):
    pltpu.make_async_copy(src_hbm.at[pl.ds(0, rows * TOKEN_ROWS), :], dst_buf, sem).wait()


def _moe_kernel(ea_ref, eb_ref, new_a_ref, new_b_ref, nvalid_ref, nused_ref, src_ref, h2c_hbm, wts_ref,
                g2_ref, wg_a_ref, wu_a_ref, wd_a_ref, wg_b_ref, wu_b_ref, wd_b_ref, yc_ref,
                xbuf, wg_a_sc, wu_a_sc, wd_a_sc, wg_b_sc, wu_b_sc, wd_b_sc, sem, *, tme):
    i = pl.program_id(0)
    n_used = nused_ref[0]
    slot = jnp.bitwise_and(i, 1)

    @pl.when(i == 0)
    def _():
        _start_padded_gather(src_ref, 0, nvalid_ref[0], tme, h2c_hbm, xbuf.at[0], sem.at[0])

    @pl.when(i + 1 < n_used)
    def _():
        _start_padded_gather(src_ref, (i + 1) * tme, nvalid_ref[i + 1], tme, h2c_hbm,
                             xbuf.at[1 - slot], sem.at[1 - slot])

    @pl.when(new_a_ref[i] == 1)
    def _():
        for src, dst in ((wg_a_ref, wg_a_sc), (wu_a_ref, wu_a_sc), (wd_a_ref, wd_a_sc)):
            dst[...] = src[0].astype(BF16)

    @pl.when(new_b_ref[i] == 1)
    def _():
        for src, dst in ((wg_b_ref, wg_b_sc), (wu_b_ref, wu_b_sc), (wd_b_ref, wd_b_sc)):
            dst[...] = src[0].astype(BF16)

    @pl.when(i < n_used)
    def _():
        _wait_gather(tme, h2c_hbm, xbuf.at[slot], sem.at[slot])
        x = _load_token_major(xbuf.at[slot], tme)
        ms = jnp.mean(x * x, axis=-1, keepdims=True)
        hn = (x * lax.rsqrt(ms + EPS) * g2_ref[...]).astype(BF16)
        wts = wts_ref[...]

        def expert_act(wg_sc, wu_sc, w):
            gate = jnp.dot(hn, wg_sc[...], preferred_element_type=F32)
            up = jnp.dot(hn, wu_sc[...], preferred_element_type=F32)
            return (gate / (1.0 + jnp.exp(-gate)) * up * w).astype(BF16)

        y = (jnp.dot(expert_act(wg_a_sc, wu_a_sc, wts[:, 0:1]), wd_a_sc[...], preferred_element_type=F32)
             + jnp.dot(expert_act(wg_b_sc, wu_b_sc, wts[:, 1:2]), wd_b_sc[...], preferred_element_type=F32))
        _store_token_major(yc_ref, y, tme)

    @pl.when(i >= n_used)
    def _():
        yc_ref[...] = jnp.zeros_like(yc_ref)


def _moe(ea, eb, new_a, new_b, nvalid, nused, src, h2c, wts_sorted, g2, wg, wu, wd, tme, n_tiles):
    used = lambda i, ea, eb, na, nb, nv, nu, src: (jnp.minimum(i, nu[0] - 1), 0)
    w_a = lambda i, ea, eb, na, nb, nv, nu, src: (ea[i], 0, 0)
    w_b = lambda i, ea, eb, na, nb, nv, nu, src: (eb[i], 0, 0)
    up_shape, down_shape = (D_MODEL, D_EXPERT), (D_EXPERT, D_MODEL)
    grid_spec = pltpu.PrefetchScalarGridSpec(
        num_scalar_prefetch=7,
        grid=(n_tiles,),
        in_specs=[
            pl.BlockSpec(memory_space=pl.ANY),
            pl.BlockSpec((tme, 2), used),
            pl.BlockSpec((1, D_MODEL), lambda i, ea, eb, na, nb, nv, nu, src: (0, 0)),
            pl.BlockSpec((1,) + up_shape, w_a),
            pl.BlockSpec((1,) + up_shape, w_a),
            pl.BlockSpec((1,) + down_shape, w_a),
            pl.BlockSpec((1,) + up_shape, w_b),
            pl.BlockSpec((1,) + up_shape, w_b),
            pl.BlockSpec((1,) + down_shape, w_b),
        ],
        out_specs=pl.BlockSpec((tme * TOKEN_ROWS, LANES), lambda i, ea, eb, na, nb, nv, nu, src: (i, 0)),
        scratch_shapes=[pltpu.VMEM((2, tme * TOKEN_ROWS, LANES), F32)]
        + [pltpu.VMEM(s, BF16) for s in (up_shape, up_shape, down_shape) * 2]
        + [pltpu.SemaphoreType.DMA((2,))],
    )
    return pl.pallas_call(
        functools.partial(_moe_kernel, tme=tme),
        grid_spec=grid_spec,
        out_shape=jax.ShapeDtypeStruct((n_tiles * tme * TOKEN_ROWS, LANES), F32),
        compiler_params=_params("arbitrary"),
        name="moe",
    )(ea, eb, new_a, new_b, nvalid, nused, src, h2c, wts_sorted, g2, wg, wu, wd, wg, wu, wd)


def _final_kernel(dest_ref, h2c_ref, yc_hbm, gf_ref, o_ref, ybuf, sem, *, tmf):
    i = pl.program_id(0)
    slot = jnp.bitwise_and(i, 1)

    @pl.when(i == 0)
    def _():
        _start_gather(dest_ref, 0, tmf // GATHER_UNROLL, yc_hbm, ybuf.at[0], sem.at[0])

    @pl.when(i + 1 < pl.num_programs(0))
    def _():
        _start_gather(dest_ref, (i + 1) * tmf, tmf // GATHER_UNROLL, yc_hbm, ybuf.at[1 - slot],
                      sem.at[1 - slot])

    _wait_gather(tmf, yc_hbm, ybuf.at[slot], sem.at[slot])
    h = _load_token_major(h2c_ref, tmf) + _load_token_major(ybuf.at[slot], tmf)
    ms = jnp.mean(h * h, axis=-1, keepdims=True)
    o_ref[...] = h * lax.rsqrt(ms + EPS) * gf_ref[...]


def _final(dest, h2c, yc, gf, tmf):
    n = h2c.shape[0] // TOKEN_ROWS
    grid_spec = pltpu.PrefetchScalarGridSpec(
        num_scalar_prefetch=1,
        grid=(n // tmf,),
        in_specs=[
            pl.BlockSpec((tmf * TOKEN_ROWS, LANES), lambda i, dest: (i, 0)),
            pl.BlockSpec(memory_space=pl.ANY),
            pl.BlockSpec((1, D_MODEL), lambda i, dest: (0, 0)),
        ],
        out_specs=pl.BlockSpec((tmf, D_MODEL), lambda i, dest: (i, 0)),
        scratch_shapes=[pltpu.VMEM((2, tmf * TOKEN_ROWS, LANES), F32), pltpu.SemaphoreType.DMA((2,))],
    )
    return pl.pallas_call(
        functools.partial(_final_kernel, tmf=tmf),
        grid_spec=grid_spec,
        out_shape=jax.ShapeDtypeStruct((n, D_MODEL), F32),
        compiler_params=_params("arbitrary"),
        name="final",
    )(dest, h2c, yc, gf)


def _rope_tables(t):
    inv = 1.0 / (ROPE_THETA ** (np.arange(0, HEAD_DIM, 2, dtype=np.float64) / HEAD_DIM))
    ang = np.arange(t, dtype=np.float64)[:, None] * inv[None, :]
    ang = np.concatenate([ang, ang, ang, ang], axis=-1)
    sign = np.where((np.arange(LANES) % HEAD_DIM) < HEAD_DIM // 2, -1.0, 1.0)
    return (jnp.asarray(np.cos(ang), dtype=F32), jnp.asarray(np.sin(ang) * sign[None, :], dtype=F32))


def kernel(x, meta, norm1_g, w_in, b_gate, lambda_q1, lambda_k1, lambda_q2, lambda_k2, subln_g, pool_w,
           pool_scale, w_attn_br, w_pool_br, w_out, norm2_g, w_router_group, b_router_group,
           w_router_expert, b_router_expert, w_e_gate, w_e_up, w_e_down, final_g):
    batch, seq, d = x.shape
    n = batch * seq
    x2 = x.reshape(n, d)
    cos, sin = _rope_tables(seq + N_META)
    q, k, v, u, gates = _inproj(x2, norm1_g, cos[N_META:], sin[N_META:], w_in[0], b_gate, tm=1024)
    qm, km, vm, um, _ = _inproj(meta, norm1_g, cos[:N_META], sin[:N_META], w_in[0], b_gate, tm=N_META)
    a = _attention(q, k, v, qm, km, vm, lambda_q1, lambda_k1, lambda_q2, lambda_k2, subln_g, batch, seq,
                   bq=256)

    n_router = N_GROUPS + N_EXPERTS
    wr = jnp.pad(jnp.concatenate([w_router_group[0], w_router_expert[0]], axis=1),
                 ((0, 0), (0, LANES - n_router)))
    br = jnp.pad(jnp.concatenate([b_router_group[0], b_router_expert[0]]), (0, LANES - n_router))
    h2c, logits = _mix(x2, a, u, um, gates, w_attn_br[0].astype(BF16), pool_w[0].astype(BF16),
                       pool_scale, w_pool_br[0].astype(BF16), w_out[0].astype(BF16), norm2_g,
                       wr, br.reshape(1, LANES), seq, tm=512)
    ids, wts, cnt = _route(logits, tl=1024)

    tme = 256
    n_tiles = n // tme + N_BUCKETS
    bucket, rank = ids[0], ids[1]
    tiles_per_bucket = (cnt[:N_BUCKETS, 0] + tme - 1) // tme
    tile_end = jnp.cumsum(tiles_per_bucket)
    n_used = tile_end[-1]
    dest = (tile_end - tiles_per_bucket)[bucket] * tme + rank
    per_token = jnp.concatenate([wts[0:2], jnp.arange(n, dtype=F32)[None, :]], axis=0).T
    per_row = jnp.zeros((n_tiles * tme, 3), F32).at[dest].set(per_token)
    wts_sorted = per_row[:, 0:2]
    src = per_row[:, 2].astype(jnp.int32)
    last_tile = jnp.minimum(jnp.arange(n_tiles), n_used - 1)
    tile_bucket = jnp.sum((tile_end[None, :] <= last_tile[:, None]).astype(jnp.int32), axis=1)
    tile_bucket = jnp.minimum(tile_bucket, N_BUCKETS - 1)
    group, pair = tile_bucket // N_PAIRS, tile_bucket % N_PAIRS
    e_a = (group * EXPERTS_PER_GROUP + jnp.asarray(_PAIR_A, jnp.int32)[pair]).astype(jnp.int32)
    e_b = (group * EXPERTS_PER_GROUP + jnp.asarray(_PAIR_B, jnp.int32)[pair]).astype(jnp.int32)
    changed = lambda e: jnp.concatenate([jnp.ones((1,), jnp.int32), (e[1:] != e[:-1]).astype(jnp.int32)])
    tile_in_bucket = jnp.arange(n_tiles) - (tile_end - tiles_per_bucket)[tile_bucket]
    n_valid = jnp.clip(cnt[:N_BUCKETS, 0][tile_bucket] - tile_in_bucket * tme, 0, tme).astype(jnp.int32)

    yc = _moe(e_a, e_b, changed(e_a), changed(e_b), n_valid, n_used.reshape(1).astype(jnp.int32), src,
              h2c, wts_sorted, norm2_g, w_e_gate[0], w_e_up[0], w_e_down[0], tme, n_tiles)
    out = _final(dest.astype(jnp.int32), h2c, yc, final_g.reshape(1, d), tmf=512)
    return out.reshape(batch, seq, d)
```

```python
import functools
import math

import jax
import jax.numpy as jnp
import numpy as np
from jax import lax
from jax.experimental import pallas as pl
from jax.experimental.pallas import tpu as pltpu

D_MODEL = 1024
N_META = 16
N_HEADS = 8
HEAD_DIM = 64
V_DIM = 2 * HEAD_DIM
POOL_WINDOWS = (2, 4, 8, 16)
N_POOL_GROUPS = len(POOL_WINDOWS)
POOL_GROUP_DIM = 128
POOL_WIDTH = N_POOL_GROUPS * POOL_GROUP_DIM
ROPE_THETA = 10000.0
N_GROUPS = 4
EXPERTS_PER_GROUP = 4
N_EXPERTS = N_GROUPS * EXPERTS_PER_GROUP
D_EXPERT = 512
EPS = 1e-6
LAMBDA_INIT = 0.8 - 0.6 * math.exp(-0.3 * 0)
LOG2_E = math.log2(math.e)

_PAIR_A = (0, 0, 0, 1, 1, 3)
_PAIR_B = (1, 2, 3, 3, 2, 2)
N_PAIRS = len(_PAIR_A)
N_BUCKETS = N_GROUPS * N_PAIRS
N_BUCKET_ROWS = 32
GATHER_UNROLL = 8

LANES = 128
NEG_BIG = -1e30
VMEM_LIMIT = 48 * 1024 * 1024

F32 = jnp.float32
BF16 = jnp.bfloat16

_HEADS_PER_STEP = 2
_IN_STEPS = N_HEADS // _HEADS_PER_STEP
_QKV_BLK = _HEADS_PER_STEP * V_DIM
_GATE_BLK = 2 * D_MODEL // _IN_STEPS
_O_K = D_MODEL // _QKV_BLK
_O_V = 2 * D_MODEL // _QKV_BLK
_O_U = 3 * D_MODEL // POOL_GROUP_DIM
_O_G = (3 * D_MODEL + POOL_WIDTH) // _GATE_BLK


def _params(*sem):
    return pltpu.CompilerParams(dimension_semantics=sem, vmem_limit_bytes=VMEM_LIMIT)


TOKEN_ROWS = D_MODEL // LANES


def _store_token_major(ref, val, rows):
    for s in range(TOKEN_ROWS):
        ref[pl.ds(s, rows, stride=TOKEN_ROWS), :] = val[:, s * LANES:(s + 1) * LANES]


def _load_token_major(ref, rows):
    return jnp.concatenate([ref[pl.ds(s, rows, stride=TOKEN_ROWS), :] for s in range(TOKEN_ROWS)], axis=1)


def _rope(z, cos, sin_signed, first_half):
    outs = []
    for c in range(z.shape[1] // LANES):
        zc = z[:, c * LANES:(c + 1) * LANES]
        rot = jnp.where(first_half, pltpu.roll(zc, LANES - HEAD_DIM // 2, 1),
                        pltpu.roll(zc, HEAD_DIM // 2, 1))
        outs.append(zc * cos + rot * sin_signed)
    return jnp.concatenate(outs, axis=1)


def _inproj_kernel(x_ref, g_ref, cos_ref, sin_ref, wq_ref, wk_ref, wv_ref, wu_ref, wg_ref, bg_ref,
                   q_ref, k_ref, v_ref, u_ref, gate_ref, wq_sc, wk_sc, wv_sc, wu_sc, wg_sc):
    @pl.when(pl.program_id(1) == 0)
    def _():
        for src, dst in ((wq_ref, wq_sc), (wk_ref, wk_sc), (wv_ref, wv_sc), (wu_ref, wu_sc),
                         (wg_ref, wg_sc)):
            dst[...] = src[...].astype(BF16)

    x = x_ref[...]
    ms = jnp.mean(x * x, axis=-1, keepdims=True)
    hn = (x * lax.rsqrt(ms + EPS) * g_ref[...]).astype(BF16)
    cos = cos_ref[...]
    sin = sin_ref[...]
    lane = lax.broadcasted_iota(jnp.int32, cos.shape, 1)
    first_half = jnp.bitwise_and(lane, HEAD_DIM - 1) < (HEAD_DIM // 2)
    zg = jnp.dot(hn, wg_sc[...], preferred_element_type=F32) + bg_ref[...]
    gate_ref[...] = (1.0 / (1.0 + jnp.exp(-zg))).astype(BF16)
    zq = jnp.dot(hn, wq_sc[...], preferred_element_type=F32)
    q_ref[...] = (_rope(zq, cos, sin, first_half) * (LOG2_E / math.sqrt(HEAD_DIM))).astype(BF16)
    zk = jnp.dot(hn, wk_sc[...], preferred_element_type=F32)
    k_ref[...] = _rope(zk, cos, sin, first_half).astype(BF16)
    u_ref[...] = jnp.dot(hn, wu_sc[...], preferred_element_type=F32)
    v_ref[...] = jnp.dot(hn, wv_sc[...], preferred_element_type=F32).astype(BF16)


def _inproj(x2, g1, cos, sin, w_in, b_gate, tm):
    n = x2.shape[0]
    n_pos_blocks = cos.shape[0] // tm
    row = lambda j, i: (i, 0)
    pos = lambda j, i: (i % n_pos_blocks, 0)
    return pl.pallas_call(
        _inproj_kernel,
        grid=(_IN_STEPS, n // tm),
        in_specs=[
            pl.BlockSpec((tm, D_MODEL), row),
            pl.BlockSpec((1, D_MODEL), lambda j, i: (0, 0)),
            pl.BlockSpec((tm, LANES), pos),
            pl.BlockSpec((tm, LANES), pos),
            pl.BlockSpec((D_MODEL, _QKV_BLK), lambda j, i: (0, j)),
            pl.BlockSpec((D_MODEL, _QKV_BLK), lambda j, i: (0, _O_K + j)),
            pl.BlockSpec((D_MODEL, _QKV_BLK), lambda j, i: (0, _O_V + j)),
            pl.BlockSpec((D_MODEL, POOL_GROUP_DIM), lambda j, i: (0, _O_U + j)),
            pl.BlockSpec((D_MODEL, _GATE_BLK), lambda j, i: (0, _O_G + j)),
            pl.BlockSpec((1, _GATE_BLK), lambda j, i: (0, j)),
        ],
        out_specs=[
            pl.BlockSpec((tm, _QKV_BLK), lambda j, i: (i, j)),
            pl.BlockSpec((tm, _QKV_BLK), lambda j, i: (i, j)),
            pl.BlockSpec((tm, _QKV_BLK), lambda j, i: (i, j)),
            pl.BlockSpec((tm, POOL_GROUP_DIM), lambda j, i: (i, j)),
            pl.BlockSpec((tm, _GATE_BLK), lambda j, i: (i, j)),
        ],
        out_shape=[
            jax.ShapeDtypeStruct((n, D_MODEL), BF16),
            jax.ShapeDtypeStruct((n, D_MODEL), BF16),
            jax.ShapeDtypeStruct((n, D_MODEL), BF16),
            jax.ShapeDtypeStruct((n, POOL_WIDTH), F32),
            jax.ShapeDtypeStruct((n, 2 * D_MODEL), BF16),
        ],
        scratch_shapes=[
            pltpu.VMEM((D_MODEL, _QKV_BLK), BF16),
            pltpu.VMEM((D_MODEL, _QKV_BLK), BF16),
            pltpu.VMEM((D_MODEL, _QKV_BLK), BF16),
            pltpu.VMEM((D_MODEL, POOL_GROUP_DIM), BF16),
            pltpu.VMEM((D_MODEL, _GATE_BLK), BF16),
        ],
        compiler_params=_params("arbitrary", "arbitrary"),
        name="inproj",
    )(x2, g1, cos, sin, w_in, w_in, w_in, w_in, w_in, b_gate)


def _attn_kernel(lq1_ref, lk1_ref, lq2_ref, lk2_ref, subg_ref, q_ref, k_ref, v_ref, qm_ref, km_ref,
                 vm_ref, o_ref, *state_sc, bq):
    seq = k_ref.shape[0]
    nq = seq // bq
    states = [tuple(state_sc[3 * g:3 * g + 3]) for g in range(nq + 1)]

    def rows_at(ref, meta_ref, p0, n):
        if p0 == 0:
            return jnp.concatenate([meta_ref[...], ref[0:n - N_META, :]], axis=0)
        real = ref[p0 - N_META:min(p0 - N_META + n, seq), :]
        if real.shape[0] < n:
            real = jnp.concatenate([real, jnp.zeros((n - real.shape[0], real.shape[1]), real.dtype)], axis=0)
        return real

    def block_rows(i):
        return (i * bq, bq, bq) if i < nq else (nq * bq, N_META, LANES)

    lam = (jnp.exp(jnp.sum(lq1_ref[...] * lk1_ref[...], axis=-1, keepdims=True))
           - jnp.exp(jnp.sum(lq2_ref[...] * lk2_ref[...], axis=-1, keepdims=True)) + LAMBDA_INIT)

    def causal_bias(n_rows, n_cols):
        r = lax.broadcasted_iota(jnp.int32, (n_rows, n_cols), 0)
        c = lax.broadcasted_iota(jnp.int32, (n_rows, n_cols), 1)
        bias = jnp.where(c <= r, 0.0, NEG_BIG)
        return jnp.concatenate([bias, bias], axis=0)

    biases = {bq: causal_bias(bq, bq), N_META: causal_bias(N_META, LANES)}

    def load_q(i, qq_sc):
        p0, n, _ = block_rows(i)
        q = rows_at(q_ref, qm_ref, p0, n)
        lane = lax.broadcasted_iota(jnp.int32, q.shape, 1)
        zero = jnp.zeros_like(q)
        qq_sc[0:n, :] = jnp.where(lane < HEAD_DIM, q, zero)
        qq_sc[n:2 * n, :] = jnp.where(lane >= HEAD_DIM, q, zero)

    def step(state, kb, vb, bias):
        qq_sc, m_sc, acc_sc = state
        first = bias is not None
        s = lax.dot_general(qq_sc[...], kb, (((1,), (1,)), ((), ())), preferred_element_type=F32)
        if first:
            s = s + bias
        n_tiles = s.shape[1] // LANES
        rm = s[:, 0:LANES]
        for t in range(1, n_tiles):
            rm = jnp.maximum(rm, s[:, t * LANES:(t + 1) * LANES])
        rmax = jnp.max(rm, axis=-1, keepdims=True)
        if first:
            m_new = jnp.broadcast_to(rmax, (s.shape[0], LANES))
        else:
            m_prev = m_sc[...]
            m_new = jnp.maximum(m_prev, rmax)
        p = jnp.exp2(s - jnp.concatenate([m_new] * n_tiles, axis=1))
        pv = jnp.dot(p.astype(BF16), vb, preferred_element_type=F32)
        if first:
            acc_sc[...] = pv
        else:
            alpha = jnp.exp2(m_prev - m_new)
            acc_sc[...] = jnp.concatenate([alpha, alpha], axis=1) * acc_sc[...] + pv
        m_sc[...] = m_new

    def key_step(state, i, j):
        p0, n, width = block_rows(i) if j == 0 else ((j - 1) * bq, None, bq)
        kb = rows_at(k_ref, km_ref, p0, width)
        vb = jnp.concatenate([rows_at(v_ref, vm_ref, p0, width), jnp.ones((width, V_DIM), BF16)], axis=1)
        step(state, kb, vb, biases[n] if j == 0 else None)

    def finish(i, acc_sc):
        p0, n, _ = block_rows(i)
        o1 = acc_sc[0:n, 0:V_DIM] / acc_sc[0:n, V_DIM:2 * V_DIM]
        o2 = acc_sc[n:2 * n, 0:V_DIM] / acc_sc[n:2 * n, V_DIM:2 * V_DIM]
        o = o1 - lam * o2
        ms = jnp.mean(o * o, axis=-1, keepdims=True)
        o = (o * lax.rsqrt(ms + EPS) * subg_ref[...] * (1.0 - LAMBDA_INIT)).astype(BF16)
        if p0 == 0:
            o_ref[0:n - N_META, :] = o[N_META:n, :]
        else:
            o_ref[p0 - N_META:p0 - N_META + n, :] = o

    order = []
    for p in range((nq + 2) // 2):
        order += [p] if p == nq - p else [p, nq - p]
    for i in order:
        load_q(i, states[i][0])
    for j in range(nq + 1):
        for i in order:
            if j <= i:
                key_step(states[i], i, j)
    for i in order:
        finish(i, states[i][2])


def _attention(q, k, v, qm, km, vm, lq1, lk1, lq2, lk2, subg, batch, seq, bq):
    nq = seq // bq
    small = lambda b, h: (0, 0)
    head = lambda b, h: (b, h)
    meta = lambda b, h: (0, h)

    def state_scratch(rows):
        return [pltpu.VMEM((2 * rows, V_DIM), BF16), pltpu.VMEM((2 * rows, LANES), F32),
                pltpu.VMEM((2 * rows, 2 * V_DIM), F32)]

    return pl.pallas_call(
        functools.partial(_attn_kernel, bq=bq),
        grid=(batch, N_HEADS),
        in_specs=[
            pl.BlockSpec((1, HEAD_DIM), small),
            pl.BlockSpec((1, HEAD_DIM), small),
            pl.BlockSpec((1, HEAD_DIM), small),
            pl.BlockSpec((1, HEAD_DIM), small),
            pl.BlockSpec((1, V_DIM), small),
            pl.BlockSpec((seq, V_DIM), head),
            pl.BlockSpec((seq, V_DIM), head),
            pl.BlockSpec((seq, V_DIM), head),
            pl.BlockSpec((N_META, V_DIM), meta),
            pl.BlockSpec((N_META, V_DIM), meta),
            pl.BlockSpec((N_META, V_DIM), meta),
        ],
        out_specs=pl.BlockSpec((seq, V_DIM), head),
        out_shape=jax.ShapeDtypeStruct(q.shape, BF16),
        scratch_shapes=state_scratch(bq) * nq + state_scratch(N_META),
        compiler_params=_params("arbitrary", "arbitrary"),
        name="diff_attn",
    )(lq1, lk1, lq2, lk2, subg, q, k, v, qm, km, vm)


def _mix_kernel(x_ref, a_ref, u_ref, uprev_ref, umeta_ref, gate_ref, wa_ref, pw_ref, ps_ref, wp_ref,
                wo_ref, g2_ref, wr_ref, br_ref, h2c_ref, logit_ref, ext_sc, sum_a_sc, sum_b_sc, *, tm,
                tiles_per_seq):
    i = pl.program_id(0)
    first = (i % tiles_per_seq) == 0
    y_attn = jnp.dot(a_ref[...], wa_ref[...], preferred_element_type=F32)

    top = 2 * N_META
    for ref in (ext_sc, sum_a_sc, sum_b_sc):
        ref[0:N_META, :] = jnp.zeros((N_META, POOL_WIDTH), F32)

    @pl.when(first)
    def _():
        ext_sc[N_META:top, :] = umeta_ref[...]

    @pl.when(jnp.logical_not(first))
    def _():
        ext_sc[N_META:top, :] = uprev_ref[...]

    ext_sc[top:top + tm, :] = u_ref[...]

    pooled = []
    for g, w in enumerate(POOL_WINDOWS):
        cs = slice(g * POOL_GROUP_DIM, (g + 1) * POOL_GROUP_DIM)
        src, shift = ext_sc, 1
        for dst in (sum_a_sc, sum_b_sc, sum_a_sc, sum_b_sc):
            if shift >= w:
                break
            dst[N_META:top + tm, cs] = src[N_META:top + tm, cs] + src[N_META - shift:top + tm - shift, cs]
            src, shift = dst, 2 * shift
        y = (src[top:top + tm, cs] * (1.0 / w) - ext_sc[top:top + tm, cs]).astype(BF16)
        pooled.append(jnp.dot(y, pw_ref[g], preferred_element_type=F32))
    yp = (jnp.concatenate(pooled, axis=1) * ps_ref[...]).astype(BF16)
    y_pool = jnp.dot(yp, wp_ref[...], preferred_element_type=F32)
    mixed = (gate_ref[:, 0:D_MODEL].astype(F32) * y_attn
             + gate_ref[:, D_MODEL:2 * D_MODEL].astype(F32) * y_pool)
    h2 = x_ref[...] + jnp.dot(mixed.astype(BF16), wo_ref[...], preferred_element_type=F32)
    _store_token_major(h2c_ref, h2, tm)
    ms = jnp.mean(h2 * h2, axis=-1, keepdims=True)
    hn2 = h2 * lax.rsqrt(ms + EPS) * g2_ref[...]
    hn2_hi = hn2.astype(BF16)
    hn2_lo = (hn2 - hn2_hi.astype(F32)).astype(BF16)
    wr = wr_ref[...]
    wr_hi = wr.astype(BF16)
    wr_lo = (wr - wr_hi.astype(F32)).astype(BF16)
    lhs = jnp.concatenate([hn2_hi, hn2_lo, hn2_hi], axis=1)
    rhs = jnp.concatenate([wr_hi, wr_hi, wr_lo], axis=0)
    for half in range(2):
        rows = slice(half * tm // 2, (half + 1) * tm // 2)
        logit_ref[rows, :] = jnp.dot(lhs[rows, :], rhs, preferred_element_type=F32) + br_ref[...]


def _mix(x2, a, u, umeta, gates, w_attn, pool_w, pool_scale, w_pool, w_out, g2, wr, br, seq, tm):
    n = x2.shape[0]
    tiles_per_seq = seq // tm
    halo_blocks = tm // N_META
    const2 = lambda i: (0, 0)
    return pl.pallas_call(
        functools.partial(_mix_kernel, tm=tm, tiles_per_seq=tiles_per_seq),
        grid=(n // tm,),
        in_specs=[
            pl.BlockSpec((tm, D_MODEL), lambda i: (i, 0)),
            pl.BlockSpec((tm, D_MODEL), lambda i: (i, 0)),
            pl.BlockSpec((tm, POOL_WIDTH), lambda i: (i, 0)),
            pl.BlockSpec((N_META, POOL_WIDTH), lambda i: (jnp.maximum(i * halo_blocks - 1, 0), 0)),
            pl.BlockSpec((N_META, POOL_WIDTH), const2),
            pl.BlockSpec((tm, 2 * D_MODEL), lambda i: (i, 0)),
            pl.BlockSpec((D_MODEL, D_MODEL), const2),
            pl.BlockSpec((N_POOL_GROUPS, POOL_GROUP_DIM, POOL_GROUP_DIM), lambda i: (0, 0, 0)),
            pl.BlockSpec((1, POOL_WIDTH), const2),
            pl.BlockSpec((POOL_WIDTH, D_MODEL), const2),
            pl.BlockSpec((D_MODEL, D_MODEL), const2),
            pl.BlockSpec((1, D_MODEL), const2),
            pl.BlockSpec((D_MODEL, LANES), const2),
            pl.BlockSpec((1, LANES), const2),
        ],
        out_specs=[
            pl.BlockSpec((tm * TOKEN_ROWS, LANES), lambda i: (i, 0)),
            pl.BlockSpec((tm, LANES), lambda i: (i, 0)),
        ],
        out_shape=[
            jax.ShapeDtypeStruct((n * TOKEN_ROWS, LANES), F32),
            jax.ShapeDtypeStruct((n, LANES), F32),
        ],
        scratch_shapes=[pltpu.VMEM((tm + 2 * N_META, POOL_WIDTH), F32)] * 3,
        compiler_params=_params("arbitrary"),
        name="mix",
    )(x2, a, u, u, umeta, gates, w_attn, pool_w, pool_scale, w_pool, w_out, g2, wr, br)


_TBL_EA, _TBL_EB, _TBL_NEW_A, _TBL_NEW_B, _TBL_NVALID, _TBL_NUSED = range(6)


def _route_kernel(logit_ref, tri_ref, wts_ref, dest_ref, tbl_ref, carry_sc, ids_sc, *, tme):
    i = pl.program_id(1)
    pl.when(pl.program_id(0) == 0)(functools.partial(_route_classify, i, logit_ref, tri_ref, wts_ref,
                                                     carry_sc, ids_sc))
    pl.when(pl.program_id(0) == 1)(functools.partial(_route_place, i, dest_ref, tbl_ref, carry_sc,
                                                     ids_sc, tme))


def _route_classify(i, logit_ref, tri_ref, wts_ref, carry_sc, ids_sc):
    @pl.when(i == 0)
    def _():
        carry_sc[...] = jnp.zeros_like(carry_sc)

    lg = logit_ref[...].T
    g = [lg[r:r + 1, :] for r in range(N_GROUPS)]
    gmax = functools.reduce(jnp.maximum, g)
    gsel = jnp.full_like(g[0], N_GROUPS - 1).astype(jnp.int32)
    for r in range(N_GROUPS - 2, -1, -1):
        gsel = jnp.where(g[r] == gmax, r, gsel)
    p_group = 1.0 / functools.reduce(lambda a, b: a + b, [jnp.exp(x - gmax) for x in g])
    e = []
    for jj in range(EXPERTS_PER_GROUP):
        v = lg[N_GROUPS + jj:N_GROUPS + jj + 1, :]
        for r in range(1, N_GROUPS):
            row = N_GROUPS + r * EXPERTS_PER_GROUP + jj
            v = jnp.where(gsel == r, lg[row:row + 1, :], v)
        e.append(v)
    v1 = functools.reduce(jnp.maximum, e)
    i1 = jnp.full_like(gsel, EXPERTS_PER_GROUP - 1)
    for jj in range(EXPERTS_PER_GROUP - 2, -1, -1):
        i1 = jnp.where(e[jj] == v1, jj, i1)
    rest = [jnp.where(i1 == jj, -jnp.inf, e[jj]) for jj in range(EXPERTS_PER_GROUP)]
    v2 = functools.reduce(jnp.maximum, rest)
    i2 = jnp.full_like(gsel, EXPERTS_PER_GROUP - 1)
    for jj in range(EXPERTS_PER_GROUP - 2, -1, -1):
        i2 = jnp.where(jnp.logical_and(rest[jj] == v2, i1 != jj), jj, i2)
    t = jnp.exp(v2 - v1)
    w1 = p_group / (1.0 + t)
    w2 = p_group * t / (1.0 + t)
    lo = jnp.minimum(i1, i2)
    hi = jnp.maximum(i1, i2)
    pair = jnp.where(lo == 0, hi - 1, jnp.where(lo == 1, jnp.where(hi == 3, 3, 4), 5))
    bucket = gsel * N_PAIRS + pair
    w_of_lo = jnp.where(i1 < i2, w1, w2)
    w_of_hi = jnp.where(i1 < i2, w2, w1)
    swapped = pair == 5
    w_a = jnp.where(swapped, w_of_hi, w_of_lo)
    w_b = jnp.where(swapped, w_of_lo, w_of_hi)

    tl = lg.shape[1]
    bid = lax.broadcasted_iota(jnp.int32, (N_BUCKET_ROWS, tl), 0)
    hot = bid == bucket
    hot_b = jnp.where(hot, 1.0, 0.0).astype(BF16)
    prefix = jnp.dot(hot_b, tri_ref[...], preferred_element_type=F32)
    carry = carry_sc[...]
    before = prefix + jnp.concatenate([carry] * (tl // LANES), axis=1)
    rank = jnp.sum(jnp.where(hot, before, 0.0), axis=0, keepdims=True)
    carry_sc[...] = carry + jnp.dot(hot_b, jnp.ones((tl, LANES), BF16), preferred_element_type=F32)
    pad = jnp.zeros((6, tl), F32)
    ids_sc[i] = jnp.concatenate([bucket.astype(F32), rank, pad], axis=0)
    wts_ref[...] = jnp.concatenate([w_a, w_b, pad], axis=0)


def _route_place(i, dest_ref, tbl_ref, carry_sc, ids_sc, tme):
    cnt = carry_sc[...]
    tiles = jnp.floor((cnt + (tme - 1)) * (1.0 / tme))
    r = lax.broadcasted_iota(jnp.int32, (N_BUCKET_ROWS, N_BUCKET_ROWS), 0)
    c = lax.broadcasted_iota(jnp.int32, (N_BUCKET_ROWS, N_BUCKET_ROWS), 1)
    tile_end = jnp.dot(jnp.where(c <= r, 1.0, 0.0).astype(BF16), tiles.astype(BF16),
                       preferred_element_type=F32)
    tile_start = tile_end - tiles

    ids = ids_sc[i]
    bucket, rank = ids[0:1, :], ids[1:2, :]
    tl = ids.shape[1]
    bid = lax.broadcasted_iota(jnp.int32, (N_BUCKET_ROWS, tl), 0).astype(F32)
    row_start = jnp.concatenate([tile_start * tme] * (tl // LANES), axis=1)
    dest = rank + jnp.sum(jnp.where(bid == bucket, row_start, 0.0), axis=0, keepdims=True)
    dest_ref[...] = jnp.concatenate([dest, jnp.zeros((7, tl), F32)], axis=0).astype(jnp.int32)

    lane = lax.broadcasted_iota(jnp.int32, (1, LANES), 1).astype(F32)
    n_used = tile_end[N_BUCKET_ROWS - 1:N_BUCKET_ROWS, :]
    last = jnp.minimum(lane, n_used - 1.0)
    tb = jnp.sum(jnp.where(tile_end <= last, 1.0, 0.0), axis=0, keepdims=True)
    tb = jnp.minimum(tb, N_BUCKETS - 1.0)
    group = functools.reduce(lambda a, b: a + b,
                             [jnp.where(tb >= g * N_PAIRS, 1.0, 0.0) for g in range(1, N_GROUPS)])
    pair = tb - N_PAIRS * group

    def lookup(table):
        out = jnp.full_like(pair, float(table[-1]))
        for k in range(len(table) - 2, -1, -1):
            out = jnp.where(pair == k, float(table[k]), out)
        return EXPERTS_PER_GROUP * group + out

    e_a, e_b = lookup(_PAIR_A), lookup(_PAIR_B)
    changed = lambda e: jnp.where(lane == 0.0, 1.0, jnp.where(e != pltpu.roll(e, 1, 1), 1.0, 0.0))
    bid128 = lax.broadcasted_iota(jnp.int32, (N_BUCKET_ROWS, LANES), 0).astype(F32)
    at_tile = lambda col: jnp.sum(jnp.where(bid128 == tb, col, 0.0), axis=0, keepdims=True)
    n_valid = jnp.clip(at_tile(cnt) - tme * (lane - at_tile(tile_start)), 0.0, float(tme))
    rows = [e_a, e_b, changed(e_a), changed(e_b), n_valid, n_used, jnp.zeros_like(lane),
            jnp.zeros_like(lane)]
    tbl_ref[...] = jnp.concatenate(rows, axis=0).astype(jnp.int32)


def _route(logits, tl, tme):
    n = logits.shape[0]
    nt = n // tl
    tri = jnp.asarray(np.triu(np.ones((tl, tl), np.float32), k=1), dtype=BF16)
    first_pass = lambda p, i: i * (1 - p) + (nt - 1) * p
    return pl.pallas_call(
        functools.partial(_route_kernel, tme=tme),
        grid=(2, nt),
        in_specs=[pl.BlockSpec((tl, LANES), lambda p, i: (first_pass(p, i), 0)),
                  pl.BlockSpec((tl, tl), lambda p, i: (0, 0))],
        out_specs=[pl.BlockSpec((8, tl), lambda p, i: (0, first_pass(p, i))),
                   pl.BlockSpec((8, tl), lambda p, i: (0, i * p)),
                   pl.BlockSpec((8, LANES), lambda p, i: (0, 0))],
        out_shape=[jax.ShapeDtypeStruct((8, n), F32),
                   jax.ShapeDtypeStruct((8, n), jnp.int32),
                   jax.ShapeDtypeStruct((8, LANES), jnp.int32)],
        scratch_shapes=[pltpu.VMEM((N_BUCKET_ROWS, LANES), F32), pltpu.VMEM((nt, 8, tl), F32)],
        compiler_params=_params("arbitrary", "arbitrary"),
        name="route",
    )(logits, tri)


def _token_copy(src_hbm, src_token, dst_buf, dst_row, sem):
    return pltpu.make_async_copy(
        src_hbm.at[pl.ds(pl.multiple_of(src_token * TOKEN_ROWS, TOKEN_ROWS), TOKEN_ROWS), :],
        dst_buf.at[pl.ds(pl.multiple_of(dst_row * TOKEN_ROWS, TOKEN_ROWS), TOKEN_ROWS), :],
        sem)


def _start_gather(idx_ref, first, n_groups, src_hbm, dst_buf, sem):
    def body(rr, carry):
        for uu in range(GATHER_UNROLL):
            r = rr * GATHER_UNROLL + uu
            _token_copy(src_hbm, idx_ref[first + r], dst_buf, r, sem).start(priority=uu % 2)
        return carry

    lax.fori_loop(0, n_groups, body, 0)


def _start_padded_gather(idx_ref, first, n_valid, rows, src_hbm, dst_buf, sem):
    n_groups = (n_valid + GATHER_UNROLL - 1) // GATHER_UNROLL
    _start_gather(idx_ref, first, n_groups, src_hbm, dst_buf, sem)
    pad_groups = rows // GATHER_UNROLL - n_groups
    off = n_groups * GATHER_UNROLL
    size = rows // 2
    while size >= GATHER_UNROLL:
        take = jnp.bitwise_and(pad_groups, size // GATHER_UNROLL) != 0

        @pl.when(take)
        def _(off=off, size=size):
            pltpu.make_async_copy(
                src_hbm.at[pl.ds(0, size * TOKEN_ROWS), :],
                dst_buf.at[pl.ds(pl.multiple_of(off * TOKEN_ROWS, GATHER_UNROLL * TOKEN_ROWS),
                                 size * TOKEN_ROWS), :],
                sem).start()

        off = off + jnp.where(take, size, 0)
        size //= 2


def _wait_gather(rows, src_hbm, dst_buf, sem):
    pltpu.make_async_copy(src_hbm.at[pl.ds(0, rows * TOKEN_ROWS), :], dst_buf, sem).wait()


def _moe_kernel(tbl_ref, src_ref, h2c_hbm, wts_ref, g2_ref, wg_a_ref, wu_a_ref, wd_a_ref, wg_b_ref,
                wu_b_ref, wd_b_ref, yc_ref, xbuf, wg_a_sc, wu_a_sc, wd_a_sc, wg_b_sc, wu_b_sc, wd_b_sc,
                sem, *, tme):
    i = pl.program_id(0)
    n_used = tbl_ref[_TBL_NUSED, 0]
    slot = jnp.bitwise_and(i, 1)

    @pl.when(i == 0)
    def _():
        _start_padded_gather(src_ref, 0, tbl_ref[_TBL_NVALID, 0], tme, h2c_hbm, xbuf.at[0], sem.at[0])

    @pl.when(i + 1 < n_used)
    def _():
        _start_padded_gather(src_ref, (i + 1) * tme, tbl_ref[_TBL_NVALID, i + 1], tme, h2c_hbm,
                             xbuf.at[1 - slot], sem.at[1 - slot])

    @pl.when(tbl_ref[_TBL_NEW_A, i] == 1)
    def _():
        for src, dst in ((wg_a_ref, wg_a_sc), (wu_a_ref, wu_a_sc), (wd_a_ref, wd_a_sc)):
            dst[...] = src[0].astype(BF16)

    @pl.when(tbl_ref[_TBL_NEW_B, i] == 1)
    def _():
        for src, dst in ((wg_b_ref, wg_b_sc), (wu_b_ref, wu_b_sc), (wd_b_ref, wd_b_sc)):
            dst[...] = src[0].astype(BF16)

    @pl.when(i < n_used)
    def _():
        _wait_gather(tme, h2c_hbm, xbuf.at[slot], sem.at[slot])
        x = _load_token_major(xbuf.at[slot], tme)
        ms = jnp.mean(x * x, axis=-1, keepdims=True)
        hn = (x * lax.rsqrt(ms + EPS) * g2_ref[...]).astype(BF16)
        wts = wts_ref[...]

        def expert_act(wg_sc, wu_sc, w):
            gate = jnp.dot(hn, wg_sc[...], preferred_element_type=F32)
            up = jnp.dot(hn, wu_sc[...], preferred_element_type=F32)
            return (gate / (1.0 + jnp.exp(-gate)) * up * w).astype(BF16)

        y = (jnp.dot(expert_act(wg_a_sc, wu_a_sc, wts[:, 0:1]), wd_a_sc[...], preferred_element_type=F32)
             + jnp.dot(expert_act(wg_b_sc, wu_b_sc, wts[:, 1:2]), wd_b_sc[...], preferred_element_type=F32))
        _store_token_major(yc_ref, y, tme)

    @pl.when(i >= n_used)
    def _():
        yc_ref[...] = jnp.zeros_like(yc_ref)


def _moe(tbl, src, h2c, wts_sorted, g2, wg, wu, wd, tme, n_tiles):
    used = lambda i, tbl, src: (jnp.minimum(i, tbl[_TBL_NUSED, 0] - 1), 0)
    w_a = lambda i, tbl, src: (tbl[_TBL_EA, i], 0, 0)
    w_b = lambda i, tbl, src: (tbl[_TBL_EB, i], 0, 0)
    up_shape, down_shape = (D_MODEL, D_EXPERT), (D_EXPERT, D_MODEL)
    grid_spec = pltpu.PrefetchScalarGridSpec(
        num_scalar_prefetch=2,
        grid=(n_tiles,),
        in_specs=[
            pl.BlockSpec(memory_space=pl.ANY),
            pl.BlockSpec((tme, wts_sorted.shape[1]), used),
            pl.BlockSpec((1, D_MODEL), lambda i, tbl, src: (0, 0)),
            pl.BlockSpec((1,) + up_shape, w_a),
            pl.BlockSpec((1,) + up_shape, w_a),
            pl.BlockSpec((1,) + down_shape, w_a),
            pl.BlockSpec((1,) + up_shape, w_b),
            pl.BlockSpec((1,) + up_shape, w_b),
            pl.BlockSpec((1,) + down_shape, w_b),
        ],
        out_specs=pl.BlockSpec((tme * TOKEN_ROWS, LANES), lambda i, tbl, src: (i, 0)),
        scratch_shapes=[pltpu.VMEM((2, tme * TOKEN_ROWS, LANES), F32)]
        + [pltpu.VMEM(s, BF16) for s in (up_shape, up_shape, down_shape) * 2]
        + [pltpu.SemaphoreType.DMA((2,))],
    )
    return pl.pallas_call(
        functools.partial(_moe_kernel, tme=tme),
        grid_spec=grid_spec,
        out_shape=jax.ShapeDtypeStruct((n_tiles * tme * TOKEN_ROWS, LANES), F32),
        compiler_params=_params("arbitrary"),
        name="moe",
    )(tbl, src, h2c, wts_sorted, g2, wg, wu, wd, wg, wu, wd)


def _final_kernel(dest_ref, h2c_ref, yc_hbm, gf_ref, o_ref, ybuf, sem, *, tmf):
    i = pl.program_id(0)
    slot = jnp.bitwise_and(i, 1)

    @pl.when(i == 0)
    def _():
        _start_gather(dest_ref, 0, tmf // GATHER_UNROLL, yc_hbm, ybuf.at[0], sem.at[0])

    @pl.when(i + 1 < pl.num_programs(0))
    def _():
        _start_gather(dest_ref, (i + 1) * tmf, tmf // GATHER_UNROLL, yc_hbm, ybuf.at[1 - slot],
                      sem.at[1 - slot])

    _wait_gather(tmf, yc_hbm, ybuf.at[slot], sem.at[slot])
    h = _load_token_major(h2c_ref, tmf) + _load_token_major(ybuf.at[slot], tmf)
    ms = jnp.mean(h * h, axis=-1, keepdims=True)
    o_ref[...] = h * lax.rsqrt(ms + EPS) * gf_ref[...]


def _final(dest, h2c, yc, gf, tmf):
    n = h2c.shape[0] // TOKEN_ROWS
    grid_spec = pltpu.PrefetchScalarGridSpec(
        num_scalar_prefetch=1,
        grid=(n // tmf,),
        in_specs=[
            pl.BlockSpec((tmf * TOKEN_ROWS, LANES), lambda i, dest: (i, 0)),
            pl.BlockSpec(memory_space=pl.ANY),
            pl.BlockSpec((1, D_MODEL), lambda i, dest: (0, 0)),
        ],
        out_specs=pl.BlockSpec((tmf, D_MODEL), lambda i, dest: (i, 0)),
        scratch_shapes=[pltpu.VMEM((2, tmf * TOKEN_ROWS, LANES), F32), pltpu.SemaphoreType.DMA((2,))],
    )
    return pl.pallas_call(
        functools.partial(_final_kernel, tmf=tmf),
        grid_spec=grid_spec,
        out_shape=jax.ShapeDtypeStruct((n, D_MODEL), F32),
        compiler_params=_params("arbitrary"),
        name="final",
    )(dest, h2c, yc, gf)


def _rope_tables(t):
    inv = 1.0 / (ROPE_THETA ** (np.arange(0, HEAD_DIM, 2, dtype=np.float64) / HEAD_DIM))
    ang = np.arange(t, dtype=np.float64)[:, None] * inv[None, :]
    ang = np.concatenate([ang, ang, ang, ang], axis=-1)
    sign = np.where((np.arange(LANES) % HEAD_DIM) < HEAD_DIM // 2, -1.0, 1.0)
    return (jnp.asarray(np.cos(ang), dtype=F32), jnp.asarray(np.sin(ang) * sign[None, :], dtype=F32))


def kernel(x, meta, norm1_g, w_in, b_gate, lambda_q1, lambda_k1, lambda_q2, lambda_k2, subln_g, pool_w,
           pool_scale, w_attn_br, w_pool_br, w_out, norm2_g, w_router_group, b_router_group,
           w_router_expert, b_router_expert, w_e_gate, w_e_up, w_e_down, final_g):
    batch, seq, d = x.shape
    n = batch * seq
    x2 = x.reshape(n, d)
    cos, sin = _rope_tables(seq + N_META)
    q, k, v, u, gates = _inproj(x2, norm1_g, cos[N_META:], sin[N_META:], w_in[0], b_gate, tm=1024)
    qm, km, vm, um, _ = _inproj(meta, norm1_g, cos[:N_META], sin[:N_META], w_in[0], b_gate, tm=N_META)
    a = _attention(q, k, v, qm, km, vm, lambda_q1, lambda_k1, lambda_q2, lambda_k2, subln_g, batch, seq,
                   bq=256)

    n_router = N_GROUPS + N_EXPERTS
    wr = jnp.pad(jnp.concatenate([w_router_group[0], w_router_expert[0]], axis=1),
                 ((0, 0), (0, LANES - n_router)))
    br = jnp.pad(jnp.concatenate([b_router_group[0], b_router_expert[0]]), (0, LANES - n_router))
    h2c, logits = _mix(x2, a, u, um, gates, w_attn_br[0].astype(BF16), pool_w[0].astype(BF16),
                       pool_scale, w_pool_br[0].astype(BF16), w_out[0].astype(BF16), norm2_g,
                       wr, br.reshape(1, LANES), seq, tm=512)
    tme = 256
    n_tiles = n // tme + N_BUCKETS
    assert n_tiles <= LANES
    wts, dest, tile_table = _route(logits, tl=1024, tme=tme)
    dest = dest[0]
    per_token = jnp.concatenate([wts[0:2], jnp.arange(n, dtype=F32)[None, :]], axis=0).T
    per_row = jnp.zeros((n_tiles * tme, 3), F32).at[dest].set(per_token)
    src = per_row[:, 2].astype(jnp.int32)

    yc = _moe(tile_table, src, h2c, per_row, norm2_g, w_e_gate[0], w_e_up[0], w_e_down[0], tme, n_tiles)
    out = _final(dest, h2c, yc, final_g.reshape(1, d), tmf=512)
    return out.reshape(batch, seq, d)
```

```python
import functools
import math

import jax
import jax.numpy as jnp
import numpy as np
from jax import lax
from jax.experimental import pallas as pl
from jax.experimental.pallas import tpu as pltpu

D_MODEL = 1024
N_META = 16
N_HEADS = 8
HEAD_DIM = 64
V_DIM = 2 * HEAD_DIM
POOL_WINDOWS = (2, 4, 8, 16)
N_POOL_GROUPS = len(POOL_WINDOWS)
POOL_GROUP_DIM = 128
POOL_WIDTH = N_POOL_GROUPS * POOL_GROUP_DIM
ROPE_THETA = 10000.0
N_GROUPS = 4
EXPERTS_PER_GROUP = 4
N_EXPERTS = N_GROUPS * EXPERTS_PER_GROUP
D_EXPERT = 512
EPS = 1e-6
LAMBDA_INIT = 0.8 - 0.6 * math.exp(-0.3 * 0)
LOG2_E = math.log2(math.e)

_PAIR_A = (0, 0, 0, 1, 1, 3)
_PAIR_B = (1, 2, 3, 3, 2, 2)
N_PAIRS = len(_PAIR_A)
N_BUCKETS = N_GROUPS * N_PAIRS
N_BUCKET_ROWS = 32
GATHER_UNROLL = 8

LANES = 128
NEG_BIG = -1e30
VMEM_LIMIT = 48 * 1024 * 1024

F32 = jnp.float32
BF16 = jnp.bfloat16

_HEADS_PER_STEP = 2
_IN_STEPS = N_HEADS // _HEADS_PER_STEP
_QKV_BLK = _HEADS_PER_STEP * V_DIM
_GATE_BLK = 2 * D_MODEL // _IN_STEPS
_O_K = D_MODEL // _QKV_BLK
_O_V = 2 * D_MODEL // _QKV_BLK
_O_U = 3 * D_MODEL // POOL_GROUP_DIM
_O_G = (3 * D_MODEL + POOL_WIDTH) // _GATE_BLK


def _params(*sem):
    return pltpu.CompilerParams(dimension_semantics=sem, vmem_limit_bytes=VMEM_LIMIT)


TOKEN_ROWS = D_MODEL // LANES


def _store_token_major(ref, val, rows):
    for s in range(TOKEN_ROWS):
        ref[pl.ds(s, rows, stride=TOKEN_ROWS), :] = val[:, s * LANES:(s + 1) * LANES]


def _load_token_major(ref, rows):
    return jnp.concatenate([ref[pl.ds(s, rows, stride=TOKEN_ROWS), :] for s in range(TOKEN_ROWS)], axis=1)


def _rope(z, cos, sin_signed, first_half):
    outs = []
    for c in range(z.shape[1] // LANES):
        zc = z[:, c * LANES:(c + 1) * LANES]
        rot = jnp.where(first_half, pltpu.roll(zc, LANES - HEAD_DIM // 2, 1),
                        pltpu.roll(zc, HEAD_DIM // 2, 1))
        outs.append(zc * cos + rot * sin_signed)
    return jnp.concatenate(outs, axis=1)


def _inproj_kernel(x_ref, meta_ref, g_ref, cos_ref, sin_ref, cosm_ref, sinm_ref, wq_ref, wk_ref, wv_ref,
                   wu_ref, wg_ref, bg_ref, q_ref, k_ref, v_ref, u_ref, gate_ref, qm_ref, km_ref, vm_ref,
                   um_ref, wq_sc, wk_sc, wv_sc, wu_sc, wg_sc):
    def normed(x):
        ms = jnp.mean(x * x, axis=-1, keepdims=True)
        return (x * lax.rsqrt(ms + EPS) * g_ref[...]).astype(BF16)

    def project(hn, cos, sin, q_out, k_out, v_out, u_out, gate_out):
        lane = lax.broadcasted_iota(jnp.int32, cos.shape, 1)
        first_half = jnp.bitwise_and(lane, HEAD_DIM - 1) < (HEAD_DIM // 2)
        if gate_out is not None:
            zg = jnp.dot(hn, wg_sc[...], preferred_element_type=F32) + bg_ref[...]
            gate_out[...] = (1.0 / (1.0 + jnp.exp(-zg))).astype(BF16)
        zq = jnp.dot(hn, wq_sc[...], preferred_element_type=F32)
        q_out[...] = (_rope(zq, cos, sin, first_half) * (LOG2_E / math.sqrt(HEAD_DIM))).astype(BF16)
        zk = jnp.dot(hn, wk_sc[...], preferred_element_type=F32)
        k_out[...] = _rope(zk, cos, sin, first_half).astype(BF16)
        u_out[...] = jnp.dot(hn, wu_sc[...], preferred_element_type=F32)
        v_out[...] = jnp.dot(hn, wv_sc[...], preferred_element_type=F32).astype(BF16)

    @pl.when(pl.program_id(1) == 0)
    def _():
        for src, dst in ((wq_ref, wq_sc), (wk_ref, wk_sc), (wv_ref, wv_sc), (wu_ref, wu_sc),
                         (wg_ref, wg_sc)):
            dst[...] = src[...].astype(BF16)
        project(normed(meta_ref[...]), cosm_ref[...], sinm_ref[...], qm_ref, km_ref, vm_ref, um_ref, None)

    project(normed(x_ref[...]), cos_ref[...], sin_ref[...], q_ref, k_ref, v_ref, u_ref, gate_ref)


def _inproj(x2, meta, g1, cos, sin, w_in, b_gate, tm):
    n = x2.shape[0]
    n_meta = meta.shape[0]
    n_pos_blocks = (cos.shape[0] - n_meta) // tm
    row = lambda j, i: (i, 0)
    pos = lambda j, i: (i % n_pos_blocks, 0)
    whole = lambda j, i: (0, 0)
    return pl.pallas_call(
        _inproj_kernel,
        grid=(_IN_STEPS, n // tm),
        in_specs=[
            pl.BlockSpec((tm, D_MODEL), row),
            pl.BlockSpec((n_meta, D_MODEL), whole),
            pl.BlockSpec((1, D_MODEL), whole),
            pl.BlockSpec((tm, LANES), pos),
            pl.BlockSpec((tm, LANES), pos),
            pl.BlockSpec((n_meta, LANES), whole),
            pl.BlockSpec((n_meta, LANES), whole),
            pl.BlockSpec((D_MODEL, _QKV_BLK), lambda j, i: (0, j)),
            pl.BlockSpec((D_MODEL, _QKV_BLK), lambda j, i: (0, _O_K + j)),
            pl.BlockSpec((D_MODEL, _QKV_BLK), lambda j, i: (0, _O_V + j)),
            pl.BlockSpec((D_MODEL, POOL_GROUP_DIM), lambda j, i: (0, _O_U + j)),
            pl.BlockSpec((D_MODEL, _GATE_BLK), lambda j, i: (0, _O_G + j)),
            pl.BlockSpec((1, _GATE_BLK), lambda j, i: (0, j)),
        ],
        out_specs=[
            pl.BlockSpec((tm, _QKV_BLK), lambda j, i: (i, j)),
            pl.BlockSpec((tm, _QKV_BLK), lambda j, i: (i, j)),
            pl.BlockSpec((tm, _QKV_BLK), lambda j, i: (i, j)),
            pl.BlockSpec((tm, POOL_GROUP_DIM), lambda j, i: (i, j)),
            pl.BlockSpec((tm, _GATE_BLK), lambda j, i: (i, j)),
            pl.BlockSpec((n_meta, _QKV_BLK), lambda j, i: (0, j)),
            pl.BlockSpec((n_meta, _QKV_BLK), lambda j, i: (0, j)),
            pl.BlockSpec((n_meta, _QKV_BLK), lambda j, i: (0, j)),
            pl.BlockSpec((n_meta, POOL_GROUP_DIM), lambda j, i: (0, j)),
        ],
        out_shape=[
            jax.ShapeDtypeStruct((n, D_MODEL), BF16),
            jax.ShapeDtypeStruct((n, D_MODEL), BF16),
            jax.ShapeDtypeStruct((n, D_MODEL), BF16),
            jax.ShapeDtypeStruct((n, POOL_WIDTH), F32),
            jax.ShapeDtypeStruct((n, 2 * D_MODEL), BF16),
            jax.ShapeDtypeStruct((n_meta, D_MODEL), BF16),
            jax.ShapeDtypeStruct((n_meta, D_MODEL), BF16),
            jax.ShapeDtypeStruct((n_meta, D_MODEL), BF16),
            jax.ShapeDtypeStruct((n_meta, POOL_WIDTH), F32),
        ],
        scratch_shapes=[
            pltpu.VMEM((D_MODEL, _QKV_BLK), BF16),
            pltpu.VMEM((D_MODEL, _QKV_BLK), BF16),
            pltpu.VMEM((D_MODEL, _QKV_BLK), BF16),
            pltpu.VMEM((D_MODEL, POOL_GROUP_DIM), BF16),
            pltpu.VMEM((D_MODEL, _GATE_BLK), BF16),
        ],
        compiler_params=_params("arbitrary", "arbitrary"),
        name="inproj",
    )(x2, meta, g1, cos[n_meta:], sin[n_meta:], cos[:n_meta], sin[:n_meta], w_in, w_in, w_in, w_in, w_in,
      b_gate)


def _attn_kernel(lq1_ref, lk1_ref, lq2_ref, lk2_ref, subg_ref, q_ref, k_ref, v_ref, qm_ref, km_ref,
                 vm_ref, o_ref, *state_sc, bq):
    seq = k_ref.shape[0]
    nq = seq // bq
    states = [tuple(state_sc[3 * g:3 * g + 3]) for g in range(nq + 1)]

    def rows_at(ref, meta_ref, p0, n):
        if p0 == 0:
            return jnp.concatenate([meta_ref[...], ref[0:n - N_META, :]], axis=0)
        real = ref[p0 - N_META:min(p0 - N_META + n, seq), :]
        if real.shape[0] < n:
            real = jnp.concatenate([real, jnp.zeros((n - real.shape[0], real.shape[1]), real.dtype)], axis=0)
        return real

    def block_rows(i):
        return (i * bq, bq, bq) if i < nq else (nq * bq, N_META, LANES)

    lam = (jnp.exp(jnp.sum(lq1_ref[...] * lk1_ref[...], axis=-1, keepdims=True))
           - jnp.exp(jnp.sum(lq2_ref[...] * lk2_ref[...], axis=-1, keepdims=True)) + LAMBDA_INIT)

    def causal_bias(n_rows, n_cols):
        r = lax.broadcasted_iota(jnp.int32, (n_rows, n_cols), 0)
        c = lax.broadcasted_iota(jnp.int32, (n_rows, n_cols), 1)
        bias = jnp.where(c <= r, 0.0, NEG_BIG)
        return jnp.concatenate([bias, bias], axis=0)

    biases = {bq: causal_bias(bq, bq), N_META: causal_bias(N_META, LANES)}

    def load_q(i, qq_sc):
        p0, n, _ = block_rows(i)
        q = rows_at(q_ref, qm_ref, p0, n)
        lane = lax.broadcasted_iota(jnp.int32, q.shape, 1)
        zero = jnp.zeros_like(q)
        qq_sc[0:n, :] = jnp.where(lane < HEAD_DIM, q, zero)
        qq_sc[n:2 * n, :] = jnp.where(lane >= HEAD_DIM, q, zero)

    def step(state, kb, vb, bias):
        qq_sc, m_sc, acc_sc = state
        first = bias is not None
        s = lax.dot_general(qq_sc[...], kb, (((1,), (1,)), ((), ())), preferred_element_type=F32)
        if first:
            s = s + bias
        n_tiles = s.shape[1] // LANES
        rm = s[:, 0:LANES]
        for t in range(1, n_tiles):
            rm = jnp.maximum(rm, s[:, t * LANES:(t + 1) * LANES])
        rmax = jnp.max(rm, axis=-1, keepdims=True)
        if first:
            m_new = jnp.broadcast_to(rmax, (s.shape[0], LANES))
        else:
            m_prev = m_sc[...]
            m_new = jnp.maximum(m_prev, rmax)
        p = jnp.exp2(s - jnp.concatenate([m_new] * n_tiles, axis=1))
        pv = jnp.dot(p.astype(BF16), vb, preferred_element_type=F32)
        if first:
            acc_sc[...] = pv
        else:
            alpha = jnp.exp2(m_prev - m_new)
            acc_sc[...] = jnp.concatenate([alpha, alpha], axis=1) * acc_sc[...] + pv
        m_sc[...] = m_new

    def key_step(state, i, j):
        p0, n, width = block_rows(i) if j == 0 else ((j - 1) * bq, None, bq)
        kb = rows_at(k_ref, km_ref, p0, width)
        vb = jnp.concatenate([rows_at(v_ref, vm_ref, p0, width), jnp.ones((width, V_DIM), BF16)], axis=1)
        step(state, kb, vb, biases[n] if j == 0 else None)

    def finish(i, acc_sc):
        p0, n, _ = block_rows(i)
        o1 = acc_sc[0:n, 0:V_DIM] / acc_sc[0:n, V_DIM:2 * V_DIM]
        o2 = acc_sc[n:2 * n, 0:V_DIM] / acc_sc[n:2 * n, V_DIM:2 * V_DIM]
        o = o1 - lam * o2
        ms = jnp.mean(o * o, axis=-1, keepdims=True)
        o = (o * lax.rsqrt(ms + EPS) * subg_ref[...] * (1.0 - LAMBDA_INIT)).astype(BF16)
        if p0 == 0:
            o_ref[0:n - N_META, :] = o[N_META:n, :]
        else:
            o_ref[p0 - N_META:p0 - N_META + n, :] = o

    order = []
    for p in range((nq + 2) // 2):
        order += [p] if p == nq - p else [p, nq - p]
    for i in order:
        load_q(i, states[i][0])
    for j in range(nq + 1):
        for i in order:
            if j <= i:
                key_step(states[i], i, j)
    for i in order:
        finish(i, states[i][2])


def _attention(q, k, v, qm, km, vm, lq1, lk1, lq2, lk2, subg, batch, seq, bq):
    nq = seq // bq
    small = lambda b, h: (0, 0)
    head = lambda b, h: (b, h)
    meta = lambda b, h: (0, h)

    def state_scratch(rows):
        return [pltpu.VMEM((2 * rows, V_DIM), BF16), pltpu.VMEM((2 * rows, LANES), F32),
                pltpu.VMEM((2 * rows, 2 * V_DIM), F32)]

    return pl.pallas_call(
        functools.partial(_attn_kernel, bq=bq),
        grid=(batch, N_HEADS),
        in_specs=[
            pl.BlockSpec((1, HEAD_DIM), small),
            pl.BlockSpec((1, HEAD_DIM), small),
            pl.BlockSpec((1, HEAD_DIM), small),
            pl.BlockSpec((1, HEAD_DIM), small),
            pl.BlockSpec((1, V_DIM), small),
            pl.BlockSpec((seq, V_DIM), head),
            pl.BlockSpec((seq, V_DIM), head),
            pl.BlockSpec((seq, V_DIM), head),
            pl.BlockSpec((N_META, V_DIM), meta),
            pl.BlockSpec((N_META, V_DIM), meta),
            pl.BlockSpec((N_META, V_DIM), meta),
        ],
        out_specs=pl.BlockSpec((seq, V_DIM), head),
        out_shape=jax.ShapeDtypeStruct(q.shape, BF16),
        scratch_shapes=state_scratch(bq) * nq + state_scratch(N_META),
        compiler_params=_params("arbitrary", "arbitrary"),
        name="diff_attn",
    )(lq1, lk1, lq2, lk2, subg, q, k, v, qm, km, vm)


def _mix_kernel(x_ref, a_ref, u_ref, uprev_ref, umeta_ref, gate_ref, wa_ref, pw_ref, ps_ref, wp_ref,
                wo_ref, g2_ref, wr_ref, br_ref, h2c_ref, logit_ref, ext_sc, sum_a_sc, sum_b_sc, *, tm,
                tiles_per_seq):
    i = pl.program_id(0)
    first = (i % tiles_per_seq) == 0
    y_attn = jnp.dot(a_ref[...], wa_ref[...], preferred_element_type=F32)

    top = 2 * N_META
    for ref in (ext_sc, sum_a_sc, sum_b_sc):
        ref[0:N_META, :] = jnp.zeros((N_META, POOL_WIDTH), F32)

    @pl.when(first)
    def _():
        ext_sc[N_META:top, :] = umeta_ref[...]

    @pl.when(jnp.logical_not(first))
    def _():
        ext_sc[N_META:top, :] = uprev_ref[...]

    ext_sc[top:top + tm, :] = u_ref[...]

    pooled = []
    for g, w in enumerate(POOL_WINDOWS):
        cs = slice(g * POOL_GROUP_DIM, (g + 1) * POOL_GROUP_DIM)
        src, shift = ext_sc, 1
        for dst in (sum_a_sc, sum_b_sc, sum_a_sc, sum_b_sc):
            if shift >= w:
                break
            dst[N_META:top + tm, cs] = src[N_META:top + tm, cs] + src[N_META - shift:top + tm - shift, cs]
            src, shift = dst, 2 * shift
        y = (src[top:top + tm, cs] * (1.0 / w) - ext_sc[top:top + tm, cs]).astype(BF16)
        pooled.append(jnp.dot(y, pw_ref[g], preferred_element_type=F32))
    yp = (jnp.concatenate(pooled, axis=1) * ps_ref[...]).astype(BF16)
    y_pool = jnp.dot(yp, wp_ref[...], preferred_element_type=F32)
    mixed = (gate_ref[:, 0:D_MODEL].astype(F32) * y_attn
             + gate_ref[:, D_MODEL:2 * D_MODEL].astype(F32) * y_pool)
    h2 = x_ref[...] + jnp.dot(mixed.astype(BF16), wo_ref[...], preferred_element_type=F32)
    _store_token_major(h2c_ref, h2, tm)
    ms = jnp.mean(h2 * h2, axis=-1, keepdims=True)
    hn2 = h2 * lax.rsqrt(ms + EPS) * g2_ref[...]
    hn2_hi = hn2.astype(BF16)
    hn2_lo = (hn2 - hn2_hi.astype(F32)).astype(BF16)
    wr = wr_ref[...]
    wr_hi = wr.astype(BF16)
    wr_lo = (wr - wr_hi.astype(F32)).astype(BF16)
    lhs = jnp.concatenate([hn2_hi, hn2_lo, hn2_hi], axis=1)
    rhs = jnp.concatenate([wr_hi, wr_hi, wr_lo], axis=0)
    for half in range(2):
        rows = slice(half * tm // 2, (half + 1) * tm // 2)
        logit_ref[rows, :] = jnp.dot(lhs[rows, :], rhs, preferred_element_type=F32) + br_ref[...]


def _mix(x2, a, u, umeta, gates, w_attn, pool_w, pool_scale, w_pool, w_out, g2, wr, br, seq, tm):
    n = x2.shape[0]
    tiles_per_seq = seq // tm
    halo_blocks = tm // N_META
    const2 = lambda i: (0, 0)
    return pl.pallas_call(
        functools.partial(_mix_kernel, tm=tm, tiles_per_seq=tiles_per_seq),
        grid=(n // tm,),
        in_specs=[
            pl.BlockSpec((tm, D_MODEL), lambda i: (i, 0)),
            pl.BlockSpec((tm, D_MODEL), lambda i: (i, 0)),
            pl.BlockSpec((tm, POOL_WIDTH), lambda i: (i, 0)),
            pl.BlockSpec((N_META, POOL_WIDTH), lambda i: (jnp.maximum(i * halo_blocks - 1, 0), 0)),
            pl.BlockSpec((N_META, POOL_WIDTH), const2),
            pl.BlockSpec((tm, 2 * D_MODEL), lambda i: (i, 0)),
            pl.BlockSpec((D_MODEL, D_MODEL), const2),
            pl.BlockSpec((N_POOL_GROUPS, POOL_GROUP_DIM, POOL_GROUP_DIM), lambda i: (0, 0, 0)),
            pl.BlockSpec((1, POOL_WIDTH), const2),
            pl.BlockSpec((POOL_WIDTH, D_MODEL), const2),
            pl.BlockSpec((D_MODEL, D_MODEL), const2),
            pl.BlockSpec((1, D_MODEL), const2),
            pl.BlockSpec((D_MODEL, LANES), const2),
            pl.BlockSpec((1, LANES), const2),
        ],
        out_specs=[
            pl.BlockSpec((tm * TOKEN_ROWS, LANES), lambda i: (i, 0)),
            pl.BlockSpec((tm, LANES), lambda i: (i, 0)),
        ],
        out_shape=[
            jax.ShapeDtypeStruct((n * TOKEN_ROWS, LANES), F32),
            jax.ShapeDtypeStruct((n, LANES), F32),
        ],
        scratch_shapes=[pltpu.VMEM((tm + 2 * N_META, POOL_WIDTH), F32)] * 3,
        compiler_params=_params("arbitrary"),
        name="mix",
    )(x2, a, u, u, umeta, gates, w_attn, pool_w, pool_scale, w_pool, w_out, g2, wr, br)


_TBL_EA, _TBL_EB, _TBL_NEW_A, _TBL_NEW_B, _TBL_NVALID, _TBL_NUSED = range(6)


def _route_kernel(logit_ref, tri_ref, wts_ref, dest_ref, tbl_ref, carry_sc, ids_sc, *, tme):
    i = pl.program_id(1)
    pl.when(pl.program_id(0) == 0)(functools.partial(_route_classify, i, logit_ref, tri_ref, wts_ref,
                                                     carry_sc, ids_sc))
    pl.when(pl.program_id(0) == 1)(functools.partial(_route_place, i, dest_ref, tbl_ref, carry_sc,
                                                     ids_sc, tme))


def _route_classify(i, logit_ref, tri_ref, wts_ref, carry_sc, ids_sc):
    @pl.when(i == 0)
    def _():
        carry_sc[...] = jnp.zeros_like(carry_sc)

    lg = logit_ref[...].T
    g = [lg[r:r + 1, :] for r in range(N_GROUPS)]
    gmax = functools.reduce(jnp.maximum, g)
    gsel = jnp.full_like(g[0], N_GROUPS - 1).astype(jnp.int32)
    for r in range(N_GROUPS - 2, -1, -1):
        gsel = jnp.where(g[r] == gmax, r, gsel)
    p_group = 1.0 / functools.reduce(lambda a, b: a + b, [jnp.exp(x - gmax) for x in g])
    e = []
    for jj in range(EXPERTS_PER_GROUP):
        v = lg[N_GROUPS + jj:N_GROUPS + jj + 1, :]
        for r in range(1, N_GROUPS):
            row = N_GROUPS + r * EXPERTS_PER_GROUP + jj
            v = jnp.where(gsel == r, lg[row:row + 1, :], v)
        e.append(v)
    v1 = functools.reduce(jnp.maximum, e)
    i1 = jnp.full_like(gsel, EXPERTS_PER_GROUP - 1)
    for jj in range(EXPERTS_PER_GROUP - 2, -1, -1):
        i1 = jnp.where(e[jj] == v1, jj, i1)
    rest = [jnp.where(i1 == jj, -jnp.inf, e[jj]) for jj in range(EXPERTS_PER_GROUP)]
    v2 = functools.reduce(jnp.maximum, rest)
    i2 = jnp.full_like(gsel, EXPERTS_PER_GROUP - 1)
    for jj in range(EXPERTS_PER_GROUP - 2, -1, -1):
        i2 = jnp.where(jnp.logical_and(rest[jj] == v2, i1 != jj), jj, i2)
    t = jnp.exp(v2 - v1)
    w1 = p_group / (1.0 + t)
    w2 = p_group * t / (1.0 + t)
    lo = jnp.minimum(i1, i2)
    hi = jnp.maximum(i1, i2)
    pair = jnp.where(lo == 0, hi - 1, jnp.where(lo == 1, jnp.where(hi == 3, 3, 4), 5))
    bucket = gsel * N_PAIRS + pair
    w_of_lo = jnp.where(i1 < i2, w1, w2)
    w_of_hi = jnp.where(i1 < i2, w2, w1)
    swapped = pair == 5
    w_a = jnp.where(swapped, w_of_hi, w_of_lo)
    w_b = jnp.where(swapped, w_of_lo, w_of_hi)

    tl = lg.shape[1]
    bid = lax.broadcasted_iota(jnp.int32, (N_BUCKET_ROWS, tl), 0)
    hot = bid == bucket
    hot_b = jnp.where(hot, 1.0, 0.0).astype(BF16)
    prefix = jnp.dot(hot_b, tri_ref[...], preferred_element_type=F32)
    carry = carry_sc[...]
    before = prefix + jnp.concatenate([carry] * (tl // LANES), axis=1)
    rank = jnp.sum(jnp.where(hot, before, 0.0), axis=0, keepdims=True)
    carry_sc[...] = carry + jnp.dot(hot_b, jnp.ones((tl, LANES), BF16), preferred_element_type=F32)
    pad = jnp.zeros((6, tl), F32)
    ids_sc[i] = jnp.concatenate([bucket.astype(F32), rank, pad], axis=0)
    wts_ref[...] = jnp.concatenate([w_a, w_b, pad], axis=0)


def _route_place(i, dest_ref, tbl_ref, carry_sc, ids_sc, tme):
    cnt = carry_sc[...]
    tiles = jnp.floor((cnt + (tme - 1)) * (1.0 / tme))
    r = lax.broadcasted_iota(jnp.int32, (N_BUCKET_ROWS, N_BUCKET_ROWS), 0)
    c = lax.broadcasted_iota(jnp.int32, (N_BUCKET_ROWS, N_BUCKET_ROWS), 1)
    tile_end = jnp.dot(jnp.where(c <= r, 1.0, 0.0).astype(BF16), tiles.astype(BF16),
                       preferred_element_type=F32)
    tile_start = tile_end - tiles

    ids = ids_sc[i]
    bucket, rank = ids[0:1, :], ids[1:2, :]
    tl = ids.shape[1]
    bid = lax.broadcasted_iota(jnp.int32, (N_BUCKET_ROWS, tl), 0).astype(F32)
    row_start = jnp.concatenate([tile_start * tme] * (tl // LANES), axis=1)
    dest = rank + jnp.sum(jnp.where(bid == bucket, row_start, 0.0), axis=0, keepdims=True)
    dest_ref[...] = jnp.concatenate([dest, jnp.zeros((7, tl), F32)], axis=0).astype(jnp.int32)

    lane = lax.broadcasted_iota(jnp.int32, (1, LANES), 1).astype(F32)
    n_used = tile_end[N_BUCKET_ROWS - 1:N_BUCKET_ROWS, :]
    last = jnp.minimum(lane, n_used - 1.0)
    tb = jnp.sum(jnp.where(tile_end <= last, 1.0, 0.0), axis=0, keepdims=True)
    tb = jnp.minimum(tb, N_BUCKETS - 1.0)
    group = functools.reduce(lambda a, b: a + b,
                             [jnp.where(tb >= g * N_PAIRS, 1.0, 0.0) for g in range(1, N_GROUPS)])
    pair = tb - N_PAIRS * group

    def lookup(table):
        out = jnp.full_like(pair, float(table[-1]))
        for k in range(len(table) - 2, -1, -1):
            out = jnp.where(pair == k, float(table[k]), out)
        return EXPERTS_PER_GROUP * group + out

    e_a, e_b = lookup(_PAIR_A), lookup(_PAIR_B)
    changed = lambda e: jnp.where(lane == 0.0, 1.0, jnp.where(e != pltpu.roll(e, 1, 1), 1.0, 0.0))
    bid128 = lax.broadcasted_iota(jnp.int32, (N_BUCKET_ROWS, LANES), 0).astype(F32)
    at_tile = lambda col: jnp.sum(jnp.where(bid128 == tb, col, 0.0), axis=0, keepdims=True)
    n_valid = jnp.clip(at_tile(cnt) - tme * (lane - at_tile(tile_start)), 0.0, float(tme))
    rows = [e_a, e_b, changed(e_a), changed(e_b), n_valid, n_used, jnp.zeros_like(lane),
            jnp.zeros_like(lane)]
    tbl_ref[...] = jnp.concatenate(rows, axis=0).astype(jnp.int32)


def _route(logits, tl, tme):
    n = logits.shape[0]
    nt = n // tl
    tri = jnp.asarray(np.triu(np.ones((tl, tl), np.float32), k=1), dtype=BF16)
    first_pass = lambda p, i: i * (1 - p) + (nt - 1) * p
    return pl.pallas_call(
        functools.partial(_route_kernel, tme=tme),
        grid=(2, nt),
        in_specs=[pl.BlockSpec((tl, LANES), lambda p, i: (first_pass(p, i), 0)),
                  pl.BlockSpec((tl, tl), lambda p, i: (0, 0))],
        out_specs=[pl.BlockSpec((8, tl), lambda p, i: (0, first_pass(p, i))),
                   pl.BlockSpec((8, tl), lambda p, i: (0, i * p)),
                   pl.BlockSpec((8, LANES), lambda p, i: (0, 0))],
        out_shape=[jax.ShapeDtypeStruct((8, n), F32),
                   jax.ShapeDtypeStruct((8, n), jnp.int32),
                   jax.ShapeDtypeStruct((8, LANES), jnp.int32)],
        scratch_shapes=[pltpu.VMEM((N_BUCKET_ROWS, LANES), F32), pltpu.VMEM((nt, 8, tl), F32)],
        compiler_params=_params("arbitrary", "arbitrary"),
        name="route",
    )(logits, tri)


def _token_copy(src_hbm, src_token, dst_buf, dst_row, sem):
    return pltpu.make_async_copy(
        src_hbm.at[pl.ds(pl.multiple_of(src_token * TOKEN_ROWS, TOKEN_ROWS), TOKEN_ROWS), :],
        dst_buf.at[pl.ds(pl.multiple_of(dst_row * TOKEN_ROWS, TOKEN_ROWS), TOKEN_ROWS), :],
        sem)


def _start_gather(idx_ref, first, n_groups, src_hbm, dst_buf, sem):
    def body(rr, carry):
        for uu in range(GATHER_UNROLL):
            r = rr * GATHER_UNROLL + uu
            _token_copy(src_hbm, idx_ref[first + r], dst_buf, r, sem).start(priority=uu % 2)
        return carry

    lax.fori_loop(0, n_groups, body, 0)


def _start_padded_gather(idx_ref, first, n_valid, rows, src_hbm, dst_buf, sem):
    n_groups = (n_valid + GATHER_UNROLL - 1) // GATHER_UNROLL
    _start_gather(idx_ref, first, n_groups, src_hbm, dst_buf, sem)
    pad_groups = rows // GATHER_UNROLL - n_groups
    off = n_groups * GATHER_UNROLL
    size = rows // 2
    while size >= GATHER_UNROLL:
        take = jnp.bitwise_and(pad_groups, size // GATHER_UNROLL) != 0

        @pl.when(take)
        def _(off=off, size=size):
            pltpu.make_async_copy(
                src_hbm.at[pl.ds(0, size * TOKEN_ROWS), :],
                dst_buf.at[pl.ds(pl.multiple_of(off * TOKEN_ROWS, GATHER_UNROLL * TOKEN_ROWS),
                                 size * TOKEN_ROWS), :],
                sem).start()

        off = off + jnp.where(take, size, 0)
        size //= 2


def _wait_gather(rows, src_hbm, dst_buf, sem):
    pltpu.make_async_copy(src_hbm.at[pl.ds(0, rows * TOKEN_ROWS), :], dst_buf, sem).wait()


def _moe_kernel(tbl_ref, src_ref, h2c_hbm, wts_ref, g2_ref, wg_a_ref, wu_a_ref, wd_a_ref, wg_b_ref,
                wu_b_ref, wd_b_ref, yc_ref, xbuf, wg_a_sc, wu_a_sc, wd_a_sc, wg_b_sc, wu_b_sc, wd_b_sc,
                sem, *, tme):
    i = pl.program_id(0)
    n_used = tbl_ref[_TBL_NUSED, 0]
    slot = jnp.bitwise_and(i, 1)

    @pl.when(i == 0)
    def _():
        _start_padded_gather(src_ref, 0, tbl_ref[_TBL_NVALID, 0], tme, h2c_hbm, xbuf.at[0], sem.at[0])

    @pl.when(i + 1 < n_used)
    def _():
        _start_padded_gather(src_ref, (i + 1) * tme, tbl_ref[_TBL_NVALID, i + 1], tme, h2c_hbm,
                             xbuf.at[1 - slot], sem.at[1 - slot])

    @pl.when(tbl_ref[_TBL_NEW_A, i] == 1)
    def _():
        for src, dst in ((wg_a_ref, wg_a_sc), (wu_a_ref, wu_a_sc), (wd_a_ref, wd_a_sc)):
            dst[...] = src[0].astype(BF16)

    @pl.when(tbl_ref[_TBL_NEW_B, i] == 1)
    def _():
        for src, dst in ((wg_b_ref, wg_b_sc), (wu_b_ref, wu_b_sc), (wd_b_ref, wd_b_sc)):
            dst[...] = src[0].astype(BF16)

    def tile(rows):
        _wait_gather(tme, h2c_hbm, xbuf.at[slot], sem.at[slot])
        x = _load_token_major(xbuf.at[slot], rows)
        ms = jnp.mean(x * x, axis=-1, keepdims=True)
        hn = (x * lax.rsqrt(ms + EPS) * g2_ref[...]).astype(BF16)
        wts = wts_ref[0:rows, :]

        def expert_act(wg_sc, wu_sc, w):
            gate = jnp.dot(hn, wg_sc[...], preferred_element_type=F32)
            up = jnp.dot(hn, wu_sc[...], preferred_element_type=F32)
            return (gate / (1.0 + jnp.exp(-gate)) * up * w).astype(BF16)

        y = (jnp.dot(expert_act(wg_a_sc, wu_a_sc, wts[:, 0:1]), wd_a_sc[...], preferred_element_type=F32)
             + jnp.dot(expert_act(wg_b_sc, wu_b_sc, wts[:, 1:2]), wd_b_sc[...], preferred_element_type=F32))
        _store_token_major(yc_ref, y, rows)
        if rows < tme:
            yc_ref[rows * TOKEN_ROWS:tme * TOKEN_ROWS, :] = jnp.zeros(((tme - rows) * TOKEN_ROWS, LANES), F32)

    n_valid = tbl_ref[_TBL_NVALID, i]
    pl.when(jnp.logical_and(i < n_used, n_valid > tme // 2))(functools.partial(tile, tme))
    pl.when(jnp.logical_and(i < n_used, n_valid <= tme // 2))(functools.partial(tile, tme // 2))

    @pl.when(i >= n_used)
    def _():
        yc_ref[...] = jnp.zeros_like(yc_ref)


def _moe(tbl, src, h2c, wts_sorted, g2, wg, wu, wd, tme, n_tiles):
    used = lambda i, tbl, src: (jnp.minimum(i, tbl[_TBL_NUSED, 0] - 1), 0)
    w_a = lambda i, tbl, src: (tbl[_TBL_EA, i], 0, 0)
    w_b = lambda i, tbl, src: (tbl[_TBL_EB, i], 0, 0)
    up_shape, down_shape = (D_MODEL, D_EXPERT), (D_EXPERT, D_MODEL)
    grid_spec = pltpu.PrefetchScalarGridSpec(
        num_scalar_prefetch=2,
        grid=(n_tiles,),
        in_specs=[
            pl.BlockSpec(memory_space=pl.ANY),
            pl.BlockSpec((tme, wts_sorted.shape[1]), used),
            pl.BlockSpec((1, D_MODEL), lambda i, tbl, src: (0, 0)),
            pl.BlockSpec((1,) + up_shape, w_a),
            pl.BlockSpec((1,) + up_shape, w_a),
            pl.BlockSpec((1,) + down_shape, w_a),
            pl.BlockSpec((1,) + up_shape, w_b),
            pl.BlockSpec((1,) + up_shape, w_b),
            pl.BlockSpec((1,) + down_shape, w_b),
        ],
        out_specs=pl.BlockSpec((tme * TOKEN_ROWS, LANES), lambda i, tbl, src: (i, 0)),
        scratch_shapes=[pltpu.VMEM((2, tme * TOKEN_ROWS, LANES), F32)]
        + [pltpu.VMEM(s, BF16) for s in (up_shape, up_shape, down_shape) * 2]
        + [pltpu.SemaphoreType.DMA((2,))],
    )
    return pl.pallas_call(
        functools.partial(_moe_kernel, tme=tme),
        grid_spec=grid_spec,
        out_shape=jax.ShapeDtypeStruct((n_tiles * tme * TOKEN_ROWS, LANES), F32),
        compiler_params=_params("arbitrary"),
        name="moe",
    )(tbl, src, h2c, wts_sorted, g2, wg, wu, wd, wg, wu, wd)


def _final_kernel(dest_ref, h2c_ref, yc_hbm, gf_ref, o_ref, ybuf, sem, *, tmf):
    i = pl.program_id(0)
    slot = jnp.bitwise_and(i, 1)

    @pl.when(i == 0)
    def _():
        _start_gather(dest_ref, 0, tmf // GATHER_UNROLL, yc_hbm, ybuf.at[0], sem.at[0])

    @pl.when(i + 1 < pl.num_programs(0))
    def _():
        _start_gather(dest_ref, (i + 1) * tmf, tmf // GATHER_UNROLL, yc_hbm, ybuf.at[1 - slot],
                      sem.at[1 - slot])

    _wait_gather(tmf, yc_hbm, ybuf.at[slot], sem.at[slot])
    h = _load_token_major(h2c_ref, tmf) + _load_token_major(ybuf.at[slot], tmf)
    ms = jnp.mean(h * h, axis=-1, keepdims=True)
    o_ref[...] = h * lax.rsqrt(ms + EPS) * gf_ref[...]


def _final(dest, h2c, yc, gf, tmf):
    n = h2c.shape[0] // TOKEN_ROWS
    grid_spec = pltpu.PrefetchScalarGridSpec(
        num_scalar_prefetch=1,
        grid=(n // tmf,),
        in_specs=[
            pl.BlockSpec((tmf * TOKEN_ROWS, LANES), lambda i, dest: (i, 0)),
            pl.BlockSpec(memory_space=pl.ANY),
            pl.BlockSpec((1, D_MODEL), lambda i, dest: (0, 0)),
        ],
        out_specs=pl.BlockSpec((tmf, D_MODEL), lambda i, dest: (i, 0)),
        scratch_shapes=[pltpu.VMEM((2, tmf * TOKEN_ROWS, LANES), F32), pltpu.SemaphoreType.DMA((2,))],
    )
    return pl.pallas_call(
        functools.partial(_final_kernel, tmf=tmf),
        grid_spec=grid_spec,
        out_shape=jax.ShapeDtypeStruct((n, D_MODEL), F32),
        compiler_params=_params("arbitrary"),
        name="final",
    )(dest, h2c, yc, gf)


def _rope_tables(t):
    inv = 1.0 / (ROPE_THETA ** (np.arange(0, HEAD_DIM, 2, dtype=np.float64) / HEAD_DIM))
    ang = np.arange(t, dtype=np.float64)[:, None] * inv[None, :]
    ang = np.concatenate([ang, ang, ang, ang], axis=-1)
    sign = np.where((np.arange(LANES) % HEAD_DIM) < HEAD_DIM // 2, -1.0, 1.0)
    return (jnp.asarray(np.cos(ang), dtype=F32), jnp.asarray(np.sin(ang) * sign[None, :], dtype=F32))


def kernel(x, meta, norm1_g, w_in, b_gate, lambda_q1, lambda_k1, lambda_q2, lambda_k2, subln_g, pool_w,
           pool_scale, w_attn_br, w_pool_br, w_out, norm2_g, w_router_group, b_router_group,
           w_router_expert, b_router_expert, w_e_gate, w_e_up, w_e_down, final_g):
    batch, seq, d = x.shape
    n = batch * seq
    x2 = x.reshape(n, d)
    cos, sin = _rope_tables(seq + N_META)
    q, k, v, u, gates, qm, km, vm, um = _inproj(x2, meta, norm1_g, cos, sin, w_in[0], b_gate, tm=1024)
    a = _attention(q, k, v, qm, km, vm, lambda_q1, lambda_k1, lambda_q2, lambda_k2, subln_g, batch, seq,
                   bq=256)

    n_router = N_GROUPS + N_EXPERTS
    wr = jnp.pad(jnp.concatenate([w_router_group[0], w_router_expert[0]], axis=1),
                 ((0, 0), (0, LANES - n_router)))
    br = jnp.pad(jnp.concatenate([b_router_group[0], b_router_expert[0]]), (0, LANES - n_router))
    h2c, logits = _mix(x2, a, u, um, gates, w_attn_br[0].astype(BF16), pool_w[0].astype(BF16),
                       pool_scale, w_pool_br[0].astype(BF16), w_out[0].astype(BF16), norm2_g,
                       wr, br.reshape(1, LANES), seq, tm=512)
    tme = 256
    n_tiles = n // tme + N_BUCKETS
    assert n_tiles <= LANES
    wts, dest, tile_table = _route(logits, tl=1024, tme=tme)
    dest = dest[0]
    per_token = jnp.concatenate([wts[0:2], jnp.arange(n, dtype=F32)[None, :]], axis=0).T
    per_row = jnp.zeros((n_tiles * tme, 3), F32).at[dest].set(per_token)
    src = per_row[:, 2].astype(jnp.int32)

    yc = _moe(tile_table, src, h2c, per_row, norm2_g, w_e_gate[0], w_e_up[0], w_e_down[0], tme, n_tiles)
    out = _final(dest, h2c, yc, final_g.reshape(1, d), tmf=512)
    return out.reshape(batch, seq, d)
```

```python
import functools
import math

import jax
import jax.numpy as jnp
import numpy as np
from jax import lax
from jax.experimental import pallas as pl
from jax.experimental.pallas import tpu as pltpu

D_MODEL = 1024
N_META = 16
N_HEADS = 8
HEAD_DIM = 64
V_DIM = 2 * HEAD_DIM
POOL_WINDOWS = (2, 4, 8, 16)
N_POOL_GROUPS = len(POOL_WINDOWS)
POOL_GROUP_DIM = 128
POOL_WIDTH = N_POOL_GROUPS * POOL_GROUP_DIM
ROPE_THETA = 10000.0
N_GROUPS = 4
EXPERTS_PER_GROUP = 4
N_EXPERTS = N_GROUPS * EXPERTS_PER_GROUP
D_EXPERT = 512
EPS = 1e-6
LAMBDA_INIT = 0.8 - 0.6 * math.exp(-0.3 * 0)
LOG2_E = math.log2(math.e)

_PAIR_A = (0, 0, 0, 1, 1, 3)
_PAIR_B = (1, 2, 3, 3, 2, 2)
N_PAIRS = len(_PAIR_A)
N_BUCKETS = N_GROUPS * N_PAIRS
N_BUCKET_ROWS = 32
GATHER_UNROLL = 8

LANES = 128
NEG_BIG = -1e30
VMEM_LIMIT = 48 * 1024 * 1024

F32 = jnp.float32
BF16 = jnp.bfloat16

_HEADS_PER_STEP = 2
_IN_STEPS = N_HEADS // _HEADS_PER_STEP
_QKV_BLK = _HEADS_PER_STEP * V_DIM
_GATE_BLK = 2 * D_MODEL // _IN_STEPS
_O_K = D_MODEL // _QKV_BLK
_O_V = 2 * D_MODEL // _QKV_BLK
_O_U = 3 * D_MODEL // POOL_GROUP_DIM
_O_G = (3 * D_MODEL + POOL_WIDTH) // _GATE_BLK


def _params(*sem):
    return pltpu.CompilerParams(dimension_semantics=sem, vmem_limit_bytes=VMEM_LIMIT)


TOKEN_ROWS = D_MODEL // LANES


def _store_token_major(ref, val, rows):
    for s in range(TOKEN_ROWS):
        ref[pl.ds(s, rows, stride=TOKEN_ROWS), :] = val[:, s * LANES:(s + 1) * LANES]


def _load_token_major(ref, rows):
    return jnp.concatenate([ref[pl.ds(s, rows, stride=TOKEN_ROWS), :] for s in range(TOKEN_ROWS)], axis=1)


def _rope(z, cos, sin_signed, first_half):
    outs = []
    for c in range(z.shape[1] // LANES):
        zc = z[:, c * LANES:(c + 1) * LANES]
        rot = jnp.where(first_half, pltpu.roll(zc, LANES - HEAD_DIM // 2, 1),
                        pltpu.roll(zc, HEAD_DIM // 2, 1))
        outs.append(zc * cos + rot * sin_signed)
    return jnp.concatenate(outs, axis=1)


def _inproj_kernel(x_ref, meta_ref, g_ref, cos_ref, sin_ref, cosm_ref, sinm_ref, wq_ref, wk_ref, wv_ref,
                   wu_ref, wg_ref, bg_ref, q_ref, k_ref, v_ref, u_ref, gate_ref, qm_ref, km_ref, vm_ref,
                   um_ref, wq_sc, wk_sc, wv_sc, wu_sc, wg_sc):
    def normed(x):
        ms = jnp.mean(x * x, axis=-1, keepdims=True)
        return (x * lax.rsqrt(ms + EPS) * g_ref[...]).astype(BF16)

    def project(hn, cos, sin, q_out, k_out, v_out, u_out, gate_out):
        lane = lax.broadcasted_iota(jnp.int32, cos.shape, 1)
        first_half = jnp.bitwise_and(lane, HEAD_DIM - 1) < (HEAD_DIM // 2)
        if gate_out is not None:
            zg = jnp.dot(hn, wg_sc[...], preferred_element_type=F32) + bg_ref[...]
            gate_out[...] = (1.0 / (1.0 + jnp.exp(-zg))).astype(BF16)
        zq = jnp.dot(hn, wq_sc[...], preferred_element_type=F32)
        q_out[...] = (_rope(zq, cos, sin, first_half) * (LOG2_E / math.sqrt(HEAD_DIM))).astype(BF16)
        zk = jnp.dot(hn, wk_sc[...], preferred_element_type=F32)
        k_out[...] = _rope(zk, cos, sin, first_half).astype(BF16)
        u_out[...] = jnp.dot(hn, wu_sc[...], preferred_element_type=F32)
        v_out[...] = jnp.dot(hn, wv_sc[...], preferred_element_type=F32).astype(BF16)

    @pl.when(pl.program_id(1) == 0)
    def _():
        for src, dst in ((wq_ref, wq_sc), (wk_ref, wk_sc), (wv_ref, wv_sc), (wu_ref, wu_sc),
                         (wg_ref, wg_sc)):
            dst[...] = src[...].astype(BF16)
        project(normed(meta_ref[...]), cosm_ref[...], sinm_ref[...], qm_ref, km_ref, vm_ref, um_ref, None)

    project(normed(x_ref[...]), cos_ref[...], sin_ref[...], q_ref, k_ref, v_ref, u_ref, gate_ref)


def _inproj(x2, meta, g1, cos, sin, w_in, b_gate, tm):
    n = x2.shape[0]
    n_meta = meta.shape[0]
    n_pos_blocks = (cos.shape[0] - n_meta) // tm
    row = lambda j, i: (i, 0)
    pos = lambda j, i: (i % n_pos_blocks, 0)
    whole = lambda j, i: (0, 0)
    return pl.pallas_call(
        _inproj_kernel,
        grid=(_IN_STEPS, n // tm),
        in_specs=[
            pl.BlockSpec((tm, D_MODEL), row),
            pl.BlockSpec((n_meta, D_MODEL), whole),
            pl.BlockSpec((1, D_MODEL), whole),
            pl.BlockSpec((tm, LANES), pos),
            pl.BlockSpec((tm, LANES), pos),
            pl.BlockSpec((n_meta, LANES), whole),
            pl.BlockSpec((n_meta, LANES), whole),
            pl.BlockSpec((D_MODEL, _QKV_BLK), lambda j, i: (0, j)),
            pl.BlockSpec((D_MODEL, _QKV_BLK), lambda j, i: (0, _O_K + j)),
            pl.BlockSpec((D_MODEL, _QKV_BLK), lambda j, i: (0, _O_V + j)),
            pl.BlockSpec((D_MODEL, POOL_GROUP_DIM), lambda j, i: (0, _O_U + j)),
            pl.BlockSpec((D_MODEL, _GATE_BLK), lambda j, i: (0, _O_G + j)),
            pl.BlockSpec((1, _GATE_BLK), lambda j, i: (0, j)),
        ],
        out_specs=[
            pl.BlockSpec((tm, _QKV_BLK), lambda j, i: (i, j)),
            pl.BlockSpec((tm, _QKV_BLK), lambda j, i: (i, j)),
            pl.BlockSpec((tm, _QKV_BLK), lambda j, i: (i, j)),
            pl.BlockSpec((tm, POOL_GROUP_DIM), lambda j, i: (i, j)),
            pl.BlockSpec((tm, _GATE_BLK), lambda j, i: (i, j)),
            pl.BlockSpec((n_meta, _QKV_BLK), lambda j, i: (0, j)),
            pl.BlockSpec((n_meta, _QKV_BLK), lambda j, i: (0, j)),
            pl.BlockSpec((n_meta, _QKV_BLK), lambda j, i: (0, j)),
            pl.BlockSpec((n_meta, POOL_GROUP_DIM), lambda j, i: (0, j)),
        ],
        out_shape=[
            jax.ShapeDtypeStruct((n, D_MODEL), BF16),
            jax.ShapeDtypeStruct((n, D_MODEL), BF16),
            jax.ShapeDtypeStruct((n, D_MODEL), BF16),
            jax.ShapeDtypeStruct((n, POOL_WIDTH), F32),
            jax.ShapeDtypeStruct((n, 2 * D_MODEL), BF16),
            jax.ShapeDtypeStruct((n_meta, D_MODEL), BF16),
            jax.ShapeDtypeStruct((n_meta, D_MODEL), BF16),
            jax.ShapeDtypeStruct((n_meta, D_MODEL), BF16),
            jax.ShapeDtypeStruct((n_meta, POOL_WIDTH), F32),
        ],
        scratch_shapes=[
            pltpu.VMEM((D_MODEL, _QKV_BLK), BF16),
            pltpu.VMEM((D_MODEL, _QKV_BLK), BF16),
            pltpu.VMEM((D_MODEL, _QKV_BLK), BF16),
            pltpu.VMEM((D_MODEL, POOL_GROUP_DIM), BF16),
            pltpu.VMEM((D_MODEL, _GATE_BLK), BF16),
        ],
        compiler_params=_params("arbitrary", "arbitrary"),
        name="inproj",
    )(x2, meta, g1, cos[n_meta:], sin[n_meta:], cos[:n_meta], sin[:n_meta], w_in, w_in, w_in, w_in, w_in,
      b_gate)


def _attn_kernel(lq1_ref, lk1_ref, lq2_ref, lk2_ref, subg_ref, q_ref, k_ref, v_ref, qm_ref, km_ref,
                 vm_ref, o_ref, *state_sc, bq):
    seq = k_ref.shape[0]
    nq = seq // bq
    states = [tuple(state_sc[3 * g:3 * g + 3]) for g in range(nq + 1)]

    def rows_at(ref, meta_ref, p0, n):
        if p0 == 0:
            return jnp.concatenate([meta_ref[...], ref[0:n - N_META, :]], axis=0)
        real = ref[p0 - N_META:min(p0 - N_META + n, seq), :]
        if real.shape[0] < n:
            real = jnp.concatenate([real, jnp.zeros((n - real.shape[0], real.shape[1]), real.dtype)], axis=0)
        return real

    def block_rows(i):
        return (i * bq, bq, bq) if i < nq else (nq * bq, N_META, LANES)

    lam = (jnp.exp(jnp.sum(lq1_ref[...] * lk1_ref[...], axis=-1, keepdims=True))
           - jnp.exp(jnp.sum(lq2_ref[...] * lk2_ref[...], axis=-1, keepdims=True)) + LAMBDA_INIT)

    def causal_bias(n_rows, n_cols):
        r = lax.broadcasted_iota(jnp.int32, (n_rows, n_cols), 0)
        c = lax.broadcasted_iota(jnp.int32, (n_rows, n_cols), 1)
        bias = jnp.where(c <= r, 0.0, NEG_BIG)
        return jnp.concatenate([bias, bias], axis=0)

    biases = {bq: causal_bias(bq, bq), N_META: causal_bias(N_META, LANES)}

    def load_q(i, qq_sc):
        p0, n, _ = block_rows(i)
        q = rows_at(q_ref, qm_ref, p0, n)
        lane = lax.broadcasted_iota(jnp.int32, q.shape, 1)
        zero = jnp.zeros_like(q)
        qq_sc[0:n, :] = jnp.where(lane < HEAD_DIM, q, zero)
        qq_sc[n:2 * n, :] = jnp.where(lane >= HEAD_DIM, q, zero)

    def step(state, kb, vb, bias):
        qq_sc, m_sc, acc_sc = state
        first = bias is not None
        s = lax.dot_general(qq_sc[...], kb, (((1,), (1,)), ((), ())), preferred_element_type=F32)
        if first:
            s = s + bias
        n_tiles = s.shape[1] // LANES
        rm = s[:, 0:LANES]
        for t in range(1, n_tiles):
            rm = jnp.maximum(rm, s[:, t * LANES:(t + 1) * LANES])
        rmax = jnp.max(rm, axis=-1, keepdims=True)
        if first:
            m_new = jnp.broadcast_to(rmax, (s.shape[0], LANES))
        else:
            m_prev = m_sc[...]
            m_new = jnp.maximum(m_prev, rmax)
        p = jnp.exp2(s - jnp.concatenate([m_new] * n_tiles, axis=1))
        pv = jnp.dot(p.astype(BF16), vb, preferred_element_type=F32)
        if first:
            acc_sc[...] = pv
        else:
            alpha = jnp.exp2(m_prev - m_new)
            acc_sc[...] = jnp.concatenate([alpha, alpha], axis=1) * acc_sc[...] + pv
        m_sc[...] = m_new

    def key_step(state, i, j):
        p0, n, width = block_rows(i) if j == 0 else ((j - 1) * bq, None, bq)
        kb = rows_at(k_ref, km_ref, p0, width)
        vb = jnp.concatenate([rows_at(v_ref, vm_ref, p0, width), jnp.ones((width, V_DIM), BF16)], axis=1)
        step(state, kb, vb, biases[n] if j == 0 else None)

    def finish(i, acc_sc):
        p0, n, _ = block_rows(i)
        o1 = acc_sc[0:n, 0:V_DIM] / acc_sc[0:n, V_DIM:2 * V_DIM]
        o2 = acc_sc[n:2 * n, 0:V_DIM] / acc_sc[n:2 * n, V_DIM:2 * V_DIM]
        o = o1 - lam * o2
        ms = jnp.mean(o * o, axis=-1, keepdims=True)
        o = (o * lax.rsqrt(ms + EPS) * subg_ref[...] * (1.0 - LAMBDA_INIT)).astype(BF16)
        if p0 == 0:
            o_ref[0:n - N_META, :] = o[N_META:n, :]
        else:
            o_ref[p0 - N_META:p0 - N_META + n, :] = o

    order = []
    for p in range((nq + 2) // 2):
        order += [p] if p == nq - p else [p, nq - p]
    for i in order:
        load_q(i, states[i][0])
    for j in range(nq + 1):
        for i in order:
            if j <= i:
                key_step(states[i], i, j)
    for i in order:
        finish(i, states[i][2])


def _attention(q, k, v, qm, km, vm, lq1, lk1, lq2, lk2, subg, batch, seq, bq):
    nq = seq // bq
    small = lambda b, h: (0, 0)
    head = lambda b, h: (b, h)
    meta = lambda b, h: (0, h)

    def state_scratch(rows):
        return [pltpu.VMEM((2 * rows, V_DIM), BF16), pltpu.VMEM((2 * rows, LANES), F32),
                pltpu.VMEM((2 * rows, 2 * V_DIM), F32)]

    return pl.pallas_call(
        functools.partial(_attn_kernel, bq=bq),
        grid=(batch, N_HEADS),
        in_specs=[
            pl.BlockSpec((1, HEAD_DIM), small),
            pl.BlockSpec((1, HEAD_DIM), small),
            pl.BlockSpec((1, HEAD_DIM), small),
            pl.BlockSpec((1, HEAD_DIM), small),
            pl.BlockSpec((1, V_DIM), small),
            pl.BlockSpec((seq, V_DIM), head),
            pl.BlockSpec((seq, V_DIM), head),
            pl.BlockSpec((seq, V_DIM), head),
            pl.BlockSpec((N_META, V_DIM), meta),
            pl.BlockSpec((N_META, V_DIM), meta),
            pl.BlockSpec((N_META, V_DIM), meta),
        ],
        out_specs=pl.BlockSpec((seq, V_DIM), head),
        out_shape=jax.ShapeDtypeStruct(q.shape, BF16),
        scratch_shapes=state_scratch(bq) * nq + state_scratch(N_META),
        compiler_params=_params("arbitrary", "arbitrary"),
        name="diff_attn",
    )(lq1, lk1, lq2, lk2, subg, q, k, v, qm, km, vm)


def _mix_kernel(x_ref, a_ref, u_ref, uprev_ref, umeta_ref, gate_ref, wa_ref, pw_ref, ps_ref, wp_ref,
                wo_ref, g2_ref, wr_ref, br_ref, h2c_ref, logit_ref, ext_sc, sum_a_sc, sum_b_sc, *, tm,
                tiles_per_seq):
    i = pl.program_id(0)
    first = (i % tiles_per_seq) == 0
    y_attn = jnp.dot(a_ref[...], wa_ref[...], preferred_element_type=F32)

    top = 2 * N_META
    for ref in (ext_sc, sum_a_sc, sum_b_sc):
        ref[0:N_META, :] = jnp.zeros((N_META, POOL_WIDTH), F32)

    @pl.when(first)
    def _():
        ext_sc[N_META:top, :] = umeta_ref[...]

    @pl.when(jnp.logical_not(first))
    def _():
        ext_sc[N_META:top, :] = uprev_ref[...]

    ext_sc[top:top + tm, :] = u_ref[...]

    pooled = []
    for g, w in enumerate(POOL_WINDOWS):
        cs = slice(g * POOL_GROUP_DIM, (g + 1) * POOL_GROUP_DIM)
        src, shift = ext_sc, 1
        for dst in (sum_a_sc, sum_b_sc, sum_a_sc, sum_b_sc):
            if shift >= w:
                break
            dst[N_META:top + tm, cs] = src[N_META:top + tm, cs] + src[N_META - shift:top + tm - shift, cs]
            src, shift = dst, 2 * shift
        y = (src[top:top + tm, cs] * (1.0 / w) - ext_sc[top:top + tm, cs]).astype(BF16)
        pooled.append(jnp.dot(y, pw_ref[g], preferred_element_type=F32))
    yp = (jnp.concatenate(pooled, axis=1) * ps_ref[...]).astype(BF16)
    y_pool = jnp.dot(yp, wp_ref[...], preferred_element_type=F32)
    mixed = (gate_ref[:, 0:D_MODEL].astype(F32) * y_attn
             + gate_ref[:, D_MODEL:2 * D_MODEL].astype(F32) * y_pool)
    h2 = x_ref[...] + jnp.dot(mixed.astype(BF16), wo_ref[...], preferred_element_type=F32)
    _store_token_major(h2c_ref, h2, tm)
    ms = jnp.mean(h2 * h2, axis=-1, keepdims=True)
    hn2 = h2 * lax.rsqrt(ms + EPS) * g2_ref[...]
    hn2_hi = hn2.astype(BF16)
    hn2_lo = (hn2 - hn2_hi.astype(F32)).astype(BF16)
    wr = wr_ref[...]
    wr_hi = wr.astype(BF16)
    wr_lo = (wr - wr_hi.astype(F32)).astype(BF16)
    lhs = jnp.concatenate([hn2_hi, hn2_lo, hn2_hi], axis=1)
    rhs = jnp.concatenate([wr_hi, wr_hi, wr_lo], axis=0)
    for half in range(2):
        rows = slice(half * tm // 2, (half + 1) * tm // 2)
        logit_ref[rows, :] = jnp.dot(lhs[rows, :], rhs, preferred_element_type=F32) + br_ref[...]


def _mix(x2, a, u, umeta, gates, w_attn, pool_w, pool_scale, w_pool, w_out, g2, wr, br, seq, tm):
    n = x2.shape[0]
    tiles_per_seq = seq // tm
    halo_blocks = tm // N_META
    const2 = lambda i: (0, 0)
    return pl.pallas_call(
        functools.partial(_mix_kernel, tm=tm, tiles_per_seq=tiles_per_seq),
        grid=(n // tm,),
        in_specs=[
            pl.BlockSpec((tm, D_MODEL), lambda i: (i, 0)),
            pl.BlockSpec((tm, D_MODEL), lambda i: (i, 0)),
            pl.BlockSpec((tm, POOL_WIDTH), lambda i: (i, 0)),
            pl.BlockSpec((N_META, POOL_WIDTH), lambda i: (jnp.maximum(i * halo_blocks - 1, 0), 0)),
            pl.BlockSpec((N_META, POOL_WIDTH), const2),
            pl.BlockSpec((tm, 2 * D_MODEL), lambda i: (i, 0)),
            pl.BlockSpec((D_MODEL, D_MODEL), const2),
            pl.BlockSpec((N_POOL_GROUPS, POOL_GROUP_DIM, POOL_GROUP_DIM), lambda i: (0, 0, 0)),
            pl.BlockSpec((1, POOL_WIDTH), const2),
            pl.BlockSpec((POOL_WIDTH, D_MODEL), const2),
            pl.BlockSpec((D_MODEL, D_MODEL), const2),
            pl.BlockSpec((1, D_MODEL), const2),
            pl.BlockSpec((D_MODEL, LANES), const2),
            pl.BlockSpec((1, LANES), const2),
        ],
        out_specs=[
            pl.BlockSpec((tm * TOKEN_ROWS, LANES), lambda i: (i, 0)),
            pl.BlockSpec((tm, LANES), lambda i: (i, 0)),
        ],
        out_shape=[
            jax.ShapeDtypeStruct((n * TOKEN_ROWS, LANES), F32),
            jax.ShapeDtypeStruct((n, LANES), F32),
        ],
        scratch_shapes=[pltpu.VMEM((tm + 2 * N_META, POOL_WIDTH), F32)] * 3,
        compiler_params=_params("arbitrary"),
        name="mix",
    )(x2, a, u, u, umeta, gates, w_attn, pool_w, pool_scale, w_pool, w_out, g2, wr, br)


_TBL_EA, _TBL_EB, _TBL_NEW_A, _TBL_NEW_B, _TBL_NVALID, _TBL_NUSED = range(6)


def _route_kernel(logit_ref, tri_ref, wts_ref, dest_ref, tbl_ref, carry_sc, ids_sc, *, tme):
    i = pl.program_id(1)
    pl.when(pl.program_id(0) == 0)(functools.partial(_route_classify, i, logit_ref, tri_ref, wts_ref,
                                                     carry_sc, ids_sc))
    pl.when(pl.program_id(0) == 1)(functools.partial(_route_place, i, dest_ref, tbl_ref, carry_sc,
                                                     ids_sc, tme))


def _route_classify(i, logit_ref, tri_ref, wts_ref, carry_sc, ids_sc):
    @pl.when(i == 0)
    def _():
        carry_sc[...] = jnp.zeros_like(carry_sc)

    lg = logit_ref[...].T
    g = [lg[r:r + 1, :] for r in range(N_GROUPS)]
    gmax = functools.reduce(jnp.maximum, g)
    gsel = jnp.full_like(g[0], N_GROUPS - 1).astype(jnp.int32)
    for r in range(N_GROUPS - 2, -1, -1):
        gsel = jnp.where(g[r] == gmax, r, gsel)
    p_group = 1.0 / functools.reduce(lambda a, b: a + b, [jnp.exp(x - gmax) for x in g])
    e = []
    for jj in range(EXPERTS_PER_GROUP):
        v = lg[N_GROUPS + jj:N_GROUPS + jj + 1, :]
        for r in range(1, N_GROUPS):
            row = N_GROUPS + r * EXPERTS_PER_GROUP + jj
            v = jnp.where(gsel == r, lg[row:row + 1, :], v)
        e.append(v)
    v1 = functools.reduce(jnp.maximum, e)
    i1 = jnp.full_like(gsel, EXPERTS_PER_GROUP - 1)
    for jj in range(EXPERTS_PER_GROUP - 2, -1, -1):
        i1 = jnp.where(e[jj] == v1, jj, i1)
    rest = [jnp.where(i1 == jj, -jnp.inf, e[jj]) for jj in range(EXPERTS_PER_GROUP)]
    v2 = functools.reduce(jnp.maximum, rest)
    i2 = jnp.full_like(gsel, EXPERTS_PER_GROUP - 1)
    for jj in range(EXPERTS_PER_GROUP - 2, -1, -1):
        i2 = jnp.where(jnp.logical_and(rest[jj] == v2, i1 != jj), jj, i2)
    t = jnp.exp(v2 - v1)
    w1 = p_group / (1.0 + t)
    w2 = p_group * t / (1.0 + t)
    lo = jnp.minimum(i1, i2)
    hi = jnp.maximum(i1, i2)
    pair = jnp.where(lo == 0, hi - 1, jnp.where(lo == 1, jnp.where(hi == 3, 3, 4), 5))
    bucket = gsel * N_PAIRS + pair
    w_of_lo = jnp.where(i1 < i2, w1, w2)
    w_of_hi = jnp.where(i1 < i2, w2, w1)
    swapped = pair == 5
    w_a = jnp.where(swapped, w_of_hi, w_of_lo)
    w_b = jnp.where(swapped, w_of_lo, w_of_hi)

    tl = lg.shape[1]
    bid = lax.broadcasted_iota(jnp.int32, (N_BUCKET_ROWS, tl), 0)
    hot = bid == bucket
    hot_b = jnp.where(hot, 1.0, 0.0).astype(BF16)
    prefix = jnp.dot(hot_b, tri_ref[...], preferred_element_type=F32)
    carry = carry_sc[...]
    before = prefix + jnp.concatenate([carry] * (tl // LANES), axis=1)
    rank = jnp.sum(jnp.where(hot, before, 0.0), axis=0, keepdims=True)
    carry_sc[...] = carry + jnp.dot(hot_b, jnp.ones((tl, LANES), BF16), preferred_element_type=F32)
    pad = jnp.zeros((6, tl), F32)
    ids_sc[i] = jnp.concatenate([bucket.astype(F32), rank, pad], axis=0)
    wts_ref[...] = jnp.concatenate([w_a, w_b, pad], axis=0)


def _route_place(i, dest_ref, tbl_ref, carry_sc, ids_sc, tme):
    cnt = carry_sc[...]
    tiles = jnp.floor((cnt + (tme - 1)) * (1.0 / tme))
    r = lax.broadcasted_iota(jnp.int32, (N_BUCKET_ROWS, N_BUCKET_ROWS), 0)
    c = lax.broadcasted_iota(jnp.int32, (N_BUCKET_ROWS, N_BUCKET_ROWS), 1)
    tile_end = jnp.dot(jnp.where(c <= r, 1.0, 0.0).astype(BF16), tiles.astype(BF16),
                       preferred_element_type=F32)
    tile_start = tile_end - tiles

    ids = ids_sc[i]
    bucket, rank = ids[0:1, :], ids[1:2, :]
    tl = ids.shape[1]
    bid = lax.broadcasted_iota(jnp.int32, (N_BUCKET_ROWS, tl), 0).astype(F32)
    row_start = jnp.concatenate([tile_start * tme] * (tl // LANES), axis=1)
    dest = rank + jnp.sum(jnp.where(bid == bucket, row_start, 0.0), axis=0, keepdims=True)
    dest_ref[...] = jnp.concatenate([dest, jnp.zeros((7, tl), F32)], axis=0).astype(jnp.int32)

    lane = lax.broadcasted_iota(jnp.int32, (1, LANES), 1).astype(F32)
    n_used = tile_end[N_BUCKET_ROWS - 1:N_BUCKET_ROWS, :]
    last = jnp.minimum(lane, n_used - 1.0)
    tb = jnp.sum(jnp.where(tile_end <= last, 1.0, 0.0), axis=0, keepdims=True)
    tb = jnp.minimum(tb, N_BUCKETS - 1.0)
    group = functools.reduce(lambda a, b: a + b,
                             [jnp.where(tb >= g * N_PAIRS, 1.0, 0.0) for g in range(1, N_GROUPS)])
    pair = tb - N_PAIRS * group

    def lookup(table):
        out = jnp.full_like(pair, float(table[-1]))
        for k in range(len(table) - 2, -1, -1):
            out = jnp.where(pair == k, float(table[k]), out)
        return EXPERTS_PER_GROUP * group + out

    e_a, e_b = lookup(_PAIR_A), lookup(_PAIR_B)
    changed = lambda e: jnp.where(lane == 0.0, 1.0, jnp.where(e != pltpu.roll(e, 1, 1), 1.0, 0.0))
    bid128 = lax.broadcasted_iota(jnp.int32, (N_BUCKET_ROWS, LANES), 0).astype(F32)
    at_tile = lambda col: jnp.sum(jnp.where(bid128 == tb, col, 0.0), axis=0, keepdims=True)
    n_valid = jnp.clip(at_tile(cnt) - tme * (lane - at_tile(tile_start)), 0.0, float(tme))
    rows = [e_a, e_b, changed(e_a), changed(e_b), n_valid, n_used, jnp.zeros_like(lane),
            jnp.zeros_like(lane)]
    tbl_ref[...] = jnp.concatenate(rows, axis=0).astype(jnp.int32)


def _route(logits, tl, tme):
    n = logits.shape[0]
    nt = n // tl
    tri = jnp.asarray(np.triu(np.ones((tl, tl), np.float32), k=1), dtype=BF16)
    first_pass = lambda p, i: i * (1 - p) + (nt - 1) * p
    return pl.pallas_call(
        functools.partial(_route_kernel, tme=tme),
        grid=(2, nt),
        in_specs=[pl.BlockSpec((tl, LANES), lambda p, i: (first_pass(p, i), 0)),
                  pl.BlockSpec((tl, tl), lambda p, i: (0, 0))],
        out_specs=[pl.BlockSpec((8, tl), lambda p, i: (0, first_pass(p, i))),
                   pl.BlockSpec((8, tl), lambda p, i: (0, i * p)),
                   pl.BlockSpec((8, LANES), lambda p, i: (0, 0))],
        out_shape=[jax.ShapeDtypeStruct((8, n), F32),
                   jax.ShapeDtypeStruct((8, n), jnp.int32),
                   jax.ShapeDtypeStruct((8, LANES), jnp.int32)],
        scratch_shapes=[pltpu.VMEM((N_BUCKET_ROWS, LANES), F32), pltpu.VMEM((nt, 8, tl), F32)],
        compiler_params=_params("arbitrary", "arbitrary"),
        name="route",
    )(logits, tri)


def _token_copy(src_hbm, src_token, dst_buf, dst_row, sem):
    return pltpu.make_async_copy(
        src_hbm.at[pl.ds(pl.multiple_of(src_token * TOKEN_ROWS, TOKEN_ROWS), TOKEN_ROWS), :],
        dst_buf.at[pl.ds(pl.multiple_of(dst_row * TOKEN_ROWS, TOKEN_ROWS), TOKEN_ROWS), :],
        sem)


def _start_gather(idx_ref, first, n_groups, src_hbm, dst_buf, sem):
    def body(rr, carry):
        for uu in range(GATHER_UNROLL):
            r = rr * GATHER_UNROLL + uu
            _token_copy(src_hbm, idx_ref[first + r], dst_buf, r, sem).start(priority=uu % 2)
        return carry

    lax.fori_loop(0, n_groups, body, 0)


def _start_padded_gather(idx_ref, first, n_valid, rows, src_hbm, dst_buf, sem):
    n_groups = (n_valid + GATHER_UNROLL - 1) // GATHER_UNROLL
    _start_gather(idx_ref, first, n_groups, src_hbm, dst_buf, sem)
    pad_groups = rows // GATHER_UNROLL - n_groups
    off = n_groups * GATHER_UNROLL
    size = rows // 2
    while size >= GATHER_UNROLL:
        take = jnp.bitwise_and(pad_groups, size // GATHER_UNROLL) != 0

        @pl.when(take)
        def _(off=off, size=size):
            pltpu.make_async_copy(
                src_hbm.at[pl.ds(0, size * TOKEN_ROWS), :],
                dst_buf.at[pl.ds(pl.multiple_of(off * TOKEN_ROWS, GATHER_UNROLL * TOKEN_ROWS),
                                 size * TOKEN_ROWS), :],
                sem).start()

        off = off + jnp.where(take, size, 0)
        size //= 2


def _wait_gather(rows, src_hbm, dst_buf, sem):
    pltpu.make_async_copy(src_hbm.at[pl.ds(0, rows * TOKEN_ROWS), :], dst_buf, sem).wait()


def _moe_kernel(tbl_ref, src_ref, h2c_hbm, wts_ref, g2_ref, wg_a_ref, wu_a_ref, wd_a_ref, wg_b_ref,
                wu_b_ref, wd_b_ref, yc_ref, xbuf, wg_a_sc, wu_a_sc, wd_a_sc, wg_b_sc, wu_b_sc, wd_b_sc,
                sem, *, tme):
    i = pl.program_id(0)
    n_used = tbl_ref[_TBL_NUSED, 0]
    slot = jnp.bitwise_and(i, 1)

    @pl.when(i == 0)
    def _():
        _start_padded_gather(src_ref, 0, tbl_ref[_TBL_NVALID, 0], tme, h2c_hbm, xbuf.at[0], sem.at[0])

    @pl.when(i + 1 < n_used)
    def _():
        _start_padded_gather(src_ref, (i + 1) * tme, tbl_ref[_TBL_NVALID, i + 1], tme, h2c_hbm,
                             xbuf.at[1 - slot], sem.at[1 - slot])

    @pl.when(tbl_ref[_TBL_NEW_A, i] == 1)
    def _():
        for src, dst in ((wg_a_ref, wg_a_sc), (wu_a_ref, wu_a_sc), (wd_a_ref, wd_a_sc)):
            dst[...] = src[0].astype(BF16)

    @pl.when(tbl_ref[_TBL_NEW_B, i] == 1)
    def _():
        for src, dst in ((wg_b_ref, wg_b_sc), (wu_b_ref, wu_b_sc), (wd_b_ref, wd_b_sc)):
            dst[...] = src[0].astype(BF16)

    def tile(rows):
        _wait_gather(tme, h2c_hbm, xbuf.at[slot], sem.at[slot])
        x = _load_token_major(xbuf.at[slot], rows)
        ms = jnp.mean(x * x, axis=-1, keepdims=True)
        hn = (x * lax.rsqrt(ms + EPS) * g2_ref[...]).astype(BF16)
        wts = wts_ref[0:rows, :]

        def expert_act(wg_sc, wu_sc, w):
            gate = jnp.dot(hn, wg_sc[...], preferred_element_type=F32)
            up = jnp.dot(hn, wu_sc[...], preferred_element_type=F32)
            return (gate / (1.0 + jnp.exp(-gate)) * up * w).astype(BF16)

        y = (jnp.dot(expert_act(wg_a_sc, wu_a_sc, wts[:, 0:1]), wd_a_sc[...], preferred_element_type=F32)
             + jnp.dot(expert_act(wg_b_sc, wu_b_sc, wts[:, 1:2]), wd_b_sc[...], preferred_element_type=F32))
        _store_token_major(yc_ref, y, rows)
        if rows < tme:
            yc_ref[rows * TOKEN_ROWS:tme * TOKEN_ROWS, :] = jnp.zeros(((tme - rows) * TOKEN_ROWS, LANES), F32)

    n_valid = tbl_ref[_TBL_NVALID, i]
    pl.when(jnp.logical_and(i < n_used, n_valid > tme // 2))(functools.partial(tile, tme))
    pl.when(jnp.logical_and(i < n_used, n_valid <= tme // 2))(functools.partial(tile, tme // 2))

    @pl.when(i >= n_used)
    def _():
        yc_ref[...] = jnp.zeros_like(yc_ref)


def _moe(tbl, src, h2c, wts_sorted, g2, wg, wu, wd, tme, n_tiles):
    used = lambda i, tbl, src: (jnp.minimum(i, tbl[_TBL_NUSED, 0] - 1), 0)
    w_a = lambda i, tbl, src: (tbl[_TBL_EA, i], 0, 0)
    w_b = lambda i, tbl, src: (tbl[_TBL_EB, i], 0, 0)
    up_shape, down_shape = (D_MODEL, D_EXPERT), (D_EXPERT, D_MODEL)
    grid_spec = pltpu.PrefetchScalarGridSpec(
        num_scalar_prefetch=2,
        grid=(n_tiles,),
        in_specs=[
            pl.BlockSpec(memory_space=pl.ANY),
            pl.BlockSpec((tme, wts_sorted.shape[1]), used),
            pl.BlockSpec((1, D_MODEL), lambda i, tbl, src: (0, 0)),
            pl.BlockSpec((1,) + up_shape, w_a),
            pl.BlockSpec((1,) + up_shape, w_a),
            pl.BlockSpec((1,) + down_shape, w_a),
            pl.BlockSpec((1,) + up_shape, w_b),
            pl.BlockSpec((1,) + up_shape, w_b),
            pl.BlockSpec((1,) + down_shape, w_b),
        ],
        out_specs=pl.BlockSpec((tme * TOKEN_ROWS, LANES), lambda i, tbl, src: (i, 0)),
        scratch_shapes=[pltpu.VMEM((2, tme * TOKEN_ROWS, LANES), F32)]
        + [pltpu.VMEM(s, BF16) for s in (up_shape, up_shape, down_shape) * 2]
        + [pltpu.SemaphoreType.DMA((2,))],
    )
    return pl.pallas_call(
        functools.partial(_moe_kernel, tme=tme),
        grid_spec=grid_spec,
        out_shape=jax.ShapeDtypeStruct((n_tiles * tme * TOKEN_ROWS, LANES), F32),
        compiler_params=_params("arbitrary"),
        name="moe",
    )(tbl, src, h2c, wts_sorted, g2, wg, wu, wd, wg, wu, wd)


def _final_kernel(dest_ref, h2c_ref, yc_hbm, gf_ref, o_ref, ybuf, sem, *, tmf):
    i = pl.program_id(0)
    slot = jnp.bitwise_and(i, 1)

    @pl.when(i == 0)
    def _():
        _start_gather(dest_ref, 0, tmf // GATHER_UNROLL, yc_hbm, ybuf.at[0], sem.at[0])

    @pl.when(i + 1 < pl.num_programs(0))
    def _():
        _start_gather(dest_ref, (i + 1) * tmf, tmf // GATHER_UNROLL, yc_hbm, ybuf.at[1 - slot],
                      sem.at[1 - slot])

    _wait_gather(tmf, yc_hbm, ybuf.at[slot], sem.at[slot])
    h = _load_token_major(h2c_ref, tmf) + _load_token_major(ybuf.at[slot], tmf)
    ms = jnp.mean(h * h, axis=-1, keepdims=True)
    o_ref[...] = h * lax.rsqrt(ms + EPS) * gf_ref[...]


def _final(dest, h2c, yc, gf, tmf):
    n = h2c.shape[0] // TOKEN_ROWS
    grid_spec = pltpu.PrefetchScalarGridSpec(
        num_scalar_prefetch=1,
        grid=(n // tmf,),
        in_specs=[
            pl.BlockSpec((tmf * TOKEN_ROWS, LANES), lambda i, dest: (i, 0)),
            pl.BlockSpec(memory_space=pl.ANY),
            pl.BlockSpec((1, D_MODEL), lambda i, dest: (0, 0)),
        ],
        out_specs=pl.BlockSpec((tmf, D_MODEL), lambda i, dest: (i, 0)),
        scratch_shapes=[pltpu.VMEM((2, tmf * TOKEN_ROWS, LANES), F32), pltpu.SemaphoreType.DMA((2,))],
    )
    return pl.pallas_call(
        functools.partial(_final_kernel, tmf=tmf),
        grid_spec=grid_spec,
        out_shape=jax.ShapeDtypeStruct((n, D_MODEL), F32),
        compiler_params=_params("arbitrary"),
        name="final",
    )(dest, h2c, yc, gf)


def _rope_tables(t):
    inv = 1.0 / (ROPE_THETA ** (np.arange(0, HEAD_DIM, 2, dtype=np.float64) / HEAD_DIM))
    ang = np.arange(t, dtype=np.float64)[:, None] * inv[None, :]
    ang = np.concatenate([ang, ang, ang, ang], axis=-1)
    sign = np.where((np.arange(LANES) % HEAD_DIM) < HEAD_DIM // 2, -1.0, 1.0)
    return (jnp.asarray(np.cos(ang), dtype=F32), jnp.asarray(np.sin(ang) * sign[None, :], dtype=F32))


def kernel(x, meta, norm1_g, w_in, b_gate, lambda_q1, lambda_k1, lambda_q2, lambda_k2, subln_g, pool_w,
           pool_scale, w_attn_br, w_pool_br, w_out, norm2_g, w_router_group, b_router_group,
           w_router_expert, b_router_expert, w_e_gate, w_e_up, w_e_down, final_g):
    batch, seq, d = x.shape
    n = batch * seq
    x2 = x.reshape(n, d)
    cos, sin = _rope_tables(seq + N_META)
    q, k, v, u, gates, qm, km, vm, um = _inproj(x2, meta, norm1_g, cos, sin, w_in[0], b_gate, tm=1024)
    a = _attention(q, k, v, qm, km, vm, lambda_q1, lambda_k1, lambda_q2, lambda_k2, subln_g, batch, seq,
                   bq=256)

    n_router = N_GROUPS + N_EXPERTS
    wr = jnp.pad(jnp.concatenate([w_router_group[0], w_router_expert[0]], axis=1),
                 ((0, 0), (0, LANES - n_router)))
    br = jnp.pad(jnp.concatenate([b_router_group[0], b_router_expert[0]]), (0, LANES - n_router))
    h2c, logits = _mix(x2, a, u, um, gates, w_attn_br[0].astype(BF16), pool_w[0].astype(BF16),
                       pool_scale, w_pool_br[0].astype(BF16), w_out[0].astype(BF16), norm2_g,
                       wr, br.reshape(1, LANES), seq, tm=512)
    tme = 512
    n_tiles = n // tme + N_BUCKETS
    assert n_tiles <= LANES
    wts, dest, tile_table = _route(logits, tl=1024, tme=tme)
    dest = dest[0]
    per_token = jnp.concatenate([wts[0:2], jnp.arange(n, dtype=F32)[None, :]], axis=0).T
    per_row = jnp.zeros((n_tiles * tme, 3), F32).at[dest].set(per_token)
    src = per_row[:, 2].astype(jnp.int32)

    yc = _moe(tile_table, src, h2c, per_row, norm2_g, w_e_gate[0], w_e_up[0], w_e_down[0], tme, n_tiles)
    out = _final(dest, h2c, yc, final_g.reshape(1, d), tmf=512)
    return out.reshape(batch, seq, d)
```

```python
import functools
import math

import jax
import jax.numpy as jnp
import numpy as np
from jax import lax
from jax.experimental import pallas as pl
from jax.experimental.pallas import tpu as pltpu

D_MODEL = 1024
N_META = 16
N_HEADS = 8
HEAD_DIM = 64
V_DIM = 2 * HEAD_DIM
POOL_WINDOWS = (2, 4, 8, 16)
N_POOL_GROUPS = len(POOL_WINDOWS)
POOL_GROUP_DIM = 128
POOL_WIDTH = N_POOL_GROUPS * POOL_GROUP_DIM
ROPE_THETA = 10000.0
N_GROUPS = 4
EXPERTS_PER_GROUP = 4
N_EXPERTS = N_GROUPS * EXPERTS_PER_GROUP
D_EXPERT = 512
EPS = 1e-6
LAMBDA_INIT = 0.8 - 0.6 * math.exp(-0.3 * 0)
LOG2_E = math.log2(math.e)

_PAIR_A = (0, 0, 0, 1, 1, 3)
_PAIR_B = (1, 2, 3, 3, 2, 2)
N_PAIRS = len(_PAIR_A)
N_BUCKETS = N_GROUPS * N_PAIRS
N_BUCKET_ROWS = 32
GATHER_UNROLL = 8

LANES = 128
NEG_BIG = -1e30
VMEM_LIMIT = 48 * 1024 * 1024

F32 = jnp.float32
BF16 = jnp.bfloat16

_HEADS_PER_STEP = 2
_IN_STEPS = N_HEADS // _HEADS_PER_STEP
_QKV_BLK = _HEADS_PER_STEP * V_DIM
_GATE_BLK = 2 * D_MODEL // _IN_STEPS
_O_K = D_MODEL // _QKV_BLK
_O_V = 2 * D_MODEL // _QKV_BLK
_O_U = 3 * D_MODEL // POOL_GROUP_DIM
_O_G = (3 * D_MODEL + POOL_WIDTH) // _GATE_BLK


def _params(*sem):
    return pltpu.CompilerParams(dimension_semantics=sem, vmem_limit_bytes=VMEM_LIMIT)


TOKEN_ROWS = D_MODEL // LANES


def _store_token_major(ref, val, rows):
    for s in range(TOKEN_ROWS):
        ref[pl.ds(s, rows, stride=TOKEN_ROWS), :] = val[:, s * LANES:(s + 1) * LANES]


def _load_token_major(ref, rows):
    return jnp.concatenate([ref[pl.ds(s, rows, stride=TOKEN_ROWS), :] for s in range(TOKEN_ROWS)], axis=1)


def _rope(z, cos, sin_signed, first_half):
    outs = []
    for c in range(z.shape[1] // LANES):
        zc = z[:, c * LANES:(c + 1) * LANES]
        rot = jnp.where(first_half, pltpu.roll(zc, LANES - HEAD_DIM // 2, 1),
                        pltpu.roll(zc, HEAD_DIM // 2, 1))
        outs.append(zc * cos + rot * sin_signed)
    return jnp.concatenate(outs, axis=1)


def _inproj_kernel(x_ref, meta_ref, g_ref, cos_ref, sin_ref, cosm_ref, sinm_ref, wq_ref, wk_ref, wv_ref,
                   wu_ref, wg_ref, bg_ref, q_ref, k_ref, v_ref, u_ref, gate_ref, qm_ref, km_ref, vm_ref,
                   um_ref, wq_sc, wk_sc, wv_sc, wu_sc, wg_sc):
    def normed(x):
        ms = jnp.mean(x * x, axis=-1, keepdims=True)
        return (x * lax.rsqrt(ms + EPS) * g_ref[...]).astype(BF16)

    def project(hn, cos, sin, q_out, k_out, v_out, u_out, gate_out):
        lane = lax.broadcasted_iota(jnp.int32, cos.shape, 1)
        first_half = jnp.bitwise_and(lane, HEAD_DIM - 1) < (HEAD_DIM // 2)
        if gate_out is not None:
            zg = jnp.dot(hn, wg_sc[...], preferred_element_type=F32) + bg_ref[...]
            gate_out[...] = (1.0 / (1.0 + jnp.exp(-zg))).astype(BF16)
        zq = jnp.dot(hn, wq_sc[...], preferred_element_type=F32)
        q_out[...] = (_rope(zq, cos, sin, first_half) * (LOG2_E / math.sqrt(HEAD_DIM))).astype(BF16)
        zk = jnp.dot(hn, wk_sc[...], preferred_element_type=F32)
        k_out[...] = _rope(zk, cos, sin, first_half).astype(BF16)
        u_out[...] = jnp.dot(hn, wu_sc[...], preferred_element_type=F32)
        v_out[...] = jnp.dot(hn, wv_sc[...], preferred_element_type=F32).astype(BF16)

    @pl.when(pl.program_id(1) == 0)
    def _():
        for src, dst in ((wq_ref, wq_sc), (wk_ref, wk_sc), (wv_ref, wv_sc), (wu_ref, wu_sc),
                         (wg_ref, wg_sc)):
            dst[...] = src[...].astype(BF16)
        project(normed(meta_ref[...]), cosm_ref[...], sinm_ref[...], qm_ref, km_ref, vm_ref, um_ref, None)

    project(normed(x_ref[...]), cos_ref[...], sin_ref[...], q_ref, k_ref, v_ref, u_ref, gate_ref)


def _inproj(x2, meta, g1, cos, sin, w_in, b_gate, tm):
    n = x2.shape[0]
    n_meta = meta.shape[0]
    n_pos_blocks = (cos.shape[0] - n_meta) // tm
    row = lambda j, i: (i, 0)
    pos = lambda j, i: (i % n_pos_blocks, 0)
    whole = lambda j, i: (0, 0)
    return pl.pallas_call(
        _inproj_kernel,
        grid=(_IN_STEPS, n // tm),
        in_specs=[
            pl.BlockSpec((tm, D_MODEL), row),
            pl.BlockSpec((n_meta, D_MODEL), whole),
            pl.BlockSpec((1, D_MODEL), whole),
            pl.BlockSpec((tm, LANES), pos),
            pl.BlockSpec((tm, LANES), pos),
            pl.BlockSpec((n_meta, LANES), whole),
            pl.BlockSpec((n_meta, LANES), whole),
            pl.BlockSpec((D_MODEL, _QKV_BLK), lambda j, i: (0, j)),
            pl.BlockSpec((D_MODEL, _QKV_BLK), lambda j, i: (0, _O_K + j)),
            pl.BlockSpec((D_MODEL, _QKV_BLK), lambda j, i: (0, _O_V + j)),
            pl.BlockSpec((D_MODEL, POOL_GROUP_DIM), lambda j, i: (0, _O_U + j)),
            pl.BlockSpec((D_MODEL, _GATE_BLK), lambda j, i: (0, _O_G + j)),
            pl.BlockSpec((1, _GATE_BLK), lambda j, i: (0, j)),
        ],
        out_specs=[
            pl.BlockSpec((tm, _QKV_BLK), lambda j, i: (i, j)),
            pl.BlockSpec((tm, _QKV_BLK), lambda j, i: (i, j)),
            pl.BlockSpec((tm, _QKV_BLK), lambda j, i: (i, j)),
            pl.BlockSpec((tm, POOL_GROUP_DIM), lambda j, i: (i, j)),
            pl.BlockSpec((tm, _GATE_BLK), lambda j, i: (i, j)),
            pl.BlockSpec((n_meta, _QKV_BLK), lambda j, i: (0, j)),
            pl.BlockSpec((n_meta, _QKV_BLK), lambda j, i: (0, j)),
            pl.BlockSpec((n_meta, _QKV_BLK), lambda j, i: (0, j)),
            pl.BlockSpec((n_meta, POOL_GROUP_DIM), lambda j, i: (0, j)),
        ],
        out_shape=[
            jax.ShapeDtypeStruct((n, D_MODEL), BF16),
            jax.ShapeDtypeStruct((n, D_MODEL), BF16),
            jax.ShapeDtypeStruct((n, D_MODEL), BF16),
            jax.ShapeDtypeStruct((n, POOL_WIDTH), F32),
            jax.ShapeDtypeStruct((n, 2 * D_MODEL), BF16),
            jax.ShapeDtypeStruct((n_meta, D_MODEL), BF16),
            jax.ShapeDtypeStruct((n_meta, D_MODEL), BF16),
            jax.ShapeDtypeStruct((n_meta, D_MODEL), BF16),
            jax.ShapeDtypeStruct((n_meta, POOL_WIDTH), F32),
        ],
        scratch_shapes=[
            pltpu.VMEM((D_MODEL, _QKV_BLK), BF16),
            pltpu.VMEM((D_MODEL, _QKV_BLK), BF16),
            pltpu.VMEM((D_MODEL, _QKV_BLK), BF16),
            pltpu.VMEM((D_MODEL, POOL_GROUP_DIM), BF16),
            pltpu.VMEM((D_MODEL, _GATE_BLK), BF16),
        ],
        compiler_params=_params("arbitrary", "arbitrary"),
        name="inproj",
    )(x2, meta, g1, cos[n_meta:], sin[n_meta:], cos[:n_meta], sin[:n_meta], w_in, w_in, w_in, w_in, w_in,
      b_gate)


def _attn_kernel(lq1_ref, lk1_ref, lq2_ref, lk2_ref, subg_ref, q_ref, k_ref, v_ref, qm_ref, km_ref,
                 vm_ref, o_ref, *state_sc, bq, bk):
    seq = k_ref.shape[0]
    nq = seq // bq
    states = [tuple(state_sc[3 * g:3 * g + 3]) for g in range(nq + 1)]

    def rows_at(ref, meta_ref, p0, n):
        if p0 == 0:
            return jnp.concatenate([meta_ref[...], ref[0:n - N_META, :]], axis=0)
        real = ref[p0 - N_META:min(p0 - N_META + n, seq), :]
        if real.shape[0] < n:
            real = jnp.concatenate([real, jnp.zeros((n - real.shape[0], real.shape[1]), real.dtype)], axis=0)
        return real

    def block_rows(i):
        return (i * bq, bq) if i < nq else (nq * bq, N_META)

    def n_full_tiles(i):
        return (block_rows(i)[0] + 1) // bk

    def diag_tile(i):
        p0 = n_full_tiles(i) * bk
        return p0, (bk if i < nq else LANES), block_rows(i)[0] - p0

    lam = (jnp.exp(jnp.sum(lq1_ref[...] * lk1_ref[...], axis=-1, keepdims=True))
           - jnp.exp(jnp.sum(lq2_ref[...] * lk2_ref[...], axis=-1, keepdims=True)) + LAMBDA_INIT)

    bias_cache = {}

    def causal_bias(n_rows, n_cols, offset):
        if (n_rows, n_cols, offset) not in bias_cache:
            r = lax.broadcasted_iota(jnp.int32, (n_rows, n_cols), 0)
            c = lax.broadcasted_iota(jnp.int32, (n_rows, n_cols), 1)
            bias = jnp.where(c <= r + offset, 0.0, NEG_BIG)
            bias_cache[(n_rows, n_cols, offset)] = jnp.concatenate([bias, bias], axis=0)
        return bias_cache[(n_rows, n_cols, offset)]

    def load_q(i, qq_sc):
        p0, n = block_rows(i)
        q = rows_at(q_ref, qm_ref, p0, n)
        lane = lax.broadcasted_iota(jnp.int32, q.shape, 1)
        zero = jnp.zeros_like(q)
        qq_sc[0:n, :] = jnp.where(lane < HEAD_DIM, q, zero)
        qq_sc[n:2 * n, :] = jnp.where(lane >= HEAD_DIM, q, zero)

    def step(state, kb, vb, bias):
        qq_sc, m_sc, acc_sc = state
        first = bias is not None
        s = lax.dot_general(qq_sc[...], kb, (((1,), (1,)), ((), ())), preferred_element_type=F32)
        if first:
            s = s + bias
        n_tiles = s.shape[1] // LANES
        rm = s[:, 0:LANES]
        for t in range(1, n_tiles):
            rm = jnp.maximum(rm, s[:, t * LANES:(t + 1) * LANES])
        rmax = jnp.max(rm, axis=-1, keepdims=True)
        if first:
            m_new = jnp.broadcast_to(rmax, (s.shape[0], LANES))
        else:
            m_prev = m_sc[...]
            m_new = jnp.maximum(m_prev, rmax)
        p = jnp.exp2(s - jnp.concatenate([m_new] * n_tiles, axis=1))
        pv = jnp.dot(p.astype(BF16), vb, preferred_element_type=F32)
        if first:
            acc_sc[...] = pv
        else:
            alpha = jnp.exp2(m_prev - m_new)
            acc_sc[...] = jnp.concatenate([alpha, alpha], axis=1) * acc_sc[...] + pv
        m_sc[...] = m_new

    def key_step(state, i, j):
        if j == 0:
            p0, width, offset = diag_tile(i)
            bias = causal_bias(block_rows(i)[1], width, offset)
        else:
            p0, width, bias = (j - 1) * bk, bk, None
        kb = rows_at(k_ref, km_ref, p0, width)
        vb = jnp.concatenate([rows_at(v_ref, vm_ref, p0, width), jnp.ones((width, V_DIM), BF16)], axis=1)
        step(state, kb, vb, bias)

    def finish(i, acc_sc):
        p0, n = block_rows(i)
        o1 = acc_sc[0:n, 0:V_DIM] / acc_sc[0:n, V_DIM:2 * V_DIM]
        o2 = acc_sc[n:2 * n, 0:V_DIM] / acc_sc[n:2 * n, V_DIM:2 * V_DIM]
        o = o1 - lam * o2
        ms = jnp.mean(o * o, axis=-1, keepdims=True)
        o = (o * lax.rsqrt(ms + EPS) * subg_ref[...] * (1.0 - LAMBDA_INIT)).astype(BF16)
        if p0 == 0:
            o_ref[0:n - N_META, :] = o[N_META:n, :]
        else:
            o_ref[p0 - N_META:p0 - N_META + n, :] = o

    order = []
    for p in range((nq + 2) // 2):
        order += [p] if p == nq - p else [p, nq - p]
    for i in order:
        load_q(i, states[i][0])
    for j in range(n_full_tiles(nq) + 1):
        for i in order:
            if j <= n_full_tiles(i):
                key_step(states[i], i, j)
    for i in order:
        finish(i, states[i][2])


def _attention(q, k, v, qm, km, vm, lq1, lk1, lq2, lk2, subg, batch, seq, bq, bk):
    nq = seq // bq
    small = lambda b, h: (0, 0)
    head = lambda b, h: (b, h)
    meta = lambda b, h: (0, h)

    def state_scratch(rows):
        return [pltpu.VMEM((2 * rows, V_DIM), BF16), pltpu.VMEM((2 * rows, LANES), F32),
                pltpu.VMEM((2 * rows, 2 * V_DIM), F32)]

    return pl.pallas_call(
        functools.partial(_attn_kernel, bq=bq, bk=bk),
        grid=(batch, N_HEADS),
        in_specs=[
            pl.BlockSpec((1, HEAD_DIM), small),
            pl.BlockSpec((1, HEAD_DIM), small),
            pl.BlockSpec((1, HEAD_DIM), small),
            pl.BlockSpec((1, HEAD_DIM), small),
            pl.BlockSpec((1, V_DIM), small),
            pl.BlockSpec((seq, V_DIM), head),
            pl.BlockSpec((seq, V_DIM), head),
            pl.BlockSpec((seq, V_DIM), head),
            pl.BlockSpec((N_META, V_DIM), meta),
            pl.BlockSpec((N_META, V_DIM), meta),
            pl.BlockSpec((N_META, V_DIM), meta),
        ],
        out_specs=pl.BlockSpec((seq, V_DIM), head),
        out_shape=jax.ShapeDtypeStruct(q.shape, BF16),
        scratch_shapes=state_scratch(bq) * nq + state_scratch(N_META),
        compiler_params=_params("arbitrary", "arbitrary"),
        name="diff_attn",
    )(lq1, lk1, lq2, lk2, subg, q, k, v, qm, km, vm)


def _mix_kernel(x_ref, a_ref, u_ref, uprev_ref, umeta_ref, gate_ref, wa_ref, pw_ref, ps_ref, wp_ref,
                wo_ref, g2_ref, wr_ref, br_ref, h2c_ref, logit_ref, ext_sc, sum_a_sc, sum_b_sc, *, tm,
                tiles_per_seq):
    i = pl.program_id(0)
    first = (i % tiles_per_seq) == 0
    y_attn = jnp.dot(a_ref[...], wa_ref[...], preferred_element_type=F32)

    top = 2 * N_META
    for ref in (ext_sc, sum_a_sc, sum_b_sc):
        ref[0:N_META, :] = jnp.zeros((N_META, POOL_WIDTH), F32)

    @pl.when(first)
    def _():
        ext_sc[N_META:top, :] = umeta_ref[...]

    @pl.when(jnp.logical_not(first))
    def _():
        ext_sc[N_META:top, :] = uprev_ref[...]

    ext_sc[top:top + tm, :] = u_ref[...]

    pooled = []
    for g, w in enumerate(POOL_WINDOWS):
        cs = slice(g * POOL_GROUP_DIM, (g + 1) * POOL_GROUP_DIM)
        src, shift = ext_sc, 1
        for dst in (sum_a_sc, sum_b_sc, sum_a_sc, sum_b_sc):
            if shift >= w:
                break
            dst[N_META:top + tm, cs] = src[N_META:top + tm, cs] + src[N_META - shift:top + tm - shift, cs]
            src, shift = dst, 2 * shift
        y = (src[top:top + tm, cs] * (1.0 / w) - ext_sc[top:top + tm, cs]).astype(BF16)
        pooled.append(jnp.dot(y, pw_ref[g], preferred_element_type=F32))
    yp = (jnp.concatenate(pooled, axis=1) * ps_ref[...]).astype(BF16)
    y_pool = jnp.dot(yp, wp_ref[...], preferred_element_type=F32)
    mixed = (gate_ref[:, 0:D_MODEL].astype(F32) * y_attn
             + gate_ref[:, D_MODEL:2 * D_MODEL].astype(F32) * y_pool)
    h2 = x_ref[...] + jnp.dot(mixed.astype(BF16), wo_ref[...], preferred_element_type=F32)
    _store_token_major(h2c_ref, h2, tm)
    ms = jnp.mean(h2 * h2, axis=-1, keepdims=True)
    hn2 = h2 * lax.rsqrt(ms + EPS) * g2_ref[...]
    hn2_hi = hn2.astype(BF16)
    hn2_lo = (hn2 - hn2_hi.astype(F32)).astype(BF16)
    wr = wr_ref[...]
    wr_hi = wr.astype(BF16)
    wr_lo = (wr - wr_hi.astype(F32)).astype(BF16)
    lhs = jnp.concatenate([hn2_hi, hn2_lo, hn2_hi], axis=1)
    rhs = jnp.concatenate([wr_hi, wr_hi, wr_lo], axis=0)
    for half in range(2):
        rows = slice(half * tm // 2, (half + 1) * tm // 2)
        logit_ref[rows, :] = jnp.dot(lhs[rows, :], rhs, preferred_element_type=F32) + br_ref[...]


def _mix(x2, a, u, umeta, gates, w_attn, pool_w, pool_scale, w_pool, w_out, g2, wr, br, seq, tm):
    n = x2.shape[0]
    tiles_per_seq = seq // tm
    halo_blocks = tm // N_META
    const2 = lambda i: (0, 0)
    return pl.pallas_call(
        functools.partial(_mix_kernel, tm=tm, tiles_per_seq=tiles_per_seq),
        grid=(n // tm,),
        in_specs=[
            pl.BlockSpec((tm, D_MODEL), lambda i: (i, 0)),
            pl.BlockSpec((tm, D_MODEL), lambda i: (i, 0)),
            pl.BlockSpec((tm, POOL_WIDTH), lambda i: (i, 0)),
            pl.BlockSpec((N_META, POOL_WIDTH), lambda i: (jnp.maximum(i * halo_blocks - 1, 0), 0)),
            pl.BlockSpec((N_META, POOL_WIDTH), const2),
            pl.BlockSpec((tm, 2 * D_MODEL), lambda i: (i, 0)),
            pl.BlockSpec((D_MODEL, D_MODEL), const2),
            pl.BlockSpec((N_POOL_GROUPS, POOL_GROUP_DIM, POOL_GROUP_DIM), lambda i: (0, 0, 0)),
            pl.BlockSpec((1, POOL_WIDTH), const2),
            pl.BlockSpec((POOL_WIDTH, D_MODEL), const2),
            pl.BlockSpec((D_MODEL, D_MODEL), const2),
            pl.BlockSpec((1, D_MODEL), const2),
            pl.BlockSpec((D_MODEL, LANES), const2),
            pl.BlockSpec((1, LANES), const2),
        ],
        out_specs=[
            pl.BlockSpec((tm * TOKEN_ROWS, LANES), lambda i: (i, 0)),
            pl.BlockSpec((tm, LANES), lambda i: (i, 0)),
        ],
        out_shape=[
            jax.ShapeDtypeStruct((n * TOKEN_ROWS, LANES), F32),
            jax.ShapeDtypeStruct((n, LANES), F32),
        ],
        scratch_shapes=[pltpu.VMEM((tm + 2 * N_META, POOL_WIDTH), F32)] * 3,
        compiler_params=_params("arbitrary"),
        name="mix",
    )(x2, a, u, u, umeta, gates, w_attn, pool_w, pool_scale, w_pool, w_out, g2, wr, br)


_TBL_EA, _TBL_EB, _TBL_NEW_A, _TBL_NEW_B, _TBL_NVALID, _TBL_NUSED = range(6)


def _route_kernel(logit_ref, tri_ref, wts_ref, dest_ref, tbl_ref, carry_sc, ids_sc, *, tme):
    i = pl.program_id(1)
    pl.when(pl.program_id(0) == 0)(functools.partial(_route_classify, i, logit_ref, tri_ref, wts_ref,
                                                     carry_sc, ids_sc))
    pl.when(pl.program_id(0) == 1)(functools.partial(_route_place, i, dest_ref, tbl_ref, carry_sc,
                                                     ids_sc, tme))


def _route_classify(i, logit_ref, tri_ref, wts_ref, carry_sc, ids_sc):
    @pl.when(i == 0)
    def _():
        carry_sc[...] = jnp.zeros_like(carry_sc)

    lg = logit_ref[...].T
    g = [lg[r:r + 1, :] for r in range(N_GROUPS)]
    gmax = functools.reduce(jnp.maximum, g)
    gsel = jnp.full_like(g[0], N_GROUPS - 1).astype(jnp.int32)
    for r in range(N_GROUPS - 2, -1, -1):
        gsel = jnp.where(g[r] == gmax, r, gsel)
    p_group = 1.0 / functools.reduce(lambda a, b: a + b, [jnp.exp(x - gmax) for x in g])
    e = []
    for jj in range(EXPERTS_PER_GROUP):
        v = lg[N_GROUPS + jj:N_GROUPS + jj + 1, :]
        for r in range(1, N_GROUPS):
            row = N_GROUPS + r * EXPERTS_PER_GROUP + jj
            v = jnp.where(gsel == r, lg[row:row + 1, :], v)
        e.append(v)
    v1 = functools.reduce(jnp.maximum, e)
    i1 = jnp.full_like(gsel, EXPERTS_PER_GROUP - 1)
    for jj in range(EXPERTS_PER_GROUP - 2, -1, -1):
        i1 = jnp.where(e[jj] == v1, jj, i1)
    rest = [jnp.where(i1 == jj, -jnp.inf, e[jj]) for jj in range(EXPERTS_PER_GROUP)]
    v2 = functools.reduce(jnp.maximum, rest)
    i2 = jnp.full_like(gsel, EXPERTS_PER_GROUP - 1)
    for jj in range(EXPERTS_PER_GROUP - 2, -1, -1):
        i2 = jnp.where(jnp.logical_and(rest[jj] == v2, i1 != jj), jj, i2)
    t = jnp.exp(v2 - v1)
    w1 = p_group / (1.0 + t)
    w2 = p_group * t / (1.0 + t)
    lo = jnp.minimum(i1, i2)
    hi = jnp.maximum(i1, i2)
    pair = jnp.where(lo == 0, hi - 1, jnp.where(lo == 1, jnp.where(hi == 3, 3, 4), 5))
    bucket = gsel * N_PAIRS + pair
    w_of_lo = jnp.where(i1 < i2, w1, w2)
    w_of_hi = jnp.where(i1 < i2, w2, w1)
    swapped = pair == 5
    w_a = jnp.where(swapped, w_of_hi, w_of_lo)
    w_b = jnp.where(swapped, w_of_lo, w_of_hi)

    tl = lg.shape[1]
    bid = lax.broadcasted_iota(jnp.int32, (N_BUCKET_ROWS, tl), 0)
    hot = bid == bucket
    hot_b = jnp.where(hot, 1.0, 0.0).astype(BF16)
    prefix = jnp.dot(hot_b, tri_ref[...], preferred_element_type=F32)
    carry = carry_sc[...]
    before = prefix + jnp.concatenate([carry] * (tl // LANES), axis=1)
    rank = jnp.sum(jnp.where(hot, before, 0.0), axis=0, keepdims=True)
    carry_sc[...] = carry + jnp.dot(hot_b, jnp.ones((tl, LANES), BF16), preferred_element_type=F32)
    pad = jnp.zeros((6, tl), F32)
    ids_sc[i] = jnp.concatenate([bucket.astype(F32), rank, pad], axis=0)
    wts_ref[...] = jnp.concatenate([w_a, w_b, pad], axis=0)


def _route_place(i, dest_ref, tbl_ref, carry_sc, ids_sc, tme):
    cnt = carry_sc[...]
    tiles = jnp.floor((cnt + (tme - 1)) * (1.0 / tme))
    r = lax.broadcasted_iota(jnp.int32, (N_BUCKET_ROWS, N_BUCKET_ROWS), 0)
    c = lax.broadcasted_iota(jnp.int32, (N_BUCKET_ROWS, N_BUCKET_ROWS), 1)
    tile_end = jnp.dot(jnp.where(c <= r, 1.0, 0.0).astype(BF16), tiles.astype(BF16),
                       preferred_element_type=F32)
    tile_start = tile_end - tiles

    ids = ids_sc[i]
    bucket, rank = ids[0:1, :], ids[1:2, :]
    tl = ids.shape[1]
    bid = lax.broadcasted_iota(jnp.int32, (N_BUCKET_ROWS, tl), 0).astype(F32)
    row_start = jnp.concatenate([tile_start * tme] * (tl // LANES), axis=1)
    dest = rank + jnp.sum(jnp.where(bid == bucket, row_start, 0.0), axis=0, keepdims=True)
    dest_ref[...] = jnp.concatenate([dest, jnp.zeros((7, tl), F32)], axis=0).astype(jnp.int32)

    lane = lax.broadcasted_iota(jnp.int32, (1, LANES), 1).astype(F32)
    n_used = tile_end[N_BUCKET_ROWS - 1:N_BUCKET_ROWS, :]
    last = jnp.minimum(lane, n_used - 1.0)
    tb = jnp.sum(jnp.where(tile_end <= last, 1.0, 0.0), axis=0, keepdims=True)
    tb = jnp.minimum(tb, N_BUCKETS - 1.0)
    group = functools.reduce(lambda a, b: a + b,
                             [jnp.where(tb >= g * N_PAIRS, 1.0, 0.0) for g in range(1, N_GROUPS)])
    pair = tb - N_PAIRS * group

    def lookup(table):
        out = jnp.full_like(pair, float(table[-1]))
        for k in range(len(table) - 2, -1, -1):
            out = jnp.where(pair == k, float(table[k]), out)
        return EXPERTS_PER_GROUP * group + out

    e_a, e_b = lookup(_PAIR_A), lookup(_PAIR_B)
    changed = lambda e: jnp.where(lane == 0.0, 1.0, jnp.where(e != pltpu.roll(e, 1, 1), 1.0, 0.0))
    bid128 = lax.broadcasted_iota(jnp.int32, (N_BUCKET_ROWS, LANES), 0).astype(F32)
    at_tile = lambda col: jnp.sum(jnp.where(bid128 == tb, col, 0.0), axis=0, keepdims=True)
    n_valid = jnp.clip(at_tile(cnt) - tme * (lane - at_tile(tile_start)), 0.0, float(tme))
    rows = [e_a, e_b, changed(e_a), changed(e_b), n_valid, n_used, jnp.zeros_like(lane),
            jnp.zeros_like(lane)]
    tbl_ref[...] = jnp.concatenate(rows, axis=0).astype(jnp.int32)


def _route(logits, tl, tme):
    n = logits.shape[0]
    nt = n // tl
    tri = jnp.asarray(np.triu(np.ones((tl, tl), np.float32), k=1), dtype=BF16)
    first_pass = lambda p, i: i * (1 - p) + (nt - 1) * p
    return pl.pallas_call(
        functools.partial(_route_kernel, tme=tme),
        grid=(2, nt),
        in_specs=[pl.BlockSpec((tl, LANES), lambda p, i: (first_pass(p, i), 0)),
                  pl.BlockSpec((tl, tl), lambda p, i: (0, 0))],
        out_specs=[pl.BlockSpec((8, tl), lambda p, i: (0, first_pass(p, i))),
                   pl.BlockSpec((8, tl), lambda p, i: (0, i * p)),
                   pl.BlockSpec((8, LANES), lambda p, i: (0, 0))],
        out_shape=[jax.ShapeDtypeStruct((8, n), F32),
                   jax.ShapeDtypeStruct((8, n), jnp.int32),
                   jax.ShapeDtypeStruct((8, LANES), jnp.int32)],
        scratch_shapes=[pltpu.VMEM((N_BUCKET_ROWS, LANES), F32), pltpu.VMEM((nt, 8, tl), F32)],
        compiler_params=_params("arbitrary", "arbitrary"),
        name="route",
    )(logits, tri)


def _token_copy(src_hbm, src_token, dst_buf, dst_row, sem):
    return pltpu.make_async_copy(
        src_hbm.at[pl.ds(pl.multiple_of(src_token * TOKEN_ROWS, TOKEN_ROWS), TOKEN_ROWS), :],
        dst_buf.at[pl.ds(pl.multiple_of(dst_row * TOKEN_ROWS, TOKEN_ROWS), TOKEN_ROWS), :],
        sem)


def _start_gather(idx_ref, first, n_groups, src_hbm, dst_buf, sem):
    def body(rr, carry):
        for uu in range(GATHER_UNROLL):
            r = rr * GATHER_UNROLL + uu
            _token_copy(src_hbm, idx_ref[first + r], dst_buf, r, sem).start(priority=uu % 2)
        return carry

    lax.fori_loop(0, n_groups, body, 0)


def _start_padded_gather(idx_ref, first, n_valid, rows, src_hbm, dst_buf, sem):
    n_groups = (n_valid + GATHER_UNROLL - 1) // GATHER_UNROLL
    _start_gather(idx_ref, first, n_groups, src_hbm, dst_buf, sem)
    pad_groups = rows // GATHER_UNROLL - n_groups
    off = n_groups * GATHER_UNROLL
    size = rows // 2
    while size >= GATHER_UNROLL:
        take = jnp.bitwise_and(pad_groups, size // GATHER_UNROLL) != 0

        @pl.when(take)
        def _(off=off, size=size):
            pltpu.make_async_copy(
                src_hbm.at[pl.ds(0, size * TOKEN_ROWS), :],
                dst_buf.at[pl.ds(pl.multiple_of(off * TOKEN_ROWS, GATHER_UNROLL * TOKEN_ROWS),
                                 size * TOKEN_ROWS), :],
                sem).start()

        off = off + jnp.where(take, size, 0)
        size //= 2


def _wait_gather(rows, src_hbm, dst_buf, sem):
    pltpu.make_async_copy(src_hbm.at[pl.ds(0, rows * TOKEN_ROWS), :], dst_buf, sem).wait()


def _moe_kernel(tbl_ref, src_ref, h2c_hbm, wts_ref, g2_ref, wg_a_ref, wu_a_ref, wd_a_ref, wg_b_ref,
                wu_b_ref, wd_b_ref, yc_ref, xbuf, wg_a_sc, wu_a_sc, wd_a_sc, wg_b_sc, wu_b_sc, wd_b_sc,
                sem, *, tme):
    i = pl.program_id(0)
    n_used = tbl_ref[_TBL_NUSED, 0]
    slot = jnp.bitwise_and(i, 1)

    @pl.when(i == 0)
    def _():
        _start_padded_gather(src_ref, 0, tbl_ref[_TBL_NVALID, 0], tme, h2c_hbm, xbuf.at[0], sem.at[0])

    @pl.when(i + 1 < n_used)
    def _():
        _start_padded_gather(src_ref, (i + 1) * tme, tbl_ref[_TBL_NVALID, i + 1], tme, h2c_hbm,
                             xbuf.at[1 - slot], sem.at[1 - slot])

    @pl.when(tbl_ref[_TBL_NEW_A, i] == 1)
    def _():
        for src, dst in ((wg_a_ref, wg_a_sc), (wu_a_ref, wu_a_sc), (wd_a_ref, wd_a_sc)):
            dst[...] = src[0].astype(BF16)

    @pl.when(tbl_ref[_TBL_NEW_B, i] == 1)
    def _():
        for src, dst in ((wg_b_ref, wg_b_sc), (wu_b_ref, wu_b_sc), (wd_b_ref, wd_b_sc)):
            dst[...] = src[0].astype(BF16)

    def tile(rows):
        _wait_gather(tme, h2c_hbm, xbuf.at[slot], sem.at[slot])
        x = _load_token_major(xbuf.at[slot], rows)
        ms = jnp.mean(x * x, axis=-1, keepdims=True)
        hn = (x * lax.rsqrt(ms + EPS) * g2_ref[...]).astype(BF16)
        wts = wts_ref[0:rows, :]

        def expert_act(wg_sc, wu_sc, w):
            gate = jnp.dot(hn, wg_sc[...], preferred_element_type=F32)
            up = jnp.dot(hn, wu_sc[...], preferred_element_type=F32)
            return (gate / (1.0 + jnp.exp(-gate)) * up * w).astype(BF16)

        y = (jnp.dot(expert_act(wg_a_sc, wu_a_sc, wts[:, 0:1]), wd_a_sc[...], preferred_element_type=F32)
             + jnp.dot(expert_act(wg_b_sc, wu_b_sc, wts[:, 1:2]), wd_b_sc[...], preferred_element_type=F32))
        _store_token_major(yc_ref, y, rows)
        if rows < tme:
            yc_ref[rows * TOKEN_ROWS:tme * TOKEN_ROWS, :] = jnp.zeros(((tme - rows) * TOKEN_ROWS, LANES), F32)

    n_valid = tbl_ref[_TBL_NVALID, i]
    quarter = tme // 4
    for k in range(1, 5):
        in_range = jnp.logical_and(n_valid > (k - 1) * quarter, n_valid <= k * quarter)
        pl.when(jnp.logical_and(i < n_used, in_range))(functools.partial(tile, k * quarter))

    @pl.when(i >= n_used)
    def _():
        yc_ref[...] = jnp.zeros_like(yc_ref)


def _moe(tbl, src, h2c, wts_sorted, g2, wg, wu, wd, tme, n_tiles):
    used = lambda i, tbl, src: (jnp.minimum(i, tbl[_TBL_NUSED, 0] - 1), 0)
    w_a = lambda i, tbl, src: (tbl[_TBL_EA, i], 0, 0)
    w_b = lambda i, tbl, src: (tbl[_TBL_EB, i], 0, 0)
    up_shape, down_shape = (D_MODEL, D_EXPERT), (D_EXPERT, D_MODEL)
    grid_spec = pltpu.PrefetchScalarGridSpec(
        num_scalar_prefetch=2,
        grid=(n_tiles,),
        in_specs=[
            pl.BlockSpec(memory_space=pl.ANY),
            pl.BlockSpec((tme, wts_sorted.shape[1]), used),
            pl.BlockSpec((1, D_MODEL), lambda i, tbl, src: (0, 0)),
            pl.BlockSpec((1,) + up_shape, w_a),
            pl.BlockSpec((1,) + up_shape, w_a),
            pl.BlockSpec((1,) + down_shape, w_a),
            pl.BlockSpec((1,) + up_shape, w_b),
            pl.BlockSpec((1,) + up_shape, w_b),
            pl.BlockSpec((1,) + down_shape, w_b),
        ],
        out_specs=pl.BlockSpec((tme * TOKEN_ROWS, LANES), lambda i, tbl, src: (i, 0)),
        scratch_shapes=[pltpu.VMEM((2, tme * TOKEN_ROWS, LANES), F32)]
        + [pltpu.VMEM(s, BF16) for s in (up_shape, up_shape, down_shape) * 2]
        + [pltpu.SemaphoreType.DMA((2,))],
    )
    return pl.pallas_call(
        functools.partial(_moe_kernel, tme=tme),
        grid_spec=grid_spec,
        out_shape=jax.ShapeDtypeStruct((n_tiles * tme * TOKEN_ROWS, LANES), F32),
        compiler_params=_params("arbitrary"),
        name="moe",
    )(tbl, src, h2c, wts_sorted, g2, wg, wu, wd, wg, wu, wd)


def _final_kernel(dest_ref, h2c_ref, yc_hbm, gf_ref, o_ref, ybuf, sem, *, tmf):
    i = pl.program_id(0)
    slot = jnp.bitwise_and(i, 1)

    @pl.when(i == 0)
    def _():
        _start_gather(dest_ref, 0, tmf // GATHER_UNROLL, yc_hbm, ybuf.at[0], sem.at[0])

    @pl.when(i + 1 < pl.num_programs(0))
    def _():
        _start_gather(dest_ref, (i + 1) * tmf, tmf // GATHER_UNROLL, yc_hbm, ybuf.at[1 - slot],
                      sem.at[1 - slot])

    _wait_gather(tmf, yc_hbm, ybuf.at[slot], sem.at[slot])
    h = _load_token_major(h2c_ref, tmf) + _load_token_major(ybuf.at[slot], tmf)
    ms = jnp.mean(h * h, axis=-1, keepdims=True)
    o_ref[...] = h * lax.rsqrt(ms + EPS) * gf_ref[...]


def _final(dest, h2c, yc, gf, tmf):
    n = h2c.shape[0] // TOKEN_ROWS
    grid_spec = pltpu.PrefetchScalarGridSpec(
        num_scalar_prefetch=1,
        grid=(n // tmf,),
        in_specs=[
            pl.BlockSpec((tmf * TOKEN_ROWS, LANES), lambda i, dest: (i, 0)),
            pl.BlockSpec(memory_space=pl.ANY),
            pl.BlockSpec((1, D_MODEL), lambda i, dest: (0, 0)),
        ],
        out_specs=pl.BlockSpec((tmf, D_MODEL), lambda i, dest: (i, 0)),
        scratch_shapes=[pltpu.VMEM((2, tmf * TOKEN_ROWS, LANES), F32), pltpu.SemaphoreType.DMA((2,))],
    )
    return pl.pallas_call(
        functools.partial(_final_kernel, tmf=tmf),
        grid_spec=grid_spec,
        out_shape=jax.ShapeDtypeStruct((n, D_MODEL), F32),
        compiler_params=_params("arbitrary"),
        name="final",
    )(dest, h2c, yc, gf)


def _rope_tables(t):
    inv = 1.0 / (ROPE_THETA ** (np.arange(0, HEAD_DIM, 2, dtype=np.float64) / HEAD_DIM))
    ang = np.arange(t, dtype=np.float64)[:, None] * inv[None, :]
    ang = np.concatenate([ang, ang, ang, ang], axis=-1)
    sign = np.where((np.arange(LANES) % HEAD_DIM) < HEAD_DIM // 2, -1.0, 1.0)
    return (jnp.asarray(np.cos(ang), dtype=F32), jnp.asarray(np.sin(ang) * sign[None, :], dtype=F32))


def kernel(x, meta, norm1_g, w_in, b_gate, lambda_q1, lambda_k1, lambda_q2, lambda_k2, subln_g, pool_w,
           pool_scale, w_attn_br, w_pool_br, w_out, norm2_g, w_router_group, b_router_group,
           w_router_expert, b_router_expert, w_e_gate, w_e_up, w_e_down, final_g):
    batch, seq, d = x.shape
    n = batch * seq
    x2 = x.reshape(n, d)
    cos, sin = _rope_tables(seq + N_META)
    q, k, v, u, gates, qm, km, vm, um = _inproj(x2, meta, norm1_g, cos, sin, w_in[0], b_gate, tm=1024)
    a = _attention(q, k, v, qm, km, vm, lambda_q1, lambda_k1, lambda_q2, lambda_k2, subln_g, batch, seq,
                   bq=256, bk=256)

    n_router = N_GROUPS + N_EXPERTS
    wr = jnp.pad(jnp.concatenate([w_router_group[0], w_router_expert[0]], axis=1),
                 ((0, 0), (0, LANES - n_router)))
    br = jnp.pad(jnp.concatenate([b_router_group[0], b_router_expert[0]]), (0, LANES - n_router))
    h2c, logits = _mix(x2, a, u, um, gates, w_attn_br[0].astype(BF16), pool_w[0].astype(BF16),
                       pool_scale, w_pool_br[0].astype(BF16), w_out[0].astype(BF16), norm2_g,
                       wr, br.reshape(1, LANES), seq, tm=512)
    tme = 512
    n_tiles = n // tme + N_BUCKETS
    assert n_tiles <= LANES
    wts, dest, tile_table = _route(logits, tl=1024, tme=tme)
    dest = dest[0]
    per_token = jnp.concatenate([wts[0:2], jnp.arange(n, dtype=F32)[None, :]], axis=0).T
    per_row = jnp.zeros((n_tiles * tme, 3), F32).at[dest].set(per_token)
    src = per_row[:, 2].astype(jnp.int32)

    yc = _moe(tile_table, src, h2c, per_row, norm2_g, w_e_gate[0], w_e_up[0], w_e_down[0], tme, n_tiles)
    out = _final(dest, h2c, yc, final_g.reshape(1, d), tmf=512)
    return out.reshape(batch, seq, d)
```

```python
import functools
import math

import jax
import jax.numpy as jnp
import numpy as np
from jax import lax
from jax.experimental import pallas as pl
from jax.experimental.pallas import tpu as pltpu

D_MODEL = 1024
N_META = 16
N_HEADS = 8
HEAD_DIM = 64
V_DIM = 2 * HEAD_DIM
POOL_WINDOWS = (2, 4, 8, 16)
N_POOL_GROUPS = len(POOL_WINDOWS)
POOL_GROUP_DIM = 128
POOL_WIDTH = N_POOL_GROUPS * POOL_GROUP_DIM
ROPE_THETA = 10000.0
N_GROUPS = 4
EXPERTS_PER_GROUP = 4
N_EXPERTS = N_GROUPS * EXPERTS_PER_GROUP
D_EXPERT = 512
EPS = 1e-6
LAMBDA_INIT = 0.8 - 0.6 * math.exp(-0.3 * 0)
LOG2_E = math.log2(math.e)

_PAIR_A = (0, 0, 0, 1, 1, 3)
_PAIR_B = (1, 2, 3, 3, 2, 2)
N_PAIRS = len(_PAIR_A)
N_BUCKETS = N_GROUPS * N_PAIRS
N_BUCKET_ROWS = 32
GATHER_UNROLL = 8

LANES = 128
NEG_BIG = -1e30
VMEM_LIMIT = 48 * 1024 * 1024

F32 = jnp.float32
BF16 = jnp.bfloat16

_HEADS_PER_STEP = 2
_IN_STEPS = N_HEADS // _HEADS_PER_STEP
_QKV_BLK = _HEADS_PER_STEP * V_DIM
_GATE_BLK = 2 * D_MODEL // _IN_STEPS
_O_K = D_MODEL // _QKV_BLK
_O_V = 2 * D_MODEL // _QKV_BLK
_O_U = 3 * D_MODEL // POOL_GROUP_DIM
_O_G = (3 * D_MODEL + POOL_WIDTH) // _GATE_BLK


def _params(*sem):
    return pltpu.CompilerParams(dimension_semantics=sem, vmem_limit_bytes=VMEM_LIMIT)


TOKEN_ROWS = D_MODEL // LANES


def _store_token_major(ref, val, rows):
    for s in range(TOKEN_ROWS):
        ref[pl.ds(s, rows, stride=TOKEN_ROWS), :] = val[:, s * LANES:(s + 1) * LANES]


def _load_token_major(ref, rows):
    return jnp.concatenate([ref[pl.ds(s, rows, stride=TOKEN_ROWS), :] for s in range(TOKEN_ROWS)], axis=1)


def _rope(z, cos, sin_signed, first_half):
    outs = []
    for c in range(z.shape[1] // LANES):
        zc = z[:, c * LANES:(c + 1) * LANES]
        rot = jnp.where(first_half, pltpu.roll(zc, LANES - HEAD_DIM // 2, 1),
                        pltpu.roll(zc, HEAD_DIM // 2, 1))
        outs.append(zc * cos + rot * sin_signed)
    return jnp.concatenate(outs, axis=1)


def _inproj_kernel(x_ref, meta_ref, g_ref, cos_ref, sin_ref, cosm_ref, sinm_ref, wq_ref, wk_ref, wv_ref,
                   wu_ref, wg_ref, bg_ref, q_ref, k_ref, v_ref, u_ref, gate_ref, qm_ref, km_ref, vm_ref,
                   um_ref, wq_sc, wk_sc, wv_sc, wu_sc, wg_sc):
    def normed(x):
        ms = jnp.mean(x * x, axis=-1, keepdims=True)
        return (x * lax.rsqrt(ms + EPS) * g_ref[...]).astype(BF16)

    def project(hn, cos, sin, q_out, k_out, v_out, u_out, gate_out):
        lane = lax.broadcasted_iota(jnp.int32, cos.shape, 1)
        first_half = jnp.bitwise_and(lane, HEAD_DIM - 1) < (HEAD_DIM // 2)
        if gate_out is not None:
            zg = jnp.dot(hn, wg_sc[...], preferred_element_type=F32) + bg_ref[...]
            gate_out[...] = (1.0 / (1.0 + jnp.exp(-zg))).astype(BF16)
        zq = jnp.dot(hn, wq_sc[...], preferred_element_type=F32)
        q_out[...] = (_rope(zq, cos, sin, first_half) * (LOG2_E / math.sqrt(HEAD_DIM))).astype(BF16)
        zk = jnp.dot(hn, wk_sc[...], preferred_element_type=F32)
        k_out[...] = _rope(zk, cos, sin, first_half).astype(BF16)
        u_out[...] = jnp.dot(hn, wu_sc[...], preferred_element_type=F32)
        v_out[...] = jnp.dot(hn, wv_sc[...], preferred_element_type=F32).astype(BF16)

    @pl.when(pl.program_id(1) == 0)
    def _():
        for src, dst in ((wq_ref, wq_sc), (wk_ref, wk_sc), (wv_ref, wv_sc), (wu_ref, wu_sc),
                         (wg_ref, wg_sc)):
            dst[...] = src[...].astype(BF16)
        project(normed(meta_ref[...]), cosm_ref[...], sinm_ref[...], qm_ref, km_ref, vm_ref, um_ref, None)

    project(normed(x_ref[...]), cos_ref[...], sin_ref[...], q_ref, k_ref, v_ref, u_ref, gate_ref)


def _inproj(x2, meta, g1, cos, sin, w_in, b_gate, tm):
    n = x2.shape[0]
    n_meta = meta.shape[0]
    n_pos_blocks = (cos.shape[0] - n_meta) // tm
    row = lambda j, i: (i, 0)
    pos = lambda j, i: (i % n_pos_blocks, 0)
    whole = lambda j, i: (0, 0)
    return pl.pallas_call(
        _inproj_kernel,
        grid=(_IN_STEPS, n // tm),
        in_specs=[
            pl.BlockSpec((tm, D_MODEL), row),
            pl.BlockSpec((n_meta, D_MODEL), whole),
            pl.BlockSpec((1, D_MODEL), whole),
            pl.BlockSpec((tm, LANES), pos),
            pl.BlockSpec((tm, LANES), pos),
            pl.BlockSpec((n_meta, LANES), whole),
            pl.BlockSpec((n_meta, LANES), whole),
            pl.BlockSpec((D_MODEL, _QKV_BLK), lambda j, i: (0, j), pipeline_mode=pl.Buffered(1)),
            pl.BlockSpec((D_MODEL, _QKV_BLK), lambda j, i: (0, _O_K + j), pipeline_mode=pl.Buffered(1)),
            pl.BlockSpec((D_MODEL, _QKV_BLK), lambda j, i: (0, _O_V + j), pipeline_mode=pl.Buffered(1)),
            pl.BlockSpec((D_MODEL, POOL_GROUP_DIM), lambda j, i: (0, _O_U + j),
                         pipeline_mode=pl.Buffered(1)),
            pl.BlockSpec((D_MODEL, _GATE_BLK), lambda j, i: (0, _O_G + j), pipeline_mode=pl.Buffered(1)),
            pl.BlockSpec((1, _GATE_BLK), lambda j, i: (0, j)),
        ],
        out_specs=[
            pl.BlockSpec((tm, _QKV_BLK), lambda j, i: (i, j)),
            pl.BlockSpec((tm, _QKV_BLK), lambda j, i: (i, j)),
            pl.BlockSpec((tm, _QKV_BLK), lambda j, i: (i, j)),
            pl.BlockSpec((tm, POOL_GROUP_DIM), lambda j, i: (i, j)),
            pl.BlockSpec((tm, _GATE_BLK), lambda j, i: (i, j)),
            pl.BlockSpec((n_meta, _QKV_BLK), lambda j, i: (0, j)),
            pl.BlockSpec((n_meta, _QKV_BLK), lambda j, i: (0, j)),
            pl.BlockSpec((n_meta, _QKV_BLK), lambda j, i: (0, j)),
            pl.BlockSpec((n_meta, POOL_GROUP_DIM), lambda j, i: (0, j)),
        ],
        out_shape=[
            jax.ShapeDtypeStruct((n, D_MODEL), BF16),
            jax.ShapeDtypeStruct((n, D_MODEL), BF16),
            jax.ShapeDtypeStruct((n, D_MODEL), BF16),
            jax.ShapeDtypeStruct((n, POOL_WIDTH), F32),
            jax.ShapeDtypeStruct((n, 2 * D_MODEL), BF16),
            jax.ShapeDtypeStruct((n_meta, D_MODEL), BF16),
            jax.ShapeDtypeStruct((n_meta, D_MODEL), BF16),
            jax.ShapeDtypeStruct((n_meta, D_MODEL), BF16),
            jax.ShapeDtypeStruct((n_meta, POOL_WIDTH), F32),
        ],
        scratch_shapes=[
            pltpu.VMEM((D_MODEL, _QKV_BLK), BF16),
            pltpu.VMEM((D_MODEL, _QKV_BLK), BF16),
            pltpu.VMEM((D_MODEL, _QKV_BLK), BF16),
            pltpu.VMEM((D_MODEL, POOL_GROUP_DIM), BF16),
            pltpu.VMEM((D_MODEL, _GATE_BLK), BF16),
        ],
        compiler_params=_params("arbitrary", "arbitrary"),
        name="inproj",
    )(x2, meta, g1, cos[n_meta:], sin[n_meta:], cos[:n_meta], sin[:n_meta], w_in, w_in, w_in, w_in, w_in,
      b_gate)


def _attn_kernel(lq1_ref, lk1_ref, lq2_ref, lk2_ref, subg_ref, q_ref, k_ref, v_ref, qm_ref, km_ref,
                 vm_ref, o_ref, *state_sc, bq, bk):
    seq = k_ref.shape[0]
    nq = seq // bq
    states = [tuple(state_sc[3 * g:3 * g + 3]) for g in range(nq + 1)]

    def rows_at(ref, meta_ref, p0, n):
        if p0 == 0:
            return jnp.concatenate([meta_ref[...], ref[0:n - N_META, :]], axis=0)
        real = ref[p0 - N_META:min(p0 - N_META + n, seq), :]
        if real.shape[0] < n:
            real = jnp.concatenate([real, jnp.zeros((n - real.shape[0], real.shape[1]), real.dtype)], axis=0)
        return real

    def block_rows(i):
        return (i * bq, bq) if i < nq else (nq * bq, N_META)

    def n_full_tiles(i):
        return (block_rows(i)[0] + 1) // bk

    def diag_tile(i):
        p0 = n_full_tiles(i) * bk
        return p0, (bk if i < nq else LANES), block_rows(i)[0] - p0

    lam = (jnp.exp(jnp.sum(lq1_ref[...] * lk1_ref[...], axis=-1, keepdims=True))
           - jnp.exp(jnp.sum(lq2_ref[...] * lk2_ref[...], axis=-1, keepdims=True)) + LAMBDA_INIT)

    bias_cache = {}

    def causal_bias(n_rows, n_cols, offset):
        if (n_rows, n_cols, offset) not in bias_cache:
            r = lax.broadcasted_iota(jnp.int32, (n_rows, n_cols), 0)
            c = lax.broadcasted_iota(jnp.int32, (n_rows, n_cols), 1)
            bias = jnp.where(c <= r + offset, 0.0, NEG_BIG)
            bias_cache[(n_rows, n_cols, offset)] = jnp.concatenate([bias, bias], axis=0)
        return bias_cache[(n_rows, n_cols, offset)]

    def load_q(i, qq_sc):
        p0, n = block_rows(i)
        q = rows_at(q_ref, qm_ref, p0, n)
        lane = lax.broadcasted_iota(jnp.int32, q.shape, 1)
        zero = jnp.zeros_like(q)
        qq_sc[0:n, :] = jnp.where(lane < HEAD_DIM, q, zero)
        qq_sc[n:2 * n, :] = jnp.where(lane >= HEAD_DIM, q, zero)

    def step(state, kb, vb, bias):
        qq_sc, m_sc, acc_sc = state
        first = bias is not None
        s = lax.dot_general(qq_sc[...], kb, (((1,), (1,)), ((), ())), preferred_element_type=F32)
        if first:
            s = s + bias
        n_tiles = s.shape[1] // LANES
        rm = s[:, 0:LANES]
        for t in range(1, n_tiles):
            rm = jnp.maximum(rm, s[:, t * LANES:(t + 1) * LANES])
        rmax = jnp.max(rm, axis=-1, keepdims=True)
        if first:
            m_new = jnp.broadcast_to(rmax, (s.shape[0], LANES))
        else:
            m_prev = m_sc[...]
            m_new = jnp.maximum(m_prev, rmax)
        p = jnp.exp2(s - jnp.concatenate([m_new] * n_tiles, axis=1))
        pv = jnp.dot(p.astype(BF16), vb, preferred_element_type=F32)
        if first:
            acc_sc[...] = pv
        else:
            alpha = jnp.exp2(m_prev - m_new)
            acc_sc[...] = jnp.concatenate([alpha, alpha], axis=1) * acc_sc[...] + pv
        m_sc[...] = m_new

    def key_step(state, i, j):
        if j == 0:
            p0, width, offset = diag_tile(i)
            bias = causal_bias(block_rows(i)[1], width, offset)
        else:
            p0, width, bias = (j - 1) * bk, bk, None
        kb = rows_at(k_ref, km_ref, p0, width)
        vb = jnp.concatenate([rows_at(v_ref, vm_ref, p0, width), jnp.ones((width, V_DIM), BF16)], axis=1)
        step(state, kb, vb, bias)

    def finish(i, acc_sc):
        p0, n = block_rows(i)
        o1 = acc_sc[0:n, 0:V_DIM] / acc_sc[0:n, V_DIM:2 * V_DIM]
        o2 = acc_sc[n:2 * n, 0:V_DIM] / acc_sc[n:2 * n, V_DIM:2 * V_DIM]
        o = o1 - lam * o2
        ms = jnp.mean(o * o, axis=-1, keepdims=True)
        o = (o * lax.rsqrt(ms + EPS) * subg_ref[...] * (1.0 - LAMBDA_INIT)).astype(BF16)
        if p0 == 0:
            o_ref[0:n - N_META, :] = o[N_META:n, :]
        else:
            o_ref[p0 - N_META:p0 - N_META + n, :] = o

    order = []
    for p in range((nq + 2) // 2):
        order += [p] if p == nq - p else [p, nq - p]
    for i in order:
        load_q(i, states[i][0])
    for j in range(n_full_tiles(nq) + 1):
        for i in order:
            if j <= n_full_tiles(i):
                key_step(states[i], i, j)
    for i in order:
        finish(i, states[i][2])


def _attention(q, k, v, qm, km, vm, lq1, lk1, lq2, lk2, subg, batch, seq, bq, bk):
    nq = seq // bq
    small = lambda b, h: (0, 0)
    head = lambda b, h: (b, h)
    meta = lambda b, h: (0, h)

    def state_scratch(rows):
        return [pltpu.VMEM((2 * rows, V_DIM), BF16), pltpu.VMEM((2 * rows, LANES), F32),
                pltpu.VMEM((2 * rows, 2 * V_DIM), F32)]

    return pl.pallas_call(
        functools.partial(_attn_kernel, bq=bq, bk=bk),
        grid=(batch, N_HEADS),
        in_specs=[
            pl.BlockSpec((1, HEAD_DIM), small),
            pl.BlockSpec((1, HEAD_DIM), small),
            pl.BlockSpec((1, HEAD_DIM), small),
            pl.BlockSpec((1, HEAD_DIM), small),
            pl.BlockSpec((1, V_DIM), small),
            pl.BlockSpec((seq, V_DIM), head),
            pl.BlockSpec((seq, V_DIM), head),
            pl.BlockSpec((seq, V_DIM), head),
            pl.BlockSpec((N_META, V_DIM), meta),
            pl.BlockSpec((N_META, V_DIM), meta),
            pl.BlockSpec((N_META, V_DIM), meta),
        ],
        out_specs=pl.BlockSpec((seq, V_DIM), head),
        out_shape=jax.ShapeDtypeStruct(q.shape, BF16),
        scratch_shapes=state_scratch(bq) * nq + state_scratch(N_META),
        compiler_params=_params("arbitrary", "arbitrary"),
        name="diff_attn",
    )(lq1, lk1, lq2, lk2, subg, q, k, v, qm, km, vm)


def _mix_kernel(x_ref, a_ref, u_ref, uprev_ref, umeta_ref, gate_ref, wa_ref, pw_ref, ps_ref, wp_ref,
                wo_ref, g2_ref, wr_ref, br_ref, h2c_ref, logit_ref, ext_sc, sum_a_sc, sum_b_sc, *, tm,
                tiles_per_seq):
    i = pl.program_id(0)
    first = (i % tiles_per_seq) == 0
    y_attn = jnp.dot(a_ref[...], wa_ref[...], preferred_element_type=F32)

    top = 2 * N_META
    for ref in (ext_sc, sum_a_sc, sum_b_sc):
        ref[0:N_META, :] = jnp.zeros((N_META, POOL_WIDTH), F32)

    @pl.when(first)
    def _():
        ext_sc[N_META:top, :] = umeta_ref[...]

    @pl.when(jnp.logical_not(first))
    def _():
        ext_sc[N_META:top, :] = uprev_ref[...]

    ext_sc[top:top + tm, :] = u_ref[...]

    pooled = []
    for g, w in enumerate(POOL_WINDOWS):
        cs = slice(g * POOL_GROUP_DIM, (g + 1) * POOL_GROUP_DIM)
        src, shift = ext_sc, 1
        for dst in (sum_a_sc, sum_b_sc, sum_a_sc, sum_b_sc):
            if shift >= w:
                break
            dst[N_META:top + tm, cs] = src[N_META:top + tm, cs] + src[N_META - shift:top + tm - shift, cs]
            src, shift = dst, 2 * shift
        y = (src[top:top + tm, cs] * (1.0 / w) - ext_sc[top:top + tm, cs]).astype(BF16)
        pooled.append(jnp.dot(y, pw_ref[g], preferred_element_type=F32))
    yp = (jnp.concatenate(pooled, axis=1) * ps_ref[...]).astype(BF16)
    y_pool = jnp.dot(yp, wp_ref[...], preferred_element_type=F32)
    mixed = (gate_ref[:, 0:D_MODEL].astype(F32) * y_attn
             + gate_ref[:, D_MODEL:2 * D_MODEL].astype(F32) * y_pool)
    h2 = x_ref[...] + jnp.dot(mixed.astype(BF16), wo_ref[...], preferred_element_type=F32)
    _store_token_major(h2c_ref, h2, tm)
    ms = jnp.mean(h2 * h2, axis=-1, keepdims=True)
    hn2 = h2 * lax.rsqrt(ms + EPS) * g2_ref[...]
    hn2_hi = hn2.astype(BF16)
    hn2_lo = (hn2 - hn2_hi.astype(F32)).astype(BF16)
    wr = wr_ref[...]
    wr_hi = wr.astype(BF16)
    wr_lo = (wr - wr_hi.astype(F32)).astype(BF16)
    lhs = jnp.concatenate([hn2_hi, hn2_lo, hn2_hi], axis=1)
    rhs = jnp.concatenate([wr_hi, wr_hi, wr_lo], axis=0)
    for half in range(2):
        rows = slice(half * tm // 2, (half + 1) * tm // 2)
        logit_ref[rows, :] = jnp.dot(lhs[rows, :], rhs, preferred_element_type=F32) + br_ref[...]


def _mix(x2, a, u, umeta, gates, w_attn, pool_w, pool_scale, w_pool, w_out, g2, wr, br, seq, tm):
    n = x2.shape[0]
    tiles_per_seq = seq // tm
    halo_blocks = tm // N_META
    const2 = lambda i: (0, 0)
    return pl.pallas_call(
        functools.partial(_mix_kernel, tm=tm, tiles_per_seq=tiles_per_seq),
        grid=(n // tm,),
        in_specs=[
            pl.BlockSpec((tm, D_MODEL), lambda i: (i, 0)),
            pl.BlockSpec((tm, D_MODEL), lambda i: (i, 0)),
            pl.BlockSpec((tm, POOL_WIDTH), lambda i: (i, 0)),
            pl.BlockSpec((N_META, POOL_WIDTH), lambda i: (jnp.maximum(i * halo_blocks - 1, 0), 0)),
            pl.BlockSpec((N_META, POOL_WIDTH), const2),
            pl.BlockSpec((tm, 2 * D_MODEL), lambda i: (i, 0)),
            pl.BlockSpec((D_MODEL, D_MODEL), const2, pipeline_mode=pl.Buffered(1)),
            pl.BlockSpec((N_POOL_GROUPS, POOL_GROUP_DIM, POOL_GROUP_DIM), lambda i: (0, 0, 0),
                         pipeline_mode=pl.Buffered(1)),
            pl.BlockSpec((1, POOL_WIDTH), const2),
            pl.BlockSpec((POOL_WIDTH, D_MODEL), const2, pipeline_mode=pl.Buffered(1)),
            pl.BlockSpec((D_MODEL, D_MODEL), const2, pipeline_mode=pl.Buffered(1)),
            pl.BlockSpec((1, D_MODEL), const2),
            pl.BlockSpec((D_MODEL, LANES), const2, pipeline_mode=pl.Buffered(1)),
            pl.BlockSpec((1, LANES), const2),
        ],
        out_specs=[
            pl.BlockSpec((tm * TOKEN_ROWS, LANES), lambda i: (i, 0)),
            pl.BlockSpec((tm, LANES), lambda i: (i, 0)),
        ],
        out_shape=[
            jax.ShapeDtypeStruct((n * TOKEN_ROWS, LANES), F32),
            jax.ShapeDtypeStruct((n, LANES), F32),
        ],
        scratch_shapes=[pltpu.VMEM((tm + 2 * N_META, POOL_WIDTH), F32)] * 3,
        compiler_params=_params("arbitrary"),
        name="mix",
    )(x2, a, u, u, umeta, gates, w_attn, pool_w, pool_scale, w_pool, w_out, g2, wr, br)


_TBL_EA, _TBL_EB, _TBL_NEW_A, _TBL_NEW_B, _TBL_NVALID, _TBL_NUSED = range(6)


def _route_kernel(logit_ref, tri_ref, wts_ref, dest_ref, tbl_ref, carry_sc, ids_sc, *, tme):
    i = pl.program_id(1)
    pl.when(pl.program_id(0) == 0)(functools.partial(_route_classify, i, logit_ref, tri_ref, wts_ref,
                                                     carry_sc, ids_sc))
    pl.when(pl.program_id(0) == 1)(functools.partial(_route_place, i, dest_ref, tbl_ref, carry_sc,
                                                     ids_sc, tme))


def _route_classify(i, logit_ref, tri_ref, wts_ref, carry_sc, ids_sc):
    @pl.when(i == 0)
    def _():
        carry_sc[...] = jnp.zeros_like(carry_sc)

    lg = logit_ref[...].T
    g = [lg[r:r + 1, :] for r in range(N_GROUPS)]
    gmax = functools.reduce(jnp.maximum, g)
    gsel = jnp.full_like(g[0], N_GROUPS - 1).astype(jnp.int32)
    for r in range(N_GROUPS - 2, -1, -1):
        gsel = jnp.where(g[r] == gmax, r, gsel)
    p_group = 1.0 / functools.reduce(lambda a, b: a + b, [jnp.exp(x - gmax) for x in g])
    e = []
    for jj in range(EXPERTS_PER_GROUP):
        v = lg[N_GROUPS + jj:N_GROUPS + jj + 1, :]
        for r in range(1, N_GROUPS):
            row = N_GROUPS + r * EXPERTS_PER_GROUP + jj
            v = jnp.where(gsel == r, lg[row:row + 1, :], v)
        e.append(v)
    v1 = functools.reduce(jnp.maximum, e)
    i1 = jnp.full_like(gsel, EXPERTS_PER_GROUP - 1)
    for jj in range(EXPERTS_PER_GROUP - 2, -1, -1):
        i1 = jnp.where(e[jj] == v1, jj, i1)
    rest = [jnp.where(i1 == jj, -jnp.inf, e[jj]) for jj in range(EXPERTS_PER_GROUP)]
    v2 = functools.reduce(jnp.maximum, rest)
    i2 = jnp.full_like(gsel, EXPERTS_PER_GROUP - 1)
    for jj in range(EXPERTS_PER_GROUP - 2, -1, -1):
        i2 = jnp.where(jnp.logical_and(rest[jj] == v2, i1 != jj), jj, i2)
    t = jnp.exp(v2 - v1)
    w1 = p_group / (1.0 + t)
    w2 = p_group * t / (1.0 + t)
    lo = jnp.minimum(i1, i2)
    hi = jnp.maximum(i1, i2)
    pair = jnp.where(lo == 0, hi - 1, jnp.where(lo == 1, jnp.where(hi == 3, 3, 4), 5))
    bucket = gsel * N_PAIRS + pair
    w_of_lo = jnp.where(i1 < i2, w1, w2)
    w_of_hi = jnp.where(i1 < i2, w2, w1)
    swapped = pair == 5
    w_a = jnp.where(swapped, w_of_hi, w_of_lo)
    w_b = jnp.where(swapped, w_of_lo, w_of_hi)

    tl = lg.shape[1]
    bid = lax.broadcasted_iota(jnp.int32, (N_BUCKET_ROWS, tl), 0)
    hot = bid == bucket
    hot_b = jnp.where(hot, 1.0, 0.0).astype(BF16)
    prefix = jnp.dot(hot_b, tri_ref[...], preferred_element_type=F32)
    carry = carry_sc[...]
    before = prefix + jnp.concatenate([carry] * (tl // LANES), axis=1)
    rank = jnp.sum(jnp.where(hot, before, 0.0), axis=0, keepdims=True)
    carry_sc[...] = carry + jnp.dot(hot_b, jnp.ones((tl, LANES), BF16), preferred_element_type=F32)
    pad = jnp.zeros((6, tl), F32)
    ids_sc[i] = jnp.concatenate([bucket.astype(F32), rank, pad], axis=0)
    wts_ref[...] = jnp.concatenate([w_a, w_b, pad], axis=0)


def _route_place(i, dest_ref, tbl_ref, carry_sc, ids_sc, tme):
    cnt = carry_sc[...]
    tiles = jnp.floor((cnt + (tme - 1)) * (1.0 / tme))
    r = lax.broadcasted_iota(jnp.int32, (N_BUCKET_ROWS, N_BUCKET_ROWS), 0)
    c = lax.broadcasted_iota(jnp.int32, (N_BUCKET_ROWS, N_BUCKET_ROWS), 1)
    tile_end = jnp.dot(jnp.where(c <= r, 1.0, 0.0).astype(BF16), tiles.astype(BF16),
                       preferred_element_type=F32)
    tile_start = tile_end - tiles

    ids = ids_sc[i]
    bucket, rank = ids[0:1, :], ids[1:2, :]
    tl = ids.shape[1]
    bid = lax.broadcasted_iota(jnp.int32, (N_BUCKET_ROWS, tl), 0).astype(F32)
    row_start = jnp.concatenate([tile_start * tme] * (tl // LANES), axis=1)
    dest = rank + jnp.sum(jnp.where(bid == bucket, row_start, 0.0), axis=0, keepdims=True)
    dest_ref[...] = jnp.concatenate([dest, jnp.zeros((7, tl), F32)], axis=0).astype(jnp.int32)

    lane = lax.broadcasted_iota(jnp.int32, (1, LANES), 1).astype(F32)
    n_used = tile_end[N_BUCKET_ROWS - 1:N_BUCKET_ROWS, :]
    last = jnp.minimum(lane, n_used - 1.0)
    tb = jnp.sum(jnp.where(tile_end <= last, 1.0, 0.0), axis=0, keepdims=True)
    tb = jnp.minimum(tb, N_BUCKETS - 1.0)
    group = functools.reduce(lambda a, b: a + b,
                             [jnp.where(tb >= g * N_PAIRS, 1.0, 0.0) for g in range(1, N_GROUPS)])
    pair = tb - N_PAIRS * group

    def lookup(table):
        out = jnp.full_like(pair, float(table[-1]))
        for k in range(len(table) - 2, -1, -1):
            out = jnp.where(pair == k, float(table[k]), out)
        return EXPERTS_PER_GROUP * group + out

    e_a, e_b = lookup(_PAIR_A), lookup(_PAIR_B)
    changed = lambda e: jnp.where(lane == 0.0, 1.0, jnp.where(e != pltpu.roll(e, 1, 1), 1.0, 0.0))
    bid128 = lax.broadcasted_iota(jnp.int32, (N_BUCKET_ROWS, LANES), 0).astype(F32)
    at_tile = lambda col: jnp.sum(jnp.where(bid128 == tb, col, 0.0), axis=0, keepdims=True)
    n_valid = jnp.clip(at_tile(cnt) - tme * (lane - at_tile(tile_start)), 0.0, float(tme))
    rows = [e_a, e_b, changed(e_a), changed(e_b), n_valid, n_used, jnp.zeros_like(lane),
            jnp.zeros_like(lane)]
    tbl_ref[...] = jnp.concatenate(rows, axis=0).astype(jnp.int32)


def _route(logits, tl, tme):
    n = logits.shape[0]
    nt = n // tl
    tri = jnp.asarray(np.triu(np.ones((tl, tl), np.float32), k=1), dtype=BF16)
    first_pass = lambda p, i: i * (1 - p) + (nt - 1) * p
    return pl.pallas_call(
        functools.partial(_route_kernel, tme=tme),
        grid=(2, nt),
        in_specs=[pl.BlockSpec((tl, LANES), lambda p, i: (first_pass(p, i), 0)),
                  pl.BlockSpec((tl, tl), lambda p, i: (0, 0))],
        out_specs=[pl.BlockSpec((8, tl), lambda p, i: (0, first_pass(p, i))),
                   pl.BlockSpec((8, tl), lambda p, i: (0, i * p)),
                   pl.BlockSpec((8, LANES), lambda p, i: (0, 0))],
        out_shape=[jax.ShapeDtypeStruct((8, n), F32),
                   jax.ShapeDtypeStruct((8, n), jnp.int32),
                   jax.ShapeDtypeStruct((8, LANES), jnp.int32)],
        scratch_shapes=[pltpu.VMEM((N_BUCKET_ROWS, LANES), F32), pltpu.VMEM((nt, 8, tl), F32)],
        compiler_params=_params("arbitrary", "arbitrary"),
        name="route",
    )(logits, tri)


def _token_copy(src_hbm, src_token, dst_buf, dst_row, sem):
    return pltpu.make_async_copy(
        src_hbm.at[pl.ds(pl.multiple_of(src_token * TOKEN_ROWS, TOKEN_ROWS), TOKEN_ROWS), :],
        dst_buf.at[pl.ds(pl.multiple_of(dst_row * TOKEN_ROWS, TOKEN_ROWS), TOKEN_ROWS), :],
        sem)


def _start_gather(idx_ref, first, n_groups, src_hbm, dst_buf, sem):
    def body(rr, carry):
        for uu in range(GATHER_UNROLL):
            r = rr * GATHER_UNROLL + uu
            _token_copy(src_hbm, idx_ref[first + r], dst_buf, r, sem).start(priority=uu % 2)
        return carry

    lax.fori_loop(0, n_groups, body, 0)


def _start_padded_gather(idx_ref, first, n_valid, rows, src_hbm, dst_buf, sem):
    n_groups = (n_valid + GATHER_UNROLL - 1) // GATHER_UNROLL
    _start_gather(idx_ref, first, n_groups, src_hbm, dst_buf, sem)
    pad_groups = rows // GATHER_UNROLL - n_groups
    off = n_groups * GATHER_UNROLL
    size = rows // 2
    while size >= GATHER_UNROLL:
        take = jnp.bitwise_and(pad_groups, size // GATHER_UNROLL) != 0

        @pl.when(take)
        def _(off=off, size=size):
            pltpu.make_async_copy(
                src_hbm.at[pl.ds(0, size * TOKEN_ROWS), :],
                dst_buf.at[pl.ds(pl.multiple_of(off * TOKEN_ROWS, GATHER_UNROLL * TOKEN_ROWS),
                                 size * TOKEN_ROWS), :],
                sem).start()

        off = off + jnp.where(take, size, 0)
        size //= 2


def _wait_gather(rows, src_hbm, dst_buf, sem):
    pltpu.make_async_copy(src_hbm.at[pl.ds(0, rows * TOKEN_ROWS), :], dst_buf, sem).wait()


def _moe_kernel(tbl_ref, src_ref, h2c_hbm, wts_ref, g2_ref, wg_a_ref, wu_a_ref, wd_a_ref, wg_b_ref,
                wu_b_ref, wd_b_ref, yc_ref, xbuf, wg_a_sc, wu_a_sc, wd_a_sc, wg_b_sc, wu_b_sc, wd_b_sc,
                sem, *, tme):
    i = pl.program_id(0)
    n_used = tbl_ref[_TBL_NUSED, 0]
    slot = jnp.bitwise_and(i, 1)

    @pl.when(i == 0)
    def _():
        _start_padded_gather(src_ref, 0, tbl_ref[_TBL_NVALID, 0], tme, h2c_hbm, xbuf.at[0], sem.at[0])

    @pl.when(i + 1 < n_used)
    def _():
        _start_padded_gather(src_ref, (i + 1) * tme, tbl_ref[_TBL_NVALID, i + 1], tme, h2c_hbm,
                             xbuf.at[1 - slot], sem.at[1 - slot])

    @pl.when(tbl_ref[_TBL_NEW_A, i] == 1)
    def _():
        for src, dst in ((wg_a_ref, wg_a_sc), (wu_a_ref, wu_a_sc), (wd_a_ref, wd_a_sc)):
            dst[...] = src[0].astype(BF16)

    @pl.when(tbl_ref[_TBL_NEW_B, i] == 1)
    def _():
        for src, dst in ((wg_b_ref, wg_b_sc), (wu_b_ref, wu_b_sc), (wd_b_ref, wd_b_sc)):
            dst[...] = src[0].astype(BF16)

    def tile(rows):
        _wait_gather(tme, h2c_hbm, xbuf.at[slot], sem.at[slot])
        x = _load_token_major(xbuf.at[slot], rows)
        ms = jnp.mean(x * x, axis=-1, keepdims=True)
        hn = (x * lax.rsqrt(ms + EPS) * g2_ref[...]).astype(BF16)
        wts = wts_ref[0:rows, :]

        def expert_act(wg_sc, wu_sc, w):
            gate = jnp.dot(hn, wg_sc[...], preferred_element_type=F32)
            up = jnp.dot(hn, wu_sc[...], preferred_element_type=F32)
            return (gate / (1.0 + jnp.exp(-gate)) * up * w).astype(BF16)

        y = (jnp.dot(expert_act(wg_a_sc, wu_a_sc, wts[:, 0:1]), wd_a_sc[...], preferred_element_type=F32)
             + jnp.dot(expert_act(wg_b_sc, wu_b_sc, wts[:, 1:2]), wd_b_sc[...], preferred_element_type=F32))
        _store_token_major(yc_ref, y, rows)
        if rows < tme:
            yc_ref[rows * TOKEN_ROWS:tme * TOKEN_ROWS, :] = jnp.zeros(((tme - rows) * TOKEN_ROWS, LANES), F32)

    n_valid = tbl_ref[_TBL_NVALID, i]
    pl.when(jnp.logical_and(i < n_used, n_valid > tme // 2))(functools.partial(tile, tme))
    pl.when(jnp.logical_and(i < n_used, n_valid <= tme // 2))(functools.partial(tile, tme // 2))

    @pl.when(i >= n_used)
    def _():
        yc_ref[...] = jnp.zeros_like(yc_ref)


def _moe(tbl, src, h2c, wts_sorted, g2, wg, wu, wd, tme, n_tiles):
    used = lambda i, tbl, src: (jnp.minimum(i, tbl[_TBL_NUSED, 0] - 1), 0)
    w_a = lambda i, tbl, src: (tbl[_TBL_EA, i], 0, 0)
    w_b = lambda i, tbl, src: (tbl[_TBL_EB, i], 0, 0)
    up_shape, down_shape = (D_MODEL, D_EXPERT), (D_EXPERT, D_MODEL)
    grid_spec = pltpu.PrefetchScalarGridSpec(
        num_scalar_prefetch=2,
        grid=(n_tiles,),
        in_specs=[
            pl.BlockSpec(memory_space=pl.ANY),
            pl.BlockSpec((tme, wts_sorted.shape[1]), used),
            pl.BlockSpec((1, D_MODEL), lambda i, tbl, src: (0, 0)),
            pl.BlockSpec((1,) + up_shape, w_a),
            pl.BlockSpec((1,) + up_shape, w_a),
            pl.BlockSpec((1,) + down_shape, w_a),
            pl.BlockSpec((1,) + up_shape, w_b),
            pl.BlockSpec((1,) + up_shape, w_b),
            pl.BlockSpec((1,) + down_shape, w_b),
        ],
        out_specs=pl.BlockSpec((tme * TOKEN_ROWS, LANES), lambda i, tbl, src: (i, 0)),
        scratch_shapes=[pltpu.VMEM((2, tme * TOKEN_ROWS, LANES), F32)]
        + [pltpu.VMEM(s, BF16) for s in (up_shape, up_shape, down_shape) * 2]
        + [pltpu.SemaphoreType.DMA((2,))],
    )
    return pl.pallas_call(
        functools.partial(_moe_kernel, tme=tme),
        grid_spec=grid_spec,
        out_shape=jax.ShapeDtypeStruct((n_tiles * tme * TOKEN_ROWS, LANES), F32),
        compiler_params=_params("arbitrary"),
        name="moe",
    )(tbl, src, h2c, wts_sorted, g2, wg, wu, wd, wg, wu, wd)


def _final_kernel(dest_ref, h2c_ref, yc_hbm, gf_ref, o_ref, ybuf, sem, *, tmf):
    i = pl.program_id(0)
    slot = jnp.bitwise_and(i, 1)

    @pl.when(i == 0)
    def _():
        _start_gather(dest_ref, 0, tmf // GATHER_UNROLL, yc_hbm, ybuf.at[0], sem.at[0])

    @pl.when(i + 1 < pl.num_programs(0))
    def _():
        _start_gather(dest_ref, (i + 1) * tmf, tmf // GATHER_UNROLL, yc_hbm, ybuf.at[1 - slot],
                      sem.at[1 - slot])

    _wait_gather(tmf, yc_hbm, ybuf.at[slot], sem.at[slot])
    h = _load_token_major(h2c_ref, tmf) + _load_token_major(ybuf.at[slot], tmf)
    ms = jnp.mean(h * h, axis=-1, keepdims=True)
    o_ref[...] = h * lax.rsqrt(ms + EPS) * gf_ref[...]


def _final(dest, h2c, yc, gf, tmf):
    n = h2c.shape[0] // TOKEN_ROWS
    grid_spec = pltpu.PrefetchScalarGridSpec(
        num_scalar_prefetch=1,
        grid=(n // tmf,),
        in_specs=[
            pl.BlockSpec((tmf * TOKEN_ROWS, LANES), lambda i, dest: (i, 0)),
            pl.BlockSpec(memory_space=pl.ANY),
            pl.BlockSpec((1, D_MODEL), lambda i, dest: (0, 0)),
        ],
        out_specs=pl.BlockSpec((tmf, D_MODEL), lambda i, dest: (i, 0)),
        scratch_shapes=[pltpu.VMEM((2, tmf * TOKEN_ROWS, LANES), F32), pltpu.SemaphoreType.DMA((2,))],
    )
    return pl.pallas_call(
        functools.partial(_final_kernel, tmf=tmf),
        grid_spec=grid_spec,
        out_shape=jax.ShapeDtypeStruct((n, D_MODEL), F32),
        compiler_params=_params("arbitrary"),
        name="final",
    )(dest, h2c, yc, gf)


def _rope_tables(t):
    inv = 1.0 / (ROPE_THETA ** (np.arange(0, HEAD_DIM, 2, dtype=np.float64) / HEAD_DIM))
    ang = np.arange(t, dtype=np.float64)[:, None] * inv[None, :]
    ang = np.concatenate([ang, ang, ang, ang], axis=-1)
    sign = np.where((np.arange(LANES) % HEAD_DIM) < HEAD_DIM // 2, -1.0, 1.0)
    return (jnp.asarray(np.cos(ang), dtype=F32), jnp.asarray(np.sin(ang) * sign[None, :], dtype=F32))


def kernel(x, meta, norm1_g, w_in, b_gate, lambda_q1, lambda_k1, lambda_q2, lambda_k2, subln_g, pool_w,
           pool_scale, w_attn_br, w_pool_br, w_out, norm2_g, w_router_group, b_router_group,
           w_router_expert, b_router_expert, w_e_gate, w_e_up, w_e_down, final_g):
    batch, seq, d = x.shape
    n = batch * seq
    x2 = x.reshape(n, d)
    cos, sin = _rope_tables(seq + N_META)
    q, k, v, u, gates, qm, km, vm, um = _inproj(x2, meta, norm1_g, cos, sin, w_in[0], b_gate, tm=2048)
    a = _attention(q, k, v, qm, km, vm, lambda_q1, lambda_k1, lambda_q2, lambda_k2, subln_g, batch, seq,
                   bq=256, bk=256)

    n_router = N_GROUPS + N_EXPERTS
    wr = jnp.pad(jnp.concatenate([w_router_group[0], w_router_expert[0]], axis=1),
                 ((0, 0), (0, LANES - n_router)))
    br = jnp.pad(jnp.concatenate([b_router_group[0], b_router_expert[0]]), (0, LANES - n_router))
    h2c, logits = _mix(x2, a, u, um, gates, w_attn_br[0].astype(BF16), pool_w[0].astype(BF16),
                       pool_scale, w_pool_br[0].astype(BF16), w_out[0].astype(BF16), norm2_g,
                       wr, br.reshape(1, LANES), seq, tm=1024)
    tme = 512
    n_tiles = n // tme + N_BUCKETS
    assert n_tiles <= LANES
    wts, dest, tile_table = _route(logits, tl=1024, tme=tme)
    dest = dest[0]
    per_token = jnp.concatenate([wts[0:2], jnp.arange(n, dtype=F32)[None, :]], axis=0).T
    per_row = jnp.zeros((n_tiles * tme, 3), F32).at[dest].set(per_token)
    src = per_row[:, 2].astype(jnp.int32)

    yc = _moe(tile_table, src, h2c, per_row, norm2_g, w_e_gate[0], w_e_up[0], w_e_down[0], tme, n_tiles)
    out = _final(dest, h2c, yc, final_g.reshape(1, d), tmf=512)
    return out.reshape(batch, seq, d)
```

```python
import functools
import math

import jax
import jax.numpy as jnp
import numpy as np
from jax import lax
from jax.experimental import pallas as pl
from jax.experimental.pallas import tpu as pltpu

D_MODEL = 1024
N_META = 16
N_HEADS = 8
HEAD_DIM = 64
V_DIM = 2 * HEAD_DIM
POOL_WINDOWS = (2, 4, 8, 16)
N_POOL_GROUPS = len(POOL_WINDOWS)
POOL_GROUP_DIM = 128
POOL_WIDTH = N_POOL_GROUPS * POOL_GROUP_DIM
ROPE_THETA = 10000.0
N_GROUPS = 4
EXPERTS_PER_GROUP = 4
N_EXPERTS = N_GROUPS * EXPERTS_PER_GROUP
D_EXPERT = 512
EPS = 1e-6
LAMBDA_INIT = 0.8 - 0.6 * math.exp(-0.3 * 0)
LOG2_E = math.log2(math.e)

_PAIR_A = (0, 0, 0, 1, 1, 3)
_PAIR_B = (1, 2, 3, 3, 2, 2)
N_PAIRS = len(_PAIR_A)
N_BUCKETS = N_GROUPS * N_PAIRS
N_BUCKET_ROWS = 32
GATHER_UNROLL = 8

LANES = 128
NEG_BIG = -1e30
VMEM_LIMIT = 48 * 1024 * 1024

F32 = jnp.float32
BF16 = jnp.bfloat16

_HEADS_PER_STEP = 2
_IN_STEPS = N_HEADS // _HEADS_PER_STEP
_QKV_BLK = _HEADS_PER_STEP * V_DIM
_GATE_BLK = 2 * D_MODEL // _IN_STEPS
_O_K = D_MODEL // _QKV_BLK
_O_V = 2 * D_MODEL // _QKV_BLK
_O_U = 3 * D_MODEL // POOL_GROUP_DIM
_O_G = (3 * D_MODEL + POOL_WIDTH) // _GATE_BLK


def _params(*sem):
    return pltpu.CompilerParams(dimension_semantics=sem, vmem_limit_bytes=VMEM_LIMIT)


TOKEN_ROWS = D_MODEL // LANES


def _store_token_major(ref, val, rows):
    for s in range(TOKEN_ROWS):
        ref[pl.ds(s, rows, stride=TOKEN_ROWS), :] = val[:, s * LANES:(s + 1) * LANES]


def _load_token_major(ref, rows):
    return jnp.concatenate([ref[pl.ds(s, rows, stride=TOKEN_ROWS), :] for s in range(TOKEN_ROWS)], axis=1)


def _rope(z, cos, sin_signed, first_half):
    outs = []
    for c in range(z.shape[1] // LANES):
        zc = z[:, c * LANES:(c + 1) * LANES]
        rot = jnp.where(first_half, pltpu.roll(zc, LANES - HEAD_DIM // 2, 1),
                        pltpu.roll(zc, HEAD_DIM // 2, 1))
        outs.append(zc * cos + rot * sin_signed)
    return jnp.concatenate(outs, axis=1)


def _inproj_kernel(x_ref, meta_ref, g_ref, cos_ref, sin_ref, cosm_ref, sinm_ref, wq_ref, wk_ref, wv_ref,
                   wu_ref, wg_ref, bg_ref, q_ref, k_ref, v_ref, u_ref, gate_ref, qm_ref, km_ref, vm_ref,
                   um_ref, wq_sc, wk_sc, wv_sc, wu_sc, wg_sc):
    def normed(x):
        ms = jnp.mean(x * x, axis=-1, keepdims=True)
        return (x * lax.rsqrt(ms + EPS) * g_ref[...]).astype(BF16)

    def project(hn, cos, sin, q_out, k_out, v_out, u_out, gate_out):
        lane = lax.broadcasted_iota(jnp.int32, cos.shape, 1)
        first_half = jnp.bitwise_and(lane, HEAD_DIM - 1) < (HEAD_DIM // 2)
        if gate_out is not None:
            zg = jnp.dot(hn, wg_sc[...], preferred_element_type=F32) + bg_ref[...]
            gate_out[...] = (1.0 / (1.0 + jnp.exp(-zg))).astype(BF16)
        zq = jnp.dot(hn, wq_sc[...], preferred_element_type=F32)
        q_out[...] = (_rope(zq, cos, sin, first_half) * (LOG2_E / math.sqrt(HEAD_DIM))).astype(BF16)
        zk = jnp.dot(hn, wk_sc[...], preferred_element_type=F32)
        k_out[...] = _rope(zk, cos, sin, first_half).astype(BF16)
        u_out[...] = jnp.dot(hn, wu_sc[...], preferred_element_type=F32)
        v_out[...] = jnp.dot(hn, wv_sc[...], preferred_element_type=F32).astype(BF16)

    @pl.when(pl.program_id(1) == 0)
    def _():
        for src, dst in ((wq_ref, wq_sc), (wk_ref, wk_sc), (wv_ref, wv_sc), (wu_ref, wu_sc),
                         (wg_ref, wg_sc)):
            dst[...] = src[...].astype(BF16)
        project(normed(meta_ref[...]), cosm_ref[...], sinm_ref[...], qm_ref, km_ref, vm_ref, um_ref, None)

    project(normed(x_ref[...]), cos_ref[...], sin_ref[...], q_ref, k_ref, v_ref, u_ref, gate_ref)


def _inproj(x2, meta, g1, cos, sin, w_in, b_gate, tm):
    n = x2.shape[0]
    n_meta = meta.shape[0]
    n_pos_blocks = (cos.shape[0] - n_meta) // tm
    row = lambda j, i: (i, 0)
    pos = lambda j, i: (i % n_pos_blocks, 0)
    whole = lambda j, i: (0, 0)
    return pl.pallas_call(
        _inproj_kernel,
        grid=(_IN_STEPS, n // tm),
        in_specs=[
            pl.BlockSpec((tm, D_MODEL), row),
            pl.BlockSpec((n_meta, D_MODEL), whole),
            pl.BlockSpec((1, D_MODEL), whole),
            pl.BlockSpec((tm, LANES), pos),
            pl.BlockSpec((tm, LANES), pos),
            pl.BlockSpec((n_meta, LANES), whole),
            pl.BlockSpec((n_meta, LANES), whole),
            pl.BlockSpec((D_MODEL, _QKV_BLK), lambda j, i: (0, j)),
            pl.BlockSpec((D_MODEL, _QKV_BLK), lambda j, i: (0, _O_K + j)),
            pl.BlockSpec((D_MODEL, _QKV_BLK), lambda j, i: (0, _O_V + j)),
            pl.BlockSpec((D_MODEL, POOL_GROUP_DIM), lambda j, i: (0, _O_U + j)),
            pl.BlockSpec((D_MODEL, _GATE_BLK), lambda j, i: (0, _O_G + j)),
            pl.BlockSpec((1, _GATE_BLK), lambda j, i: (0, j)),
        ],
        out_specs=[
            pl.BlockSpec((tm, _QKV_BLK), lambda j, i: (i, j)),
            pl.BlockSpec((tm, _QKV_BLK), lambda j, i: (i, j)),
            pl.BlockSpec((tm, _QKV_BLK), lambda j, i: (i, j)),
            pl.BlockSpec((tm, POOL_GROUP_DIM), lambda j, i: (i, j)),
            pl.BlockSpec((tm, _GATE_BLK), lambda j, i: (i, j)),
            pl.BlockSpec((n_meta, _QKV_BLK), lambda j, i: (0, j)),
            pl.BlockSpec((n_meta, _QKV_BLK), lambda j, i: (0, j)),
            pl.BlockSpec((n_meta, _QKV_BLK), lambda j, i: (0, j)),
            pl.BlockSpec((n_meta, POOL_GROUP_DIM), lambda j, i: (0, j)),
        ],
        out_shape=[
            jax.ShapeDtypeStruct((n, D_MODEL), BF16),
            jax.ShapeDtypeStruct((n, D_MODEL), BF16),
            jax.ShapeDtypeStruct((n, D_MODEL), BF16),
            jax.ShapeDtypeStruct((n, POOL_WIDTH), F32),
            jax.ShapeDtypeStruct((n, 2 * D_MODEL), BF16),
            jax.ShapeDtypeStruct((n_meta, D_MODEL), BF16),
            jax.ShapeDtypeStruct((n_meta, D_MODEL), BF16),
            jax.ShapeDtypeStruct((n_meta, D_MODEL), BF16),
            jax.ShapeDtypeStruct((n_meta, POOL_WIDTH), F32),
        ],
        scratch_shapes=[
            pltpu.VMEM((D_MODEL, _QKV_BLK), BF16),
            pltpu.VMEM((D_MODEL, _QKV_BLK), BF16),
            pltpu.VMEM((D_MODEL, _QKV_BLK), BF16),
            pltpu.VMEM((D_MODEL, POOL_GROUP_DIM), BF16),
            pltpu.VMEM((D_MODEL, _GATE_BLK), BF16),
        ],
        compiler_params=_params("arbitrary", "arbitrary"),
        name="inproj",
    )(x2, meta, g1, cos[n_meta:], sin[n_meta:], cos[:n_meta], sin[:n_meta], w_in, w_in, w_in, w_in, w_in,
      b_gate)


def _attn_kernel(lq1_ref, lk1_ref, lq2_ref, lk2_ref, subg_ref, q_ref, k_ref, v_ref, qm_ref, km_ref,
                 vm_ref, o_ref, *state_sc, bq, bk):
    seq = k_ref.shape[0]
    nq = seq // bq
    states = [tuple(state_sc[3 * g:3 * g + 3]) for g in range(nq + 1)]

    def rows_at(ref, meta_ref, p0, n):
        if p0 == 0:
            return jnp.concatenate([meta_ref[...], ref[0:n - N_META, :]], axis=0)
        real = ref[p0 - N_META:min(p0 - N_META + n, seq), :]
        if real.shape[0] < n:
            real = jnp.concatenate([real, jnp.zeros((n - real.shape[0], real.shape[1]), real.dtype)], axis=0)
        return real

    def block_rows(i):
        return (i * bq, bq) if i < nq else (nq * bq, N_META)

    def n_full_tiles(i):
        return (block_rows(i)[0] + 1) // bk

    def diag_tile(i):
        p0 = n_full_tiles(i) * bk
        return p0, (bk if i < nq else LANES), block_rows(i)[0] - p0

    lam = (jnp.exp(jnp.sum(lq1_ref[...] * lk1_ref[...], axis=-1, keepdims=True))
           - jnp.exp(jnp.sum(lq2_ref[...] * lk2_ref[...], axis=-1, keepdims=True)) + LAMBDA_INIT)

    bias_cache = {}

    def causal_bias(n_rows, n_cols, offset):
        if (n_rows, n_cols, offset) not in bias_cache:
            r = lax.broadcasted_iota(jnp.int32, (n_rows, n_cols), 0)
            c = lax.broadcasted_iota(jnp.int32, (n_rows, n_cols), 1)
            bias = jnp.where(c <= r + offset, 0.0, NEG_BIG)
            bias_cache[(n_rows, n_cols, offset)] = jnp.concatenate([bias, bias], axis=0)
        return bias_cache[(n_rows, n_cols, offset)]

    def load_q(i, qq_sc):
        p0, n = block_rows(i)
        q = rows_at(q_ref, qm_ref, p0, n)
        lane = lax.broadcasted_iota(jnp.int32, q.shape, 1)
        zero = jnp.zeros_like(q)
        qq_sc[0:n, :] = jnp.where(lane < HEAD_DIM, q, zero)
        qq_sc[n:2 * n, :] = jnp.where(lane >= HEAD_DIM, q, zero)

    def step(state, kb, vb, bias):
        qq_sc, m_sc, acc_sc = state
        first = bias is not None
        s = lax.dot_general(qq_sc[...], kb, (((1,), (1,)), ((), ())), preferred_element_type=F32)
        if first:
            s = s + bias
        n_tiles = s.shape[1] // LANES
        rm = s[:, 0:LANES]
        for t in range(1, n_tiles):
            rm = jnp.maximum(rm, s[:, t * LANES:(t + 1) * LANES])
        rmax = jnp.max(rm, axis=-1, keepdims=True)
        if first:
            m_new = jnp.broadcast_to(rmax, (s.shape[0], LANES))
        else:
            m_prev = m_sc[...]
            m_new = jnp.maximum(m_prev, rmax)
        p = jnp.exp2(s - jnp.concatenate([m_new] * n_tiles, axis=1))
        pv = jnp.dot(p.astype(BF16), vb, preferred_element_type=F32)
        if first:
            acc_sc[...] = pv
        else:
            alpha = jnp.exp2(m_prev - m_new)
            acc_sc[...] = jnp.concatenate([alpha, alpha], axis=1) * acc_sc[...] + pv
        m_sc[...] = m_new

    def key_step(state, i, j):
        if j == 0:
            p0, width, offset = diag_tile(i)
            bias = causal_bias(block_rows(i)[1], width, offset)
        else:
            p0, width, bias = (j - 1) * bk, bk, None
        kb = rows_at(k_ref, km_ref, p0, width)
        vb = jnp.concatenate([rows_at(v_ref, vm_ref, p0, width), jnp.ones((width, V_DIM), BF16)], axis=1)
        step(state, kb, vb, bias)

    def finish(i, acc_sc):
        p0, n = block_rows(i)
        o1 = acc_sc[0:n, 0:V_DIM] / acc_sc[0:n, V_DIM:2 * V_DIM]
        o2 = acc_sc[n:2 * n, 0:V_DIM] / acc_sc[n:2 * n, V_DIM:2 * V_DIM]
        o = o1 - lam * o2
        ms = jnp.mean(o * o, axis=-1, keepdims=True)
        o = (o * lax.rsqrt(ms + EPS) * subg_ref[...] * (1.0 - LAMBDA_INIT)).astype(BF16)
        if p0 == 0:
            o_ref[0:n - N_META, :] = o[N_META:n, :]
        else:
            o_ref[p0 - N_META:p0 - N_META + n, :] = o

    order = []
    for p in range((nq + 2) // 2):
        order += [p] if p == nq - p else [p, nq - p]
    for i in order:
        load_q(i, states[i][0])
    for j in range(n_full_tiles(nq) + 1):
        for i in order:
            if j <= n_full_tiles(i):
                key_step(states[i], i, j)
    for i in order:
        finish(i, states[i][2])


def _attention(q, k, v, qm, km, vm, lq1, lk1, lq2, lk2, subg, batch, seq, bq, bk):
    nq = seq // bq
    small = lambda b, h: (0, 0)
    head = lambda b, h: (b, h)
    meta = lambda b, h: (0, h)

    def state_scratch(rows):
        return [pltpu.VMEM((2 * rows, V_DIM), BF16), pltpu.VMEM((2 * rows, LANES), F32),
                pltpu.VMEM((2 * rows, 2 * V_DIM), F32)]

    return pl.pallas_call(
        functools.partial(_attn_kernel, bq=bq, bk=bk),
        grid=(batch, N_HEADS),
        in_specs=[
            pl.BlockSpec((1, HEAD_DIM), small),
            pl.BlockSpec((1, HEAD_DIM), small),
            pl.BlockSpec((1, HEAD_DIM), small),
            pl.BlockSpec((1, HEAD_DIM), small),
            pl.BlockSpec((1, V_DIM), small),
            pl.BlockSpec((seq, V_DIM), head),
            pl.BlockSpec((seq, V_DIM), head),
            pl.BlockSpec((seq, V_DIM), head),
            pl.BlockSpec((N_META, V_DIM), meta),
            pl.BlockSpec((N_META, V_DIM), meta),
            pl.BlockSpec((N_META, V_DIM), meta),
        ],
        out_specs=pl.BlockSpec((seq, V_DIM), head),
        out_shape=jax.ShapeDtypeStruct(q.shape, BF16),
        scratch_shapes=state_scratch(bq) * nq + state_scratch(N_META),
        compiler_params=_params("arbitrary", "arbitrary"),
        name="diff_attn",
    )(lq1, lk1, lq2, lk2, subg, q, k, v, qm, km, vm)


def _mix_kernel(x_ref, a_ref, u_ref, uprev_ref, umeta_ref, gate_ref, wa_ref, pw_ref, ps_ref, wp_ref,
                wo_ref, g2_ref, wr_ref, br_ref, h2c_ref, logit_ref, ext_sc, sum_a_sc, sum_b_sc, *, tm,
                tiles_per_seq):
    i = pl.program_id(0)
    first = (i % tiles_per_seq) == 0
    y_attn = jnp.dot(a_ref[...], wa_ref[...], preferred_element_type=F32)

    top = 2 * N_META
    for ref in (ext_sc, sum_a_sc, sum_b_sc):
        ref[0:N_META, :] = jnp.zeros((N_META, POOL_WIDTH), F32)

    @pl.when(first)
    def _():
        ext_sc[N_META:top, :] = umeta_ref[...]

    @pl.when(jnp.logical_not(first))
    def _():
        ext_sc[N_META:top, :] = uprev_ref[...]

    ext_sc[top:top + tm, :] = u_ref[...]

    pooled = []
    for g, w in enumerate(POOL_WINDOWS):
        cs = slice(g * POOL_GROUP_DIM, (g + 1) * POOL_GROUP_DIM)
        src, shift = ext_sc, 1
        for dst in (sum_a_sc, sum_b_sc, sum_a_sc, sum_b_sc):
            if shift >= w:
                break
            dst[N_META:top + tm, cs] = src[N_META:top + tm, cs] + src[N_META - shift:top + tm - shift, cs]
            src, shift = dst, 2 * shift
        y = (src[top:top + tm, cs] * (1.0 / w) - ext_sc[top:top + tm, cs]).astype(BF16)
        pooled.append(jnp.dot(y, pw_ref[g], preferred_element_type=F32))
    yp = (jnp.concatenate(pooled, axis=1) * ps_ref[...]).astype(BF16)
    y_pool = jnp.dot(yp, wp_ref[...], preferred_element_type=F32)
    mixed = (gate_ref[:, 0:D_MODEL].astype(F32) * y_attn
             + gate_ref[:, D_MODEL:2 * D_MODEL].astype(F32) * y_pool)
    h2 = x_ref[...] + jnp.dot(mixed.astype(BF16), wo_ref[...], preferred_element_type=F32)
    _store_token_major(h2c_ref, h2, tm)
    ms = jnp.mean(h2 * h2, axis=-1, keepdims=True)
    hn2 = h2 * lax.rsqrt(ms + EPS) * g2_ref[...]
    hn2_hi = hn2.astype(BF16)
    hn2_lo = (hn2 - hn2_hi.astype(F32)).astype(BF16)
    wr = wr_ref[...]
    wr_hi = wr.astype(BF16)
    wr_lo = (wr - wr_hi.astype(F32)).astype(BF16)
    lhs = jnp.concatenate([hn2_hi, hn2_lo, hn2_hi], axis=1)
    rhs = jnp.concatenate([wr_hi, wr_hi, wr_lo], axis=0)
    for half in range(2):
        rows = slice(half * tm // 2, (half + 1) * tm // 2)
        logit_ref[rows, :] = jnp.dot(lhs[rows, :], rhs, preferred_element_type=F32) + br_ref[...]


def _mix(x2, a, u, umeta, gates, w_attn, pool_w, pool_scale, w_pool, w_out, g2, wr, br, seq, tm):
    n = x2.shape[0]
    tiles_per_seq = seq // tm
    halo_blocks = tm // N_META
    const2 = lambda i: (0, 0)
    return pl.pallas_call(
        functools.partial(_mix_kernel, tm=tm, tiles_per_seq=tiles_per_seq),
        grid=(n // tm,),
        in_specs=[
            pl.BlockSpec((tm, D_MODEL), lambda i: (i, 0)),
            pl.BlockSpec((tm, D_MODEL), lambda i: (i, 0)),
            pl.BlockSpec((tm, POOL_WIDTH), lambda i: (i, 0)),
            pl.BlockSpec((N_META, POOL_WIDTH), lambda i: (jnp.maximum(i * halo_blocks - 1, 0), 0)),
            pl.BlockSpec((N_META, POOL_WIDTH), const2),
            pl.BlockSpec((tm, 2 * D_MODEL), lambda i: (i, 0)),
            pl.BlockSpec((D_MODEL, D_MODEL), const2, pipeline_mode=pl.Buffered(1)),
            pl.BlockSpec((N_POOL_GROUPS, POOL_GROUP_DIM, POOL_GROUP_DIM), lambda i: (0, 0, 0),
                         pipeline_mode=pl.Buffered(1)),
            pl.BlockSpec((1, POOL_WIDTH), const2),
            pl.BlockSpec((POOL_WIDTH, D_MODEL), const2, pipeline_mode=pl.Buffered(1)),
            pl.BlockSpec((D_MODEL, D_MODEL), const2, pipeline_mode=pl.Buffered(1)),
            pl.BlockSpec((1, D_MODEL), const2),
            pl.BlockSpec((D_MODEL, LANES), const2, pipeline_mode=pl.Buffered(1)),
            pl.BlockSpec((1, LANES), const2),
        ],
        out_specs=[
            pl.BlockSpec((tm * TOKEN_ROWS, LANES), lambda i: (i, 0)),
            pl.BlockSpec((tm, LANES), lambda i: (i, 0)),
        ],
        out_shape=[
            jax.ShapeDtypeStruct((n * TOKEN_ROWS, LANES), F32),
            jax.ShapeDtypeStruct((n, LANES), F32),
        ],
        scratch_shapes=[pltpu.VMEM((tm + 2 * N_META, POOL_WIDTH), F32)] * 3,
        compiler_params=_params("arbitrary"),
        name="mix",
    )(x2, a, u, u, umeta, gates, w_attn, pool_w, pool_scale, w_pool, w_out, g2, wr, br)


_TBL_EA, _TBL_EB, _TBL_NEW_A, _TBL_NEW_B, _TBL_NVALID, _TBL_NUSED = range(6)


def _route_kernel(logit_ref, tri_ref, wts_ref, dest_ref, tbl_ref, carry_sc, ids_sc, *, tme):
    i = pl.program_id(1)
    pl.when(pl.program_id(0) == 0)(functools.partial(_route_classify, i, logit_ref, tri_ref, wts_ref,
                                                     carry_sc, ids_sc))
    pl.when(pl.program_id(0) == 1)(functools.partial(_route_place, i, dest_ref, tbl_ref, carry_sc,
                                                     ids_sc, tme))


def _route_classify(i, logit_ref, tri_ref, wts_ref, carry_sc, ids_sc):
    @pl.when(i == 0)
    def _():
        carry_sc[...] = jnp.zeros_like(carry_sc)

    lg = logit_ref[...].T
    g = [lg[r:r + 1, :] for r in range(N_GROUPS)]
    gmax = functools.reduce(jnp.maximum, g)
    gsel = jnp.full_like(g[0], N_GROUPS - 1).astype(jnp.int32)
    for r in range(N_GROUPS - 2, -1, -1):
        gsel = jnp.where(g[r] == gmax, r, gsel)
    p_group = 1.0 / functools.reduce(lambda a, b: a + b, [jnp.exp(x - gmax) for x in g])
    e = []
    for jj in range(EXPERTS_PER_GROUP):
        v = lg[N_GROUPS + jj:N_GROUPS + jj + 1, :]
        for r in range(1, N_GROUPS):
            row = N_GROUPS + r * EXPERTS_PER_GROUP + jj
            v = jnp.where(gsel == r, lg[row:row + 1, :], v)
        e.append(v)
    v1 = functools.reduce(jnp.maximum, e)
    i1 = jnp.full_like(gsel, EXPERTS_PER_GROUP - 1)
    for jj in range(EXPERTS_PER_GROUP - 2, -1, -1):
        i1 = jnp.where(e[jj] == v1, jj, i1)
    rest = [jnp.where(i1 == jj, -jnp.inf, e[jj]) for jj in range(EXPERTS_PER_GROUP)]
    v2 = functools.reduce(jnp.maximum, rest)
    i2 = jnp.full_like(gsel, EXPERTS_PER_GROUP - 1)
    for jj in range(EXPERTS_PER_GROUP - 2, -1, -1):
        i2 = jnp.where(jnp.logical_and(rest[jj] == v2, i1 != jj), jj, i2)
    t = jnp.exp(v2 - v1)
    w1 = p_group / (1.0 + t)
    w2 = p_group * t / (1.0 + t)
    lo = jnp.minimum(i1, i2)
    hi = jnp.maximum(i1, i2)
    pair = jnp.where(lo == 0, hi - 1, jnp.where(lo == 1, jnp.where(hi == 3, 3, 4), 5))
    bucket = gsel * N_PAIRS + pair
    w_of_lo = jnp.where(i1 < i2, w1, w2)
    w_of_hi = jnp.where(i1 < i2, w2, w1)
    swapped = pair == 5
    w_a = jnp.where(swapped, w_of_hi, w_of_lo)
    w_b = jnp.where(swapped, w_of_lo, w_of_hi)

    tl = lg.shape[1]
    bid = lax.broadcasted_iota(jnp.int32, (N_BUCKET_ROWS, tl), 0)
    hot = bid == bucket
    hot_b = jnp.where(hot, 1.0, 0.0).astype(BF16)
    prefix = jnp.dot(hot_b, tri_ref[...], preferred_element_type=F32)
    carry = carry_sc[...]
    before = prefix + jnp.concatenate([carry] * (tl // LANES), axis=1)
    rank = jnp.sum(jnp.where(hot, before, 0.0), axis=0, keepdims=True)
    carry_sc[...] = carry + jnp.dot(hot_b, jnp.ones((tl, LANES), BF16), preferred_element_type=F32)
    pad = jnp.zeros((6, tl), F32)
    ids_sc[i] = jnp.concatenate([bucket.astype(F32), rank, pad], axis=0)
    wts_ref[...] = jnp.concatenate([w_a, w_b, pad], axis=0)


def _route_place(i, dest_ref, tbl_ref, carry_sc, ids_sc, tme):
    cnt = carry_sc[...]
    tiles = jnp.floor((cnt + (tme - 1)) * (1.0 / tme))
    r = lax.broadcasted_iota(jnp.int32, (N_BUCKET_ROWS, N_BUCKET_ROWS), 0)
    c = lax.broadcasted_iota(jnp.int32, (N_BUCKET_ROWS, N_BUCKET_ROWS), 1)
    tile_end = jnp.dot(jnp.where(c <= r, 1.0, 0.0).astype(BF16), tiles.astype(BF16),
                       preferred_element_type=F32)
    tile_start = tile_end - tiles

    ids = ids_sc[i]
    bucket, rank = ids[0:1, :], ids[1:2, :]
    tl = ids.shape[1]
    bid = lax.broadcasted_iota(jnp.int32, (N_BUCKET_ROWS, tl), 0).astype(F32)
    row_start = jnp.concatenate([tile_start * tme] * (tl // LANES), axis=1)
    dest = rank + jnp.sum(jnp.where(bid == bucket, row_start, 0.0), axis=0, keepdims=True)
    dest_ref[...] = jnp.concatenate([dest, jnp.zeros((7, tl), F32)], axis=0).astype(jnp.int32)

    lane = lax.broadcasted_iota(jnp.int32, (1, LANES), 1).astype(F32)
    n_used = tile_end[N_BUCKET_ROWS - 1:N_BUCKET_ROWS, :]
    last = jnp.minimum(lane, n_used - 1.0)
    tb = jnp.sum(jnp.where(tile_end <= last, 1.0, 0.0), axis=0, keepdims=True)
    tb = jnp.minimum(tb, N_BUCKETS - 1.0)
    group = functools.reduce(lambda a, b: a + b,
                             [jnp.where(tb >= g * N_PAIRS, 1.0, 0.0) for g in range(1, N_GROUPS)])
    pair = tb - N_PAIRS * group

    def lookup(table):
        out = jnp.full_like(pair, float(table[-1]))
        for k in range(len(table) - 2, -1, -1):
            out = jnp.where(pair == k, float(table[k]), out)
        return EXPERTS_PER_GROUP * group + out

    e_a, e_b = lookup(_PAIR_A), lookup(_PAIR_B)
    changed = lambda e: jnp.where(lane == 0.0, 1.0, jnp.where(e != pltpu.roll(e, 1, 1), 1.0, 0.0))
    bid128 = lax.broadcasted_iota(jnp.int32, (N_BUCKET_ROWS, LANES), 0).astype(F32)
    at_tile = lambda col: jnp.sum(jnp.where(bid128 == tb, col, 0.0), axis=0, keepdims=True)
    n_valid = jnp.clip(at_tile(cnt) - tme * (lane - at_tile(tile_start)), 0.0, float(tme))
    rows = [e_a, e_b, changed(e_a), changed(e_b), n_valid, n_used, jnp.zeros_like(lane),
            jnp.zeros_like(lane)]
    tbl_ref[...] = jnp.concatenate(rows, axis=0).astype(jnp.int32)


def _route(logits, tl, tme):
    n = logits.shape[0]
    nt = n // tl
    tri = jnp.asarray(np.triu(np.ones((tl, tl), np.float32), k=1), dtype=BF16)
    first_pass = lambda p, i: i * (1 - p) + (nt - 1) * p
    return pl.pallas_call(
        functools.partial(_route_kernel, tme=tme),
        grid=(2, nt),
        in_specs=[pl.BlockSpec((tl, LANES), lambda p, i: (first_pass(p, i), 0)),
                  pl.BlockSpec((tl, tl), lambda p, i: (0, 0))],
        out_specs=[pl.BlockSpec((8, tl), lambda p, i: (0, first_pass(p, i))),
                   pl.BlockSpec((8, tl), lambda p, i: (0, i * p)),
                   pl.BlockSpec((8, LANES), lambda p, i: (0, 0))],
        out_shape=[jax.ShapeDtypeStruct((8, n), F32),
                   jax.ShapeDtypeStruct((8, n), jnp.int32),
                   jax.ShapeDtypeStruct((8, LANES), jnp.int32)],
        scratch_shapes=[pltpu.VMEM((N_BUCKET_ROWS, LANES), F32), pltpu.VMEM((nt, 8, tl), F32)],
        compiler_params=_params("arbitrary", "arbitrary"),
        name="route",
    )(logits, tri)


def _token_copy(src_hbm, src_token, dst_buf, dst_row, sem):
    return pltpu.make_async_copy(
        src_hbm.at[pl.ds(pl.multiple_of(src_token * TOKEN_ROWS, TOKEN_ROWS), TOKEN_ROWS), :],
        dst_buf.at[pl.ds(pl.multiple_of(dst_row * TOKEN_ROWS, TOKEN_ROWS), TOKEN_ROWS), :],
        sem)


def _start_gather(idx_ref, first, n_groups, src_hbm, dst_buf, sem):
    def body(rr, carry):
        for uu in range(GATHER_UNROLL):
            r = rr * GATHER_UNROLL + uu
            _token_copy(src_hbm, idx_ref[first + r], dst_buf, r, sem).start(priority=uu % 2)
        return carry

    lax.fori_loop(0, n_groups, body, 0)


def _start_padded_gather(idx_ref, first, n_valid, rows, src_hbm, dst_buf, sem):
    n_groups = (n_valid + GATHER_UNROLL - 1) // GATHER_UNROLL
    _start_gather(idx_ref, first, n_groups, src_hbm, dst_buf, sem)
    pad_groups = rows // GATHER_UNROLL - n_groups
    off = n_groups * GATHER_UNROLL
    size = rows // 2
    while size >= GATHER_UNROLL:
        take = jnp.bitwise_and(pad_groups, size // GATHER_UNROLL) != 0

        @pl.when(take)
        def _(off=off, size=size):
            pltpu.make_async_copy(
                src_hbm.at[pl.ds(0, size * TOKEN_ROWS), :],
                dst_buf.at[pl.ds(pl.multiple_of(off * TOKEN_ROWS, GATHER_UNROLL * TOKEN_ROWS),
                                 size * TOKEN_ROWS), :],
                sem).start()

        off = off + jnp.where(take, size, 0)
        size //= 2


def _wait_gather(rows, src_hbm, dst_buf, sem):
    pltpu.make_async_copy(src_hbm.at[pl.ds(0, rows * TOKEN_ROWS), :], dst_buf, sem).wait()


def _moe_kernel(tbl_ref, src_ref, h2c_hbm, wts_ref, g2_ref, wg_a_ref, wu_a_ref, wd_a_ref, wg_b_ref,
                wu_b_ref, wd_b_ref, yc_ref, xbuf, wg_a_sc, wu_a_sc, wd_a_sc, wg_b_sc, wu_b_sc, wd_b_sc,
                sem, *, tme):
    i = pl.program_id(0)
    n_used = tbl_ref[_TBL_NUSED, 0]
    slot = jnp.bitwise_and(i, 1)

    @pl.when(i == 0)
    def _():
        _start_padded_gather(src_ref, 0, tbl_ref[_TBL_NVALID, 0], tme, h2c_hbm, xbuf.at[0], sem.at[0])

    @pl.when(i + 1 < n_used)
    def _():
        _start_padded_gather(src_ref, (i + 1) * tme, tbl_ref[_TBL_NVALID, i + 1], tme, h2c_hbm,
                             xbuf.at[1 - slot], sem.at[1 - slot])

    @pl.when(tbl_ref[_TBL_NEW_A, i] == 1)
    def _():
        for src, dst in ((wg_a_ref, wg_a_sc), (wu_a_ref, wu_a_sc), (wd_a_ref, wd_a_sc)):
            dst[...] = src[0].astype(BF16)

    @pl.when(tbl_ref[_TBL_NEW_B, i] == 1)
    def _():
        for src, dst in ((wg_b_ref, wg_b_sc), (wu_b_ref, wu_b_sc), (wd_b_ref, wd_b_sc)):
            dst[...] = src[0].astype(BF16)

    def tile(rows):
        _wait_gather(tme, h2c_hbm, xbuf.at[slot], sem.at[slot])
        x = _load_token_major(xbuf.at[slot], rows)
        ms = jnp.mean(x * x, axis=-1, keepdims=True)
        hn = (x * lax.rsqrt(ms + EPS) * g2_ref[...]).astype(BF16)
        wts = wts_ref[0:rows, :]

        def expert_act(wg_sc, wu_sc, w):
            gate = jnp.dot(hn, wg_sc[...], preferred_element_type=F32)
            up = jnp.dot(hn, wu_sc[...], preferred_element_type=F32)
            return (gate / (1.0 + jnp.exp(-gate)) * up * w).astype(BF16)

        y = (jnp.dot(expert_act(wg_a_sc, wu_a_sc, wts[:, 0:1]), wd_a_sc[...], preferred_element_type=F32)
             + jnp.dot(expert_act(wg_b_sc, wu_b_sc, wts[:, 1:2]), wd_b_sc[...], preferred_element_type=F32))
        _store_token_major(yc_ref, y, rows)
        if rows < tme:
            yc_ref[rows * TOKEN_ROWS:tme * TOKEN_ROWS, :] = jnp.zeros(((tme - rows) * TOKEN_ROWS, LANES), F32)

    n_valid = tbl_ref[_TBL_NVALID, i]
    pl.when(jnp.logical_and(i < n_used, n_valid > tme // 2))(functools.partial(tile, tme))
    pl.when(jnp.logical_and(i < n_used, n_valid <= tme // 2))(functools.partial(tile, tme // 2))

    @pl.when(i >= n_used)
    def _():
        yc_ref[...] = jnp.zeros_like(yc_ref)


def _moe(tbl, src, h2c, wts_sorted, g2, wg, wu, wd, tme, n_tiles):
    used = lambda i, tbl, src: (jnp.minimum(i, tbl[_TBL_NUSED, 0] - 1), 0)
    w_a = lambda i, tbl, src: (tbl[_TBL_EA, i], 0, 0)
    w_b = lambda i, tbl, src: (tbl[_TBL_EB, i], 0, 0)
    up_shape, down_shape = (D_MODEL, D_EXPERT), (D_EXPERT, D_MODEL)
    grid_spec = pltpu.PrefetchScalarGridSpec(
        num_scalar_prefetch=2,
        grid=(n_tiles,),
        in_specs=[
            pl.BlockSpec(memory_space=pl.ANY),
            pl.BlockSpec((tme, wts_sorted.shape[1]), used),
            pl.BlockSpec((1, D_MODEL), lambda i, tbl, src: (0, 0)),
            pl.BlockSpec((1,) + up_shape, w_a),
            pl.BlockSpec((1,) + up_shape, w_a),
            pl.BlockSpec((1,) + down_shape, w_a),
            pl.BlockSpec((1,) + up_shape, w_b),
            pl.BlockSpec((1,) + up_shape, w_b),
            pl.BlockSpec((1,) + down_shape, w_b),
        ],
        out_specs=pl.BlockSpec((tme * TOKEN_ROWS, LANES), lambda i, tbl, src: (i, 0)),
        scratch_shapes=[pltpu.VMEM((2, tme * TOKEN_ROWS, LANES), F32)]
        + [pltpu.VMEM(s, BF16) for s in (up_shape, up_shape, down_shape) * 2]
        + [pltpu.SemaphoreType.DMA((2,))],
    )
    return pl.pallas_call(
        functools.partial(_moe_kernel, tme=tme),
        grid_spec=grid_spec,
        out_shape=jax.ShapeDtypeStruct((n_tiles * tme * TOKEN_ROWS, LANES), F32),
        compiler_params=_params("arbitrary"),
        name="moe",
    )(tbl, src, h2c, wts_sorted, g2, wg, wu, wd, wg, wu, wd)


def _final_kernel(dest_ref, h2c_ref, yc_hbm, gf_ref, o_ref, ybuf, sem, *, tmf):
    i = pl.program_id(0)
    slot = jnp.bitwise_and(i, 1)

    @pl.when(i == 0)
    def _():
        _start_gather(dest_ref, 0, tmf // GATHER_UNROLL, yc_hbm, ybuf.at[0], sem.at[0])

    @pl.when(i + 1 < pl.num_programs(0))
    def _():
        _start_gather(dest_ref, (i + 1) * tmf, tmf // GATHER_UNROLL, yc_hbm, ybuf.at[1 - slot],
                      sem.at[1 - slot])

    _wait_gather(tmf, yc_hbm, ybuf.at[slot], sem.at[slot])
    h = _load_token_major(h2c_ref, tmf) + _load_token_major(ybuf.at[slot], tmf)
    ms = jnp.mean(h * h, axis=-1, keepdims=True)
    o_ref[...] = h * lax.rsqrt(ms + EPS) * gf_ref[...]


def _final(dest, h2c, yc, gf, tmf):
    n = h2c.shape[0] // TOKEN_ROWS
    grid_spec = pltpu.PrefetchScalarGridSpec(
        num_scalar_prefetch=1,
        grid=(n // tmf,),
        in_specs=[
            pl.BlockSpec((tmf * TOKEN_ROWS, LANES), lambda i, dest: (i, 0)),
            pl.BlockSpec(memory_space=pl.ANY),
            pl.BlockSpec((1, D_MODEL), lambda i, dest: (0, 0)),
        ],
        out_specs=pl.BlockSpec((tmf, D_MODEL), lambda i, dest: (i, 0)),
        scratch_shapes=[pltpu.VMEM((2, tmf * TOKEN_ROWS, LANES), F32), pltpu.SemaphoreType.DMA((2,))],
    )
    return pl.pallas_call(
        functools.partial(_final_kernel, tmf=tmf),
        grid_spec=grid_spec,
        out_shape=jax.ShapeDtypeStruct((n, D_MODEL), F32),
        compiler_params=_params("arbitrary"),
        name="final",
    )(dest, h2c, yc, gf)


def _rope_tables(t):
    inv = 1.0 / (ROPE_THETA ** (np.arange(0, HEAD_DIM, 2, dtype=np.float64) / HEAD_DIM))
    ang = np.arange(t, dtype=np.float64)[:, None] * inv[None, :]
    ang = np.concatenate([ang, ang, ang, ang], axis=-1)
    sign = np.where((np.arange(LANES) % HEAD_DIM) < HEAD_DIM // 2, -1.0, 1.0)
    return (jnp.asarray(np.cos(ang), dtype=F32), jnp.asarray(np.sin(ang) * sign[None, :], dtype=F32))


def kernel(x, meta, norm1_g, w_in, b_gate, lambda_q1, lambda_k1, lambda_q2, lambda_k2, subln_g, pool_w,
           pool_scale, w_attn_br, w_pool_br, w_out, norm2_g, w_router_group, b_router_group,
           w_router_expert, b_router_expert, w_e_gate, w_e_up, w_e_down, final_g):
    batch, seq, d = x.shape
    n = batch * seq
    x2 = x.reshape(n, d)
    cos, sin = _rope_tables(seq + N_META)
    q, k, v, u, gates, qm, km, vm, um = _inproj(x2, meta, norm1_g, cos, sin, w_in[0], b_gate, tm=1024)
    a = _attention(q, k, v, qm, km, vm, lambda_q1, lambda_k1, lambda_q2, lambda_k2, subln_g, batch, seq,
                   bq=256, bk=256)

    n_router = N_GROUPS + N_EXPERTS
    wr = jnp.pad(jnp.concatenate([w_router_group[0], w_router_expert[0]], axis=1),
                 ((0, 0), (0, LANES - n_router)))
    br = jnp.pad(jnp.concatenate([b_router_group[0], b_router_expert[0]]), (0, LANES - n_router))
    h2c, logits = _mix(x2, a, u, um, gates, w_attn_br[0].astype(BF16), pool_w[0].astype(BF16),
                       pool_scale, w_pool_br[0].astype(BF16), w_out[0].astype(BF16), norm2_g,
                       wr, br.reshape(1, LANES), seq, tm=1024)
    tme = 512
    n_tiles = n // tme + N_BUCKETS
    assert n_tiles <= LANES
    wts, dest, tile_table = _route(logits, tl=1024, tme=tme)
    dest = dest[0]
    per_token = jnp.concatenate([wts[0:2], jnp.arange(n, dtype=F32)[None, :]], axis=0).T
    per_row = jnp.zeros((n_tiles * tme, 3), F32).at[dest].set(per_token)
    src = per_row[:, 2].astype(jnp.int32)

    yc = _moe(tile_table, src, h2c, per_row, norm2_g, w_e_gate[0], w_e_up[0], w_e_down[0], tme, n_tiles)
    out = _final(dest, h2c, yc, final_g.reshape(1, d), tmf=512)
    return out.reshape(batch, seq, d)
```

```python
import functools
import math

import jax
import jax.numpy as jnp
import numpy as np
from jax import lax
from jax.experimental import pallas as pl
from jax.experimental.pallas import tpu as pltpu

D_MODEL = 1024
N_META = 16
N_HEADS = 8
HEAD_DIM = 64
V_DIM = 2 * HEAD_DIM
POOL_WINDOWS = (2, 4, 8, 16)
N_POOL_GROUPS = len(POOL_WINDOWS)
POOL_GROUP_DIM = 128
POOL_WIDTH = N_POOL_GROUPS * POOL_GROUP_DIM
ROPE_THETA = 10000.0
N_GROUPS = 4
EXPERTS_PER_GROUP = 4
N_EXPERTS = N_GROUPS * EXPERTS_PER_GROUP
D_EXPERT = 512
EPS = 1e-6
LAMBDA_INIT = 0.8 - 0.6 * math.exp(-0.3 * 0)
LOG2_E = math.log2(math.e)

_PAIR_A = (0, 0, 0, 1, 1, 3)
_PAIR_B = (1, 2, 3, 3, 2, 2)
N_PAIRS = len(_PAIR_A)
N_BUCKETS = N_GROUPS * N_PAIRS
N_BUCKET_ROWS = 32
GATHER_UNROLL = 8

LANES = 128
NEG_BIG = -1e30
VMEM_LIMIT = 48 * 1024 * 1024

F32 = jnp.float32
BF16 = jnp.bfloat16

_HEADS_PER_STEP = 2
_IN_STEPS = N_HEADS // _HEADS_PER_STEP
_QKV_BLK = _HEADS_PER_STEP * V_DIM
_GATE_BLK = 2 * D_MODEL // _IN_STEPS
_O_K = D_MODEL // _QKV_BLK
_O_V = 2 * D_MODEL // _QKV_BLK
_O_U = 3 * D_MODEL // POOL_GROUP_DIM
_O_G = (3 * D_MODEL + POOL_WIDTH) // _GATE_BLK


def _params(*sem):
    return pltpu.CompilerParams(dimension_semantics=sem, vmem_limit_bytes=VMEM_LIMIT)


TOKEN_ROWS = D_MODEL // LANES


def _store_token_major(ref, val, rows):
    for s in range(TOKEN_ROWS):
        ref[pl.ds(s, rows, stride=TOKEN_ROWS), :] = val[:, s * LANES:(s + 1) * LANES]


def _load_token_major(ref, rows):
    return jnp.concatenate([ref[pl.ds(s, rows, stride=TOKEN_ROWS), :] for s in range(TOKEN_ROWS)], axis=1)


def _rope(z, cos, sin_signed, first_half):
    outs = []
    for c in range(z.shape[1] // LANES):
        zc = z[:, c * LANES:(c + 1) * LANES]
        rot = jnp.where(first_half, pltpu.roll(zc, LANES - HEAD_DIM // 2, 1),
                        pltpu.roll(zc, HEAD_DIM // 2, 1))
        outs.append(zc * cos + rot * sin_signed)
    return jnp.concatenate(outs, axis=1)


def _inproj_kernel(x_ref, meta_ref, g_ref, cos_ref, sin_ref, cosm_ref, sinm_ref, wq_ref, wk_ref, wv_ref,
                   wu_ref, wg_ref, bg_ref, q_ref, k_ref, v_ref, u_ref, gate_ref, qm_ref, km_ref, vm_ref,
                   um_ref, wq_sc, wk_sc, wv_sc, wu_sc, wg_sc):
    def normed(x):
        ms = jnp.mean(x * x, axis=-1, keepdims=True)
        return (x * lax.rsqrt(ms + EPS) * g_ref[...]).astype(BF16)

    def project(hn, cos, sin, q_out, k_out, v_out, u_out, gate_out):
        lane = lax.broadcasted_iota(jnp.int32, cos.shape, 1)
        first_half = jnp.bitwise_and(lane, HEAD_DIM - 1) < (HEAD_DIM // 2)
        if gate_out is not None:
            zg = jnp.dot(hn, wg_sc[...], preferred_element_type=F32) + bg_ref[...]
            gate_out[...] = (1.0 / (1.0 + jnp.exp(-zg))).astype(BF16)
        zq = jnp.dot(hn, wq_sc[...], preferred_element_type=F32)
        q_out[...] = (_rope(zq, cos, sin, first_half) * (LOG2_E / math.sqrt(HEAD_DIM))).astype(BF16)
        zk = jnp.dot(hn, wk_sc[...], preferred_element_type=F32)
        k_out[...] = _rope(zk, cos, sin, first_half).astype(BF16)
        u_out[...] = jnp.dot(hn, wu_sc[...], preferred_element_type=F32)
        v_out[...] = jnp.dot(hn, wv_sc[...], preferred_element_type=F32).astype(BF16)

    @pl.when(pl.program_id(1) == 0)
    def _():
        for src, dst in ((wq_ref, wq_sc), (wk_ref, wk_sc), (wv_ref, wv_sc), (wu_ref, wu_sc),
                         (wg_ref, wg_sc)):
            dst[...] = src[...].astype(BF16)
        project(normed(meta_ref[...]), cosm_ref[...], sinm_ref[...], qm_ref, km_ref, vm_ref, um_ref, None)

    project(normed(x_ref[...]), cos_ref[...], sin_ref[...], q_ref, k_ref, v_ref, u_ref, gate_ref)


def _inproj(x2, meta, g1, cos, sin, w_in, b_gate, tm):
    n = x2.shape[0]
    n_meta = meta.shape[0]
    n_pos_blocks = (cos.shape[0] - n_meta) // tm
    row = lambda j, i: (i, 0)
    pos = lambda j, i: (i % n_pos_blocks, 0)
    whole = lambda j, i: (0, 0)
    return pl.pallas_call(
        _inproj_kernel,
        grid=(_IN_STEPS, n // tm),
        in_specs=[
            pl.BlockSpec((tm, D_MODEL), row),
            pl.BlockSpec((n_meta, D_MODEL), whole),
            pl.BlockSpec((1, D_MODEL), whole),
            pl.BlockSpec((tm, LANES), pos),
            pl.BlockSpec((tm, LANES), pos),
            pl.BlockSpec((n_meta, LANES), whole),
            pl.BlockSpec((n_meta, LANES), whole),
            pl.BlockSpec((D_MODEL, _QKV_BLK), lambda j, i: (0, j)),
            pl.BlockSpec((D_MODEL, _QKV_BLK), lambda j, i: (0, _O_K + j)),
            pl.BlockSpec((D_MODEL, _QKV_BLK), lambda j, i: (0, _O_V + j)),
            pl.BlockSpec((D_MODEL, POOL_GROUP_DIM), lambda j, i: (0, _O_U + j)),
            pl.BlockSpec((D_MODEL, _GATE_BLK), lambda j, i: (0, _O_G + j)),
            pl.BlockSpec((1, _GATE_BLK), lambda j, i: (0, j)),
        ],
        out_specs=[
            pl.BlockSpec((tm, _QKV_BLK), lambda j, i: (i, j)),
            pl.BlockSpec((tm, _QKV_BLK), lambda j, i: (i, j)),
            pl.BlockSpec((tm, _QKV_BLK), lambda j, i: (i, j)),
            pl.BlockSpec((tm, POOL_GROUP_DIM), lambda j, i: (i, j)),
            pl.BlockSpec((tm, _GATE_BLK), lambda j, i: (i, j)),
            pl.BlockSpec((n_meta, _QKV_BLK), lambda j, i: (0, j)),
            pl.BlockSpec((n_meta, _QKV_BLK), lambda j, i: (0, j)),
            pl.BlockSpec((n_meta, _QKV_BLK), lambda j, i: (0, j)),
            pl.BlockSpec((n_meta, POOL_GROUP_DIM), lambda j, i: (0, j)),
        ],
        out_shape=[
            jax.ShapeDtypeStruct((n, D_MODEL), BF16),
            jax.ShapeDtypeStruct((n, D_MODEL), BF16),
            jax.ShapeDtypeStruct((n, D_MODEL), BF16),
            jax.ShapeDtypeStruct((n, POOL_WIDTH), F32),
            jax.ShapeDtypeStruct((n, 2 * D_MODEL), BF16),
            jax.ShapeDtypeStruct((n_meta, D_MODEL), BF16),
            jax.ShapeDtypeStruct((n_meta, D_MODEL), BF16),
            jax.ShapeDtypeStruct((n_meta, D_MODEL), BF16),
            jax.ShapeDtypeStruct((n_meta, POOL_WIDTH), F32),
        ],
        scratch_shapes=[
            pltpu.VMEM((D_MODEL, _QKV_BLK), BF16),
            pltpu.VMEM((D_MODEL, _QKV_BLK), BF16),
            pltpu.VMEM((D_MODEL, _QKV_BLK), BF16),
            pltpu.VMEM((D_MODEL, POOL_GROUP_DIM), BF16),
            pltpu.VMEM((D_MODEL, _GATE_BLK), BF16),
        ],
        compiler_params=_params("arbitrary", "arbitrary"),
        name="inproj",
    )(x2, meta, g1, cos[n_meta:], sin[n_meta:], cos[:n_meta], sin[:n_meta], w_in, w_in, w_in, w_in, w_in,
      b_gate)


def _attn_kernel(lq1_ref, lk1_ref, lq2_ref, lk2_ref, subg_ref, q_ref, k_ref, v_ref, qm_ref, km_ref,
                 vm_ref, o_ref, *state_sc, bq, bk):
    seq = k_ref.shape[0]
    nq = seq // bq
    states = [tuple(state_sc[3 * g:3 * g + 3]) for g in range(nq + 1)]

    def rows_at(ref, meta_ref, p0, n):
        if p0 == 0:
            return jnp.concatenate([meta_ref[...], ref[0:n - N_META, :]], axis=0)
        real = ref[p0 - N_META:min(p0 - N_META + n, seq), :]
        if real.shape[0] < n:
            real = jnp.concatenate([real, jnp.zeros((n - real.shape[0], real.shape[1]), real.dtype)], axis=0)
        return real

    def block_rows(i):
        return (i * bq, bq) if i < nq else (nq * bq, N_META)

    def n_full_tiles(i):
        return (block_rows(i)[0] + 1) // bk

    def diag_tile(i):
        p0 = n_full_tiles(i) * bk
        return p0, (bk if i < nq else LANES), block_rows(i)[0] - p0

    lam = (jnp.exp(jnp.sum(lq1_ref[...] * lk1_ref[...], axis=-1, keepdims=True))
           - jnp.exp(jnp.sum(lq2_ref[...] * lk2_ref[...], axis=-1, keepdims=True)) + LAMBDA_INIT)

    bias_cache = {}

    def causal_bias(n_rows, n_cols, offset):
        if (n_rows, n_cols, offset) not in bias_cache:
            r = lax.broadcasted_iota(jnp.int32, (n_rows, n_cols), 0)
            c = lax.broadcasted_iota(jnp.int32, (n_rows, n_cols), 1)
            bias = jnp.where(c <= r + offset, 0.0, NEG_BIG)
            bias_cache[(n_rows, n_cols, offset)] = jnp.concatenate([bias, bias], axis=0)
        return bias_cache[(n_rows, n_cols, offset)]

    def load_q(i, qq_sc):
        p0, n = block_rows(i)
        q = rows_at(q_ref, qm_ref, p0, n)
        lane = lax.broadcasted_iota(jnp.int32, q.shape, 1)
        zero = jnp.zeros_like(q)
        qq_sc[0:n, :] = jnp.where(lane < HEAD_DIM, q, zero)
        qq_sc[n:2 * n, :] = jnp.where(lane >= HEAD_DIM, q, zero)

    def step(state, kb, vb, bias):
        qq_sc, m_sc, acc_sc = state
        first = bias is not None
        s = lax.dot_general(qq_sc[...], kb, (((1,), (1,)), ((), ())), preferred_element_type=F32)
        if first:
            s = s + bias
        n_tiles = s.shape[1] // LANES
        rm = s[:, 0:LANES]
        for t in range(1, n_tiles):
            rm = jnp.maximum(rm, s[:, t * LANES:(t + 1) * LANES])
        rmax = jnp.max(rm, axis=-1, keepdims=True)
        if first:
            m_new = jnp.broadcast_to(rmax, (s.shape[0], LANES))
        else:
            m_prev = m_sc[...]
            m_new = jnp.maximum(m_prev, rmax)
        p = jnp.exp2(s - jnp.concatenate([m_new] * n_tiles, axis=1))
        pv = jnp.dot(p.astype(BF16), vb, preferred_element_type=F32)
        if first:
            acc_sc[...] = pv
        else:
            alpha = jnp.exp2(m_prev - m_new)
            acc_sc[...] = jnp.concatenate([alpha, alpha], axis=1) * acc_sc[...] + pv
        m_sc[...] = m_new

    def key_step(state, i, j):
        if j == 0:
            p0, width, offset = diag_tile(i)
            bias = causal_bias(block_rows(i)[1], width, offset)
        else:
            p0, width, bias = (j - 1) * bk, bk, None
        kb = rows_at(k_ref, km_ref, p0, width)
        vb = jnp.concatenate([rows_at(v_ref, vm_ref, p0, width), jnp.ones((width, V_DIM), BF16)], axis=1)
        step(state, kb, vb, bias)

    def finish(i, acc_sc):
        p0, n = block_rows(i)
        o1 = acc_sc[0:n, 0:V_DIM] / acc_sc[0:n, V_DIM:2 * V_DIM]
        o2 = acc_sc[n:2 * n, 0:V_DIM] / acc_sc[n:2 * n, V_DIM:2 * V_DIM]
        o = o1 - lam * o2
        ms = jnp.mean(o * o, axis=-1, keepdims=True)
        o = (o * lax.rsqrt(ms + EPS) * subg_ref[...] * (1.0 - LAMBDA_INIT)).astype(BF16)
        if p0 == 0:
            o_ref[0:n - N_META, :] = o[N_META:n, :]
        else:
            o_ref[p0 - N_META:p0 - N_META + n, :] = o

    order = list(range(nq, -1, -1))
    for i in order:
        load_q(i, states[i][0])
    for j in range(n_full_tiles(nq) + 1):
        for i in order:
            if j <= n_full_tiles(i):
                key_step(states[i], i, j)
    for i in order:
        finish(i, states[i][2])


def _attention(q, k, v, qm, km, vm, lq1, lk1, lq2, lk2, subg, batch, seq, bq, bk):
    nq = seq // bq
    small = lambda b, h: (0, 0)
    head = lambda b, h: (b, h)
    meta = lambda b, h: (0, h)

    def state_scratch(rows):
        return [pltpu.VMEM((2 * rows, V_DIM), BF16), pltpu.VMEM((2 * rows, LANES), F32),
                pltpu.VMEM((2 * rows, 2 * V_DIM), F32)]

    return pl.pallas_call(
        functools.partial(_attn_kernel, bq=bq, bk=bk),
        grid=(batch, N_HEADS),
        in_specs=[
            pl.BlockSpec((1, HEAD_DIM), small),
            pl.BlockSpec((1, HEAD_DIM), small),
            pl.BlockSpec((1, HEAD_DIM), small),
            pl.BlockSpec((1, HEAD_DIM), small),
            pl.BlockSpec((1, V_DIM), small),
            pl.BlockSpec((seq, V_DIM), head),
            pl.BlockSpec((seq, V_DIM), head),
            pl.BlockSpec((seq, V_DIM), head),
            pl.BlockSpec((N_META, V_DIM), meta),
            pl.BlockSpec((N_META, V_DIM), meta),
            pl.BlockSpec((N_META, V_DIM), meta),
        ],
        out_specs=pl.BlockSpec((seq, V_DIM), head),
        out_shape=jax.ShapeDtypeStruct(q.shape, BF16),
        scratch_shapes=state_scratch(bq) * nq + state_scratch(N_META),
        compiler_params=_params("arbitrary", "arbitrary"),
        name="diff_attn",
    )(lq1, lk1, lq2, lk2, subg, q, k, v, qm, km, vm)


def _mix_kernel(x_ref, a_ref, u_ref, uprev_ref, umeta_ref, gate_ref, wa_ref, pw_ref, ps_ref, wp_ref,
                wo_ref, g2_ref, wr_ref, br_ref, h2c_ref, logit_ref, ext_sc, sum_a_sc, sum_b_sc, *, tm,
                tiles_per_seq):
    i = pl.program_id(0)
    first = (i % tiles_per_seq) == 0
    y_attn = jnp.dot(a_ref[...], wa_ref[...], preferred_element_type=F32)

    top = 2 * N_META
    for ref in (ext_sc, sum_a_sc, sum_b_sc):
        ref[0:N_META, :] = jnp.zeros((N_META, POOL_WIDTH), F32)

    @pl.when(first)
    def _():
        ext_sc[N_META:top, :] = umeta_ref[...]

    @pl.when(jnp.logical_not(first))
    def _():
        ext_sc[N_META:top, :] = uprev_ref[...]

    ext_sc[top:top + tm, :] = u_ref[...]

    pooled = []
    for g, w in enumerate(POOL_WINDOWS):
        cs = slice(g * POOL_GROUP_DIM, (g + 1) * POOL_GROUP_DIM)
        src, shift = ext_sc, 1
        for dst in (sum_a_sc, sum_b_sc, sum_a_sc, sum_b_sc):
            if shift >= w:
                break
            dst[N_META:top + tm, cs] = src[N_META:top + tm, cs] + src[N_META - shift:top + tm - shift, cs]
            src, shift = dst, 2 * shift
        y = (src[top:top + tm, cs] * (1.0 / w) - ext_sc[top:top + tm, cs]).astype(BF16)
        pooled.append(jnp.dot(y, pw_ref[g], preferred_element_type=F32))
    yp = (jnp.concatenate(pooled, axis=1) * ps_ref[...]).astype(BF16)
    y_pool = jnp.dot(yp, wp_ref[...], preferred_element_type=F32)
    mixed = (gate_ref[:, 0:D_MODEL].astype(F32) * y_attn
             + gate_ref[:, D_MODEL:2 * D_MODEL].astype(F32) * y_pool)
    h2 = x_ref[...] + jnp.dot(mixed.astype(BF16), wo_ref[...], preferred_element_type=F32)
    _store_token_major(h2c_ref, h2, tm)
    ms = jnp.mean(h2 * h2, axis=-1, keepdims=True)
    hn2 = h2 * lax.rsqrt(ms + EPS) * g2_ref[...]
    hn2_hi = hn2.astype(BF16)
    hn2_lo = (hn2 - hn2_hi.astype(F32)).astype(BF16)
    wr = wr_ref[...]
    wr_hi = wr.astype(BF16)
    wr_lo = (wr - wr_hi.astype(F32)).astype(BF16)
    lhs = jnp.concatenate([hn2_hi, hn2_lo, hn2_hi], axis=1)
    rhs = jnp.concatenate([wr_hi, wr_hi, wr_lo], axis=0)
    for half in range(2):
        rows = slice(half * tm // 2, (half + 1) * tm // 2)
        logit_ref[rows, :] = jnp.dot(lhs[rows, :], rhs, preferred_element_type=F32) + br_ref[...]


def _mix(x2, a, u, umeta, gates, w_attn, pool_w, pool_scale, w_pool, w_out, g2, wr, br, seq, tm):
    n = x2.shape[0]
    tiles_per_seq = seq // tm
    halo_blocks = tm // N_META
    const2 = lambda i: (0, 0)
    return pl.pallas_call(
        functools.partial(_mix_kernel, tm=tm, tiles_per_seq=tiles_per_seq),
        grid=(n // tm,),
        in_specs=[
            pl.BlockSpec((tm, D_MODEL), lambda i: (i, 0)),
            pl.BlockSpec((tm, D_MODEL), lambda i: (i, 0)),
            pl.BlockSpec((tm, POOL_WIDTH), lambda i: (i, 0)),
            pl.BlockSpec((N_META, POOL_WIDTH), lambda i: (jnp.maximum(i * halo_blocks - 1, 0), 0)),
            pl.BlockSpec((N_META, POOL_WIDTH), const2),
            pl.BlockSpec((tm, 2 * D_MODEL), lambda i: (i, 0)),
            pl.BlockSpec((D_MODEL, D_MODEL), const2, pipeline_mode=pl.Buffered(1)),
            pl.BlockSpec((N_POOL_GROUPS, POOL_GROUP_DIM, POOL_GROUP_DIM), lambda i: (0, 0, 0),
                         pipeline_mode=pl.Buffered(1)),
            pl.BlockSpec((1, POOL_WIDTH), const2),
            pl.BlockSpec((POOL_WIDTH, D_MODEL), const2, pipeline_mode=pl.Buffered(1)),
            pl.BlockSpec((D_MODEL, D_MODEL), const2, pipeline_mode=pl.Buffered(1)),
            pl.BlockSpec((1, D_MODEL), const2),
            pl.BlockSpec((D_MODEL, LANES), const2, pipeline_mode=pl.Buffered(1)),
            pl.BlockSpec((1, LANES), const2),
        ],
        out_specs=[
            pl.BlockSpec((tm * TOKEN_ROWS, LANES), lambda i: (i, 0)),
            pl.BlockSpec((tm, LANES), lambda i: (i, 0)),
        ],
        out_shape=[
            jax.ShapeDtypeStruct((n * TOKEN_ROWS, LANES), F32),
            jax.ShapeDtypeStruct((n, LANES), F32),
        ],
        scratch_shapes=[pltpu.VMEM((tm + 2 * N_META, POOL_WIDTH), F32)] * 3,
        compiler_params=_params("arbitrary"),
        name="mix",
    )(x2, a, u, u, umeta, gates, w_attn, pool_w, pool_scale, w_pool, w_out, g2, wr, br)


_TBL_EA, _TBL_EB, _TBL_NEW_A, _TBL_NEW_B, _TBL_NVALID, _TBL_NUSED = range(6)


def _route_kernel(logit_ref, tri_ref, wts_ref, dest_ref, tbl_ref, carry_sc, ids_sc, *, tme):
    i = pl.program_id(1)
    pl.when(pl.program_id(0) == 0)(functools.partial(_route_classify, i, logit_ref, tri_ref, wts_ref,
                                                     carry_sc, ids_sc))
    pl.when(pl.program_id(0) == 1)(functools.partial(_route_place, i, dest_ref, tbl_ref, carry_sc,
                                                     ids_sc, tme))


def _route_classify(i, logit_ref, tri_ref, wts_ref, carry_sc, ids_sc):
    @pl.when(i == 0)
    def _():
        carry_sc[...] = jnp.zeros_like(carry_sc)

    lg = logit_ref[...].T
    g = [lg[r:r + 1, :] for r in range(N_GROUPS)]
    gmax = functools.reduce(jnp.maximum, g)
    gsel = jnp.full_like(g[0], N_GROUPS - 1).astype(jnp.int32)
    for r in range(N_GROUPS - 2, -1, -1):
        gsel = jnp.where(g[r] == gmax, r, gsel)
    p_group = 1.0 / functools.reduce(lambda a, b: a + b, [jnp.exp(x - gmax) for x in g])
    e = []
    for jj in range(EXPERTS_PER_GROUP):
        v = lg[N_GROUPS + jj:N_GROUPS + jj + 1, :]
        for r in range(1, N_GROUPS):
            row = N_GROUPS + r * EXPERTS_PER_GROUP + jj
            v = jnp.where(gsel == r, lg[row:row + 1, :], v)
        e.append(v)
    v1 = functools.reduce(jnp.maximum, e)
    i1 = jnp.full_like(gsel, EXPERTS_PER_GROUP - 1)
    for jj in range(EXPERTS_PER_GROUP - 2, -1, -1):
        i1 = jnp.where(e[jj] == v1, jj, i1)
    rest = [jnp.where(i1 == jj, -jnp.inf, e[jj]) for jj in range(EXPERTS_PER_GROUP)]
    v2 = functools.reduce(jnp.maximum, rest)
    i2 = jnp.full_like(gsel, EXPERTS_PER_GROUP - 1)
    for jj in range(EXPERTS_PER_GROUP - 2, -1, -1):
        i2 = jnp.where(jnp.logical_and(rest[jj] == v2, i1 != jj), jj, i2)
    t = jnp.exp(v2 - v1)
    w1 = p_group / (1.0 + t)
    w2 = p_group * t / (1.0 + t)
    lo = jnp.minimum(i1, i2)
    hi = jnp.maximum(i1, i2)
    pair = jnp.where(lo == 0, hi - 1, jnp.where(lo == 1, jnp.where(hi == 3, 3, 4), 5))
    bucket = gsel * N_PAIRS + pair
    w_of_lo = jnp.where(i1 < i2, w1, w2)
    w_of_hi = jnp.where(i1 < i2, w2, w1)
    swapped = pair == 5
    w_a = jnp.where(swapped, w_of_hi, w_of_lo)
    w_b = jnp.where(swapped, w_of_lo, w_of_hi)

    tl = lg.shape[1]
    bid = lax.broadcasted_iota(jnp.int32, (N_BUCKET_ROWS, tl), 0)
    hot = bid == bucket
    hot_b = jnp.where(hot, 1.0, 0.0).astype(BF16)
    prefix = jnp.dot(hot_b, tri_ref[...], preferred_element_type=F32)
    carry = carry_sc[...]
    before = prefix + jnp.concatenate([carry] * (tl // LANES), axis=1)
    rank = jnp.sum(jnp.where(hot, before, 0.0), axis=0, keepdims=True)
    carry_sc[...] = carry + jnp.dot(hot_b, jnp.ones((tl, LANES), BF16), preferred_element_type=F32)
    pad = jnp.zeros((6, tl), F32)
    ids_sc[i] = jnp.concatenate([bucket.astype(F32), rank, pad], axis=0)
    wts_ref[...] = jnp.concatenate([w_a, w_b, pad], axis=0)


def _route_place(i, dest_ref, tbl_ref, carry_sc, ids_sc, tme):
    cnt = carry_sc[...]
    tiles = jnp.floor((cnt + (tme - 1)) * (1.0 / tme))
    r = lax.broadcasted_iota(jnp.int32, (N_BUCKET_ROWS, N_BUCKET_ROWS), 0)
    c = lax.broadcasted_iota(jnp.int32, (N_BUCKET_ROWS, N_BUCKET_ROWS), 1)
    tile_end = jnp.dot(jnp.where(c <= r, 1.0, 0.0).astype(BF16), tiles.astype(BF16),
                       preferred_element_type=F32)
    tile_start = tile_end - tiles

    ids = ids_sc[i]
    bucket, rank = ids[0:1, :], ids[1:2, :]
    tl = ids.shape[1]
    bid = lax.broadcasted_iota(jnp.int32, (N_BUCKET_ROWS, tl), 0).astype(F32)
    row_start = jnp.concatenate([tile_start * tme] * (tl // LANES), axis=1)
    dest = rank + jnp.sum(jnp.where(bid == bucket, row_start, 0.0), axis=0, keepdims=True)
    dest_ref[...] = jnp.concatenate([dest, jnp.zeros((7, tl), F32)], axis=0).astype(jnp.int32)

    lane = lax.broadcasted_iota(jnp.int32, (1, LANES), 1).astype(F32)
    n_used = tile_end[N_BUCKET_ROWS - 1:N_BUCKET_ROWS, :]
    last = jnp.minimum(lane, n_used - 1.0)
    tb = jnp.sum(jnp.where(tile_end <= last, 1.0, 0.0), axis=0, keepdims=True)
    tb = jnp.minimum(tb, N_BUCKETS - 1.0)
    group = functools.reduce(lambda a, b: a + b,
                             [jnp.where(tb >= g * N_PAIRS, 1.0, 0.0) for g in range(1, N_GROUPS)])
    pair = tb - N_PAIRS * group

    def lookup(table):
        out = jnp.full_like(pair, float(table[-1]))
        for k in range(len(table) - 2, -1, -1):
            out = jnp.where(pair == k, float(table[k]), out)
        return EXPERTS_PER_GROUP * group + out

    e_a, e_b = lookup(_PAIR_A), lookup(_PAIR_B)
    changed = lambda e: jnp.where(lane == 0.0, 1.0, jnp.where(e != pltpu.roll(e, 1, 1), 1.0, 0.0))
    bid128 = lax.broadcasted_iota(jnp.int32, (N_BUCKET_ROWS, LANES), 0).astype(F32)
    at_tile = lambda col: jnp.sum(jnp.where(bid128 == tb, col, 0.0), axis=0, keepdims=True)
    n_valid = jnp.clip(at_tile(cnt) - tme * (lane - at_tile(tile_start)), 0.0, float(tme))
    rows = [e_a, e_b, changed(e_a), changed(e_b), n_valid, n_used, jnp.zeros_like(lane),
            jnp.zeros_like(lane)]
    tbl_ref[...] = jnp.concatenate(rows, axis=0).astype(jnp.int32)


def _route(logits, tl, tme):
    n = logits.shape[0]
    nt = n // tl
    tri = jnp.asarray(np.triu(np.ones((tl, tl), np.float32), k=1), dtype=BF16)
    first_pass = lambda p, i: i * (1 - p) + (nt - 1) * p
    return pl.pallas_call(
        functools.partial(_route_kernel, tme=tme),
        grid=(2, nt),
        in_specs=[pl.BlockSpec((tl, LANES), lambda p, i: (first_pass(p, i), 0)),
                  pl.BlockSpec((tl, tl), lambda p, i: (0, 0))],
        out_specs=[pl.BlockSpec((8, tl), lambda p, i: (0, first_pass(p, i))),
                   pl.BlockSpec((8, tl), lambda p, i: (0, i * p)),
                   pl.BlockSpec((8, LANES), lambda p, i: (0, 0))],
        out_shape=[jax.ShapeDtypeStruct((8, n), F32),
                   jax.ShapeDtypeStruct((8, n), jnp.int32),
                   jax.ShapeDtypeStruct((8, LANES), jnp.int32)],
        scratch_shapes=[pltpu.VMEM((N_BUCKET_ROWS, LANES), F32), pltpu.VMEM((nt, 8, tl), F32)],
        compiler_params=_params("arbitrary", "arbitrary"),
        name="route",
    )(logits, tri)


def _token_copy(src_hbm, src_token, dst_buf, dst_row, sem):
    return pltpu.make_async_copy(
        src_hbm.at[pl.ds(pl.multiple_of(src_token * TOKEN_ROWS, TOKEN_ROWS), TOKEN_ROWS), :],
        dst_buf.at[pl.ds(pl.multiple_of(dst_row * TOKEN_ROWS, TOKEN_ROWS), TOKEN_ROWS), :],
        sem)


def _start_gather(idx_ref, first, n_groups, src_hbm, dst_buf, sem):
    def body(rr, carry):
        for uu in range(GATHER_UNROLL):
            r = rr * GATHER_UNROLL + uu
            _token_copy(src_hbm, idx_ref[first + r], dst_buf, r, sem).start(priority=uu % 2)
        return carry

    lax.fori_loop(0, n_groups, body, 0)


def _start_padded_gather(idx_ref, first, n_valid, rows, src_hbm, dst_buf, sem):
    n_groups = (n_valid + GATHER_UNROLL - 1) // GATHER_UNROLL
    _start_gather(idx_ref, first, n_groups, src_hbm, dst_buf, sem)
    pad_groups = rows // GATHER_UNROLL - n_groups
    off = n_groups * GATHER_UNROLL
    size = rows // 2
    while size >= GATHER_UNROLL:
        take = jnp.bitwise_and(pad_groups, size // GATHER_UNROLL) != 0

        @pl.when(take)
        def _(off=off, size=size):
            pltpu.make_async_copy(
                src_hbm.at[pl.ds(0, size * TOKEN_ROWS), :],
                dst_buf.at[pl.ds(pl.multiple_of(off * TOKEN_ROWS, GATHER_UNROLL * TOKEN_ROWS),
                                 size * TOKEN_ROWS), :],
                sem).start()

        off = off + jnp.where(take, size, 0)
        size //= 2


def _wait_gather(rows, src_hbm, dst_buf, sem):
    pltpu.make_async_copy(src_hbm.at[pl.ds(0, rows * TOKEN_ROWS), :], dst_buf, sem).wait()


def _moe_kernel(tbl_ref, src_ref, h2c_hbm, wts_ref, g2_ref, wg_a_ref, wu_a_ref, wd_a_ref, wg_b_ref,
                wu_b_ref, wd_b_ref, yc_ref, xbuf, wg_a_sc, wu_a_sc, wd_a_sc, wg_b_sc, wu_b_sc, wd_b_sc,
                sem, *, tme):
    i = pl.program_id(0)
    n_used = tbl_ref[_TBL_NUSED, 0]
    slot = jnp.bitwise_and(i, 1)

    @pl.when(i == 0)
    def _():
        _start_padded_gather(src_ref, 0, tbl_ref[_TBL_NVALID, 0], tme, h2c_hbm, xbuf.at[0], sem.at[0])

    @pl.when(i + 1 < n_used)
    def _():
        _start_padded_gather(src_ref, (i + 1) * tme, tbl_ref[_TBL_NVALID, i + 1], tme, h2c_hbm,
                             xbuf.at[1 - slot], sem.at[1 - slot])

    @pl.when(tbl_ref[_TBL_NEW_A, i] == 1)
    def _():
        for src, dst in ((wg_a_ref, wg_a_sc), (wu_a_ref, wu_a_sc), (wd_a_ref, wd_a_sc)):
            dst[...] = src[0].astype(BF16)

    @pl.when(tbl_ref[_TBL_NEW_B, i] == 1)
    def _():
        for src, dst in ((wg_b_ref, wg_b_sc), (wu_b_ref, wu_b_sc), (wd_b_ref, wd_b_sc)):
            dst[...] = src[0].astype(BF16)

    def tile(rows):
        _wait_gather(tme, h2c_hbm, xbuf.at[slot], sem.at[slot])
        x = _load_token_major(xbuf.at[slot], rows)
        ms = jnp.mean(x * x, axis=-1, keepdims=True)
        hn = (x * lax.rsqrt(ms + EPS) * g2_ref[...]).astype(BF16)
        wts = wts_ref[0:rows, :]

        def expert_act(wg_sc, wu_sc, w):
            gate = jnp.dot(hn, wg_sc[...], preferred_element_type=F32)
            up = jnp.dot(hn, wu_sc[...], preferred_element_type=F32)
            return (gate / (1.0 + jnp.exp(-gate)) * up * w).astype(BF16)

        y = (jnp.dot(expert_act(wg_a_sc, wu_a_sc, wts[:, 0:1]), wd_a_sc[...], preferred_element_type=F32)
             + jnp.dot(expert_act(wg_b_sc, wu_b_sc, wts[:, 1:2]), wd_b_sc[...], preferred_element_type=F32))
        _store_token_major(yc_ref, y, rows)
        if rows < tme:
            yc_ref[rows * TOKEN_ROWS:tme * TOKEN_ROWS, :] = jnp.zeros(((tme - rows) * TOKEN_ROWS, LANES), F32)

    n_valid = tbl_ref[_TBL_NVALID, i]
    pl.when(jnp.logical_and(i < n_used, n_valid > tme // 2))(functools.partial(tile, tme))
    pl.when(jnp.logical_and(i < n_used, n_valid <= tme // 2))(functools.partial(tile, tme // 2))

    @pl.when(i >= n_used)
    def _():
        yc_ref[...] = jnp.zeros_like(yc_ref)


def _moe(tbl, src, h2c, wts_sorted, g2, wg, wu, wd, tme, n_tiles):
    used = lambda i, tbl, src: (jnp.minimum(i, tbl[_TBL_NUSED, 0] - 1), 0)
    w_a = lambda i, tbl, src: (tbl[_TBL_EA, i], 0, 0)
    w_b = lambda i, tbl, src: (tbl[_TBL_EB, i], 0, 0)
    up_shape, down_shape = (D_MODEL, D_EXPERT), (D_EXPERT, D_MODEL)
    grid_spec = pltpu.PrefetchScalarGridSpec(
        num_scalar_prefetch=2,
        grid=(n_tiles,),
        in_specs=[
            pl.BlockSpec(memory_space=pl.ANY),
            pl.BlockSpec((tme, wts_sorted.shape[1]), used),
            pl.BlockSpec((1, D_MODEL), lambda i, tbl, src: (0, 0)),
            pl.BlockSpec((1,) + up_shape, w_a),
            pl.BlockSpec((1,) + up_shape, w_a),
            pl.BlockSpec((1,) + down_shape, w_a),
            pl.BlockSpec((1,) + up_shape, w_b),
            pl.BlockSpec((1,) + up_shape, w_b),
            pl.BlockSpec((1,) + down_shape, w_b),
        ],
        out_specs=pl.BlockSpec((tme * TOKEN_ROWS, LANES), lambda i, tbl, src: (i, 0)),
        scratch_shapes=[pltpu.VMEM((2, tme * TOKEN_ROWS, LANES), F32)]
        + [pltpu.VMEM(s, BF16) for s in (up_shape, up_shape, down_shape) * 2]
        + [pltpu.SemaphoreType.DMA((2,))],
    )
    return pl.pallas_call(
        functools.partial(_moe_kernel, tme=tme),
        grid_spec=grid_spec,
        out_shape=jax.ShapeDtypeStruct((n_tiles * tme * TOKEN_ROWS, LANES), F32),
        compiler_params=_params("arbitrary"),
        name="moe",
    )(tbl, src, h2c, wts_sorted, g2, wg, wu, wd, wg, wu, wd)


def _final_kernel(dest_ref, h2c_ref, yc_hbm, gf_ref, o_ref, ybuf, sem, *, tmf):
    i = pl.program_id(0)
    slot = jnp.bitwise_and(i, 1)

    @pl.when(i == 0)
    def _():
        _start_gather(dest_ref, 0, tmf // GATHER_UNROLL, yc_hbm, ybuf.at[0], sem.at[0])

    @pl.when(i + 1 < pl.num_programs(0))
    def _():
        _start_gather(dest_ref, (i + 1) * tmf, tmf // GATHER_UNROLL, yc_hbm, ybuf.at[1 - slot],
                      sem.at[1 - slot])

    _wait_gather(tmf, yc_hbm, ybuf.at[slot], sem.at[slot])
    h = _load_token_major(h2c_ref, tmf) + _load_token_major(ybuf.at[slot], tmf)
    ms = jnp.mean(h * h, axis=-1, keepdims=True)
    o_ref[...] = h * lax.rsqrt(ms + EPS) * gf_ref[...]


def _final(dest, h2c, yc, gf, tmf):
    n = h2c.shape[0] // TOKEN_ROWS
    grid_spec = pltpu.PrefetchScalarGridSpec(
        num_scalar_prefetch=1,
        grid=(n // tmf,),
        in_specs=[
            pl.BlockSpec((tmf * TOKEN_ROWS, LANES), lambda i, dest: (i, 0)),
            pl.BlockSpec(memory_space=pl.ANY),
            pl.BlockSpec((1, D_MODEL), lambda i, dest: (0, 0)),
        ],
        out_specs=pl.BlockSpec((tmf, D_MODEL), lambda i, dest: (i, 0)),
        scratch_shapes=[pltpu.VMEM((2, tmf * TOKEN_ROWS, LANES), F32), pltpu.SemaphoreType.DMA((2,))],
    )
    return pl.pallas_call(
        functools.partial(_final_kernel, tmf=tmf),
        grid_spec=grid_spec,
        out_shape=jax.ShapeDtypeStruct((n, D_MODEL), F32),
        compiler_params=_params("arbitrary"),
        name="final",
    )(dest, h2c, yc, gf)


def _rope_tables(t):
    inv = 1.0 / (ROPE_THETA ** (np.arange(0, HEAD_DIM, 2, dtype=np.float64) / HEAD_DIM))
    ang = np.arange(t, dtype=np.float64)[:, None] * inv[None, :]
    ang = np.concatenate([ang, ang, ang, ang], axis=-1)
    sign = np.where((np.arange(LANES) % HEAD_DIM) < HEAD_DIM // 2, -1.0, 1.0)
    return (jnp.asarray(np.cos(ang), dtype=F32), jnp.asarray(np.sin(ang) * sign[None, :], dtype=F32))


def kernel(x, meta, norm1_g, w_in, b_gate, lambda_q1, lambda_k1, lambda_q2, lambda_k2, subln_g, pool_w,
           pool_scale, w_attn_br, w_pool_br, w_out, norm2_g, w_router_group, b_router_group,
           w_router_expert, b_router_expert, w_e_gate, w_e_up, w_e_down, final_g):
    batch, seq, d = x.shape
    n = batch * seq
    x2 = x.reshape(n, d)
    cos, sin = _rope_tables(seq + N_META)
    q, k, v, u, gates, qm, km, vm, um = _inproj(x2, meta, norm1_g, cos, sin, w_in[0], b_gate, tm=1024)
    a = _attention(q, k, v, qm, km, vm, lambda_q1, lambda_k1, lambda_q2, lambda_k2, subln_g, batch, seq,
                   bq=256, bk=256)

    n_router = N_GROUPS + N_EXPERTS
    wr = jnp.pad(jnp.concatenate([w_router_group[0], w_router_expert[0]], axis=1),
                 ((0, 0), (0, LANES - n_router)))
    br = jnp.pad(jnp.concatenate([b_router_group[0], b_router_expert[0]]), (0, LANES - n_router))
    h2c, logits = _mix(x2, a, u, um, gates, w_attn_br[0].astype(BF16), pool_w[0].astype(BF16),
                       pool_scale, w_pool_br[0].astype(BF16), w_out[0].astype(BF16), norm2_g,
                       wr, br.reshape(1, LANES), seq, tm=1024)
    tme = 512
    n_tiles = n // tme + N_BUCKETS
    assert n_tiles <= LANES
    wts, dest, tile_table = _route(logits, tl=1024, tme=tme)
    dest = dest[0]
    per_token = jnp.concatenate([wts[0:2], jnp.arange(n, dtype=F32)[None, :]], axis=0).T
    per_row = jnp.zeros((n_tiles * tme, 3), F32).at[dest].set(per_token)
    src = per_row[:, 2].astype(jnp.int32)

    yc = _moe(tile_table, src, h2c, per_row, norm2_g, w_e_gate[0], w_e_up[0], w_e_down[0], tme, n_tiles)
    out = _final(dest, h2c, yc, final_g.reshape(1, d), tmf=1024)
    return out.reshape(batch, seq, d)
```

```python
import functools
import math

import jax
import jax.numpy as jnp
import numpy as np
from jax import lax
from jax.experimental import pallas as pl
from jax.experimental.pallas import tpu as pltpu

D_MODEL = 1024
N_META = 16
N_HEADS = 8
HEAD_DIM = 64
V_DIM = 2 * HEAD_DIM
POOL_WINDOWS = (2, 4, 8, 16)
N_POOL_GROUPS = len(POOL_WINDOWS)
POOL_GROUP_DIM = 128
POOL_WIDTH = N_POOL_GROUPS * POOL_GROUP_DIM
ROPE_THETA = 10000.0
N_GROUPS = 4
EXPERTS_PER_GROUP = 4
N_EXPERTS = N_GROUPS * EXPERTS_PER_GROUP
D_EXPERT = 512
EPS = 1e-6
LAMBDA_INIT = 0.8 - 0.6 * math.exp(-0.3 * 0)
LOG2_E = math.log2(math.e)

_PAIR_A = (0, 0, 0, 1, 1, 3)
_PAIR_B = (1, 2, 3, 3, 2, 2)
N_PAIRS = len(_PAIR_A)
N_BUCKETS = N_GROUPS * N_PAIRS
N_BUCKET_ROWS = 32
GATHER_UNROLL = 8

LANES = 128
NEG_BIG = -1e30
VMEM_LIMIT = 48 * 1024 * 1024

F32 = jnp.float32
BF16 = jnp.bfloat16

_HEADS_PER_STEP = 4
_IN_STEPS = N_HEADS // _HEADS_PER_STEP
_QKV_BLK = _HEADS_PER_STEP * V_DIM
_U_BLK = POOL_WIDTH // _IN_STEPS
_GATE_BLK = 512
_GATES_PER_STEP = 2 * D_MODEL // _IN_STEPS // _GATE_BLK
_O_K = D_MODEL // _QKV_BLK
_O_V = 2 * D_MODEL // _QKV_BLK
_O_U = 3 * D_MODEL // _U_BLK
_O_G = (3 * D_MODEL + POOL_WIDTH) // _GATE_BLK


def _params(*sem):
    return pltpu.CompilerParams(dimension_semantics=sem, vmem_limit_bytes=VMEM_LIMIT)


TOKEN_ROWS = D_MODEL // LANES


def _store_token_major(ref, val, rows):
    for s in range(TOKEN_ROWS):
        ref[pl.ds(s, rows, stride=TOKEN_ROWS), :] = val[:, s * LANES:(s + 1) * LANES]


def _load_token_major(ref, rows):
    return jnp.concatenate([ref[pl.ds(s, rows, stride=TOKEN_ROWS), :] for s in range(TOKEN_ROWS)], axis=1)


def _rope(z, cos, sin_signed, first_half):
    outs = []
    for c in range(z.shape[1] // LANES):
        zc = z[:, c * LANES:(c + 1) * LANES]
        rot = jnp.where(first_half, pltpu.roll(zc, LANES - HEAD_DIM // 2, 1),
                        pltpu.roll(zc, HEAD_DIM // 2, 1))
        outs.append(zc * cos + rot * sin_signed)
    return jnp.concatenate(outs, axis=1)


def _inproj_kernel(x_ref, meta_ref, g_ref, cos_ref, sin_ref, cosm_ref, sinm_ref, wq_ref, wk_ref, wv_ref,
                   wu_ref, wg0_ref, wg1_ref, bg_ref, q_ref, k_ref, v_ref, u_ref, gate_ref, qm_ref, km_ref,
                   vm_ref, um_ref, wq_sc, wk_sc, wv_sc, wu_sc, wg0_sc, wg1_sc):
    def normed(x):
        ms = jnp.mean(x * x, axis=-1, keepdims=True)
        return (x * lax.rsqrt(ms + EPS) * g_ref[...]).astype(BF16)

    def project(hn, cos, sin, q_out, k_out, v_out, u_out, gate_out):
        lane = lax.broadcasted_iota(jnp.int32, cos.shape, 1)
        first_half = jnp.bitwise_and(lane, HEAD_DIM - 1) < (HEAD_DIM // 2)
        if gate_out is not None:
            for blk, wg_sc in enumerate((wg0_sc, wg1_sc)):
                cols = slice(blk * _GATE_BLK, (blk + 1) * _GATE_BLK)
                zg = jnp.dot(hn, wg_sc[...], preferred_element_type=F32) + bg_ref[:, cols]
                gate_out[:, cols] = (1.0 / (1.0 + jnp.exp(-zg))).astype(BF16)
        zq = jnp.dot(hn, wq_sc[...], preferred_element_type=F32)
        q_out[...] = (_rope(zq, cos, sin, first_half) * (LOG2_E / math.sqrt(HEAD_DIM))).astype(BF16)
        zk = jnp.dot(hn, wk_sc[...], preferred_element_type=F32)
        k_out[...] = _rope(zk, cos, sin, first_half).astype(BF16)
        u_out[...] = jnp.dot(hn, wu_sc[...], preferred_element_type=F32)
        v_out[...] = jnp.dot(hn, wv_sc[...], preferred_element_type=F32).astype(BF16)

    @pl.when(pl.program_id(1) == 0)
    def _():
        for src, dst in ((wq_ref, wq_sc), (wk_ref, wk_sc), (wv_ref, wv_sc), (wu_ref, wu_sc),
                         (wg0_ref, wg0_sc), (wg1_ref, wg1_sc)):
            dst[...] = src[...].astype(BF16)
        project(normed(meta_ref[...]), cosm_ref[...], sinm_ref[...], qm_ref, km_ref, vm_ref, um_ref, None)

    project(normed(x_ref[...]), cos_ref[...], sin_ref[...], q_ref, k_ref, v_ref, u_ref, gate_ref)


def _inproj(x2, meta, g1, cos, sin, w_in, b_gate, tm):
    n = x2.shape[0]
    n_meta = meta.shape[0]
    n_pos_blocks = (cos.shape[0] - n_meta) // tm
    row = lambda j, i: (i, 0)
    pos = lambda j, i: (i % n_pos_blocks, 0)
    whole = lambda j, i: (0, 0)
    once = pl.Buffered(1)
    gate_cols = _GATES_PER_STEP * _GATE_BLK
    assert _GATES_PER_STEP == 2
    return pl.pallas_call(
        _inproj_kernel,
        grid=(_IN_STEPS, n // tm),
        in_specs=[
            pl.BlockSpec((tm, D_MODEL), row),
            pl.BlockSpec((n_meta, D_MODEL), whole),
            pl.BlockSpec((1, D_MODEL), whole),
            pl.BlockSpec((tm, LANES), pos),
            pl.BlockSpec((tm, LANES), pos),
            pl.BlockSpec((n_meta, LANES), whole),
            pl.BlockSpec((n_meta, LANES), whole),
            pl.BlockSpec((D_MODEL, _QKV_BLK), lambda j, i: (0, j), pipeline_mode=once),
            pl.BlockSpec((D_MODEL, _QKV_BLK), lambda j, i: (0, _O_K + j), pipeline_mode=once),
            pl.BlockSpec((D_MODEL, _QKV_BLK), lambda j, i: (0, _O_V + j), pipeline_mode=once),
            pl.BlockSpec((D_MODEL, _U_BLK), lambda j, i: (0, _O_U + j), pipeline_mode=once),
            pl.BlockSpec((D_MODEL, _GATE_BLK), lambda j, i: (0, _O_G + _GATES_PER_STEP * j),
                         pipeline_mode=once),
            pl.BlockSpec((D_MODEL, _GATE_BLK), lambda j, i: (0, _O_G + _GATES_PER_STEP * j + 1),
                         pipeline_mode=once),
            pl.BlockSpec((1, gate_cols), lambda j, i: (0, j)),
        ],
        out_specs=[
            pl.BlockSpec((tm, _QKV_BLK), lambda j, i: (i, j)),
            pl.BlockSpec((tm, _QKV_BLK), lambda j, i: (i, j)),
            pl.BlockSpec((tm, _QKV_BLK), lambda j, i: (i, j)),
            pl.BlockSpec((tm, _U_BLK), lambda j, i: (i, j)),
            pl.BlockSpec((tm, gate_cols), lambda j, i: (i, j)),
            pl.BlockSpec((n_meta, _QKV_BLK), lambda j, i: (0, j)),
            pl.BlockSpec((n_meta, _QKV_BLK), lambda j, i: (0, j)),
            pl.BlockSpec((n_meta, _QKV_BLK), lambda j, i: (0, j)),
            pl.BlockSpec((n_meta, _U_BLK), lambda j, i: (0, j)),
        ],
        out_shape=[
            jax.ShapeDtypeStruct((n, D_MODEL), BF16),
            jax.ShapeDtypeStruct((n, D_MODEL), BF16),
            jax.ShapeDtypeStruct((n, D_MODEL), BF16),
            jax.ShapeDtypeStruct((n, POOL_WIDTH), F32),
            jax.ShapeDtypeStruct((n, 2 * D_MODEL), BF16),
            jax.ShapeDtypeStruct((n_meta, D_MODEL), BF16),
            jax.ShapeDtypeStruct((n_meta, D_MODEL), BF16),
            jax.ShapeDtypeStruct((n_meta, D_MODEL), BF16),
            jax.ShapeDtypeStruct((n_meta, POOL_WIDTH), F32),
        ],
        scratch_shapes=[
            pltpu.VMEM((D_MODEL, _QKV_BLK), BF16),
            pltpu.VMEM((D_MODEL, _QKV_BLK), BF16),
            pltpu.VMEM((D_MODEL, _QKV_BLK), BF16),
            pltpu.VMEM((D_MODEL, _U_BLK), BF16),
            pltpu.VMEM((D_MODEL, _GATE_BLK), BF16),
            pltpu.VMEM((D_MODEL, _GATE_BLK), BF16),
        ],
        compiler_params=_params("arbitrary", "arbitrary"),
        name="inproj",
    )(x2, meta, g1, cos[n_meta:], sin[n_meta:], cos[:n_meta], sin[:n_meta], w_in, w_in, w_in, w_in, w_in,
      w_in, b_gate)


def _attn_kernel(lq1_ref, lk1_ref, lq2_ref, lk2_ref, subg_ref, q_ref, k_ref, v_ref, qm_ref, km_ref,
                 vm_ref, o_ref, *state_sc, bq, bk):
    seq = k_ref.shape[0]
    nq = seq // bq
    states = [tuple(state_sc[3 * g:3 * g + 3]) for g in range(nq + 1)]

    def rows_at(ref, meta_ref, p0, n):
        if p0 == 0:
            return jnp.concatenate([meta_ref[...], ref[0:n - N_META, :]], axis=0)
        real = ref[p0 - N_META:min(p0 - N_META + n, seq), :]
        if real.shape[0] < n:
            real = jnp.concatenate([real, jnp.zeros((n - real.shape[0], real.shape[1]), real.dtype)], axis=0)
        return real

    def block_rows(i):
        return (i * bq, bq) if i < nq else (nq * bq, N_META)

    def n_full_tiles(i):
        return (block_rows(i)[0] + 1) // bk

    def diag_tile(i):
        p0 = n_full_tiles(i) * bk
        return p0, (bk if i < nq else LANES), block_rows(i)[0] - p0

    lam = (jnp.exp(jnp.sum(lq1_ref[...] * lk1_ref[...], axis=-1, keepdims=True))
           - jnp.exp(jnp.sum(lq2_ref[...] * lk2_ref[...], axis=-1, keepdims=True)) + LAMBDA_INIT)

    bias_cache = {}

    def causal_bias(n_rows, n_cols, offset):
        if (n_rows, n_cols, offset) not in bias_cache:
            r = lax.broadcasted_iota(jnp.int32, (n_rows, n_cols), 0)
            c = lax.broadcasted_iota(jnp.int32, (n_rows, n_cols), 1)
            bias = jnp.where(c <= r + offset, 0.0, NEG_BIG)
            bias_cache[(n_rows, n_cols, offset)] = jnp.concatenate([bias, bias], axis=0)
        return bias_cache[(n_rows, n_cols, offset)]

    def load_q(i, qq_sc):
        p0, n = block_rows(i)
        q = rows_at(q_ref, qm_ref, p0, n)
        lane = lax.broadcasted_iota(jnp.int32, q.shape, 1)
        zero = jnp.zeros_like(q)
        qq_sc[0:n, :] = jnp.where(lane < HEAD_DIM, q, zero)
        qq_sc[n:2 * n, :] = jnp.where(lane >= HEAD_DIM, q, zero)

    def step(state, kb, vb, bias):
        qq_sc, m_sc, acc_sc = state
        first = bias is not None
        s = lax.dot_general(qq_sc[...], kb, (((1,), (1,)), ((), ())), preferred_element_type=F32)
        if first:
            s = s + bias
        n_tiles = s.shape[1] // LANES
        rm = s[:, 0:LANES]
        for t in range(1, n_tiles):
            rm = jnp.maximum(rm, s[:, t * LANES:(t + 1) * LANES])
        rmax = jnp.max(rm, axis=-1, keepdims=True)
        if first:
            m_new = jnp.broadcast_to(rmax, (s.shape[0], LANES))
        else:
            m_prev = m_sc[...]
            m_new = jnp.maximum(m_prev, rmax)
        p = jnp.exp2(s - jnp.concatenate([m_new] * n_tiles, axis=1))
        pv = jnp.dot(p.astype(BF16), vb, preferred_element_type=F32)
        if first:
            acc_sc[...] = pv
        else:
            alpha = jnp.exp2(m_prev - m_new)
            acc_sc[...] = jnp.concatenate([alpha, alpha], axis=1) * acc_sc[...] + pv
        m_sc[...] = m_new

    def key_step(state, i, j):
        if j == 0:
            p0, width, offset = diag_tile(i)
            bias = causal_bias(block_rows(i)[1], width, offset)
        else:
            p0, width, bias = (j - 1) * bk, bk, None
        kb = rows_at(k_ref, km_ref, p0, width)
        vb = jnp.concatenate([rows_at(v_ref, vm_ref, p0, width), jnp.ones((width, V_DIM), BF16)], axis=1)
        step(state, kb, vb, bias)

    def finish(i, acc_sc):
        p0, n = block_rows(i)
        o1 = acc_sc[0:n, 0:V_DIM] / acc_sc[0:n, V_DIM:2 * V_DIM]
        o2 = acc_sc[n:2 * n, 0:V_DIM] / acc_sc[n:2 * n, V_DIM:2 * V_DIM]
        o = o1 - lam * o2
        ms = jnp.mean(o * o, axis=-1, keepdims=True)
        o = (o * lax.rsqrt(ms + EPS) * subg_ref[...] * (1.0 - LAMBDA_INIT)).astype(BF16)
        if p0 == 0:
            o_ref[0:n - N_META, :] = o[N_META:n, :]
        else:
            o_ref[p0 - N_META:p0 - N_META + n, :] = o

    order = list(range(nq, -1, -1))
    for i in order:
        load_q(i, states[i][0])
    for j in range(n_full_tiles(nq) + 1):
        for i in order:
            if j <= n_full_tiles(i):
                key_step(states[i], i, j)
    for i in order:
        finish(i, states[i][2])


def _attention(q, k, v, qm, km, vm, lq1, lk1, lq2, lk2, subg, batch, seq, bq, bk):
    nq = seq // bq
    small = lambda b, h: (0, 0)
    head = lambda b, h: (b, h)
    meta = lambda b, h: (0, h)

    def state_scratch(rows):
        return [pltpu.VMEM((2 * rows, V_DIM), BF16), pltpu.VMEM((2 * rows, LANES), F32),
                pltpu.VMEM((2 * rows, 2 * V_DIM), F32)]

    return pl.pallas_call(
        functools.partial(_attn_kernel, bq=bq, bk=bk),
        grid=(batch, N_HEADS),
        in_specs=[
            pl.BlockSpec((1, HEAD_DIM), small),
            pl.BlockSpec((1, HEAD_DIM), small),
            pl.BlockSpec((1, HEAD_DIM), small),
            pl.BlockSpec((1, HEAD_DIM), small),
            pl.BlockSpec((1, V_DIM), small),
            pl.BlockSpec((seq, V_DIM), head),
            pl.BlockSpec((seq, V_DIM), head),
            pl.BlockSpec((seq, V_DIM), head),
            pl.BlockSpec((N_META, V_DIM), meta),
            pl.BlockSpec((N_META, V_DIM), meta),
            pl.BlockSpec((N_META, V_DIM), meta),
        ],
        out_specs=pl.BlockSpec((seq, V_DIM), head),
        out_shape=jax.ShapeDtypeStruct(q.shape, BF16),
        scratch_shapes=state_scratch(bq) * nq + state_scratch(N_META),
        compiler_params=_params("arbitrary", "arbitrary"),
        name="diff_attn",
    )(lq1, lk1, lq2, lk2, subg, q, k, v, qm, km, vm)


def _mix_kernel(x_ref, a_ref, u_ref, uprev_ref, umeta_ref, gate_ref, wa_ref, pw_ref, ps_ref, wp_ref,
                wo_ref, g2_ref, wr_ref, br_ref, h2c_ref, logit_ref, ext_sc, sum_a_sc, sum_b_sc, wfold_sc, *, tm,
                tiles_per_seq):
    i = pl.program_id(0)
    first = (i % tiles_per_seq) == 0

    @pl.when(i == 0)
    def _():
        for g in range(N_POOL_GROUPS):
            rows = slice(g * POOL_GROUP_DIM, (g + 1) * POOL_GROUP_DIM)
            left = pw_ref[g] * ps_ref[:, rows]
            right = wp_ref[rows, :]
            left_hi, right_hi = left.astype(BF16), right.astype(BF16)
            left_lo = (left - left_hi.astype(F32)).astype(BF16)
            right_lo = (right - right_hi.astype(F32)).astype(BF16)
            wfold_sc[rows, :] = (jnp.dot(left_hi, right_hi, preferred_element_type=F32)
                                 + jnp.dot(left_lo, right_hi, preferred_element_type=F32)
                                 + jnp.dot(left_hi, right_lo, preferred_element_type=F32)).astype(BF16)

    y_attn = jnp.dot(a_ref[...], wa_ref[...], preferred_element_type=F32)

    top = 2 * N_META
    for ref in (ext_sc, sum_a_sc, sum_b_sc):
        ref[0:N_META, :] = jnp.zeros((N_META, ref.shape[1]), F32)

    @pl.when(first)
    def _():
        ext_sc[N_META:top, :] = umeta_ref[...]

    @pl.when(jnp.logical_not(first))
    def _():
        ext_sc[N_META:top, :] = uprev_ref[...]

    ext_sc[top:top + tm, :] = u_ref[...]

    pooled = []
    for g, w in enumerate(POOL_WINDOWS):
        cs = slice(g * POOL_GROUP_DIM, (g + 1) * POOL_GROUP_DIM)
        src, lanes, shift = ext_sc, cs, 1
        for dst in (sum_a_sc, sum_b_sc, sum_a_sc, sum_b_sc):
            if shift >= w:
                break
            dst[N_META:top + tm, :] = (src[N_META:top + tm, lanes]
                                       + src[N_META - shift:top + tm - shift, lanes])
            src, lanes, shift = dst, slice(None), 2 * shift
        pooled.append((src[top:top + tm, lanes] * (1.0 / w) - ext_sc[top:top + tm, cs]).astype(BF16))
    y_pool = jnp.dot(jnp.concatenate(pooled, axis=1), wfold_sc[...], preferred_element_type=F32)
    mixed = (gate_ref[:, 0:D_MODEL].astype(F32) * y_attn
             + gate_ref[:, D_MODEL:2 * D_MODEL].astype(F32) * y_pool)
    h2 = x_ref[...] + jnp.dot(mixed.astype(BF16), wo_ref[...], preferred_element_type=F32)
    _store_token_major(h2c_ref, h2, tm)
    ms = jnp.mean(h2 * h2, axis=-1, keepdims=True)
    hn2 = h2 * lax.rsqrt(ms + EPS) * g2_ref[...]
    hn2_hi = hn2.astype(BF16)
    hn2_lo = (hn2 - hn2_hi.astype(F32)).astype(BF16)
    wr = wr_ref[...]
    wr_hi = wr.astype(BF16)
    wr_lo = (wr - wr_hi.astype(F32)).astype(BF16)
    wr_both = jnp.concatenate([wr_hi, wr_lo], axis=1)
    for half in range(2):
        rows = slice(half * tm // 2, (half + 1) * tm // 2)
        both = jnp.dot(hn2_hi[rows, :], wr_both, preferred_element_type=F32)
        logit_ref[rows, :] = (both[:, 0:LANES] + both[:, LANES:2 * LANES]
                              + jnp.dot(hn2_lo[rows, :], wr_hi, preferred_element_type=F32) + br_ref[...])


def _mix(x2, a, u, umeta, gates, w_attn, pool_w, pool_scale, w_pool, w_out, g2, wr, br, seq, tm):
    n = x2.shape[0]
    tiles_per_seq = seq // tm
    halo_blocks = tm // N_META
    const2 = lambda i: (0, 0)
    return pl.pallas_call(
        functools.partial(_mix_kernel, tm=tm, tiles_per_seq=tiles_per_seq),
        grid=(n // tm,),
        in_specs=[
            pl.BlockSpec((tm, D_MODEL), lambda i: (i, 0)),
            pl.BlockSpec((tm, D_MODEL), lambda i: (i, 0)),
            pl.BlockSpec((tm, POOL_WIDTH), lambda i: (i, 0)),
            pl.BlockSpec((N_META, POOL_WIDTH), lambda i: (jnp.maximum(i * halo_blocks - 1, 0), 0)),
            pl.BlockSpec((N_META, POOL_WIDTH), const2),
            pl.BlockSpec((tm, 2 * D_MODEL), lambda i: (i, 0)),
            pl.BlockSpec((D_MODEL, D_MODEL), const2, pipeline_mode=pl.Buffered(1)),
            pl.BlockSpec((N_POOL_GROUPS, POOL_GROUP_DIM, POOL_GROUP_DIM), lambda i: (0, 0, 0),
                         pipeline_mode=pl.Buffered(1)),
            pl.BlockSpec((1, POOL_WIDTH), const2),
            pl.BlockSpec((POOL_WIDTH, D_MODEL), const2, pipeline_mode=pl.Buffered(1)),
            pl.BlockSpec((D_MODEL, D_MODEL), const2, pipeline_mode=pl.Buffered(1)),
            pl.BlockSpec((1, D_MODEL), const2),
            pl.BlockSpec((D_MODEL, LANES), const2, pipeline_mode=pl.Buffered(1)),
            pl.BlockSpec((1, LANES), const2),
        ],
        out_specs=[
            pl.BlockSpec((tm * TOKEN_ROWS, LANES), lambda i: (i, 0)),
            pl.BlockSpec((tm, LANES), lambda i: (i, 0)),
        ],
        out_shape=[
            jax.ShapeDtypeStruct((n * TOKEN_ROWS, LANES), F32),
            jax.ShapeDtypeStruct((n, LANES), F32),
        ],
        scratch_shapes=[pltpu.VMEM((tm + 2 * N_META, POOL_WIDTH), F32),
                        pltpu.VMEM((tm + 2 * N_META, POOL_GROUP_DIM), F32),
                        pltpu.VMEM((tm + 2 * N_META, POOL_GROUP_DIM), F32),
                        pltpu.VMEM((POOL_WIDTH, D_MODEL), BF16)],
        compiler_params=_params("arbitrary"),
        name="mix",
    )(x2, a, u, u, umeta, gates, w_attn, pool_w, pool_scale, w_pool, w_out, g2, wr, br)


_TBL_EA, _TBL_EB, _TBL_NEW_A, _TBL_NEW_B, _TBL_NVALID, _TBL_NUSED = range(6)


def _route_kernel(logit_ref, tri_ref, wts_ref, dest_ref, tbl_ref, carry_sc, ids_sc, *, tme):
    i = pl.program_id(1)
    pl.when(pl.program_id(0) == 0)(functools.partial(_route_classify, i, logit_ref, tri_ref, wts_ref,
                                                     carry_sc, ids_sc))
    pl.when(pl.program_id(0) == 1)(functools.partial(_route_place, i, dest_ref, tbl_ref, carry_sc,
                                                     ids_sc, tme))


def _route_classify(i, logit_ref, tri_ref, wts_ref, carry_sc, ids_sc):
    @pl.when(i == 0)
    def _():
        carry_sc[...] = jnp.zeros_like(carry_sc)

    lg = logit_ref[...].T
    g = [lg[r:r + 1, :] for r in range(N_GROUPS)]
    gmax = functools.reduce(jnp.maximum, g)
    gsel = jnp.full_like(g[0], N_GROUPS - 1).astype(jnp.int32)
    for r in range(N_GROUPS - 2, -1, -1):
        gsel = jnp.where(g[r] == gmax, r, gsel)
    p_group = 1.0 / functools.reduce(lambda a, b: a + b, [jnp.exp(x - gmax) for x in g])
    e = []
    for jj in range(EXPERTS_PER_GROUP):
        v = lg[N_GROUPS + jj:N_GROUPS + jj + 1, :]
        for r in range(1, N_GROUPS):
            row = N_GROUPS + r * EXPERTS_PER_GROUP + jj
            v = jnp.where(gsel == r, lg[row:row + 1, :], v)
        e.append(v)
    v1 = functools.reduce(jnp.maximum, e)
    i1 = jnp.full_like(gsel, EXPERTS_PER_GROUP - 1)
    for jj in range(EXPERTS_PER_GROUP - 2, -1, -1):
        i1 = jnp.where(e[jj] == v1, jj, i1)
    rest = [jnp.where(i1 == jj, -jnp.inf, e[jj]) for jj in range(EXPERTS_PER_GROUP)]
    v2 = functools.reduce(jnp.maximum, rest)
    i2 = jnp.full_like(gsel, EXPERTS_PER_GROUP - 1)
    for jj in range(EXPERTS_PER_GROUP - 2, -1, -1):
        i2 = jnp.where(jnp.logical_and(rest[jj] == v2, i1 != jj), jj, i2)
    t = jnp.exp(v2 - v1)
    w1 = p_group / (1.0 + t)
    w2 = p_group * t / (1.0 + t)
    lo = jnp.minimum(i1, i2)
    hi = jnp.maximum(i1, i2)
    pair = jnp.where(lo == 0, hi - 1, jnp.where(lo == 1, jnp.where(hi == 3, 3, 4), 5))
    bucket = gsel * N_PAIRS + pair
    w_of_lo = jnp.where(i1 < i2, w1, w2)
    w_of_hi = jnp.where(i1 < i2, w2, w1)
    swapped = pair == 5
    w_a = jnp.where(swapped, w_of_hi, w_of_lo)
    w_b = jnp.where(swapped, w_of_lo, w_of_hi)

    tl = lg.shape[1]
    bid = lax.broadcasted_iota(jnp.int32, (N_BUCKET_ROWS, tl), 0)
    hot = bid == bucket
    hot_b = jnp.where(hot, 1.0, 0.0).astype(BF16)
    prefix = jnp.dot(hot_b, tri_ref[...], preferred_element_type=F32)
    carry = carry_sc[...]
    before = prefix + jnp.concatenate([carry] * (tl // LANES), axis=1)
    rank = jnp.sum(jnp.where(hot, before, 0.0), axis=0, keepdims=True)
    carry_sc[...] = carry + jnp.dot(hot_b, jnp.ones((tl, LANES), BF16), preferred_element_type=F32)
    pad = jnp.zeros((6, tl), F32)
    ids_sc[i] = jnp.concatenate([bucket.astype(F32), rank, pad], axis=0)
    wts_ref[...] = jnp.concatenate([w_a, w_b, pad], axis=0)


def _route_place(i, dest_ref, tbl_ref, carry_sc, ids_sc, tme):
    cnt = carry_sc[...]
    tiles = jnp.floor((cnt + (tme - 1)) * (1.0 / tme))
    r = lax.broadcasted_iota(jnp.int32, (N_BUCKET_ROWS, N_BUCKET_ROWS), 0)
    c = lax.broadcasted_iota(jnp.int32, (N_BUCKET_ROWS, N_BUCKET_ROWS), 1)
    tile_end = jnp.dot(jnp.where(c <= r, 1.0, 0.0).astype(BF16), tiles.astype(BF16),
                       preferred_element_type=F32)
    tile_start = tile_end - tiles

    ids = ids_sc[i]
    bucket, rank = ids[0:1, :], ids[1:2, :]
    tl = ids.shape[1]
    bid = lax.broadcasted_iota(jnp.int32, (N_BUCKET_ROWS, tl), 0).astype(F32)
    row_start = jnp.concatenate([tile_start * tme] * (tl // LANES), axis=1)
    dest = rank + jnp.sum(jnp.where(bid == bucket, row_start, 0.0), axis=0, keepdims=True)
    dest_ref[...] = jnp.concatenate([dest, jnp.zeros((7, tl), F32)], axis=0).astype(jnp.int32)

    lane = lax.broadcasted_iota(jnp.int32, (1, LANES), 1).astype(F32)
    n_used = tile_end[N_BUCKET_ROWS - 1:N_BUCKET_ROWS, :]
    last = jnp.minimum(lane, n_used - 1.0)
    tb = jnp.sum(jnp.where(tile_end <= last, 1.0, 0.0), axis=0, keepdims=True)
    tb = jnp.minimum(tb, N_BUCKETS - 1.0)
    group = functools.reduce(lambda a, b: a + b,
                             [jnp.where(tb >= g * N_PAIRS, 1.0, 0.0) for g in range(1, N_GROUPS)])
    pair = tb - N_PAIRS * group

    def lookup(table):
        out = jnp.full_like(pair, float(table[-1]))
        for k in range(len(table) - 2, -1, -1):
            out = jnp.where(pair == k, float(table[k]), out)
        return EXPERTS_PER_GROUP * group + out

    e_a, e_b = lookup(_PAIR_A), lookup(_PAIR_B)
    changed = lambda e: jnp.where(lane == 0.0, 1.0, jnp.where(e != pltpu.roll(e, 1, 1), 1.0, 0.0))
    bid128 = lax.broadcasted_iota(jnp.int32, (N_BUCKET_ROWS, LANES), 0).astype(F32)
    at_tile = lambda col: jnp.sum(jnp.where(bid128 == tb, col, 0.0), axis=0, keepdims=True)
    n_valid = jnp.clip(at_tile(cnt) - tme * (lane - at_tile(tile_start)), 0.0, float(tme))
    rows = [e_a, e_b, changed(e_a), changed(e_b), n_valid, n_used, jnp.zeros_like(lane),
            jnp.zeros_like(lane)]
    tbl_ref[...] = jnp.concatenate(rows, axis=0).astype(jnp.int32)


def _route(logits, tl, tme):
    n = logits.shape[0]
    nt = n // tl
    tri = jnp.asarray(np.triu(np.ones((tl, tl), np.float32), k=1), dtype=BF16)
    first_pass = lambda p, i: i * (1 - p) + (nt - 1) * p
    return pl.pallas_call(
        functools.partial(_route_kernel, tme=tme),
        grid=(2, nt),
        in_specs=[pl.BlockSpec((tl, LANES), lambda p, i: (first_pass(p, i), 0)),
                  pl.BlockSpec((tl, tl), lambda p, i: (0, 0))],
        out_specs=[pl.BlockSpec((8, tl), lambda p, i: (0, first_pass(p, i))),
                   pl.BlockSpec((8, tl), lambda p, i: (0, i * p)),
                   pl.BlockSpec((8, LANES), lambda p, i: (0, 0))],
        out_shape=[jax.ShapeDtypeStruct((8, n), F32),
                   jax.ShapeDtypeStruct((8, n), jnp.int32),
                   jax.ShapeDtypeStruct((8, LANES), jnp.int32)],
        scratch_shapes=[pltpu.VMEM((N_BUCKET_ROWS, LANES), F32), pltpu.VMEM((nt, 8, tl), F32)],
        compiler_params=_params("arbitrary", "arbitrary"),
        name="route",
    )(logits, tri)


def _token_copy(src_hbm, src_token, dst_buf, dst_row, sem):
    return pltpu.make_async_copy(
        src_hbm.at[pl.ds(pl.multiple_of(src_token * TOKEN_ROWS, TOKEN_ROWS), TOKEN_ROWS), :],
        dst_buf.at[pl.ds(pl.multiple_of(dst_row * TOKEN_ROWS, TOKEN_ROWS), TOKEN_ROWS), :],
        sem)


def _start_gather(idx_ref, first, n_groups, src_hbm, dst_buf, sem):
    def body(rr, carry):
        for uu in range(GATHER_UNROLL):
            r = rr * GATHER_UNROLL + uu
            _token_copy(src_hbm, idx_ref[first + r], dst_buf, r, sem).start(priority=uu % 2)
        return carry

    lax.fori_loop(0, n_groups, body, 0)


def _start_padded_gather(idx_ref, first, n_valid, rows, src_hbm, dst_buf, sem):
    n_groups = (n_valid + GATHER_UNROLL - 1) // GATHER_UNROLL
    _start_gather(idx_ref, first, n_groups, src_hbm, dst_buf, sem)
    pad_groups = rows // GATHER_UNROLL - n_groups
    off = n_groups * GATHER_UNROLL
    size = rows // 2
    while size >= GATHER_UNROLL:
        take = jnp.bitwise_and(pad_groups, size // GATHER_UNROLL) != 0

        @pl.when(take)
        def _(off=off, size=size):
            pltpu.make_async_copy(
                src_hbm.at[pl.ds(0, size * TOKEN_ROWS), :],
                dst_buf.at[pl.ds(pl.multiple_of(off * TOKEN_ROWS, GATHER_UNROLL * TOKEN_ROWS),
                                 size * TOKEN_ROWS), :],
                sem).start()

        off = off + jnp.where(take, size, 0)
        size //= 2


def _wait_gather(rows, src_hbm, dst_buf, sem):
    pltpu.make_async_copy(src_hbm.at[pl.ds(0, rows * TOKEN_ROWS), :], dst_buf, sem).wait()


def _moe_kernel(tbl_ref, src_ref, h2c_hbm, wts_ref, g2_ref, wg_a_ref, wu_a_ref, wd_a_ref, wg_b_ref,
                wu_b_ref, wd_b_ref, yc_ref, xbuf, wg_a_sc, wu_a_sc, wd_a_sc, wg_b_sc, wu_b_sc, wd_b_sc,
                sem, *, tme):
    i = pl.program_id(0)
    n_used = tbl_ref[_TBL_NUSED, 0]
    slot = jnp.bitwise_and(i, 1)

    @pl.when(i == 0)
    def _():
        _start_padded_gather(src_ref, 0, tbl_ref[_TBL_NVALID, 0], tme, h2c_hbm, xbuf.at[0], sem.at[0])

    @pl.when(i + 1 < n_used)
    def _():
        _start_padded_gather(src_ref, (i + 1) * tme, tbl_ref[_TBL_NVALID, i + 1], tme, h2c_hbm,
                             xbuf.at[1 - slot], sem.at[1 - slot])

    @pl.when(tbl_ref[_TBL_NEW_A, i] == 1)
    def _():
        for src, dst in ((wg_a_ref, wg_a_sc), (wu_a_ref, wu_a_sc), (wd_a_ref, wd_a_sc)):
            dst[...] = src[0].astype(BF16)

    @pl.when(tbl_ref[_TBL_NEW_B, i] == 1)
    def _():
        for src, dst in ((wg_b_ref, wg_b_sc), (wu_b_ref, wu_b_sc), (wd_b_ref, wd_b_sc)):
            dst[...] = src[0].astype(BF16)

    def tile(rows):
        _wait_gather(tme, h2c_hbm, xbuf.at[slot], sem.at[slot])
        x = _load_token_major(xbuf.at[slot], rows)
        ms = jnp.mean(x * x, axis=-1, keepdims=True)
        hn = (x * lax.rsqrt(ms + EPS) * g2_ref[...]).astype(BF16)
        wts = wts_ref[0:rows, :]

        def expert_act(wg_sc, wu_sc, w):
            gate = jnp.dot(hn, wg_sc[...], preferred_element_type=F32)
            up = jnp.dot(hn, wu_sc[...], preferred_element_type=F32)
            return (gate / (1.0 + jnp.exp(-gate)) * up * w).astype(BF16)

        y = (jnp.dot(expert_act(wg_a_sc, wu_a_sc, wts[:, 0:1]), wd_a_sc[...], preferred_element_type=F32)
             + jnp.dot(expert_act(wg_b_sc, wu_b_sc, wts[:, 1:2]), wd_b_sc[...], preferred_element_type=F32))
        _store_token_major(yc_ref, y, rows)
        if rows < tme:
            yc_ref[rows * TOKEN_ROWS:tme * TOKEN_ROWS, :] = jnp.zeros(((tme - rows) * TOKEN_ROWS, LANES), F32)

    n_valid = tbl_ref[_TBL_NVALID, i]
    pl.when(jnp.logical_and(i < n_used, n_valid > tme // 2))(functools.partial(tile, tme))
    pl.when(jnp.logical_and(i < n_used, n_valid <= tme // 2))(functools.partial(tile, tme // 2))

    @pl.when(i >= n_used)
    def _():
        yc_ref[...] = jnp.zeros_like(yc_ref)


def _moe(tbl, src, h2c, wts_sorted, g2, wg, wu, wd, tme, n_tiles):
    used = lambda i, tbl, src: (jnp.minimum(i, tbl[_TBL_NUSED, 0] - 1), 0)
    w_a = lambda i, tbl, src: (tbl[_TBL_EA, i], 0, 0)
    w_b = lambda i, tbl, src: (tbl[_TBL_EB, i], 0, 0)
    up_shape, down_shape = (D_MODEL, D_EXPERT), (D_EXPERT, D_MODEL)
    grid_spec = pltpu.PrefetchScalarGridSpec(
        num_scalar_prefetch=2,
        grid=(n_tiles,),
        in_specs=[
            pl.BlockSpec(memory_space=pl.ANY),
            pl.BlockSpec((tme, wts_sorted.shape[1]), used),
            pl.BlockSpec((1, D_MODEL), lambda i, tbl, src: (0, 0)),
            pl.BlockSpec((1,) + up_shape, w_a),
            pl.BlockSpec((1,) + up_shape, w_a),
            pl.BlockSpec((1,) + down_shape, w_a),
            pl.BlockSpec((1,) + up_shape, w_b),
            pl.BlockSpec((1,) + up_shape, w_b),
            pl.BlockSpec((1,) + down_shape, w_b),
        ],
        out_specs=pl.BlockSpec((tme * TOKEN_ROWS, LANES), lambda i, tbl, src: (i, 0)),
        scratch_shapes=[pltpu.VMEM((2, tme * TOKEN_ROWS, LANES), F32)]
        + [pltpu.VMEM(s, BF16) for s in (up_shape, up_shape, down_shape) * 2]
        + [pltpu.SemaphoreType.DMA((2,))],
    )
    return pl.pallas_call(
        functools.partial(_moe_kernel, tme=tme),
        grid_spec=grid_spec,
        out_shape=jax.ShapeDtypeStruct((n_tiles * tme * TOKEN_ROWS, LANES), F32),
        compiler_params=_params("arbitrary"),
        name="moe",
    )(tbl, src, h2c, wts_sorted, g2, wg, wu, wd, wg, wu, wd)


def _final_kernel(dest_ref, h2c_ref, yc_hbm, gf_ref, o_ref, ybuf, sem, *, tmf):
    i = pl.program_id(0)
    slot = jnp.bitwise_and(i, 1)

    @pl.when(i == 0)
    def _():
        _start_gather(dest_ref, 0, tmf // GATHER_UNROLL, yc_hbm, ybuf.at[0], sem.at[0])

    @pl.when(i + 1 < pl.num_programs(0))
    def _():
        _start_gather(dest_ref, (i + 1) * tmf, tmf // GATHER_UNROLL, yc_hbm, ybuf.at[1 - slot],
                      sem.at[1 - slot])

    _wait_gather(tmf, yc_hbm, ybuf.at[slot], sem.at[slot])
    h = _load_token_major(h2c_ref, tmf) + _load_token_major(ybuf.at[slot], tmf)
    ms = jnp.mean(h * h, axis=-1, keepdims=True)
    o_ref[...] = h * lax.rsqrt(ms + EPS) * gf_ref[...]


def _final(dest, h2c, yc, gf, tmf):
    n = h2c.shape[0] // TOKEN_ROWS
    grid_spec = pltpu.PrefetchScalarGridSpec(
        num_scalar_prefetch=1,
        grid=(n // tmf,),
        in_specs=[
            pl.BlockSpec((tmf * TOKEN_ROWS, LANES), lambda i, dest: (i, 0)),
            pl.BlockSpec(memory_space=pl.ANY),
            pl.BlockSpec((1, D_MODEL), lambda i, dest: (0, 0)),
        ],
        out_specs=pl.BlockSpec((tmf, D_MODEL), lambda i, dest: (i, 0)),
        scratch_shapes=[pltpu.VMEM((2, tmf * TOKEN_ROWS, LANES), F32), pltpu.SemaphoreType.DMA((2,))],
    )
    return pl.pallas_call(
        functools.partial(_final_kernel, tmf=tmf),
        grid_spec=grid_spec,
        out_shape=jax.ShapeDtypeStruct((n, D_MODEL), F32),
        compiler_params=_params("arbitrary"),
        name="final",
    )(dest, h2c, yc, gf)


def _rope_tables(t):
    inv = 1.0 / (ROPE_THETA ** (np.arange(0, HEAD_DIM, 2, dtype=np.float64) / HEAD_DIM))
    ang = np.arange(t, dtype=np.float64)[:, None] * inv[None, :]
    ang = np.concatenate([ang, ang, ang, ang], axis=-1)
    sign = np.where((np.arange(LANES) % HEAD_DIM) < HEAD_DIM // 2, -1.0, 1.0)
    return (jnp.asarray(np.cos(ang), dtype=F32), jnp.asarray(np.sin(ang) * sign[None, :], dtype=F32))


def kernel(x, meta, norm1_g, w_in, b_gate, lambda_q1, lambda_k1, lambda_q2, lambda_k2, subln_g, pool_w,
           pool_scale, w_attn_br, w_pool_br, w_out, norm2_g, w_router_group, b_router_group,
           w_router_expert, b_router_expert, w_e_gate, w_e_up, w_e_down, final_g):
    batch, seq, d = x.shape
    n = batch * seq
    x2 = x.reshape(n, d)
    cos, sin = _rope_tables(seq + N_META)
    q, k, v, u, gates, qm, km, vm, um = _inproj(x2, meta, norm1_g, cos, sin, w_in[0], b_gate, tm=1024)
    a = _attention(q, k, v, qm, km, vm, lambda_q1, lambda_k1, lambda_q2, lambda_k2, subln_g, batch, seq,
                   bq=256, bk=256)

    n_router = N_GROUPS + N_EXPERTS
    wr = jnp.pad(jnp.concatenate([w_router_group[0], w_router_expert[0]], axis=1),
                 ((0, 0), (0, LANES - n_router)))
    br = jnp.pad(jnp.concatenate([b_router_group[0], b_router_expert[0]]), (0, LANES - n_router))
    h2c, logits = _mix(x2, a, u, um, gates, w_attn_br[0].astype(BF16), pool_w[0],
                       pool_scale, w_pool_br[0], w_out[0].astype(BF16), norm2_g,
                       wr, br.reshape(1, LANES), seq, tm=1024)
    tme = 512
    n_tiles = n // tme + N_BUCKETS
    assert n_tiles <= LANES
    wts, dest, tile_table = _route(logits, tl=1024, tme=tme)
    dest = dest[0]
    per_token = jnp.concatenate([wts[0:2], jnp.arange(n, dtype=F32)[None, :]], axis=0).T
    per_row = jnp.zeros((n_tiles * tme, 3), F32).at[dest].set(per_token)
    src = per_row[:, 2].astype(jnp.int32)

    yc = _moe(tile_table, src, h2c, per_row, norm2_g, w_e_gate[0], w_e_up[0], w_e_down[0], tme, n_tiles)
    out = _final(dest, h2c, yc, final_g.reshape(1, d), tmf=512)
    return out.reshape(batch, seq, d)
```

```python
import functools
import math

import jax
import jax.numpy as jnp
import numpy as np
from jax import lax
from jax.experimental import pallas as pl
from jax.experimental.pallas import tpu as pltpu

D_MODEL = 1024
N_META = 16
N_HEADS = 8
HEAD_DIM = 64
V_DIM = 2 * HEAD_DIM
POOL_WINDOWS = (2, 4, 8, 16)
N_POOL_GROUPS = len(POOL_WINDOWS)
POOL_GROUP_DIM = 128
POOL_WIDTH = N_POOL_GROUPS * POOL_GROUP_DIM
ROPE_THETA = 10000.0
N_GROUPS = 4
EXPERTS_PER_GROUP = 4
N_EXPERTS = N_GROUPS * EXPERTS_PER_GROUP
D_EXPERT = 512
EPS = 1e-6
LAMBDA_INIT = 0.8 - 0.6 * math.exp(-0.3 * 0)
LOG2_E = math.log2(math.e)

_PAIR_A = (0, 0, 0, 1, 1, 3)
_PAIR_B = (1, 2, 3, 3, 2, 2)
N_PAIRS = len(_PAIR_A)
N_BUCKETS = N_GROUPS * N_PAIRS
N_BUCKET_ROWS = 32
GATHER_UNROLL = 16

LANES = 128
NEG_BIG = -1e30
VMEM_LIMIT = 48 * 1024 * 1024

F32 = jnp.float32
BF16 = jnp.bfloat16

_HEADS_PER_STEP = 4
_IN_STEPS = N_HEADS // _HEADS_PER_STEP
_QKV_BLK = _HEADS_PER_STEP * V_DIM
_U_BLK = POOL_WIDTH // _IN_STEPS
_GATE_BLK = 512
_GATES_PER_STEP = 2 * D_MODEL // _IN_STEPS // _GATE_BLK
_O_K = D_MODEL // _QKV_BLK
_O_V = 2 * D_MODEL // _QKV_BLK
_O_U = 3 * D_MODEL // _U_BLK
_O_G = (3 * D_MODEL + POOL_WIDTH) // _GATE_BLK


def _params(*sem):
    return pltpu.CompilerParams(dimension_semantics=sem, vmem_limit_bytes=VMEM_LIMIT)


TOKEN_ROWS = D_MODEL // LANES


def _store_token_major(ref, val, rows):
    for s in range(TOKEN_ROWS):
        ref[pl.ds(s, rows, stride=TOKEN_ROWS), :] = val[:, s * LANES:(s + 1) * LANES]


def _load_token_major(ref, rows):
    return jnp.concatenate([ref[pl.ds(s, rows, stride=TOKEN_ROWS), :] for s in range(TOKEN_ROWS)], axis=1)


def _rope(z, cos, sin_signed, first_half):
    outs = []
    for c in range(z.shape[1] // LANES):
        zc = z[:, c * LANES:(c + 1) * LANES]
        rot = jnp.where(first_half, pltpu.roll(zc, LANES - HEAD_DIM // 2, 1),
                        pltpu.roll(zc, HEAD_DIM // 2, 1))
        outs.append(zc * cos + rot * sin_signed)
    return jnp.concatenate(outs, axis=1)


def _inproj_kernel(x_ref, meta_ref, g_ref, cos_ref, sin_ref, cosm_ref, sinm_ref, wq_ref, wk_ref, wv_ref,
                   wu_ref, wg0_ref, wg1_ref, bg_ref, q_ref, k_ref, v_ref, u_ref, gate_ref, qm_ref, km_ref,
                   vm_ref, um_ref, wq_sc, wk_sc, wv_sc, wu_sc, wg0_sc, wg1_sc):
    def normed(x):
        ms = jnp.mean(x * x, axis=-1, keepdims=True)
        return (x * lax.rsqrt(ms + EPS) * g_ref[...]).astype(BF16)

    def project(hn, cos, sin, q_out, k_out, v_out, u_out, gate_out):
        lane = lax.broadcasted_iota(jnp.int32, cos.shape, 1)
        first_half = jnp.bitwise_and(lane, HEAD_DIM - 1) < (HEAD_DIM // 2)
        if gate_out is not None:
            for blk, wg_sc in enumerate((wg0_sc, wg1_sc)):
                cols = slice(blk * _GATE_BLK, (blk + 1) * _GATE_BLK)
                zg = jnp.dot(hn, wg_sc[...], preferred_element_type=F32) + bg_ref[:, cols]
                gate_out[:, cols] = (1.0 / (1.0 + jnp.exp(-zg))).astype(BF16)
        zq = jnp.dot(hn, wq_sc[...], preferred_element_type=F32)
        q_out[...] = (_rope(zq, cos, sin, first_half) * (LOG2_E / math.sqrt(HEAD_DIM))).astype(BF16)
        zk = jnp.dot(hn, wk_sc[...], preferred_element_type=F32)
        k_out[...] = _rope(zk, cos, sin, first_half).astype(BF16)
        u_out[...] = jnp.dot(hn, wu_sc[...], preferred_element_type=F32)
        v_out[...] = jnp.dot(hn, wv_sc[...], preferred_element_type=F32).astype(BF16)

    @pl.when(pl.program_id(1) == 0)
    def _():
        for src, dst in ((wq_ref, wq_sc), (wk_ref, wk_sc), (wv_ref, wv_sc), (wu_ref, wu_sc),
                         (wg0_ref, wg0_sc), (wg1_ref, wg1_sc)):
            dst[...] = src[...].astype(BF16)
        project(normed(meta_ref[...]), cosm_ref[...], sinm_ref[...], qm_ref, km_ref, vm_ref, um_ref, None)

    project(normed(x_ref[...]), cos_ref[...], sin_ref[...], q_ref, k_ref, v_ref, u_ref, gate_ref)


def _inproj(x2, meta, g1, cos, sin, w_in, b_gate, tm):
    n = x2.shape[0]
    n_meta = meta.shape[0]
    n_pos_blocks = (cos.shape[0] - n_meta) // tm
    row = lambda j, i: (i, 0)
    pos = lambda j, i: (i % n_pos_blocks, 0)
    whole = lambda j, i: (0, 0)
    once = pl.Buffered(1)
    gate_cols = _GATES_PER_STEP * _GATE_BLK
    assert _GATES_PER_STEP == 2
    return pl.pallas_call(
        _inproj_kernel,
        grid=(_IN_STEPS, n // tm),
        in_specs=[
            pl.BlockSpec((tm, D_MODEL), row),
            pl.BlockSpec((n_meta, D_MODEL), whole),
            pl.BlockSpec((1, D_MODEL), whole),
            pl.BlockSpec((tm, LANES), pos),
            pl.BlockSpec((tm, LANES), pos),
            pl.BlockSpec((n_meta, LANES), whole),
            pl.BlockSpec((n_meta, LANES), whole),
            pl.BlockSpec((D_MODEL, _QKV_BLK), lambda j, i: (0, j), pipeline_mode=once),
            pl.BlockSpec((D_MODEL, _QKV_BLK), lambda j, i: (0, _O_K + j), pipeline_mode=once),
            pl.BlockSpec((D_MODEL, _QKV_BLK), lambda j, i: (0, _O_V + j), pipeline_mode=once),
            pl.BlockSpec((D_MODEL, _U_BLK), lambda j, i: (0, _O_U + j), pipeline_mode=once),
            pl.BlockSpec((D_MODEL, _GATE_BLK), lambda j, i: (0, _O_G + _GATES_PER_STEP * j),
                         pipeline_mode=once),
            pl.BlockSpec((D_MODEL, _GATE_BLK), lambda j, i: (0, _O_G + _GATES_PER_STEP * j + 1),
                         pipeline_mode=once),
            pl.BlockSpec((1, gate_cols), lambda j, i: (0, j)),
        ],
        out_specs=[
            pl.BlockSpec((tm, _QKV_BLK), lambda j, i: (i, j)),
            pl.BlockSpec((tm, _QKV_BLK), lambda j, i: (i, j)),
            pl.BlockSpec((tm, _QKV_BLK), lambda j, i: (i, j)),
            pl.BlockSpec((tm, _U_BLK), lambda j, i: (i, j)),
            pl.BlockSpec((tm, gate_cols), lambda j, i: (i, j)),
            pl.BlockSpec((n_meta, _QKV_BLK), lambda j, i: (0, j)),
            pl.BlockSpec((n_meta, _QKV_BLK), lambda j, i: (0, j)),
            pl.BlockSpec((n_meta, _QKV_BLK), lambda j, i: (0, j)),
            pl.BlockSpec((n_meta, _U_BLK), lambda j, i: (0, j)),
        ],
        out_shape=[
            jax.ShapeDtypeStruct((n, D_MODEL), BF16),
            jax.ShapeDtypeStruct((n, D_MODEL), BF16),
            jax.ShapeDtypeStruct((n, D_MODEL), BF16),
            jax.ShapeDtypeStruct((n, POOL_WIDTH), F32),
            jax.ShapeDtypeStruct((n, 2 * D_MODEL), BF16),
            jax.ShapeDtypeStruct((n_meta, D_MODEL), BF16),
            jax.ShapeDtypeStruct((n_meta, D_MODEL), BF16),
            jax.ShapeDtypeStruct((n_meta, D_MODEL), BF16),
            jax.ShapeDtypeStruct((n_meta, POOL_WIDTH), F32),
        ],
        scratch_shapes=[
            pltpu.VMEM((D_MODEL, _QKV_BLK), BF16),
            pltpu.VMEM((D_MODEL, _QKV_BLK), BF16),
            pltpu.VMEM((D_MODEL, _QKV_BLK), BF16),
            pltpu.VMEM((D_MODEL, _U_BLK), BF16),
            pltpu.VMEM((D_MODEL, _GATE_BLK), BF16),
            pltpu.VMEM((D_MODEL, _GATE_BLK), BF16),
        ],
        compiler_params=_params("arbitrary", "arbitrary"),
        name="inproj",
    )(x2, meta, g1, cos[n_meta:], sin[n_meta:], cos[:n_meta], sin[:n_meta], w_in, w_in, w_in, w_in, w_in,
      w_in, b_gate)


def _attn_kernel(lq1_ref, lk1_ref, lq2_ref, lk2_ref, subg_ref, q_ref, k_ref, v_ref, qm_ref, km_ref,
                 vm_ref, o_ref, *state_sc, bq, bk):
    seq = k_ref.shape[0]
    nq = seq // bq
    states = [tuple(state_sc[3 * g:3 * g + 3]) for g in range(nq + 1)]

    def rows_at(ref, meta_ref, p0, n):
        if p0 == 0:
            return jnp.concatenate([meta_ref[...], ref[0:n - N_META, :]], axis=0)
        real = ref[p0 - N_META:min(p0 - N_META + n, seq), :]
        if real.shape[0] < n:
            real = jnp.concatenate([real, jnp.zeros((n - real.shape[0], real.shape[1]), real.dtype)], axis=0)
        return real

    def block_rows(i):
        return (i * bq, bq) if i < nq else (nq * bq, N_META)

    def n_full_tiles(i):
        return (block_rows(i)[0] + 1) // bk

    def diag_tile(i):
        p0 = n_full_tiles(i) * bk
        return p0, (bk if i < nq else LANES), block_rows(i)[0] - p0

    lam = (jnp.exp(jnp.sum(lq1_ref[...] * lk1_ref[...], axis=-1, keepdims=True))
           - jnp.exp(jnp.sum(lq2_ref[...] * lk2_ref[...], axis=-1, keepdims=True)) + LAMBDA_INIT)

    bias_cache = {}

    def causal_bias(n_rows, n_cols, offset):
        if (n_rows, n_cols, offset) not in bias_cache:
            r = lax.broadcasted_iota(jnp.int32, (n_rows, n_cols), 0)
            c = lax.broadcasted_iota(jnp.int32, (n_rows, n_cols), 1)
            bias = jnp.where(c <= r + offset, 0.0, NEG_BIG)
            bias_cache[(n_rows, n_cols, offset)] = jnp.concatenate([bias, bias], axis=0)
        return bias_cache[(n_rows, n_cols, offset)]

    def load_q(i, qq_sc):
        p0, n = block_rows(i)
        q = rows_at(q_ref, qm_ref, p0, n)
        lane = lax.broadcasted_iota(jnp.int32, q.shape, 1)
        zero = jnp.zeros_like(q)
        qq_sc[0:n, :] = jnp.where(lane < HEAD_DIM, q, zero)
        qq_sc[n:2 * n, :] = jnp.where(lane >= HEAD_DIM, q, zero)

    def step(state, kb, vb, bias):
        qq_sc, m_sc, acc_sc = state
        first = bias is not None
        s = lax.dot_general(qq_sc[...], kb, (((1,), (1,)), ((), ())), preferred_element_type=F32)
        if first:
            s = s + bias
        n_tiles = s.shape[1] // LANES
        rm = s[:, 0:LANES]
        for t in range(1, n_tiles):
            rm = jnp.maximum(rm, s[:, t * LANES:(t + 1) * LANES])
        rmax = jnp.max(rm, axis=-1, keepdims=True)
        if first:
            m_new = jnp.broadcast_to(rmax, (s.shape[0], LANES))
        else:
            m_prev = m_sc[...]
            m_new = jnp.maximum(m_prev, rmax)
        p = jnp.exp2(s - jnp.concatenate([m_new] * n_tiles, axis=1))
        pv = jnp.dot(p.astype(BF16), vb, preferred_element_type=F32)
        if first:
            acc_sc[...] = pv
        else:
            alpha = jnp.exp2(m_prev - m_new)
            acc_sc[...] = jnp.concatenate([alpha, alpha], axis=1) * acc_sc[...] + pv
        m_sc[...] = m_new

    def key_step(state, i, j):
        if j == 0:
            p0, width, offset = diag_tile(i)
            bias = causal_bias(block_rows(i)[1], width, offset)
        else:
            p0, width, bias = (j - 1) * bk, bk, None
        kb = rows_at(k_ref, km_ref, p0, width)
        vb = jnp.concatenate([rows_at(v_ref, vm_ref, p0, width), jnp.ones((width, V_DIM), BF16)], axis=1)
        step(state, kb, vb, bias)

    def finish(i, acc_sc):
        p0, n = block_rows(i)
        o1 = acc_sc[0:n, 0:V_DIM] / acc_sc[0:n, V_DIM:2 * V_DIM]
        o2 = acc_sc[n:2 * n, 0:V_DIM] / acc_sc[n:2 * n, V_DIM:2 * V_DIM]
        o = o1 - lam * o2
        ms = jnp.mean(o * o, axis=-1, keepdims=True)
        o = (o * lax.rsqrt(ms + EPS) * subg_ref[...] * (1.0 - LAMBDA_INIT)).astype(BF16)
        if p0 == 0:
            o_ref[0:n - N_META, :] = o[N_META:n, :]
        else:
            o_ref[p0 - N_META:p0 - N_META + n, :] = o

    order = list(range(nq, -1, -1))
    for i in order:
        load_q(i, states[i][0])
    for j in range(n_full_tiles(nq) + 1):
        for i in order:
            if j <= n_full_tiles(i):
                key_step(states[i], i, j)
    for i in order:
        finish(i, states[i][2])


def _attention(q, k, v, qm, km, vm, lq1, lk1, lq2, lk2, subg, batch, seq, bq, bk):
    nq = seq // bq
    small = lambda b, h: (0, 0)
    head = lambda b, h: (b, h)
    meta = lambda b, h: (0, h)

    def state_scratch(rows):
        return [pltpu.VMEM((2 * rows, V_DIM), BF16), pltpu.VMEM((2 * rows, LANES), F32),
                pltpu.VMEM((2 * rows, 2 * V_DIM), F32)]

    return pl.pallas_call(
        functools.partial(_attn_kernel, bq=bq, bk=bk),
        grid=(batch, N_HEADS),
        in_specs=[
            pl.BlockSpec((1, HEAD_DIM), small),
            pl.BlockSpec((1, HEAD_DIM), small),
            pl.BlockSpec((1, HEAD_DIM), small),
            pl.BlockSpec((1, HEAD_DIM), small),
            pl.BlockSpec((1, V_DIM), small),
            pl.BlockSpec((seq, V_DIM), head),
            pl.BlockSpec((seq, V_DIM), head),
            pl.BlockSpec((seq, V_DIM), head),
            pl.BlockSpec((N_META, V_DIM), meta),
            pl.BlockSpec((N_META, V_DIM), meta),
            pl.BlockSpec((N_META, V_DIM), meta),
        ],
        out_specs=pl.BlockSpec((seq, V_DIM), head),
        out_shape=jax.ShapeDtypeStruct(q.shape, BF16),
        scratch_shapes=state_scratch(bq) * nq + state_scratch(N_META),
        compiler_params=_params("arbitrary", "arbitrary"),
        name="diff_attn",
    )(lq1, lk1, lq2, lk2, subg, q, k, v, qm, km, vm)


def _mix_kernel(x_ref, a_ref, u_ref, uprev_ref, umeta_ref, gate_ref, wa_ref, pw_ref, ps_ref, wp_ref,
                wo_ref, g2_ref, wr_ref, br_ref, h2c_ref, logit_ref, ext_sc, sum_a_sc, sum_b_sc, wfold_sc,
                *, tm, tiles_per_seq):
    i = pl.program_id(0)
    first = (i % tiles_per_seq) == 0

    @pl.when(i == 0)
    def _():
        for g in range(N_POOL_GROUPS):
            rows = slice(g * POOL_GROUP_DIM, (g + 1) * POOL_GROUP_DIM)
            left = pw_ref[g] * ps_ref[:, rows]
            right = wp_ref[rows, :]
            left_hi, right_hi = left.astype(BF16), right.astype(BF16)
            left_lo = (left - left_hi.astype(F32)).astype(BF16)
            right_lo = (right - right_hi.astype(F32)).astype(BF16)
            wfold_sc[rows, :] = (jnp.dot(left_hi, right_hi, preferred_element_type=F32)
                                 + jnp.dot(left_lo, right_hi, preferred_element_type=F32)
                                 + jnp.dot(left_hi, right_lo, preferred_element_type=F32)).astype(BF16)

    y_attn = jnp.dot(a_ref[...], wa_ref[...], preferred_element_type=F32)

    top = 2 * N_META
    for ref in (ext_sc, sum_a_sc, sum_b_sc):
        ref[0:N_META, :] = jnp.zeros((N_META, ref.shape[1]), F32)

    @pl.when(first)
    def _():
        ext_sc[N_META:top, :] = umeta_ref[...]

    @pl.when(jnp.logical_not(first))
    def _():
        ext_sc[N_META:top, :] = uprev_ref[...]

    ext_sc[top:top + tm, :] = u_ref[...]

    pooled = []
    for g, w in enumerate(POOL_WINDOWS):
        cs = slice(g * POOL_GROUP_DIM, (g + 1) * POOL_GROUP_DIM)
        src, lanes, shift = ext_sc, cs, 1
        for dst in (sum_a_sc, sum_b_sc, sum_a_sc, sum_b_sc):
            if shift >= w:
                break
            dst[N_META:top + tm, :] = (src[N_META:top + tm, lanes]
                                       + src[N_META - shift:top + tm - shift, lanes])
            src, lanes, shift = dst, slice(None), 2 * shift
        pooled.append((src[top:top + tm, lanes] * (1.0 / w) - ext_sc[top:top + tm, cs]).astype(BF16))
    y_pool = jnp.dot(jnp.concatenate(pooled, axis=1), wfold_sc[...], preferred_element_type=F32)
    mixed = (gate_ref[:, 0:D_MODEL].astype(F32) * y_attn
             + gate_ref[:, D_MODEL:2 * D_MODEL].astype(F32) * y_pool)
    h2 = x_ref[...] + jnp.dot(mixed.astype(BF16), wo_ref[...], preferred_element_type=F32)
    _store_token_major(h2c_ref, h2, tm)
    ms = jnp.mean(h2 * h2, axis=-1, keepdims=True)
    hn2 = h2 * lax.rsqrt(ms + EPS) * g2_ref[...]
    hn2_hi = hn2.astype(BF16)
    hn2_lo = (hn2 - hn2_hi.astype(F32)).astype(BF16)
    wr = wr_ref[...]
    wr_hi = wr.astype(BF16)
    wr_lo = (wr - wr_hi.astype(F32)).astype(BF16)
    wr_both = jnp.concatenate([wr_hi, wr_lo], axis=1)
    for half in range(2):
        rows = slice(half * tm // 2, (half + 1) * tm // 2)
        both = jnp.dot(hn2_hi[rows, :], wr_both, preferred_element_type=F32)
        logit_ref[rows, :] = (both[:, 0:LANES] + both[:, LANES:2 * LANES]
                              + jnp.dot(hn2_lo[rows, :], wr_hi, preferred_element_type=F32) + br_ref[...])


def _mix(x2, a, u, umeta, gates, w_attn, pool_w, pool_scale, w_pool, w_out, g2, wr, br, seq, tm):
    n = x2.shape[0]
    tiles_per_seq = seq // tm
    halo_blocks = tm // N_META
    const2 = lambda i: (0, 0)
    return pl.pallas_call(
        functools.partial(_mix_kernel, tm=tm, tiles_per_seq=tiles_per_seq),
        grid=(n // tm,),
        in_specs=[
            pl.BlockSpec((tm, D_MODEL), lambda i: (i, 0)),
            pl.BlockSpec((tm, D_MODEL), lambda i: (i, 0)),
            pl.BlockSpec((tm, POOL_WIDTH), lambda i: (i, 0)),
            pl.BlockSpec((N_META, POOL_WIDTH), lambda i: (jnp.maximum(i * halo_blocks - 1, 0), 0)),
            pl.BlockSpec((N_META, POOL_WIDTH), const2),
            pl.BlockSpec((tm, 2 * D_MODEL), lambda i: (i, 0)),
            pl.BlockSpec((D_MODEL, D_MODEL), const2, pipeline_mode=pl.Buffered(1)),
            pl.BlockSpec((N_POOL_GROUPS, POOL_GROUP_DIM, POOL_GROUP_DIM), lambda i: (0, 0, 0),
                         pipeline_mode=pl.Buffered(1)),
            pl.BlockSpec((1, POOL_WIDTH), const2),
            pl.BlockSpec((POOL_WIDTH, D_MODEL), const2, pipeline_mode=pl.Buffered(1)),
            pl.BlockSpec((D_MODEL, D_MODEL), const2, pipeline_mode=pl.Buffered(1)),
            pl.BlockSpec((1, D_MODEL), const2),
            pl.BlockSpec((D_MODEL, LANES), const2, pipeline_mode=pl.Buffered(1)),
            pl.BlockSpec((1, LANES), const2),
        ],
        out_specs=[
            pl.BlockSpec((tm * TOKEN_ROWS, LANES), lambda i: (i, 0)),
            pl.BlockSpec((tm, LANES), lambda i: (i, 0)),
        ],
        out_shape=[
            jax.ShapeDtypeStruct((n * TOKEN_ROWS, LANES), F32),
            jax.ShapeDtypeStruct((n, LANES), F32),
        ],
        scratch_shapes=[pltpu.VMEM((tm + 2 * N_META, POOL_WIDTH), F32),
                        pltpu.VMEM((tm + 2 * N_META, POOL_GROUP_DIM), F32),
                        pltpu.VMEM((tm + 2 * N_META, POOL_GROUP_DIM), F32),
                        pltpu.VMEM((POOL_WIDTH, D_MODEL), BF16)],
        compiler_params=_params("arbitrary"),
        name="mix",
    )(x2, a, u, u, umeta, gates, w_attn, pool_w, pool_scale, w_pool, w_out, g2, wr, br)


_TBL_EA, _TBL_EB, _TBL_NEW_A, _TBL_NEW_B, _TBL_NVALID, _TBL_NUSED = range(6)


def _route_kernel(logit_ref, tri_ref, wts_ref, dest_ref, tbl_ref, carry_sc, ids_sc, *, tme):
    i = pl.program_id(1)
    pl.when(pl.program_id(0) == 0)(functools.partial(_route_classify, i, logit_ref, tri_ref, wts_ref,
                                                     carry_sc, ids_sc))
    pl.when(pl.program_id(0) == 1)(functools.partial(_route_place, i, dest_ref, tbl_ref, carry_sc,
                                                     ids_sc, tme))


def _route_classify(i, logit_ref, tri_ref, wts_ref, carry_sc, ids_sc):
    @pl.when(i == 0)
    def _():
        carry_sc[...] = jnp.zeros_like(carry_sc)

    lg = logit_ref[...].T
    g = [lg[r:r + 1, :] for r in range(N_GROUPS)]
    gmax = functools.reduce(jnp.maximum, g)
    gsel = jnp.full_like(g[0], N_GROUPS - 1).astype(jnp.int32)
    for r in range(N_GROUPS - 2, -1, -1):
        gsel = jnp.where(g[r] == gmax, r, gsel)
    p_group = 1.0 / functools.reduce(lambda a, b: a + b, [jnp.exp(x - gmax) for x in g])
    e = []
    for jj in range(EXPERTS_PER_GROUP):
        v = lg[N_GROUPS + jj:N_GROUPS + jj + 1, :]
        for r in range(1, N_GROUPS):
            row = N_GROUPS + r * EXPERTS_PER_GROUP + jj
            v = jnp.where(gsel == r, lg[row:row + 1, :], v)
        e.append(v)
    v1 = functools.reduce(jnp.maximum, e)
    i1 = jnp.full_like(gsel, EXPERTS_PER_GROUP - 1)
    for jj in range(EXPERTS_PER_GROUP - 2, -1, -1):
        i1 = jnp.where(e[jj] == v1, jj, i1)
    rest = [jnp.where(i1 == jj, -jnp.inf, e[jj]) for jj in range(EXPERTS_PER_GROUP)]
    v2 = functools.reduce(jnp.maximum, rest)
    i2 = jnp.full_like(gsel, EXPERTS_PER_GROUP - 1)
    for jj in range(EXPERTS_PER_GROUP - 2, -1, -1):
        i2 = jnp.where(jnp.logical_and(rest[jj] == v2, i1 != jj), jj, i2)
    t = jnp.exp(v2 - v1)
    w1 = p_group / (1.0 + t)
    w2 = p_group * t / (1.0 + t)
    lo = jnp.minimum(i1, i2)
    hi = jnp.maximum(i1, i2)
    pair = jnp.where(lo == 0, hi - 1, jnp.where(lo == 1, jnp.where(hi == 3, 3, 4), 5))
    bucket = gsel * N_PAIRS + pair
    w_of_lo = jnp.where(i1 < i2, w1, w2)
    w_of_hi = jnp.where(i1 < i2, w2, w1)
    swapped = pair == 5
    w_a = jnp.where(swapped, w_of_hi, w_of_lo)
    w_b = jnp.where(swapped, w_of_lo, w_of_hi)

    tl = lg.shape[1]
    bid = lax.broadcasted_iota(jnp.int32, (N_BUCKET_ROWS, tl), 0)
    hot = bid == bucket
    hot_b = jnp.where(hot, 1.0, 0.0).astype(BF16)
    prefix = jnp.dot(hot_b, tri_ref[...], preferred_element_type=F32)
    carry = carry_sc[...]
    before = prefix + jnp.concatenate([carry] * (tl // LANES), axis=1)
    rank = jnp.sum(jnp.where(hot, before, 0.0), axis=0, keepdims=True)
    carry_sc[...] = carry + jnp.dot(hot_b, jnp.ones((tl, LANES), BF16), preferred_element_type=F32)
    pad = jnp.zeros((6, tl), F32)
    ids_sc[i] = jnp.concatenate([bucket.astype(F32), rank, pad], axis=0)
    wts_ref[...] = jnp.concatenate([w_a, w_b, pad], axis=0)


def _route_place(i, dest_ref, tbl_ref, carry_sc, ids_sc, tme):
    cnt = carry_sc[...]
    tiles = jnp.floor((cnt + (tme - 1)) * (1.0 / tme))
    r = lax.broadcasted_iota(jnp.int32, (N_BUCKET_ROWS, N_BUCKET_ROWS), 0)
    c = lax.broadcasted_iota(jnp.int32, (N_BUCKET_ROWS, N_BUCKET_ROWS), 1)
    tile_end = jnp.dot(jnp.where(c <= r, 1.0, 0.0).astype(BF16), tiles.astype(BF16),
                       preferred_element_type=F32)
    tile_start = tile_end - tiles

    ids = ids_sc[i]
    bucket, rank = ids[0:1, :], ids[1:2, :]
    tl = ids.shape[1]
    bid = lax.broadcasted_iota(jnp.int32, (N_BUCKET_ROWS, tl), 0).astype(F32)
    row_start = jnp.concatenate([tile_start * tme] * (tl // LANES), axis=1)
    dest = rank + jnp.sum(jnp.where(bid == bucket, row_start, 0.0), axis=0, keepdims=True)
    dest_ref[...] = jnp.concatenate([dest, jnp.zeros((7, tl), F32)], axis=0).astype(jnp.int32)

    lane = lax.broadcasted_iota(jnp.int32, (1, LANES), 1).astype(F32)
    n_used = tile_end[N_BUCKET_ROWS - 1:N_BUCKET_ROWS, :]
    last = jnp.minimum(lane, n_used - 1.0)
    tb = jnp.sum(jnp.where(tile_end <= last, 1.0, 0.0), axis=0, keepdims=True)
    tb = jnp.minimum(tb, N_BUCKETS - 1.0)
    group = functools.reduce(lambda a, b: a + b,
                             [jnp.where(tb >= g * N_PAIRS, 1.0, 0.0) for g in range(1, N_GROUPS)])
    pair = tb - N_PAIRS * group

    def lookup(table):
        out = jnp.full_like(pair, float(table[-1]))
        for k in range(len(table) - 2, -1, -1):
            out = jnp.where(pair == k, float(table[k]), out)
        return EXPERTS_PER_GROUP * group + out

    e_a, e_b = lookup(_PAIR_A), lookup(_PAIR_B)
    changed = lambda e: jnp.where(lane == 0.0, 1.0, jnp.where(e != pltpu.roll(e, 1, 1), 1.0, 0.0))
    bid128 = lax.broadcasted_iota(jnp.int32, (N_BUCKET_ROWS, LANES), 0).astype(F32)
    at_tile = lambda col: jnp.sum(jnp.where(bid128 == tb, col, 0.0), axis=0, keepdims=True)
    n_valid = jnp.clip(at_tile(cnt) - tme * (lane - at_tile(tile_start)), 0.0, float(tme))
    rows = [e_a, e_b, changed(e_a), changed(e_b), n_valid, n_used, jnp.zeros_like(lane),
            jnp.zeros_like(lane)]
    tbl_ref[...] = jnp.concatenate(rows, axis=0).astype(jnp.int32)


def _route(logits, tl, tme):
    n = logits.shape[0]
    nt = n // tl
    tri = jnp.asarray(np.triu(np.ones((tl, tl), np.float32), k=1), dtype=BF16)
    first_pass = lambda p, i: i * (1 - p) + (nt - 1) * p
    return pl.pallas_call(
        functools.partial(_route_kernel, tme=tme),
        grid=(2, nt),
        in_specs=[pl.BlockSpec((tl, LANES), lambda p, i: (first_pass(p, i), 0)),
                  pl.BlockSpec((tl, tl), lambda p, i: (0, 0))],
        out_specs=[pl.BlockSpec((8, tl), lambda p, i: (0, first_pass(p, i))),
                   pl.BlockSpec((8, tl), lambda p, i: (0, i * p)),
                   pl.BlockSpec((8, LANES), lambda p, i: (0, 0))],
        out_shape=[jax.ShapeDtypeStruct((8, n), F32),
                   jax.ShapeDtypeStruct((8, n), jnp.int32),
                   jax.ShapeDtypeStruct((8, LANES), jnp.int32)],
        scratch_shapes=[pltpu.VMEM((N_BUCKET_ROWS, LANES), F32), pltpu.VMEM((nt, 8, tl), F32)],
        compiler_params=_params("arbitrary", "arbitrary"),
        name="route",
    )(logits, tri)


def _token_copy(src_hbm, src_token, dst_buf, dst_row, sem):
    return pltpu.make_async_copy(
        src_hbm.at[pl.ds(pl.multiple_of(src_token * TOKEN_ROWS, TOKEN_ROWS), TOKEN_ROWS), :],
        dst_buf.at[pl.ds(pl.multiple_of(dst_row * TOKEN_ROWS, TOKEN_ROWS), TOKEN_ROWS), :],
        sem)


def _start_gather(idx_ref, first, n_groups, src_hbm, dst_buf, sem):
    def body(rr, carry):
        for uu in range(GATHER_UNROLL):
            r = rr * GATHER_UNROLL + uu
            _token_copy(src_hbm, idx_ref[first + r], dst_buf, r, sem).start(priority=uu % 2)
        return carry

    lax.fori_loop(0, n_groups, body, 0)


def _start_padded_gather(idx_ref, first, n_valid, rows, src_hbm, dst_buf, sem):
    n_groups = (n_valid + GATHER_UNROLL - 1) // GATHER_UNROLL
    _start_gather(idx_ref, first, n_groups, src_hbm, dst_buf, sem)
    pad_groups = rows // GATHER_UNROLL - n_groups
    off = n_groups * GATHER_UNROLL
    size = rows // 2
    while size >= GATHER_UNROLL:
        take = jnp.bitwise_and(pad_groups, size // GATHER_UNROLL) != 0

        @pl.when(take)
        def _(off=off, size=size):
            pltpu.make_async_copy(
                src_hbm.at[pl.ds(0, size * TOKEN_ROWS), :],
                dst_buf.at[pl.ds(pl.multiple_of(off * TOKEN_ROWS, GATHER_UNROLL * TOKEN_ROWS),
                                 size * TOKEN_ROWS), :],
                sem).start()

        off = off + jnp.where(take, size, 0)
        size //= 2


def _wait_gather(rows, src_hbm, dst_buf, sem):
    pltpu.make_async_copy(src_hbm.at[pl.ds(0, rows * TOKEN_ROWS), :], dst_buf, sem).wait()


def _moe_kernel(tbl_ref, src_ref, h2c_hbm, wts_ref, g2_ref, wg_a_ref, wu_a_ref, wd_a_ref, wg_b_ref,
                wu_b_ref, wd_b_ref, yc_ref, xbuf, wg_a_sc, wu_a_sc, wd_a_sc, wg_b_sc, wu_b_sc, wd_b_sc,
                sem, *, tme):
    i = pl.program_id(0)
    n_used = tbl_ref[_TBL_NUSED, 0]
    slot = jnp.bitwise_and(i, 1)

    @pl.when(i == 0)
    def _():
        _start_padded_gather(src_ref, 0, tbl_ref[_TBL_NVALID, 0], tme, h2c_hbm, xbuf.at[0], sem.at[0])

    @pl.when(i + 1 < n_used)
    def _():
        _start_padded_gather(src_ref, (i + 1) * tme, tbl_ref[_TBL_NVALID, i + 1], tme, h2c_hbm,
                             xbuf.at[1 - slot], sem.at[1 - slot])

    @pl.when(tbl_ref[_TBL_NEW_A, i] == 1)
    def _():
        for src, dst in ((wg_a_ref, wg_a_sc), (wu_a_ref, wu_a_sc), (wd_a_ref, wd_a_sc)):
            dst[...] = src[0].astype(BF16)

    @pl.when(tbl_ref[_TBL_NEW_B, i] == 1)
    def _():
        for src, dst in ((wg_b_ref, wg_b_sc), (wu_b_ref, wu_b_sc), (wd_b_ref, wd_b_sc)):
            dst[...] = src[0].astype(BF16)

    def tile(rows):
        _wait_gather(tme, h2c_hbm, xbuf.at[slot], sem.at[slot])
        x = _load_token_major(xbuf.at[slot], rows)
        ms = jnp.mean(x * x, axis=-1, keepdims=True)
        hn = (x * lax.rsqrt(ms + EPS) * g2_ref[...]).astype(BF16)
        wts = wts_ref[0:rows, :]

        def expert_act(wg_sc, wu_sc, w):
            gate = jnp.dot(hn, wg_sc[...], preferred_element_type=F32)
            up = jnp.dot(hn, wu_sc[...], preferred_element_type=F32)
            return (gate / (1.0 + jnp.exp(-gate)) * up * w).astype(BF16)

        y = (jnp.dot(expert_act(wg_a_sc, wu_a_sc, wts[:, 0:1]), wd_a_sc[...], preferred_element_type=F32)
             + jnp.dot(expert_act(wg_b_sc, wu_b_sc, wts[:, 1:2]), wd_b_sc[...], preferred_element_type=F32))
        _store_token_major(yc_ref, y, rows)
        if rows < tme:
            yc_ref[rows * TOKEN_ROWS:tme * TOKEN_ROWS, :] = jnp.zeros(((tme - rows) * TOKEN_ROWS, LANES), F32)

    n_valid = tbl_ref[_TBL_NVALID, i]
    pl.when(jnp.logical_and(i < n_used, n_valid > tme // 2))(functools.partial(tile, tme))
    pl.when(jnp.logical_and(i < n_used, n_valid <= tme // 2))(functools.partial(tile, tme // 2))

    @pl.when(i >= n_used)
    def _():
        yc_ref[...] = jnp.zeros_like(yc_ref)


def _moe(tbl, src, h2c, wts_sorted, g2, wg, wu, wd, tme, n_tiles):
    used = lambda i, tbl, src: (jnp.minimum(i, tbl[_TBL_NUSED, 0] - 1), 0)
    w_a = lambda i, tbl, src: (tbl[_TBL_EA, i], 0, 0)
    w_b = lambda i, tbl, src: (tbl[_TBL_EB, i], 0, 0)
    up_shape, down_shape = (D_MODEL, D_EXPERT), (D_EXPERT, D_MODEL)
    grid_spec = pltpu.PrefetchScalarGridSpec(
        num_scalar_prefetch=2,
        grid=(n_tiles,),
        in_specs=[
            pl.BlockSpec(memory_space=pl.ANY),
            pl.BlockSpec((tme, wts_sorted.shape[1]), used),
            pl.BlockSpec((1, D_MODEL), lambda i, tbl, src: (0, 0)),
            pl.BlockSpec((1,) + up_shape, w_a),
            pl.BlockSpec((1,) + up_shape, w_a),
            pl.BlockSpec((1,) + down_shape, w_a),
            pl.BlockSpec((1,) + up_shape, w_b),
            pl.BlockSpec((1,) + up_shape, w_b),
            pl.BlockSpec((1,) + down_shape, w_b),
        ],
        out_specs=pl.BlockSpec((tme * TOKEN_ROWS, LANES), lambda i, tbl, src: (i, 0)),
        scratch_shapes=[pltpu.VMEM((2, tme * TOKEN_ROWS, LANES), F32)]
        + [pltpu.VMEM(s, BF16) for s in (up_shape, up_shape, down_shape) * 2]
        + [pltpu.SemaphoreType.DMA((2,))],
    )
    return pl.pallas_call(
        functools.partial(_moe_kernel, tme=tme),
        grid_spec=grid_spec,
        out_shape=jax.ShapeDtypeStruct((n_tiles * tme * TOKEN_ROWS, LANES), F32),
        compiler_params=_params("arbitrary"),
        name="moe",
    )(tbl, src, h2c, wts_sorted, g2, wg, wu, wd, wg, wu, wd)


def _final_kernel(dest_ref, h2c_ref, yc_hbm, gf_ref, o_ref, ybuf, sem, *, tmf):
    i = pl.program_id(0)
    slot = jnp.bitwise_and(i, 1)

    @pl.when(i == 0)
    def _():
        _start_gather(dest_ref, 0, tmf // GATHER_UNROLL, yc_hbm, ybuf.at[0], sem.at[0])

    @pl.when(i + 1 < pl.num_programs(0))
    def _():
        _start_gather(dest_ref, (i + 1) * tmf, tmf // GATHER_UNROLL, yc_hbm, ybuf.at[1 - slot],
                      sem.at[1 - slot])

    _wait_gather(tmf, yc_hbm, ybuf.at[slot], sem.at[slot])
    h = _load_token_major(h2c_ref, tmf) + _load_token_major(ybuf.at[slot], tmf)
    ms = jnp.mean(h * h, axis=-1, keepdims=True)
    o_ref[...] = h * lax.rsqrt(ms + EPS) * gf_ref[...]


def _final(dest, h2c, yc, gf, tmf):
    n = h2c.shape[0] // TOKEN_ROWS
    grid_spec = pltpu.PrefetchScalarGridSpec(
        num_scalar_prefetch=1,
        grid=(n // tmf,),
        in_specs=[
            pl.BlockSpec((tmf * TOKEN_ROWS, LANES), lambda i, dest: (i, 0)),
            pl.BlockSpec(memory_space=pl.ANY),
            pl.BlockSpec((1, D_MODEL), lambda i, dest: (0, 0)),
        ],
        out_specs=pl.BlockSpec((tmf, D_MODEL), lambda i, dest: (i, 0)),
        scratch_shapes=[pltpu.VMEM((2, tmf * TOKEN_ROWS, LANES), F32), pltpu.SemaphoreType.DMA((2,))],
    )
    return pl.pallas_call(
        functools.partial(_final_kernel, tmf=tmf),
        grid_spec=grid_spec,
        out_shape=jax.ShapeDtypeStruct((n, D_MODEL), F32),
        compiler_params=_params("arbitrary"),
        name="final",
    )(dest, h2c, yc, gf)


def _rope_tables(t):
    inv = 1.0 / (ROPE_THETA ** (np.arange(0, HEAD_DIM, 2, dtype=np.float64) / HEAD_DIM))
    ang = np.arange(t, dtype=np.float64)[:, None] * inv[None, :]
    ang = np.concatenate([ang, ang, ang, ang], axis=-1)
    sign = np.where((np.arange(LANES) % HEAD_DIM) < HEAD_DIM // 2, -1.0, 1.0)
    return (jnp.asarray(np.cos(ang), dtype=F32), jnp.asarray(np.sin(ang) * sign[None, :], dtype=F32))


def kernel(x, meta, norm1_g, w_in, b_gate, lambda_q1, lambda_k1, lambda_q2, lambda_k2, subln_g, pool_w,
           pool_scale, w_attn_br, w_pool_br, w_out, norm2_g, w_router_group, b_router_group,
           w_router_expert, b_router_expert, w_e_gate, w_e_up, w_e_down, final_g):
    batch, seq, d = x.shape
    n = batch * seq
    assert w_in.shape[0] == 1 and d == D_MODEL and meta.shape == (N_META, D_MODEL)
    assert seq % 1024 == 0 and n < 2 ** 24
    x2 = x.reshape(n, d)
    cos, sin = _rope_tables(seq + N_META)
    q, k, v, u, gates, qm, km, vm, um = _inproj(x2, meta, norm1_g, cos, sin, w_in[0], b_gate, tm=1024)
    a = _attention(q, k, v, qm, km, vm, lambda_q1, lambda_k1, lambda_q2, lambda_k2, subln_g, batch, seq,
                   bq=256, bk=256)

    n_router = N_GROUPS + N_EXPERTS
    wr = jnp.pad(jnp.concatenate([w_router_group[0], w_router_expert[0]], axis=1),
                 ((0, 0), (0, LANES - n_router)))
    br = jnp.pad(jnp.concatenate([b_router_group[0], b_router_expert[0]]), (0, LANES - n_router))
    h2c, logits = _mix(x2, a, u, um, gates, w_attn_br[0].astype(BF16), pool_w[0],
                       pool_scale, w_pool_br[0], w_out[0].astype(BF16), norm2_g,
                       wr, br.reshape(1, LANES), seq, tm=1024)
    tme = 512
    n_tiles = n // tme + N_BUCKETS
    assert n_tiles <= LANES
    wts, dest, tile_table = _route(logits, tl=1024, tme=tme)
    dest = dest[0]
    per_token = jnp.concatenate([wts[0:2], jnp.arange(n, dtype=F32)[None, :]], axis=0).T
    per_row = jnp.zeros((n_tiles * tme, 3), F32).at[dest].set(per_token)
    src = per_row[:, 2].astype(jnp.int32)

    yc = _moe(tile_table, src, h2c, per_row, norm2_g, w_e_gate[0], w_e_up[0], w_e_down[0], tme, n_tiles)
    out = _final(dest, h2c, yc, final_g.reshape(1, d), tmf=512)
    return out.reshape(batch, seq, d)
```

```python
import functools
import math

import jax
import jax.numpy as jnp
import numpy as np
from jax import lax
from jax.experimental import pallas as pl
from jax.experimental.pallas import tpu as pltpu

D_MODEL = 1024
N_META = 16
N_HEADS = 8
HEAD_DIM = 64
V_DIM = 2 * HEAD_DIM
POOL_WINDOWS = (2, 4, 8, 16)
N_POOL_GROUPS = len(POOL_WINDOWS)
POOL_GROUP_DIM = 128
POOL_WIDTH = N_POOL_GROUPS * POOL_GROUP_DIM
ROPE_THETA = 10000.0
N_GROUPS = 4
EXPERTS_PER_GROUP = 4
N_EXPERTS = N_GROUPS * EXPERTS_PER_GROUP
D_EXPERT = 512
EPS = 1e-6
LAMBDA_INIT = 0.8 - 0.6 * math.exp(-0.3 * 0)
LOG2_E = math.log2(math.e)

_PAIR_A = (0, 0, 0, 1, 1, 3)
_PAIR_B = (1, 2, 3, 3, 2, 2)
N_PAIRS = len(_PAIR_A)
N_BUCKETS = N_GROUPS * N_PAIRS
N_BUCKET_ROWS = 32
GATHER_UNROLL = 16

LANES = 128
NEG_BIG = -1e30
VMEM_LIMIT = 48 * 1024 * 1024

F32 = jnp.float32
BF16 = jnp.bfloat16

_HEADS_PER_STEP = 4
_IN_STEPS = N_HEADS // _HEADS_PER_STEP
_QKV_BLK = _HEADS_PER_STEP * V_DIM
_U_BLK = POOL_WIDTH // _IN_STEPS
_GATE_BLK = 512
_GATES_PER_STEP = 2 * D_MODEL // _IN_STEPS // _GATE_BLK
_O_K = D_MODEL // _QKV_BLK
_O_V = 2 * D_MODEL // _QKV_BLK
_O_U = 3 * D_MODEL // _U_BLK
_O_G = (3 * D_MODEL + POOL_WIDTH) // _GATE_BLK


def _params(*sem):
    return pltpu.CompilerParams(dimension_semantics=sem, vmem_limit_bytes=VMEM_LIMIT)


TOKEN_ROWS = D_MODEL // LANES


def _store_token_major(ref, val, rows):
    for s in range(TOKEN_ROWS):
        ref[pl.ds(s, rows, stride=TOKEN_ROWS), :] = val[:, s * LANES:(s + 1) * LANES]


def _load_token_major(ref, rows):
    return jnp.concatenate([ref[pl.ds(s, rows, stride=TOKEN_ROWS), :] for s in range(TOKEN_ROWS)], axis=1)


def _rope(z, cos, sin_signed, first_half):
    outs = []
    for c in range(z.shape[1] // LANES):
        zc = z[:, c * LANES:(c + 1) * LANES]
        rot = jnp.where(first_half, pltpu.roll(zc, LANES - HEAD_DIM // 2, 1),
                        pltpu.roll(zc, HEAD_DIM // 2, 1))
        outs.append(zc * cos + rot * sin_signed)
    return jnp.concatenate(outs, axis=1)


def _inproj_kernel(x_ref, meta_ref, g_ref, cos_ref, sin_ref, cosm_ref, sinm_ref, wq_ref, wk_ref, wv_ref,
                   wu_ref, wg0_ref, wg1_ref, bg_ref, q_ref, k_ref, v_ref, u_ref, gate_ref, qm_ref, km_ref,
                   vm_ref, um_ref, wq_sc, wk_sc, wv_sc, wu_sc, wg0_sc, wg1_sc):
    def normed(x):
        ms = jnp.mean(x * x, axis=-1, keepdims=True)
        return (x * lax.rsqrt(ms + EPS) * g_ref[...]).astype(BF16)

    def project(hn, cos, sin, q_out, k_out, v_out, u_out, gate_out):
        lane = lax.broadcasted_iota(jnp.int32, cos.shape, 1)
        first_half = jnp.bitwise_and(lane, HEAD_DIM - 1) < (HEAD_DIM // 2)
        if gate_out is not None:
            for blk, wg_sc in enumerate((wg0_sc, wg1_sc)):
                cols = slice(blk * _GATE_BLK, (blk + 1) * _GATE_BLK)
                zg = jnp.dot(hn, wg_sc[...], preferred_element_type=F32) + bg_ref[:, cols]
                gate_out[:, cols] = (1.0 / (1.0 + jnp.exp(-zg))).astype(BF16)
        zq = jnp.dot(hn, wq_sc[...], preferred_element_type=F32)
        q_out[...] = (_rope(zq, cos, sin, first_half) * (LOG2_E / math.sqrt(HEAD_DIM))).astype(BF16)
        zk = jnp.dot(hn, wk_sc[...], preferred_element_type=F32)
        k_out[...] = _rope(zk, cos, sin, first_half).astype(BF16)
        u_out[...] = jnp.dot(hn, wu_sc[...], preferred_element_type=F32)
        v_out[...] = jnp.dot(hn, wv_sc[...], preferred_element_type=F32).astype(BF16)

    @pl.when(pl.program_id(1) == 0)
    def _():
        for src, dst in ((wq_ref, wq_sc), (wk_ref, wk_sc), (wv_ref, wv_sc), (wu_ref, wu_sc),
                         (wg0_ref, wg0_sc), (wg1_ref, wg1_sc)):
            dst[...] = src[...].astype(BF16)
        project(normed(meta_ref[...]), cosm_ref[...], sinm_ref[...], qm_ref, km_ref, vm_ref, um_ref, None)

    project(normed(x_ref[...]), cos_ref[...], sin_ref[...], q_ref, k_ref, v_ref, u_ref, gate_ref)


def _inproj(x2, meta, g1, cos, sin, w_in, b_gate, tm):
    n = x2.shape[0]
    n_meta = meta.shape[0]
    n_pos_blocks = (cos.shape[0] - n_meta) // tm
    row = lambda j, i: (i, 0)
    pos = lambda j, i: (i % n_pos_blocks, 0)
    whole = lambda j, i: (0, 0)
    once = pl.Buffered(1)
    gate_cols = _GATES_PER_STEP * _GATE_BLK
    assert _GATES_PER_STEP == 2
    return pl.pallas_call(
        _inproj_kernel,
        grid=(_IN_STEPS, n // tm),
        in_specs=[
            pl.BlockSpec((tm, D_MODEL), row),
            pl.BlockSpec((n_meta, D_MODEL), whole),
            pl.BlockSpec((1, D_MODEL), whole),
            pl.BlockSpec((tm, LANES), pos),
            pl.BlockSpec((tm, LANES), pos),
            pl.BlockSpec((n_meta, LANES), whole),
            pl.BlockSpec((n_meta, LANES), whole),
            pl.BlockSpec((D_MODEL, _QKV_BLK), lambda j, i: (0, j), pipeline_mode=once),
            pl.BlockSpec((D_MODEL, _QKV_BLK), lambda j, i: (0, _O_K + j), pipeline_mode=once),
            pl.BlockSpec((D_MODEL, _QKV_BLK), lambda j, i: (0, _O_V + j), pipeline_mode=once),
            pl.BlockSpec((D_MODEL, _U_BLK), lambda j, i: (0, _O_U + j), pipeline_mode=once),
            pl.BlockSpec((D_MODEL, _GATE_BLK), lambda j, i: (0, _O_G + _GATES_PER_STEP * j),
                         pipeline_mode=once),
            pl.BlockSpec((D_MODEL, _GATE_BLK), lambda j, i: (0, _O_G + _GATES_PER_STEP * j + 1),
                         pipeline_mode=once),
            pl.BlockSpec((1, gate_cols), lambda j, i: (0, j)),
        ],
        out_specs=[
            pl.BlockSpec((tm, _QKV_BLK), lambda j, i: (i, j)),
            pl.BlockSpec((tm, _QKV_BLK), lambda j, i: (i, j)),
            pl.BlockSpec((tm, _QKV_BLK), lambda j, i: (i, j)),
            pl.BlockSpec((tm, _U_BLK), lambda j, i: (i, j)),
            pl.BlockSpec((tm, gate_cols), lambda j, i: (i, j)),
            pl.BlockSpec((n_meta, _QKV_BLK), lambda j, i: (0, j)),
            pl.BlockSpec((n_meta, _QKV_BLK), lambda j, i: (0, j)),
            pl.BlockSpec((n_meta, _QKV_BLK), lambda j, i: (0, j)),
            pl.BlockSpec((n_meta, _U_BLK), lambda j, i: (0, j)),
        ],
        out_shape=[
            jax.ShapeDtypeStruct((n, D_MODEL), BF16),
            jax.ShapeDtypeStruct((n, D_MODEL), BF16),
            jax.ShapeDtypeStruct((n, D_MODEL), BF16),
            jax.ShapeDtypeStruct((n, POOL_WIDTH), F32),
            jax.ShapeDtypeStruct((n, 2 * D_MODEL), BF16),
            jax.ShapeDtypeStruct((n_meta, D_MODEL), BF16),
            jax.ShapeDtypeStruct((n_meta, D_MODEL), BF16),
            jax.ShapeDtypeStruct((n_meta, D_MODEL), BF16),
            jax.ShapeDtypeStruct((n_meta, POOL_WIDTH), F32),
        ],
        scratch_shapes=[
            pltpu.VMEM((D_MODEL, _QKV_BLK), BF16),
            pltpu.VMEM((D_MODEL, _QKV_BLK), BF16),
            pltpu.VMEM((D_MODEL, _QKV_BLK), BF16),
            pltpu.VMEM((D_MODEL, _U_BLK), BF16),
            pltpu.VMEM((D_MODEL, _GATE_BLK), BF16),
            pltpu.VMEM((D_MODEL, _GATE_BLK), BF16),
        ],
        compiler_params=_params("arbitrary", "arbitrary"),
        name="inproj",
    )(x2, meta, g1, cos[n_meta:], sin[n_meta:], cos[:n_meta], sin[:n_meta], w_in, w_in, w_in, w_in, w_in,
      w_in, b_gate)


def _attn_kernel(lq1_ref, lk1_ref, lq2_ref, lk2_ref, subg_ref, q_ref, k_ref, v_ref, qm_ref, km_ref,
                 vm_ref, o_ref, *state_sc, bq, bk):
    seq = k_ref.shape[0]
    nq = seq // bq
    states = [tuple(state_sc[3 * g:3 * g + 3]) for g in range(nq + 1)]

    def rows_at(ref, meta_ref, p0, n):
        if p0 == 0:
            return jnp.concatenate([meta_ref[...], ref[0:n - N_META, :]], axis=0)
        real = ref[p0 - N_META:min(p0 - N_META + n, seq), :]
        if real.shape[0] < n:
            real = jnp.concatenate([real, jnp.zeros((n - real.shape[0], real.shape[1]), real.dtype)], axis=0)
        return real

    def block_rows(i):
        return (i * bq, bq) if i < nq else (nq * bq, N_META)

    def n_full_tiles(i):
        return (block_rows(i)[0] + 1) // bk

    def diag_tile(i):
        p0 = n_full_tiles(i) * bk
        return p0, (bk if i < nq else LANES), block_rows(i)[0] - p0

    lam = (jnp.exp(jnp.sum(lq1_ref[...] * lk1_ref[...], axis=-1, keepdims=True))
           - jnp.exp(jnp.sum(lq2_ref[...] * lk2_ref[...], axis=-1, keepdims=True)) + LAMBDA_INIT)

    bias_cache = {}

    def causal_bias(n_rows, n_cols, offset):
        if (n_rows, n_cols, offset) not in bias_cache:
            r = lax.broadcasted_iota(jnp.int32, (n_rows, n_cols), 0)
            c = lax.broadcasted_iota(jnp.int32, (n_rows, n_cols), 1)
            bias = jnp.where(c <= r + offset, 0.0, NEG_BIG)
            bias_cache[(n_rows, n_cols, offset)] = jnp.concatenate([bias, bias], axis=0)
        return bias_cache[(n_rows, n_cols, offset)]

    def load_q(i, qq_sc):
        p0, n = block_rows(i)
        q = rows_at(q_ref, qm_ref, p0, n)
        lane = lax.broadcasted_iota(jnp.int32, q.shape, 1)
        zero = jnp.zeros_like(q)
        qq_sc[0:n, :] = jnp.where(lane < HEAD_DIM, q, zero)
        qq_sc[n:2 * n, :] = jnp.where(lane >= HEAD_DIM, q, zero)

    def step(state, kb, vb, bias):
        qq_sc, m_sc, acc_sc = state
        first = bias is not None
        s = lax.dot_general(qq_sc[...], kb, (((1,), (1,)), ((), ())), preferred_element_type=F32)
        if first:
            s = s + bias
        n_tiles = s.shape[1] // LANES
        rm = s[:, 0:LANES]
        for t in range(1, n_tiles):
            rm = jnp.maximum(rm, s[:, t * LANES:(t + 1) * LANES])
        rmax = jnp.max(rm, axis=-1, keepdims=True)
        if first:
            m_new = jnp.broadcast_to(rmax, (s.shape[0], LANES))
        else:
            m_prev = m_sc[...]
            m_new = jnp.maximum(m_prev, rmax)
        p = jnp.exp2(s - jnp.concatenate([m_new] * n_tiles, axis=1))
        pv = jnp.dot(p.astype(BF16), vb, preferred_element_type=F32)
        if first:
            acc_sc[...] = pv
        else:
            alpha = jnp.exp2(m_prev - m_new)
            acc_sc[...] = jnp.concatenate([alpha, alpha], axis=1) * acc_sc[...] + pv
        m_sc[...] = m_new

    def key_step(state, i, j):
        if j == 0:
            p0, width, offset = diag_tile(i)
            bias = causal_bias(block_rows(i)[1], width, offset)
        else:
            p0, width, bias = (j - 1) * bk, bk, None
        kb = rows_at(k_ref, km_ref, p0, width)
        vb = jnp.concatenate([rows_at(v_ref, vm_ref, p0, width), jnp.ones((width, V_DIM), BF16)], axis=1)
        step(state, kb, vb, bias)

    def finish(i, acc_sc):
        p0, n = block_rows(i)
        o1 = acc_sc[0:n, 0:V_DIM] / acc_sc[0:n, V_DIM:2 * V_DIM]
        o2 = acc_sc[n:2 * n, 0:V_DIM] / acc_sc[n:2 * n, V_DIM:2 * V_DIM]
        o = o1 - lam * o2
        ms = jnp.mean(o * o, axis=-1, keepdims=True)
        o = (o * lax.rsqrt(ms + EPS) * subg_ref[...] * (1.0 - LAMBDA_INIT)).astype(BF16)
        if p0 == 0:
            o_ref[0:n - N_META, :] = o[N_META:n, :]
        else:
            o_ref[p0 - N_META:p0 - N_META + n, :] = o

    order = list(range(nq, -1, -1))
    for i in order:
        load_q(i, states[i][0])
    for j in range(n_full_tiles(nq) + 1):
        for i in order:
            if j <= n_full_tiles(i):
                key_step(states[i], i, j)
    for i in order:
        finish(i, states[i][2])


def _attention(q, k, v, qm, km, vm, lq1, lk1, lq2, lk2, subg, batch, seq, bq, bk):
    nq = seq // bq
    small = lambda b, h: (0, 0)
    head = lambda b, h: (b, h)
    meta = lambda b, h: (0, h)

    def state_scratch(rows):
        return [pltpu.VMEM((2 * rows, V_DIM), BF16), pltpu.VMEM((2 * rows, LANES), F32),
                pltpu.VMEM((2 * rows, 2 * V_DIM), F32)]

    return pl.pallas_call(
        functools.partial(_attn_kernel, bq=bq, bk=bk),
        grid=(batch, N_HEADS),
        in_specs=[
            pl.BlockSpec((1, HEAD_DIM), small),
            pl.BlockSpec((1, HEAD_DIM), small),
            pl.BlockSpec((1, HEAD_DIM), small),
            pl.BlockSpec((1, HEAD_DIM), small),
            pl.BlockSpec((1, V_DIM), small),
            pl.BlockSpec((seq, V_DIM), head),
            pl.BlockSpec((seq, V_DIM), head),
            pl.BlockSpec((seq, V_DIM), head),
            pl.BlockSpec((N_META, V_DIM), meta),
            pl.BlockSpec((N_META, V_DIM), meta),
            pl.BlockSpec((N_META, V_DIM), meta),
        ],
        out_specs=pl.BlockSpec((seq, V_DIM), head),
        out_shape=jax.ShapeDtypeStruct(q.shape, BF16),
        scratch_shapes=state_scratch(bq) * nq + state_scratch(N_META),
        compiler_params=_params("arbitrary", "arbitrary"),
        name="diff_attn",
    )(lq1, lk1, lq2, lk2, subg, q, k, v, qm, km, vm)


def _mix_kernel(x_ref, a_ref, u_ref, uprev_ref, umeta_ref, gate_ref, wa_ref, pw_ref, ps_ref, wp_ref,
                wo_ref, g2_ref, wr_ref, br_ref, h2c_ref, logit_ref, ext_sc, sum_a_sc, sum_b_sc, wfold_sc,
                *, tm, tiles_per_seq):
    i = pl.program_id(0)
    first = (i % tiles_per_seq) == 0

    @pl.when(i == 0)
    def _():
        for g in range(N_POOL_GROUPS):
            rows = slice(g * POOL_GROUP_DIM, (g + 1) * POOL_GROUP_DIM)
            left = pw_ref[g] * ps_ref[:, rows]
            right = wp_ref[rows, :]
            left_hi, right_hi = left.astype(BF16), right.astype(BF16)
            left_lo = (left - left_hi.astype(F32)).astype(BF16)
            right_lo = (right - right_hi.astype(F32)).astype(BF16)
            wfold_sc[rows, :] = (jnp.dot(left_hi, right_hi, preferred_element_type=F32)
                                 + jnp.dot(left_lo, right_hi, preferred_element_type=F32)
                                 + jnp.dot(left_hi, right_lo, preferred_element_type=F32)).astype(BF16)

    y_attn = jnp.dot(a_ref[...], wa_ref[...], preferred_element_type=F32)

    top = 2 * N_META
    for ref in (ext_sc, sum_a_sc, sum_b_sc):
        ref[0:N_META, :] = jnp.zeros((N_META, ref.shape[1]), F32)

    @pl.when(first)
    def _():
        ext_sc[N_META:top, :] = umeta_ref[...]

    @pl.when(jnp.logical_not(first))
    def _():
        ext_sc[N_META:top, :] = uprev_ref[...]

    ext_sc[top:top + tm, :] = u_ref[...]

    pooled = []
    for g, w in enumerate(POOL_WINDOWS):
        cs = slice(g * POOL_GROUP_DIM, (g + 1) * POOL_GROUP_DIM)
        src, lanes, shift = ext_sc, cs, 1
        for dst in (sum_a_sc, sum_b_sc, sum_a_sc, sum_b_sc):
            if shift >= w:
                break
            dst[N_META:top + tm, :] = (src[N_META:top + tm, lanes]
                                       + src[N_META - shift:top + tm - shift, lanes])
            src, lanes, shift = dst, slice(None), 2 * shift
        pooled.append((src[top:top + tm, lanes] * (1.0 / w) - ext_sc[top:top + tm, cs]).astype(BF16))
    y_pool = jnp.dot(jnp.concatenate(pooled, axis=1), wfold_sc[...], preferred_element_type=F32)
    mixed = (gate_ref[:, 0:D_MODEL].astype(F32) * y_attn
             + gate_ref[:, D_MODEL:2 * D_MODEL].astype(F32) * y_pool)
    h2 = x_ref[...] + jnp.dot(mixed.astype(BF16), wo_ref[...], preferred_element_type=F32)
    _store_token_major(h2c_ref, h2, tm)
    ms = jnp.mean(h2 * h2, axis=-1, keepdims=True)
    hn2 = h2 * lax.rsqrt(ms + EPS) * g2_ref[...]
    hn2_hi = hn2.astype(BF16)
    hn2_lo = (hn2 - hn2_hi.astype(F32)).astype(BF16)
    wr = wr_ref[...]
    wr_hi = wr.astype(BF16)
    wr_lo = (wr - wr_hi.astype(F32)).astype(BF16)
    wr_both = jnp.concatenate([wr_hi, wr_lo], axis=1)
    for half in range(2):
        rows = slice(half * tm // 2, (half + 1) * tm // 2)
        both = jnp.dot(hn2_hi[rows, :], wr_both, preferred_element_type=F32)
        logit_ref[rows, :] = (both[:, 0:LANES] + both[:, LANES:2 * LANES]
                              + jnp.dot(hn2_lo[rows, :], wr_hi, preferred_element_type=F32) + br_ref[...])


def _mix(x2, a, u, umeta, gates, w_attn, pool_w, pool_scale, w_pool, w_out, g2, wr, br, seq, tm):
    n = x2.shape[0]
    tiles_per_seq = seq // tm
    halo_blocks = tm // N_META
    const2 = lambda i: (0, 0)
    return pl.pallas_call(
        functools.partial(_mix_kernel, tm=tm, tiles_per_seq=tiles_per_seq),
        grid=(n // tm,),
        in_specs=[
            pl.BlockSpec((tm, D_MODEL), lambda i: (i, 0)),
            pl.BlockSpec((tm, D_MODEL), lambda i: (i, 0)),
            pl.BlockSpec((tm, POOL_WIDTH), lambda i: (i, 0)),
            pl.BlockSpec((N_META, POOL_WIDTH), lambda i: (jnp.maximum(i * halo_blocks - 1, 0), 0)),
            pl.BlockSpec((N_META, POOL_WIDTH), const2),
            pl.BlockSpec((tm, 2 * D_MODEL), lambda i: (i, 0)),
            pl.BlockSpec((D_MODEL, D_MODEL), const2, pipeline_mode=pl.Buffered(1)),
            pl.BlockSpec((N_POOL_GROUPS, POOL_GROUP_DIM, POOL_GROUP_DIM), lambda i: (0, 0, 0),
                         pipeline_mode=pl.Buffered(1)),
            pl.BlockSpec((1, POOL_WIDTH), const2),
            pl.BlockSpec((POOL_WIDTH, D_MODEL), const2, pipeline_mode=pl.Buffered(1)),
            pl.BlockSpec((D_MODEL, D_MODEL), const2, pipeline_mode=pl.Buffered(1)),
            pl.BlockSpec((1, D_MODEL), const2),
            pl.BlockSpec((D_MODEL, LANES), const2, pipeline_mode=pl.Buffered(1)),
            pl.BlockSpec((1, LANES), const2),
        ],
        out_specs=[
            pl.BlockSpec((tm * TOKEN_ROWS, LANES), lambda i: (i, 0)),
            pl.BlockSpec((tm, LANES), lambda i: (i, 0)),
        ],
        out_shape=[
            jax.ShapeDtypeStruct((n * TOKEN_ROWS, LANES), F32),
            jax.ShapeDtypeStruct((n, LANES), F32),
        ],
        scratch_shapes=[pltpu.VMEM((tm + 2 * N_META, POOL_WIDTH), F32),
                        pltpu.VMEM((tm + 2 * N_META, POOL_GROUP_DIM), F32),
                        pltpu.VMEM((tm + 2 * N_META, POOL_GROUP_DIM), F32),
                        pltpu.VMEM((POOL_WIDTH, D_MODEL), BF16)],
        compiler_params=_params("arbitrary"),
        name="mix",
    )(x2, a, u, u, umeta, gates, w_attn, pool_w, pool_scale, w_pool, w_out, g2, wr, br)


_TBL_EA, _TBL_EB, _TBL_NEW_A, _TBL_NEW_B, _TBL_NVALID, _TBL_NUSED = range(6)


def _route_kernel(logit_ref, tri_ref, wts_ref, dest_ref, tbl_ref, carry_sc, ids_sc, *, tme):
    i = pl.program_id(1)
    pl.when(pl.program_id(0) == 0)(functools.partial(_route_classify, i, logit_ref, tri_ref, wts_ref,
                                                     carry_sc, ids_sc))
    pl.when(pl.program_id(0) == 1)(functools.partial(_route_place, i, dest_ref, tbl_ref, carry_sc,
                                                     ids_sc, tme))


def _route_classify(i, logit_ref, tri_ref, wts_ref, carry_sc, ids_sc):
    @pl.when(i == 0)
    def _():
        carry_sc[...] = jnp.zeros_like(carry_sc)

    lg = logit_ref[...].T
    g = [lg[r:r + 1, :] for r in range(N_GROUPS)]
    gmax = functools.reduce(jnp.maximum, g)
    gsel = jnp.full_like(g[0], N_GROUPS - 1).astype(jnp.int32)
    for r in range(N_GROUPS - 2, -1, -1):
        gsel = jnp.where(g[r] == gmax, r, gsel)
    p_group = 1.0 / functools.reduce(lambda a, b: a + b, [jnp.exp(x - gmax) for x in g])
    e = []
    for jj in range(EXPERTS_PER_GROUP):
        v = lg[N_GROUPS + jj:N_GROUPS + jj + 1, :]
        for r in range(1, N_GROUPS):
            row = N_GROUPS + r * EXPERTS_PER_GROUP + jj
            v = jnp.where(gsel == r, lg[row:row + 1, :], v)
        e.append(v)
    v1 = functools.reduce(jnp.maximum, e)
    i1 = jnp.full_like(gsel, EXPERTS_PER_GROUP - 1)
    for jj in range(EXPERTS_PER_GROUP - 2, -1, -1):
        i1 = jnp.where(e[jj] == v1, jj, i1)
    rest = [jnp.where(i1 == jj, -jnp.inf, e[jj]) for jj in range(EXPERTS_PER_GROUP)]
    v2 = functools.reduce(jnp.maximum, rest)
    i2 = jnp.full_like(gsel, EXPERTS_PER_GROUP - 1)
    for jj in range(EXPERTS_PER_GROUP - 2, -1, -1):
        i2 = jnp.where(jnp.logical_and(rest[jj] == v2, i1 != jj), jj, i2)
    t = jnp.exp(v2 - v1)
    w1 = p_group / (1.0 + t)
    w2 = p_group * t / (1.0 + t)
    lo = jnp.minimum(i1, i2)
    hi = jnp.maximum(i1, i2)
    pair = jnp.where(lo == 0, hi - 1, jnp.where(lo == 1, jnp.where(hi == 3, 3, 4), 5))
    bucket = gsel * N_PAIRS + pair
    w_of_lo = jnp.where(i1 < i2, w1, w2)
    w_of_hi = jnp.where(i1 < i2, w2, w1)
    swapped = pair == 5
    w_a = jnp.where(swapped, w_of_hi, w_of_lo)
    w_b = jnp.where(swapped, w_of_lo, w_of_hi)

    tl = lg.shape[1]
    bid = lax.broadcasted_iota(jnp.int32, (N_BUCKET_ROWS, tl), 0)
    hot = bid == bucket
    hot_b = jnp.where(hot, 1.0, 0.0).astype(BF16)
    prefix = jnp.dot(hot_b, tri_ref[...], preferred_element_type=F32)
    carry = carry_sc[...]
    before = prefix + jnp.concatenate([carry] * (tl // LANES), axis=1)
    rank = jnp.sum(jnp.where(hot, before, 0.0), axis=0, keepdims=True)
    carry_sc[...] = carry + jnp.dot(hot_b, jnp.ones((tl, LANES), BF16), preferred_element_type=F32)
    pad = jnp.zeros((6, tl), F32)
    ids_sc[i] = jnp.concatenate([bucket.astype(F32), rank, pad], axis=0)
    wts_ref[...] = jnp.concatenate([w_a, w_b, pad], axis=0)


def _route_place(i, dest_ref, tbl_ref, carry_sc, ids_sc, tme):
    cnt = carry_sc[...]
    tiles = jnp.floor((cnt + (tme - 1)) * (1.0 / tme))
    r = lax.broadcasted_iota(jnp.int32, (N_BUCKET_ROWS, N_BUCKET_ROWS), 0)
    c = lax.broadcasted_iota(jnp.int32, (N_BUCKET_ROWS, N_BUCKET_ROWS), 1)
    tile_end = jnp.dot(jnp.where(c <= r, 1.0, 0.0).astype(BF16), tiles.astype(BF16),
                       preferred_element_type=F32)
    tile_start = tile_end - tiles

    ids = ids_sc[i]
    bucket, rank = ids[0:1, :], ids[1:2, :]
    tl = ids.shape[1]
    bid = lax.broadcasted_iota(jnp.int32, (N_BUCKET_ROWS, tl), 0).astype(F32)
    row_start = jnp.concatenate([tile_start * tme] * (tl // LANES), axis=1)
    dest = rank + jnp.sum(jnp.where(bid == bucket, row_start, 0.0), axis=0, keepdims=True)
    dest_ref[...] = jnp.concatenate([dest, jnp.zeros((7, tl), F32)], axis=0).astype(jnp.int32)

    lane = lax.broadcasted_iota(jnp.int32, (1, LANES), 1).astype(F32)
    n_used = tile_end[N_BUCKET_ROWS - 1:N_BUCKET_ROWS, :]
    last = jnp.minimum(lane, n_used - 1.0)
    tb = jnp.sum(jnp.where(tile_end <= last, 1.0, 0.0), axis=0, keepdims=True)
    tb = jnp.minimum(tb, N_BUCKETS - 1.0)
    group = functools.reduce(lambda a, b: a + b,
                             [jnp.where(tb >= g * N_PAIRS, 1.0, 0.0) for g in range(1, N_GROUPS)])
    pair = tb - N_PAIRS * group

    def lookup(table):
        out = jnp.full_like(pair, float(table[-1]))
        for k in range(len(table) - 2, -1, -1):
            out = jnp.where(pair == k, float(table[k]), out)
        return EXPERTS_PER_GROUP * group + out

    e_a, e_b = lookup(_PAIR_A), lookup(_PAIR_B)
    changed = lambda e: jnp.where(lane == 0.0, 1.0, jnp.where(e != pltpu.roll(e, 1, 1), 1.0, 0.0))
    bid128 = lax.broadcasted_iota(jnp.int32, (N_BUCKET_ROWS, LANES), 0).astype(F32)
    at_tile = lambda col: jnp.sum(jnp.where(bid128 == tb, col, 0.0), axis=0, keepdims=True)
    n_valid = jnp.clip(at_tile(cnt) - tme * (lane - at_tile(tile_start)), 0.0, float(tme))
    rows = [e_a, e_b, changed(e_a), changed(e_b), n_valid, n_used, jnp.zeros_like(lane),
            jnp.zeros_like(lane)]
    tbl_ref[...] = jnp.concatenate(rows, axis=0).astype(jnp.int32)


def _route(logits, tl, tme):
    n = logits.shape[0]
    nt = n // tl
    tri = jnp.asarray(np.triu(np.ones((tl, tl), np.float32), k=1), dtype=BF16)
    first_pass = lambda p, i: i * (1 - p) + (nt - 1) * p
    return pl.pallas_call(
        functools.partial(_route_kernel, tme=tme),
        grid=(2, nt),
        in_specs=[pl.BlockSpec((tl, LANES), lambda p, i: (first_pass(p, i), 0)),
                  pl.BlockSpec((tl, tl), lambda p, i: (0, 0), pipeline_mode=pl.Buffered(1))],
        out_specs=[pl.BlockSpec((8, tl), lambda p, i: (0, first_pass(p, i))),
                   pl.BlockSpec((8, tl), lambda p, i: (0, i * p)),
                   pl.BlockSpec((8, LANES), lambda p, i: (0, 0))],
        out_shape=[jax.ShapeDtypeStruct((8, n), F32),
                   jax.ShapeDtypeStruct((8, n), jnp.int32),
                   jax.ShapeDtypeStruct((8, LANES), jnp.int32)],
        scratch_shapes=[pltpu.VMEM((N_BUCKET_ROWS, LANES), F32), pltpu.VMEM((nt, 8, tl), F32)],
        compiler_params=_params("arbitrary", "arbitrary"),
        name="route",
    )(logits, tri)


def _token_copy(src_hbm, src_token, dst_buf, dst_row, sem):
    return pltpu.make_async_copy(
        src_hbm.at[pl.ds(pl.multiple_of(src_token * TOKEN_ROWS, TOKEN_ROWS), TOKEN_ROWS), :],
        dst_buf.at[pl.ds(pl.multiple_of(dst_row * TOKEN_ROWS, TOKEN_ROWS), TOKEN_ROWS), :],
        sem)


def _start_gather(idx_ref, first, n_groups, src_hbm, dst_buf, sem):
    def body(rr, carry):
        for uu in range(GATHER_UNROLL):
            r = rr * GATHER_UNROLL + uu
            _token_copy(src_hbm, idx_ref[first + r], dst_buf, r, sem).start(priority=uu % 2)
        return carry

    lax.fori_loop(0, n_groups, body, 0)


def _start_padded_gather(idx_ref, first, n_valid, rows, src_hbm, dst_buf, sem):
    n_groups = (n_valid + GATHER_UNROLL - 1) // GATHER_UNROLL
    _start_gather(idx_ref, first, n_groups, src_hbm, dst_buf, sem)
    pad_groups = rows // GATHER_UNROLL - n_groups
    off = n_groups * GATHER_UNROLL
    size = rows // 2
    while size >= GATHER_UNROLL:
        take = jnp.bitwise_and(pad_groups, size // GATHER_UNROLL) != 0

        @pl.when(take)
        def _(off=off, size=size):
            pltpu.make_async_copy(
                src_hbm.at[pl.ds(0, size * TOKEN_ROWS), :],
                dst_buf.at[pl.ds(pl.multiple_of(off * TOKEN_ROWS, GATHER_UNROLL * TOKEN_ROWS),
                                 size * TOKEN_ROWS), :],
                sem).start()

        off = off + jnp.where(take, size, 0)
        size //= 2


def _wait_gather(rows, src_hbm, dst_buf, sem):
    pltpu.make_async_copy(src_hbm.at[pl.ds(0, rows * TOKEN_ROWS), :], dst_buf, sem).wait()


def _moe_kernel(tbl_ref, src_ref, h2c_hbm, wts_ref, g2_ref, wg_a_ref, wu_a_ref, wd_a_ref, wg_b_ref,
                wu_b_ref, wd_b_ref, yc_ref, xbuf, wg_a_sc, wu_a_sc, wd_a_sc, wg_b_sc, wu_b_sc, wd_b_sc,
                sem, *, tme):
    i = pl.program_id(0)
    n_used = tbl_ref[_TBL_NUSED, 0]
    slot = jnp.bitwise_and(i, 1)

    @pl.when(i == 0)
    def _():
        _start_padded_gather(src_ref, 0, tbl_ref[_TBL_NVALID, 0], tme, h2c_hbm, xbuf.at[0], sem.at[0])

    @pl.when(i + 1 < n_used)
    def _():
        _start_padded_gather(src_ref, (i + 1) * tme, tbl_ref[_TBL_NVALID, i + 1], tme, h2c_hbm,
                             xbuf.at[1 - slot], sem.at[1 - slot])

    @pl.when(tbl_ref[_TBL_NEW_A, i] == 1)
    def _():
        for src, dst in ((wg_a_ref, wg_a_sc), (wu_a_ref, wu_a_sc), (wd_a_ref, wd_a_sc)):
            dst[...] = src[0].astype(BF16)

    @pl.when(tbl_ref[_TBL_NEW_B, i] == 1)
    def _():
        for src, dst in ((wg_b_ref, wg_b_sc), (wu_b_ref, wu_b_sc), (wd_b_ref, wd_b_sc)):
            dst[...] = src[0].astype(BF16)

    def tile(rows):
        _wait_gather(tme, h2c_hbm, xbuf.at[slot], sem.at[slot])
        x = _load_token_major(xbuf.at[slot], rows)
        ms = jnp.mean(x * x, axis=-1, keepdims=True)
        hn = (x * lax.rsqrt(ms + EPS) * g2_ref[...]).astype(BF16)
        wts = wts_ref[0:rows, :]

        def expert_act(wg_sc, wu_sc, w):
            gate = jnp.dot(hn, wg_sc[...], preferred_element_type=F32)
            up = jnp.dot(hn, wu_sc[...], preferred_element_type=F32)
            return (gate / (1.0 + jnp.exp(-gate)) * up * w).astype(BF16)

        y = (jnp.dot(expert_act(wg_a_sc, wu_a_sc, wts[:, 0:1]), wd_a_sc[...], preferred_element_type=F32)
             + jnp.dot(expert_act(wg_b_sc, wu_b_sc, wts[:, 1:2]), wd_b_sc[...], preferred_element_type=F32))
        _store_token_major(yc_ref, y, rows)
        if rows < tme:
            yc_ref[rows * TOKEN_ROWS:tme * TOKEN_ROWS, :] = jnp.zeros(((tme - rows) * TOKEN_ROWS, LANES), F32)

    n_valid = tbl_ref[_TBL_NVALID, i]
    pl.when(jnp.logical_and(i < n_used, n_valid > tme // 2))(functools.partial(tile, tme))
    pl.when(jnp.logical_and(i < n_used, n_valid <= tme // 2))(functools.partial(tile, tme // 2))

    @pl.when(i >= n_used)
    def _():
        yc_ref[...] = jnp.zeros_like(yc_ref)


def _moe(tbl, src, h2c, wts_sorted, g2, wg, wu, wd, tme, n_tiles):
    used = lambda i, tbl, src: (jnp.minimum(i, tbl[_TBL_NUSED, 0] - 1), 0)
    w_a = lambda i, tbl, src: (tbl[_TBL_EA, i], 0, 0)
    w_b = lambda i, tbl, src: (tbl[_TBL_EB, i], 0, 0)
    up_shape, down_shape = (D_MODEL, D_EXPERT), (D_EXPERT, D_MODEL)
    grid_spec = pltpu.PrefetchScalarGridSpec(
        num_scalar_prefetch=2,
        grid=(n_tiles,),
        in_specs=[
            pl.BlockSpec(memory_space=pl.ANY),
            pl.BlockSpec((tme, wts_sorted.shape[1]), used),
            pl.BlockSpec((1, D_MODEL), lambda i, tbl, src: (0, 0)),
            pl.BlockSpec((1,) + up_shape, w_a),
            pl.BlockSpec((1,) + up_shape, w_a),
            pl.BlockSpec((1,) + down_shape, w_a),
            pl.BlockSpec((1,) + up_shape, w_b),
            pl.BlockSpec((1,) + up_shape, w_b),
            pl.BlockSpec((1,) + down_shape, w_b),
        ],
        out_specs=pl.BlockSpec((tme * TOKEN_ROWS, LANES), lambda i, tbl, src: (i, 0)),
        scratch_shapes=[pltpu.VMEM((2, tme * TOKEN_ROWS, LANES), F32)]
        + [pltpu.VMEM(s, BF16) for s in (up_shape, up_shape, down_shape) * 2]
        + [pltpu.SemaphoreType.DMA((2,))],
    )
    return pl.pallas_call(
        functools.partial(_moe_kernel, tme=tme),
        grid_spec=grid_spec,
        out_shape=jax.ShapeDtypeStruct((n_tiles * tme * TOKEN_ROWS, LANES), F32),
        compiler_params=_params("arbitrary"),
        name="moe",
    )(tbl, src, h2c, wts_sorted, g2, wg, wu, wd, wg, wu, wd)


def _final_kernel(dest_ref, h2c_ref, yc_hbm, gf_ref, o_ref, ybuf, sem, *, tmf):
    i = pl.program_id(0)
    slot = jnp.bitwise_and(i, 1)

    @pl.when(i == 0)
    def _():
        _start_gather(dest_ref, 0, tmf // GATHER_UNROLL, yc_hbm, ybuf.at[0], sem.at[0])

    @pl.when(i + 1 < pl.num_programs(0))
    def _():
        _start_gather(dest_ref, (i + 1) * tmf, tmf // GATHER_UNROLL, yc_hbm, ybuf.at[1 - slot],
                      sem.at[1 - slot])

    _wait_gather(tmf, yc_hbm, ybuf.at[slot], sem.at[slot])
    h = _load_token_major(h2c_ref, tmf) + _load_token_major(ybuf.at[slot], tmf)
    ms = jnp.mean(h * h, axis=-1, keepdims=True)
    o_ref[...] = h * lax.rsqrt(ms + EPS) * gf_ref[...]


def _final(dest, h2c, yc, gf, tmf):
    n = h2c.shape[0] // TOKEN_ROWS
    grid_spec = pltpu.PrefetchScalarGridSpec(
        num_scalar_prefetch=1,
        grid=(n // tmf,),
        in_specs=[
            pl.BlockSpec((tmf * TOKEN_ROWS, LANES), lambda i, dest: (i, 0)),
            pl.BlockSpec(memory_space=pl.ANY),
            pl.BlockSpec((1, D_MODEL), lambda i, dest: (0, 0)),
        ],
        out_specs=pl.BlockSpec((tmf, D_MODEL), lambda i, dest: (i, 0)),
        scratch_shapes=[pltpu.VMEM((2, tmf * TOKEN_ROWS, LANES), F32), pltpu.SemaphoreType.DMA((2,))],
    )
    return pl.pallas_call(
        functools.partial(_final_kernel, tmf=tmf),
        grid_spec=grid_spec,
        out_shape=jax.ShapeDtypeStruct((n, D_MODEL), F32),
        compiler_params=_params("arbitrary"),
        name="final",
    )(dest, h2c, yc, gf)


def _rope_tables(t):
    inv = 1.0 / (ROPE_THETA ** (np.arange(0, HEAD_DIM, 2, dtype=np.float64) / HEAD_DIM))
    ang = np.arange(t, dtype=np.float64)[:, None] * inv[None, :]
    ang = np.concatenate([ang, ang, ang, ang], axis=-1)
    sign = np.where((np.arange(LANES) % HEAD_DIM) < HEAD_DIM // 2, -1.0, 1.0)
    return (jnp.asarray(np.cos(ang), dtype=F32), jnp.asarray(np.sin(ang) * sign[None, :], dtype=F32))


def kernel(x, meta, norm1_g, w_in, b_gate, lambda_q1, lambda_k1, lambda_q2, lambda_k2, subln_g, pool_w,
           pool_scale, w_attn_br, w_pool_br, w_out, norm2_g, w_router_group, b_router_group,
           w_router_expert, b_router_expert, w_e_gate, w_e_up, w_e_down, final_g):
    batch, seq, d = x.shape
    n = batch * seq
    assert w_in.shape[0] == 1 and d == D_MODEL and meta.shape == (N_META, D_MODEL)
    assert seq % 1024 == 0 and n < 2 ** 24
    x2 = x.reshape(n, d)
    cos, sin = _rope_tables(seq + N_META)
    q, k, v, u, gates, qm, km, vm, um = _inproj(x2, meta, norm1_g, cos, sin, w_in[0], b_gate, tm=1024)
    a = _attention(q, k, v, qm, km, vm, lambda_q1, lambda_k1, lambda_q2, lambda_k2, subln_g, batch, seq,
                   bq=256, bk=256)

    n_router = N_GROUPS + N_EXPERTS
    wr = jnp.pad(jnp.concatenate([w_router_group[0], w_router_expert[0]], axis=1),
                 ((0, 0), (0, LANES - n_router)))
    br = jnp.pad(jnp.concatenate([b_router_group[0], b_router_expert[0]]), (0, LANES - n_router))
    h2c, logits = _mix(x2, a, u, um, gates, w_attn_br[0].astype(BF16), pool_w[0],
                       pool_scale, w_pool_br[0], w_out[0].astype(BF16), norm2_g,
                       wr, br.reshape(1, LANES), seq, tm=1024)
    tme = 512
    n_tiles = n // tme + N_BUCKETS
    assert n_tiles <= LANES
    wts, dest, tile_table = _route(logits, tl=2048, tme=tme)
    dest = dest[0]
    per_token = jnp.concatenate([wts[0:2], jnp.arange(n, dtype=F32)[None, :]], axis=0).T
    per_row = jnp.zeros((n_tiles * tme, 3), F32).at[dest].set(per_token)
    src = per_row[:, 2].astype(jnp.int32)

    yc = _moe(tile_table, src, h2c, per_row, norm2_g, w_e_gate[0], w_e_up[0], w_e_down[0], tme, n_tiles)
    out = _final(dest, h2c, yc, final_g.reshape(1, d), tmf=512)
    return out.reshape(batch, seq, d)
```

```python
import functools
import math

import jax
import jax.numpy as jnp
import numpy as np
from jax import lax
from jax.experimental import pallas as pl
from jax.experimental.pallas import tpu as pltpu

D_MODEL = 1024
N_META = 16
N_HEADS = 8
HEAD_DIM = 64
V_DIM = 2 * HEAD_DIM
POOL_WINDOWS = (2, 4, 8, 16)
N_POOL_GROUPS = len(POOL_WINDOWS)
POOL_GROUP_DIM = 128
POOL_WIDTH = N_POOL_GROUPS * POOL_GROUP_DIM
ROPE_THETA = 10000.0
N_GROUPS = 4
EXPERTS_PER_GROUP = 4
N_EXPERTS = N_GROUPS * EXPERTS_PER_GROUP
D_EXPERT = 512
EPS = 1e-6
LAMBDA_INIT = 0.8 - 0.6 * math.exp(-0.3 * 0)
LOG2_E = math.log2(math.e)

_PAIR_A = (0, 0, 0, 1, 1, 3)
_PAIR_B = (1, 2, 3, 3, 2, 2)
N_PAIRS = len(_PAIR_A)
N_BUCKETS = N_GROUPS * N_PAIRS
N_BUCKET_ROWS = 32
GATHER_UNROLL = 16

LANES = 128
NEG_BIG = -1e30
VMEM_LIMIT = 48 * 1024 * 1024

F32 = jnp.float32
BF16 = jnp.bfloat16

_HEADS_PER_STEP = 4
_IN_STEPS = N_HEADS // _HEADS_PER_STEP
_QKV_BLK = _HEADS_PER_STEP * V_DIM
_U_BLK = POOL_WIDTH // _IN_STEPS
_GATE_BLK = 512
_GATES_PER_STEP = 2 * D_MODEL // _IN_STEPS // _GATE_BLK
_O_K = D_MODEL // _QKV_BLK
_O_V = 2 * D_MODEL // _QKV_BLK
_O_U = 3 * D_MODEL // _U_BLK
_O_G = (3 * D_MODEL + POOL_WIDTH) // _GATE_BLK


def _params(*sem):
    return pltpu.CompilerParams(dimension_semantics=sem, vmem_limit_bytes=VMEM_LIMIT)


TOKEN_ROWS = D_MODEL // LANES


def _store_token_major(ref, val, rows):
    for s in range(TOKEN_ROWS):
        ref[pl.ds(s, rows, stride=TOKEN_ROWS), :] = val[:, s * LANES:(s + 1) * LANES]


def _load_token_major(ref, rows):
    return jnp.concatenate([ref[pl.ds(s, rows, stride=TOKEN_ROWS), :] for s in range(TOKEN_ROWS)], axis=1)


def _rope(z, cos, sin_signed, first_half):
    outs = []
    for c in range(z.shape[1] // LANES):
        zc = z[:, c * LANES:(c + 1) * LANES]
        rot = jnp.where(first_half, pltpu.roll(zc, LANES - HEAD_DIM // 2, 1),
                        pltpu.roll(zc, HEAD_DIM // 2, 1))
        outs.append(zc * cos + rot * sin_signed)
    return jnp.concatenate(outs, axis=1)


def _inproj_kernel(x_ref, meta_ref, g_ref, cos_ref, sin_ref, cosm_ref, sinm_ref, wq_ref, wk_ref, wv_ref,
                   wu_ref, wg0_ref, wg1_ref, bg_ref, q_ref, k_ref, v_ref, u_ref, gate_ref, qm_ref, km_ref,
                   vm_ref, um_ref, wq_sc, wk_sc, wv_sc, wu_sc, wg0_sc, wg1_sc):
    def normed(x):
        ms = jnp.mean(x * x, axis=-1, keepdims=True)
        return (x * lax.rsqrt(ms + EPS) * g_ref[...]).astype(BF16)

    def project(hn, cos, sin, q_out, k_out, v_out, u_out, gate_out):
        lane = lax.broadcasted_iota(jnp.int32, cos.shape, 1)
        first_half = jnp.bitwise_and(lane, HEAD_DIM - 1) < (HEAD_DIM // 2)
        if gate_out is not None:
            for blk, wg_sc in enumerate((wg0_sc, wg1_sc)):
                cols = slice(blk * _GATE_BLK, (blk + 1) * _GATE_BLK)
                zg = jnp.dot(hn, wg_sc[...], preferred_element_type=F32) + bg_ref[:, cols]
                gate_out[:, cols] = (1.0 / (1.0 + jnp.exp(-zg))).astype(BF16)
        zq = jnp.dot(hn, wq_sc[...], preferred_element_type=F32)
        q_out[...] = (_rope(zq, cos, sin, first_half) * (LOG2_E / math.sqrt(HEAD_DIM))).astype(BF16)
        zk = jnp.dot(hn, wk_sc[...], preferred_element_type=F32)
        k_out[...] = _rope(zk, cos, sin, first_half).astype(BF16)
        u_out[...] = jnp.dot(hn, wu_sc[...], preferred_element_type=F32)
        v_out[...] = jnp.dot(hn, wv_sc[...], preferred_element_type=F32).astype(BF16)

    @pl.when(pl.program_id(1) == 0)
    def _():
        for src, dst in ((wq_ref, wq_sc), (wk_ref, wk_sc), (wv_ref, wv_sc), (wu_ref, wu_sc),
                         (wg0_ref, wg0_sc), (wg1_ref, wg1_sc)):
            dst[...] = src[...].astype(BF16)
        project(normed(meta_ref[...]), cosm_ref[...], sinm_ref[...], qm_ref, km_ref, vm_ref, um_ref, None)

    project(normed(x_ref[...]), cos_ref[...], sin_ref[...], q_ref, k_ref, v_ref, u_ref, gate_ref)


def _inproj(x2, meta, g1, cos, sin, w_in, b_gate, tm):
    n = x2.shape[0]
    n_meta = meta.shape[0]
    n_pos_blocks = (cos.shape[0] - n_meta) // tm
    row = lambda j, i: (i, 0)
    pos = lambda j, i: (i % n_pos_blocks, 0)
    whole = lambda j, i: (0, 0)
    once = pl.Buffered(1)
    gate_cols = _GATES_PER_STEP * _GATE_BLK
    assert _GATES_PER_STEP == 2
    return pl.pallas_call(
        _inproj_kernel,
        grid=(_IN_STEPS, n // tm),
        in_specs=[
            pl.BlockSpec((tm, D_MODEL), row),
            pl.BlockSpec((n_meta, D_MODEL), whole),
            pl.BlockSpec((1, D_MODEL), whole),
            pl.BlockSpec((tm, LANES), pos),
            pl.BlockSpec((tm, LANES), pos),
            pl.BlockSpec((n_meta, LANES), whole),
            pl.BlockSpec((n_meta, LANES), whole),
            pl.BlockSpec((D_MODEL, _QKV_BLK), lambda j, i: (0, j), pipeline_mode=once),
            pl.BlockSpec((D_MODEL, _QKV_BLK), lambda j, i: (0, _O_K + j), pipeline_mode=once),
            pl.BlockSpec((D_MODEL, _QKV_BLK), lambda j, i: (0, _O_V + j), pipeline_mode=once),
            pl.BlockSpec((D_MODEL, _U_BLK), lambda j, i: (0, _O_U + j), pipeline_mode=once),
            pl.BlockSpec((D_MODEL, _GATE_BLK), lambda j, i: (0, _O_G + _GATES_PER_STEP * j),
                         pipeline_mode=once),
            pl.BlockSpec((D_MODEL, _GATE_BLK), lambda j, i: (0, _O_G + _GATES_PER_STEP * j + 1),
                         pipeline_mode=once),
            pl.BlockSpec((1, gate_cols), lambda j, i: (0, j)),
        ],
        out_specs=[
            pl.BlockSpec((tm, _QKV_BLK), lambda j, i: (i, j)),
            pl.BlockSpec((tm, _QKV_BLK), lambda j, i: (i, j)),
            pl.BlockSpec((tm, _QKV_BLK), lambda j, i: (i, j)),
            pl.BlockSpec((tm, _U_BLK), lambda j, i: (i, j)),
            pl.BlockSpec((tm, gate_cols), lambda j, i: (i, j)),
            pl.BlockSpec((n_meta, _QKV_BLK), lambda j, i: (0, j)),
            pl.BlockSpec((n_meta, _QKV_BLK), lambda j, i: (0, j)),
            pl.BlockSpec((n_meta, _QKV_BLK), lambda j, i: (0, j)),
            pl.BlockSpec((n_meta, _U_BLK), lambda j, i: (0, j)),
        ],
        out_shape=[
            jax.ShapeDtypeStruct((n, D_MODEL), BF16),
            jax.ShapeDtypeStruct((n, D_MODEL), BF16),
            jax.ShapeDtypeStruct((n, D_MODEL), BF16),
            jax.ShapeDtypeStruct((n, POOL_WIDTH), F32),
            jax.ShapeDtypeStruct((n, 2 * D_MODEL), BF16),
            jax.ShapeDtypeStruct((n_meta, D_MODEL), BF16),
            jax.ShapeDtypeStruct((n_meta, D_MODEL), BF16),
            jax.ShapeDtypeStruct((n_meta, D_MODEL), BF16),
            jax.ShapeDtypeStruct((n_meta, POOL_WIDTH), F32),
        ],
        scratch_shapes=[
            pltpu.VMEM((D_MODEL, _QKV_BLK), BF16),
            pltpu.VMEM((D_MODEL, _QKV_BLK), BF16),
            pltpu.VMEM((D_MODEL, _QKV_BLK), BF16),
            pltpu.VMEM((D_MODEL, _U_BLK), BF16),
            pltpu.VMEM((D_MODEL, _GATE_BLK), BF16),
            pltpu.VMEM((D_MODEL, _GATE_BLK), BF16),
        ],
        compiler_params=_params("arbitrary", "arbitrary"),
        name="inproj",
    )(x2, meta, g1, cos[n_meta:], sin[n_meta:], cos[:n_meta], sin[:n_meta], w_in, w_in, w_in, w_in, w_in,
      w_in, b_gate)


def _attn_kernel(lq1_ref, lk1_ref, lq2_ref, lk2_ref, subg_ref, q_ref, k_ref, v_ref, qm_ref, km_ref,
                 vm_ref, o_ref, *state_sc, bq, bk):
    seq = k_ref.shape[0]
    nq = seq // bq
    states = [tuple(state_sc[3 * g:3 * g + 3]) for g in range(nq + 1)]

    def rows_at(ref, meta_ref, p0, n):
        if p0 == 0:
            return jnp.concatenate([meta_ref[...], ref[0:n - N_META, :]], axis=0)
        real = ref[p0 - N_META:min(p0 - N_META + n, seq), :]
        if real.shape[0] < n:
            real = jnp.concatenate([real, jnp.zeros((n - real.shape[0], real.shape[1]), real.dtype)], axis=0)
        return real

    def block_rows(i):
        return (i * bq, bq) if i < nq else (nq * bq, N_META)

    def n_full_tiles(i):
        return (block_rows(i)[0] + 1) // bk

    def partial_tiles(i):
        p0, n = block_rows(i)
        if i == nq:
            return [(n_full_tiles(i) * bk, LANES, p0 - n_full_tiles(i) * bk)]
        return [(t * bk, bk, p0 - t * bk) for t in range(n_full_tiles(i), (p0 + n - 1) // bk + 1)]

    lam = (jnp.exp(jnp.sum(lq1_ref[...] * lk1_ref[...], axis=-1, keepdims=True))
           - jnp.exp(jnp.sum(lq2_ref[...] * lk2_ref[...], axis=-1, keepdims=True)) + LAMBDA_INIT)

    bias_cache = {}

    def causal_bias(n_rows, n_cols, offset):
        if (n_rows, n_cols, offset) not in bias_cache:
            r = lax.broadcasted_iota(jnp.int32, (n_rows, n_cols), 0)
            c = lax.broadcasted_iota(jnp.int32, (n_rows, n_cols), 1)
            bias = jnp.where(c <= r + offset, 0.0, NEG_BIG)
            bias_cache[(n_rows, n_cols, offset)] = jnp.concatenate([bias, bias], axis=0)
        return bias_cache[(n_rows, n_cols, offset)]

    def load_q(i, qq_sc):
        p0, n = block_rows(i)
        q = rows_at(q_ref, qm_ref, p0, n)
        lane = lax.broadcasted_iota(jnp.int32, q.shape, 1)
        zero = jnp.zeros_like(q)
        qq_sc[0:n, :] = jnp.where(lane < HEAD_DIM, q, zero)
        qq_sc[n:2 * n, :] = jnp.where(lane >= HEAD_DIM, q, zero)

    def step(state, kb, vb, bias, first):
        qq_sc, m_sc, acc_sc = state
        s = lax.dot_general(qq_sc[...], kb, (((1,), (1,)), ((), ())), preferred_element_type=F32)
        if bias is not None:
            s = s + bias
        n_tiles = s.shape[1] // LANES
        rm = s[:, 0:LANES]
        for t in range(1, n_tiles):
            rm = jnp.maximum(rm, s[:, t * LANES:(t + 1) * LANES])
        rmax = jnp.max(rm, axis=-1, keepdims=True)
        if first:
            m_new = jnp.broadcast_to(rmax, (s.shape[0], LANES))
        else:
            m_prev = m_sc[...]
            m_new = jnp.maximum(m_prev, rmax)
        p = jnp.exp2(s - jnp.concatenate([m_new] * n_tiles, axis=1))
        pv = jnp.dot(p.astype(BF16), vb, preferred_element_type=F32)
        if first:
            acc_sc[...] = pv
        else:
            alpha = jnp.exp2(m_prev - m_new)
            acc_sc[...] = jnp.concatenate([alpha, alpha], axis=1) * acc_sc[...] + pv
        m_sc[...] = m_new

    def chain_steps(i):
        n = block_rows(i)[1]
        steps = [(p0, width, causal_bias(n, width, offset)) for p0, width, offset in partial_tiles(i)]
        return steps + [(t * bk, bk, None) for t in range(n_full_tiles(i))]

    def key_step(state, p0, width, bias, first):
        kb = rows_at(k_ref, km_ref, p0, width)
        vb = jnp.concatenate([rows_at(v_ref, vm_ref, p0, width), jnp.ones((width, V_DIM), BF16)], axis=1)
        step(state, kb, vb, bias, first)

    def finish(i, acc_sc):
        p0, n = block_rows(i)
        o1 = acc_sc[0:n, 0:V_DIM] / acc_sc[0:n, V_DIM:2 * V_DIM]
        o2 = acc_sc[n:2 * n, 0:V_DIM] / acc_sc[n:2 * n, V_DIM:2 * V_DIM]
        o = o1 - lam * o2
        ms = jnp.mean(o * o, axis=-1, keepdims=True)
        o = (o * lax.rsqrt(ms + EPS) * subg_ref[...] * (1.0 - LAMBDA_INIT)).astype(BF16)
        if p0 == 0:
            o_ref[0:n - N_META, :] = o[N_META:n, :]
        else:
            o_ref[p0 - N_META:p0 - N_META + n, :] = o

    order = list(range(nq, -1, -1))
    plans = {i: chain_steps(i) for i in order}
    for i in order:
        load_q(i, states[i][0])
    for j in range(max(len(plan) for plan in plans.values())):
        for i in order:
            if j < len(plans[i]):
                key_step(states[i], *plans[i][j], first=(j == 0))
    for i in order:
        finish(i, states[i][2])


def _attention(q, k, v, qm, km, vm, lq1, lk1, lq2, lk2, subg, batch, seq, bq, bk):
    nq = seq // bq
    small = lambda b, h: (0, 0)
    head = lambda b, h: (b, h)
    meta = lambda b, h: (0, h)

    def state_scratch(rows):
        return [pltpu.VMEM((2 * rows, V_DIM), BF16), pltpu.VMEM((2 * rows, LANES), F32),
                pltpu.VMEM((2 * rows, 2 * V_DIM), F32)]

    return pl.pallas_call(
        functools.partial(_attn_kernel, bq=bq, bk=bk),
        grid=(batch, N_HEADS),
        in_specs=[
            pl.BlockSpec((1, HEAD_DIM), small),
            pl.BlockSpec((1, HEAD_DIM), small),
            pl.BlockSpec((1, HEAD_DIM), small),
            pl.BlockSpec((1, HEAD_DIM), small),
            pl.BlockSpec((1, V_DIM), small),
            pl.BlockSpec((seq, V_DIM), head),
            pl.BlockSpec((seq, V_DIM), head),
            pl.BlockSpec((seq, V_DIM), head),
            pl.BlockSpec((N_META, V_DIM), meta),
            pl.BlockSpec((N_META, V_DIM), meta),
            pl.BlockSpec((N_META, V_DIM), meta),
        ],
        out_specs=pl.BlockSpec((seq, V_DIM), head),
        out_shape=jax.ShapeDtypeStruct(q.shape, BF16),
        scratch_shapes=state_scratch(bq) * nq + state_scratch(N_META),
        compiler_params=_params("arbitrary", "arbitrary"),
        name="diff_attn",
    )(lq1, lk1, lq2, lk2, subg, q, k, v, qm, km, vm)


def _mix_kernel(x_ref, a_ref, u_ref, uprev_ref, umeta_ref, gate_ref, wa_ref, pw_ref, ps_ref, wp_ref,
                wo_ref, g2_ref, wr_ref, br_ref, h2c_ref, logit_ref, ext_sc, sum_a_sc, sum_b_sc, wfold_sc,
                *, tm, tiles_per_seq):
    i = pl.program_id(0)
    first = (i % tiles_per_seq) == 0

    @pl.when(i == 0)
    def _():
        for g in range(N_POOL_GROUPS):
            rows = slice(g * POOL_GROUP_DIM, (g + 1) * POOL_GROUP_DIM)
            left = pw_ref[g] * ps_ref[:, rows]
            right = wp_ref[rows, :]
            left_hi, right_hi = left.astype(BF16), right.astype(BF16)
            left_lo = (left - left_hi.astype(F32)).astype(BF16)
            right_lo = (right - right_hi.astype(F32)).astype(BF16)
            wfold_sc[rows, :] = (jnp.dot(left_hi, right_hi, preferred_element_type=F32)
                                 + jnp.dot(left_lo, right_hi, preferred_element_type=F32)
                                 + jnp.dot(left_hi, right_lo, preferred_element_type=F32)).astype(BF16)

    y_attn = jnp.dot(a_ref[...], wa_ref[...], preferred_element_type=F32)

    top = 2 * N_META
    for ref in (ext_sc, sum_a_sc, sum_b_sc):
        ref[0:N_META, :] = jnp.zeros((N_META, ref.shape[1]), F32)

    @pl.when(first)
    def _():
        ext_sc[N_META:top, :] = umeta_ref[...]

    @pl.when(jnp.logical_not(first))
    def _():
        ext_sc[N_META:top, :] = uprev_ref[...]

    ext_sc[top:top + tm, :] = u_ref[...]

    pooled = []
    for g, w in enumerate(POOL_WINDOWS):
        cs = slice(g * POOL_GROUP_DIM, (g + 1) * POOL_GROUP_DIM)
        src, lanes, shift = ext_sc, cs, 1
        for dst in (sum_a_sc, sum_b_sc, sum_a_sc, sum_b_sc):
            if shift >= w:
                break
            dst[N_META:top + tm, :] = (src[N_META:top + tm, lanes]
                                       + src[N_META - shift:top + tm - shift, lanes])
            src, lanes, shift = dst, slice(None), 2 * shift
        pooled.append((src[top:top + tm, lanes] * (1.0 / w) - ext_sc[top:top + tm, cs]).astype(BF16))
    y_pool = jnp.dot(jnp.concatenate(pooled, axis=1), wfold_sc[...], preferred_element_type=F32)
    mixed = (gate_ref[:, 0:D_MODEL].astype(F32) * y_attn
             + gate_ref[:, D_MODEL:2 * D_MODEL].astype(F32) * y_pool)
    h2 = x_ref[...] + jnp.dot(mixed.astype(BF16), wo_ref[...], preferred_element_type=F32)
    _store_token_major(h2c_ref, h2, tm)
    ms = jnp.mean(h2 * h2, axis=-1, keepdims=True)
    hn2 = h2 * lax.rsqrt(ms + EPS) * g2_ref[...]
    hn2_hi = hn2.astype(BF16)
    hn2_lo = (hn2 - hn2_hi.astype(F32)).astype(BF16)
    wr = wr_ref[...]
    wr_hi = wr.astype(BF16)
    wr_lo = (wr - wr_hi.astype(F32)).astype(BF16)
    wr_both = jnp.concatenate([wr_hi, wr_lo], axis=1)
    for half in range(2):
        rows = slice(half * tm // 2, (half + 1) * tm // 2)
        both = jnp.dot(hn2_hi[rows, :], wr_both, preferred_element_type=F32)
        logit_ref[rows, :] = (both[:, 0:LANES] + both[:, LANES:2 * LANES]
                              + jnp.dot(hn2_lo[rows, :], wr_hi, preferred_element_type=F32) + br_ref[...])


def _mix(x2, a, u, umeta, gates, w_attn, pool_w, pool_scale, w_pool, w_out, g2, wr, br, seq, tm):
    n = x2.shape[0]
    tiles_per_seq = seq // tm
    halo_blocks = tm // N_META
    const2 = lambda i: (0, 0)
    return pl.pallas_call(
        functools.partial(_mix_kernel, tm=tm, tiles_per_seq=tiles_per_seq),
        grid=(n // tm,),
        in_specs=[
            pl.BlockSpec((tm, D_MODEL), lambda i: (i, 0)),
            pl.BlockSpec((tm, D_MODEL), lambda i: (i, 0)),
            pl.BlockSpec((tm, POOL_WIDTH), lambda i: (i, 0)),
            pl.BlockSpec((N_META, POOL_WIDTH), lambda i: (jnp.maximum(i * halo_blocks - 1, 0), 0)),
            pl.BlockSpec((N_META, POOL_WIDTH), const2),
            pl.BlockSpec((tm, 2 * D_MODEL), lambda i: (i, 0)),
            pl.BlockSpec((D_MODEL, D_MODEL), const2, pipeline_mode=pl.Buffered(1)),
            pl.BlockSpec((N_POOL_GROUPS, POOL_GROUP_DIM, POOL_GROUP_DIM), lambda i: (0, 0, 0),
                         pipeline_mode=pl.Buffered(1)),
            pl.BlockSpec((1, POOL_WIDTH), const2),
            pl.BlockSpec((POOL_WIDTH, D_MODEL), const2, pipeline_mode=pl.Buffered(1)),
            pl.BlockSpec((D_MODEL, D_MODEL), const2, pipeline_mode=pl.Buffered(1)),
            pl.BlockSpec((1, D_MODEL), const2),
            pl.BlockSpec((D_MODEL, LANES), const2, pipeline_mode=pl.Buffered(1)),
            pl.BlockSpec((1, LANES), const2),
        ],
        out_specs=[
            pl.BlockSpec((tm * TOKEN_ROWS, LANES), lambda i: (i, 0)),
            pl.BlockSpec((tm, LANES), lambda i: (i, 0)),
        ],
        out_shape=[
            jax.ShapeDtypeStruct((n * TOKEN_ROWS, LANES), F32),
            jax.ShapeDtypeStruct((n, LANES), F32),
        ],
        scratch_shapes=[pltpu.VMEM((tm + 2 * N_META, POOL_WIDTH), F32),
                        pltpu.VMEM((tm + 2 * N_META, POOL_GROUP_DIM), F32),
                        pltpu.VMEM((tm + 2 * N_META, POOL_GROUP_DIM), F32),
                        pltpu.VMEM((POOL_WIDTH, D_MODEL), BF16)],
        compiler_params=_params("arbitrary"),
        name="mix",
    )(x2, a, u, u, umeta, gates, w_attn, pool_w, pool_scale, w_pool, w_out, g2, wr, br)


_TBL_EA, _TBL_EB, _TBL_NEW_A, _TBL_NEW_B, _TBL_NVALID, _TBL_NUSED = range(6)


def _route_kernel(logit_ref, tri_ref, wts_ref, dest_ref, tbl_ref, carry_sc, ids_sc, *, tme):
    i = pl.program_id(1)
    pl.when(pl.program_id(0) == 0)(functools.partial(_route_classify, i, logit_ref, tri_ref, wts_ref,
                                                     carry_sc, ids_sc))
    pl.when(pl.program_id(0) == 1)(functools.partial(_route_place, i, dest_ref, tbl_ref, carry_sc,
                                                     ids_sc, tme))


def _route_classify(i, logit_ref, tri_ref, wts_ref, carry_sc, ids_sc):
    @pl.when(i == 0)
    def _():
        carry_sc[...] = jnp.zeros_like(carry_sc)

    lg = logit_ref[...].T
    g = [lg[r:r + 1, :] for r in range(N_GROUPS)]
    gmax = functools.reduce(jnp.maximum, g)
    gsel = jnp.full_like(g[0], N_GROUPS - 1).astype(jnp.int32)
    for r in range(N_GROUPS - 2, -1, -1):
        gsel = jnp.where(g[r] == gmax, r, gsel)
    p_group = 1.0 / functools.reduce(lambda a, b: a + b, [jnp.exp(x - gmax) for x in g])
    e = []
    for jj in range(EXPERTS_PER_GROUP):
        v = lg[N_GROUPS + jj:N_GROUPS + jj + 1, :]
        for r in range(1, N_GROUPS):
            row = N_GROUPS + r * EXPERTS_PER_GROUP + jj
            v = jnp.where(gsel == r, lg[row:row + 1, :], v)
        e.append(v)
    v1 = functools.reduce(jnp.maximum, e)
    i1 = jnp.full_like(gsel, EXPERTS_PER_GROUP - 1)
    for jj in range(EXPERTS_PER_GROUP - 2, -1, -1):
        i1 = jnp.where(e[jj] == v1, jj, i1)
    rest = [jnp.where(i1 == jj, -jnp.inf, e[jj]) for jj in range(EXPERTS_PER_GROUP)]
    v2 = functools.reduce(jnp.maximum, rest)
    i2 = jnp.full_like(gsel, EXPERTS_PER_GROUP - 1)
    for jj in range(EXPERTS_PER_GROUP - 2, -1, -1):
        i2 = jnp.where(jnp.logical_and(rest[jj] == v2, i1 != jj), jj, i2)
    t = jnp.exp(v2 - v1)
    w1 = p_group / (1.0 + t)
    w2 = p_group * t / (1.0 + t)
    lo = jnp.minimum(i1, i2)
    hi = jnp.maximum(i1, i2)
    pair = jnp.where(lo == 0, hi - 1, jnp.where(lo == 1, jnp.where(hi == 3, 3, 4), 5))
    bucket = gsel * N_PAIRS + pair
    w_of_lo = jnp.where(i1 < i2, w1, w2)
    w_of_hi = jnp.where(i1 < i2, w2, w1)
    swapped = pair == 5
    w_a = jnp.where(swapped, w_of_hi, w_of_lo)
    w_b = jnp.where(swapped, w_of_lo, w_of_hi)

    tl = lg.shape[1]
    bid = lax.broadcasted_iota(jnp.int32, (N_BUCKET_ROWS, tl), 0)
    hot = bid == bucket
    hot_b = jnp.where(hot, 1.0, 0.0).astype(BF16)
    prefix = jnp.dot(hot_b, tri_ref[...], preferred_element_type=F32)
    carry = carry_sc[...]
    before = prefix + jnp.concatenate([carry] * (tl // LANES), axis=1)
    rank = jnp.sum(jnp.where(hot, before, 0.0), axis=0, keepdims=True)
    carry_sc[...] = carry + jnp.dot(hot_b, jnp.ones((tl, LANES), BF16), preferred_element_type=F32)
    pad = jnp.zeros((6, tl), F32)
    ids_sc[i] = jnp.concatenate([bucket.astype(F32), rank, pad], axis=0)
    wts_ref[...] = jnp.concatenate([w_a, w_b, pad], axis=0)


def _route_place(i, dest_ref, tbl_ref, carry_sc, ids_sc, tme):
    cnt = carry_sc[...]
    tiles = jnp.floor((cnt + (tme - 1)) * (1.0 / tme))
    r = lax.broadcasted_iota(jnp.int32, (N_BUCKET_ROWS, N_BUCKET_ROWS), 0)
    c = lax.broadcasted_iota(jnp.int32, (N_BUCKET_ROWS, N_BUCKET_ROWS), 1)
    tile_end = jnp.dot(jnp.where(c <= r, 1.0, 0.0).astype(BF16), tiles.astype(BF16),
                       preferred_element_type=F32)
    tile_start = tile_end - tiles

    ids = ids_sc[i]
    bucket, rank = ids[0:1, :], ids[1:2, :]
    tl = ids.shape[1]
    bid = lax.broadcasted_iota(jnp.int32, (N_BUCKET_ROWS, tl), 0).astype(F32)
    row_start = jnp.concatenate([tile_start * tme] * (tl // LANES), axis=1)
    dest = rank + jnp.sum(jnp.where(bid == bucket, row_start, 0.0), axis=0, keepdims=True)
    dest_ref[...] = jnp.concatenate([dest, jnp.zeros((7, tl), F32)], axis=0).astype(jnp.int32)

    lane = lax.broadcasted_iota(jnp.int32, (1, LANES), 1).astype(F32)
    n_used = tile_end[N_BUCKET_ROWS - 1:N_BUCKET_ROWS, :]
    last = jnp.minimum(lane, n_used - 1.0)
    tb = jnp.sum(jnp.where(tile_end <= last, 1.0, 0.0), axis=0, keepdims=True)
    tb = jnp.minimum(tb, N_BUCKETS - 1.0)
    group = functools.reduce(lambda a, b: a + b,
                             [jnp.where(tb >= g * N_PAIRS, 1.0, 0.0) for g in range(1, N_GROUPS)])
    pair = tb - N_PAIRS * group

    def lookup(table):
        out = jnp.full_like(pair, float(table[-1]))
        for k in range(len(table) - 2, -1, -1):
            out = jnp.where(pair == k, float(table[k]), out)
        return EXPERTS_PER_GROUP * group + out

    e_a, e_b = lookup(_PAIR_A), lookup(_PAIR_B)
    changed = lambda e: jnp.where(lane == 0.0, 1.0, jnp.where(e != pltpu.roll(e, 1, 1), 1.0, 0.0))
    bid128 = lax.broadcasted_iota(jnp.int32, (N_BUCKET_ROWS, LANES), 0).astype(F32)
    at_tile = lambda col: jnp.sum(jnp.where(bid128 == tb, col, 0.0), axis=0, keepdims=True)
    n_valid = jnp.clip(at_tile(cnt) - tme * (lane - at_tile(tile_start)), 0.0, float(tme))
    rows = [e_a, e_b, changed(e_a), changed(e_b), n_valid, n_used, jnp.zeros_like(lane),
            jnp.zeros_like(lane)]
    tbl_ref[...] = jnp.concatenate(rows, axis=0).astype(jnp.int32)


def _route(logits, tl, tme):
    n = logits.shape[0]
    nt = n // tl
    tri = jnp.asarray(np.triu(np.ones((tl, tl), np.float32), k=1), dtype=BF16)
    first_pass = lambda p, i: i * (1 - p) + (nt - 1) * p
    return pl.pallas_call(
        functools.partial(_route_kernel, tme=tme),
        grid=(2, nt),
        in_specs=[pl.BlockSpec((tl, LANES), lambda p, i: (first_pass(p, i), 0)),
                  pl.BlockSpec((tl, tl), lambda p, i: (0, 0))],
        out_specs=[pl.BlockSpec((8, tl), lambda p, i: (0, first_pass(p, i))),
                   pl.BlockSpec((8, tl), lambda p, i: (0, i * p)),
                   pl.BlockSpec((8, LANES), lambda p, i: (0, 0))],
        out_shape=[jax.ShapeDtypeStruct((8, n), F32),
                   jax.ShapeDtypeStruct((8, n), jnp.int32),
                   jax.ShapeDtypeStruct((8, LANES), jnp.int32)],
        scratch_shapes=[pltpu.VMEM((N_BUCKET_ROWS, LANES), F32), pltpu.VMEM((nt, 8, tl), F32)],
        compiler_params=_params("arbitrary", "arbitrary"),
        name="route",
    )(logits, tri)


def _token_copy(src_hbm, src_token, dst_buf, dst_row, sem):
    return pltpu.make_async_copy(
        src_hbm.at[pl.ds(pl.multiple_of(src_token * TOKEN_ROWS, TOKEN_ROWS), TOKEN_ROWS), :],
        dst_buf.at[pl.ds(pl.multiple_of(dst_row * TOKEN_ROWS, TOKEN_ROWS), TOKEN_ROWS), :],
        sem)


def _start_gather(idx_ref, first, n_groups, src_hbm, dst_buf, sem):
    def body(rr, carry):
        for uu in range(GATHER_UNROLL):
            r = rr * GATHER_UNROLL + uu
            _token_copy(src_hbm, idx_ref[first + r], dst_buf, r, sem).start(priority=uu % 2)
        return carry

    lax.fori_loop(0, n_groups, body, 0)


def _start_padded_gather(idx_ref, first, n_valid, rows, src_hbm, dst_buf, sem):
    n_groups = (n_valid + GATHER_UNROLL - 1) // GATHER_UNROLL
    _start_gather(idx_ref, first, n_groups, src_hbm, dst_buf, sem)
    pad_groups = rows // GATHER_UNROLL - n_groups
    off = n_groups * GATHER_UNROLL
    size = rows // 2
    while size >= GATHER_UNROLL:
        take = jnp.bitwise_and(pad_groups, size // GATHER_UNROLL) != 0

        @pl.when(take)
        def _(off=off, size=size):
            pltpu.make_async_copy(
                src_hbm.at[pl.ds(0, size * TOKEN_ROWS), :],
                dst_buf.at[pl.ds(pl.multiple_of(off * TOKEN_ROWS, GATHER_UNROLL * TOKEN_ROWS),
                                 size * TOKEN_ROWS), :],
                sem).start()

        off = off + jnp.where(take, size, 0)
        size //= 2


def _wait_gather(rows, src_hbm, dst_buf, sem):
    pltpu.make_async_copy(src_hbm.at[pl.ds(0, rows * TOKEN_ROWS), :], dst_buf, sem).wait()


def _moe_kernel(tbl_ref, src_ref, h2c_hbm, wts_ref, g2_ref, wg_a_ref, wu_a_ref, wd_a_ref, wg_b_ref,
                wu_b_ref, wd_b_ref, yc_ref, xbuf, wg_a_sc, wu_a_sc, wd_a_sc, wg_b_sc, wu_b_sc, wd_b_sc,
                sem, *, tme):
    i = pl.program_id(0)
    n_used = tbl_ref[_TBL_NUSED, 0]
    slot = jnp.bitwise_and(i, 1)

    @pl.when(i == 0)
    def _():
        _start_padded_gather(src_ref, 0, tbl_ref[_TBL_NVALID, 0], tme, h2c_hbm, xbuf.at[0], sem.at[0])

    @pl.when(i + 1 < n_used)
    def _():
        _start_padded_gather(src_ref, (i + 1) * tme, tbl_ref[_TBL_NVALID, i + 1], tme, h2c_hbm,
                             xbuf.at[1 - slot], sem.at[1 - slot])

    @pl.when(tbl_ref[_TBL_NEW_A, i] == 1)
    def _():
        for src, dst in ((wg_a_ref, wg_a_sc), (wu_a_ref, wu_a_sc), (wd_a_ref, wd_a_sc)):
            dst[...] = src[0].astype(BF16)

    @pl.when(tbl_ref[_TBL_NEW_B, i] == 1)
    def _():
        for src, dst in ((wg_b_ref, wg_b_sc), (wu_b_ref, wu_b_sc), (wd_b_ref, wd_b_sc)):
            dst[...] = src[0].astype(BF16)

    def tile(rows):
        _wait_gather(tme, h2c_hbm, xbuf.at[slot], sem.at[slot])
        x = _load_token_major(xbuf.at[slot], rows)
        ms = jnp.mean(x * x, axis=-1, keepdims=True)
        hn = (x * lax.rsqrt(ms + EPS) * g2_ref[...]).astype(BF16)
        wts = wts_ref[0:rows, :]

        def expert_act(wg_sc, wu_sc, w):
            gate = jnp.dot(hn, wg_sc[...], preferred_element_type=F32)
            up = jnp.dot(hn, wu_sc[...], preferred_element_type=F32)
            return (gate / (1.0 + jnp.exp(-gate)) * up * w).astype(BF16)

        y = (jnp.dot(expert_act(wg_a_sc, wu_a_sc, wts[:, 0:1]), wd_a_sc[...], preferred_element_type=F32)
             + jnp.dot(expert_act(wg_b_sc, wu_b_sc, wts[:, 1:2]), wd_b_sc[...], preferred_element_type=F32))
        _store_token_major(yc_ref, y, rows)
        if rows < tme:
            yc_ref[rows * TOKEN_ROWS:tme * TOKEN_ROWS, :] = jnp.zeros(((tme - rows) * TOKEN_ROWS, LANES), F32)

    n_valid = tbl_ref[_TBL_NVALID, i]
    pl.when(jnp.logical_and(i < n_used, n_valid > tme // 2))(functools.partial(tile, tme))
    pl.when(jnp.logical_and(i < n_used, n_valid <= tme // 2))(functools.partial(tile, tme // 2))

    @pl.when(i >= n_used)
    def _():
        yc_ref[...] = jnp.zeros_like(yc_ref)


def _moe(tbl, src, h2c, wts_sorted, g2, wg, wu, wd, tme, n_tiles):
    used = lambda i, tbl, src: (jnp.minimum(i, tbl[_TBL_NUSED, 0] - 1), 0)
    w_a = lambda i, tbl, src: (tbl[_TBL_EA, i], 0, 0)
    w_b = lambda i, tbl, src: (tbl[_TBL_EB, i], 0, 0)
    up_shape, down_shape = (D_MODEL, D_EXPERT), (D_EXPERT, D_MODEL)
    grid_spec = pltpu.PrefetchScalarGridSpec(
        num_scalar_prefetch=2,
        grid=(n_tiles,),
        in_specs=[
            pl.BlockSpec(memory_space=pl.ANY),
            pl.BlockSpec((tme, wts_sorted.shape[1]), used),
            pl.BlockSpec((1, D_MODEL), lambda i, tbl, src: (0, 0)),
            pl.BlockSpec((1,) + up_shape, w_a),
            pl.BlockSpec((1,) + up_shape, w_a),
            pl.BlockSpec((1,) + down_shape, w_a),
            pl.BlockSpec((1,) + up_shape, w_b),
            pl.BlockSpec((1,) + up_shape, w_b),
            pl.BlockSpec((1,) + down_shape, w_b),
        ],
        out_specs=pl.BlockSpec((tme * TOKEN_ROWS, LANES), lambda i, tbl, src: (i, 0)),
        scratch_shapes=[pltpu.VMEM((2, tme * TOKEN_ROWS, LANES), F32)]
        + [pltpu.VMEM(s, BF16) for s in (up_shape, up_shape, down_shape) * 2]
        + [pltpu.SemaphoreType.DMA((2,))],
    )
    return pl.pallas_call(
        functools.partial(_moe_kernel, tme=tme),
        grid_spec=grid_spec,
        out_shape=jax.ShapeDtypeStruct((n_tiles * tme * TOKEN_ROWS, LANES), F32),
        compiler_params=_params("arbitrary"),
        name="moe",
    )(tbl, src, h2c, wts_sorted, g2, wg, wu, wd, wg, wu, wd)


def _final_kernel(dest_ref, h2c_ref, yc_hbm, gf_ref, o_ref, ybuf, sem, *, tmf):
    i = pl.program_id(0)
    slot = jnp.bitwise_and(i, 1)

    @pl.when(i == 0)
    def _():
        _start_gather(dest_ref, 0, tmf // GATHER_UNROLL, yc_hbm, ybuf.at[0], sem.at[0])

    @pl.when(i + 1 < pl.num_programs(0))
    def _():
        _start_gather(dest_ref, (i + 1) * tmf, tmf // GATHER_UNROLL, yc_hbm, ybuf.at[1 - slot],
                      sem.at[1 - slot])

    _wait_gather(tmf, yc_hbm, ybuf.at[slot], sem.at[slot])
    h = _load_token_major(h2c_ref, tmf) + _load_token_major(ybuf.at[slot], tmf)
    ms = jnp.mean(h * h, axis=-1, keepdims=True)
    o_ref[...] = h * lax.rsqrt(ms + EPS) * gf_ref[...]


def _final(dest, h2c, yc, gf, tmf):
    n = h2c.shape[0] // TOKEN_ROWS
    grid_spec = pltpu.PrefetchScalarGridSpec(
        num_scalar_prefetch=1,
        grid=(n // tmf,),
        in_specs=[
            pl.BlockSpec((tmf * TOKEN_ROWS, LANES), lambda i, dest: (i, 0)),
            pl.BlockSpec(memory_space=pl.ANY),
            pl.BlockSpec((1, D_MODEL), lambda i, dest: (0, 0)),
        ],
        out_specs=pl.BlockSpec((tmf, D_MODEL), lambda i, dest: (i, 0)),
        scratch_shapes=[pltpu.VMEM((2, tmf * TOKEN_ROWS, LANES), F32), pltpu.SemaphoreType.DMA((2,))],
    )
    return pl.pallas_call(
        functools.partial(_final_kernel, tmf=tmf),
        grid_spec=grid_spec,
        out_shape=jax.ShapeDtypeStruct((n, D_MODEL), F32),
        compiler_params=_params("arbitrary"),
        name="final",
    )(dest, h2c, yc, gf)


def _rope_tables(t):
    inv = 1.0 / (ROPE_THETA ** (np.arange(0, HEAD_DIM, 2, dtype=np.float64) / HEAD_DIM))
    ang = np.arange(t, dtype=np.float64)[:, None] * inv[None, :]
    ang = np.concatenate([ang, ang, ang, ang], axis=-1)
    sign = np.where((np.arange(LANES) % HEAD_DIM) < HEAD_DIM // 2, -1.0, 1.0)
    return (jnp.asarray(np.cos(ang), dtype=F32), jnp.asarray(np.sin(ang) * sign[None, :], dtype=F32))


def kernel(x, meta, norm1_g, w_in, b_gate, lambda_q1, lambda_k1, lambda_q2, lambda_k2, subln_g, pool_w,
           pool_scale, w_attn_br, w_pool_br, w_out, norm2_g, w_router_group, b_router_group,
           w_router_expert, b_router_expert, w_e_gate, w_e_up, w_e_down, final_g):
    batch, seq, d = x.shape
    n = batch * seq
    assert w_in.shape[0] == 1 and d == D_MODEL and meta.shape == (N_META, D_MODEL)
    assert seq % 1024 == 0 and n < 2 ** 24
    x2 = x.reshape(n, d)
    cos, sin = _rope_tables(seq + N_META)
    q, k, v, u, gates, qm, km, vm, um = _inproj(x2, meta, norm1_g, cos, sin, w_in[0], b_gate, tm=1024)
    a = _attention(q, k, v, qm, km, vm, lambda_q1, lambda_k1, lambda_q2, lambda_k2, subln_g, batch, seq,
                   bq=256, bk=256)

    n_router = N_GROUPS + N_EXPERTS
    wr = jnp.pad(jnp.concatenate([w_router_group[0], w_router_expert[0]], axis=1),
                 ((0, 0), (0, LANES - n_router)))
    br = jnp.pad(jnp.concatenate([b_router_group[0], b_router_expert[0]]), (0, LANES - n_router))
    h2c, logits = _mix(x2, a, u, um, gates, w_attn_br[0].astype(BF16), pool_w[0],
                       pool_scale, w_pool_br[0], w_out[0].astype(BF16), norm2_g,
                       wr, br.reshape(1, LANES), seq, tm=1024)
    tme = 512
    n_tiles = n // tme + N_BUCKETS
    assert n_tiles <= LANES
    wts, dest, tile_table = _route(logits, tl=1024, tme=tme)
    dest = dest[0]
    per_token = jnp.concatenate([wts[0:2], jnp.arange(n, dtype=F32)[None, :]], axis=0).T
    per_row = jnp.zeros((n_tiles * tme, 3), F32).at[dest].set(per_token)
    src = per_row[:, 2].astype(jnp.int32)

    yc = _moe(tile_table, src, h2c, per_row, norm2_g, w_e_gate[0], w_e_up[0], w_e_down[0], tme, n_tiles)
    out = _final(dest, h2c, yc, final_g.reshape(1, d), tmf=512)
    return out.reshape(batch, seq, d)
```

```python
import functools
import math

import jax
import jax.numpy as jnp
import numpy as np
from jax import lax
from jax.experimental import pallas as pl
from jax.experimental.pallas import tpu as pltpu

D_MODEL = 1024
N_META = 16
N_HEADS = 8
HEAD_DIM = 64
V_DIM = 2 * HEAD_DIM
POOL_WINDOWS = (2, 4, 8, 16)
N_POOL_GROUPS = len(POOL_WINDOWS)
POOL_GROUP_DIM = 128
POOL_WIDTH = N_POOL_GROUPS * POOL_GROUP_DIM
ROPE_THETA = 10000.0
N_GROUPS = 4
EXPERTS_PER_GROUP = 4
N_EXPERTS = N_GROUPS * EXPERTS_PER_GROUP
D_EXPERT = 512
EPS = 1e-6
LAMBDA_INIT = 0.8 - 0.6 * math.exp(-0.3 * 0)
LOG2_E = math.log2(math.e)

_PAIR_A = (0, 0, 0, 1, 1, 3)
_PAIR_B = (1, 2, 3, 3, 2, 2)
N_PAIRS = len(_PAIR_A)
N_BUCKETS = N_GROUPS * N_PAIRS
N_BUCKET_ROWS = 32
GATHER_UNROLL = 16

LANES = 128
NEG_BIG = -1e30
VMEM_LIMIT = 48 * 1024 * 1024

F32 = jnp.float32
BF16 = jnp.bfloat16

_HEADS_PER_STEP = 4
_IN_STEPS = N_HEADS // _HEADS_PER_STEP
_QKV_BLK = _HEADS_PER_STEP * V_DIM
_U_BLK = POOL_WIDTH // _IN_STEPS
_GATE_BLK = 512
_GATES_PER_STEP = 2 * D_MODEL // _IN_STEPS // _GATE_BLK
_O_K = D_MODEL // _QKV_BLK
_O_V = 2 * D_MODEL // _QKV_BLK
_O_U = 3 * D_MODEL // _U_BLK
_O_G = (3 * D_MODEL + POOL_WIDTH) // _GATE_BLK


def _params(*sem):
    return pltpu.CompilerParams(dimension_semantics=sem, vmem_limit_bytes=VMEM_LIMIT)


TOKEN_ROWS = D_MODEL // LANES


def _store_token_major(ref, val, rows):
    for s in range(TOKEN_ROWS):
        ref[pl.ds(s, rows, stride=TOKEN_ROWS), :] = val[:, s * LANES:(s + 1) * LANES]


def _load_token_major(ref, rows):
    return jnp.concatenate([ref[pl.ds(s, rows, stride=TOKEN_ROWS), :] for s in range(TOKEN_ROWS)], axis=1)


def _rope(z, cos, sin_signed, first_half):
    outs = []
    for c in range(z.shape[1] // LANES):
        zc = z[:, c * LANES:(c + 1) * LANES]
        rot = jnp.where(first_half, pltpu.roll(zc, LANES - HEAD_DIM // 2, 1),
                        pltpu.roll(zc, HEAD_DIM // 2, 1))
        outs.append(zc * cos + rot * sin_signed)
    return jnp.concatenate(outs, axis=1)


def _inproj_kernel(x_ref, meta_ref, g_ref, cos_ref, sin_ref, cosm_ref, sinm_ref, wq_ref, wk_ref, wv_ref,
                   wu_ref, wg0_ref, wg1_ref, bg_ref, q_ref, k_ref, v_ref, u_ref, gate_ref, qm_ref, km_ref,
                   vm_ref, um_ref, wq_sc, wk_sc, wv_sc, wu_sc, wg0_sc, wg1_sc):
    def normed(x):
        ms = jnp.mean(x * x, axis=-1, keepdims=True)
        return (x * lax.rsqrt(ms + EPS) * g_ref[...]).astype(BF16)

    def project(hn, cos, sin, q_out, k_out, v_out, u_out, gate_out):
        lane = lax.broadcasted_iota(jnp.int32, cos.shape, 1)
        first_half = jnp.bitwise_and(lane, HEAD_DIM - 1) < (HEAD_DIM // 2)
        if gate_out is not None:
            for blk, wg_sc in enumerate((wg0_sc, wg1_sc)):
                cols = slice(blk * _GATE_BLK, (blk + 1) * _GATE_BLK)
                zg = jnp.dot(hn, wg_sc[...], preferred_element_type=F32) + bg_ref[:, cols]
                gate_out[:, cols] = (1.0 / (1.0 + jnp.exp(-zg))).astype(BF16)
        zq = jnp.dot(hn, wq_sc[...], preferred_element_type=F32)
        q_out[...] = (_rope(zq, cos, sin, first_half) * (LOG2_E / math.sqrt(HEAD_DIM))).astype(BF16)
        zk = jnp.dot(hn, wk_sc[...], preferred_element_type=F32)
        k_out[...] = _rope(zk, cos, sin, first_half).astype(BF16)
        u_out[...] = jnp.dot(hn, wu_sc[...], preferred_element_type=F32)
        v_out[...] = jnp.dot(hn, wv_sc[...], preferred_element_type=F32).astype(BF16)

    @pl.when(pl.program_id(1) == 0)
    def _():
        for src, dst in ((wq_ref, wq_sc), (wk_ref, wk_sc), (wv_ref, wv_sc), (wu_ref, wu_sc),
                         (wg0_ref, wg0_sc), (wg1_ref, wg1_sc)):
            dst[...] = src[...].astype(BF16)
        project(normed(meta_ref[...]), cosm_ref[...], sinm_ref[...], qm_ref, km_ref, vm_ref, um_ref, None)

    project(normed(x_ref[...]), cos_ref[...], sin_ref[...], q_ref, k_ref, v_ref, u_ref, gate_ref)


def _inproj(x2, meta, g1, cos, sin, w_in, b_gate, tm):
    n = x2.shape[0]
    n_meta = meta.shape[0]
    n_pos_blocks = (cos.shape[0] - n_meta) // tm
    row = lambda j, i: (i, 0)
    pos = lambda j, i: (i % n_pos_blocks, 0)
    whole = lambda j, i: (0, 0)
    once = pl.Buffered(1)
    gate_cols = _GATES_PER_STEP * _GATE_BLK
    assert _GATES_PER_STEP == 2
    return pl.pallas_call(
        _inproj_kernel,
        grid=(_IN_STEPS, n // tm),
        in_specs=[
            pl.BlockSpec((tm, D_MODEL), row),
            pl.BlockSpec((n_meta, D_MODEL), whole),
            pl.BlockSpec((1, D_MODEL), whole),
            pl.BlockSpec((tm, LANES), pos),
            pl.BlockSpec((tm, LANES), pos),
            pl.BlockSpec((n_meta, LANES), whole),
            pl.BlockSpec((n_meta, LANES), whole),
            pl.BlockSpec((D_MODEL, _QKV_BLK), lambda j, i: (0, j), pipeline_mode=once),
            pl.BlockSpec((D_MODEL, _QKV_BLK), lambda j, i: (0, _O_K + j), pipeline_mode=once),
            pl.BlockSpec((D_MODEL, _QKV_BLK), lambda j, i: (0, _O_V + j), pipeline_mode=once),
            pl.BlockSpec((D_MODEL, _U_BLK), lambda j, i: (0, _O_U + j), pipeline_mode=once),
            pl.BlockSpec((D_MODEL, _GATE_BLK), lambda j, i: (0, _O_G + _GATES_PER_STEP * j),
                         pipeline_mode=once),
            pl.BlockSpec((D_MODEL, _GATE_BLK), lambda j, i: (0, _O_G + _GATES_PER_STEP * j + 1),
                         pipeline_mode=once),
            pl.BlockSpec((1, gate_cols), lambda j, i: (0, j)),
        ],
        out_specs=[
            pl.BlockSpec((tm, _QKV_BLK), lambda j, i: (i, j)),
            pl.BlockSpec((tm, _QKV_BLK), lambda j, i: (i, j)),
            pl.BlockSpec((tm, _QKV_BLK), lambda j, i: (i, j)),
            pl.BlockSpec((tm, _U_BLK), lambda j, i: (i, j)),
            pl.BlockSpec((tm, gate_cols), lambda j, i: (i, j)),
            pl.BlockSpec((n_meta, _QKV_BLK), lambda j, i: (0, j)),
            pl.BlockSpec((n_meta, _QKV_BLK), lambda j, i: (0, j)),
            pl.BlockSpec((n_meta, _QKV_BLK), lambda j, i: (0, j)),
            pl.BlockSpec((n_meta, _U_BLK), lambda j, i: (0, j)),
        ],
        out_shape=[
            jax.ShapeDtypeStruct((n, D_MODEL), BF16),
            jax.ShapeDtypeStruct((n, D_MODEL), BF16),
            jax.ShapeDtypeStruct((n, D_MODEL), BF16),
            jax.ShapeDtypeStruct((n, POOL_WIDTH), F32),
            jax.ShapeDtypeStruct((n, 2 * D_MODEL), BF16),
            jax.ShapeDtypeStruct((n_meta, D_MODEL), BF16),
            jax.ShapeDtypeStruct((n_meta, D_MODEL), BF16),
            jax.ShapeDtypeStruct((n_meta, D_MODEL), BF16),
            jax.ShapeDtypeStruct((n_meta, POOL_WIDTH), F32),
        ],
        scratch_shapes=[
            pltpu.VMEM((D_MODEL, _QKV_BLK), BF16),
            pltpu.VMEM((D_MODEL, _QKV_BLK), BF16),
            pltpu.VMEM((D_MODEL, _QKV_BLK), BF16),
            pltpu.VMEM((D_MODEL, _U_BLK), BF16),
            pltpu.VMEM((D_MODEL, _GATE_BLK), BF16),
            pltpu.VMEM((D_MODEL, _GATE_BLK), BF16),
        ],
        compiler_params=_params("arbitrary", "arbitrary"),
        name="inproj",
    )(x2, meta, g1, cos[n_meta:], sin[n_meta:], cos[:n_meta], sin[:n_meta], w_in, w_in, w_in, w_in, w_in,
      w_in, b_gate)


def _attn_kernel(lq1_ref, lk1_ref, lq2_ref, lk2_ref, subg_ref, q_ref, k_ref, v_ref, qm_ref, km_ref,
                 vm_ref, o_ref, *state_sc, bq, bk):
    seq = k_ref.shape[0]
    nq = seq // bq
    states = [tuple(state_sc[3 * g:3 * g + 3]) for g in range(nq + 1)]

    def rows_at(ref, meta_ref, p0, n):
        if p0 == 0:
            return jnp.concatenate([meta_ref[...], ref[0:n - N_META, :]], axis=0)
        real = ref[p0 - N_META:min(p0 - N_META + n, seq), :]
        if real.shape[0] < n:
            real = jnp.concatenate([real, jnp.zeros((n - real.shape[0], real.shape[1]), real.dtype)], axis=0)
        return real

    def block_rows(i):
        return (i * bq, bq) if i < nq else (nq * bq, N_META)

    def n_full_tiles(i):
        return (block_rows(i)[0] + 1) // bk

    def partial_tiles(i):
        p0, n = block_rows(i)
        if i == nq:
            return [(n_full_tiles(i) * bk, LANES, p0 - n_full_tiles(i) * bk)]
        return [(t * bk, bk, p0 - t * bk) for t in range(n_full_tiles(i), (p0 + n - 1) // bk + 1)]

    lam = (jnp.exp(jnp.sum(lq1_ref[...] * lk1_ref[...], axis=-1, keepdims=True))
           - jnp.exp(jnp.sum(lq2_ref[...] * lk2_ref[...], axis=-1, keepdims=True)) + LAMBDA_INIT)

    bias_cache = {}

    def causal_bias(n_rows, n_cols, offset):
        if (n_rows, n_cols, offset) not in bias_cache:
            r = lax.broadcasted_iota(jnp.int32, (n_rows, n_cols), 0)
            c = lax.broadcasted_iota(jnp.int32, (n_rows, n_cols), 1)
            bias = jnp.where(c <= r + offset, 0.0, NEG_BIG)
            bias_cache[(n_rows, n_cols, offset)] = jnp.concatenate([bias, bias], axis=0)
        return bias_cache[(n_rows, n_cols, offset)]

    def load_q(i, qq_sc):
        p0, n = block_rows(i)
        q = rows_at(q_ref, qm_ref, p0, n)
        lane = lax.broadcasted_iota(jnp.int32, q.shape, 1)
        zero = jnp.zeros_like(q)
        qq_sc[0:n, :] = jnp.where(lane < HEAD_DIM, q, zero)
        qq_sc[n:2 * n, :] = jnp.where(lane >= HEAD_DIM, q, zero)

    def step(state, kb, vb, bias, first):
        qq_sc, m_sc, acc_sc = state
        s = lax.dot_general(qq_sc[...], kb, (((1,), (1,)), ((), ())), preferred_element_type=F32)
        if bias is not None:
            s = s + bias
        n_tiles = s.shape[1] // LANES
        rm = s[:, 0:LANES]
        for t in range(1, n_tiles):
            rm = jnp.maximum(rm, s[:, t * LANES:(t + 1) * LANES])
        rmax = jnp.max(rm, axis=-1, keepdims=True)
        if first:
            m_new = jnp.broadcast_to(rmax, (s.shape[0], LANES))
        else:
            m_prev = m_sc[...]
            m_new = jnp.maximum(m_prev, rmax)
        p = jnp.exp2(s - jnp.concatenate([m_new] * n_tiles, axis=1))
        pv = jnp.dot(p.astype(BF16), vb, preferred_element_type=F32)
        if first:
            acc_sc[...] = pv
        else:
            alpha = jnp.exp2(m_prev - m_new)
            acc_sc[...] = jnp.concatenate([alpha, alpha], axis=1) * acc_sc[...] + pv
        m_sc[...] = m_new

    def chain_steps(i):
        n = block_rows(i)[1]
        steps = [(p0, width, causal_bias(n, width, offset)) for p0, width, offset in partial_tiles(i)]
        return steps + [(t * bk, bk, None) for t in range(n_full_tiles(i))]

    def key_step(state, p0, width, bias, first):
        kb = rows_at(k_ref, km_ref, p0, width)
        vb = jnp.concatenate([rows_at(v_ref, vm_ref, p0, width), jnp.ones((width, V_DIM), BF16)], axis=1)
        step(state, kb, vb, bias, first)

    def finish(i, acc_sc):
        p0, n = block_rows(i)
        o1 = acc_sc[0:n, 0:V_DIM] / acc_sc[0:n, V_DIM:2 * V_DIM]
        o2 = acc_sc[n:2 * n, 0:V_DIM] / acc_sc[n:2 * n, V_DIM:2 * V_DIM]
        o = o1 - lam * o2
        ms = jnp.mean(o * o, axis=-1, keepdims=True)
        o = (o * lax.rsqrt(ms + EPS) * subg_ref[...] * (1.0 - LAMBDA_INIT)).astype(BF16)
        if p0 == 0:
            o_ref[0:n - N_META, :] = o[N_META:n, :]
        else:
            o_ref[p0 - N_META:p0 - N_META + n, :] = o

    order = list(range(nq, -1, -1))
    plans = {i: chain_steps(i) for i in order}
    for i in order:
        load_q(i, states[i][0])
    for j in range(max(len(plan) for plan in plans.values())):
        for i in order:
            if j < len(plans[i]):
                key_step(states[i], *plans[i][j], first=(j == 0))
    for i in order:
        finish(i, states[i][2])


def _attention(q, k, v, qm, km, vm, lq1, lk1, lq2, lk2, subg, batch, seq, bq, bk):
    nq = seq // bq
    small = lambda b, h: (0, 0)
    head = lambda b, h: (b, h)
    meta = lambda b, h: (0, h)

    def state_scratch(rows):
        return [pltpu.VMEM((2 * rows, V_DIM), BF16), pltpu.VMEM((2 * rows, LANES), F32),
                pltpu.VMEM((2 * rows, 2 * V_DIM), F32)]

    return pl.pallas_call(
        functools.partial(_attn_kernel, bq=bq, bk=bk),
        grid=(batch, N_HEADS),
        in_specs=[
            pl.BlockSpec((1, HEAD_DIM), small),
            pl.BlockSpec((1, HEAD_DIM), small),
            pl.BlockSpec((1, HEAD_DIM), small),
            pl.BlockSpec((1, HEAD_DIM), small),
            pl.BlockSpec((1, V_DIM), small),
            pl.BlockSpec((seq, V_DIM), head),
            pl.BlockSpec((seq, V_DIM), head),
            pl.BlockSpec((seq, V_DIM), head),
            pl.BlockSpec((N_META, V_DIM), meta),
            pl.BlockSpec((N_META, V_DIM), meta),
            pl.BlockSpec((N_META, V_DIM), meta),
        ],
        out_specs=pl.BlockSpec((seq, V_DIM), head),
        out_shape=jax.ShapeDtypeStruct(q.shape, BF16),
        scratch_shapes=state_scratch(bq) * nq + state_scratch(N_META),
        compiler_params=_params("arbitrary", "arbitrary"),
        name="diff_attn",
    )(lq1, lk1, lq2, lk2, subg, q, k, v, qm, km, vm)


def _mix_kernel(x_ref, a_ref, u_ref, uprev_ref, umeta_ref, gate_ref, wa_ref, pw_ref, ps_ref, wp_ref,
                wo_ref, g2_ref, wr_ref, br_ref, h2c_ref, logit_ref, ext_sc, sum_a_sc, sum_b_sc, wfold_sc,
                *, tm, tiles_per_seq):
    i = pl.program_id(0)
    first = (i % tiles_per_seq) == 0

    @pl.when(i == 0)
    def _():
        for g in range(N_POOL_GROUPS):
            rows = slice(g * POOL_GROUP_DIM, (g + 1) * POOL_GROUP_DIM)
            left = pw_ref[g] * ps_ref[:, rows]
            right = wp_ref[rows, :]
            left_hi, right_hi = left.astype(BF16), right.astype(BF16)
            left_lo = (left - left_hi.astype(F32)).astype(BF16)
            right_lo = (right - right_hi.astype(F32)).astype(BF16)
            wfold_sc[rows, :] = (jnp.dot(left_hi, right_hi, preferred_element_type=F32)
                                 + jnp.dot(left_lo, right_hi, preferred_element_type=F32)
                                 + jnp.dot(left_hi, right_lo, preferred_element_type=F32)).astype(BF16)

    y_attn = jnp.dot(a_ref[...], wa_ref[...], preferred_element_type=F32)

    top = 2 * N_META
    for ref in (ext_sc, sum_a_sc, sum_b_sc):
        ref[0:N_META, :] = jnp.zeros((N_META, ref.shape[1]), F32)

    @pl.when(first)
    def _():
        ext_sc[N_META:top, :] = umeta_ref[...]

    @pl.when(jnp.logical_not(first))
    def _():
        ext_sc[N_META:top, :] = uprev_ref[...]

    ext_sc[top:top + tm, :] = u_ref[...]

    pooled = []
    for g, w in enumerate(POOL_WINDOWS):
        cs = slice(g * POOL_GROUP_DIM, (g + 1) * POOL_GROUP_DIM)
        src, lanes, shift = ext_sc, cs, 1
        for dst in (sum_a_sc, sum_b_sc, sum_a_sc, sum_b_sc):
            if shift >= w:
                break
            dst[N_META:top + tm, :] = (src[N_META:top + tm, lanes]
                                       + src[N_META - shift:top + tm - shift, lanes])
            src, lanes, shift = dst, slice(None), 2 * shift
        pooled.append((src[top:top + tm, lanes] * (1.0 / w) - ext_sc[top:top + tm, cs]).astype(BF16))
    y_pool = jnp.dot(jnp.concatenate(pooled, axis=1), wfold_sc[...], preferred_element_type=F32)
    mixed = (gate_ref[:, 0:D_MODEL].astype(F32) * y_attn
             + gate_ref[:, D_MODEL:2 * D_MODEL].astype(F32) * y_pool)
    h2 = x_ref[...] + jnp.dot(mixed.astype(BF16), wo_ref[...], preferred_element_type=F32)
    _store_token_major(h2c_ref, h2, tm)
    ms = jnp.mean(h2 * h2, axis=-1, keepdims=True)
    hn2 = h2 * lax.rsqrt(ms + EPS) * g2_ref[...]
    hn2_hi = hn2.astype(BF16)
    hn2_lo = (hn2 - hn2_hi.astype(F32)).astype(BF16)
    wr = wr_ref[...]
    wr_hi = wr.astype(BF16)
    wr_lo = (wr - wr_hi.astype(F32)).astype(BF16)
    wr_both = jnp.concatenate([wr_hi, wr_lo], axis=1)
    for half in range(2):
        rows = slice(half * tm // 2, (half + 1) * tm // 2)
        both = jnp.dot(hn2_hi[rows, :], wr_both, preferred_element_type=F32)
        logit_ref[rows, :] = (both[:, 0:LANES] + both[:, LANES:2 * LANES]
                              + jnp.dot(hn2_lo[rows, :], wr_hi, preferred_element_type=F32) + br_ref[...])


def _mix(x2, a, u, umeta, gates, w_attn, pool_w, pool_scale, w_pool, w_out, g2, wr, br, seq, tm):
    n = x2.shape[0]
    tiles_per_seq = seq // tm
    halo_blocks = tm // N_META
    const2 = lambda i: (0, 0)
    return pl.pallas_call(
        functools.partial(_mix_kernel, tm=tm, tiles_per_seq=tiles_per_seq),
        grid=(n // tm,),
        in_specs=[
            pl.BlockSpec((tm, D_MODEL), lambda i: (i, 0)),
            pl.BlockSpec((tm, D_MODEL), lambda i: (i, 0)),
            pl.BlockSpec((tm, POOL_WIDTH), lambda i: (i, 0)),
            pl.BlockSpec((N_META, POOL_WIDTH), lambda i: (jnp.maximum(i * halo_blocks - 1, 0), 0)),
            pl.BlockSpec((N_META, POOL_WIDTH), const2),
            pl.BlockSpec((tm, 2 * D_MODEL), lambda i: (i, 0)),
            pl.BlockSpec((D_MODEL, D_MODEL), const2, pipeline_mode=pl.Buffered(1)),
            pl.BlockSpec((N_POOL_GROUPS, POOL_GROUP_DIM, POOL_GROUP_DIM), lambda i: (0, 0, 0),
                         pipeline_mode=pl.Buffered(1)),
            pl.BlockSpec((1, POOL_WIDTH), const2),
            pl.BlockSpec((POOL_WIDTH, D_MODEL), const2, pipeline_mode=pl.Buffered(1)),
            pl.BlockSpec((D_MODEL, D_MODEL), const2, pipeline_mode=pl.Buffered(1)),
            pl.BlockSpec((1, D_MODEL), const2),
            pl.BlockSpec((D_MODEL, LANES), const2, pipeline_mode=pl.Buffered(1)),
            pl.BlockSpec((1, LANES), const2),
        ],
        out_specs=[
            pl.BlockSpec((tm * TOKEN_ROWS, LANES), lambda i: (i, 0)),
            pl.BlockSpec((tm, LANES), lambda i: (i, 0)),
        ],
        out_shape=[
            jax.ShapeDtypeStruct((n * TOKEN_ROWS, LANES), F32),
            jax.ShapeDtypeStruct((n, LANES), F32),
        ],
        scratch_shapes=[pltpu.VMEM((tm + 2 * N_META, POOL_WIDTH), F32),
                        pltpu.VMEM((tm + 2 * N_META, POOL_GROUP_DIM), F32),
                        pltpu.VMEM((tm + 2 * N_META, POOL_GROUP_DIM), F32),
                        pltpu.VMEM((POOL_WIDTH, D_MODEL), BF16)],
        compiler_params=_params("arbitrary"),
        name="mix",
    )(x2, a, u, u, umeta, gates, w_attn, pool_w, pool_scale, w_pool, w_out, g2, wr, br)


_TBL_EA, _TBL_EB, _TBL_NEW_A, _TBL_NEW_B, _TBL_NVALID, _TBL_NUSED = range(6)


def _route_kernel(logit_ref, tri_ref, wts_ref, dest_ref, tbl_ref, carry_sc, ids_sc, *, tme):
    i = pl.program_id(1)
    pl.when(pl.program_id(0) == 0)(functools.partial(_route_classify, i, logit_ref, tri_ref, wts_ref,
                                                     carry_sc, ids_sc))
    pl.when(pl.program_id(0) == 1)(functools.partial(_route_place, i, dest_ref, tbl_ref, carry_sc,
                                                     ids_sc, tme))


def _route_classify(i, logit_ref, tri_ref, wts_ref, carry_sc, ids_sc):
    @pl.when(i == 0)
    def _():
        carry_sc[...] = jnp.zeros_like(carry_sc)

    lg = logit_ref[...].T
    g = [lg[r:r + 1, :] for r in range(N_GROUPS)]
    gmax = functools.reduce(jnp.maximum, g)
    gsel = jnp.full_like(g[0], N_GROUPS - 1).astype(jnp.int32)
    for r in range(N_GROUPS - 2, -1, -1):
        gsel = jnp.where(g[r] == gmax, r, gsel)
    p_group = 1.0 / functools.reduce(lambda a, b: a + b, [jnp.exp(x - gmax) for x in g])
    e = []
    for jj in range(EXPERTS_PER_GROUP):
        v = lg[N_GROUPS + jj:N_GROUPS + jj + 1, :]
        for r in range(1, N_GROUPS):
            row = N_GROUPS + r * EXPERTS_PER_GROUP + jj
            v = jnp.where(gsel == r, lg[row:row + 1, :], v)
        e.append(v)
    v1 = functools.reduce(jnp.maximum, e)
    i1 = jnp.full_like(gsel, EXPERTS_PER_GROUP - 1)
    for jj in range(EXPERTS_PER_GROUP - 2, -1, -1):
        i1 = jnp.where(e[jj] == v1, jj, i1)
    rest = [jnp.where(i1 == jj, -jnp.inf, e[jj]) for jj in range(EXPERTS_PER_GROUP)]
    v2 = functools.reduce(jnp.maximum, rest)
    i2 = jnp.full_like(gsel, EXPERTS_PER_GROUP - 1)
    for jj in range(EXPERTS_PER_GROUP - 2, -1, -1):
        i2 = jnp.where(jnp.logical_and(rest[jj] == v2, i1 != jj), jj, i2)
    t = jnp.exp(v2 - v1)
    w1 = p_group / (1.0 + t)
    w2 = p_group * t / (1.0 + t)
    lo = jnp.minimum(i1, i2)
    hi = jnp.maximum(i1, i2)
    pair = jnp.where(lo == 0, hi - 1, jnp.where(lo == 1, jnp.where(hi == 3, 3, 4), 5))
    bucket = gsel * N_PAIRS + pair
    w_of_lo = jnp.where(i1 < i2, w1, w2)
    w_of_hi = jnp.where(i1 < i2, w2, w1)
    swapped = pair == 5
    w_a = jnp.where(swapped, w_of_hi, w_of_lo)
    w_b = jnp.where(swapped, w_of_lo, w_of_hi)

    tl = lg.shape[1]
    bid = lax.broadcasted_iota(jnp.int32, (N_BUCKET_ROWS, tl), 0)
    hot = bid == bucket
    hot_b = jnp.where(hot, 1.0, 0.0).astype(BF16)
    prefix = jnp.dot(hot_b, tri_ref[...], preferred_element_type=F32)
    carry = carry_sc[...]
    before = prefix + jnp.concatenate([carry] * (tl // LANES), axis=1)
    rank = jnp.sum(jnp.where(hot, before, 0.0), axis=0, keepdims=True)
    carry_sc[...] = carry + jnp.dot(hot_b, jnp.ones((tl, LANES), BF16), preferred_element_type=F32)
    pad = jnp.zeros((6, tl), F32)
    ids_sc[i] = jnp.concatenate([bucket.astype(F32), rank, pad], axis=0)
    wts_ref[...] = jnp.concatenate([w_a, w_b, pad], axis=0)


def _route_place(i, dest_ref, tbl_ref, carry_sc, ids_sc, tme):
    cnt = carry_sc[...]
    tiles = jnp.floor((cnt + (tme - 1)) * (1.0 / tme))
    r = lax.broadcasted_iota(jnp.int32, (N_BUCKET_ROWS, N_BUCKET_ROWS), 0)
    c = lax.broadcasted_iota(jnp.int32, (N_BUCKET_ROWS, N_BUCKET_ROWS), 1)
    tile_end = jnp.dot(jnp.where(c <= r, 1.0, 0.0).astype(BF16), tiles.astype(BF16),
                       preferred_element_type=F32)
    tile_start = tile_end - tiles

    ids = ids_sc[i]
    bucket, rank = ids[0:1, :], ids[1:2, :]
    tl = ids.shape[1]
    bid = lax.broadcasted_iota(jnp.int32, (N_BUCKET_ROWS, tl), 0).astype(F32)
    row_start = jnp.concatenate([tile_start * tme] * (tl // LANES), axis=1)
    dest = rank + jnp.sum(jnp.where(bid == bucket, row_start, 0.0), axis=0, keepdims=True)
    dest_ref[...] = jnp.concatenate([dest, jnp.zeros((7, tl), F32)], axis=0).astype(jnp.int32)

    lane = lax.broadcasted_iota(jnp.int32, (1, LANES), 1).astype(F32)
    n_used = tile_end[N_BUCKET_ROWS - 1:N_BUCKET_ROWS, :]
    last = jnp.minimum(lane, n_used - 1.0)
    tb = jnp.sum(jnp.where(tile_end <= last, 1.0, 0.0), axis=0, keepdims=True)
    tb = jnp.minimum(tb, N_BUCKETS - 1.0)
    group = functools.reduce(lambda a, b: a + b,
                             [jnp.where(tb >= g * N_PAIRS, 1.0, 0.0) for g in range(1, N_GROUPS)])
    pair = tb - N_PAIRS * group

    def lookup(table):
        out = jnp.full_like(pair, float(table[-1]))
        for k in range(len(table) - 2, -1, -1):
            out = jnp.where(pair == k, float(table[k]), out)
        return EXPERTS_PER_GROUP * group + out

    e_a, e_b = lookup(_PAIR_A), lookup(_PAIR_B)
    changed = lambda e: jnp.where(lane == 0.0, 1.0, jnp.where(e != pltpu.roll(e, 1, 1), 1.0, 0.0))
    bid128 = lax.broadcasted_iota(jnp.int32, (N_BUCKET_ROWS, LANES), 0).astype(F32)
    at_tile = lambda col: jnp.sum(jnp.where(bid128 == tb, col, 0.0), axis=0, keepdims=True)
    n_valid = jnp.clip(at_tile(cnt) - tme * (lane - at_tile(tile_start)), 0.0, float(tme))
    rows = [e_a, e_b, changed(e_a), changed(e_b), n_valid, n_used, jnp.zeros_like(lane),
            jnp.zeros_like(lane)]
    tbl_ref[...] = jnp.concatenate(rows, axis=0).astype(jnp.int32)


def _route(logits, tl, tme):
    n = logits.shape[0]
    nt = n // tl
    tri = jnp.asarray(np.triu(np.ones((tl, tl), np.float32), k=1), dtype=BF16)
    first_pass = lambda p, i: i * (1 - p) + (nt - 1) * p
    return pl.pallas_call(
        functools.partial(_route_kernel, tme=tme),
        grid=(2, nt),
        in_specs=[pl.BlockSpec((tl, LANES), lambda p, i: (first_pass(p, i), 0)),
                  pl.BlockSpec((tl, tl), lambda p, i: (0, 0))],
        out_specs=[pl.BlockSpec((8, tl), lambda p, i: (0, first_pass(p, i))),
                   pl.BlockSpec((8, tl), lambda p, i: (0, i * p)),
                   pl.BlockSpec((8, LANES), lambda p, i: (0, 0))],
        out_shape=[jax.ShapeDtypeStruct((8, n), F32),
                   jax.ShapeDtypeStruct((8, n), jnp.int32),
                   jax.ShapeDtypeStruct((8, LANES), jnp.int32)],
        scratch_shapes=[pltpu.VMEM((N_BUCKET_ROWS, LANES), F32), pltpu.VMEM((nt, 8, tl), F32)],
        compiler_params=_params("arbitrary", "arbitrary"),
        name="route",
    )(logits, tri)


def _token_copy(src_hbm, src_token, dst_buf, dst_row, sem):
    return pltpu.make_async_copy(
        src_hbm.at[pl.ds(pl.multiple_of(src_token * TOKEN_ROWS, TOKEN_ROWS), TOKEN_ROWS), :],
        dst_buf.at[pl.ds(pl.multiple_of(dst_row * TOKEN_ROWS, TOKEN_ROWS), TOKEN_ROWS), :],
        sem)


def _start_gather(idx_ref, first, n_groups, src_hbm, dst_buf, sem):
    def body(rr, carry):
        for uu in range(GATHER_UNROLL):
            r = rr * GATHER_UNROLL + uu
            _token_copy(src_hbm, idx_ref[first + r], dst_buf, r, sem).start(priority=uu % 2)
        return carry

    lax.fori_loop(0, n_groups, body, 0)


def _start_padded_gather(idx_ref, first, n_valid, rows, src_hbm, dst_buf, sem):
    n_groups = (n_valid + GATHER_UNROLL - 1) // GATHER_UNROLL
    _start_gather(idx_ref, first, n_groups, src_hbm, dst_buf, sem)
    pad_groups = rows // GATHER_UNROLL - n_groups
    off = n_groups * GATHER_UNROLL
    size = rows // 2
    while size >= GATHER_UNROLL:
        take = jnp.bitwise_and(pad_groups, size // GATHER_UNROLL) != 0

        @pl.when(take)
        def _(off=off, size=size):
            pltpu.make_async_copy(
                src_hbm.at[pl.ds(0, size * TOKEN_ROWS), :],
                dst_buf.at[pl.ds(pl.multiple_of(off * TOKEN_ROWS, GATHER_UNROLL * TOKEN_ROWS),
                                 size * TOKEN_ROWS), :],
                sem).start()

        off = off + jnp.where(take, size, 0)
        size //= 2


def _wait_gather(rows, src_hbm, dst_buf, sem):
    pltpu.make_async_copy(src_hbm.at[pl.ds(0, rows * TOKEN_ROWS), :], dst_buf, sem).wait()


def _moe_kernel(tbl_ref, src_ref, h2c_hbm, wts_ref, g2_ref, wg_a_ref, wu_a_ref, wd_a_ref, wg_b_ref,
                wu_b_ref, wd_b_ref, yc_ref, xbuf, wg_a_sc, wu_a_sc, wd_a_sc, wg_b_sc, wu_b_sc, wd_b_sc,
                sem, *, tme):
    i = pl.program_id(0)
    n_used = tbl_ref[_TBL_NUSED, 0]
    slot = jnp.bitwise_and(i, 1)

    @pl.when(i == 0)
    def _():
        _start_padded_gather(src_ref, 0, tbl_ref[_TBL_NVALID, 0], tme, h2c_hbm, xbuf.at[0], sem.at[0])

    @pl.when(i + 1 < n_used)
    def _():
        _start_padded_gather(src_ref, (i + 1) * tme, tbl_ref[_TBL_NVALID, i + 1], tme, h2c_hbm,
                             xbuf.at[1 - slot], sem.at[1 - slot])

    @pl.when(tbl_ref[_TBL_NEW_A, i] == 1)
    def _():
        for src, dst in ((wg_a_ref, wg_a_sc), (wu_a_ref, wu_a_sc), (wd_a_ref, wd_a_sc)):
            dst[...] = src[0].astype(BF16)

    @pl.when(tbl_ref[_TBL_NEW_B, i] == 1)
    def _():
        for src, dst in ((wg_b_ref, wg_b_sc), (wu_b_ref, wu_b_sc), (wd_b_ref, wd_b_sc)):
            dst[...] = src[0].astype(BF16)

    def tile(rows):
        _wait_gather(tme, h2c_hbm, xbuf.at[slot], sem.at[slot])
        x = _load_token_major(xbuf.at[slot], rows)
        ms = jnp.mean(x * x, axis=-1, keepdims=True)
        hn = (x * lax.rsqrt(ms + EPS) * g2_ref[...]).astype(BF16)
        wts = wts_ref[0:rows, :]

        def expert_act(wg_sc, wu_sc, w):
            gate = jnp.dot(hn, wg_sc[...], preferred_element_type=F32)
            up = jnp.dot(hn, wu_sc[...], preferred_element_type=F32)
            return (gate / (1.0 + jnp.exp(-gate)) * up * w).astype(BF16)

        y = (jnp.dot(expert_act(wg_a_sc, wu_a_sc, wts[:, 0:1]), wd_a_sc[...], preferred_element_type=F32)
             + jnp.dot(expert_act(wg_b_sc, wu_b_sc, wts[:, 1:2]), wd_b_sc[...], preferred_element_type=F32))
        _store_token_major(yc_ref, y, rows)
        if rows < tme:
            yc_ref[rows * TOKEN_ROWS:tme * TOKEN_ROWS, :] = jnp.zeros(((tme - rows) * TOKEN_ROWS, LANES), F32)

    n_valid = tbl_ref[_TBL_NVALID, i]
    pl.when(jnp.logical_and(i < n_used, n_valid > tme // 2))(functools.partial(tile, tme))
    pl.when(jnp.logical_and(i < n_used, n_valid <= tme // 2))(functools.partial(tile, tme // 2))

    @pl.when(i >= n_used)
    def _():
        yc_ref[...] = jnp.zeros_like(yc_ref)


def _moe(tbl, src, h2c, wts_sorted, g2, wg, wu, wd, tme, n_tiles):
    used = lambda i, tbl, src: (jnp.minimum(i, tbl[_TBL_NUSED, 0] - 1), 0)
    w_a = lambda i, tbl, src: (tbl[_TBL_EA, i], 0, 0)
    w_b = lambda i, tbl, src: (tbl[_TBL_EB, i], 0, 0)
    up_shape, down_shape = (D_MODEL, D_EXPERT), (D_EXPERT, D_MODEL)
    grid_spec = pltpu.PrefetchScalarGridSpec(
        num_scalar_prefetch=2,
        grid=(n_tiles,),
        in_specs=[
            pl.BlockSpec(memory_space=pl.ANY),
            pl.BlockSpec((tme, wts_sorted.shape[1]), used),
            pl.BlockSpec((1, D_MODEL), lambda i, tbl, src: (0, 0)),
            pl.BlockSpec((1,) + up_shape, w_a),
            pl.BlockSpec((1,) + up_shape, w_a),
            pl.BlockSpec((1,) + down_shape, w_a),
            pl.BlockSpec((1,) + up_shape, w_b),
            pl.BlockSpec((1,) + up_shape, w_b),
            pl.BlockSpec((1,) + down_shape, w_b),
        ],
        out_specs=pl.BlockSpec((tme * TOKEN_ROWS, LANES), lambda i, tbl, src: (i, 0)),
        scratch_shapes=[pltpu.VMEM((2, tme * TOKEN_ROWS, LANES), F32)]
        + [pltpu.VMEM(s, BF16) for s in (up_shape, up_shape, down_shape) * 2]
        + [pltpu.SemaphoreType.DMA((2,))],
    )
    return pl.pallas_call(
        functools.partial(_moe_kernel, tme=tme),
        grid_spec=grid_spec,
        out_shape=jax.ShapeDtypeStruct((n_tiles * tme * TOKEN_ROWS, LANES), F32),
        compiler_params=_params("arbitrary"),
        name="moe",
    )(tbl, src, h2c, wts_sorted, g2, wg, wu, wd, wg, wu, wd)


def _final_kernel(dest_ref, h2c_ref, yc_hbm, gf_ref, o_ref, ybuf, sem, *, tmf):
    i = pl.program_id(0)
    slot = jnp.bitwise_and(i, 1)

    @pl.when(i == 0)
    def _():
        _start_gather(dest_ref, 0, tmf // GATHER_UNROLL, yc_hbm, ybuf.at[0], sem.at[0])

    @pl.when(i + 1 < pl.num_programs(0))
    def _():
        _start_gather(dest_ref, (i + 1) * tmf, tmf // GATHER_UNROLL, yc_hbm, ybuf.at[1 - slot],
                      sem.at[1 - slot])

    _wait_gather(tmf, yc_hbm, ybuf.at[slot], sem.at[slot])
    h = _load_token_major(h2c_ref, tmf) + _load_token_major(ybuf.at[slot], tmf)
    ms = jnp.mean(h * h, axis=-1, keepdims=True)
    o_ref[...] = h * lax.rsqrt(ms + EPS) * gf_ref[...]


def _final(dest, h2c, yc, gf, tmf):
    n = h2c.shape[0] // TOKEN_ROWS
    grid_spec = pltpu.PrefetchScalarGridSpec(
        num_scalar_prefetch=1,
        grid=(n // tmf,),
        in_specs=[
            pl.BlockSpec((tmf * TOKEN_ROWS, LANES), lambda i, dest: (i, 0)),
            pl.BlockSpec(memory_space=pl.ANY),
            pl.BlockSpec((1, D_MODEL), lambda i, dest: (0, 0)),
        ],
        out_specs=pl.BlockSpec((tmf, D_MODEL), lambda i, dest: (i, 0)),
        scratch_shapes=[pltpu.VMEM((2, tmf * TOKEN_ROWS, LANES), F32), pltpu.SemaphoreType.DMA((2,))],
    )
    return pl.pallas_call(
        functools.partial(_final_kernel, tmf=tmf),
        grid_spec=grid_spec,
        out_shape=jax.ShapeDtypeStruct((n, D_MODEL), F32),
        compiler_params=_params("arbitrary"),
        name="final",
    )(dest, h2c, yc, gf)


def _rope_tables(t):
    inv = 1.0 / (ROPE_THETA ** (np.arange(0, HEAD_DIM, 2, dtype=np.float64) / HEAD_DIM))
    ang = np.arange(t, dtype=np.float64)[:, None] * inv[None, :]
    ang = np.concatenate([ang, ang, ang, ang], axis=-1)
    sign = np.where((np.arange(LANES) % HEAD_DIM) < HEAD_DIM // 2, -1.0, 1.0)
    return (jnp.asarray(np.cos(ang), dtype=F32), jnp.asarray(np.sin(ang) * sign[None, :], dtype=F32))


def kernel(x, meta, norm1_g, w_in, b_gate, lambda_q1, lambda_k1, lambda_q2, lambda_k2, subln_g, pool_w,
           pool_scale, w_attn_br, w_pool_br, w_out, norm2_g, w_router_group, b_router_group,
           w_router_expert, b_router_expert, w_e_gate, w_e_up, w_e_down, final_g):
    batch, seq, d = x.shape
    n = batch * seq
    assert w_in.shape[0] == 1 and d == D_MODEL and meta.shape == (N_META, D_MODEL)
    assert seq % 1024 == 0 and n < 2 ** 24
    x2 = x.reshape(n, d)
    cos, sin = _rope_tables(seq + N_META)
    q, k, v, u, gates, qm, km, vm, um = _inproj(x2, meta, norm1_g, cos, sin, w_in[0], b_gate, tm=1024)
    a = _attention(q, k, v, qm, km, vm, lambda_q1, lambda_k1, lambda_q2, lambda_k2, subln_g, batch, seq,
                   bq=256, bk=256)

    n_router = N_GROUPS + N_EXPERTS
    wr = jnp.pad(jnp.concatenate([w_router_group[0], w_router_expert[0]], axis=1),
                 ((0, 0), (0, LANES - n_router)))
    br = jnp.pad(jnp.concatenate([b_router_group[0], b_router_expert[0]]), (0, LANES - n_router))
    h2c, logits = _mix(x2, a, u, um, gates, w_attn_br[0].astype(BF16), pool_w[0],
                       pool_scale, w_pool_br[0], w_out[0].astype(BF16), norm2_g,
                       wr, br.reshape(1, LANES), seq, tm=1024)
    tme = 512
    n_tiles = n // tme + N_BUCKETS
    assert n_tiles <= LANES
    wts, dest, tile_table = _route(logits, tl=1024, tme=tme)
    dest = dest[0]
    per_token = jnp.concatenate([wts[0:2], jnp.arange(n, dtype=F32)[None, :]], axis=0).T
    per_row = jnp.zeros((n_tiles * tme, 3), F32).at[dest].set(per_token)
    src = per_row[:, 2].astype(jnp.int32)

    yc = _moe(tile_table, src, h2c, per_row, norm2_g, w_e_gate[0], w_e_up[0], w_e_down[0], tme, n_tiles)
    out = _final(dest, h2c, yc, final_g.reshape(1, d), tmf=256)
    return out.reshape(batch, seq, d)
```

```python
import functools
import math

import jax
import jax.numpy as jnp
import numpy as np
from jax import lax
from jax.experimental import pallas as pl
from jax.experimental.pallas import tpu as pltpu

D_MODEL = 1024
N_META = 16
N_HEADS = 8
HEAD_DIM = 64
V_DIM = 2 * HEAD_DIM
POOL_WINDOWS = (2, 4, 8, 16)
N_POOL_GROUPS = len(POOL_WINDOWS)
POOL_GROUP_DIM = 128
POOL_WIDTH = N_POOL_GROUPS * POOL_GROUP_DIM
ROPE_THETA = 10000.0
N_GROUPS = 4
EXPERTS_PER_GROUP = 4
N_EXPERTS = N_GROUPS * EXPERTS_PER_GROUP
D_EXPERT = 512
EPS = 1e-6
LAMBDA_INIT = 0.8 - 0.6 * math.exp(-0.3 * 0)
LOG2_E = math.log2(math.e)

_PAIR_A = (0, 0, 0, 1, 1, 3)
_PAIR_B = (1, 2, 3, 3, 2, 2)
N_PAIRS = len(_PAIR_A)
N_BUCKETS = N_GROUPS * N_PAIRS
N_BUCKET_ROWS = 32
GATHER_UNROLL = 16

LANES = 128
NEG_BIG = -1e30
VMEM_LIMIT = 48 * 1024 * 1024

F32 = jnp.float32
BF16 = jnp.bfloat16

_ATTN_HEADS = 2
_HEADS_PER_STEP = 4
_IN_STEPS = N_HEADS // _HEADS_PER_STEP
_QKV_BLK = _HEADS_PER_STEP * V_DIM
_U_BLK = POOL_WIDTH // _IN_STEPS
_GATE_BLK = 512
_GATES_PER_STEP = 2 * D_MODEL // _IN_STEPS // _GATE_BLK
_O_K = D_MODEL // _QKV_BLK
_O_V = 2 * D_MODEL // _QKV_BLK
_O_U = 3 * D_MODEL // _U_BLK
_O_G = (3 * D_MODEL + POOL_WIDTH) // _GATE_BLK


def _params(*sem):
    return pltpu.CompilerParams(dimension_semantics=sem, vmem_limit_bytes=VMEM_LIMIT)


TOKEN_ROWS = D_MODEL // LANES


def _store_token_major(ref, val, rows):
    for s in range(TOKEN_ROWS):
        ref[pl.ds(s, rows, stride=TOKEN_ROWS), :] = val[:, s * LANES:(s + 1) * LANES]


def _load_token_major(ref, rows):
    return jnp.concatenate([ref[pl.ds(s, rows, stride=TOKEN_ROWS), :] for s in range(TOKEN_ROWS)], axis=1)


def _rope(z, cos, sin_signed, first_half):
    outs = []
    for c in range(z.shape[1] // LANES):
        zc = z[:, c * LANES:(c + 1) * LANES]
        rot = jnp.where(first_half, pltpu.roll(zc, LANES - HEAD_DIM // 2, 1),
                        pltpu.roll(zc, HEAD_DIM // 2, 1))
        outs.append(zc * cos + rot * sin_signed)
    return jnp.concatenate(outs, axis=1)


def _inproj_kernel(x_ref, meta_ref, g_ref, cos_ref, sin_ref, cosm_ref, sinm_ref, wq_ref, wk_ref, wv_ref,
                   wu_ref, wg0_ref, wg1_ref, bg_ref, q_ref, k_ref, v_ref, u_ref, gate_ref, qm_ref, km_ref,
                   vm_ref, um_ref, wq_sc, wk_sc, wv_sc, wu_sc, wg0_sc, wg1_sc):
    def normed(x):
        ms = jnp.mean(x * x, axis=-1, keepdims=True)
        return (x * lax.rsqrt(ms + EPS) * g_ref[...]).astype(BF16)

    def project(hn, cos, sin, q_out, k_out, v_out, u_out, gate_out):
        lane = lax.broadcasted_iota(jnp.int32, cos.shape, 1)
        first_half = jnp.bitwise_and(lane, HEAD_DIM - 1) < (HEAD_DIM // 2)
        if gate_out is not None:
            for blk, wg_sc in enumerate((wg0_sc, wg1_sc)):
                cols = slice(blk * _GATE_BLK, (blk + 1) * _GATE_BLK)
                zg = jnp.dot(hn, wg_sc[...], preferred_element_type=F32) + bg_ref[:, cols]
                gate_out[:, cols] = (1.0 / (1.0 + jnp.exp(-zg))).astype(BF16)
        zq = jnp.dot(hn, wq_sc[...], preferred_element_type=F32)
        q_out[...] = (_rope(zq, cos, sin, first_half) * (LOG2_E / math.sqrt(HEAD_DIM))).astype(BF16)
        zk = jnp.dot(hn, wk_sc[...], preferred_element_type=F32)
        k_out[...] = _rope(zk, cos, sin, first_half).astype(BF16)
        u_out[...] = jnp.dot(hn, wu_sc[...], preferred_element_type=F32)
        v_out[...] = jnp.dot(hn, wv_sc[...], preferred_element_type=F32).astype(BF16)

    @pl.when(pl.program_id(1) == 0)
    def _():
        for src, dst in ((wq_ref, wq_sc), (wk_ref, wk_sc), (wv_ref, wv_sc), (wu_ref, wu_sc),
                         (wg0_ref, wg0_sc), (wg1_ref, wg1_sc)):
            dst[...] = src[...].astype(BF16)
        project(normed(meta_ref[...]), cosm_ref[...], sinm_ref[...], qm_ref, km_ref, vm_ref, um_ref, None)

    project(normed(x_ref[...]), cos_ref[...], sin_ref[...], q_ref, k_ref, v_ref, u_ref, gate_ref)


def _inproj(x2, meta, g1, cos, sin, w_in, b_gate, tm):
    n = x2.shape[0]
    n_meta = meta.shape[0]
    n_pos_blocks = (cos.shape[0] - n_meta) // tm
    row = lambda j, i: (i, 0)
    pos = lambda j, i: (i % n_pos_blocks, 0)
    whole = lambda j, i: (0, 0)
    once = pl.Buffered(1)
    gate_cols = _GATES_PER_STEP * _GATE_BLK
    assert _GATES_PER_STEP == 2
    return pl.pallas_call(
        _inproj_kernel,
        grid=(_IN_STEPS, n // tm),
        in_specs=[
            pl.BlockSpec((tm, D_MODEL), row),
            pl.BlockSpec((n_meta, D_MODEL), whole),
            pl.BlockSpec((1, D_MODEL), whole),
            pl.BlockSpec((tm, LANES), pos),
            pl.BlockSpec((tm, LANES), pos),
            pl.BlockSpec((n_meta, LANES), whole),
            pl.BlockSpec((n_meta, LANES), whole),
            pl.BlockSpec((D_MODEL, _QKV_BLK), lambda j, i: (0, j), pipeline_mode=once),
            pl.BlockSpec((D_MODEL, _QKV_BLK), lambda j, i: (0, _O_K + j), pipeline_mode=once),
            pl.BlockSpec((D_MODEL, _QKV_BLK), lambda j, i: (0, _O_V + j), pipeline_mode=once),
            pl.BlockSpec((D_MODEL, _U_BLK), lambda j, i: (0, _O_U + j), pipeline_mode=once),
            pl.BlockSpec((D_MODEL, _GATE_BLK), lambda j, i: (0, _O_G + _GATES_PER_STEP * j),
                         pipeline_mode=once),
            pl.BlockSpec((D_MODEL, _GATE_BLK), lambda j, i: (0, _O_G + _GATES_PER_STEP * j + 1),
                         pipeline_mode=once),
            pl.BlockSpec((1, gate_cols), lambda j, i: (0, j)),
        ],
        out_specs=[
            pl.BlockSpec((tm, _QKV_BLK), lambda j, i: (i, j)),
            pl.BlockSpec((tm, _QKV_BLK), lambda j, i: (i, j)),
            pl.BlockSpec((tm, _QKV_BLK), lambda j, i: (i, j)),
            pl.BlockSpec((tm, _U_BLK), lambda j, i: (i, j)),
            pl.BlockSpec((tm, gate_cols), lambda j, i: (i, j)),
            pl.BlockSpec((n_meta, _QKV_BLK), lambda j, i: (0, j)),
            pl.BlockSpec((n_meta, _QKV_BLK), lambda j, i: (0, j)),
            pl.BlockSpec((n_meta, _QKV_BLK), lambda j, i: (0, j)),
            pl.BlockSpec((n_meta, _U_BLK), lambda j, i: (0, j)),
        ],
        out_shape=[
            jax.ShapeDtypeStruct((n, D_MODEL), BF16),
            jax.ShapeDtypeStruct((n, D_MODEL), BF16),
            jax.ShapeDtypeStruct((n, D_MODEL), BF16),
            jax.ShapeDtypeStruct((n, POOL_WIDTH), F32),
            jax.ShapeDtypeStruct((n, 2 * D_MODEL), BF16),
            jax.ShapeDtypeStruct((n_meta, D_MODEL), BF16),
            jax.ShapeDtypeStruct((n_meta, D_MODEL), BF16),
            jax.ShapeDtypeStruct((n_meta, D_MODEL), BF16),
            jax.ShapeDtypeStruct((n_meta, POOL_WIDTH), F32),
        ],
        scratch_shapes=[
            pltpu.VMEM((D_MODEL, _QKV_BLK), BF16),
            pltpu.VMEM((D_MODEL, _QKV_BLK), BF16),
            pltpu.VMEM((D_MODEL, _QKV_BLK), BF16),
            pltpu.VMEM((D_MODEL, _U_BLK), BF16),
            pltpu.VMEM((D_MODEL, _GATE_BLK), BF16),
            pltpu.VMEM((D_MODEL, _GATE_BLK), BF16),
        ],
        compiler_params=_params("arbitrary", "arbitrary"),
        name="inproj",
    )(x2, meta, g1, cos[n_meta:], sin[n_meta:], cos[:n_meta], sin[:n_meta], w_in, w_in, w_in, w_in, w_in,
      w_in, b_gate)


def _attn_kernel(lq1_ref, lk1_ref, lq2_ref, lk2_ref, subg_ref, q_ref, k_ref, v_ref, qm_ref, km_ref,
                 vm_ref, o_ref, *state_sc, bq, bk):
    seq = k_ref.shape[0]
    nq = seq // bq
    heads = k_ref.shape[1] // V_DIM
    chains = [(h, i) for i in range(nq, -1, -1) for h in range(heads)]
    states = {c: tuple(state_sc[3 * g:3 * g + 3]) for g, c in enumerate(chains)}

    def rows_at(ref, meta_ref, h, p0, n):
        cols = slice(h * V_DIM, (h + 1) * V_DIM)
        if p0 == 0:
            return jnp.concatenate([meta_ref[:, cols], ref[0:n - N_META, cols]], axis=0)
        real = ref[p0 - N_META:min(p0 - N_META + n, seq), cols]
        if real.shape[0] < n:
            real = jnp.concatenate([real, jnp.zeros((n - real.shape[0], real.shape[1]), real.dtype)], axis=0)
        return real

    def block_rows(i):
        return (i * bq, bq) if i < nq else (nq * bq, N_META)

    def n_full_tiles(i):
        return (block_rows(i)[0] + 1) // bk

    def partial_tiles(i):
        p0, n = block_rows(i)
        if i == nq:
            return [(n_full_tiles(i) * bk, LANES, p0 - n_full_tiles(i) * bk)]
        return [(t * bk, bk, p0 - t * bk) for t in range(n_full_tiles(i), (p0 + n - 1) // bk + 1)]

    lam = (jnp.exp(jnp.sum(lq1_ref[...] * lk1_ref[...], axis=-1, keepdims=True))
           - jnp.exp(jnp.sum(lq2_ref[...] * lk2_ref[...], axis=-1, keepdims=True)) + LAMBDA_INIT)

    bias_cache = {}

    def causal_bias(n_rows, n_cols, offset):
        if (n_rows, n_cols, offset) not in bias_cache:
            r = lax.broadcasted_iota(jnp.int32, (n_rows, n_cols), 0)
            c = lax.broadcasted_iota(jnp.int32, (n_rows, n_cols), 1)
            bias = jnp.where(c <= r + offset, 0.0, NEG_BIG)
            bias_cache[(n_rows, n_cols, offset)] = jnp.concatenate([bias, bias], axis=0)
        return bias_cache[(n_rows, n_cols, offset)]

    def load_q(h, i, qq_sc):
        p0, n = block_rows(i)
        q = rows_at(q_ref, qm_ref, h, p0, n)
        lane = lax.broadcasted_iota(jnp.int32, q.shape, 1)
        zero = jnp.zeros_like(q)
        qq_sc[0:n, :] = jnp.where(lane < HEAD_DIM, q, zero)
        qq_sc[n:2 * n, :] = jnp.where(lane >= HEAD_DIM, q, zero)

    def step(state, kb, vb, bias, first):
        qq_sc, m_sc, acc_sc = state
        s = lax.dot_general(qq_sc[...], kb, (((1,), (1,)), ((), ())), preferred_element_type=F32)
        if bias is not None:
            s = s + bias
        n_tiles = s.shape[1] // LANES
        rm = s[:, 0:LANES]
        for t in range(1, n_tiles):
            rm = jnp.maximum(rm, s[:, t * LANES:(t + 1) * LANES])
        rmax = jnp.max(rm, axis=-1, keepdims=True)
        if first:
            m_new = jnp.broadcast_to(rmax, (s.shape[0], LANES))
        else:
            m_prev = m_sc[...]
            m_new = jnp.maximum(m_prev, rmax)
        p = jnp.exp2(s - jnp.concatenate([m_new] * n_tiles, axis=1))
        pv = jnp.dot(p.astype(BF16), vb, preferred_element_type=F32)
        if first:
            acc_sc[...] = pv
        else:
            alpha = jnp.exp2(m_prev - m_new)
            acc_sc[...] = jnp.concatenate([alpha, alpha], axis=1) * acc_sc[...] + pv
        m_sc[...] = m_new

    def chain_steps(i):
        n = block_rows(i)[1]
        steps = [(p0, width, causal_bias(n, width, offset)) for p0, width, offset in partial_tiles(i)]
        return steps + [(t * bk, bk, None) for t in range(n_full_tiles(i))]

    def key_step(state, h, p0, width, bias, first):
        kb = rows_at(k_ref, km_ref, h, p0, width)
        vb = jnp.concatenate([rows_at(v_ref, vm_ref, h, p0, width), jnp.ones((width, V_DIM), BF16)],
                             axis=1)
        step(state, kb, vb, bias, first)

    def finish(h, i, acc_sc):
        p0, n = block_rows(i)
        cols = slice(h * V_DIM, (h + 1) * V_DIM)
        o1 = acc_sc[0:n, 0:V_DIM] / acc_sc[0:n, V_DIM:2 * V_DIM]
        o2 = acc_sc[n:2 * n, 0:V_DIM] / acc_sc[n:2 * n, V_DIM:2 * V_DIM]
        o = o1 - lam * o2
        ms = jnp.mean(o * o, axis=-1, keepdims=True)
        o = (o * lax.rsqrt(ms + EPS) * subg_ref[...] * (1.0 - LAMBDA_INIT)).astype(BF16)
        if p0 == 0:
            o_ref[0:n - N_META, cols] = o[N_META:n, :]
        else:
            o_ref[p0 - N_META:p0 - N_META + n, cols] = o

    plans = {i: chain_steps(i) for i in range(nq + 1)}
    for h, i in chains:
        load_q(h, i, states[h, i][0])
    for j in range(max(len(plan) for plan in plans.values())):
        for h, i in chains:
            if j < len(plans[i]):
                key_step(states[h, i], h, *plans[i][j], first=(j == 0))
    for h, i in chains:
        finish(h, i, states[h, i][2])


def _attention(q, k, v, qm, km, vm, lq1, lk1, lq2, lk2, subg, batch, seq, bq, bk):
    nq = seq // bq
    width = _ATTN_HEADS * V_DIM
    small = lambda b, h: (0, 0)
    head = lambda b, h: (b, h)
    meta = lambda b, h: (0, h)

    def state_scratch(rows):
        return [pltpu.VMEM((2 * rows, V_DIM), BF16), pltpu.VMEM((2 * rows, LANES), F32),
                pltpu.VMEM((2 * rows, 2 * V_DIM), F32)]

    return pl.pallas_call(
        functools.partial(_attn_kernel, bq=bq, bk=bk),
        grid=(batch, N_HEADS // _ATTN_HEADS),
        in_specs=[
            pl.BlockSpec((1, HEAD_DIM), small),
            pl.BlockSpec((1, HEAD_DIM), small),
            pl.BlockSpec((1, HEAD_DIM), small),
            pl.BlockSpec((1, HEAD_DIM), small),
            pl.BlockSpec((1, V_DIM), small),
            pl.BlockSpec((seq, width), head),
            pl.BlockSpec((seq, width), head),
            pl.BlockSpec((seq, width), head),
            pl.BlockSpec((N_META, width), meta),
            pl.BlockSpec((N_META, width), meta),
            pl.BlockSpec((N_META, width), meta),
        ],
        out_specs=pl.BlockSpec((seq, width), head),
        out_shape=jax.ShapeDtypeStruct(q.shape, BF16),
        scratch_shapes=state_scratch(N_META) * _ATTN_HEADS + state_scratch(bq) * (nq * _ATTN_HEADS),
        compiler_params=_params("arbitrary", "arbitrary"),
        name="diff_attn",
    )(lq1, lk1, lq2, lk2, subg, q, k, v, qm, km, vm)


def _mix_kernel(x_ref, a_ref, u_ref, uprev_ref, umeta_ref, gate_ref, wa_ref, pw_ref, ps_ref, wp_ref,
                wo_ref, g2_ref, wr_ref, br_ref, h2c_ref, logit_ref, ext_sc, sum_a_sc, sum_b_sc, wfold_sc,
                *, tm, tiles_per_seq):
    i = pl.program_id(0)
    first = (i % tiles_per_seq) == 0

    @pl.when(i == 0)
    def _():
        for g in range(N_POOL_GROUPS):
            rows = slice(g * POOL_GROUP_DIM, (g + 1) * POOL_GROUP_DIM)
            left = pw_ref[g] * ps_ref[:, rows]
            right = wp_ref[rows, :]
            left_hi, right_hi = left.astype(BF16), right.astype(BF16)
            left_lo = (left - left_hi.astype(F32)).astype(BF16)
            right_lo = (right - right_hi.astype(F32)).astype(BF16)
            wfold_sc[rows, :] = (jnp.dot(left_hi, right_hi, preferred_element_type=F32)
                                 + jnp.dot(left_lo, right_hi, preferred_element_type=F32)
                                 + jnp.dot(left_hi, right_lo, preferred_element_type=F32)).astype(BF16)

    y_attn = jnp.dot(a_ref[...], wa_ref[...], preferred_element_type=F32)

    top = 2 * N_META
    for ref in (ext_sc, sum_a_sc, sum_b_sc):
        ref[0:N_META, :] = jnp.zeros((N_META, ref.shape[1]), F32)

    @pl.when(first)
    def _():
        ext_sc[N_META:top, :] = umeta_ref[...]

    @pl.when(jnp.logical_not(first))
    def _():
        ext_sc[N_META:top, :] = uprev_ref[...]

    ext_sc[top:top + tm, :] = u_ref[...]

    pooled = []
    for g, w in enumerate(POOL_WINDOWS):
        cs = slice(g * POOL_GROUP_DIM, (g + 1) * POOL_GROUP_DIM)
        src, lanes, shift = ext_sc, cs, 1
        for dst in (sum_a_sc, sum_b_sc, sum_a_sc, sum_b_sc):
            if shift >= w:
                break
            dst[N_META:top + tm, :] = (src[N_META:top + tm, lanes]
                                       + src[N_META - shift:top + tm - shift, lanes])
            src, lanes, shift = dst, slice(None), 2 * shift
        pooled.append((src[top:top + tm, lanes] * (1.0 / w) - ext_sc[top:top + tm, cs]).astype(BF16))
    y_pool = jnp.dot(jnp.concatenate(pooled, axis=1), wfold_sc[...], preferred_element_type=F32)
    mixed = (gate_ref[:, 0:D_MODEL].astype(F32) * y_attn
             + gate_ref[:, D_MODEL:2 * D_MODEL].astype(F32) * y_pool)
    h2 = x_ref[...] + jnp.dot(mixed.astype(BF16), wo_ref[...], preferred_element_type=F32)
    _store_token_major(h2c_ref, h2, tm)
    ms = jnp.mean(h2 * h2, axis=-1, keepdims=True)
    hn2 = h2 * lax.rsqrt(ms + EPS) * g2_ref[...]
    hn2_hi = hn2.astype(BF16)
    hn2_lo = (hn2 - hn2_hi.astype(F32)).astype(BF16)
    wr = wr_ref[...]
    wr_hi = wr.astype(BF16)
    wr_lo = (wr - wr_hi.astype(F32)).astype(BF16)
    wr_both = jnp.concatenate([wr_hi, wr_lo], axis=1)
    for half in range(2):
        rows = slice(half * tm // 2, (half + 1) * tm // 2)
        both = jnp.dot(hn2_hi[rows, :], wr_both, preferred_element_type=F32)
        logit_ref[rows, :] = (both[:, 0:LANES] + both[:, LANES:2 * LANES]
                              + jnp.dot(hn2_lo[rows, :], wr_hi, preferred_element_type=F32) + br_ref[...])


def _mix(x2, a, u, umeta, gates, w_attn, pool_w, pool_scale, w_pool, w_out, g2, wr, br, seq, tm):
    n = x2.shape[0]
    tiles_per_seq = seq // tm
    halo_blocks = tm // N_META
    const2 = lambda i: (0, 0)
    return pl.pallas_call(
        functools.partial(_mix_kernel, tm=tm, tiles_per_seq=tiles_per_seq),
        grid=(n // tm,),
        in_specs=[
            pl.BlockSpec((tm, D_MODEL), lambda i: (i, 0)),
            pl.BlockSpec((tm, D_MODEL), lambda i: (i, 0)),
            pl.BlockSpec((tm, POOL_WIDTH), lambda i: (i, 0)),
            pl.BlockSpec((N_META, POOL_WIDTH), lambda i: (jnp.maximum(i * halo_blocks - 1, 0), 0)),
            pl.BlockSpec((N_META, POOL_WIDTH), const2),
            pl.BlockSpec((tm, 2 * D_MODEL), lambda i: (i, 0)),
            pl.BlockSpec((D_MODEL, D_MODEL), const2, pipeline_mode=pl.Buffered(1)),
            pl.BlockSpec((N_POOL_GROUPS, POOL_GROUP_DIM, POOL_GROUP_DIM), lambda i: (0, 0, 0),
                         pipeline_mode=pl.Buffered(1)),
            pl.BlockSpec((1, POOL_WIDTH), const2),
            pl.BlockSpec((POOL_WIDTH, D_MODEL), const2, pipeline_mode=pl.Buffered(1)),
            pl.BlockSpec((D_MODEL, D_MODEL), const2, pipeline_mode=pl.Buffered(1)),
            pl.BlockSpec((1, D_MODEL), const2),
            pl.BlockSpec((D_MODEL, LANES), const2, pipeline_mode=pl.Buffered(1)),
            pl.BlockSpec((1, LANES), const2),
        ],
        out_specs=[
            pl.BlockSpec((tm * TOKEN_ROWS, LANES), lambda i: (i, 0)),
            pl.BlockSpec((tm, LANES), lambda i: (i, 0)),
        ],
        out_shape=[
            jax.ShapeDtypeStruct((n * TOKEN_ROWS, LANES), F32),
            jax.ShapeDtypeStruct((n, LANES), F32),
        ],
        scratch_shapes=[pltpu.VMEM((tm + 2 * N_META, POOL_WIDTH), F32),
                        pltpu.VMEM((tm + 2 * N_META, POOL_GROUP_DIM), F32),
                        pltpu.VMEM((tm + 2 * N_META, POOL_GROUP_DIM), F32),
                        pltpu.VMEM((POOL_WIDTH, D_MODEL), BF16)],
        compiler_params=_params("arbitrary"),
        name="mix",
    )(x2, a, u, u, umeta, gates, w_attn, pool_w, pool_scale, w_pool, w_out, g2, wr, br)


_TBL_EA, _TBL_EB, _TBL_NEW_A, _TBL_NEW_B, _TBL_NVALID, _TBL_NUSED = range(6)


def _route_kernel(logit_ref, tri_ref, wts_ref, dest_ref, tbl_ref, carry_sc, ids_sc, *, tme):
    i = pl.program_id(1)
    pl.when(pl.program_id(0) == 0)(functools.partial(_route_classify, i, logit_ref, tri_ref, wts_ref,
                                                     carry_sc, ids_sc))
    pl.when(pl.program_id(0) == 1)(functools.partial(_route_place, i, dest_ref, tbl_ref, carry_sc,
                                                     ids_sc, tme))


def _route_classify(i, logit_ref, tri_ref, wts_ref, carry_sc, ids_sc):
    @pl.when(i == 0)
    def _():
        carry_sc[...] = jnp.zeros_like(carry_sc)

    lg = logit_ref[...].T
    g = [lg[r:r + 1, :] for r in range(N_GROUPS)]
    gmax = functools.reduce(jnp.maximum, g)
    gsel = jnp.full_like(g[0], N_GROUPS - 1).astype(jnp.int32)
    for r in range(N_GROUPS - 2, -1, -1):
        gsel = jnp.where(g[r] == gmax, r, gsel)
    p_group = 1.0 / functools.reduce(lambda a, b: a + b, [jnp.exp(x - gmax) for x in g])
    e = []
    for jj in range(EXPERTS_PER_GROUP):
        v = lg[N_GROUPS + jj:N_GROUPS + jj + 1, :]
        for r in range(1, N_GROUPS):
            row = N_GROUPS + r * EXPERTS_PER_GROUP + jj
            v = jnp.where(gsel == r, lg[row:row + 1, :], v)
        e.append(v)
    v1 = functools.reduce(jnp.maximum, e)
    i1 = jnp.full_like(gsel, EXPERTS_PER_GROUP - 1)
    for jj in range(EXPERTS_PER_GROUP - 2, -1, -1):
        i1 = jnp.where(e[jj] == v1, jj, i1)
    rest = [jnp.where(i1 == jj, -jnp.inf, e[jj]) for jj in range(EXPERTS_PER_GROUP)]
    v2 = functools.reduce(jnp.maximum, rest)
    i2 = jnp.full_like(gsel, EXPERTS_PER_GROUP - 1)
    for jj in range(EXPERTS_PER_GROUP - 2, -1, -1):
        i2 = jnp.where(jnp.logical_and(rest[jj] == v2, i1 != jj), jj, i2)
    t = jnp.exp(v2 - v1)
    w1 = p_group / (1.0 + t)
    w2 = p_group * t / (1.0 + t)
    lo = jnp.minimum(i1, i2)
    hi = jnp.maximum(i1, i2)
    pair = jnp.where(lo == 0, hi - 1, jnp.where(lo == 1, jnp.where(hi == 3, 3, 4), 5))
    bucket = gsel * N_PAIRS + pair
    w_of_lo = jnp.where(i1 < i2, w1, w2)
    w_of_hi = jnp.where(i1 < i2, w2, w1)
    swapped = pair == 5
    w_a = jnp.where(swapped, w_of_hi, w_of_lo)
    w_b = jnp.where(swapped, w_of_lo, w_of_hi)

    tl = lg.shape[1]
    bid = lax.broadcasted_iota(jnp.int32, (N_BUCKET_ROWS, tl), 0)
    hot = bid == bucket
    hot_b = jnp.where(hot, 1.0, 0.0).astype(BF16)
    prefix = jnp.dot(hot_b, tri_ref[...], preferred_element_type=F32)
    carry = carry_sc[...]
    before = prefix + jnp.concatenate([carry] * (tl // LANES), axis=1)
    rank = jnp.sum(jnp.where(hot, before, 0.0), axis=0, keepdims=True)
    carry_sc[...] = carry + jnp.dot(hot_b, jnp.ones((tl, LANES), BF16), preferred_element_type=F32)
    pad = jnp.zeros((6, tl), F32)
    ids_sc[i] = jnp.concatenate([bucket.astype(F32), rank, pad], axis=0)
    wts_ref[...] = jnp.concatenate([w_a, w_b, pad], axis=0)


def _route_place(i, dest_ref, tbl_ref, carry_sc, ids_sc, tme):
    cnt = carry_sc[...]
    tiles = jnp.floor((cnt + (tme - 1)) * (1.0 / tme))
    r = lax.broadcasted_iota(jnp.int32, (N_BUCKET_ROWS, N_BUCKET_ROWS), 0)
    c = lax.broadcasted_iota(jnp.int32, (N_BUCKET_ROWS, N_BUCKET_ROWS), 1)
    tile_end = jnp.dot(jnp.where(c <= r, 1.0, 0.0).astype(BF16), tiles.astype(BF16),
                       preferred_element_type=F32)
    tile_start = tile_end - tiles

    ids = ids_sc[i]
    bucket, rank = ids[0:1, :], ids[1:2, :]
    tl = ids.shape[1]
    bid = lax.broadcasted_iota(jnp.int32, (N_BUCKET_ROWS, tl), 0).astype(F32)
    row_start = jnp.concatenate([tile_start * tme] * (tl // LANES), axis=1)
    dest = rank + jnp.sum(jnp.where(bid == bucket, row_start, 0.0), axis=0, keepdims=True)
    dest_ref[...] = jnp.concatenate([dest, jnp.zeros((7, tl), F32)], axis=0).astype(jnp.int32)

    lane = lax.broadcasted_iota(jnp.int32, (1, LANES), 1).astype(F32)
    n_used = tile_end[N_BUCKET_ROWS - 1:N_BUCKET_ROWS, :]
    last = jnp.minimum(lane, n_used - 1.0)
    tb = jnp.sum(jnp.where(tile_end <= last, 1.0, 0.0), axis=0, keepdims=True)
    tb = jnp.minimum(tb, N_BUCKETS - 1.0)
    group = functools.reduce(lambda a, b: a + b,
                             [jnp.where(tb >= g * N_PAIRS, 1.0, 0.0) for g in range(1, N_GROUPS)])
    pair = tb - N_PAIRS * group

    def lookup(table):
        out = jnp.full_like(pair, float(table[-1]))
        for k in range(len(table) - 2, -1, -1):
            out = jnp.where(pair == k, float(table[k]), out)
        return EXPERTS_PER_GROUP * group + out

    e_a, e_b = lookup(_PAIR_A), lookup(_PAIR_B)
    changed = lambda e: jnp.where(lane == 0.0, 1.0, jnp.where(e != pltpu.roll(e, 1, 1), 1.0, 0.0))
    bid128 = lax.broadcasted_iota(jnp.int32, (N_BUCKET_ROWS, LANES), 0).astype(F32)
    at_tile = lambda col: jnp.sum(jnp.where(bid128 == tb, col, 0.0), axis=0, keepdims=True)
    n_valid = jnp.clip(at_tile(cnt) - tme * (lane - at_tile(tile_start)), 0.0, float(tme))
    rows = [e_a, e_b, changed(e_a), changed(e_b), n_valid, n_used, jnp.zeros_like(lane),
            jnp.zeros_like(lane)]
    tbl_ref[...] = jnp.concatenate(rows, axis=0).astype(jnp.int32)


def _route(logits, tl, tme):
    n = logits.shape[0]
    nt = n // tl
    tri = jnp.asarray(np.triu(np.ones((tl, tl), np.float32), k=1), dtype=BF16)
    first_pass = lambda p, i: i * (1 - p) + (nt - 1) * p
    return pl.pallas_call(
        functools.partial(_route_kernel, tme=tme),
        grid=(2, nt),
        in_specs=[pl.BlockSpec((tl, LANES), lambda p, i: (first_pass(p, i), 0)),
                  pl.BlockSpec((tl, tl), lambda p, i: (0, 0))],
        out_specs=[pl.BlockSpec((8, tl), lambda p, i: (0, first_pass(p, i))),
                   pl.BlockSpec((8, tl), lambda p, i: (0, i * p)),
                   pl.BlockSpec((8, LANES), lambda p, i: (0, 0))],
        out_shape=[jax.ShapeDtypeStruct((8, n), F32),
                   jax.ShapeDtypeStruct((8, n), jnp.int32),
                   jax.ShapeDtypeStruct((8, LANES), jnp.int32)],
        scratch_shapes=[pltpu.VMEM((N_BUCKET_ROWS, LANES), F32), pltpu.VMEM((nt, 8, tl), F32)],
        compiler_params=_params("arbitrary", "arbitrary"),
        name="route",
    )(logits, tri)


def _token_copy(src_hbm, src_token, dst_buf, dst_row, sem):
    return pltpu.make_async_copy(
        src_hbm.at[pl.ds(pl.multiple_of(src_token * TOKEN_ROWS, TOKEN_ROWS), TOKEN_ROWS), :],
        dst_buf.at[pl.ds(pl.multiple_of(dst_row * TOKEN_ROWS, TOKEN_ROWS), TOKEN_ROWS), :],
        sem)


def _start_gather(idx_ref, first, n_groups, src_hbm, dst_buf, sem):
    def body(rr, carry):
        for uu in range(GATHER_UNROLL):
            r = rr * GATHER_UNROLL + uu
            _token_copy(src_hbm, idx_ref[first + r], dst_buf, r, sem).start(priority=uu % 2)
        return carry

    lax.fori_loop(0, n_groups, body, 0)


def _start_padded_gather(idx_ref, first, n_valid, rows, src_hbm, dst_buf, sem):
    n_groups = (n_valid + GATHER_UNROLL - 1) // GATHER_UNROLL
    _start_gather(idx_ref, first, n_groups, src_hbm, dst_buf, sem)
    pad_groups = rows // GATHER_UNROLL - n_groups
    off = n_groups * GATHER_UNROLL
    size = rows // 2
    while size >= GATHER_UNROLL:
        take = jnp.bitwise_and(pad_groups, size // GATHER_UNROLL) != 0

        @pl.when(take)
        def _(off=off, size=size):
            pltpu.make_async_copy(
                src_hbm.at[pl.ds(0, size * TOKEN_ROWS), :],
                dst_buf.at[pl.ds(pl.multiple_of(off * TOKEN_ROWS, GATHER_UNROLL * TOKEN_ROWS),
                                 size * TOKEN_ROWS), :],
                sem).start()

        off = off + jnp.where(take, size, 0)
        size //= 2


def _wait_gather(rows, src_hbm, dst_buf, sem):
    pltpu.make_async_copy(src_hbm.at[pl.ds(0, rows * TOKEN_ROWS), :], dst_buf, sem).wait()


def _moe_kernel(tbl_ref, src_ref, h2c_hbm, wts_ref, g2_ref, wg_a_ref, wu_a_ref, wd_a_ref, wg_b_ref,
                wu_b_ref, wd_b_ref, yc_ref, xbuf, wg_a_sc, wu_a_sc, wd_a_sc, wg_b_sc, wu_b_sc, wd_b_sc,
                sem, *, tme):
    i = pl.program_id(0)
    n_used = tbl_ref[_TBL_NUSED, 0]
    slot = jnp.bitwise_and(i, 1)

    @pl.when(i == 0)
    def _():
        _start_padded_gather(src_ref, 0, tbl_ref[_TBL_NVALID, 0], tme, h2c_hbm, xbuf.at[0], sem.at[0])

    @pl.when(i + 1 < n_used)
    def _():
        _start_padded_gather(src_ref, (i + 1) * tme, tbl_ref[_TBL_NVALID, i + 1], tme, h2c_hbm,
                             xbuf.at[1 - slot], sem.at[1 - slot])

    @pl.when(tbl_ref[_TBL_NEW_A, i] == 1)
    def _():
        for src, dst in ((wg_a_ref, wg_a_sc), (wu_a_ref, wu_a_sc), (wd_a_ref, wd_a_sc)):
            dst[...] = src[0].astype(BF16)

    @pl.when(tbl_ref[_TBL_NEW_B, i] == 1)
    def _():
        for src, dst in ((wg_b_ref, wg_b_sc), (wu_b_ref, wu_b_sc), (wd_b_ref, wd_b_sc)):
            dst[...] = src[0].astype(BF16)

    def tile(rows):
        _wait_gather(tme, h2c_hbm, xbuf.at[slot], sem.at[slot])
        x = _load_token_major(xbuf.at[slot], rows)
        ms = jnp.mean(x * x, axis=-1, keepdims=True)
        hn = (x * lax.rsqrt(ms + EPS) * g2_ref[...]).astype(BF16)
        wts = wts_ref[0:rows, :]

        def expert_act(wg_sc, wu_sc, w):
            gate = jnp.dot(hn, wg_sc[...], preferred_element_type=F32)
            up = jnp.dot(hn, wu_sc[...], preferred_element_type=F32)
            return (gate / (1.0 + jnp.exp(-gate)) * up * w).astype(BF16)

        y = (jnp.dot(expert_act(wg_a_sc, wu_a_sc, wts[:, 0:1]), wd_a_sc[...], preferred_element_type=F32)
             + jnp.dot(expert_act(wg_b_sc, wu_b_sc, wts[:, 1:2]), wd_b_sc[...], preferred_element_type=F32))
        _store_token_major(yc_ref, y, rows)
        if rows < tme:
            yc_ref[rows * TOKEN_ROWS:tme * TOKEN_ROWS, :] = jnp.zeros(((tme - rows) * TOKEN_ROWS, LANES), F32)

    n_valid = tbl_ref[_TBL_NVALID, i]
    pl.when(jnp.logical_and(i < n_used, n_valid > tme // 2))(functools.partial(tile, tme))
    pl.when(jnp.logical_and(i < n_used, n_valid <= tme // 2))(functools.partial(tile, tme // 2))

    @pl.when(i >= n_used)
    def _():
        yc_ref[...] = jnp.zeros_like(yc_ref)


def _moe(tbl, src, h2c, wts_sorted, g2, wg, wu, wd, tme, n_tiles):
    used = lambda i, tbl, src: (jnp.minimum(i, tbl[_TBL_NUSED, 0] - 1), 0)
    w_a = lambda i, tbl, src: (tbl[_TBL_EA, i], 0, 0)
    w_b = lambda i, tbl, src: (tbl[_TBL_EB, i], 0, 0)
    up_shape, down_shape = (D_MODEL, D_EXPERT), (D_EXPERT, D_MODEL)
    grid_spec = pltpu.PrefetchScalarGridSpec(
        num_scalar_prefetch=2,
        grid=(n_tiles,),
        in_specs=[
            pl.BlockSpec(memory_space=pl.ANY),
            pl.BlockSpec((tme, wts_sorted.shape[1]), used),
            pl.BlockSpec((1, D_MODEL), lambda i, tbl, src: (0, 0)),
            pl.BlockSpec((1,) + up_shape, w_a),
            pl.BlockSpec((1,) + up_shape, w_a),
            pl.BlockSpec((1,) + down_shape, w_a),
            pl.BlockSpec((1,) + up_shape, w_b),
            pl.BlockSpec((1,) + up_shape, w_b),
            pl.BlockSpec((1,) + down_shape, w_b),
        ],
        out_specs=pl.BlockSpec((tme * TOKEN_ROWS, LANES), lambda i, tbl, src: (i, 0)),
        scratch_shapes=[pltpu.VMEM((2, tme * TOKEN_ROWS, LANES), F32)]
        + [pltpu.VMEM(s, BF16) for s in (up_shape, up_shape, down_shape) * 2]
        + [pltpu.SemaphoreType.DMA((2,))],
    )
    return pl.pallas_call(
        functools.partial(_moe_kernel, tme=tme),
        grid_spec=grid_spec,
        out_shape=jax.ShapeDtypeStruct((n_tiles * tme * TOKEN_ROWS, LANES), F32),
        compiler_params=_params("arbitrary"),
        name="moe",
    )(tbl, src, h2c, wts_sorted, g2, wg, wu, wd, wg, wu, wd)


def _final_kernel(dest_ref, h2c_ref, yc_hbm, gf_ref, o_ref, ybuf, sem, *, tmf):
    i = pl.program_id(0)
    slot = jnp.bitwise_and(i, 1)

    @pl.when(i == 0)
    def _():
        _start_gather(dest_ref, 0, tmf // GATHER_UNROLL, yc_hbm, ybuf.at[0], sem.at[0])

    @pl.when(i + 1 < pl.num_programs(0))
    def _():
        _start_gather(dest_ref, (i + 1) * tmf, tmf // GATHER_UNROLL, yc_hbm, ybuf.at[1 - slot],
                      sem.at[1 - slot])

    _wait_gather(tmf, yc_hbm, ybuf.at[slot], sem.at[slot])
    h = _load_token_major(h2c_ref, tmf) + _load_token_major(ybuf.at[slot], tmf)
    ms = jnp.mean(h * h, axis=-1, keepdims=True)
    o_ref[...] = h * lax.rsqrt(ms + EPS) * gf_ref[...]


def _final(dest, h2c, yc, gf, tmf):
    n = h2c.shape[0] // TOKEN_ROWS
    grid_spec = pltpu.PrefetchScalarGridSpec(
        num_scalar_prefetch=1,
        grid=(n // tmf,),
        in_specs=[
            pl.BlockSpec((tmf * TOKEN_ROWS, LANES), lambda i, dest: (i, 0)),
            pl.BlockSpec(memory_space=pl.ANY),
            pl.BlockSpec((1, D_MODEL), lambda i, dest: (0, 0)),
        ],
        out_specs=pl.BlockSpec((tmf, D_MODEL), lambda i, dest: (i, 0)),
        scratch_shapes=[pltpu.VMEM((2, tmf * TOKEN_ROWS, LANES), F32), pltpu.SemaphoreType.DMA((2,))],
    )
    return pl.pallas_call(
        functools.partial(_final_kernel, tmf=tmf),
        grid_spec=grid_spec,
        out_shape=jax.ShapeDtypeStruct((n, D_MODEL), F32),
        compiler_params=_params("arbitrary"),
        name="final",
    )(dest, h2c, yc, gf)


def _rope_tables(t):
    inv = 1.0 / (ROPE_THETA ** (np.arange(0, HEAD_DIM, 2, dtype=np.float64) / HEAD_DIM))
    ang = np.arange(t, dtype=np.float64)[:, None] * inv[None, :]
    ang = np.concatenate([ang, ang, ang, ang], axis=-1)
    sign = np.where((np.arange(LANES) % HEAD_DIM) < HEAD_DIM // 2, -1.0, 1.0)
    return (jnp.asarray(np.cos(ang), dtype=F32), jnp.asarray(np.sin(ang) * sign[None, :], dtype=F32))


def kernel(x, meta, norm1_g, w_in, b_gate, lambda_q1, lambda_k1, lambda_q2, lambda_k2, subln_g, pool_w,
           pool_scale, w_attn_br, w_pool_br, w_out, norm2_g, w_router_group, b_router_group,
           w_router_expert, b_router_expert, w_e_gate, w_e_up, w_e_down, final_g):
    batch, seq, d = x.shape
    n = batch * seq
    assert w_in.shape[0] == 1 and d == D_MODEL and meta.shape == (N_META, D_MODEL)
    assert seq % 1024 == 0 and n < 2 ** 24
    x2 = x.reshape(n, d)
    cos, sin = _rope_tables(seq + N_META)
    q, k, v, u, gates, qm, km, vm, um = _inproj(x2, meta, norm1_g, cos, sin, w_in[0], b_gate, tm=1024)
    a = _attention(q, k, v, qm, km, vm, lambda_q1, lambda_k1, lambda_q2, lambda_k2, subln_g, batch, seq,
                   bq=256, bk=256)

    n_router = N_GROUPS + N_EXPERTS
    wr = jnp.pad(jnp.concatenate([w_router_group[0], w_router_expert[0]], axis=1),
                 ((0, 0), (0, LANES - n_router)))
    br = jnp.pad(jnp.concatenate([b_router_group[0], b_router_expert[0]]), (0, LANES - n_router))
    h2c, logits = _mix(x2, a, u, um, gates, w_attn_br[0].astype(BF16), pool_w[0],
                       pool_scale, w_pool_br[0], w_out[0].astype(BF16), norm2_g,
                       wr, br.reshape(1, LANES), seq, tm=1024)
    tme = 512
    n_tiles = n // tme + N_BUCKETS
    assert n_tiles <= LANES
    wts, dest, tile_table = _route(logits, tl=1024, tme=tme)
    dest = dest[0]
    per_token = jnp.concatenate([wts[0:2], jnp.arange(n, dtype=F32)[None, :]], axis=0).T
    per_row = jnp.zeros((n_tiles * tme, 3), F32).at[dest].set(per_token)
    src = per_row[:, 2].astype(jnp.int32)

    yc = _moe(tile_table, src, h2c, per_row, norm2_g, w_e_gate[0], w_e_up[0], w_e_down[0], tme, n_tiles)
    out = _final(dest, h2c, yc, final_g.reshape(1, d), tmf=512)
    return out.reshape(batch, seq, d)
```

```python
import functools
import math

import jax
import jax.numpy as jnp
import numpy as np
from jax import lax
from jax.experimental import pallas as pl
from jax.experimental.pallas import tpu as pltpu

D_MODEL = 1024
N_META = 16
N_HEADS = 8
HEAD_DIM = 64
V_DIM = 2 * HEAD_DIM
POOL_WINDOWS = (2, 4, 8, 16)
N_POOL_GROUPS = len(POOL_WINDOWS)
POOL_GROUP_DIM = 128
POOL_WIDTH = N_POOL_GROUPS * POOL_GROUP_DIM
ROPE_THETA = 10000.0
N_GROUPS = 4
EXPERTS_PER_GROUP = 4
N_EXPERTS = N_GROUPS * EXPERTS_PER_GROUP
D_EXPERT = 512
EPS = 1e-6
LAMBDA_INIT = 0.8 - 0.6 * math.exp(-0.3 * 0)
LOG2_E = math.log2(math.e)

_PAIR_A = (0, 0, 0, 1, 1, 3)
_PAIR_B = (1, 2, 3, 3, 2, 2)
N_PAIRS = len(_PAIR_A)
N_BUCKETS = N_GROUPS * N_PAIRS
N_BUCKET_ROWS = 32
GATHER_UNROLL = 16

LANES = 128
NEG_BIG = -1e30
VMEM_LIMIT = 48 * 1024 * 1024

F32 = jnp.float32
BF16 = jnp.bfloat16

_ATTN_HEADS = 4
_HEADS_PER_STEP = 4
_IN_STEPS = N_HEADS // _HEADS_PER_STEP
_QKV_BLK = _HEADS_PER_STEP * V_DIM
_U_BLK = POOL_WIDTH // _IN_STEPS
_GATE_BLK = 512
_GATES_PER_STEP = 2 * D_MODEL // _IN_STEPS // _GATE_BLK
_O_K = D_MODEL // _QKV_BLK
_O_V = 2 * D_MODEL // _QKV_BLK
_O_U = 3 * D_MODEL // _U_BLK
_O_G = (3 * D_MODEL + POOL_WIDTH) // _GATE_BLK


def _params(*sem):
    return pltpu.CompilerParams(dimension_semantics=sem, vmem_limit_bytes=VMEM_LIMIT)


TOKEN_ROWS = D_MODEL // LANES


def _store_token_major(ref, val, rows):
    for s in range(TOKEN_ROWS):
        ref[pl.ds(s, rows, stride=TOKEN_ROWS), :] = val[:, s * LANES:(s + 1) * LANES]


def _load_token_major(ref, rows):
    return jnp.concatenate([ref[pl.ds(s, rows, stride=TOKEN_ROWS), :] for s in range(TOKEN_ROWS)], axis=1)


def _rope(z, cos, sin_signed, first_half):
    outs = []
    for c in range(z.shape[1] // LANES):
        zc = z[:, c * LANES:(c + 1) * LANES]
        rot = jnp.where(first_half, pltpu.roll(zc, LANES - HEAD_DIM // 2, 1),
                        pltpu.roll(zc, HEAD_DIM // 2, 1))
        outs.append(zc * cos + rot * sin_signed)
    return jnp.concatenate(outs, axis=1)


def _inproj_kernel(x_ref, meta_ref, g_ref, cos_ref, sin_ref, cosm_ref, sinm_ref, wq_ref, wk_ref, wv_ref,
                   wu_ref, wg0_ref, wg1_ref, bg_ref, q_ref, k_ref, v_ref, u_ref, gate_ref, qm_ref, km_ref,
                   vm_ref, um_ref, wq_sc, wk_sc, wv_sc, wu_sc, wg0_sc, wg1_sc):
    def normed(x):
        ms = jnp.mean(x * x, axis=-1, keepdims=True)
        return (x * lax.rsqrt(ms + EPS) * g_ref[...]).astype(BF16)

    def project(hn, cos, sin, q_out, k_out, v_out, u_out, gate_out):
        lane = lax.broadcasted_iota(jnp.int32, cos.shape, 1)
        first_half = jnp.bitwise_and(lane, HEAD_DIM - 1) < (HEAD_DIM // 2)
        if gate_out is not None:
            for blk, wg_sc in enumerate((wg0_sc, wg1_sc)):
                cols = slice(blk * _GATE_BLK, (blk + 1) * _GATE_BLK)
                zg = jnp.dot(hn, wg_sc[...], preferred_element_type=F32) + bg_ref[:, cols]
                gate_out[:, cols] = (1.0 / (1.0 + jnp.exp(-zg))).astype(BF16)
        zq = jnp.dot(hn, wq_sc[...], preferred_element_type=F32)
        q_out[...] = (_rope(zq, cos, sin, first_half) * (LOG2_E / math.sqrt(HEAD_DIM))).astype(BF16)
        zk = jnp.dot(hn, wk_sc[...], preferred_element_type=F32)
        k_out[...] = _rope(zk, cos, sin, first_half).astype(BF16)
        u_out[...] = jnp.dot(hn, wu_sc[...], preferred_element_type=F32)
        v_out[...] = jnp.dot(hn, wv_sc[...], preferred_element_type=F32).astype(BF16)

    @pl.when(pl.program_id(1) == 0)
    def _():
        for src, dst in ((wq_ref, wq_sc), (wk_ref, wk_sc), (wv_ref, wv_sc), (wu_ref, wu_sc),
                         (wg0_ref, wg0_sc), (wg1_ref, wg1_sc)):
            dst[...] = src[...].astype(BF16)
        project(normed(meta_ref[...]), cosm_ref[...], sinm_ref[...], qm_ref, km_ref, vm_ref, um_ref, None)

    project(normed(x_ref[...]), cos_ref[...], sin_ref[...], q_ref, k_ref, v_ref, u_ref, gate_ref)


def _inproj(x2, meta, g1, cos, sin, w_in, b_gate, tm):
    n = x2.shape[0]
    n_meta = meta.shape[0]
    n_pos_blocks = (cos.shape[0] - n_meta) // tm
    row = lambda j, i: (i, 0)
    pos = lambda j, i: (i % n_pos_blocks, 0)
    whole = lambda j, i: (0, 0)
    once = pl.Buffered(1)
    gate_cols = _GATES_PER_STEP * _GATE_BLK
    assert _GATES_PER_STEP == 2
    return pl.pallas_call(
        _inproj_kernel,
        grid=(_IN_STEPS, n // tm),
        in_specs=[
            pl.BlockSpec((tm, D_MODEL), row),
            pl.BlockSpec((n_meta, D_MODEL), whole),
            pl.BlockSpec((1, D_MODEL), whole),
            pl.BlockSpec((tm, LANES), pos),
            pl.BlockSpec((tm, LANES), pos),
            pl.BlockSpec((n_meta, LANES), whole),
            pl.BlockSpec((n_meta, LANES), whole),
            pl.BlockSpec((D_MODEL, _QKV_BLK), lambda j, i: (0, j), pipeline_mode=once),
            pl.BlockSpec((D_MODEL, _QKV_BLK), lambda j, i: (0, _O_K + j), pipeline_mode=once),
            pl.BlockSpec((D_MODEL, _QKV_BLK), lambda j, i: (0, _O_V + j), pipeline_mode=once),
            pl.BlockSpec((D_MODEL, _U_BLK), lambda j, i: (0, _O_U + j), pipeline_mode=once),
            pl.BlockSpec((D_MODEL, _GATE_BLK), lambda j, i: (0, _O_G + _GATES_PER_STEP * j),
                         pipeline_mode=once),
            pl.BlockSpec((D_MODEL, _GATE_BLK), lambda j, i: (0, _O_G + _GATES_PER_STEP * j + 1),
                         pipeline_mode=once),
            pl.BlockSpec((1, gate_cols), lambda j, i: (0, j)),
        ],
        out_specs=[
            pl.BlockSpec((tm, _QKV_BLK), lambda j, i: (i, j)),
            pl.BlockSpec((tm, _QKV_BLK), lambda j, i: (i, j)),
            pl.BlockSpec((tm, _QKV_BLK), lambda j, i: (i, j)),
            pl.BlockSpec((tm, _U_BLK), lambda j, i: (i, j)),
            pl.BlockSpec((tm, gate_cols), lambda j, i: (i, j)),
            pl.BlockSpec((n_meta, _QKV_BLK), lambda j, i: (0, j)),
            pl.BlockSpec((n_meta, _QKV_BLK), lambda j, i: (0, j)),
            pl.BlockSpec((n_meta, _QKV_BLK), lambda j, i: (0, j)),
            pl.BlockSpec((n_meta, _U_BLK), lambda j, i: (0, j)),
        ],
        out_shape=[
            jax.ShapeDtypeStruct((n, D_MODEL), BF16),
            jax.ShapeDtypeStruct((n, D_MODEL), BF16),
            jax.ShapeDtypeStruct((n, D_MODEL), BF16),
            jax.ShapeDtypeStruct((n, POOL_WIDTH), F32),
            jax.ShapeDtypeStruct((n, 2 * D_MODEL), BF16),
            jax.ShapeDtypeStruct((n_meta, D_MODEL), BF16),
            jax.ShapeDtypeStruct((n_meta, D_MODEL), BF16),
            jax.ShapeDtypeStruct((n_meta, D_MODEL), BF16),
            jax.ShapeDtypeStruct((n_meta, POOL_WIDTH), F32),
        ],
        scratch_shapes=[
            pltpu.VMEM((D_MODEL, _QKV_BLK), BF16),
            pltpu.VMEM((D_MODEL, _QKV_BLK), BF16),
            pltpu.VMEM((D_MODEL, _QKV_BLK), BF16),
            pltpu.VMEM((D_MODEL, _U_BLK), BF16),
            pltpu.VMEM((D_MODEL, _GATE_BLK), BF16),
            pltpu.VMEM((D_MODEL, _GATE_BLK), BF16),
        ],
        compiler_params=_params("arbitrary", "arbitrary"),
        name="inproj",
    )(x2, meta, g1, cos[n_meta:], sin[n_meta:], cos[:n_meta], sin[:n_meta], w_in, w_in, w_in, w_in, w_in,
      w_in, b_gate)


def _attn_kernel(lq1_ref, lk1_ref, lq2_ref, lk2_ref, subg_ref, q_ref, k_ref, v_ref, qm_ref, km_ref,
                 vm_ref, o_ref, *state_sc, bq, bk):
    seq = k_ref.shape[0]
    nq = seq // bq
    heads = k_ref.shape[1] // V_DIM
    chains = [(h, i) for i in range(nq, -1, -1) for h in range(heads)]
    states = {c: tuple(state_sc[3 * g:3 * g + 3]) for g, c in enumerate(chains)}

    def rows_at(ref, meta_ref, h, p0, n):
        cols = slice(h * V_DIM, (h + 1) * V_DIM)
        if p0 == 0:
            return jnp.concatenate([meta_ref[:, cols], ref[0:n - N_META, cols]], axis=0)
        real = ref[p0 - N_META:min(p0 - N_META + n, seq), cols]
        if real.shape[0] < n:
            real = jnp.concatenate([real, jnp.zeros((n - real.shape[0], real.shape[1]), real.dtype)], axis=0)
        return real

    def block_rows(i):
        return (i * bq, bq) if i < nq else (nq * bq, N_META)

    def n_full_tiles(i):
        return (block_rows(i)[0] + 1) // bk

    def partial_tiles(i):
        p0, n = block_rows(i)
        if i == nq:
            return [(n_full_tiles(i) * bk, LANES, p0 - n_full_tiles(i) * bk)]
        return [(t * bk, bk, p0 - t * bk) for t in range(n_full_tiles(i), (p0 + n - 1) // bk + 1)]

    lam = (jnp.exp(jnp.sum(lq1_ref[...] * lk1_ref[...], axis=-1, keepdims=True))
           - jnp.exp(jnp.sum(lq2_ref[...] * lk2_ref[...], axis=-1, keepdims=True)) + LAMBDA_INIT)

    bias_cache = {}

    def causal_bias(n_rows, n_cols, offset):
        if (n_rows, n_cols, offset) not in bias_cache:
            r = lax.broadcasted_iota(jnp.int32, (n_rows, n_cols), 0)
            c = lax.broadcasted_iota(jnp.int32, (n_rows, n_cols), 1)
            bias = jnp.where(c <= r + offset, 0.0, NEG_BIG)
            bias_cache[(n_rows, n_cols, offset)] = jnp.concatenate([bias, bias], axis=0)
        return bias_cache[(n_rows, n_cols, offset)]

    def load_q(h, i, qq_sc):
        p0, n = block_rows(i)
        q = rows_at(q_ref, qm_ref, h, p0, n)
        lane = lax.broadcasted_iota(jnp.int32, q.shape, 1)
        zero = jnp.zeros_like(q)
        qq_sc[0:n, :] = jnp.where(lane < HEAD_DIM, q, zero)
        qq_sc[n:2 * n, :] = jnp.where(lane >= HEAD_DIM, q, zero)

    def step(state, kb, vb, bias, first):
        qq_sc, m_sc, acc_sc = state
        s = lax.dot_general(qq_sc[...], kb, (((1,), (1,)), ((), ())), preferred_element_type=F32)
        if bias is not None:
            s = s + bias
        n_tiles = s.shape[1] // LANES
        rm = s[:, 0:LANES]
        for t in range(1, n_tiles):
            rm = jnp.maximum(rm, s[:, t * LANES:(t + 1) * LANES])
        rmax = jnp.max(rm, axis=-1, keepdims=True)
        if first:
            m_new = jnp.broadcast_to(rmax, (s.shape[0], LANES))
        else:
            m_prev = m_sc[...]
            m_new = jnp.maximum(m_prev, rmax)
        p = jnp.exp2(s - jnp.concatenate([m_new] * n_tiles, axis=1))
        pv = jnp.dot(p.astype(BF16), vb, preferred_element_type=F32)
        if first:
            acc_sc[...] = pv
        else:
            alpha = jnp.exp2(m_prev - m_new)
            acc_sc[...] = jnp.concatenate([alpha, alpha], axis=1) * acc_sc[...] + pv
        m_sc[...] = m_new

    def chain_steps(i):
        n = block_rows(i)[1]
        steps = [(p0, width, causal_bias(n, width, offset)) for p0, width, offset in partial_tiles(i)]
        return steps + [(t * bk, bk, None) for t in range(n_full_tiles(i))]

    def key_step(state, h, p0, width, bias, first):
        kb = rows_at(k_ref, km_ref, h, p0, width)
        vb = jnp.concatenate([rows_at(v_ref, vm_ref, h, p0, width), jnp.ones((width, V_DIM), BF16)],
                             axis=1)
        step(state, kb, vb, bias, first)

    def finish(h, i, acc_sc):
        p0, n = block_rows(i)
        cols = slice(h * V_DIM, (h + 1) * V_DIM)
        o1 = acc_sc[0:n, 0:V_DIM] / acc_sc[0:n, V_DIM:2 * V_DIM]
        o2 = acc_sc[n:2 * n, 0:V_DIM] / acc_sc[n:2 * n, V_DIM:2 * V_DIM]
        o = o1 - lam * o2
        ms = jnp.mean(o * o, axis=-1, keepdims=True)
        o = (o * lax.rsqrt(ms + EPS) * subg_ref[...] * (1.0 - LAMBDA_INIT)).astype(BF16)
        if p0 == 0:
            o_ref[0:n - N_META, cols] = o[N_META:n, :]
        else:
            o_ref[p0 - N_META:p0 - N_META + n, cols] = o

    plans = {i: chain_steps(i) for i in range(nq + 1)}
    for h, i in chains:
        load_q(h, i, states[h, i][0])
    for j in range(max(len(plan) for plan in plans.values())):
        for h, i in chains:
            if j < len(plans[i]):
                key_step(states[h, i], h, *plans[i][j], first=(j == 0))
    for h, i in chains:
        finish(h, i, states[h, i][2])


def _attention(q, k, v, qm, km, vm, lq1, lk1, lq2, lk2, subg, batch, seq, bq, bk):
    nq = seq // bq
    width = _ATTN_HEADS * V_DIM
    small = lambda b, h: (0, 0)
    head = lambda b, h: (b, h)
    meta = lambda b, h: (0, h)

    def state_scratch(rows):
        return [pltpu.VMEM((2 * rows, V_DIM), BF16), pltpu.VMEM((2 * rows, LANES), F32),
                pltpu.VMEM((2 * rows, 2 * V_DIM), F32)]

    return pl.pallas_call(
        functools.partial(_attn_kernel, bq=bq, bk=bk),
        grid=(batch, N_HEADS // _ATTN_HEADS),
        in_specs=[
            pl.BlockSpec((1, HEAD_DIM), small),
            pl.BlockSpec((1, HEAD_DIM), small),
            pl.BlockSpec((1, HEAD_DIM), small),
            pl.BlockSpec((1, HEAD_DIM), small),
            pl.BlockSpec((1, V_DIM), small),
            pl.BlockSpec((seq, width), head),
            pl.BlockSpec((seq, width), head),
            pl.BlockSpec((seq, width), head),
            pl.BlockSpec((N_META, width), meta),
            pl.BlockSpec((N_META, width), meta),
            pl.BlockSpec((N_META, width), meta),
        ],
        out_specs=pl.BlockSpec((seq, width), head),
        out_shape=jax.ShapeDtypeStruct(q.shape, BF16),
        scratch_shapes=state_scratch(N_META) * _ATTN_HEADS + state_scratch(bq) * (nq * _ATTN_HEADS),
        compiler_params=_params("arbitrary", "arbitrary"),
        name="diff_attn",
    )(lq1, lk1, lq2, lk2, subg, q, k, v, qm, km, vm)


def _mix_kernel(x_ref, a_ref, u_ref, uprev_ref, umeta_ref, gate_ref, wa_ref, pw_ref, ps_ref, wp_ref,
                wo_ref, g2_ref, wr_ref, br_ref, h2c_ref, logit_ref, ext_sc, sum_a_sc, sum_b_sc, wfold_sc,
                *, tm, tiles_per_seq):
    i = pl.program_id(0)
    first = (i % tiles_per_seq) == 0

    @pl.when(i == 0)
    def _():
        for g in range(N_POOL_GROUPS):
            rows = slice(g * POOL_GROUP_DIM, (g + 1) * POOL_GROUP_DIM)
            left = pw_ref[g] * ps_ref[:, rows]
            right = wp_ref[rows, :]
            left_hi, right_hi = left.astype(BF16), right.astype(BF16)
            left_lo = (left - left_hi.astype(F32)).astype(BF16)
            right_lo = (right - right_hi.astype(F32)).astype(BF16)
            wfold_sc[rows, :] = (jnp.dot(left_hi, right_hi, preferred_element_type=F32)
                                 + jnp.dot(left_lo, right_hi, preferred_element_type=F32)
                                 + jnp.dot(left_hi, right_lo, preferred_element_type=F32)).astype(BF16)

    y_attn = jnp.dot(a_ref[...], wa_ref[...], preferred_element_type=F32)

    top = 2 * N_META
    for ref in (ext_sc, sum_a_sc, sum_b_sc):
        ref[0:N_META, :] = jnp.zeros((N_META, ref.shape[1]), F32)

    @pl.when(first)
    def _():
        ext_sc[N_META:top, :] = umeta_ref[...]

    @pl.when(jnp.logical_not(first))
    def _():
        ext_sc[N_META:top, :] = uprev_ref[...]

    ext_sc[top:top + tm, :] = u_ref[...]

    pooled = []
    for g, w in enumerate(POOL_WINDOWS):
        cs = slice(g * POOL_GROUP_DIM, (g + 1) * POOL_GROUP_DIM)
        src, lanes, shift = ext_sc, cs, 1
        for dst in (sum_a_sc, sum_b_sc, sum_a_sc, sum_b_sc):
            if shift >= w:
                break
            dst[N_META:top + tm, :] = (src[N_META:top + tm, lanes]
                                       + src[N_META - shift:top + tm - shift, lanes])
            src, lanes, shift = dst, slice(None), 2 * shift
        pooled.append((src[top:top + tm, lanes] * (1.0 / w) - ext_sc[top:top + tm, cs]).astype(BF16))
    y_pool = jnp.dot(jnp.concatenate(pooled, axis=1), wfold_sc[...], preferred_element_type=F32)
    mixed = (gate_ref[:, 0:D_MODEL].astype(F32) * y_attn
             + gate_ref[:, D_MODEL:2 * D_MODEL].astype(F32) * y_pool)
    h2 = x_ref[...] + jnp.dot(mixed.astype(BF16), wo_ref[...], preferred_element_type=F32)
    _store_token_major(h2c_ref, h2, tm)
    ms = jnp.mean(h2 * h2, axis=-1, keepdims=True)
    hn2 = h2 * lax.rsqrt(ms + EPS) * g2_ref[...]
    hn2_hi = hn2.astype(BF16)
    hn2_lo = (hn2 - hn2_hi.astype(F32)).astype(BF16)
    wr = wr_ref[...]
    wr_hi = wr.astype(BF16)
    wr_lo = (wr - wr_hi.astype(F32)).astype(BF16)
    wr_both = jnp.concatenate([wr_hi, wr_lo], axis=1)
    for half in range(2):
        rows = slice(half * tm // 2, (half + 1) * tm // 2)
        both = jnp.dot(hn2_hi[rows, :], wr_both, preferred_element_type=F32)
        logit_ref[rows, :] = (both[:, 0:LANES] + both[:, LANES:2 * LANES]
                              + jnp.dot(hn2_lo[rows, :], wr_hi, preferred_element_type=F32) + br_ref[...])


def _mix(x2, a, u, umeta, gates, w_attn, pool_w, pool_scale, w_pool, w_out, g2, wr, br, seq, tm):
    n = x2.shape[0]
    tiles_per_seq = seq // tm
    halo_blocks = tm // N_META
    const2 = lambda i: (0, 0)
    return pl.pallas_call(
        functools.partial(_mix_kernel, tm=tm, tiles_per_seq=tiles_per_seq),
        grid=(n // tm,),
        in_specs=[
            pl.BlockSpec((tm, D_MODEL), lambda i: (i, 0)),
            pl.BlockSpec((tm, D_MODEL), lambda i: (i, 0)),
            pl.BlockSpec((tm, POOL_WIDTH), lambda i: (i, 0)),
            pl.BlockSpec((N_META, POOL_WIDTH), lambda i: (jnp.maximum(i * halo_blocks - 1, 0), 0)),
            pl.BlockSpec((N_META, POOL_WIDTH), const2),
            pl.BlockSpec((tm, 2 * D_MODEL), lambda i: (i, 0)),
            pl.BlockSpec((D_MODEL, D_MODEL), const2, pipeline_mode=pl.Buffered(1)),
            pl.BlockSpec((N_POOL_GROUPS, POOL_GROUP_DIM, POOL_GROUP_DIM), lambda i: (0, 0, 0),
                         pipeline_mode=pl.Buffered(1)),
            pl.BlockSpec((1, POOL_WIDTH), const2),
            pl.BlockSpec((POOL_WIDTH, D_MODEL), const2, pipeline_mode=pl.Buffered(1)),
            pl.BlockSpec((D_MODEL, D_MODEL), const2, pipeline_mode=pl.Buffered(1)),
            pl.BlockSpec((1, D_MODEL), const2),
            pl.BlockSpec((D_MODEL, LANES), const2, pipeline_mode=pl.Buffered(1)),
            pl.BlockSpec((1, LANES), const2),
        ],
        out_specs=[
            pl.BlockSpec((tm * TOKEN_ROWS, LANES), lambda i: (i, 0)),
            pl.BlockSpec((tm, LANES), lambda i: (i, 0)),
        ],
        out_shape=[
            jax.ShapeDtypeStruct((n * TOKEN_ROWS, LANES), F32),
            jax.ShapeDtypeStruct((n, LANES), F32),
        ],
        scratch_shapes=[pltpu.VMEM((tm + 2 * N_META, POOL_WIDTH), F32),
                        pltpu.VMEM((tm + 2 * N_META, POOL_GROUP_DIM), F32),
                        pltpu.VMEM((tm + 2 * N_META, POOL_GROUP_DIM), F32),
                        pltpu.VMEM((POOL_WIDTH, D_MODEL), BF16)],
        compiler_params=_params("arbitrary"),
        name="mix",
    )(x2, a, u, u, umeta, gates, w_attn, pool_w, pool_scale, w_pool, w_out, g2, wr, br)


_TBL_EA, _TBL_EB, _TBL_NEW_A, _TBL_NEW_B, _TBL_NVALID, _TBL_NUSED = range(6)


def _route_kernel(logit_ref, tri_ref, wts_ref, dest_ref, tbl_ref, carry_sc, ids_sc, *, tme):
    i = pl.program_id(1)
    pl.when(pl.program_id(0) == 0)(functools.partial(_route_classify, i, logit_ref, tri_ref, wts_ref,
                                                     carry_sc, ids_sc))
    pl.when(pl.program_id(0) == 1)(functools.partial(_route_place, i, dest_ref, tbl_ref, carry_sc,
                                                     ids_sc, tme))


def _route_classify(i, logit_ref, tri_ref, wts_ref, carry_sc, ids_sc):
    @pl.when(i == 0)
    def _():
        carry_sc[...] = jnp.zeros_like(carry_sc)

    lg = logit_ref[...].T
    g = [lg[r:r + 1, :] for r in range(N_GROUPS)]
    gmax = functools.reduce(jnp.maximum, g)
    gsel = jnp.full_like(g[0], N_GROUPS - 1).astype(jnp.int32)
    for r in range(N_GROUPS - 2, -1, -1):
        gsel = jnp.where(g[r] == gmax, r, gsel)
    p_group = 1.0 / functools.reduce(lambda a, b: a + b, [jnp.exp(x - gmax) for x in g])
    e = []
    for jj in range(EXPERTS_PER_GROUP):
        v = lg[N_GROUPS + jj:N_GROUPS + jj + 1, :]
        for r in range(1, N_GROUPS):
            row = N_GROUPS + r * EXPERTS_PER_GROUP + jj
            v = jnp.where(gsel == r, lg[row:row + 1, :], v)
        e.append(v)
    v1 = functools.reduce(jnp.maximum, e)
    i1 = jnp.full_like(gsel, EXPERTS_PER_GROUP - 1)
    for jj in range(EXPERTS_PER_GROUP - 2, -1, -1):
        i1 = jnp.where(e[jj] == v1, jj, i1)
    rest = [jnp.where(i1 == jj, -jnp.inf, e[jj]) for jj in range(EXPERTS_PER_GROUP)]
    v2 = functools.reduce(jnp.maximum, rest)
    i2 = jnp.full_like(gsel, EXPERTS_PER_GROUP - 1)
    for jj in range(EXPERTS_PER_GROUP - 2, -1, -1):
        i2 = jnp.where(jnp.logical_and(rest[jj] == v2, i1 != jj), jj, i2)
    t = jnp.exp(v2 - v1)
    w1 = p_group / (1.0 + t)
    w2 = p_group * t / (1.0 + t)
    lo = jnp.minimum(i1, i2)
    hi = jnp.maximum(i1, i2)
    pair = jnp.where(lo == 0, hi - 1, jnp.where(lo == 1, jnp.where(hi == 3, 3, 4), 5))
    bucket = gsel * N_PAIRS + pair
    w_of_lo = jnp.where(i1 < i2, w1, w2)
    w_of_hi = jnp.where(i1 < i2, w2, w1)
    swapped = pair == 5
    w_a = jnp.where(swapped, w_of_hi, w_of_lo)
    w_b = jnp.where(swapped, w_of_lo, w_of_hi)

    tl = lg.shape[1]
    bid = lax.broadcasted_iota(jnp.int32, (N_BUCKET_ROWS, tl), 0)
    hot = bid == bucket
    hot_b = jnp.where(hot, 1.0, 0.0).astype(BF16)
    prefix = jnp.dot(hot_b, tri_ref[...], preferred_element_type=F32)
    carry = carry_sc[...]
    before = prefix + jnp.concatenate([carry] * (tl // LANES), axis=1)
    rank = jnp.sum(jnp.where(hot, before, 0.0), axis=0, keepdims=True)
    carry_sc[...] = carry + jnp.dot(hot_b, jnp.ones((tl, LANES), BF16), preferred_element_type=F32)
    pad = jnp.zeros((6, tl), F32)
    ids_sc[i] = jnp.concatenate([bucket.astype(F32), rank, pad], axis=0)
    wts_ref[...] = jnp.concatenate([w_a, w_b, pad], axis=0)


def _route_place(i, dest_ref, tbl_ref, carry_sc, ids_sc, tme):
    cnt = carry_sc[...]
    tiles = jnp.floor((cnt + (tme - 1)) * (1.0 / tme))
    r = lax.broadcasted_iota(jnp.int32, (N_BUCKET_ROWS, N_BUCKET_ROWS), 0)
    c = lax.broadcasted_iota(jnp.int32, (N_BUCKET_ROWS, N_BUCKET_ROWS), 1)
    tile_end = jnp.dot(jnp.where(c <= r, 1.0, 0.0).astype(BF16), tiles.astype(BF16),
                       preferred_element_type=F32)
    tile_start = tile_end - tiles

    ids = ids_sc[i]
    bucket, rank = ids[0:1, :], ids[1:2, :]
    tl = ids.shape[1]
    bid = lax.broadcasted_iota(jnp.int32, (N_BUCKET_ROWS, tl), 0).astype(F32)
    row_start = jnp.concatenate([tile_start * tme] * (tl // LANES), axis=1)
    dest = rank + jnp.sum(jnp.where(bid == bucket, row_start, 0.0), axis=0, keepdims=True)
    dest_ref[...] = jnp.concatenate([dest, jnp.zeros((7, tl), F32)], axis=0).astype(jnp.int32)

    lane = lax.broadcasted_iota(jnp.int32, (1, LANES), 1).astype(F32)
    n_used = tile_end[N_BUCKET_ROWS - 1:N_BUCKET_ROWS, :]
    last = jnp.minimum(lane, n_used - 1.0)
    tb = jnp.sum(jnp.where(tile_end <= last, 1.0, 0.0), axis=0, keepdims=True)
    tb = jnp.minimum(tb, N_BUCKETS - 1.0)
    group = functools.reduce(lambda a, b: a + b,
                             [jnp.where(tb >= g * N_PAIRS, 1.0, 0.0) for g in range(1, N_GROUPS)])
    pair = tb - N_PAIRS * group

    def lookup(table):
        out = jnp.full_like(pair, float(table[-1]))
        for k in range(len(table) - 2, -1, -1):
            out = jnp.where(pair == k, float(table[k]), out)
        return EXPERTS_PER_GROUP * group + out

    e_a, e_b = lookup(_PAIR_A), lookup(_PAIR_B)
    changed = lambda e: jnp.where(lane == 0.0, 1.0, jnp.where(e != pltpu.roll(e, 1, 1), 1.0, 0.0))
    bid128 = lax.broadcasted_iota(jnp.int32, (N_BUCKET_ROWS, LANES), 0).astype(F32)
    at_tile = lambda col: jnp.sum(jnp.where(bid128 == tb, col, 0.0), axis=0, keepdims=True)
    n_valid = jnp.clip(at_tile(cnt) - tme * (lane - at_tile(tile_start)), 0.0, float(tme))
    rows = [e_a, e_b, changed(e_a), changed(e_b), n_valid, n_used, jnp.zeros_like(lane),
            jnp.zeros_like(lane)]
    tbl_ref[...] = jnp.concatenate(rows, axis=0).astype(jnp.int32)


def _route(logits, tl, tme):
    n = logits.shape[0]
    nt = n // tl
    tri = jnp.asarray(np.triu(np.ones((tl, tl), np.float32), k=1), dtype=BF16)
    first_pass = lambda p, i: i * (1 - p) + (nt - 1) * p
    return pl.pallas_call(
        functools.partial(_route_kernel, tme=tme),
        grid=(2, nt),
        in_specs=[pl.BlockSpec((tl, LANES), lambda p, i: (first_pass(p, i), 0)),
                  pl.BlockSpec((tl, tl), lambda p, i: (0, 0))],
        out_specs=[pl.BlockSpec((8, tl), lambda p, i: (0, first_pass(p, i))),
                   pl.BlockSpec((8, tl), lambda p, i: (0, i * p)),
                   pl.BlockSpec((8, LANES), lambda p, i: (0, 0))],
        out_shape=[jax.ShapeDtypeStruct((8, n), F32),
                   jax.ShapeDtypeStruct((8, n), jnp.int32),
                   jax.ShapeDtypeStruct((8, LANES), jnp.int32)],
        scratch_shapes=[pltpu.VMEM((N_BUCKET_ROWS, LANES), F32), pltpu.VMEM((nt, 8, tl), F32)],
        compiler_params=_params("arbitrary", "arbitrary"),
        name="route",
    )(logits, tri)


def _token_copy(src_hbm, src_token, dst_buf, dst_row, sem):
    return pltpu.make_async_copy(
        src_hbm.at[pl.ds(pl.multiple_of(src_token * TOKEN_ROWS, TOKEN_ROWS), TOKEN_ROWS), :],
        dst_buf.at[pl.ds(pl.multiple_of(dst_row * TOKEN_ROWS, TOKEN_ROWS), TOKEN_ROWS), :],
        sem)


def _start_gather(idx_ref, first, n_groups, src_hbm, dst_buf, sem):
    def body(rr, carry):
        for uu in range(GATHER_UNROLL):
            r = rr * GATHER_UNROLL + uu
            _token_copy(src_hbm, idx_ref[first + r], dst_buf, r, sem).start(priority=uu % 2)
        return carry

    lax.fori_loop(0, n_groups, body, 0)


def _start_padded_gather(idx_ref, first, n_valid, rows, src_hbm, dst_buf, sem):
    n_groups = (n_valid + GATHER_UNROLL - 1) // GATHER_UNROLL
    _start_gather(idx_ref, first, n_groups, src_hbm, dst_buf, sem)
    pad_groups = rows // GATHER_UNROLL - n_groups
    off = n_groups * GATHER_UNROLL
    size = rows // 2
    while size >= GATHER_UNROLL:
        take = jnp.bitwise_and(pad_groups, size // GATHER_UNROLL) != 0

        @pl.when(take)
        def _(off=off, size=size):
            pltpu.make_async_copy(
                src_hbm.at[pl.ds(0, size * TOKEN_ROWS), :],
                dst_buf.at[pl.ds(pl.multiple_of(off * TOKEN_ROWS, GATHER_UNROLL * TOKEN_ROWS),
                                 size * TOKEN_ROWS), :],
                sem).start()

        off = off + jnp.where(take, size, 0)
        size //= 2


def _wait_gather(rows, src_hbm, dst_buf, sem):
    pltpu.make_async_copy(src_hbm.at[pl.ds(0, rows * TOKEN_ROWS), :], dst_buf, sem).wait()


def _moe_kernel(tbl_ref, src_ref, h2c_hbm, wts_ref, g2_ref, wg_a_ref, wu_a_ref, wd_a_ref, wg_b_ref,
                wu_b_ref, wd_b_ref, yc_ref, xbuf, wg_a_sc, wu_a_sc, wd_a_sc, wg_b_sc, wu_b_sc, wd_b_sc,
                sem, *, tme):
    i = pl.program_id(0)
    n_used = tbl_ref[_TBL_NUSED, 0]
    slot = jnp.bitwise_and(i, 1)

    @pl.when(i == 0)
    def _():
        _start_padded_gather(src_ref, 0, tbl_ref[_TBL_NVALID, 0], tme, h2c_hbm, xbuf.at[0], sem.at[0])

    @pl.when(i + 1 < n_used)
    def _():
        _start_padded_gather(src_ref, (i + 1) * tme, tbl_ref[_TBL_NVALID, i + 1], tme, h2c_hbm,
                             xbuf.at[1 - slot], sem.at[1 - slot])

    @pl.when(tbl_ref[_TBL_NEW_A, i] == 1)
    def _():
        for src, dst in ((wg_a_ref, wg_a_sc), (wu_a_ref, wu_a_sc), (wd_a_ref, wd_a_sc)):
            dst[...] = src[0].astype(BF16)

    @pl.when(tbl_ref[_TBL_NEW_B, i] == 1)
    def _():
        for src, dst in ((wg_b_ref, wg_b_sc), (wu_b_ref, wu_b_sc), (wd_b_ref, wd_b_sc)):
            dst[...] = src[0].astype(BF16)

    def tile(rows):
        _wait_gather(tme, h2c_hbm, xbuf.at[slot], sem.at[slot])
        x = _load_token_major(xbuf.at[slot], rows)
        ms = jnp.mean(x * x, axis=-1, keepdims=True)
        hn = (x * lax.rsqrt(ms + EPS) * g2_ref[...]).astype(BF16)
        wts = wts_ref[0:rows, :]

        def expert_act(wg_sc, wu_sc, w):
            gate = jnp.dot(hn, wg_sc[...], preferred_element_type=F32)
            up = jnp.dot(hn, wu_sc[...], preferred_element_type=F32)
            return (gate / (1.0 + jnp.exp(-gate)) * up * w).astype(BF16)

        y = (jnp.dot(expert_act(wg_a_sc, wu_a_sc, wts[:, 0:1]), wd_a_sc[...], preferred_element_type=F32)
             + jnp.dot(expert_act(wg_b_sc, wu_b_sc, wts[:, 1:2]), wd_b_sc[...], preferred_element_type=F32))
        _store_token_major(yc_ref, y, rows)
        if rows < tme:
            yc_ref[rows * TOKEN_ROWS:tme * TOKEN_ROWS, :] = jnp.zeros(((tme - rows) * TOKEN_ROWS, LANES), F32)

    n_valid = tbl_ref[_TBL_NVALID, i]
    pl.when(jnp.logical_and(i < n_used, n_valid > tme // 2))(functools.partial(tile, tme))
    pl.when(jnp.logical_and(i < n_used, n_valid <= tme // 2))(functools.partial(tile, tme // 2))

    @pl.when(i >= n_used)
    def _():
        yc_ref[...] = jnp.zeros_like(yc_ref)


def _moe(tbl, src, h2c, wts_sorted, g2, wg, wu, wd, tme, n_tiles):
    used = lambda i, tbl, src: (jnp.minimum(i, tbl[_TBL_NUSED, 0] - 1), 0)
    w_a = lambda i, tbl, src: (tbl[_TBL_EA, i], 0, 0)
    w_b = lambda i, tbl, src: (tbl[_TBL_EB, i], 0, 0)
    up_shape, down_shape = (D_MODEL, D_EXPERT), (D_EXPERT, D_MODEL)
    grid_spec = pltpu.PrefetchScalarGridSpec(
        num_scalar_prefetch=2,
        grid=(n_tiles,),
        in_specs=[
            pl.BlockSpec(memory_space=pl.ANY),
            pl.BlockSpec((tme, wts_sorted.shape[1]), used),
            pl.BlockSpec((1, D_MODEL), lambda i, tbl, src: (0, 0)),
            pl.BlockSpec((1,) + up_shape, w_a),
            pl.BlockSpec((1,) + up_shape, w_a),
            pl.BlockSpec((1,) + down_shape, w_a),
            pl.BlockSpec((1,) + up_shape, w_b),
            pl.BlockSpec((1,) + up_shape, w_b),
            pl.BlockSpec((1,) + down_shape, w_b),
        ],
        out_specs=pl.BlockSpec((tme * TOKEN_ROWS, LANES), lambda i, tbl, src: (i, 0)),
        scratch_shapes=[pltpu.VMEM((2, tme * TOKEN_ROWS, LANES), F32)]
        + [pltpu.VMEM(s, BF16) for s in (up_shape, up_shape, down_shape) * 2]
        + [pltpu.SemaphoreType.DMA((2,))],
    )
    return pl.pallas_call(
        functools.partial(_moe_kernel, tme=tme),
        grid_spec=grid_spec,
        out_shape=jax.ShapeDtypeStruct((n_tiles * tme * TOKEN_ROWS, LANES), F32),
        compiler_params=_params("arbitrary"),
        name="moe",
    )(tbl, src, h2c, wts_sorted, g2, wg, wu, wd, wg, wu, wd)


def _final_kernel(dest_ref, h2c_ref, yc_hbm, gf_ref, o_ref, ybuf, sem, *, tmf):
    i = pl.program_id(0)
    slot = jnp.bitwise_and(i, 1)

    @pl.when(i == 0)
    def _():
        _start_gather(dest_ref, 0, tmf // GATHER_UNROLL, yc_hbm, ybuf.at[0], sem.at[0])

    @pl.when(i + 1 < pl.num_programs(0))
    def _():
        _start_gather(dest_ref, (i + 1) * tmf, tmf // GATHER_UNROLL, yc_hbm, ybuf.at[1 - slot],
                      sem.at[1 - slot])

    _wait_gather(tmf, yc_hbm, ybuf.at[slot], sem.at[slot])
    h = _load_token_major(h2c_ref, tmf) + _load_token_major(ybuf.at[slot], tmf)
    ms = jnp.mean(h * h, axis=-1, keepdims=True)
    o_ref[...] = h * lax.rsqrt(ms + EPS) * gf_ref[...]


def _final(dest, h2c, yc, gf, tmf):
    n = h2c.shape[0] // TOKEN_ROWS
    grid_spec = pltpu.PrefetchScalarGridSpec(
        num_scalar_prefetch=1,
        grid=(n // tmf,),
        in_specs=[
            pl.BlockSpec((tmf * TOKEN_ROWS, LANES), lambda i, dest: (i, 0)),
            pl.BlockSpec(memory_space=pl.ANY),
            pl.BlockSpec((1, D_MODEL), lambda i, dest: (0, 0)),
        ],
        out_specs=pl.BlockSpec((tmf, D_MODEL), lambda i, dest: (i, 0)),
        scratch_shapes=[pltpu.VMEM((2, tmf * TOKEN_ROWS, LANES), F32), pltpu.SemaphoreType.DMA((2,))],
    )
    return pl.pallas_call(
        functools.partial(_final_kernel, tmf=tmf),
        grid_spec=grid_spec,
        out_shape=jax.ShapeDtypeStruct((n, D_MODEL), F32),
        compiler_params=_params("arbitrary"),
        name="final",
    )(dest, h2c, yc, gf)


def _rope_tables(t):
    inv = 1.0 / (ROPE_THETA ** (np.arange(0, HEAD_DIM, 2, dtype=np.float64) / HEAD_DIM))
    ang = np.arange(t, dtype=np.float64)[:, None] * inv[None, :]
    ang = np.concatenate([ang, ang, ang, ang], axis=-1)
    sign = np.where((np.arange(LANES) % HEAD_DIM) < HEAD_DIM // 2, -1.0, 1.0)
    return (jnp.asarray(np.cos(ang), dtype=F32), jnp.asarray(np.sin(ang) * sign[None, :], dtype=F32))


def kernel(x, meta, norm1_g, w_in, b_gate, lambda_q1, lambda_k1, lambda_q2, lambda_k2, subln_g, pool_w,
           pool_scale, w_attn_br, w_pool_br, w_out, norm2_g, w_router_group, b_router_group,
           w_router_expert, b_router_expert, w_e_gate, w_e_up, w_e_down, final_g):
    batch, seq, d = x.shape
    n = batch * seq
    assert w_in.shape[0] == 1 and d == D_MODEL and meta.shape == (N_META, D_MODEL)
    assert seq % 1024 == 0 and n < 2 ** 24
    x2 = x.reshape(n, d)
    cos, sin = _rope_tables(seq + N_META)
    q, k, v, u, gates, qm, km, vm, um = _inproj(x2, meta, norm1_g, cos, sin, w_in[0], b_gate, tm=1024)
    a = _attention(q, k, v, qm, km, vm, lambda_q1, lambda_k1, lambda_q2, lambda_k2, subln_g, batch, seq,
                   bq=256, bk=256)

    n_router = N_GROUPS + N_EXPERTS
    wr = jnp.pad(jnp.concatenate([w_router_group[0], w_router_expert[0]], axis=1),
                 ((0, 0), (0, LANES - n_router)))
    br = jnp.pad(jnp.concatenate([b_router_group[0], b_router_expert[0]]), (0, LANES - n_router))
    h2c, logits = _mix(x2, a, u, um, gates, w_attn_br[0].astype(BF16), pool_w[0],
                       pool_scale, w_pool_br[0], w_out[0].astype(BF16), norm2_g,
                       wr, br.reshape(1, LANES), seq, tm=1024)
    tme = 512
    n_tiles = n // tme + N_BUCKETS
    assert n_tiles <= LANES
    wts, dest, tile_table = _route(logits, tl=1024, tme=tme)
    dest = dest[0]
    per_token = jnp.concatenate([wts[0:2], jnp.arange(n, dtype=F32)[None, :]], axis=0).T
    per_row = jnp.zeros((n_tiles * tme, 3), F32).at[dest].set(per_token)
    src = per_row[:, 2].astype(jnp.int32)

    yc = _moe(tile_table, src, h2c, per_row, norm2_g, w_e_gate[0], w_e_up[0], w_e_down[0], tme, n_tiles)
    out = _final(dest, h2c, yc, final_g.reshape(1, d), tmf=512)
    return out.reshape(batch, seq, d)
```

```python
import functools
import math

import jax
import jax.numpy as jnp
import numpy as np
from jax import lax
from jax.experimental import pallas as pl
from jax.experimental.pallas import tpu as pltpu

D_MODEL = 1024
N_META = 16
N_HEADS = 8
HEAD_DIM = 64
V_DIM = 2 * HEAD_DIM
POOL_WINDOWS = (2, 4, 8, 16)
N_POOL_GROUPS = len(POOL_WINDOWS)
POOL_GROUP_DIM = 128
POOL_WIDTH = N_POOL_GROUPS * POOL_GROUP_DIM
ROPE_THETA = 10000.0
N_GROUPS = 4
EXPERTS_PER_GROUP = 4
N_EXPERTS = N_GROUPS * EXPERTS_PER_GROUP
D_EXPERT = 512
EPS = 1e-6
LAMBDA_INIT = 0.8 - 0.6 * math.exp(-0.3 * 0)
LOG2_E = math.log2(math.e)

_PAIR_A = (0, 0, 0, 1, 1, 3)
_PAIR_B = (1, 2, 3, 3, 2, 2)
N_PAIRS = len(_PAIR_A)
N_BUCKETS = N_GROUPS * N_PAIRS
N_BUCKET_ROWS = 32
GATHER_UNROLL = 16

LANES = 128
NEG_BIG = -1e30
VMEM_LIMIT = 48 * 1024 * 1024

F32 = jnp.float32
BF16 = jnp.bfloat16

_ATTN_HEADS = 2
_HEADS_PER_STEP = 4
_IN_STEPS = N_HEADS // _HEADS_PER_STEP
_QKV_BLK = _HEADS_PER_STEP * V_DIM
_U_BLK = POOL_WIDTH // _IN_STEPS
_GATE_BLK = 512
_GATES_PER_STEP = 2 * D_MODEL // _IN_STEPS // _GATE_BLK
_O_K = D_MODEL // _QKV_BLK
_O_V = 2 * D_MODEL // _QKV_BLK
_O_U = 3 * D_MODEL // _U_BLK
_O_G = (3 * D_MODEL + POOL_WIDTH) // _GATE_BLK


def _params(*sem):
    return pltpu.CompilerParams(dimension_semantics=sem, vmem_limit_bytes=VMEM_LIMIT)


TOKEN_ROWS = D_MODEL // LANES


def _store_token_major(ref, val, rows):
    for s in range(TOKEN_ROWS):
        ref[pl.ds(s, rows, stride=TOKEN_ROWS), :] = val[:, s * LANES:(s + 1) * LANES]


def _load_token_major(ref, rows):
    return jnp.concatenate([ref[pl.ds(s, rows, stride=TOKEN_ROWS), :] for s in range(TOKEN_ROWS)], axis=1)


def _rope(z, cos, sin_signed, first_half):
    outs = []
    for c in range(z.shape[1] // LANES):
        zc = z[:, c * LANES:(c + 1) * LANES]
        rot = jnp.where(first_half, pltpu.roll(zc, LANES - HEAD_DIM // 2, 1),
                        pltpu.roll(zc, HEAD_DIM // 2, 1))
        outs.append(zc * cos + rot * sin_signed)
    return jnp.concatenate(outs, axis=1)


def _inproj_kernel(x_ref, meta_ref, g_ref, cos_ref, sin_ref, cosm_ref, sinm_ref, wq_ref, wk_ref, wv_ref,
                   wu_ref, wg0_ref, wg1_ref, bg_ref, q_ref, k_ref, v_ref, u_ref, gate_ref, qm_ref, km_ref,
                   vm_ref, um_ref, wq_sc, wk_sc, wv_sc, wu_sc, wg0_sc, wg1_sc):
    def normed(x):
        ms = jnp.mean(x * x, axis=-1, keepdims=True)
        return (x * lax.rsqrt(ms + EPS) * g_ref[...]).astype(BF16)

    def project(hn, cos, sin, q_out, k_out, v_out, u_out, gate_out):
        lane = lax.broadcasted_iota(jnp.int32, cos.shape, 1)
        first_half = jnp.bitwise_and(lane, HEAD_DIM - 1) < (HEAD_DIM // 2)
        if gate_out is not None:
            for blk, wg_sc in enumerate((wg0_sc, wg1_sc)):
                cols = slice(blk * _GATE_BLK, (blk + 1) * _GATE_BLK)
                zg = jnp.dot(hn, wg_sc[...], preferred_element_type=F32) + bg_ref[:, cols]
                gate_out[:, cols] = (1.0 / (1.0 + jnp.exp(-zg))).astype(BF16)
        zq = jnp.dot(hn, wq_sc[...], preferred_element_type=F32)
        q_out[...] = (_rope(zq, cos, sin, first_half) * (LOG2_E / math.sqrt(HEAD_DIM))).astype(BF16)
        zk = jnp.dot(hn, wk_sc[...], preferred_element_type=F32)
        k_out[...] = _rope(zk, cos, sin, first_half).astype(BF16)
        u_out[...] = jnp.dot(hn, wu_sc[...], preferred_element_type=F32)
        v_out[...] = jnp.dot(hn, wv_sc[...], preferred_element_type=F32).astype(BF16)

    @pl.when(pl.program_id(1) == 0)
    def _():
        for src, dst in ((wq_ref, wq_sc), (wk_ref, wk_sc), (wv_ref, wv_sc), (wu_ref, wu_sc),
                         (wg0_ref, wg0_sc), (wg1_ref, wg1_sc)):
            dst[...] = src[...].astype(BF16)
        project(normed(meta_ref[...]), cosm_ref[...], sinm_ref[...], qm_ref, km_ref, vm_ref, um_ref, None)

    project(normed(x_ref[...]), cos_ref[...], sin_ref[...], q_ref, k_ref, v_ref, u_ref, gate_ref)


def _inproj(x2, meta, g1, cos, sin, w_in, b_gate, tm):
    n = x2.shape[0]
    n_meta = meta.shape[0]
    n_pos_blocks = (cos.shape[0] - n_meta) // tm
    row = lambda j, i: (i, 0)
    pos = lambda j, i: (i % n_pos_blocks, 0)
    whole = lambda j, i: (0, 0)
    once = pl.Buffered(1)
    gate_cols = _GATES_PER_STEP * _GATE_BLK
    assert _GATES_PER_STEP == 2
    return pl.pallas_call(
        _inproj_kernel,
        grid=(_IN_STEPS, n // tm),
        in_specs=[
            pl.BlockSpec((tm, D_MODEL), row),
            pl.BlockSpec((n_meta, D_MODEL), whole),
            pl.BlockSpec((1, D_MODEL), whole),
            pl.BlockSpec((tm, LANES), pos),
            pl.BlockSpec((tm, LANES), pos),
            pl.BlockSpec((n_meta, LANES), whole),
            pl.BlockSpec((n_meta, LANES), whole),
            pl.BlockSpec((D_MODEL, _QKV_BLK), lambda j, i: (0, j), pipeline_mode=once),
            pl.BlockSpec((D_MODEL, _QKV_BLK), lambda j, i: (0, _O_K + j), pipeline_mode=once),
            pl.BlockSpec((D_MODEL, _QKV_BLK), lambda j, i: (0, _O_V + j), pipeline_mode=once),
            pl.BlockSpec((D_MODEL, _U_BLK), lambda j, i: (0, _O_U + j), pipeline_mode=once),
            pl.BlockSpec((D_MODEL, _GATE_BLK), lambda j, i: (0, _O_G + _GATES_PER_STEP * j),
                         pipeline_mode=once),
            pl.BlockSpec((D_MODEL, _GATE_BLK), lambda j, i: (0, _O_G + _GATES_PER_STEP * j + 1),
                         pipeline_mode=once),
            pl.BlockSpec((1, gate_cols), lambda j, i: (0, j)),
        ],
        out_specs=[
            pl.BlockSpec((tm, _QKV_BLK), lambda j, i: (i, j)),
            pl.BlockSpec((tm, _QKV_BLK), lambda j, i: (i, j)),
            pl.BlockSpec((tm, _QKV_BLK), lambda j, i: (i, j)),
            pl.BlockSpec((tm, _U_BLK), lambda j, i: (i, j)),
            pl.BlockSpec((tm, gate_cols), lambda j, i: (i, j)),
            pl.BlockSpec((n_meta, _QKV_BLK), lambda j, i: (0, j)),
            pl.BlockSpec((n_meta, _QKV_BLK), lambda j, i: (0, j)),
            pl.BlockSpec((n_meta, _QKV_BLK), lambda j, i: (0, j)),
            pl.BlockSpec((n_meta, _U_BLK), lambda j, i: (0, j)),
        ],
        out_shape=[
            jax.ShapeDtypeStruct((n, D_MODEL), BF16),
            jax.ShapeDtypeStruct((n, D_MODEL), BF16),
            jax.ShapeDtypeStruct((n, D_MODEL), BF16),
            jax.ShapeDtypeStruct((n, POOL_WIDTH), F32),
            jax.ShapeDtypeStruct((n, 2 * D_MODEL), BF16),
            jax.ShapeDtypeStruct((n_meta, D_MODEL), BF16),
            jax.ShapeDtypeStruct((n_meta, D_MODEL), BF16),
            jax.ShapeDtypeStruct((n_meta, D_MODEL), BF16),
            jax.ShapeDtypeStruct((n_meta, POOL_WIDTH), F32),
        ],
        scratch_shapes=[
            pltpu.VMEM((D_MODEL, _QKV_BLK), BF16),
            pltpu.VMEM((D_MODEL, _QKV_BLK), BF16),
            pltpu.VMEM((D_MODEL, _QKV_BLK), BF16),
            pltpu.VMEM((D_MODEL, _U_BLK), BF16),
            pltpu.VMEM((D_MODEL, _GATE_BLK), BF16),
            pltpu.VMEM((D_MODEL, _GATE_BLK), BF16),
        ],
        compiler_params=_params("arbitrary", "arbitrary"),
        name="inproj",
    )(x2, meta, g1, cos[n_meta:], sin[n_meta:], cos[:n_meta], sin[:n_meta], w_in, w_in, w_in, w_in, w_in,
      w_in, b_gate)


def _attn_kernel(lq1_ref, lk1_ref, lq2_ref, lk2_ref, subg_ref, q_ref, k_ref, v_ref, qm_ref, km_ref,
                 vm_ref, o_ref, *state_sc, bq, bk):
    seq = k_ref.shape[0]
    nq = seq // bq
    heads = k_ref.shape[1] // V_DIM
    chains = [(h, i) for i in range(nq, -1, -1) for h in range(heads)]
    states = {c: tuple(state_sc[3 * g:3 * g + 3]) for g, c in enumerate(chains)}

    def rows_at(ref, meta_ref, h, p0, n):
        cols = slice(h * V_DIM, (h + 1) * V_DIM)
        if p0 == 0:
            return jnp.concatenate([meta_ref[:, cols], ref[0:n - N_META, cols]], axis=0)
        real = ref[p0 - N_META:min(p0 - N_META + n, seq), cols]
        if real.shape[0] < n:
            real = jnp.concatenate([real, jnp.zeros((n - real.shape[0], real.shape[1]), real.dtype)], axis=0)
        return real

    def block_rows(i):
        return (i * bq, bq) if i < nq else (nq * bq, N_META)

    def n_full_tiles(i):
        return (block_rows(i)[0] + 1) // bk

    def partial_tiles(i):
        p0, n = block_rows(i)
        if i == nq:
            return [(n_full_tiles(i) * bk, LANES, p0 - n_full_tiles(i) * bk)]
        return [(t * bk, bk, p0 - t * bk) for t in range(n_full_tiles(i), (p0 + n - 1) // bk + 1)]

    lam = (jnp.exp(jnp.sum(lq1_ref[...] * lk1_ref[...], axis=-1, keepdims=True))
           - jnp.exp(jnp.sum(lq2_ref[...] * lk2_ref[...], axis=-1, keepdims=True)) + LAMBDA_INIT)

    bias_cache = {}

    def causal_bias(n_rows, n_cols, offset):
        if (n_rows, n_cols, offset) not in bias_cache:
            r = lax.broadcasted_iota(jnp.int32, (n_rows, n_cols), 0)
            c = lax.broadcasted_iota(jnp.int32, (n_rows, n_cols), 1)
            bias = jnp.where(c <= r + offset, 0.0, NEG_BIG)
            bias_cache[(n_rows, n_cols, offset)] = jnp.concatenate([bias, bias], axis=0)
        return bias_cache[(n_rows, n_cols, offset)]

    def load_q(h, i, qq_sc):
        p0, n = block_rows(i)
        q = rows_at(q_ref, qm_ref, h, p0, n)
        lane = lax.broadcasted_iota(jnp.int32, q.shape, 1)
        zero = jnp.zeros_like(q)
        qq_sc[0:n, :] = jnp.where(lane < HEAD_DIM, q, zero)
        qq_sc[n:2 * n, :] = jnp.where(lane >= HEAD_DIM, q, zero)

    def step(state, kb, vb, bias, first):
        qq_sc, m_sc, acc_sc = state
        s = lax.dot_general(qq_sc[...], kb, (((1,), (1,)), ((), ())), preferred_element_type=F32)
        if bias is not None:
            s = s + bias
        n_tiles = s.shape[1] // LANES
        rm = s[:, 0:LANES]
        for t in range(1, n_tiles):
            rm = jnp.maximum(rm, s[:, t * LANES:(t + 1) * LANES])
        rmax = jnp.max(rm, axis=-1, keepdims=True)
        if first:
            m_new = jnp.broadcast_to(rmax, (s.shape[0], LANES))
        else:
            m_prev = m_sc[...]
            m_new = jnp.maximum(m_prev, rmax)
        p = jnp.exp2(s - jnp.concatenate([m_new] * n_tiles, axis=1))
        pv = jnp.dot(p.astype(BF16), vb, preferred_element_type=F32)
        if first:
            acc_sc[...] = pv
        else:
            alpha = jnp.exp2(m_prev - m_new)
            acc_sc[...] = jnp.concatenate([alpha, alpha], axis=1) * acc_sc[...] + pv
        m_sc[...] = m_new

    def chain_steps(i):
        n = block_rows(i)[1]
        steps = [(p0, width, causal_bias(n, width, offset)) for p0, width, offset in partial_tiles(i)]
        return steps + [(t * bk, bk, None) for t in range(n_full_tiles(i))]

    def key_step(state, h, p0, width, bias, first):
        kb = rows_at(k_ref, km_ref, h, p0, width)
        vb = jnp.concatenate([rows_at(v_ref, vm_ref, h, p0, width), jnp.ones((width, V_DIM), BF16)],
                             axis=1)
        step(state, kb, vb, bias, first)

    def finish(h, i, acc_sc):
        p0, n = block_rows(i)
        cols = slice(h * V_DIM, (h + 1) * V_DIM)
        o1 = acc_sc[0:n, 0:V_DIM] / acc_sc[0:n, V_DIM:2 * V_DIM]
        o2 = acc_sc[n:2 * n, 0:V_DIM] / acc_sc[n:2 * n, V_DIM:2 * V_DIM]
        o = o1 - lam * o2
        ms = jnp.mean(o * o, axis=-1, keepdims=True)
        o = (o * lax.rsqrt(ms + EPS) * subg_ref[...] * (1.0 - LAMBDA_INIT)).astype(BF16)
        if p0 == 0:
            o_ref[0:n - N_META, cols] = o[N_META:n, :]
        else:
            o_ref[p0 - N_META:p0 - N_META + n, cols] = o

    plans = {i: chain_steps(i) for i in range(nq + 1)}
    for h, i in chains:
        load_q(h, i, states[h, i][0])
    for j in range(max(len(plan) for plan in plans.values())):
        for h, i in chains:
            if j < len(plans[i]):
                key_step(states[h, i], h, *plans[i][j], first=(j == 0))
    for h, i in chains:
        finish(h, i, states[h, i][2])


def _attention(q, k, v, qm, km, vm, lq1, lk1, lq2, lk2, subg, batch, seq, bq, bk):
    nq = seq // bq
    width = _ATTN_HEADS * V_DIM
    small = lambda b, h: (0, 0)
    head = lambda b, h: (b, h)
    meta = lambda b, h: (0, h)

    def state_scratch(rows):
        return [pltpu.VMEM((2 * rows, V_DIM), BF16), pltpu.VMEM((2 * rows, LANES), F32),
                pltpu.VMEM((2 * rows, 2 * V_DIM), F32)]

    return pl.pallas_call(
        functools.partial(_attn_kernel, bq=bq, bk=bk),
        grid=(batch, N_HEADS // _ATTN_HEADS),
        in_specs=[
            pl.BlockSpec((1, HEAD_DIM), small),
            pl.BlockSpec((1, HEAD_DIM), small),
            pl.BlockSpec((1, HEAD_DIM), small),
            pl.BlockSpec((1, HEAD_DIM), small),
            pl.BlockSpec((1, V_DIM), small),
            pl.BlockSpec((seq, width), head),
            pl.BlockSpec((seq, width), head),
            pl.BlockSpec((seq, width), head),
            pl.BlockSpec((N_META, width), meta),
            pl.BlockSpec((N_META, width), meta),
            pl.BlockSpec((N_META, width), meta),
        ],
        out_specs=pl.BlockSpec((seq, width), head),
        out_shape=jax.ShapeDtypeStruct(q.shape, BF16),
        scratch_shapes=state_scratch(N_META) * _ATTN_HEADS + state_scratch(bq) * (nq * _ATTN_HEADS),
        compiler_params=_params("arbitrary", "arbitrary"),
        name="diff_attn",
    )(lq1, lk1, lq2, lk2, subg, q, k, v, qm, km, vm)


def _mix_kernel(x_ref, a_ref, u_ref, uprev_ref, umeta_ref, gate_ref, wa_ref, pw_ref, ps_ref, wp_ref,
                wo_ref, g2_ref, wr_ref, br_ref, h2c_ref, logit_ref, ext_sc, sum_a_sc, sum_b_sc, wfold_sc,
                *, tm, tiles_per_seq):
    i = pl.program_id(0)
    first = (i % tiles_per_seq) == 0

    @pl.when(i == 0)
    def _():
        for g in range(N_POOL_GROUPS):
            rows = slice(g * POOL_GROUP_DIM, (g + 1) * POOL_GROUP_DIM)
            left = pw_ref[g] * ps_ref[:, rows]
            right = wp_ref[rows, :]
            left_hi, right_hi = left.astype(BF16), right.astype(BF16)
            left_lo = (left - left_hi.astype(F32)).astype(BF16)
            right_lo = (right - right_hi.astype(F32)).astype(BF16)
            wfold_sc[rows, :] = (jnp.dot(left_hi, right_hi, preferred_element_type=F32)
                                 + jnp.dot(left_lo, right_hi, preferred_element_type=F32)
                                 + jnp.dot(left_hi, right_lo, preferred_element_type=F32)).astype(BF16)

    y_attn = jnp.dot(a_ref[...], wa_ref[...], preferred_element_type=F32)

    top = 2 * N_META
    for ref in (ext_sc, sum_a_sc, sum_b_sc):
        ref[0:N_META, :] = jnp.zeros((N_META, ref.shape[1]), F32)

    @pl.when(first)
    def _():
        ext_sc[N_META:top, :] = umeta_ref[...]

    @pl.when(jnp.logical_not(first))
    def _():
        ext_sc[N_META:top, :] = uprev_ref[...]

    ext_sc[top:top + tm, :] = u_ref[...]

    pooled = []
    for g, w in enumerate(POOL_WINDOWS):
        cs = slice(g * POOL_GROUP_DIM, (g + 1) * POOL_GROUP_DIM)
        src, lanes, shift = ext_sc, cs, 1
        for dst in (sum_a_sc, sum_b_sc, sum_a_sc, sum_b_sc):
            if shift >= w:
                break
            dst[N_META:top + tm, :] = (src[N_META:top + tm, lanes]
                                       + src[N_META - shift:top + tm - shift, lanes])
            src, lanes, shift = dst, slice(None), 2 * shift
        pooled.append((src[top:top + tm, lanes] * (1.0 / w) - ext_sc[top:top + tm, cs]).astype(BF16))
    y_pool = jnp.dot(jnp.concatenate(pooled, axis=1), wfold_sc[...], preferred_element_type=F32)
    mixed = (gate_ref[:, 0:D_MODEL].astype(F32) * y_attn
             + gate_ref[:, D_MODEL:2 * D_MODEL].astype(F32) * y_pool)
    h2 = x_ref[...] + jnp.dot(mixed.astype(BF16), wo_ref[...], preferred_element_type=F32)
    _store_token_major(h2c_ref, h2, tm)
    ms = jnp.mean(h2 * h2, axis=-1, keepdims=True)
    hn2 = h2 * lax.rsqrt(ms + EPS) * g2_ref[...]
    hn2_hi = hn2.astype(BF16)
    hn2_lo = (hn2 - hn2_hi.astype(F32)).astype(BF16)
    wr = wr_ref[...]
    wr_hi = wr.astype(BF16)
    wr_lo = (wr - wr_hi.astype(F32)).astype(BF16)
    wr_both = jnp.concatenate([wr_hi, wr_lo], axis=1)
    for half in range(2):
        rows = slice(half * tm // 2, (half + 1) * tm // 2)
        both = jnp.dot(hn2_hi[rows, :], wr_both, preferred_element_type=F32)
        logit_ref[rows, :] = (both[:, 0:LANES] + both[:, LANES:2 * LANES]
                              + jnp.dot(hn2_lo[rows, :], wr_hi, preferred_element_type=F32) + br_ref[...])


def _mix(x2, a, u, umeta, gates, w_attn, pool_w, pool_scale, w_pool, w_out, g2, wr, br, seq, tm):
    n = x2.shape[0]
    tiles_per_seq = seq // tm
    halo_blocks = tm // N_META
    const2 = lambda i: (0, 0)
    return pl.pallas_call(
        functools.partial(_mix_kernel, tm=tm, tiles_per_seq=tiles_per_seq),
        grid=(n // tm,),
        in_specs=[
            pl.BlockSpec((tm, D_MODEL), lambda i: (i, 0)),
            pl.BlockSpec((tm, D_MODEL), lambda i: (i, 0)),
            pl.BlockSpec((tm, POOL_WIDTH), lambda i: (i, 0)),
            pl.BlockSpec((N_META, POOL_WIDTH), lambda i: (jnp.maximum(i * halo_blocks - 1, 0), 0)),
            pl.BlockSpec((N_META, POOL_WIDTH), const2),
            pl.BlockSpec((tm, 2 * D_MODEL), lambda i: (i, 0)),
            pl.BlockSpec((D_MODEL, D_MODEL), const2, pipeline_mode=pl.Buffered(1)),
            pl.BlockSpec((N_POOL_GROUPS, POOL_GROUP_DIM, POOL_GROUP_DIM), lambda i: (0, 0, 0),
                         pipeline_mode=pl.Buffered(1)),
            pl.BlockSpec((1, POOL_WIDTH), const2),
            pl.BlockSpec((POOL_WIDTH, D_MODEL), const2, pipeline_mode=pl.Buffered(1)),
            pl.BlockSpec((D_MODEL, D_MODEL), const2, pipeline_mode=pl.Buffered(1)),
            pl.BlockSpec((1, D_MODEL), const2),
            pl.BlockSpec((D_MODEL, LANES), const2, pipeline_mode=pl.Buffered(1)),
            pl.BlockSpec((1, LANES), const2),
        ],
        out_specs=[
            pl.BlockSpec((tm * TOKEN_ROWS, LANES), lambda i: (i, 0)),
            pl.BlockSpec((tm, LANES), lambda i: (i, 0)),
        ],
        out_shape=[
            jax.ShapeDtypeStruct((n * TOKEN_ROWS, LANES), F32),
            jax.ShapeDtypeStruct((n, LANES), F32),
        ],
        scratch_shapes=[pltpu.VMEM((tm + 2 * N_META, POOL_WIDTH), F32),
                        pltpu.VMEM((tm + 2 * N_META, POOL_GROUP_DIM), F32),
                        pltpu.VMEM((tm + 2 * N_META, POOL_GROUP_DIM), F32),
                        pltpu.VMEM((POOL_WIDTH, D_MODEL), BF16)],
        compiler_params=_params("arbitrary"),
        name="mix",
    )(x2, a, u, u, umeta, gates, w_attn, pool_w, pool_scale, w_pool, w_out, g2, wr, br)


_TBL_EA, _TBL_EB, _TBL_NEW_A, _TBL_NEW_B, _TBL_NVALID, _TBL_NUSED = range(6)


def _route_kernel(logit_ref, tri_ref, wts_ref, dest_ref, tbl_ref, carry_sc, ids_sc, *, tme):
    i = pl.program_id(0)
    nt = ids_sc.shape[0]
    pl.when(i < nt)(functools.partial(_route_classify, i, logit_ref, tri_ref, wts_ref, carry_sc, ids_sc))
    pl.when(i == nt)(functools.partial(_route_place, dest_ref, tbl_ref, carry_sc, ids_sc, tme))


def _route_classify(i, logit_ref, tri_ref, wts_ref, carry_sc, ids_sc):
    @pl.when(i == 0)
    def _():
        carry_sc[...] = jnp.zeros_like(carry_sc)

    lg = logit_ref[...].T
    g = [lg[r:r + 1, :] for r in range(N_GROUPS)]
    gmax = functools.reduce(jnp.maximum, g)
    gsel = jnp.full_like(g[0], N_GROUPS - 1).astype(jnp.int32)
    for r in range(N_GROUPS - 2, -1, -1):
        gsel = jnp.where(g[r] == gmax, r, gsel)
    p_group = 1.0 / functools.reduce(lambda a, b: a + b, [jnp.exp(x - gmax) for x in g])
    e = []
    for jj in range(EXPERTS_PER_GROUP):
        v = lg[N_GROUPS + jj:N_GROUPS + jj + 1, :]
        for r in range(1, N_GROUPS):
            row = N_GROUPS + r * EXPERTS_PER_GROUP + jj
            v = jnp.where(gsel == r, lg[row:row + 1, :], v)
        e.append(v)
    v1 = functools.reduce(jnp.maximum, e)
    i1 = jnp.full_like(gsel, EXPERTS_PER_GROUP - 1)
    for jj in range(EXPERTS_PER_GROUP - 2, -1, -1):
        i1 = jnp.where(e[jj] == v1, jj, i1)
    rest = [jnp.where(i1 == jj, -jnp.inf, e[jj]) for jj in range(EXPERTS_PER_GROUP)]
    v2 = functools.reduce(jnp.maximum, rest)
    i2 = jnp.full_like(gsel, EXPERTS_PER_GROUP - 1)
    for jj in range(EXPERTS_PER_GROUP - 2, -1, -1):
        i2 = jnp.where(jnp.logical_and(rest[jj] == v2, i1 != jj), jj, i2)
    t = jnp.exp(v2 - v1)
    w1 = p_group / (1.0 + t)
    w2 = p_group * t / (1.0 + t)
    lo = jnp.minimum(i1, i2)
    hi = jnp.maximum(i1, i2)
    pair = jnp.where(lo == 0, hi - 1, jnp.where(lo == 1, jnp.where(hi == 3, 3, 4), 5))
    bucket = gsel * N_PAIRS + pair
    w_of_lo = jnp.where(i1 < i2, w1, w2)
    w_of_hi = jnp.where(i1 < i2, w2, w1)
    swapped = pair == 5
    w_a = jnp.where(swapped, w_of_hi, w_of_lo)
    w_b = jnp.where(swapped, w_of_lo, w_of_hi)

    tl = lg.shape[1]
    bid = lax.broadcasted_iota(jnp.int32, (N_BUCKET_ROWS, tl), 0)
    hot = bid == bucket
    hot_b = jnp.where(hot, 1.0, 0.0).astype(BF16)
    prefix = jnp.dot(hot_b, tri_ref[...], preferred_element_type=F32)
    carry = carry_sc[...]
    before = prefix + jnp.concatenate([carry] * (tl // LANES), axis=1)
    rank = jnp.sum(jnp.where(hot, before, 0.0), axis=0, keepdims=True)
    carry_sc[...] = carry + jnp.dot(hot_b, jnp.ones((tl, LANES), BF16), preferred_element_type=F32)
    pad = jnp.zeros((6, tl), F32)
    ids_sc[i] = jnp.concatenate([bucket.astype(F32), rank, pad], axis=0)
    wts_ref[...] = jnp.concatenate([w_a, w_b, pad], axis=0)


def _route_place(dest_ref, tbl_ref, carry_sc, ids_sc, tme):
    cnt = carry_sc[...]
    tiles = jnp.floor((cnt + (tme - 1)) * (1.0 / tme))
    r = lax.broadcasted_iota(jnp.int32, (N_BUCKET_ROWS, N_BUCKET_ROWS), 0)
    c = lax.broadcasted_iota(jnp.int32, (N_BUCKET_ROWS, N_BUCKET_ROWS), 1)
    tile_end = jnp.dot(jnp.where(c <= r, 1.0, 0.0).astype(BF16), tiles.astype(BF16),
                       preferred_element_type=F32)
    tile_start = tile_end - tiles

    nt, _, tl = ids_sc.shape
    bid = lax.broadcasted_iota(jnp.int32, (N_BUCKET_ROWS, tl), 0).astype(F32)
    row_start = jnp.concatenate([tile_start * tme] * (tl // LANES), axis=1)
    for i in range(nt):
        ids = ids_sc[i]
        bucket, rank = ids[0:1, :], ids[1:2, :]
        dest = rank + jnp.sum(jnp.where(bid == bucket, row_start, 0.0), axis=0, keepdims=True)
        dest_ref[:, i * tl:(i + 1) * tl] = jnp.concatenate([dest, jnp.zeros((7, tl), F32)],
                                                           axis=0).astype(jnp.int32)

    lane = lax.broadcasted_iota(jnp.int32, (1, LANES), 1).astype(F32)
    n_used = tile_end[N_BUCKET_ROWS - 1:N_BUCKET_ROWS, :]
    last = jnp.minimum(lane, n_used - 1.0)
    tb = jnp.sum(jnp.where(tile_end <= last, 1.0, 0.0), axis=0, keepdims=True)
    tb = jnp.minimum(tb, N_BUCKETS - 1.0)
    group = functools.reduce(lambda a, b: a + b,
                             [jnp.where(tb >= g * N_PAIRS, 1.0, 0.0) for g in range(1, N_GROUPS)])
    pair = tb - N_PAIRS * group

    def lookup(table):
        out = jnp.full_like(pair, float(table[-1]))
        for k in range(len(table) - 2, -1, -1):
            out = jnp.where(pair == k, float(table[k]), out)
        return EXPERTS_PER_GROUP * group + out

    e_a, e_b = lookup(_PAIR_A), lookup(_PAIR_B)
    changed = lambda e: jnp.where(lane == 0.0, 1.0, jnp.where(e != pltpu.roll(e, 1, 1), 1.0, 0.0))
    bid128 = lax.broadcasted_iota(jnp.int32, (N_BUCKET_ROWS, LANES), 0).astype(F32)
    at_tile = lambda col: jnp.sum(jnp.where(bid128 == tb, col, 0.0), axis=0, keepdims=True)
    n_valid = jnp.clip(at_tile(cnt) - tme * (lane - at_tile(tile_start)), 0.0, float(tme))
    rows = [e_a, e_b, changed(e_a), changed(e_b), n_valid, n_used, jnp.zeros_like(lane),
            jnp.zeros_like(lane)]
    tbl_ref[...] = jnp.concatenate(rows, axis=0).astype(jnp.int32)


def _route(logits, tl, tme):
    n = logits.shape[0]
    nt = n // tl
    tri = jnp.asarray(np.triu(np.ones((tl, tl), np.float32), k=1), dtype=BF16)
    tile = lambda i: jnp.minimum(i, nt - 1)
    return pl.pallas_call(
        functools.partial(_route_kernel, tme=tme),
        grid=(nt + 1,),
        in_specs=[pl.BlockSpec((tl, LANES), lambda i: (tile(i), 0)),
                  pl.BlockSpec((tl, tl), lambda i: (0, 0))],
        out_specs=[pl.BlockSpec((8, tl), lambda i: (0, tile(i))),
                   pl.BlockSpec((8, n), lambda i: (0, 0)),
                   pl.BlockSpec((8, LANES), lambda i: (0, 0))],
        out_shape=[jax.ShapeDtypeStruct((8, n), F32),
                   jax.ShapeDtypeStruct((8, n), jnp.int32),
                   jax.ShapeDtypeStruct((8, LANES), jnp.int32)],
        scratch_shapes=[pltpu.VMEM((N_BUCKET_ROWS, LANES), F32), pltpu.VMEM((nt, 8, tl), F32)],
        compiler_params=_params("arbitrary"),
        name="route",
    )(logits, tri)


def _token_copy(src_hbm, src_token, dst_buf, dst_row, sem):
    return pltpu.make_async_copy(
        src_hbm.at[pl.ds(pl.multiple_of(src_token * TOKEN_ROWS, TOKEN_ROWS), TOKEN_ROWS), :],
        dst_buf.at[pl.ds(pl.multiple_of(dst_row * TOKEN_ROWS, TOKEN_ROWS), TOKEN_ROWS), :],
        sem)


def _start_gather(idx_ref, first, n_groups, src_hbm, dst_buf, sem):
    def body(rr, carry):
        for uu in range(GATHER_UNROLL):
            r = rr * GATHER_UNROLL + uu
            _token_copy(src_hbm, idx_ref[first + r], dst_buf, r, sem).start(priority=uu % 2)
        return carry

    lax.fori_loop(0, n_groups, body, 0)


def _start_padded_gather(idx_ref, first, n_valid, rows, src_hbm, dst_buf, sem):
    n_groups = (n_valid + GATHER_UNROLL - 1) // GATHER_UNROLL
    _start_gather(idx_ref, first, n_groups, src_hbm, dst_buf, sem)
    pad_groups = rows // GATHER_UNROLL - n_groups
    off = n_groups * GATHER_UNROLL
    size = rows // 2
    while size >= GATHER_UNROLL:
        take = jnp.bitwise_and(pad_groups, size // GATHER_UNROLL) != 0

        @pl.when(take)
        def _(off=off, size=size):
            pltpu.make_async_copy(
                src_hbm.at[pl.ds(0, size * TOKEN_ROWS), :],
                dst_buf.at[pl.ds(pl.multiple_of(off * TOKEN_ROWS, GATHER_UNROLL * TOKEN_ROWS),
                                 size * TOKEN_ROWS), :],
                sem).start()

        off = off + jnp.where(take, size, 0)
        size //= 2


def _wait_gather(rows, src_hbm, dst_buf, sem):
    pltpu.make_async_copy(src_hbm.at[pl.ds(0, rows * TOKEN_ROWS), :], dst_buf, sem).wait()


def _moe_kernel(tbl_ref, src_ref, h2c_hbm, wts_ref, g2_ref, wg_a_ref, wu_a_ref, wd_a_ref, wg_b_ref,
                wu_b_ref, wd_b_ref, yc_ref, xbuf, wg_a_sc, wu_a_sc, wd_a_sc, wg_b_sc, wu_b_sc, wd_b_sc,
                sem, *, tme):
    i = pl.program_id(0)
    n_used = tbl_ref[_TBL_NUSED, 0]
    slot = jnp.bitwise_and(i, 1)

    @pl.when(i == 0)
    def _():
        _start_padded_gather(src_ref, 0, tbl_ref[_TBL_NVALID, 0], tme, h2c_hbm, xbuf.at[0], sem.at[0])

    @pl.when(i + 1 < n_used)
    def _():
        _start_padded_gather(src_ref, (i + 1) * tme, tbl_ref[_TBL_NVALID, i + 1], tme, h2c_hbm,
                             xbuf.at[1 - slot], sem.at[1 - slot])

    @pl.when(tbl_ref[_TBL_NEW_A, i] == 1)
    def _():
        for src, dst in ((wg_a_ref, wg_a_sc), (wu_a_ref, wu_a_sc), (wd_a_ref, wd_a_sc)):
            dst[...] = src[0].astype(BF16)

    @pl.when(tbl_ref[_TBL_NEW_B, i] == 1)
    def _():
        for src, dst in ((wg_b_ref, wg_b_sc), (wu_b_ref, wu_b_sc), (wd_b_ref, wd_b_sc)):
            dst[...] = src[0].astype(BF16)

    def tile(rows):
        _wait_gather(tme, h2c_hbm, xbuf.at[slot], sem.at[slot])
        x = _load_token_major(xbuf.at[slot], rows)
        ms = jnp.mean(x * x, axis=-1, keepdims=True)
        hn = (x * lax.rsqrt(ms + EPS) * g2_ref[...]).astype(BF16)
        wts = wts_ref[0:rows, :]

        def expert_act(wg_sc, wu_sc, w):
            gate = jnp.dot(hn, wg_sc[...], preferred_element_type=F32)
            up = jnp.dot(hn, wu_sc[...], preferred_element_type=F32)
            return (gate / (1.0 + jnp.exp(-gate)) * up * w).astype(BF16)

        y = (jnp.dot(expert_act(wg_a_sc, wu_a_sc, wts[:, 0:1]), wd_a_sc[...], preferred_element_type=F32)
             + jnp.dot(expert_act(wg_b_sc, wu_b_sc, wts[:, 1:2]), wd_b_sc[...], preferred_element_type=F32))
        _store_token_major(yc_ref, y, rows)
        if rows < tme:
            yc_ref[rows * TOKEN_ROWS:tme * TOKEN_ROWS, :] = jnp.zeros(((tme - rows) * TOKEN_ROWS, LANES), F32)

    n_valid = tbl_ref[_TBL_NVALID, i]
    pl.when(jnp.logical_and(i < n_used, n_valid > tme // 2))(functools.partial(tile, tme))
    pl.when(jnp.logical_and(i < n_used, n_valid <= tme // 2))(functools.partial(tile, tme // 2))

    @pl.when(i >= n_used)
    def _():
        yc_ref[...] = jnp.zeros_like(yc_ref)


def _moe(tbl, src, h2c, wts_sorted, g2, wg, wu, wd, tme, n_tiles):
    used = lambda i, tbl, src: (jnp.minimum(i, tbl[_TBL_NUSED, 0] - 1), 0)
    w_a = lambda i, tbl, src: (tbl[_TBL_EA, i], 0, 0)
    w_b = lambda i, tbl, src: (tbl[_TBL_EB, i], 0, 0)
    up_shape, down_shape = (D_MODEL, D_EXPERT), (D_EXPERT, D_MODEL)
    grid_spec = pltpu.PrefetchScalarGridSpec(
        num_scalar_prefetch=2,
        grid=(n_tiles,),
        in_specs=[
            pl.BlockSpec(memory_space=pl.ANY),
            pl.BlockSpec((tme, wts_sorted.shape[1]), used),
            pl.BlockSpec((1, D_MODEL), lambda i, tbl, src: (0, 0)),
            pl.BlockSpec((1,) + up_shape, w_a),
            pl.BlockSpec((1,) + up_shape, w_a),
            pl.BlockSpec((1,) + down_shape, w_a),
            pl.BlockSpec((1,) + up_shape, w_b),
            pl.BlockSpec((1,) + up_shape, w_b),
            pl.BlockSpec((1,) + down_shape, w_b),
        ],
        out_specs=pl.BlockSpec((tme * TOKEN_ROWS, LANES), lambda i, tbl, src: (i, 0)),
        scratch_shapes=[pltpu.VMEM((2, tme * TOKEN_ROWS, LANES), F32)]
        + [pltpu.VMEM(s, BF16) for s in (up_shape, up_shape, down_shape) * 2]
        + [pltpu.SemaphoreType.DMA((2,))],
    )
    return pl.pallas_call(
        functools.partial(_moe_kernel, tme=tme),
        grid_spec=grid_spec,
        out_shape=jax.ShapeDtypeStruct((n_tiles * tme * TOKEN_ROWS, LANES), F32),
        compiler_params=_params("arbitrary"),
        name="moe",
    )(tbl, src, h2c, wts_sorted, g2, wg, wu, wd, wg, wu, wd)


def _final_kernel(dest_ref, h2c_ref, yc_hbm, gf_ref, o_ref, ybuf, sem, *, tmf):
    i = pl.program_id(0)
    slot = jnp.bitwise_and(i, 1)

    @pl.when(i == 0)
    def _():
        _start_gather(dest_ref, 0, tmf // GATHER_UNROLL, yc_hbm, ybuf.at[0], sem.at[0])

    @pl.when(i + 1 < pl.num_programs(0))
    def _():
        _start_gather(dest_ref, (i + 1) * tmf, tmf // GATHER_UNROLL, yc_hbm, ybuf.at[1 - slot],
                      sem.at[1 - slot])

    _wait_gather(tmf, yc_hbm, ybuf.at[slot], sem.at[slot])
    h = _load_token_major(h2c_ref, tmf) + _load_token_major(ybuf.at[slot], tmf)
    ms = jnp.mean(h * h, axis=-1, keepdims=True)
    o_ref[...] = h * lax.rsqrt(ms + EPS) * gf_ref[...]


def _final(dest, h2c, yc, gf, tmf):
    n = h2c.shape[0] // TOKEN_ROWS
    grid_spec = pltpu.PrefetchScalarGridSpec(
        num_scalar_prefetch=1,
        grid=(n // tmf,),
        in_specs=[
            pl.BlockSpec((tmf * TOKEN_ROWS, LANES), lambda i, dest: (i, 0)),
            pl.BlockSpec(memory_space=pl.ANY),
            pl.BlockSpec((1, D_MODEL), lambda i, dest: (0, 0)),
        ],
        out_specs=pl.BlockSpec((tmf, D_MODEL), lambda i, dest: (i, 0)),
        scratch_shapes=[pltpu.VMEM((2, tmf * TOKEN_ROWS, LANES), F32), pltpu.SemaphoreType.DMA((2,))],
    )
    return pl.pallas_call(
        functools.partial(_final_kernel, tmf=tmf),
        grid_spec=grid_spec,
        out_shape=jax.ShapeDtypeStruct((n, D_MODEL), F32),
        compiler_params=_params("arbitrary"),
        name="final",
    )(dest, h2c, yc, gf)


def _rope_tables(t):
    inv = 1.0 / (ROPE_THETA ** (np.arange(0, HEAD_DIM, 2, dtype=np.float64) / HEAD_DIM))
    ang = np.arange(t, dtype=np.float64)[:, None] * inv[None, :]
    ang = np.concatenate([ang, ang, ang, ang], axis=-1)
    sign = np.where((np.arange(LANES) % HEAD_DIM) < HEAD_DIM // 2, -1.0, 1.0)
    return (jnp.asarray(np.cos(ang), dtype=F32), jnp.asarray(np.sin(ang) * sign[None, :], dtype=F32))


def kernel(x, meta, norm1_g, w_in, b_gate, lambda_q1, lambda_k1, lambda_q2, lambda_k2, subln_g, pool_w,
           pool_scale, w_attn_br, w_pool_br, w_out, norm2_g, w_router_group, b_router_group,
           w_router_expert, b_router_expert, w_e_gate, w_e_up, w_e_down, final_g):
    batch, seq, d = x.shape
    n = batch * seq
    assert w_in.shape[0] == 1 and d == D_MODEL and meta.shape == (N_META, D_MODEL)
    assert seq % 1024 == 0 and n < 2 ** 24
    x2 = x.reshape(n, d)
    cos, sin = _rope_tables(seq + N_META)
    q, k, v, u, gates, qm, km, vm, um = _inproj(x2, meta, norm1_g, cos, sin, w_in[0], b_gate, tm=1024)
    a = _attention(q, k, v, qm, km, vm, lambda_q1, lambda_k1, lambda_q2, lambda_k2, subln_g, batch, seq,
                   bq=256, bk=256)

    n_router = N_GROUPS + N_EXPERTS
    wr = jnp.pad(jnp.concatenate([w_router_group[0], w_router_expert[0]], axis=1),
                 ((0, 0), (0, LANES - n_router)))
    br = jnp.pad(jnp.concatenate([b_router_group[0], b_router_expert[0]]), (0, LANES - n_router))
    h2c, logits = _mix(x2, a, u, um, gates, w_attn_br[0].astype(BF16), pool_w[0],
                       pool_scale, w_pool_br[0], w_out[0].astype(BF16), norm2_g,
                       wr, br.reshape(1, LANES), seq, tm=1024)
    tme = 512
    n_tiles = n // tme + N_BUCKETS
    assert n_tiles <= LANES
    wts, dest, tile_table = _route(logits, tl=1024, tme=tme)
    dest = dest[0]
    per_token = jnp.concatenate([wts[0:2], jnp.arange(n, dtype=F32)[None, :]], axis=0).T
    per_row = jnp.zeros((n_tiles * tme, 3), F32).at[dest].set(per_token)
    src = per_row[:, 2].astype(jnp.int32)

    yc = _moe(tile_table, src, h2c, per_row, norm2_g, w_e_gate[0], w_e_up[0], w_e_down[0], tme, n_tiles)
    out = _final(dest, h2c, yc, final_g.reshape(1, d), tmf=512)
    return out.reshape(batch, seq, d)
```
